```python
import math
import jax, jax.numpy as jnp
from jax import lax
import numpy as np

D_MODEL = 1024
BATCH = 16
SEQ = 4096
DEPTH = 4

ATTN_HEADS = 8
ATTN_HEAD_DIM = 64
ATTN_WIDTH = ATTN_HEADS * ATTN_HEAD_DIM
Q_BLOCK = 128
SSM_GROUPS = 32
SSM_GROUP_CH = 16
SSM_WIDTH = SSM_GROUPS * SSM_GROUP_CH
SSM_STATE = 64
D_FF = 4 * D_MODEL
N_IN = 3 * ATTN_WIDTH + ATTN_HEADS + SSM_WIDTH + 2 * D_MODEL
RMS_EPS = 1e-6
DT_MIN = 1e-3
DT_MAX = 1e-1

kernel_name = 'fox_s5_gated_hybrid_trunk'


def rmsnorm(x, g):
    xf = x.astype(jnp.float32)
    xf = xf * lax.rsqrt(jnp.mean(xf * xf, axis=-1, keepdims=True) + RMS_EPS)
    return (xf * g.astype(jnp.float32)).astype(x.dtype)


def forgetting_attention(q, k, v, log_f):
    seq = q.shape[2]
    scale = ATTN_HEAD_DIM ** -0.5
    cum = jnp.cumsum(log_f, axis=-1)
    outs = []
    for i in range(seq // Q_BLOCK):
        lo, hi = i * Q_BLOCK, (i + 1) * Q_BLOCK
        s = jnp.einsum('bhqd,bhkd->bhqk', q[:, :, lo:hi], k[:, :, :hi]).astype(jnp.float32) * scale
        s = s + cum[:, :, lo:hi, None] - cum[:, :, None, :hi]
        causal = (lo + jnp.arange(Q_BLOCK))[:, None] >= jnp.arange(hi)[None, :]
        p = jax.nn.softmax(jnp.where(causal, s, -jnp.inf), axis=-1)
        outs.append(jnp.einsum('bhqk,bhkd->bhqd', p.astype(v.dtype), v[:, :, :hi]))
    return jnp.concatenate(outs, axis=2)


def _linear_recurrence(e1, e2):
    a1, b1 = e1
    a2, b2 = e2
    return a1 * a2, a2 * b1 + b2


def s5_ssm(u, lam_re, lam_im, log_dt, b_re, b_im, c_re, c_im, d_skip):
    bsz, seq, _ = u.shape
    f32 = jnp.float32
    ug = u.astype(f32).reshape(bsz, seq, SSM_GROUPS, SSM_GROUP_CH)
    lam = lax.complex(lam_re.astype(f32), lam_im.astype(f32))
    dt = jnp.exp(log_dt.astype(f32))[:, None]
    lam_bar = jnp.exp(lam * dt)
    b_mat = lax.complex(b_re.astype(f32), b_im.astype(f32))
    b_bar = ((lam_bar - 1.0) / lam)[:, :, None] * b_mat
    bu = jnp.einsum('bsgc,gpc->bsgp', ug.astype(jnp.complex64), b_bar)
    a = jnp.broadcast_to(lam_bar[None, None], (1, seq, SSM_GROUPS, SSM_STATE))
    _, states = lax.associative_scan(_linear_recurrence, (a, bu), axis=1)
    c_mat = lax.complex(c_re.astype(f32), c_im.astype(f32))
    y = jnp.einsum('bsgp,gcp->bsgc', states, c_mat).real
    y = y + d_skip.astype(f32).reshape(SSM_GROUPS, SSM_GROUP_CH) * ug
    return y.reshape(bsz, seq, SSM_WIDTH).astype(u.dtype)


def hybrid_layer(x, norm_mix, w_in, b_forget, lam_re, lam_im, log_dt, b_re, b_im,
                 c_re, c_im, d_skip, w_glu, b_glu, w_branch_a, w_branch_b, w_out,
                 norm_mlp, w_mlp_up, w_mlp_down):
    bsz, seq, _ = x.shape
    h = rmsnorm(x, norm_mix)
    proj = h @ w_in
    o1 = ATTN_WIDTH
    o2 = o1 + ATTN_WIDTH
    o3 = o2 + ATTN_WIDTH
    o4 = o3 + ATTN_HEADS
    o5 = o4 + SSM_WIDTH
    o6 = o5 + D_MODEL
    q, k, v, f_logit, u, gate_a, gate_b = jnp.split(proj, [o1, o2, o3, o4, o5, o6], axis=-1)

    def heads(t):
        return t.reshape(bsz, seq, ATTN_HEADS, ATTN_HEAD_DIM).transpose(0, 2, 1, 3)
    log_f = jax.nn.log_sigmoid((f_logit + b_forget).astype(jnp.float32)).transpose(0, 2, 1)
    y_a = forgetting_attention(heads(q), heads(k), heads(v), log_f)
    y_a = y_a.transpose(0, 2, 1, 3).reshape(bsz, seq, ATTN_WIDTH)

    y_b = jax.nn.gelu(s5_ssm(u, lam_re, lam_im, log_dt, b_re, b_im, c_re, c_im, d_skip))
    y_b = y_b * jax.nn.sigmoid(y_b @ w_glu + b_glu)

    mixed = jax.nn.sigmoid(gate_a) * (y_a @ w_branch_a) + jax.nn.sigmoid(gate_b) * (y_b @ w_branch_b)
    x = x + mixed @ w_out

    h = rmsnorm(x, norm_mlp)
    x = x + jnp.square(jax.nn.relu(h @ w_mlp_up)) @ w_mlp_down
    return x


def _fwd_setup_inputs(seed: int = 0) -> dict:
    key = jax.random.key(seed)
    ks = jax.random.split(key, 24)
    f32 = jnp.float32
    L, G, P, C = DEPTH, SSM_GROUPS, SSM_STATE, SSM_GROUP_CH

    def nrm(k, shape, scale):
        return jax.random.normal(k, shape, f32) * scale

    n_idx = jnp.arange(P, dtype=f32)
    return {
        'x': nrm(ks[0], (BATCH, SEQ, D_MODEL), 1.0),
        'norm_mix': 1.0 + nrm(ks[1], (L, D_MODEL), 0.02),
        'w_in': nrm(ks[2], (L, D_MODEL, N_IN), D_MODEL ** -0.5),
        'b_forget': jax.random.uniform(ks[3], (L, ATTN_HEADS), f32, 1.0, 5.0),
        'ssm_lambda_re': -0.5 + nrm(ks[4], (L, G, P), 0.01),
        'ssm_lambda_im': jnp.pi * n_idx + nrm(ks[5], (L, G, P), 0.01),
        'ssm_log_dt': jax.random.uniform(ks[6], (L, G), f32, math.log(DT_MIN), math.log(DT_MAX)),
        'ssm_b_re': nrm(ks[7], (L, G, P, C), (2 * C) ** -0.5),
        'ssm_b_im': nrm(ks[8], (L, G, P, C), (2 * C) ** -0.5),
        'ssm_c_re': nrm(ks[9], (L, G, C, P), P ** -0.5),
        'ssm_c_im': nrm(ks[10], (L, G, C, P), P ** -0.5),
        'ssm_d': nrm(ks[11], (L, SSM_WIDTH), 1.0),
        'w_glu': nrm(ks[12], (L, SSM_WIDTH, SSM_WIDTH), SSM_WIDTH ** -0.5),
        'b_glu': nrm(ks[13], (L, SSM_WIDTH), 0.01),
        'w_branch_a': nrm(ks[14], (L, ATTN_WIDTH, D_MODEL), ATTN_WIDTH ** -0.5),
        'w_branch_b': nrm(ks[15], (L, SSM_WIDTH, D_MODEL), SSM_WIDTH ** -0.5),
        'w_out': nrm(ks[16], (L, D_MODEL, D_MODEL), D_MODEL ** -0.5),
        'norm_mlp': 1.0 + nrm(ks[17], (L, D_MODEL), 0.02),
        'w_mlp_up': nrm(ks[18], (L, D_MODEL, D_FF), D_MODEL ** -0.5),
        'w_mlp_down': nrm(ks[19], (L, D_FF, D_MODEL), D_FF ** -0.5),
        'norm_final': 1.0 + nrm(ks[20], (D_MODEL,), 0.02),
    }


def _fwd_reference(x, norm_mix, w_in, b_forget, ssm_lambda_re, ssm_lambda_im, ssm_log_dt,
              ssm_b_re, ssm_b_im, ssm_c_re, ssm_c_im, ssm_d, w_glu, b_glu,
              w_branch_a, w_branch_b, w_out, norm_mlp, w_mlp_up, w_mlp_down, norm_final):
    for l in range(DEPTH):
        x = hybrid_layer(x, norm_mix[l], w_in[l], b_forget[l], ssm_lambda_re[l], ssm_lambda_im[l],
                         ssm_log_dt[l], ssm_b_re[l], ssm_b_im[l], ssm_c_re[l], ssm_c_im[l],
                         ssm_d[l], w_glu[l], b_glu[l], w_branch_a[l], w_branch_b[l], w_out[l],
                         norm_mlp[l], w_mlp_up[l], w_mlp_down[l])
    return rmsnorm(x, norm_final)


import jax as _jax
import jax.numpy as _jnp

TWIN_FORMAT = 'train_step'
FWD_PARAMS = ['x', 'norm_mix', 'w_in', 'b_forget', 'ssm_lambda_re', 'ssm_lambda_im', 'ssm_log_dt', 'ssm_b_re', 'ssm_b_im', 'ssm_c_re', 'ssm_c_im', 'ssm_d', 'w_glu', 'b_glu', 'w_branch_a', 'w_branch_b', 'w_out', 'norm_mlp', 'w_mlp_up', 'w_mlp_down', 'norm_final']
TWIN_WEIGHTS = ['norm_mix', 'w_in', 'b_forget', 'ssm_lambda_re', 'ssm_lambda_im', 'ssm_log_dt', 'ssm_b_re', 'ssm_b_im', 'ssm_c_re', 'ssm_c_im', 'ssm_d', 'w_glu', 'b_glu', 'w_branch_a', 'w_branch_b', 'w_out', 'norm_mlp', 'w_mlp_up', 'w_mlp_down', 'norm_final']
TWIN_DIFF_INPUT = 'x'
TWIN_INPUTS = ['x', 'norm_mix', 'w_in', 'b_forget', 'ssm_lambda_re', 'ssm_lambda_im', 'ssm_log_dt', 'ssm_b_re', 'ssm_b_im', 'ssm_c_re', 'ssm_c_im', 'ssm_d', 'w_glu', 'b_glu', 'w_branch_a', 'w_branch_b', 'w_out', 'norm_mlp', 'w_mlp_up', 'w_mlp_down', 'norm_final', 'loss_target', 'm_norm_mix', 'm_w_in', 'm_b_forget', 'm_ssm_lambda_re', 'm_ssm_lambda_im', 'm_ssm_log_dt', 'm_ssm_b_re', 'm_ssm_b_im', 'm_ssm_c_re', 'm_ssm_c_im', 'm_ssm_d', 'm_w_glu', 'm_b_glu', 'm_w_branch_a', 'm_w_branch_b', 'm_w_out', 'm_norm_mlp', 'm_w_mlp_up', 'm_w_mlp_down', 'm_norm_final', 'v_norm_mix', 'v_w_in', 'v_b_forget', 'v_ssm_lambda_re', 'v_ssm_lambda_im', 'v_ssm_log_dt', 'v_ssm_b_re', 'v_ssm_b_im', 'v_ssm_c_re', 'v_ssm_c_im', 'v_ssm_d', 'v_w_glu', 'v_b_glu', 'v_w_branch_a', 'v_w_branch_b', 'v_w_out', 'v_norm_mlp', 'v_w_mlp_up', 'v_w_mlp_down', 'v_norm_final']
TWIN_OUTPUTS = ['loss', 'grad_x', 'grad_norm_mix', 'grad_w_in', 'grad_b_forget', 'grad_ssm_lambda_re', 'grad_ssm_lambda_im', 'grad_ssm_log_dt', 'grad_ssm_b_re', 'grad_ssm_b_im', 'grad_ssm_c_re', 'grad_ssm_c_im', 'grad_ssm_d', 'grad_w_glu', 'grad_b_glu', 'grad_w_branch_a', 'grad_w_branch_b', 'grad_w_out', 'grad_norm_mlp', 'grad_w_mlp_up', 'grad_w_mlp_down', 'grad_norm_final', 'delta_norm_mix', 'delta_w_in', 'delta_b_forget', 'delta_ssm_lambda_re', 'delta_ssm_lambda_im', 'delta_ssm_log_dt', 'delta_ssm_b_re', 'delta_ssm_b_im', 'delta_ssm_c_re', 'delta_ssm_c_im', 'delta_ssm_d', 'delta_w_glu', 'delta_b_glu', 'delta_w_branch_a', 'delta_w_branch_b', 'delta_w_out', 'delta_norm_mlp', 'delta_w_mlp_up', 'delta_w_mlp_down', 'delta_norm_final', 'new_m_norm_mix', 'new_m_w_in', 'new_m_b_forget', 'new_m_ssm_lambda_re', 'new_m_ssm_lambda_im', 'new_m_ssm_log_dt', 'new_m_ssm_b_re', 'new_m_ssm_b_im', 'new_m_ssm_c_re', 'new_m_ssm_c_im', 'new_m_ssm_d', 'new_m_w_glu', 'new_m_b_glu', 'new_m_w_branch_a', 'new_m_w_branch_b', 'new_m_w_out', 'new_m_norm_mlp', 'new_m_w_mlp_up', 'new_m_w_mlp_down', 'new_m_norm_final', 'new_v_norm_mix', 'new_v_w_in', 'new_v_b_forget', 'new_v_ssm_lambda_re', 'new_v_ssm_lambda_im', 'new_v_ssm_log_dt', 'new_v_ssm_b_re', 'new_v_ssm_b_im', 'new_v_ssm_c_re', 'new_v_ssm_c_im', 'new_v_ssm_d', 'new_v_w_glu', 'new_v_b_glu', 'new_v_w_branch_a', 'new_v_w_branch_b', 'new_v_w_out', 'new_v_norm_mlp', 'new_v_w_mlp_up', 'new_v_w_mlp_down', 'new_v_norm_final']
TWIN_LEAF_KINDS = {'loss': 'loss', 'grad_x': 'grad_x', 'grad_norm_mix': 'grad_w', 'grad_w_in': 'grad_w', 'grad_b_forget': 'grad_w', 'grad_ssm_lambda_re': 'grad_w', 'grad_ssm_lambda_im': 'grad_w', 'grad_ssm_log_dt': 'grad_w', 'grad_ssm_b_re': 'grad_w', 'grad_ssm_b_im': 'grad_w', 'grad_ssm_c_re': 'grad_w', 'grad_ssm_c_im': 'grad_w', 'grad_ssm_d': 'grad_w', 'grad_w_glu': 'grad_w', 'grad_b_glu': 'grad_w', 'grad_w_branch_a': 'grad_w', 'grad_w_branch_b': 'grad_w', 'grad_w_out': 'grad_w', 'grad_norm_mlp': 'grad_w', 'grad_w_mlp_up': 'grad_w', 'grad_w_mlp_down': 'grad_w', 'grad_norm_final': 'grad_w', 'delta_norm_mix': 'delta_w', 'delta_w_in': 'delta_w', 'delta_b_forget': 'delta_w', 'delta_ssm_lambda_re': 'delta_w', 'delta_ssm_lambda_im': 'delta_w', 'delta_ssm_log_dt': 'delta_w', 'delta_ssm_b_re': 'delta_w', 'delta_ssm_b_im': 'delta_w', 'delta_ssm_c_re': 'delta_w', 'delta_ssm_c_im': 'delta_w', 'delta_ssm_d': 'delta_w', 'delta_w_glu': 'delta_w', 'delta_b_glu': 'delta_w', 'delta_w_branch_a': 'delta_w', 'delta_w_branch_b': 'delta_w', 'delta_w_out': 'delta_w', 'delta_norm_mlp': 'delta_w', 'delta_w_mlp_up': 'delta_w', 'delta_w_mlp_down': 'delta_w', 'delta_norm_final': 'delta_w', 'new_m_norm_mix': 'new_m', 'new_m_w_in': 'new_m', 'new_m_b_forget': 'new_m', 'new_m_ssm_lambda_re': 'new_m', 'new_m_ssm_lambda_im': 'new_m', 'new_m_ssm_log_dt': 'new_m', 'new_m_ssm_b_re': 'new_m', 'new_m_ssm_b_im': 'new_m', 'new_m_ssm_c_re': 'new_m', 'new_m_ssm_c_im': 'new_m', 'new_m_ssm_d': 'new_m', 'new_m_w_glu': 'new_m', 'new_m_b_glu': 'new_m', 'new_m_w_branch_a': 'new_m', 'new_m_w_branch_b': 'new_m', 'new_m_w_out': 'new_m', 'new_m_norm_mlp': 'new_m', 'new_m_w_mlp_up': 'new_m', 'new_m_w_mlp_down': 'new_m', 'new_m_norm_final': 'new_m', 'new_v_norm_mix': 'new_v', 'new_v_w_in': 'new_v', 'new_v_b_forget': 'new_v', 'new_v_ssm_lambda_re': 'new_v', 'new_v_ssm_lambda_im': 'new_v', 'new_v_ssm_log_dt': 'new_v', 'new_v_ssm_b_re': 'new_v', 'new_v_ssm_b_im': 'new_v', 'new_v_ssm_c_re': 'new_v', 'new_v_ssm_c_im': 'new_v', 'new_v_ssm_d': 'new_v', 'new_v_w_glu': 'new_v', 'new_v_b_glu': 'new_v', 'new_v_w_branch_a': 'new_v', 'new_v_w_branch_b': 'new_v', 'new_v_w_out': 'new_v', 'new_v_norm_mlp': 'new_v', 'new_v_w_mlp_up': 'new_v', 'new_v_w_mlp_down': 'new_v', 'new_v_norm_final': 'new_v'}


def _forward(args):
    return _fwd_reference(*[args[k] for k in FWD_PARAMS])


def _output_shape():
    out = _jax.eval_shape(lambda: _forward(_fwd_setup_inputs(0)))
    return out.shape, out.dtype

N_MICROBATCH = 1
ADAM_LR = 0.001
ADAM_B1 = 0.9
ADAM_B2 = 0.999
ADAM_EPS = 1e-08
ADAM_WD = 0.01
ADAM_STEP = 10
PER_EXAMPLE_BATCH_AXIS = {'x': 0, 'loss_target': 0}
SHARED_INPUTS = []
_WEIGHT_DTYPES = {'norm_mix': _jnp.float32, 'w_in': _jnp.float32, 'b_forget': _jnp.float32, 'ssm_lambda_re': _jnp.float32, 'ssm_lambda_im': _jnp.float32, 'ssm_log_dt': _jnp.float32, 'ssm_b_re': _jnp.float32, 'ssm_b_im': _jnp.float32, 'ssm_c_re': _jnp.float32, 'ssm_c_im': _jnp.float32, 'ssm_d': _jnp.float32, 'w_glu': _jnp.float32, 'b_glu': _jnp.float32, 'w_branch_a': _jnp.float32, 'w_branch_b': _jnp.float32, 'w_out': _jnp.float32, 'norm_mlp': _jnp.float32, 'w_mlp_up': _jnp.float32, 'w_mlp_down': _jnp.float32, 'norm_final': _jnp.float32}
MOMENT_SCALE = {'norm_mix': 7.735627e-02, 'w_in': 3.894031e-02, 'b_forget': 2.958893e-01, 'ssm_lambda_re': 6.855472e-03, 'ssm_lambda_im': 7.084395e-03, 'ssm_log_dt': 7.850850e+00, 'ssm_b_re': 3.830880e-03, 'ssm_b_im': 3.933425e-03, 'ssm_c_re': 5.394807e-03, 'ssm_c_im': 5.289192e-03, 'ssm_d': 6.759339e-02, 'w_glu': 1.594532e-02, 'b_glu': 2.434175e-02, 'w_branch_a': 5.240914e-02, 'w_branch_b': 4.446023e-02, 'w_out': 6.853639e-02, 'norm_mlp': 2.024522e-01, 'w_mlp_up': 9.763257e-02, 'w_mlp_down': 2.043103e-01, 'norm_final': 6.648340e+01}


def _to_microbatches(a, axis):
    t = _jnp.moveaxis(a, axis, 0)
    t = t.reshape((N_MICROBATCH, t.shape[0] // N_MICROBATCH) + t.shape[1:])
    return _jnp.moveaxis(t, 1, axis + 1)


def setup_inputs(seed: int = 0) -> dict:
    inp = _fwd_setup_inputs(seed)
    key = _jax.random.fold_in(_jax.random.key(seed), 7919)
    shape, _ = _output_shape()
    out = dict(inp)
    out["loss_target"] = _jax.random.normal(_jax.random.fold_in(key, 0), shape, _jnp.float32)
    for i, name in enumerate(TWIN_WEIGHTS):
        w = inp[name].astype(_jnp.float32)
        if MOMENT_SCALE is None:
            s = _jnp.sqrt(_jnp.mean(_jnp.square(w)) + 1e-30)
        else:
            s = MOMENT_SCALE[name]
        km, kv = _jax.random.split(_jax.random.fold_in(key, i + 1))
        out[name] = w
        out["m_" + name] = s * _jax.random.normal(km, w.shape, _jnp.float32)
        out["v_" + name] = (s * s) * _jax.random.uniform(kv, w.shape, _jnp.float32, 0.5, 1.5)
    if N_MICROBATCH > 1:
        for name, axis in PER_EXAMPLE_BATCH_AXIS.items():
            out[name] = _to_microbatches(out[name], axis)
    return {'x': out['x'], 'norm_mix': out['norm_mix'], 'w_in': out['w_in'], 'b_forget': out['b_forget'], 'ssm_lambda_re': out['ssm_lambda_re'], 'ssm_lambda_im': out['ssm_lambda_im'], 'ssm_log_dt': out['ssm_log_dt'], 'ssm_b_re': out['ssm_b_re'], 'ssm_b_im': out['ssm_b_im'], 'ssm_c_re': out['ssm_c_re'], 'ssm_c_im': out['ssm_c_im'], 'ssm_d': out['ssm_d'], 'w_glu': out['w_glu'], 'b_glu': out['b_glu'], 'w_branch_a': out['w_branch_a'], 'w_branch_b': out['w_branch_b'], 'w_out': out['w_out'], 'norm_mlp': out['norm_mlp'], 'w_mlp_up': out['w_mlp_up'], 'w_mlp_down': out['w_mlp_down'], 'norm_final': out['norm_final'], 'loss_target': out['loss_target'], 'm_norm_mix': out['m_norm_mix'], 'm_w_in': out['m_w_in'], 'm_b_forget': out['m_b_forget'], 'm_ssm_lambda_re': out['m_ssm_lambda_re'], 'm_ssm_lambda_im': out['m_ssm_lambda_im'], 'm_ssm_log_dt': out['m_ssm_log_dt'], 'm_ssm_b_re': out['m_ssm_b_re'], 'm_ssm_b_im': out['m_ssm_b_im'], 'm_ssm_c_re': out['m_ssm_c_re'], 'm_ssm_c_im': out['m_ssm_c_im'], 'm_ssm_d': out['m_ssm_d'], 'm_w_glu': out['m_w_glu'], 'm_b_glu': out['m_b_glu'], 'm_w_branch_a': out['m_w_branch_a'], 'm_w_branch_b': out['m_w_branch_b'], 'm_w_out': out['m_w_out'], 'm_norm_mlp': out['m_norm_mlp'], 'm_w_mlp_up': out['m_w_mlp_up'], 'm_w_mlp_down': out['m_w_mlp_down'], 'm_norm_final': out['m_norm_final'], 'v_norm_mix': out['v_norm_mix'], 'v_w_in': out['v_w_in'], 'v_b_forget': out['v_b_forget'], 'v_ssm_lambda_re': out['v_ssm_lambda_re'], 'v_ssm_lambda_im': out['v_ssm_lambda_im'], 'v_ssm_log_dt': out['v_ssm_log_dt'], 'v_ssm_b_re': out['v_ssm_b_re'], 'v_ssm_b_im': out['v_ssm_b_im'], 'v_ssm_c_re': out['v_ssm_c_re'], 'v_ssm_c_im': out['v_ssm_c_im'], 'v_ssm_d': out['v_ssm_d'], 'v_w_glu': out['v_w_glu'], 'v_b_glu': out['v_b_glu'], 'v_w_branch_a': out['v_w_branch_a'], 'v_w_branch_b': out['v_w_branch_b'], 'v_w_out': out['v_w_out'], 'v_norm_mlp': out['v_norm_mlp'], 'v_w_mlp_up': out['v_w_mlp_up'], 'v_w_mlp_down': out['v_w_mlp_down'], 'v_norm_final': out['v_norm_final']}


def _loss(weights, diff, rest, loss_target):
    with _jax.named_scope("forward"):
        args = {**rest, TWIN_DIFF_INPUT: diff, **{k: w.astype(_WEIGHT_DTYPES[k]) for k, w in weights.items()}}
        y = _forward(args)
    with _jax.named_scope("loss_head"):
        err = _jnp.square(y.astype(_jnp.float32) - loss_target)
        return 0.5 * _jnp.sum(_jnp.mean(err, axis=-1)) if err.ndim else 0.5 * err


def _adamw(w, g, m, v):
    m = ADAM_B1 * m + (1.0 - ADAM_B1) * g
    v = ADAM_B2 * v + (1.0 - ADAM_B2) * _jnp.square(g)
    m_hat = m / (1.0 - ADAM_B1 ** ADAM_STEP)
    v_hat = v / (1.0 - ADAM_B2 ** ADAM_STEP)
    delta = -ADAM_LR * (m_hat / (_jnp.sqrt(v_hat) + ADAM_EPS) + ADAM_WD * w)
    return delta, m, v


def reference(x, norm_mix, w_in, b_forget, ssm_lambda_re, ssm_lambda_im, ssm_log_dt, ssm_b_re, ssm_b_im, ssm_c_re, ssm_c_im, ssm_d, w_glu, b_glu, w_branch_a, w_branch_b, w_out, norm_mlp, w_mlp_up, w_mlp_down, norm_final, loss_target, m_norm_mix, m_w_in, m_b_forget, m_ssm_lambda_re, m_ssm_lambda_im, m_ssm_log_dt, m_ssm_b_re, m_ssm_b_im, m_ssm_c_re, m_ssm_c_im, m_ssm_d, m_w_glu, m_b_glu, m_w_branch_a, m_w_branch_b, m_w_out, m_norm_mlp, m_w_mlp_up, m_w_mlp_down, m_norm_final, v_norm_mix, v_w_in, v_b_forget, v_ssm_lambda_re, v_ssm_lambda_im, v_ssm_log_dt, v_ssm_b_re, v_ssm_b_im, v_ssm_c_re, v_ssm_c_im, v_ssm_d, v_w_glu, v_b_glu, v_w_branch_a, v_w_branch_b, v_w_out, v_norm_mlp, v_w_mlp_up, v_w_mlp_down, v_norm_final):
    given = dict(x=x, norm_mix=norm_mix, w_in=w_in, b_forget=b_forget, ssm_lambda_re=ssm_lambda_re, ssm_lambda_im=ssm_lambda_im, ssm_log_dt=ssm_log_dt, ssm_b_re=ssm_b_re, ssm_b_im=ssm_b_im, ssm_c_re=ssm_c_re, ssm_c_im=ssm_c_im, ssm_d=ssm_d, w_glu=w_glu, b_glu=b_glu, w_branch_a=w_branch_a, w_branch_b=w_branch_b, w_out=w_out, norm_mlp=norm_mlp, w_mlp_up=w_mlp_up, w_mlp_down=w_mlp_down, norm_final=norm_final, loss_target=loss_target, m_norm_mix=m_norm_mix, m_w_in=m_w_in, m_b_forget=m_b_forget, m_ssm_lambda_re=m_ssm_lambda_re, m_ssm_lambda_im=m_ssm_lambda_im, m_ssm_log_dt=m_ssm_log_dt, m_ssm_b_re=m_ssm_b_re, m_ssm_b_im=m_ssm_b_im, m_ssm_c_re=m_ssm_c_re, m_ssm_c_im=m_ssm_c_im, m_ssm_d=m_ssm_d, m_w_glu=m_w_glu, m_b_glu=m_b_glu, m_w_branch_a=m_w_branch_a, m_w_branch_b=m_w_branch_b, m_w_out=m_w_out, m_norm_mlp=m_norm_mlp, m_w_mlp_up=m_w_mlp_up, m_w_mlp_down=m_w_mlp_down, m_norm_final=m_norm_final, v_norm_mix=v_norm_mix, v_w_in=v_w_in, v_b_forget=v_b_forget, v_ssm_lambda_re=v_ssm_lambda_re, v_ssm_lambda_im=v_ssm_lambda_im, v_ssm_log_dt=v_ssm_log_dt, v_ssm_b_re=v_ssm_b_re, v_ssm_b_im=v_ssm_b_im, v_ssm_c_re=v_ssm_c_re, v_ssm_c_im=v_ssm_c_im, v_ssm_d=v_ssm_d, v_w_glu=v_w_glu, v_b_glu=v_b_glu, v_w_branch_a=v_w_branch_a, v_w_branch_b=v_w_branch_b, v_w_out=v_w_out, v_norm_mlp=v_norm_mlp, v_w_mlp_up=v_w_mlp_up, v_w_mlp_down=v_w_mlp_down, v_norm_final=v_norm_final)
    weights = {n: given[n] for n in TWIN_WEIGHTS}
    shared = {n: given[n] for n in SHARED_INPUTS}
    per_example = {n: given[n] for n in ['x']}
    grad_fn = _jax.value_and_grad(_loss, argnums=(0, 1))

    def one_microbatch(ex, loss_target):
        ex = dict(ex)
        diff = ex.pop(TWIN_DIFF_INPUT)
        return grad_fn(weights, diff, {**shared, **ex}, loss_target)

    if N_MICROBATCH == 1:
        loss, (grad_w, grad_x) = one_microbatch(per_example, given["loss_target"])
    else:
        def body(carry, xs):
            loss_sum, grad_sum = carry
            l_k, (gw_k, gx_k) = one_microbatch(xs[0], xs[1])
            with _jax.named_scope("update"):
                return (loss_sum + l_k, _jax.tree.map(_jnp.add, grad_sum, gw_k)), gx_k

        init = (_jnp.zeros((), _jnp.float32), _jax.tree.map(_jnp.zeros_like, weights))
        (loss, grad_w), grad_x = _jax.lax.scan(body, init, (per_example, given["loss_target"]))
    with _jax.named_scope("update"):
        delta_w, new_m, new_v = {}, {}, {}
        for n in TWIN_WEIGHTS:
            delta_w[n], new_m[n], new_v[n] = _adamw(weights[n], grad_w[n], given["m_" + n], given["v_" + n])
    return (loss, grad_x, *[grad_w[n] for n in TWIN_WEIGHTS], *[delta_w[n] for n in TWIN_WEIGHTS],
            *[new_m[n] for n in TWIN_WEIGHTS], *[new_v[n] for n in TWIN_WEIGHTS])
```

```python
import functools
import math

import jax
import jax.numpy as jnp
from jax import lax
from jax.experimental import pallas as pl
from jax.experimental.pallas import tpu as pltpu

F32 = jnp.float32
BF16 = jnp.bfloat16
MESH = pl.DeviceIdType.MESH
ANY = pl.BlockSpec(memory_space=pl.ANY)

ATTN_HEADS = 8
HEAD_DIM = 64
ATTN_WIDTH = ATTN_HEADS * HEAD_DIM
HEAD_PAIRS = ATTN_HEADS // 2
SSM_GROUPS = 32
SSM_GROUP_CH = 16
SSM_STATE = 64
SSM_WIDTH = SSM_GROUPS * SSM_GROUP_CH
LANES = 128
SUBLANES = 8
SSM_CHUNKS = SSM_WIDTH // LANES
CHUNK_GROUPS = SSM_GROUPS // SSM_CHUNKS
CHUNK_STATES = CHUNK_GROUPS * SSM_STATE
CHUNK_LANES = 2 * CHUNK_STATES
STATE_LANES = SSM_CHUNKS * CHUNK_LANES
F_PAD = LANES
RMS_EPS = 1e-6
ADAM_LR = 0.001
ADAM_B1 = 0.9
ADAM_B2 = 0.999
ADAM_EPS = 1e-08
ADAM_WD = 0.01
ADAM_STEP = 10
PACK_COLS = 1024
PACK_ROW_TILE = 256
VMEM_LIMIT = 52 * 1024 * 1024
NEG_BIG = -1e30
GELU_C = math.sqrt(2.0 / math.pi)
GELU_A = 0.044715

NN = (((1,), (0,)), ((), ()))
NT = (((1,), (1,)), ((), ()))
TN = (((0,), (0,)), ((), ()))


def _pick(n, pref):
    if n <= pref:
        return n
    best = LANES
    for t in range(LANES, pref + 1, LANES):
        if n % t == 0:
            best = t
    assert n % best == 0, (n, pref)
    return best


def _rows(n, pref):
    t = min(n, pref)
    while n % t:
        t //= 2
    assert t % 16 == 0 or t == n, (n, pref)
    return t


def _params(sem):
    return pltpu.CompilerParams(dimension_semantics=sem, vmem_limit_bytes=VMEM_LIMIT)


def _fold8(v):
    r, c = v.shape
    return jnp.sum(v.reshape(r // SUBLANES, SUBLANES, c), axis=0)


def _dot(a, b, dims=None):
    if dims is None:
        return jnp.dot(a, b, preferred_element_type=F32)
    return lax.dot_general(a, b, dims, preferred_element_type=F32)


def _dot_exact(a, b, dims):
    return lax.dot_general(a, b, dims, preferred_element_type=F32, precision=lax.Precision.HIGHEST)


def _sigmoid(v):
    return 1.0 / (1.0 + jnp.exp(-v))


def _rms_scale(x):
    return lax.rsqrt(jnp.mean(x * x, axis=-1, keepdims=True) + RMS_EPS)


def _rms_bwd(x, g, dh):
    r = _rms_scale(x)
    xn = x * r
    dxn = dh * g
    dx = r * (dxn - xn * jnp.mean(dxn * xn, axis=-1, keepdims=True))
    return dx, dh * xn


def norm_matmul(x, g, w, out_dtype, name):
    t, d = x.shape
    m = w.shape[1]
    tm, tn = _rows(t, 512), _pick(m, 768)

    def body(x_ref, g_ref, w_ref, o_ref):
        xv = x_ref[...]
        h = (xv * _rms_scale(xv) * g_ref[...]).astype(BF16)
        o_ref[...] = _dot(h, w_ref[...]).astype(o_ref.dtype)

    return pl.pallas_call(
        body, name=name, grid=(t // tm, m // tn),
        in_specs=[pl.BlockSpec((tm, d), lambda i, j: (i, 0)),
                  pl.BlockSpec((1, d), lambda i, j: (0, 0)),
                  pl.BlockSpec((d, tn), lambda i, j: (0, j))],
        out_specs=pl.BlockSpec((tm, tn), lambda i, j: (i, j)),
        out_shape=jax.ShapeDtypeStruct((t, m), out_dtype),
        compiler_params=_params(("parallel", "arbitrary")),
    )(x, g, w)


def matmul_tn(a, b, name, a_kind="plain", gain=None, tm_pref=1024, tn_pref=1536, tk_pref=512):
    t, ma = a.shape
    nb = b.shape[1]
    tm = ma if a_kind == "norm" else _pick(ma, tm_pref)
    tn = _pick(nb, tn_pref)
    tk = _rows(t, tk_pref)
    nk = t // tk

    def body(*refs):
        if a_kind == "norm":
            a_ref, g_ref, b_ref, o_ref, acc = refs
        else:
            a_ref, b_ref, o_ref, acc = refs
        k = pl.program_id(2)

        @pl.when(k == 0)
        def _():
            acc[...] = jnp.zeros_like(acc)

        av = a_ref[...]
        if a_kind == "norm":
            av = av * _rms_scale(av) * g_ref[...]
        elif a_kind == "relu2":
            av = jnp.square(jnp.maximum(av.astype(F32), 0.0))
        acc[...] += _dot(av.astype(BF16), b_ref[...].astype(BF16), TN)

        @pl.when(k == nk - 1)
        def _():
            o_ref[...] = acc[...]

    in_specs = [pl.BlockSpec((tk, tm), lambda i, j, k: (k, i))]
    args = [a]
    if a_kind == "norm":
        in_specs.append(pl.BlockSpec((1, ma), lambda i, j, k: (0, 0)))
        args.append(gain)
    in_specs.append(pl.BlockSpec((tk, tn), lambda i, j, k: (k, j)))
    args.append(b)
    return pl.pallas_call(
        body, name=name, grid=(ma // tm, nb // tn, nk),
        in_specs=in_specs,
        out_specs=pl.BlockSpec((tm, tn), lambda i, j, k: (i, j)),
        out_shape=jax.ShapeDtypeStruct((ma, nb), F32),
        scratch_shapes=[pltpu.VMEM((tm, tn), F32)],
        compiler_params=_params(("parallel", "parallel", "arbitrary")),
    )(*args)


def _tri(n, upper):
    r = lax.broadcasted_iota(jnp.int32, (n, n), 0)
    c = lax.broadcasted_iota(jnp.int32, (n, n), 1)
    return jnp.where((c >= r) if upper else (c <= r), 1.0, 0.0).astype(F32)


def _head_rows():
    r = lax.broadcasted_iota(jnp.int32, (SUBLANES, LANES), 0)
    c = lax.broadcasted_iota(jnp.int32, (SUBLANES, LANES), 1)
    return jnp.where(r == c, 1.0, 0.0).astype(F32)


def forget_cumsum(rest, bf, bsz, seq, f_blk):
    tc = _rows(seq, 512)
    nc = seq // tc

    def body(f_ref, b_ref, col_ref, row_ref, carry):
        c = pl.program_id(1)

        @pl.when(c == 0)
        def _():
            carry[...] = jnp.zeros_like(carry)

        z = f_ref[...] + b_ref[...]
        logf = jnp.minimum(z, 0.0) - jnp.log(1.0 + jnp.exp(-jnp.abs(z)))
        cum = _dot_exact(_tri(tc, False), logf, NN) + carry[0:1, :]
        col_ref[0] = cum
        row_ref[0] = _dot_exact(_head_rows(), cum, NT)
        carry[...] = jnp.broadcast_to(cum[tc - 1:tc, :], carry.shape)

    return pl.pallas_call(
        body, name="forget_cumsum", grid=(bsz, nc),
        in_specs=[pl.BlockSpec((tc, F_PAD), lambda b, c: (b * nc + c, f_blk)),
                  pl.BlockSpec((1, F_PAD), lambda b, c: (0, 0))],
        out_specs=[pl.BlockSpec((1, tc, LANES), lambda b, c: (b, c, 0)),
                   pl.BlockSpec((1, SUBLANES, tc), lambda b, c: (b, 0, c))],
        out_shape=[jax.ShapeDtypeStruct((bsz, seq, LANES), F32),
                   jax.ShapeDtypeStruct((bsz, SUBLANES, seq), F32)],
        scratch_shapes=[pltpu.VMEM((SUBLANES, LANES), F32)],
        compiler_params=_params(("parallel", "arbitrary")),
    )(rest, bf)


def forget_bwd(dcq, dcp, rest, bf, bsz, seq, f_blk):
    tc = _rows(seq, 512)
    nc = seq // tc

    def body(dq_ref, dc_ref, f_ref, b_ref, df_ref, db_ref, carry):
        b = pl.program_id(0)
        c = pl.program_id(1)

        @pl.when(c == 0)
        def _():
            carry[...] = jnp.zeros_like(carry)

        @pl.when((b == 0) & (c == 0))
        def _():
            db_ref[...] = jnp.zeros_like(db_ref)

        dc = dc_ref[0, 0] + dq_ref[0, 0]
        for p in range(1, HEAD_PAIRS):
            dc = dc + (dc_ref[0, p] + dq_ref[0, p])
        dlogf = _dot_exact(_tri(tc, True), dc, NN) + carry[0:1, :]
        carry[...] = jnp.broadcast_to(dlogf[0:1, :], carry.shape)
        z = f_ref[...] + b_ref[...]
        lane = lax.broadcasted_iota(jnp.int32, z.shape, 1)
        df = jnp.where(lane < ATTN_HEADS, dlogf * _sigmoid(-z), 0.0)
        df_ref[...] = df.astype(df_ref.dtype)
        db_ref[...] += _fold8(df)

    return pl.pallas_call(
        body, name="forget_bwd", grid=(bsz, nc),
        in_specs=[pl.BlockSpec((1, HEAD_PAIRS, tc, LANES), lambda b, c: (b, 0, nc - 1 - c, 0)),
                  pl.BlockSpec((1, HEAD_PAIRS, tc, LANES), lambda b, c: (b, 0, nc - 1 - c, 0)),
                  pl.BlockSpec((tc, F_PAD), lambda b, c: (b * nc + nc - 1 - c, f_blk)),
                  pl.BlockSpec((1, F_PAD), lambda b, c: (0, 0))],
        out_specs=[pl.BlockSpec((tc, F_PAD), lambda b, c: (b * nc + nc - 1 - c, 0)),
                   pl.BlockSpec((SUBLANES, F_PAD), lambda b, c: (0, 0))],
        out_shape=[jax.ShapeDtypeStruct((bsz * seq, F_PAD), BF16),
                   jax.ShapeDtypeStruct((SUBLANES, F_PAD), F32)],
        scratch_shapes=[pltpu.VMEM((SUBLANES, LANES), F32)],
        compiler_params=_params(("arbitrary", "arbitrary")),
    )(dcq, dcp, rest, bf)


def _attn_tile(seq):
    return 512 if seq >= 2048 else 128


def _lane_head(shape, par):
    lane = lax.broadcasted_iota(jnp.int32, shape, len(shape) - 1)
    return (lane >= HEAD_DIM) if par else (lane < HEAD_DIM)


def _pick_lane(block, idx):
    lane = lax.broadcasted_iota(jnp.int32, block.shape, 1)
    return jnp.sum(jnp.where(lane == idx, block, 0.0), axis=1, keepdims=True)


def _pair_rows(lo_lane, hi_lane):
    r = lax.broadcasted_iota(jnp.int32, (SUBLANES, LANES), 0)
    c = lax.broadcasted_iota(jnp.int32, (SUBLANES, LANES), 1)
    if lo_lane is None:
        sel = ((r == 0) & (c < HEAD_DIM)) | ((r == 1) & (c >= HEAD_DIM))
    else:
        sel = ((r == 0) & (c == lo_lane)) | ((r == 1) & (c == hi_lane))
    return jnp.where(sel, 1.0, 0.0).astype(F32)


def fox_fwd(qkv, cumcol, cumrow, bsz, seq):
    tq = _attn_tile(seq)
    nq = seq // tq
    scale = HEAD_DIM ** -0.5
    kb, vb = ATTN_WIDTH // LANES, 2 * ATTN_WIDTH // LANES

    def body(q_ref, k_ref, v_ref, cc_ref, cr_ref, o_ref, lc_ref, lr_ref, m_s, l_s, acc_s):
        hp = pl.program_id(1)
        qi = pl.program_id(2)
        kj = pl.program_id(3)

        @pl.when(kj == 0)
        def _():
            m_s[...] = jnp.full_like(m_s, NEG_BIG)
            l_s[...] = jnp.zeros_like(l_s)
            acc_s[...] = jnp.zeros_like(acc_s)

        def step(diag):
            q = q_ref[...]
            k = k_ref[...]
            v = v_ref[...]
            for par in range(2):
                head = 2 * hp + par
                qh = jnp.where(_lane_head(q.shape, par), q, 0.0) * scale
                s = _dot(qh.astype(BF16), k, NT)
                ci = _pick_lane(cc_ref[0], head)
                cj = cr_ref[0, pl.ds(head, 1), :]
                s = s + (ci - cj)
                if diag:
                    r = lax.broadcasted_iota(jnp.int32, s.shape, 0)
                    c = lax.broadcasted_iota(jnp.int32, s.shape, 1)
                    s = jnp.where(r >= c, s, NEG_BIG)
                m_prev = m_s[par]
                m_new = jnp.maximum(m_prev, jnp.max(s, axis=1, keepdims=True))
                alpha = jnp.exp(m_prev - m_new)
                p = jnp.exp(s - m_new)
                l_s[par] = alpha * l_s[par] + jnp.sum(p, axis=1, keepdims=True)
                acc_s[par] = alpha * acc_s[par] + _dot(p.astype(BF16), v)
                m_s[par] = m_new

        @pl.when(kj < qi)
        def _():
            step(False)

        @pl.when(kj == qi)
        def _():
            step(True)
            lo = _lane_head((tq, LANES), 0)
            out = jnp.where(lo, acc_s[0] / l_s[0], acc_s[1] / l_s[1])
            o_ref[...] = out.astype(o_ref.dtype)
            lse = jnp.where(lo, m_s[0] + jnp.log(l_s[0]), m_s[1] + jnp.log(l_s[1]))
            lc_ref[0, 0] = lse
            lr_ref[0, 0] = _dot_exact(_pair_rows(0, HEAD_DIM), lse, NT)

    return pl.pallas_call(
        body, name="fox_fwd", grid=(bsz, HEAD_PAIRS, nq, nq),
        in_specs=[pl.BlockSpec((tq, LANES), lambda b, h, i, j: (b * nq + i, h)),
                  pl.BlockSpec((tq, LANES), lambda b, h, i, j: (b * nq + jnp.minimum(i, j), kb + h)),
                  pl.BlockSpec((tq, LANES), lambda b, h, i, j: (b * nq + jnp.minimum(i, j), vb + h)),
                  pl.BlockSpec((1, tq, LANES), lambda b, h, i, j: (b, i, 0)),
                  pl.BlockSpec((1, SUBLANES, tq), lambda b, h, i, j: (b, 0, jnp.minimum(i, j)))],
        out_specs=[pl.BlockSpec((tq, LANES), lambda b, h, i, j: (b * nq + i, h)),
                   pl.BlockSpec((1, 1, tq, LANES), lambda b, h, i, j: (b, h, i, 0)),
                   pl.BlockSpec((1, 1, SUBLANES, tq), lambda b, h, i, j: (b, h, 0, i))],
        out_shape=[jax.ShapeDtypeStruct((bsz * seq, ATTN_WIDTH), BF16),
                   jax.ShapeDtypeStruct((bsz, HEAD_PAIRS, seq, LANES), F32),
                   jax.ShapeDtypeStruct((bsz, HEAD_PAIRS, SUBLANES, seq), F32)],
        scratch_shapes=[pltpu.VMEM((2, tq, 1), F32), pltpu.VMEM((2, tq, 1), F32),
                        pltpu.VMEM((2, tq, LANES), F32)],
        compiler_params=_params(("parallel", "parallel", "parallel", "arbitrary")),
    )(qkv, qkv, qkv, cumcol, cumrow)


def fox_bwd_q(qkv, do, o, lsec, cumcol, cumrow, bsz, seq):
    tq = _attn_tile(seq)
    nq = seq // tq
    scale = HEAD_DIM ** -0.5
    kb, vb = ATTN_WIDTH // LANES, 2 * ATTN_WIDTH // LANES

    def body(q_ref, k_ref, v_ref, do_ref, o_ref, lc_ref, cc_ref, cr_ref, dq_ref, dc_ref, d_s, acc_s, dc_s):
        hp = pl.program_id(1)
        qi = pl.program_id(2)
        kj = pl.program_id(3)

        @pl.when(kj == 0)
        def _():
            prod = do_ref[...].astype(F32) * o_ref[...].astype(F32)
            for par in range(2):
                d_s[par] = jnp.sum(jnp.where(_lane_head(prod.shape, par), prod, 0.0), axis=1, keepdims=True)
            acc_s[...] = jnp.zeros_like(acc_s)
            dc_s[...] = jnp.zeros_like(dc_s)

        def step(diag):
            q = q_ref[...]
            k = k_ref[...]
            v = v_ref[...]
            dov = do_ref[...]
            lse = lc_ref[0, 0]
            for par in range(2):
                head = 2 * hp + par
                sel = _lane_head(q.shape, par)
                qh = jnp.where(sel, q, 0.0) * scale
                s = _dot(qh.astype(BF16), k, NT)
                ci = _pick_lane(cc_ref[0], head)
                cj = cr_ref[0, pl.ds(head, 1), :]
                s = s + (ci - cj)
                if diag:
                    r = lax.broadcasted_iota(jnp.int32, s.shape, 0)
                    c = lax.broadcasted_iota(jnp.int32, s.shape, 1)
                    s = jnp.where(r >= c, s, NEG_BIG)
                p = jnp.exp(s - lse[:, par * HEAD_DIM:par * HEAD_DIM + 1])
                doh = jnp.where(sel, dov, 0.0)
                dp = _dot(doh.astype(BF16), v, NT)
                ds = p * (dp - d_s[par])
                acc_s[...] += jnp.where(sel, _dot(ds.astype(BF16), k), 0.0)
                lane = lax.broadcasted_iota(jnp.int32, (tq, LANES), 1)
                dc_s[...] += jnp.where(lane == head, jnp.sum(ds, axis=1, keepdims=True), 0.0)

        @pl.when(kj < qi)
        def _():
            step(False)

        @pl.when(kj == qi)
        def _():
            step(True)
            dq_ref[...] = (acc_s[...] * scale).astype(dq_ref.dtype)
            dc_ref[0, 0] = dc_s[...]

    return pl.pallas_call(
        body, name="fox_bwd_q", grid=(bsz, HEAD_PAIRS, nq, nq),
        in_specs=[pl.BlockSpec((tq, LANES), lambda b, h, i, j: (b * nq + i, h)),
                  pl.BlockSpec((tq, LANES), lambda b, h, i, j: (b * nq + jnp.minimum(i, j), kb + h)),
                  pl.BlockSpec((tq, LANES), lambda b, h, i, j: (b * nq + jnp.minimum(i, j), vb + h)),
                  pl.BlockSpec((tq, LANES), lambda b, h, i, j: (b * nq + i, h)),
                  pl.BlockSpec((tq, LANES), lambda b, h, i, j: (b * nq + i, h)),
                  pl.BlockSpec((1, 1, tq, LANES), lambda b, h, i, j: (b, h, i, 0)),
                  pl.BlockSpec((1, tq, LANES), lambda b, h, i, j: (b, i, 0)),
                  pl.BlockSpec((1, SUBLANES, tq), lambda b, h, i, j: (b, 0, jnp.minimum(i, j)))],
        out_specs=[pl.BlockSpec((tq, LANES), lambda b, h, i, j: (b * nq + i, h)),
                   pl.BlockSpec((1, 1, tq, LANES), lambda b, h, i, j: (b, h, i, 0))],
        out_shape=[jax.ShapeDtypeStruct((bsz * seq, ATTN_WIDTH), BF16),
                   jax.ShapeDtypeStruct((bsz, HEAD_PAIRS, seq, LANES), F32)],
        scratch_shapes=[pltpu.VMEM((2, tq, 1), F32), pltpu.VMEM((tq, LANES), F32), pltpu.VMEM((tq, LANES), F32)],
        compiler_params=_params(("parallel", "parallel", "parallel", "arbitrary")),
    )(qkv, qkv, qkv, do, o, lsec, cumcol, cumrow)


def fox_bwd_kv(qkv, do, o, lser, cumcol, cumrow, bsz, seq):
    tk = _attn_tile(seq)
    nk = seq // tk
    scale = HEAD_DIM ** -0.5
    kb, vb = ATTN_WIDTH // LANES, 2 * ATTN_WIDTH // LANES

    def body(q_ref, k_ref, v_ref, do_ref, o_ref, lr_ref, cc_ref, cr_ref,
             dk_ref, dv_ref, dc_ref, dk_s, dv_s, dc_s):
        hp = pl.program_id(1)
        kj = pl.program_id(2)
        qi = pl.program_id(3)

        @pl.when(qi == 0)
        def _():
            dk_s[...] = jnp.zeros_like(dk_s)
            dv_s[...] = jnp.zeros_like(dv_s)
            dc_s[...] = jnp.zeros_like(dc_s)

        def step(diag):
            q = q_ref[...]
            k = k_ref[...]
            v = v_ref[...]
            dov = do_ref[...]
            prod = dov.astype(F32) * o_ref[...].astype(F32)
            drow = _dot_exact(_pair_rows(None, None), prod, NT)
            lrow = lr_ref[0, 0]
            lane = lax.broadcasted_iota(jnp.int32, (tk, LANES), 1)
            for par in range(2):
                head = 2 * hp + par
                sel = _lane_head(k.shape, par)
                kh = jnp.where(sel, k, 0.0) * scale
                st = _dot(kh.astype(BF16), q, NT)
                ci = cr_ref[0, pl.ds(head, 1), :]
                cj = _pick_lane(cc_ref[0], head)
                st = st + (ci - cj)
                if diag:
                    r = lax.broadcasted_iota(jnp.int32, st.shape, 0)
                    c = lax.broadcasted_iota(jnp.int32, st.shape, 1)
                    st = jnp.where(c >= r, st, NEG_BIG)
                pt = jnp.exp(st - lrow[par:par + 1, :])
                vh = jnp.where(sel, v, 0.0)
                dpt = _dot(vh.astype(BF16), dov, NT)
                dst = pt * (dpt - drow[par:par + 1, :])
                dv_s[...] += jnp.where(sel, _dot(pt.astype(BF16), dov), 0.0)
                dk_s[...] += jnp.where(sel, _dot(dst.astype(BF16), q), 0.0)
                dc_s[...] += jnp.where(lane == head, -jnp.sum(dst, axis=1, keepdims=True), 0.0)

        @pl.when(qi > kj)
        def _():
            step(False)

        @pl.when(qi == kj)
        def _():
            step(True)

        @pl.when(qi == nk - 1)
        def _():
            dk_ref[...] = (dk_s[...] * scale).astype(dk_ref.dtype)
            dv_ref[...] = dv_s[...].astype(dv_ref.dtype)
            dc_ref[0, 0] = dc_s[...]

    return pl.pallas_call(
        body, name="fox_bwd_kv", grid=(bsz, HEAD_PAIRS, nk, nk),
        in_specs=[pl.BlockSpec((tk, LANES), lambda b, h, j, i: (b * nk + jnp.maximum(i, j), h)),
                  pl.BlockSpec((tk, LANES), lambda b, h, j, i: (b * nk + j, kb + h)),
                  pl.BlockSpec((tk, LANES), lambda b, h, j, i: (b * nk + j, vb + h)),
                  pl.BlockSpec((tk, LANES), lambda b, h, j, i: (b * nk + jnp.maximum(i, j), h)),
                  pl.BlockSpec((tk, LANES), lambda b, h, j, i: (b * nk + jnp.maximum(i, j), h)),
                  pl.BlockSpec((1, 1, SUBLANES, tk), lambda b, h, j, i: (b, h, 0, jnp.maximum(i, j))),
                  pl.BlockSpec((1, tk, LANES), lambda b, h, j, i: (b, j, 0)),
                  pl.BlockSpec((1, SUBLANES, tk), lambda b, h, j, i: (b, 0, jnp.maximum(i, j)))],
        out_specs=[pl.BlockSpec((tk, LANES), lambda b, h, j, i: (b * nk + j, h)),
                   pl.BlockSpec((tk, LANES), lambda b, h, j, i: (b * nk + j, h)),
                   pl.BlockSpec((1, 1, tk, LANES), lambda b, h, j, i: (b, h, j, 0))],
        out_shape=[jax.ShapeDtypeStruct((bsz * seq, ATTN_WIDTH), BF16),
                   jax.ShapeDtypeStruct((bsz * seq, ATTN_WIDTH), BF16),
                   jax.ShapeDtypeStruct((bsz, HEAD_PAIRS, seq, LANES), F32)],
        scratch_shapes=[pltpu.VMEM((tk, LANES), F32), pltpu.VMEM((tk, LANES), F32),
                        pltpu.VMEM((tk, LANES), F32)],
        compiler_params=_params(("parallel", "parallel", "parallel", "arbitrary")),
    )(qkv, qkv, qkv, do, o, lser, cumcol, cumrow)


def fox_bwd(qkv, do, o, lsec, lser, cumcol, cumrow, rest, bf, bsz, seq, f_blk):
    dq, dcq = fox_bwd_q(qkv, do, o, lsec, cumcol, cumrow, bsz, seq)
    dk, dv, dcp = fox_bwd_kv(qkv, do, o, lser, cumcol, cumrow, bsz, seq)
    df, dbf = forget_bwd(dcq, dcp, rest, bf, bsz, seq, f_blk)
    return dq, dk, dv, df, dbf


TAB_STEP = 0
TAB_FWD = 8
TAB_BWD = 16
TAB_ROWS = 24


def _ssm_tile(seq):
    return 256 if seq >= 1024 else 64


def _scan_block(xr, xi, tab_ref, re, im, cr, ci, reverse):
    row = lax.broadcasted_iota(jnp.int32, xr.shape, 0)
    sign = -1.0 if reverse else 1.0
    for n, s in enumerate((1, 2, 4)):
        ar = tab_ref[TAB_STEP + n:TAB_STEP + n + 1, re]
        ai = tab_ref[TAB_STEP + n:TAB_STEP + n + 1, im] * sign
        if reverse:
            keep = row < SUBLANES - s
            sr = jnp.where(keep, pltpu.roll(xr, SUBLANES - s, 0), 0.0)
            si = jnp.where(keep, pltpu.roll(xi, SUBLANES - s, 0), 0.0)
        else:
            keep = row >= s
            sr = jnp.where(keep, pltpu.roll(xr, s, 0), 0.0)
            si = jnp.where(keep, pltpu.roll(xi, s, 0), 0.0)
        xr, xi = xr + ar * sr - ai * si, xi + ar * si + ai * sr
    base = TAB_BWD if reverse else TAB_FWD
    pr = tab_ref[base:base + SUBLANES, re]
    pi = tab_ref[base:base + SUBLANES, im] * sign
    xr, xi = xr + pr * cr - pi * ci, xi + pr * ci + pi * cr
    return xr, xi


def ssm_fwd(rest, wb4, wc4, tabs, dskip, bsz, seq, u_blk):
    tt = _ssm_tile(seq)
    nt = seq // tt

    def body(u_ref, wb_ref, wc_ref, tab_ref, d_ref, y_ref, h_ref, carry):
        c = pl.program_id(1)

        @pl.when(c == 0)
        def _():
            carry[...] = jnp.zeros_like(carry)

        u = u_ref[...]
        ub = u.astype(BF16)
        for j in range(SSM_CHUNKS):
            h_ref[:, j * CHUNK_LANES:(j + 1) * CHUNK_LANES] = _dot(ub[:, j * LANES:(j + 1) * LANES], wb_ref[j])
        for j in range(SSM_CHUNKS):
            re = slice(j * CHUNK_LANES, j * CHUNK_LANES + CHUNK_STATES)
            im = slice(j * CHUNK_LANES + CHUNK_STATES, (j + 1) * CHUNK_LANES)

            def blk(bi, car):
                r0 = pl.multiple_of(bi * SUBLANES, SUBLANES)
                xr, xi = _scan_block(h_ref[pl.ds(r0, SUBLANES), re], h_ref[pl.ds(r0, SUBLANES), im],
                                     tab_ref, re, im, car[0], car[1], False)
                h_ref[pl.ds(r0, SUBLANES), re] = xr
                h_ref[pl.ds(r0, SUBLANES), im] = xi
                return xr[SUBLANES - 1:SUBLANES], xi[SUBLANES - 1:SUBLANES]

            cr, ci = lax.fori_loop(0, tt // SUBLANES, blk, (carry[0:1, re], carry[0:1, im]))
            carry[0:1, re] = cr
            carry[0:1, im] = ci
        for j in range(SSM_CHUNKS):
            hj = h_ref[:, j * CHUNK_LANES:(j + 1) * CHUNK_LANES].astype(BF16)
            cols = slice(j * LANES, (j + 1) * LANES)
            y_ref[:, cols] = _dot(hj, wc_ref[j]) + d_ref[:, cols] * u[:, cols]

    return pl.pallas_call(
        body, name="ssm_fwd", grid=(bsz, nt),
        in_specs=[pl.BlockSpec((tt, SSM_WIDTH), lambda b, c: (b * nt + c, u_blk)),
                  pl.BlockSpec((SSM_CHUNKS, LANES, CHUNK_LANES), lambda b, c: (0, 0, 0)),
                  pl.BlockSpec((SSM_CHUNKS, CHUNK_LANES, LANES), lambda b, c: (0, 0, 0)),
                  pl.BlockSpec((TAB_ROWS, STATE_LANES), lambda b, c: (0, 0)),
                  pl.BlockSpec((1, SSM_WIDTH), lambda b, c: (0, 0))],
        out_specs=[pl.BlockSpec((tt, SSM_WIDTH), lambda b, c: (b * nt + c, 0)),
                   pl.BlockSpec((tt, STATE_LANES), lambda b, c: (b * nt + c, 0))],
        out_shape=[jax.ShapeDtypeStruct((bsz * seq, SSM_WIDTH), F32),
                   jax.ShapeDtypeStruct((bsz * seq, STATE_LANES), F32)],
        scratch_shapes=[pltpu.VMEM((SUBLANES, STATE_LANES), F32)],
        compiler_params=_params(("parallel", "arbitrary")),
    )(rest, wb4, wc4, tabs, dskip)


def ssm_bwd(dys, rest, hs, wb4, wc4, tabs, dskip, bsz, seq, u_blk):
    tt = _ssm_tile(seq)
    nt = seq // tt
    nb = tt // SUBLANES

    def body(dy_ref, u_ref, h_ref, hp_ref, wb_ref, wc_ref, tab_ref, d_ref,
             du_ref, ga_ref, gwb_ref, gwc_ref, gd_ref, g_s, carry):
        b = pl.program_id(0)
        c = pl.program_id(1)

        @pl.when(c == 0)
        def _():
            carry[...] = jnp.zeros_like(carry)

        @pl.when((b == 0) & (c == 0))
        def _():
            ga_ref[...] = jnp.zeros_like(ga_ref)
            gwb_ref[...] = jnp.zeros_like(gwb_ref)
            gwc_ref[...] = jnp.zeros_like(gwc_ref)
            gd_ref[...] = jnp.zeros_like(gd_ref)

        dy = dy_ref[...].astype(F32)
        dyb = dy.astype(BF16)
        u = u_ref[...]
        ub = u.astype(BF16)
        first_chunk = c == nt - 1
        for j in range(SSM_CHUNKS):
            g_s[:, j * CHUNK_LANES:(j + 1) * CHUNK_LANES] = _dot(dyb[:, j * LANES:(j + 1) * LANES], wc_ref[j], NT)
        for j in range(SSM_CHUNKS):
            re = slice(j * CHUNK_LANES, j * CHUNK_LANES + CHUNK_STATES)
            im = slice(j * CHUNK_LANES + CHUNK_STATES, (j + 1) * CHUNK_LANES)
            row = lax.broadcasted_iota(jnp.int32, (SUBLANES, CHUNK_STATES), 0)

            def blk(n, car):
                bi = nb - 1 - n
                r0 = pl.multiple_of(bi * SUBLANES, SUBLANES)
                gr, gi = _scan_block(g_s[pl.ds(r0, SUBLANES), re], g_s[pl.ds(r0, SUBLANES), im],
                                     tab_ref, re, im, car[0], car[1], True)
                g_s[pl.ds(r0, SUBLANES), re] = gr
                g_s[pl.ds(r0, SUBLANES), im] = gi
                rp = pl.multiple_of(jnp.maximum(bi - 1, 0) * SUBLANES, SUBLANES)
                inside = bi > 0
                live = jnp.where(jnp.logical_or(inside, jnp.logical_not(first_chunk)), 1.0, 0.0)
                pr = jnp.where(inside, h_ref[pl.ds(rp, SUBLANES), re], hp_ref[:, re])[SUBLANES - 1:SUBLANES] * live
                pi = jnp.where(inside, h_ref[pl.ds(rp, SUBLANES), im], hp_ref[:, im])[SUBLANES - 1:SUBLANES] * live
                hr = jnp.where(row >= 1, pltpu.roll(h_ref[pl.ds(r0, SUBLANES), re], 1, 0), pr)
                hi = jnp.where(row >= 1, pltpu.roll(h_ref[pl.ds(r0, SUBLANES), im], 1, 0), pi)
                return (gr[0:1], gi[0:1], car[2] + gr * hr + gi * hi, car[3] + gi * hr - gr * hi)

            zero = jnp.zeros((SUBLANES, CHUNK_STATES), F32)
            cr, ci, sr, si = lax.fori_loop(0, nb, blk, (carry[0:1, re], carry[0:1, im], zero, zero))
            carry[0:1, re] = cr
            carry[0:1, im] = ci
            ga_ref[:, re] += sr
            ga_ref[:, im] += si
        for j in range(SSM_CHUNKS):
            cols = slice(j * LANES, (j + 1) * LANES)
            lanes = slice(j * CHUNK_LANES, (j + 1) * CHUNK_LANES)
            gj = g_s[:, lanes].astype(BF16)
            du_ref[:, cols] = (_dot(gj, wb_ref[j], NT) + d_ref[:, cols] * dy[:, cols]).astype(du_ref.dtype)
            gwb_ref[j] += _dot(ub[:, cols], gj, TN)
            gwc_ref[j] += _dot(h_ref[:, lanes].astype(BF16), dyb[:, cols], TN)
        gd_ref[...] += _fold8(dy * u)

    def prev_rows(b, c):
        chunk = nt - 1 - c
        return (jnp.maximum((b * nt + chunk) * nb - 1, 0), 0)

    return pl.pallas_call(
        body, name="ssm_bwd", grid=(bsz, nt),
        in_specs=[pl.BlockSpec((tt, SSM_WIDTH), lambda b, c: (b * nt + nt - 1 - c, 0)),
                  pl.BlockSpec((tt, SSM_WIDTH), lambda b, c: (b * nt + nt - 1 - c, u_blk)),
                  pl.BlockSpec((tt, STATE_LANES), lambda b, c: (b * nt + nt - 1 - c, 0)),
                  pl.BlockSpec((SUBLANES, STATE_LANES), prev_rows),
                  pl.BlockSpec((SSM_CHUNKS, LANES, CHUNK_LANES), lambda b, c: (0, 0, 0)),
                  pl.BlockSpec((SSM_CHUNKS, CHUNK_LANES, LANES), lambda b, c: (0, 0, 0)),
                  pl.BlockSpec((TAB_ROWS, STATE_LANES), lambda b, c: (0, 0)),
                  pl.BlockSpec((1, SSM_WIDTH), lambda b, c: (0, 0))],
        out_specs=[pl.BlockSpec((tt, SSM_WIDTH), lambda b, c: (b * nt + nt - 1 - c, 0)),
                   pl.BlockSpec((SUBLANES, STATE_LANES), lambda b, c: (0, 0)),
                   pl.BlockSpec((SSM_CHUNKS, LANES, CHUNK_LANES), lambda b, c: (0, 0, 0)),
                   pl.BlockSpec((SSM_CHUNKS, CHUNK_LANES, LANES), lambda b, c: (0, 0, 0)),
                   pl.BlockSpec((SUBLANES, SSM_WIDTH), lambda b, c: (0, 0))],
        out_shape=[jax.ShapeDtypeStruct((bsz * seq, SSM_WIDTH), BF16),
                   jax.ShapeDtypeStruct((SUBLANES, STATE_LANES), F32),
                   jax.ShapeDtypeStruct((SSM_CHUNKS, LANES, CHUNK_LANES), F32),
                   jax.ShapeDtypeStruct((SSM_CHUNKS, CHUNK_LANES, LANES), F32),
                   jax.ShapeDtypeStruct((SUBLANES, SSM_WIDTH), F32)],
        scratch_shapes=[pltpu.VMEM((tt, STATE_LANES), F32), pltpu.VMEM((SUBLANES, STATE_LANES), F32)],
        compiler_params=_params(("arbitrary", "arbitrary")),
    )(dys, rest, hs, hs, wb4, wc4, tabs, dskip)


def _gelu(v):
    t = jnp.tanh(GELU_C * (v + GELU_A * v * v * v))
    return 0.5 * v * (1.0 + t), t


def mix_fwd(ya, ys, rest, x0, wglu, bglu, wba, wbb, wout):
    t, d = x0.shape
    tm = _rows(t, 256)

    def body(ya_ref, ys_ref, ga_ref, gb_ref, x_ref, wg_ref, bg_ref, wa_ref, wb_ref, wo_ref,
             x1_ref, z_ref, pa_ref, pb_ref, yb_ref, yb2_ref, mx_ref):
        yb, _ = _gelu(ys_ref[...])
        ybb = yb.astype(BF16)
        z = _dot(ybb, wg_ref[...]) + bg_ref[...]
        yb2 = (yb * _sigmoid(z)).astype(BF16)
        pa = _dot(ya_ref[...], wa_ref[...])
        pb = _dot(yb2, wb_ref[...])
        mixed = (_sigmoid(ga_ref[...]) * pa + _sigmoid(gb_ref[...]) * pb).astype(BF16)
        x1_ref[...] = x_ref[...] + _dot(mixed, wo_ref[...])
        z_ref[...] = z.astype(z_ref.dtype)
        pa_ref[...] = pa.astype(pa_ref.dtype)
        pb_ref[...] = pb.astype(pb_ref.dtype)
        yb_ref[...] = ybb
        yb2_ref[...] = yb2
        mx_ref[...] = mixed

    row = lambda w: pl.BlockSpec((tm, w), lambda i: (i, 0))
    full = lambda a: pl.BlockSpec(a.shape, lambda i: (0,) * a.ndim)
    return pl.pallas_call(
        body, name="mix_fwd", grid=(t // tm,),
        in_specs=[row(ATTN_WIDTH), row(SSM_WIDTH),
                  pl.BlockSpec((tm, d), lambda i: (i, 0)), pl.BlockSpec((tm, d), lambda i: (i, 1)),
                  row(d), full(wglu), full(bglu), full(wba), full(wbb), full(wout)],
        out_specs=[row(d), row(SSM_WIDTH), row(d), row(d), row(SSM_WIDTH), row(SSM_WIDTH), row(d)],
        out_shape=[jax.ShapeDtypeStruct((t, d), F32), jax.ShapeDtypeStruct((t, SSM_WIDTH), BF16),
                   jax.ShapeDtypeStruct((t, d), BF16), jax.ShapeDtypeStruct((t, d), BF16),
                   jax.ShapeDtypeStruct((t, SSM_WIDTH), BF16), jax.ShapeDtypeStruct((t, SSM_WIDTH), BF16),
                   jax.ShapeDtypeStruct((t, d), BF16)],
        compiler_params=_params(("parallel",)),
    )(ya, ys, rest, rest, x0, wglu, bglu, wba, wbb, wout)


def mix_bwd(dx1, rest, pa, pb, z, ys, wglu, wba, wbb, wout):
    t, d = dx1.shape
    tm = _rows(t, 256)

    def body(dx_ref, ga_ref, gb_ref, pa_ref, pb_ref, z_ref, ys_ref, wg_ref, wa_ref, wb_ref, wo_ref,
             dya_ref, dys_ref, dg_ref, dpa_ref, dpb_ref, dz_ref, dbg_ref):
        @pl.when(pl.program_id(0) == 0)
        def _():
            dbg_ref[...] = jnp.zeros_like(dbg_ref)

        dmix = _dot(dx_ref[...].astype(BF16), wo_ref[...], NT)
        sa = _sigmoid(ga_ref[...])
        sb = _sigmoid(gb_ref[...])
        dpa = (dmix * sa).astype(BF16)
        dpb = (dmix * sb).astype(BF16)
        dg_ref[:, 0:d] = (dmix * pa_ref[...].astype(F32) * sa * (1.0 - sa)).astype(dg_ref.dtype)
        dg_ref[:, d:2 * d] = (dmix * pb_ref[...].astype(F32) * sb * (1.0 - sb)).astype(dg_ref.dtype)
        dpa_ref[...] = dpa
        dpb_ref[...] = dpb
        dya_ref[...] = _dot(dpa, wa_ref[...], NT).astype(dya_ref.dtype)
        dyb2 = _dot(dpb, wb_ref[...], NT)
        ys = ys_ref[...]
        yb, th = _gelu(ys)
        sg = _sigmoid(z_ref[...].astype(F32))
        dz = dyb2 * yb * sg * (1.0 - sg)
        dzb = dz.astype(BF16)
        dz_ref[...] = dzb
        dbg_ref[...] += _fold8(dz)
        dyb = dyb2 * sg + _dot(dzb, wg_ref[...], NT)
        dgelu = 0.5 * (1.0 + th) + 0.5 * ys * (1.0 - th * th) * GELU_C * (1.0 + 3.0 * GELU_A * ys * ys)
        dys_ref[...] = (dyb * dgelu).astype(dys_ref.dtype)

    row = lambda w: pl.BlockSpec((tm, w), lambda i: (i, 0))
    full = lambda a: pl.BlockSpec(a.shape, lambda i: (0,) * a.ndim)
    return pl.pallas_call(
        body, name="mix_bwd", grid=(t // tm,),
        in_specs=[row(d), pl.BlockSpec((tm, d), lambda i: (i, 0)), pl.BlockSpec((tm, d), lambda i: (i, 1)),
                  row(d), row(d), row(SSM_WIDTH), row(SSM_WIDTH), full(wglu), full(wba), full(wbb), full(wout)],
        out_specs=[row(ATTN_WIDTH), row(SSM_WIDTH), row(2 * d), row(d), row(d), row(SSM_WIDTH),
                   pl.BlockSpec((SUBLANES, SSM_WIDTH), lambda i: (0, 0))],
        out_shape=[jax.ShapeDtypeStruct((t, ATTN_WIDTH), BF16), jax.ShapeDtypeStruct((t, SSM_WIDTH), BF16),
                   jax.ShapeDtypeStruct((t, 2 * d), BF16), jax.ShapeDtypeStruct((t, d), BF16),
                   jax.ShapeDtypeStruct((t, d), BF16), jax.ShapeDtypeStruct((t, SSM_WIDTH), BF16),
                   jax.ShapeDtypeStruct((SUBLANES, SSM_WIDTH), F32)],
        compiler_params=_params(("arbitrary",)),
    )(dx1, rest, rest, pa, pb, z, ys, wglu, wba, wbb, wout)


def mlp_fwd(x1, g, wup, wdown):
    t, d = x1.shape
    ff = wup.shape[1]
    tm, tf = _rows(t, 512), _pick(ff, 1024)
    nf = ff // tf

    def body(x_ref, g_ref, wu_ref, wd_ref, x2_ref, up_ref, h_s, acc_s):
        f = pl.program_id(1)

        @pl.when(f == 0)
        def _():
            xv = x_ref[...]
            h_s[...] = (xv * _rms_scale(xv) * g_ref[...]).astype(BF16)
            acc_s[...] = jnp.zeros_like(acc_s)

        up = _dot(h_s[...], wu_ref[...])
        up_ref[...] = up.astype(up_ref.dtype)
        act = jnp.square(jnp.maximum(up, 0.0)).astype(BF16)
        acc_s[...] += _dot(act, wd_ref[...])

        @pl.when(f == nf - 1)
        def _():
            x2_ref[...] = x_ref[...] + acc_s[...]

    return pl.pallas_call(
        body, name="mlp_fwd", grid=(t // tm, nf),
        in_specs=[pl.BlockSpec((tm, d), lambda i, f: (i, 0)), pl.BlockSpec((1, d), lambda i, f: (0, 0)),
                  pl.BlockSpec((d, tf), lambda i, f: (0, f)), pl.BlockSpec((tf, d), lambda i, f: (f, 0))],
        out_specs=[pl.BlockSpec((tm, d), lambda i, f: (i, 0)), pl.BlockSpec((tm, tf), lambda i, f: (i, f))],
        out_shape=[jax.ShapeDtypeStruct((t, d), F32), jax.ShapeDtypeStruct((t, ff), BF16)],
        scratch_shapes=[pltpu.VMEM((tm, d), BF16), pltpu.VMEM((tm, d), F32)],
        compiler_params=_params(("parallel", "arbitrary")),
    )(x1, g, wup, wdown)


def mlp_bwd(dx2, up, x1, g, wup, wdown):
    t, d = x1.shape
    ff = wup.shape[1]
    tm, tf = _rows(t, 512), _pick(ff, 1024)
    nf = ff // tf

    def body(dx_ref, up_ref, x_ref, g_ref, wu_ref, wd_ref, dup_ref, dx1_ref, dg_ref, dxb_s, acc_s):
        i = pl.program_id(0)
        f = pl.program_id(1)

        @pl.when((i == 0) & (f == 0))
        def _():
            dg_ref[...] = jnp.zeros_like(dg_ref)

        @pl.when(f == 0)
        def _():
            dxb_s[...] = dx_ref[...].astype(BF16)
            acc_s[...] = jnp.zeros_like(acc_s)

        dact = _dot(dxb_s[...], wd_ref[...], NT)
        dup = (dact * 2.0 * jnp.maximum(up_ref[...].astype(F32), 0.0)).astype(BF16)
        dup_ref[...] = dup
        acc_s[...] += _dot(dup, wu_ref[...], NT)

        @pl.when(f == nf - 1)
        def _():
            dxn, dgain = _rms_bwd(x_ref[...], g_ref[...], acc_s[...])
            dx1_ref[...] = dx_ref[...] + dxn
            dg_ref[...] += _fold8(dgain)

    return pl.pallas_call(
        body, name="mlp_bwd", grid=(t // tm, nf),
        in_specs=[pl.BlockSpec((tm, d), lambda i, f: (i, 0)), pl.BlockSpec((tm, tf), lambda i, f: (i, f)),
                  pl.BlockSpec((tm, d), lambda i, f: (i, 0)), pl.BlockSpec((1, d), lambda i, f: (0, 0)),
                  pl.BlockSpec((d, tf), lambda i, f: (0, f)), pl.BlockSpec((tf, d), lambda i, f: (f, 0))],
        out_specs=[pl.BlockSpec((tm, tf), lambda i, f: (i, f)), pl.BlockSpec((tm, d), lambda i, f: (i, 0)),
                   pl.BlockSpec((SUBLANES, d), lambda i, f: (0, 0))],
        out_shape=[jax.ShapeDtypeStruct((t, ff), BF16), jax.ShapeDtypeStruct((t, d), F32),
                   jax.ShapeDtypeStruct((SUBLANES, d), F32)],
        scratch_shapes=[pltpu.VMEM((tm, d), BF16), pltpu.VMEM((tm, d), F32)],
        compiler_params=_params(("arbitrary", "arbitrary")),
    )(dx2, up, x1, g, wup, wdown)


def proj_bwd(dproj, wpad, x0, dx1, g):
    t, d = x0.shape
    m = wpad.shape[1]
    tm = _rows(t, 256)

    def body(dp_ref, w_ref, x_ref, dx1_ref, g_ref, dx0_ref, dg_ref):
        @pl.when(pl.program_id(0) == 0)
        def _():
            dg_ref[...] = jnp.zeros_like(dg_ref)

        dh = _dot(dp_ref[...], w_ref[...], NT)
        dxn, dgain = _rms_bwd(x_ref[...], g_ref[...], dh)
        dx0_ref[...] = dx1_ref[...] + dxn
        dg_ref[...] += _fold8(dgain)

    return pl.pallas_call(
        body, name="proj_bwd", grid=(t // tm,),
        in_specs=[pl.BlockSpec((tm, m), lambda i: (i, 0)), pl.BlockSpec((d, m), lambda i: (0, 0)),
                  pl.BlockSpec((tm, d), lambda i: (i, 0)), pl.BlockSpec((tm, d), lambda i: (i, 0)),
                  pl.BlockSpec((1, d), lambda i: (0, 0))],
        out_specs=[pl.BlockSpec((tm, d), lambda i: (i, 0)), pl.BlockSpec((SUBLANES, d), lambda i: (0, 0))],
        out_shape=[jax.ShapeDtypeStruct((t, d), F32), jax.ShapeDtypeStruct((SUBLANES, d), F32)],
        compiler_params=_params(("arbitrary",)),
    )(dproj, wpad, x0, dx1, g)


def final_loss(x, g, target):
    t, d = x.shape
    tm = _rows(t, 512)

    def body(x_ref, g_ref, t_ref, dx_ref, ls_ref, dg_ref):
        @pl.when(pl.program_id(0) == 0)
        def _():
            ls_ref[...] = jnp.zeros_like(ls_ref)
            dg_ref[...] = jnp.zeros_like(dg_ref)

        xv = x_ref[...]
        gv = g_ref[...]
        err = xv * _rms_scale(xv) * gv - t_ref[...]
        ls_ref[...] += _fold8(err * err) * (0.5 / d)
        dxn, dgain = _rms_bwd(xv, gv, err * (1.0 / d))
        dx_ref[...] = dxn
        dg_ref[...] += _fold8(dgain)

    return pl.pallas_call(
        body, name="final_loss", grid=(t // tm,),
        in_specs=[pl.BlockSpec((tm, d), lambda i: (i, 0)), pl.BlockSpec((1, d), lambda i: (0, 0)),
                  pl.BlockSpec((tm, d), lambda i: (i, 0))],
        out_specs=[pl.BlockSpec((tm, d), lambda i: (i, 0)), pl.BlockSpec((SUBLANES, d), lambda i: (0, 0)),
                   pl.BlockSpec((SUBLANES, d), lambda i: (0, 0))],
        out_shape=[jax.ShapeDtypeStruct((t, d), F32), jax.ShapeDtypeStruct((SUBLANES, d), F32),
                   jax.ShapeDtypeStruct((SUBLANES, d), F32)],
        compiler_params=_params(("arbitrary",)),
    )(x, g, target)


def _place():
    x, y, c = lax.axis_index("x"), lax.axis_index("y"), lax.axis_index("c")
    chips = [(1 - x, y), (x, 1 - y), (1 - x, 1 - y)]
    return x, y, c, chips


def gather_weights(wp):
    nl = wp.shape[0]
    lh = nl // 2

    def body(w_ref, out_ref, send_sems, recv_sems, local_sem):
        x, y, c, chips = _place()
        me = 2 * x + y
        sibling = (x, y, 1 - c)

        def half(chip_idx, core):
            return out_ref.at[chip_idx, pl.ds(core * lh, lh)]

        def copy(k, src, dst, to):
            return pltpu.make_async_remote_copy(src_ref=src, dst_ref=dst, send_sem=send_sems.at[k],
                                                recv_sem=recv_sems.at[k], device_id=to, device_id_type=MESH)

        mine = pltpu.make_async_copy(w_ref, out_ref.at[me], local_sem)
        mine.start()
        first = [copy(j, w_ref.at[pl.ds(c * lh, lh)], half(me, c), (*chip, c)) for j, chip in enumerate(chips)]
        for cp in first:
            cp.start()
        passed = []
        for j, (cx, cy) in enumerate(chips):
            idx = 2 * cx + cy
            copy(j, half(idx, c), half(idx, c), (x, y, c)).wait_recv()
            fwd = copy(3 + j, half(idx, c), half(idx, c), sibling)
            fwd.start()
            passed.append(fwd)
        for j, (cx, cy) in enumerate(chips):
            idx = 2 * cx + cy
            copy(3 + j, half(idx, 1 - c), half(idx, 1 - c), (x, y, c)).wait_recv()
        for cp in first + passed:
            cp.wait_send()
        mine.wait()

    return pl.pallas_call(
        body, name="gather_weights",
        in_specs=[ANY], out_specs=ANY,
        out_shape=jax.ShapeDtypeStruct((4,) + wp.shape, wp.dtype),
        scratch_shapes=[pltpu.SemaphoreType.DMA((6,)), pltpu.SemaphoreType.DMA((6,)), pltpu.SemaphoreType.DMA],
    )(wp)


def exchange_sibling_half(gp):
    nl = gp.shape[1]
    lh = nl // 2

    def body(g_ref, out_ref, send_sem, recv_sem):
        x, y, c, _ = _place()
        cp = pltpu.make_async_remote_copy(src_ref=g_ref.at[:, pl.ds((1 - c) * lh, lh)], dst_ref=out_ref,
                                          send_sem=send_sem, recv_sem=recv_sem,
                                          device_id=(x, y, 1 - c), device_id_type=MESH)
        cp.start()
        cp.wait()

    return pl.pallas_call(
        body, name="exchange_sibling_half",
        in_specs=[ANY], out_specs=ANY,
        out_shape=jax.ShapeDtypeStruct((4, lh) + gp.shape[2:], gp.dtype),
        scratch_shapes=[pltpu.SemaphoreType.DMA, pltpu.SemaphoreType.DMA],
    )(gp)


def scatter_to_owners(a):
    def body(a_ref, out_ref, send_sems, recv_sems, local_sem):
        x, y, c, chips = _place()
        me = 2 * x + y
        mine = pltpu.make_async_copy(a_ref.at[me], out_ref.at[3], local_sem)
        mine.start()
        sends = []
        for j, (cx, cy) in enumerate(chips):
            cp = pltpu.make_async_remote_copy(src_ref=a_ref.at[2 * cx + cy], dst_ref=out_ref.at[j],
                                              send_sem=send_sems.at[j], recv_sem=recv_sems.at[j],
                                              device_id=(cx, cy, c), device_id_type=MESH)
            cp.start()
            sends.append(cp)
        for cp in sends:
            cp.wait()
        mine.wait()

    return pl.pallas_call(
        body, name="scatter_to_owners",
        in_specs=[ANY], out_specs=ANY,
        out_shape=jax.ShapeDtypeStruct(a.shape, a.dtype),
        scratch_shapes=[pltpu.SemaphoreType.DMA((3,)), pltpu.SemaphoreType.DMA((3,)), pltpu.SemaphoreType.DMA],
    )(a)


def share_with_sibling(r):
    lh = r.shape[0]

    def body(r_ref, out_ref, send_sem, recv_sem, local_sem):
        x, y, c, _ = _place()
        mine = pltpu.make_async_copy(r_ref, out_ref.at[pl.ds(c * lh, lh)], local_sem)
        mine.start()
        cp = pltpu.make_async_remote_copy(src_ref=r_ref, dst_ref=out_ref.at[pl.ds(c * lh, lh)],
                                          send_sem=send_sem, recv_sem=recv_sem,
                                          device_id=(x, y, 1 - c), device_id_type=MESH)
        cp.start()
        cp.wait()
        mine.wait()

    return pl.pallas_call(
        body, name="share_with_sibling",
        in_specs=[ANY], out_specs=ANY,
        out_shape=jax.ShapeDtypeStruct((2 * lh,) + r.shape[1:], r.dtype),
        scratch_shapes=[pltpu.SemaphoreType.DMA, pltpu.SemaphoreType.DMA, pltpu.SemaphoreType.DMA],
    )(r)


def exchange_all(sp):
    def body(s_ref, out_ref, send_sems, recv_sems, local_sem):
        x, y, c, _ = _place()
        me = 4 * x + 2 * y + c
        mine = pltpu.make_async_copy(s_ref, out_ref.at[me], local_sem)
        mine.start()
        sends = []
        for m in range(1, 8):
            fx, fy, fc = (m >> 2) & 1, (m >> 1) & 1, m & 1
            peer = ((1 - x) if fx else x, (1 - y) if fy else y, (1 - c) if fc else c)
            cp = pltpu.make_async_remote_copy(src_ref=s_ref, dst_ref=out_ref.at[me],
                                              send_sem=send_sems.at[m - 1], recv_sem=recv_sems.at[m - 1],
                                              device_id=peer, device_id_type=MESH)
            cp.start()
            sends.append(cp)
        for cp in sends:
            cp.wait()
        mine.wait()

    return pl.pallas_call(
        body, name="exchange_all",
        in_specs=[ANY], out_specs=ANY,
        out_shape=jax.ShapeDtypeStruct((8,) + sp.shape, sp.dtype),
        scratch_shapes=[pltpu.SemaphoreType.DMA((7,)), pltpu.SemaphoreType.DMA((7,)), pltpu.SemaphoreType.DMA],
    )(sp)


def add_sibling(gp, s1, core):
    _, nl, r, cdim = gp.shape
    lh = nl // 2
    tr = _rows(r, PACK_ROW_TILE)

    def body(c_ref, g_ref, s_ref, o_ref):
        o_ref[...] = g_ref[...] + s_ref[...]

    return pl.pallas_call(
        body, name="add_sibling",
        grid_spec=pltpu.PrefetchScalarGridSpec(
            num_scalar_prefetch=1, grid=(4, lh, r // tr),
            in_specs=[pl.BlockSpec((1, 1, tr, cdim), lambda k, l, i, c_ref: (k, c_ref[0] * lh + l, i, 0)),
                      pl.BlockSpec((1, 1, tr, cdim), lambda k, l, i, c_ref: (k, l, i, 0))],
            out_specs=pl.BlockSpec((1, 1, tr, cdim), lambda k, l, i, c_ref: (k, l, i, 0))),
        out_shape=jax.ShapeDtypeStruct(s1.shape, F32),
        compiler_params=_params(("parallel", "parallel", "parallel")),
    )(core, gp, s1)


def add_chips(s2):
    _, lh, r, cdim = s2.shape
    tr = _rows(r, PACK_ROW_TILE)

    def body(s_ref, o_ref):
        o_ref[0] = ((s_ref[0, 0] + s_ref[1, 0]) + s_ref[2, 0]) + s_ref[3, 0]

    return pl.pallas_call(
        body, name="add_chips", grid=(lh, r // tr),
        in_specs=[pl.BlockSpec((4, 1, tr, cdim), lambda l, i: (0, l, i, 0))],
        out_specs=pl.BlockSpec((1, tr, cdim), lambda l, i: (l, i, 0)),
        out_shape=jax.ShapeDtypeStruct(s2.shape[1:], F32),
        compiler_params=_params(("parallel", "parallel")),
    )(s2)


def _adamw_math(w, g, m, v):
    m = ADAM_B1 * m + (1.0 - ADAM_B1) * g
    v = ADAM_B2 * v + (1.0 - ADAM_B2) * (g * g)
    m_hat = m / (1.0 - ADAM_B1 ** ADAM_STEP)
    v_hat = v / (1.0 - ADAM_B2 ** ADAM_STEP)
    delta = -ADAM_LR * (m_hat / (jnp.sqrt(v_hat) + ADAM_EPS) + ADAM_WD * w)
    return delta, m, v


def adamw(w, g, m, v, name):
    r, cdim = w.shape
    tr = _rows(r, PACK_ROW_TILE)

    def body(w_ref, g_ref, m_ref, v_ref, d_ref, nm_ref, nv_ref):
        d, nm, nv = _adamw_math(w_ref[...], g_ref[...], m_ref[...], v_ref[...])
        d_ref[...] = d
        nm_ref[...] = nm
        nv_ref[...] = nv

    spec = pl.BlockSpec((tr, cdim), lambda i: (i, 0))
    return pl.pallas_call(
        body, name=name, grid=(r // tr,),
        in_specs=[spec] * 4, out_specs=[spec] * 3,
        out_shape=[jax.ShapeDtypeStruct(w.shape, F32)] * 3,
        compiler_params=_params(("parallel",)),
    )(w, g, m, v)


def sum_and_adamw(parts, w, m, v):
    _, r, cdim = parts.shape
    tr = _rows(r, PACK_ROW_TILE)

    def body(p_ref, w_ref, m_ref, v_ref, g_ref, d_ref, nm_ref, nv_ref):
        g = p_ref[0]
        for k in range(1, 8):
            g = g + p_ref[k]
        d, nm, nv = _adamw_math(w_ref[...], g, m_ref[...], v_ref[...])
        g_ref[...] = g
        d_ref[...] = d
        nm_ref[...] = nm
        nv_ref[...] = nv

    spec = pl.BlockSpec((tr, cdim), lambda i: (i, 0))
    return pl.pallas_call(
        body, name="sum_and_adamw", grid=(r // tr,),
        in_specs=[pl.BlockSpec((8, tr, cdim), lambda i: (0, i, 0)), spec, spec, spec],
        out_specs=[spec] * 4,
        out_shape=[jax.ShapeDtypeStruct((r, cdim), F32)] * 4,
        compiler_params=_params(("parallel",)),
    )(parts, w, m, v)


SHARDED = ("w_in", "w_glu", "w_branch_a", "w_branch_b", "w_out", "w_mlp_up", "w_mlp_down")
SMALL = ("norm_mix", "b_forget", "ssm_lambda_re", "ssm_lambda_im", "ssm_log_dt", "ssm_b_re", "ssm_b_im",
         "ssm_c_re", "ssm_c_im", "ssm_d", "b_glu", "norm_mlp", "norm_final")


def _pack_rows(total):
    unit = PACK_COLS * PACK_ROW_TILE
    return (total + unit - 1) // unit * PACK_ROW_TILE


def pack_layers(arrs, dtype):
    nl = arrs[0].shape[0]
    flat = jnp.concatenate([a.reshape(nl, -1).astype(dtype) for a in arrs], axis=1)
    rows = _pack_rows(flat.shape[1])
    flat = jnp.pad(flat, ((0, 0), (0, rows * PACK_COLS - flat.shape[1])))
    return flat.reshape(nl, rows, PACK_COLS)


def unpack_layers(packed, shapes):
    nl = packed.shape[0]
    flat = packed.reshape(nl, -1)
    out, off = [], 0
    for s in shapes:
        n = math.prod(s)
        out.append(flat[:, off:off + n].reshape((nl,) + tuple(s)))
        off += n
    return out


def pack_flat(arrs):
    flat = jnp.concatenate([a.reshape(-1).astype(F32) for a in arrs])
    rows = _pack_rows(flat.shape[0])
    return jnp.pad(flat, (0, rows * PACK_COLS - flat.shape[0])).reshape(rows, PACK_COLS)


def unpack_flat(packed, shapes):
    flat = packed.reshape(-1)
    out, off = [], 0
    for s in shapes:
        n = math.prod(s)
        out.append(flat[off:off + n].reshape(tuple(s)))
        off += n
    return out


def _discretise(lam_re, lam_im, log_dt, b_re, b_im):
    dt = jnp.exp(log_dt)[:, None]
    mag = jnp.exp(lam_re * dt)
    ar = mag * jnp.cos(lam_im * dt)
    ai = mag * jnp.sin(lam_im * dt)
    den = lam_re * lam_re + lam_im * lam_im
    cr = ((ar - 1.0) * lam_re + ai * lam_im) / den
    ci = (ai * lam_re - (ar - 1.0) * lam_im) / den
    bbr = cr[:, :, None] * b_re - ci[:, :, None] * b_im
    bbi = cr[:, :, None] * b_im + ci[:, :, None] * b_re
    return ar, ai, bbr, bbi


def _state_lanes(re, im):
    return jnp.concatenate([re.reshape(SSM_CHUNKS, CHUNK_STATES), im.reshape(SSM_CHUNKS, CHUNK_STATES)],
                           axis=1).reshape(STATE_LANES)


def _ssm_inputs(ar, ai, bbr, bbi, c_re, c_im):
    pr, pi = [ar], [ai]
    for _ in range(SUBLANES - 1):
        pr, pi = pr + [pr[-1] * ar - pi[-1] * ai], pi + [pr[-1] * ai + pi[-1] * ar]
    power = lambda n: _state_lanes(pr[n - 1], pi[n - 1])
    zero = jnp.zeros((STATE_LANES,), F32)
    rows = [power(1), power(2), power(4)] + [zero] * 5
    rows += [power(i + 1) for i in range(SUBLANES)]
    rows += [power(SUBLANES - i) for i in range(SUBLANES)]
    tabs = jnp.stack(rows)
    eye = jnp.eye(CHUNK_GROUPS, dtype=F32)

    def to_wb(bb):
        t = bb.reshape(SSM_CHUNKS, CHUNK_GROUPS, SSM_STATE, SSM_GROUP_CH).transpose(0, 1, 3, 2)
        return jnp.einsum("jgcp,gh->jgchp", t, eye).reshape(SSM_CHUNKS, LANES, CHUNK_STATES)

    def to_wc(cc):
        t = cc.reshape(SSM_CHUNKS, CHUNK_GROUPS, SSM_GROUP_CH, SSM_STATE)
        return jnp.einsum("jgcp,gh->jhpgc", t, eye).reshape(SSM_CHUNKS, CHUNK_STATES, LANES)

    wb4 = jnp.concatenate([to_wb(bbr), to_wb(bbi)], axis=2).astype(BF16)
    wc4 = jnp.concatenate([to_wc(c_re), -to_wc(c_im)], axis=1).astype(BF16)
    return tabs, wb4, wc4


def _ssm_param_grads(ga8, gwb, gwc):
    eye = jnp.eye(CHUNK_GROUPS, dtype=F32)
    ga = jnp.sum(ga8, axis=0).reshape(SSM_CHUNKS, 2, CHUNK_STATES)
    gar = ga[:, 0].reshape(SSM_GROUPS, SSM_STATE)
    gai = ga[:, 1].reshape(SSM_GROUPS, SSM_STATE)

    def from_wb(g):
        t = g.reshape(SSM_CHUNKS, CHUNK_GROUPS, SSM_GROUP_CH, CHUNK_GROUPS, SSM_STATE)
        return jnp.einsum("jgchp,gh->jgpc", t, eye).reshape(SSM_GROUPS, SSM_STATE, SSM_GROUP_CH)

    def from_wc(g):
        t = g.reshape(SSM_CHUNKS, CHUNK_GROUPS, SSM_STATE, CHUNK_GROUPS, SSM_GROUP_CH)
        return jnp.einsum("jhpgc,gh->jgcp", t, eye).reshape(SSM_GROUPS, SSM_GROUP_CH, SSM_STATE)

    return (gar, gai, from_wb(gwb[:, :, :CHUNK_STATES]), from_wb(gwb[:, :, CHUNK_STATES:]),
            from_wc(gwc[:, :CHUNK_STATES]), -from_wc(gwc[:, CHUNK_STATES:]))


def kernel(x, norm_mix, w_in, b_forget, ssm_lambda_re, ssm_lambda_im, ssm_log_dt, ssm_b_re, ssm_b_im, ssm_c_re, ssm_c_im, ssm_d, w_glu, b_glu, w_branch_a, w_branch_b, w_out, norm_mlp, w_mlp_up, w_mlp_down, norm_final, loss_target, m_norm_mix, m_w_in, m_b_forget, m_ssm_lambda_re, m_ssm_lambda_im, m_ssm_log_dt, m_ssm_b_re, m_ssm_b_im, m_ssm_c_re, m_ssm_c_im, m_ssm_d, m_w_glu, m_b_glu, m_w_branch_a, m_w_branch_b, m_w_out, m_norm_mlp, m_w_mlp_up, m_w_mlp_down, m_norm_final, v_norm_mix, v_w_in, v_b_forget, v_ssm_lambda_re, v_ssm_lambda_im, v_ssm_log_dt, v_ssm_b_re, v_ssm_b_im, v_ssm_c_re, v_ssm_c_im, v_ssm_d, v_w_glu, v_b_glu, v_w_branch_a, v_w_branch_b, v_w_out, v_norm_mlp, v_w_mlp_up, v_w_mlp_down, v_norm_final):
    args = dict(locals())
    bsz, seq, d = x.shape
    nl = norm_mix.shape[0]
    tokens = bsz * seq
    aw, sw = ATTN_WIDTH, SSM_WIDTH
    n_in = 3 * aw + ATTN_HEADS + sw + 2 * d
    core = lax.axis_index("c").astype(jnp.int32).reshape(1)

    shard_shapes = [args[n].shape[1:] for n in SHARDED]
    wfull = gather_weights(pack_layers([args[n] for n in SHARDED], BF16))
    parts = [unpack_layers(wfull[k], shard_shapes) for k in range(4)]
    cat = lambda i, axis: jnp.concatenate([parts[k][i] for k in range(4)], axis=axis)
    win_f, wglu_f, wba_f, wbb_f = cat(0, 2), cat(1, 1), cat(2, 2), cat(3, 2)
    wout_f, wup_f, wdown_f = cat(4, 1), cat(5, 2), cat(6, 1)
    o_f, o_u, o_ga, o_gb = 3 * aw, 3 * aw + ATTN_HEADS, 3 * aw + ATTN_HEADS + sw, 3 * aw + ATTN_HEADS + sw + d
    w_qkv = win_f[:, :, :o_f]
    w_rest = jnp.concatenate([win_f[:, :, o_ga:o_gb], win_f[:, :, o_gb:], win_f[:, :, o_u:o_ga],
                              jnp.pad(win_f[:, :, o_f:o_u], ((0, 0), (0, 0), (0, F_PAD - ATTN_HEADS)))], axis=2)
    w_pad = jnp.concatenate([w_qkv, w_rest], axis=2)
    u_blk = 2 * d // sw
    f_blk = (2 * d + sw) // F_PAD
    bf_pad = jnp.pad(b_forget, ((0, 0), (0, F_PAD - ATTN_HEADS)))

    disc = [jax.vjp(_discretise, ssm_lambda_re[l], ssm_lambda_im[l], ssm_log_dt[l], ssm_b_re[l], ssm_b_im[l])
            for l in range(nl)]

    xs = x.reshape(tokens, d)
    saved = []
    for l in range(nl):
        g1 = norm_mix[l].reshape(1, d)
        qkv = norm_matmul(xs, g1, w_qkv[l], BF16, "proj_qkv")
        rest = norm_matmul(xs, g1, w_rest[l], F32, "proj_rest")
        cumcol, cumrow = forget_cumsum(rest, bf_pad[l:l + 1], bsz, seq, f_blk)
        ya, lsec, lser = fox_fwd(qkv, cumcol, cumrow, bsz, seq)
        tabs, wb4, wc4 = _ssm_inputs(*disc[l][0], ssm_c_re[l], ssm_c_im[l])
        dskip = ssm_d[l].reshape(1, sw)
        ys, hs = ssm_fwd(rest, wb4, wc4, tabs, dskip, bsz, seq, u_blk)
        x1, z, pa, pb, yb, yb2, mixed = mix_fwd(ya, ys, rest, xs, wglu_f[l], b_glu[l].reshape(1, sw),
                                                 wba_f[l], wbb_f[l], wout_f[l])
        x2, up = mlp_fwd(x1, norm_mlp[l].reshape(1, d), wup_f[l], wdown_f[l])
        saved.append(dict(x0=xs, qkv=qkv, rest=rest, cumcol=cumcol, cumrow=cumrow, ya=ya, lsec=lsec, lser=lser,
                          tabs=tabs, wb4=wb4, wc4=wc4, dskip=dskip, ys=ys, hs=hs, x1=x1, z=z, pa=pa, pb=pb,
                          yb=yb, yb2=yb2, mixed=mixed, up=up))
        xs = x2
    dx, loss_rows, dgf_rows = final_loss(xs, norm_final.reshape(1, d), loss_target.reshape(tokens, d))
    loss = lax.psum(jnp.sum(loss_rows), ("x", "y", "c"))

    big = {n: [None] * nl for n in SHARDED}
    small = {n: [None] * nl for n in SMALL if n != "norm_final"}
    for l in reversed(range(nl)):
        s = saved[l]
        g2 = norm_mlp[l].reshape(1, d)
        dup, dx1, dg2 = mlp_bwd(dx, s["up"], s["x1"], g2, wup_f[l], wdown_f[l])
        big["w_mlp_down"][l] = matmul_tn(s["up"], dx, "grad_w_mlp_down", a_kind="relu2")
        big["w_mlp_up"][l] = matmul_tn(s["x1"], dup, "grad_w_mlp_up", a_kind="norm", gain=g2)
        small["norm_mlp"][l] = jnp.sum(dg2, axis=0)
        dya, dys, dgab, dpa, dpb, dz, dbg = mix_bwd(dx1, s["rest"], s["pa"], s["pb"], s["z"], s["ys"],
                                                    wglu_f[l], wba_f[l], wbb_f[l], wout_f[l])
        big["w_out"][l] = matmul_tn(s["mixed"], dx1, "grad_w_out")
        big["w_branch_a"][l] = matmul_tn(s["ya"], dpa, "grad_w_branch_a")
        big["w_branch_b"][l] = matmul_tn(s["yb2"], dpb, "grad_w_branch_b")
        big["w_glu"][l] = matmul_tn(s["yb"], dz, "grad_w_glu")
        small["b_glu"][l] = jnp.sum(dbg, axis=0)
        du, ga8, gwb, gwc, gd8 = ssm_bwd(dys, s["rest"], s["hs"], s["wb4"], s["wc4"], s["tabs"], s["dskip"],
                                         bsz, seq, u_blk)
        gar, gai, gbbr, gbbi, gcr, gci = _ssm_param_grads(ga8, gwb, gwc)
        glr, gli, gdt, gbr, gbi = disc[l][1]((gar, gai, gbbr, gbbi))
        small["ssm_lambda_re"][l], small["ssm_lambda_im"][l], small["ssm_log_dt"][l] = glr, gli, gdt
        small["ssm_b_re"][l], small["ssm_b_im"][l] = gbr, gbi
        small["ssm_c_re"][l], small["ssm_c_im"][l] = gcr, gci
        small["ssm_d"][l] = jnp.sum(gd8, axis=0)
        dq, dk, dv, df, dbf = fox_bwd(s["qkv"], dya, s["ya"], s["lsec"], s["lser"], s["cumcol"], s["cumrow"],
                                      s["rest"], bf_pad[l:l + 1], bsz, seq, f_blk)
        small["b_forget"][l] = jnp.sum(dbf, axis=0)[:ATTN_HEADS]
        dproj = jnp.concatenate([dq, dk, dv, dgab, du, df], axis=1)
        g1 = norm_mix[l].reshape(1, d)
        dwp = matmul_tn(s["x0"], dproj, "grad_w_in", a_kind="norm", gain=g1)
        big["w_in"][l] = jnp.concatenate(
            [dwp[:, :o_f], dwp[:, o_f + 2 * d + sw:o_f + 2 * d + sw + ATTN_HEADS],
             dwp[:, o_f + 2 * d:o_f + 2 * d + sw], dwp[:, o_f:o_f + 2 * d]], axis=1)
        dx, dg1 = proj_bwd(dproj, w_pad[l], s["x0"], dx1, g1)
        small["norm_mix"][l] = jnp.sum(dg1, axis=0)
    grad_x = dx.reshape(bsz, seq, d)

    def shards(name, g):
        axis = {"w_in": 2, "w_branch_a": 2, "w_branch_b": 2, "w_mlp_up": 2}.get(name, 1)
        return jnp.split(g, 4, axis=axis)

    stacked = {n: shards(n, jnp.stack(big[n])) for n in SHARDED}
    gp = jnp.stack([pack_layers([stacked[n][k] for n in SHARDED], F32) for k in range(4)])
    pair = add_sibling(gp, exchange_sibling_half(gp), core)
    reduced = add_chips(scatter_to_owners(pair))
    gsh = share_with_sibling(reduced)
    rows = gsh.shape[1]
    flat2 = lambda a: a.reshape(nl * rows, PACK_COLS)
    wpk = pack_layers([args[n] for n in SHARDED], F32)
    mpk = pack_layers([args["m_" + n] for n in SHARDED], F32)
    vpk = pack_layers([args["v_" + n] for n in SHARDED], F32)
    dpk, nmpk, nvpk = adamw(flat2(wpk), flat2(gsh), flat2(mpk), flat2(vpk), "adamw_shards")
    unp = lambda a: dict(zip(SHARDED, unpack_layers(a.reshape(nl, rows, PACK_COLS), shard_shapes)))
    out_g, out_d, out_m, out_v = unp(gsh), unp(dpk), unp(nmpk), unp(nvpk)

    small_g = [jnp.stack(small[n]) if n != "norm_final" else jnp.sum(dgf_rows, axis=0) for n in SMALL]
    small_shapes = [args[n].shape for n in SMALL]
    sg, sd, sm, sv = sum_and_adamw(exchange_all(pack_flat(small_g)),
                                   pack_flat([args[n] for n in SMALL]),
                                   pack_flat([args["m_" + n] for n in SMALL]),
                                   pack_flat([args["v_" + n] for n in SMALL]))
    for res, packed in ((out_g, sg), (out_d, sd), (out_m, sm), (out_v, sv)):
        res.update(zip(SMALL, unpack_flat(packed, small_shapes)))

    order = ("norm_mix", "w_in", "b_forget", "ssm_lambda_re", "ssm_lambda_im", "ssm_log_dt", "ssm_b_re",
             "ssm_b_im", "ssm_c_re", "ssm_c_im", "ssm_d", "w_glu", "b_glu", "w_branch_a", "w_branch_b", "w_out",
             "norm_mlp", "w_mlp_up", "w_mlp_down", "norm_final")
    return (loss, grad_x, *[out_g[n] for n in order], *[out_d[n] for n in order],
            *[out_m[n] for n in order], *[out_v[n] for n in order])
```

```python
import math

import jax
import jax.numpy as jnp
from jax import lax
from jax.experimental import pallas as pl
from jax.experimental.pallas import tpu as pltpu

F32 = jnp.float32
BF16 = jnp.bfloat16
MESH = pl.DeviceIdType.MESH
ANY = pl.BlockSpec(memory_space=pl.ANY)

ATTN_HEADS = 8
HEAD_DIM = 64
ATTN_WIDTH = ATTN_HEADS * HEAD_DIM
HEAD_PAIRS = ATTN_HEADS // 2
SSM_GROUPS = 32
SSM_GROUP_CH = 16
SSM_STATE = 64
SSM_WIDTH = SSM_GROUPS * SSM_GROUP_CH
LANES = 128
SUBLANES = 8
SSM_CHUNKS = SSM_WIDTH // LANES
CHUNK_GROUPS = SSM_GROUPS // SSM_CHUNKS
CHUNK_STATES = CHUNK_GROUPS * SSM_STATE
CHUNK_LANES = 2 * CHUNK_STATES
STATE_LANES = SSM_CHUNKS * CHUNK_LANES
F_PAD = LANES
RMS_EPS = 1e-6
ADAM_LR = 0.001
ADAM_B1 = 0.9
ADAM_B2 = 0.999
ADAM_EPS = 1e-08
ADAM_WD = 0.01
ADAM_STEP = 10
PACK_COLS = 1024
PACK_ROW_TILE = 256
VMEM_LIMIT = 52 * 1024 * 1024
NEG_BIG = -1e30
GELU_C = math.sqrt(2.0 / math.pi)
GELU_A = 0.044715

NN = (((1,), (0,)), ((), ()))
NT = (((1,), (1,)), ((), ()))
TN = (((0,), (0,)), ((), ()))


def _pick(n, pref):
    if n <= pref:
        return n
    best = LANES
    for t in range(LANES, pref + 1, LANES):
        if n % t == 0:
            best = t
    assert n % best == 0, (n, pref)
    return best


def _rows(n, pref):
    t = min(n, pref)
    while n % t:
        t //= 2
    assert t % 16 == 0 or t == n, (n, pref)
    return t


def _params(sem):
    return pltpu.CompilerParams(dimension_semantics=sem, vmem_limit_bytes=VMEM_LIMIT)


def _fold8(v):
    r, c = v.shape
    return jnp.sum(v.reshape(r // SUBLANES, SUBLANES, c), axis=0)


def _dot(a, b, dims=None):
    if dims is None:
        return jnp.dot(a, b, preferred_element_type=F32)
    return lax.dot_general(a, b, dims, preferred_element_type=F32)


def _dot_exact(a, b, dims):
    return lax.dot_general(a, b, dims, preferred_element_type=F32, precision=lax.Precision.HIGHEST)


def _sigmoid(v):
    return 1.0 / (1.0 + jnp.exp(-v))


def _rms_scale(x):
    return lax.rsqrt(jnp.mean(x * x, axis=-1, keepdims=True) + RMS_EPS)


def _rms_bwd(x, g, dh):
    r = _rms_scale(x)
    xn = x * r
    dxn = dh * g
    dx = r * (dxn - xn * jnp.mean(dxn * xn, axis=-1, keepdims=True))
    return dx, dh * xn


def norm_matmul(x, g, w, out_dtype, name):
    t, d = x.shape
    m = w.shape[1]
    tm, tn = _rows(t, 512), _pick(m, 768)

    def body(x_ref, g_ref, w_ref, o_ref):
        xv = x_ref[...]
        h = (xv * _rms_scale(xv) * g_ref[...]).astype(BF16)
        o_ref[...] = _dot(h, w_ref[...]).astype(o_ref.dtype)

    return pl.pallas_call(
        body, name=name, grid=(t // tm, m // tn),
        in_specs=[pl.BlockSpec((tm, d), lambda i, j: (i, 0)),
                  pl.BlockSpec((1, d), lambda i, j: (0, 0)),
                  pl.BlockSpec((d, tn), lambda i, j: (0, j))],
        out_specs=pl.BlockSpec((tm, tn), lambda i, j: (i, j)),
        out_shape=jax.ShapeDtypeStruct((t, m), out_dtype),
        compiler_params=_params(("parallel", "arbitrary")),
    )(x, g, w)


def matmul_tn(a, b, name, a_kind="plain", gain=None, tm_pref=1024, tn_pref=1536, tk_pref=512):
    t, ma = a.shape
    nb = b.shape[1]
    tm = ma if a_kind == "norm" else _pick(ma, tm_pref)
    tn = _pick(nb, tn_pref)
    tk = _rows(t, tk_pref)
    nk = t // tk

    def body(*refs):
        if a_kind == "norm":
            a_ref, g_ref, b_ref, o_ref, acc = refs
        else:
            a_ref, b_ref, o_ref, acc = refs
        k = pl.program_id(2)

        @pl.when(k == 0)
        def _():
            acc[...] = jnp.zeros_like(acc)

        av = a_ref[...]
        if a_kind == "norm":
            av = av * _rms_scale(av) * g_ref[...]
        elif a_kind == "relu2":
            av = jnp.square(jnp.maximum(av.astype(F32), 0.0))
        acc[...] += _dot(av.astype(BF16), b_ref[...].astype(BF16), TN)

        @pl.when(k == nk - 1)
        def _():
            o_ref[...] = acc[...]

    in_specs = [pl.BlockSpec((tk, tm), lambda i, j, k: (k, i))]
    args = [a]
    if a_kind == "norm":
        in_specs.append(pl.BlockSpec((1, ma), lambda i, j, k: (0, 0)))
        args.append(gain)
    in_specs.append(pl.BlockSpec((tk, tn), lambda i, j, k: (k, j)))
    args.append(b)
    return pl.pallas_call(
        body, name=name, grid=(ma // tm, nb // tn, nk),
        in_specs=in_specs,
        out_specs=pl.BlockSpec((tm, tn), lambda i, j, k: (i, j)),
        out_shape=jax.ShapeDtypeStruct((ma, nb), F32),
        scratch_shapes=[pltpu.VMEM((tm, tn), F32)],
        compiler_params=_params(("parallel", "parallel", "arbitrary")),
    )(*args)


def _tri(n, upper):
    r = lax.broadcasted_iota(jnp.int32, (n, n), 0)
    c = lax.broadcasted_iota(jnp.int32, (n, n), 1)
    return jnp.where((c >= r) if upper else (c <= r), 1.0, 0.0).astype(F32)


def _head_rows():
    r = lax.broadcasted_iota(jnp.int32, (SUBLANES, LANES), 0)
    c = lax.broadcasted_iota(jnp.int32, (SUBLANES, LANES), 1)
    return jnp.where(r == c, 1.0, 0.0).astype(F32)


def forget_cumsum(rest, bf, bsz, seq, f_blk):
    tc = _rows(seq, 512)
    nc = seq // tc

    def body(f_ref, b_ref, col_ref, row_ref, carry):
        c = pl.program_id(1)

        @pl.when(c == 0)
        def _():
            carry[...] = jnp.zeros_like(carry)

        z = f_ref[...] + b_ref[...]
        logf = jnp.minimum(z, 0.0) - jnp.log(1.0 + jnp.exp(-jnp.abs(z)))
        cum = _dot_exact(_tri(tc, False), logf, NN) + carry[0:1, :]
        col_ref[0] = cum
        row_ref[0] = _dot_exact(_head_rows(), cum, NT)
        carry[...] = jnp.broadcast_to(cum[tc - 1:tc, :], carry.shape)

    return pl.pallas_call(
        body, name="forget_cumsum", grid=(bsz, nc),
        in_specs=[pl.BlockSpec((tc, F_PAD), lambda b, c: (b * nc + c, f_blk)),
                  pl.BlockSpec((1, F_PAD), lambda b, c: (0, 0))],
        out_specs=[pl.BlockSpec((1, tc, LANES), lambda b, c: (b, c, 0)),
                   pl.BlockSpec((1, SUBLANES, tc), lambda b, c: (b, 0, c))],
        out_shape=[jax.ShapeDtypeStruct((bsz, seq, LANES), F32),
                   jax.ShapeDtypeStruct((bsz, SUBLANES, seq), F32)],
        scratch_shapes=[pltpu.VMEM((SUBLANES, LANES), F32)],
        compiler_params=_params(("parallel", "arbitrary")),
    )(rest, bf)


def forget_bwd(dcq, dcp, rest, bf, bsz, seq, f_blk):
    tc = _rows(seq, 512)
    nc = seq // tc

    def body(dq_ref, dc_ref, f_ref, b_ref, df_ref, db_ref, carry):
        b = pl.program_id(0)
        c = pl.program_id(1)

        @pl.when(c == 0)
        def _():
            carry[...] = jnp.zeros_like(carry)

        @pl.when((b == 0) & (c == 0))
        def _():
            db_ref[...] = jnp.zeros_like(db_ref)

        dc = dc_ref[0, 0] + dq_ref[0, 0]
        for p in range(1, HEAD_PAIRS):
            dc = dc + (dc_ref[0, p] + dq_ref[0, p])
        dlogf = _dot_exact(_tri(tc, True), dc, NN) + carry[0:1, :]
        carry[...] = jnp.broadcast_to(dlogf[0:1, :], carry.shape)
        z = f_ref[...] + b_ref[...]
        lane = lax.broadcasted_iota(jnp.int32, z.shape, 1)
        df = jnp.where(lane < ATTN_HEADS, dlogf * _sigmoid(-z), 0.0)
        df_ref[...] = df.astype(df_ref.dtype)
        db_ref[...] += _fold8(df)

    return pl.pallas_call(
        body, name="forget_bwd", grid=(bsz, nc),
        in_specs=[pl.BlockSpec((1, HEAD_PAIRS, tc, LANES), lambda b, c: (b, 0, nc - 1 - c, 0)),
                  pl.BlockSpec((1, HEAD_PAIRS, tc, LANES), lambda b, c: (b, 0, nc - 1 - c, 0)),
                  pl.BlockSpec((tc, F_PAD), lambda b, c: (b * nc + nc - 1 - c, f_blk)),
                  pl.BlockSpec((1, F_PAD), lambda b, c: (0, 0))],
        out_specs=[pl.BlockSpec((tc, F_PAD), lambda b, c: (b * nc + nc - 1 - c, 0)),
                   pl.BlockSpec((SUBLANES, F_PAD), lambda b, c: (0, 0))],
        out_shape=[jax.ShapeDtypeStruct((bsz * seq, F_PAD), BF16),
                   jax.ShapeDtypeStruct((SUBLANES, F_PAD), F32)],
        scratch_shapes=[pltpu.VMEM((SUBLANES, LANES), F32)],
        compiler_params=_params(("arbitrary", "arbitrary")),
    )(dcq, dcp, rest, bf)


def _attn_tile(seq):
    return 512 if seq >= 2048 else 128


def _lane_head(shape, par):
    lane = lax.broadcasted_iota(jnp.int32, shape, len(shape) - 1)
    return (lane >= HEAD_DIM) if par else (lane < HEAD_DIM)


def _pick_lane(block, idx):
    lane = lax.broadcasted_iota(jnp.int32, block.shape, 1)
    return jnp.sum(jnp.where(lane == idx, block, 0.0), axis=1, keepdims=True)


def _pair_rows(lo_lane, hi_lane):
    r = lax.broadcasted_iota(jnp.int32, (SUBLANES, LANES), 0)
    c = lax.broadcasted_iota(jnp.int32, (SUBLANES, LANES), 1)
    if lo_lane is None:
        sel = ((r == 0) & (c < HEAD_DIM)) | ((r == 1) & (c >= HEAD_DIM))
    else:
        sel = ((r == 0) & (c == lo_lane)) | ((r == 1) & (c == hi_lane))
    return jnp.where(sel, 1.0, 0.0).astype(F32)


def _causal(s, transposed):
    r = lax.broadcasted_iota(jnp.int32, s.shape, 0)
    c = lax.broadcasted_iota(jnp.int32, s.shape, 1)
    return jnp.where((c >= r) if transposed else (r >= c), s, NEG_BIG)


def fox_fwd(qkv, cumrow, bsz, seq):
    tq = _attn_tile(seq)
    nq = seq // tq
    scale = HEAD_DIM ** -0.5
    kb, vb = ATTN_WIDTH // LANES, 2 * ATTN_WIDTH // LANES

    def body(q_ref, k_ref, v_ref, cr_ref, o_ref, lc_ref, lr_ref, m_s, l_s, acc_s):
        hp = pl.program_id(1)
        qi = pl.program_id(2)
        kj = pl.program_id(3)

        @pl.when(kj == 0)
        def _():
            m_s[...] = jnp.full_like(m_s, NEG_BIG)
            l_s[...] = jnp.zeros_like(l_s)
            acc_s[...] = jnp.zeros_like(acc_s)

        def step(diag):
            q = q_ref[...]
            k = k_ref[...]
            v = v_ref[...]
            ones = jnp.ones((tq, LANES), BF16)
            for par in range(2):
                qh = jnp.where(_lane_head(q.shape, par), q, 0.0) * scale
                s = _dot(qh.astype(BF16), k, NT) - cr_ref[0, pl.ds(2 * hp + par, 1), :]
                if diag:
                    s = _causal(s, False)
                m_prev = m_s[par]
                m_new = jnp.maximum(m_prev, jnp.max(s, axis=1, keepdims=True))
                alpha = jnp.exp(m_prev - m_new)
                p = jnp.exp(s - m_new).astype(BF16)
                l_s[par] = alpha * l_s[par] + _dot(p, ones)
                acc_s[par] = alpha * acc_s[par] + _dot(p, v)
                m_s[par] = m_new

        @pl.when(kj < qi)
        def _():
            step(False)

        @pl.when(kj == qi)
        def _():
            step(True)
            lo = _lane_head((tq, LANES), 0)
            out = jnp.where(lo, acc_s[0] / l_s[0], acc_s[1] / l_s[1])
            o_ref[...] = out.astype(o_ref.dtype)
            lse = jnp.where(lo, m_s[0] + jnp.log(l_s[0]), m_s[1] + jnp.log(l_s[1]))
            lc_ref[0, 0] = lse
            lr_ref[0, 0] = _dot_exact(_pair_rows(0, HEAD_DIM), lse, NT)

    return pl.pallas_call(
        body, name="fox_fwd", grid=(bsz, HEAD_PAIRS, nq, nq),
        in_specs=[pl.BlockSpec((tq, LANES), lambda b, h, i, j: (b * nq + i, h)),
                  pl.BlockSpec((tq, LANES), lambda b, h, i, j: (b * nq + jnp.minimum(i, j), kb + h)),
                  pl.BlockSpec((tq, LANES), lambda b, h, i, j: (b * nq + jnp.minimum(i, j), vb + h)),
                  pl.BlockSpec((1, SUBLANES, tq), lambda b, h, i, j: (b, 0, jnp.minimum(i, j)))],
        out_specs=[pl.BlockSpec((tq, LANES), lambda b, h, i, j: (b * nq + i, h)),
                   pl.BlockSpec((1, 1, tq, LANES), lambda b, h, i, j: (b, h, i, 0)),
                   pl.BlockSpec((1, 1, SUBLANES, tq), lambda b, h, i, j: (b, h, 0, i))],
        out_shape=[jax.ShapeDtypeStruct((bsz * seq, ATTN_WIDTH), BF16),
                   jax.ShapeDtypeStruct((bsz, HEAD_PAIRS, seq, LANES), F32),
                   jax.ShapeDtypeStruct((bsz, HEAD_PAIRS, SUBLANES, seq), F32)],
        scratch_shapes=[pltpu.VMEM((2, tq, 1), F32), pltpu.VMEM((2, tq, LANES), F32),
                        pltpu.VMEM((2, tq, LANES), F32)],
        compiler_params=_params(("parallel", "parallel", "parallel", "arbitrary")),
    )(qkv, qkv, qkv, cumrow)


def fox_bwd_q(qkv, do, o, lsec, cumrow, bsz, seq):
    tq = _attn_tile(seq)
    nq = seq // tq
    scale = HEAD_DIM ** -0.5
    kb, vb = ATTN_WIDTH // LANES, 2 * ATTN_WIDTH // LANES

    def body(q_ref, k_ref, v_ref, do_ref, o_ref, lc_ref, cr_ref, dq_ref, dc_ref, d_s, acc_s, dc_s):
        hp = pl.program_id(1)
        qi = pl.program_id(2)
        kj = pl.program_id(3)

        @pl.when(kj == 0)
        def _():
            prod = do_ref[...].astype(F32) * o_ref[...].astype(F32)
            for par in range(2):
                d_s[par] = jnp.sum(jnp.where(_lane_head(prod.shape, par), prod, 0.0), axis=1, keepdims=True)
            acc_s[...] = jnp.zeros_like(acc_s)
            dc_s[...] = jnp.zeros_like(dc_s)

        def step(diag):
            q = q_ref[...]
            k = k_ref[...]
            v = v_ref[...]
            dov = do_ref[...]
            lse = lc_ref[0, 0]
            lane = lax.broadcasted_iota(jnp.int32, (tq, LANES), 1)
            for par in range(2):
                head = 2 * hp + par
                sel = _lane_head(q.shape, par)
                qh = jnp.where(sel, q, 0.0) * scale
                s = _dot(qh.astype(BF16), k, NT) - cr_ref[0, pl.ds(head, 1), :]
                if diag:
                    s = _causal(s, False)
                p = jnp.exp(s - lse[:, par * HEAD_DIM:par * HEAD_DIM + 1])
                doh = jnp.where(sel, dov, 0.0)
                dp = _dot(doh.astype(BF16), v, NT)
                ds = p * (dp - d_s[par])
                acc_s[...] += jnp.where(sel, _dot(ds.astype(BF16), k), 0.0)
                dc_s[...] += jnp.where(lane == head, jnp.sum(ds, axis=1, keepdims=True), 0.0)

        @pl.when(kj < qi)
        def _():
            step(False)

        @pl.when(kj == qi)
        def _():
            step(True)
            dq_ref[...] = (acc_s[...] * scale).astype(dq_ref.dtype)
            dc_ref[0, 0] = dc_s[...]

    return pl.pallas_call(
        body, name="fox_bwd_q", grid=(bsz, HEAD_PAIRS, nq, nq),
        in_specs=[pl.BlockSpec((tq, LANES), lambda b, h, i, j: (b * nq + i, h)),
                  pl.BlockSpec((tq, LANES), lambda b, h, i, j: (b * nq + jnp.minimum(i, j), kb + h)),
                  pl.BlockSpec((tq, LANES), lambda b, h, i, j: (b * nq + jnp.minimum(i, j), vb + h)),
                  pl.BlockSpec((tq, LANES), lambda b, h, i, j: (b * nq + i, h)),
                  pl.BlockSpec((tq, LANES), lambda b, h, i, j: (b * nq + i, h)),
                  pl.BlockSpec((1, 1, tq, LANES), lambda b, h, i, j: (b, h, i, 0)),
                  pl.BlockSpec((1, SUBLANES, tq), lambda b, h, i, j: (b, 0, jnp.minimum(i, j)))],
        out_specs=[pl.BlockSpec((tq, LANES), lambda b, h, i, j: (b * nq + i, h)),
                   pl.BlockSpec((1, 1, tq, LANES), lambda b, h, i, j: (b, h, i, 0))],
        out_shape=[jax.ShapeDtypeStruct((bsz * seq, ATTN_WIDTH), BF16),
                   jax.ShapeDtypeStruct((bsz, HEAD_PAIRS, seq, LANES), F32)],
        scratch_shapes=[pltpu.VMEM((2, tq, 1), F32), pltpu.VMEM((tq, LANES), F32), pltpu.VMEM((tq, LANES), F32)],
        compiler_params=_params(("parallel", "parallel", "parallel", "arbitrary")),
    )(qkv, qkv, qkv, do, o, lsec, cumrow)


def fox_bwd_kv(qkv, do, o, lser, cumcol, bsz, seq):
    tk = _attn_tile(seq)
    nk = seq // tk
    scale = HEAD_DIM ** -0.5
    kb, vb = ATTN_WIDTH // LANES, 2 * ATTN_WIDTH // LANES

    def body(q_ref, k_ref, v_ref, do_ref, o_ref, lr_ref, cc_ref, dk_ref, dv_ref, dc_ref, dk_s, dv_s, dc_s):
        hp = pl.program_id(1)
        kj = pl.program_id(2)
        qi = pl.program_id(3)

        @pl.when(qi == 0)
        def _():
            dk_s[...] = jnp.zeros_like(dk_s)
            dv_s[...] = jnp.zeros_like(dv_s)
            dc_s[...] = jnp.zeros_like(dc_s)

        def step(diag):
            q = q_ref[...]
            k = k_ref[...]
            v = v_ref[...]
            dov = do_ref[...]
            prod = dov.astype(F32) * o_ref[...].astype(F32)
            drow = _dot_exact(_pair_rows(None, None), prod, NT)
            lrow = lr_ref[0, 0]
            lane = lax.broadcasted_iota(jnp.int32, (tk, LANES), 1)
            for par in range(2):
                head = 2 * hp + par
                sel = _lane_head(k.shape, par)
                kh = jnp.where(sel, k, 0.0) * scale
                st = _dot(kh.astype(BF16), q, NT) - _pick_lane(cc_ref[0], head)
                if diag:
                    st = _causal(st, True)
                pt = jnp.exp(st - lrow[par:par + 1, :])
                vh = jnp.where(sel, v, 0.0)
                dpt = _dot(vh.astype(BF16), dov, NT)
                dst = pt * (dpt - drow[par:par + 1, :])
                dv_s[...] += jnp.where(sel, _dot(pt.astype(BF16), dov), 0.0)
                dk_s[...] += jnp.where(sel, _dot(dst.astype(BF16), q), 0.0)
                dc_s[...] += jnp.where(lane == head, -jnp.sum(dst, axis=1, keepdims=True), 0.0)

        @pl.when(qi > kj)
        def _():
            step(False)

        @pl.when(qi == kj)
        def _():
            step(True)

        @pl.when(qi == nk - 1)
        def _():
            dk_ref[...] = (dk_s[...] * scale).astype(dk_ref.dtype)
            dv_ref[...] = dv_s[...].astype(dv_ref.dtype)
            dc_ref[0, 0] = dc_s[...]

    return pl.pallas_call(
        body, name="fox_bwd_kv", grid=(bsz, HEAD_PAIRS, nk, nk),
        in_specs=[pl.BlockSpec((tk, LANES), lambda b, h, j, i: (b * nk + jnp.maximum(i, j), h)),
                  pl.BlockSpec((tk, LANES), lambda b, h, j, i: (b * nk + j, kb + h)),
                  pl.BlockSpec((tk, LANES), lambda b, h, j, i: (b * nk + j, vb + h)),
                  pl.BlockSpec((tk, LANES), lambda b, h, j, i: (b * nk + jnp.maximum(i, j), h)),
                  pl.BlockSpec((tk, LANES), lambda b, h, j, i: (b * nk + jnp.maximum(i, j), h)),
                  pl.BlockSpec((1, 1, SUBLANES, tk), lambda b, h, j, i: (b, h, 0, jnp.maximum(i, j))),
                  pl.BlockSpec((1, tk, LANES), lambda b, h, j, i: (b, j, 0))],
        out_specs=[pl.BlockSpec((tk, LANES), lambda b, h, j, i: (b * nk + j, h)),
                   pl.BlockSpec((tk, LANES), lambda b, h, j, i: (b * nk + j, h)),
                   pl.BlockSpec((1, 1, tk, LANES), lambda b, h, j, i: (b, h, j, 0))],
        out_shape=[jax.ShapeDtypeStruct((bsz * seq, ATTN_WIDTH), BF16),
                   jax.ShapeDtypeStruct((bsz * seq, ATTN_WIDTH), BF16),
                   jax.ShapeDtypeStruct((bsz, HEAD_PAIRS, seq, LANES), F32)],
        scratch_shapes=[pltpu.VMEM((tk, LANES), F32), pltpu.VMEM((tk, LANES), F32),
                        pltpu.VMEM((tk, LANES), F32)],
        compiler_params=_params(("parallel", "parallel", "parallel", "arbitrary")),
    )(qkv, qkv, qkv, do, o, lser, cumcol)


def fox_bwd(qkv, do, o, lsec, lser, cumcol, cumrow, rest, bf, bsz, seq, f_blk):
    dq, dcq = fox_bwd_q(qkv, do, o, lsec, cumrow, bsz, seq)
    dk, dv, dcp = fox_bwd_kv(qkv, do, o, lser, cumcol, bsz, seq)
    df, dbf = forget_bwd(dcq, dcp, rest, bf, bsz, seq, f_blk)
    return dq, dk, dv, df, dbf


TAB_STEP = 0
TAB_FWD = 8
TAB_BWD = 16
TAB_ROWS = 24


def _ssm_tile(seq):
    return 256 if seq >= 1024 else 64


def _scan_block(xr, xi, tab_ref, re, im, cr, ci, reverse):
    row = lax.broadcasted_iota(jnp.int32, xr.shape, 0)
    sign = -1.0 if reverse else 1.0
    for n, s in enumerate((1, 2, 4)):
        ar = tab_ref[TAB_STEP + n:TAB_STEP + n + 1, re]
        ai = tab_ref[TAB_STEP + n:TAB_STEP + n + 1, im] * sign
        if reverse:
            keep = row < SUBLANES - s
            sr = jnp.where(keep, pltpu.roll(xr, SUBLANES - s, 0), 0.0)
            si = jnp.where(keep, pltpu.roll(xi, SUBLANES - s, 0), 0.0)
        else:
            keep = row >= s
            sr = jnp.where(keep, pltpu.roll(xr, s, 0), 0.0)
            si = jnp.where(keep, pltpu.roll(xi, s, 0), 0.0)
        xr, xi = xr + ar * sr - ai * si, xi + ar * si + ai * sr
    base = TAB_BWD if reverse else TAB_FWD
    pr = tab_ref[base:base + SUBLANES, re]
    pi = tab_ref[base:base + SUBLANES, im] * sign
    xr, xi = xr + pr * cr - pi * ci, xi + pr * ci + pi * cr
    return xr, xi


def ssm_fwd(rest, wb4, wc4, tabs, dskip, bsz, seq, u_blk):
    tt = _ssm_tile(seq)
    nt = seq // tt

    def body(u_ref, wb_ref, wc_ref, tab_ref, d_ref, y_ref, h_ref, carry):
        c = pl.program_id(1)

        @pl.when(c == 0)
        def _():
            carry[...] = jnp.zeros_like(carry)

        u = u_ref[...]
        ub = u.astype(BF16)
        for j in range(SSM_CHUNKS):
            h_ref[:, j * CHUNK_LANES:(j + 1) * CHUNK_LANES] = _dot(ub[:, j * LANES:(j + 1) * LANES], wb_ref[j])
        for j in range(SSM_CHUNKS):
            re = slice(j * CHUNK_LANES, j * CHUNK_LANES + CHUNK_STATES)
            im = slice(j * CHUNK_LANES + CHUNK_STATES, (j + 1) * CHUNK_LANES)

            def blk(bi, car):
                r0 = pl.multiple_of(bi * SUBLANES, SUBLANES)
                xr, xi = _scan_block(h_ref[pl.ds(r0, SUBLANES), re], h_ref[pl.ds(r0, SUBLANES), im],
                                     tab_ref, re, im, car[0], car[1], False)
                h_ref[pl.ds(r0, SUBLANES), re] = xr
                h_ref[pl.ds(r0, SUBLANES), im] = xi
                return xr[SUBLANES - 1:SUBLANES], xi[SUBLANES - 1:SUBLANES]

            cr, ci = lax.fori_loop(0, tt // SUBLANES, blk, (carry[0:1, re], carry[0:1, im]))
            carry[0:1, re] = cr
            carry[0:1, im] = ci
        for j in range(SSM_CHUNKS):
            hj = h_ref[:, j * CHUNK_LANES:(j + 1) * CHUNK_LANES].astype(BF16)
            cols = slice(j * LANES, (j + 1) * LANES)
            y_ref[:, cols] = _dot(hj, wc_ref[j]) + d_ref[:, cols] * u[:, cols]

    return pl.pallas_call(
        body, name="ssm_fwd", grid=(bsz, nt),
        in_specs=[pl.BlockSpec((tt, SSM_WIDTH), lambda b, c: (b * nt + c, u_blk)),
                  pl.BlockSpec((SSM_CHUNKS, LANES, CHUNK_LANES), lambda b, c: (0, 0, 0)),
                  pl.BlockSpec((SSM_CHUNKS, CHUNK_LANES, LANES), lambda b, c: (0, 0, 0)),
                  pl.BlockSpec((TAB_ROWS, STATE_LANES), lambda b, c: (0, 0)),
                  pl.BlockSpec((1, SSM_WIDTH), lambda b, c: (0, 0))],
        out_specs=[pl.BlockSpec((tt, SSM_WIDTH), lambda b, c: (b * nt + c, 0)),
                   pl.BlockSpec((tt, STATE_LANES), lambda b, c: (b * nt + c, 0))],
        out_shape=[jax.ShapeDtypeStruct((bsz * seq, SSM_WIDTH), F32),
                   jax.ShapeDtypeStruct((bsz * seq, STATE_LANES), F32)],
        scratch_shapes=[pltpu.VMEM((SUBLANES, STATE_LANES), F32)],
        compiler_params=_params(("parallel", "arbitrary")),
    )(rest, wb4, wc4, tabs, dskip)


def ssm_bwd(dys, rest, hs, wb4, wc4, tabs, dskip, bsz, seq, u_blk):
    tt = _ssm_tile(seq)
    nt = seq // tt
    nb = tt // SUBLANES

    def body(dy_ref, u_ref, h_ref, hp_ref, wb_ref, wc_ref, tab_ref, d_ref,
             du_ref, ga_ref, gwb_ref, gwc_ref, gd_ref, g_s, carry):
        b = pl.program_id(0)
        c = pl.program_id(1)

        @pl.when(c == 0)
        def _():
            carry[...] = jnp.zeros_like(carry)

        @pl.when((b == 0) & (c == 0))
        def _():
            ga_ref[...] = jnp.zeros_like(ga_ref)
            gwb_ref[...] = jnp.zeros_like(gwb_ref)
            gwc_ref[...] = jnp.zeros_like(gwc_ref)
            gd_ref[...] = jnp.zeros_like(gd_ref)

        dy = dy_ref[...].astype(F32)
        dyb = dy.astype(BF16)
        u = u_ref[...]
        ub = u.astype(BF16)
        first_chunk = c == nt - 1
        for j in range(SSM_CHUNKS):
            g_s[:, j * CHUNK_LANES:(j + 1) * CHUNK_LANES] = _dot(dyb[:, j * LANES:(j + 1) * LANES], wc_ref[j], NT)
        for j in range(SSM_CHUNKS):
            re = slice(j * CHUNK_LANES, j * CHUNK_LANES + CHUNK_STATES)
            im = slice(j * CHUNK_LANES + CHUNK_STATES, (j + 1) * CHUNK_LANES)
            row = lax.broadcasted_iota(jnp.int32, (SUBLANES, CHUNK_STATES), 0)

            def blk(n, car):
                bi = nb - 1 - n
                r0 = pl.multiple_of(bi * SUBLANES, SUBLANES)
                gr, gi = _scan_block(g_s[pl.ds(r0, SUBLANES), re], g_s[pl.ds(r0, SUBLANES), im],
                                     tab_ref, re, im, car[0], car[1], True)
                g_s[pl.ds(r0, SUBLANES), re] = gr
                g_s[pl.ds(r0, SUBLANES), im] = gi
                rp = pl.multiple_of(jnp.maximum(bi - 1, 0) * SUBLANES, SUBLANES)
                inside = bi > 0
                live = jnp.where(jnp.logical_or(inside, jnp.logical_not(first_chunk)), 1.0, 0.0)
                pr = jnp.where(inside, h_ref[pl.ds(rp, SUBLANES), re], hp_ref[:, re])[SUBLANES - 1:SUBLANES] * live
                pi = jnp.where(inside, h_ref[pl.ds(rp, SUBLANES), im], hp_ref[:, im])[SUBLANES - 1:SUBLANES] * live
                hr = jnp.where(row >= 1, pltpu.roll(h_ref[pl.ds(r0, SUBLANES), re], 1, 0), pr)
                hi = jnp.where(row >= 1, pltpu.roll(h_ref[pl.ds(r0, SUBLANES), im], 1, 0), pi)
                return (gr[0:1], gi[0:1], car[2] + gr * hr + gi * hi, car[3] + gi * hr - gr * hi)

            zero = jnp.zeros((SUBLANES, CHUNK_STATES), F32)
            cr, ci, sr, si = lax.fori_loop(0, nb, blk, (carry[0:1, re], carry[0:1, im], zero, zero))
            carry[0:1, re] = cr
            carry[0:1, im] = ci
            ga_ref[:, re] += sr
            ga_ref[:, im] += si
        for j in range(SSM_CHUNKS):
            cols = slice(j * LANES, (j + 1) * LANES)
            lanes = slice(j * CHUNK_LANES, (j + 1) * CHUNK_LANES)
            gj = g_s[:, lanes].astype(BF16)
            du_ref[:, cols] = (_dot(gj, wb_ref[j], NT) + d_ref[:, cols] * dy[:, cols]).astype(du_ref.dtype)
            gwb_ref[j] += _dot(ub[:, cols], gj, TN)
            gwc_ref[j] += _dot(h_ref[:, lanes].astype(BF16), dyb[:, cols], TN)
        gd_ref[...] += _fold8(dy * u)

    def prev_rows(b, c):
        chunk = nt - 1 - c
        return (jnp.maximum((b * nt + chunk) * nb - 1, 0), 0)

    return pl.pallas_call(
        body, name="ssm_bwd", grid=(bsz, nt),
        in_specs=[pl.BlockSpec((tt, SSM_WIDTH), lambda b, c: (b * nt + nt - 1 - c, 0)),
                  pl.BlockSpec((tt, SSM_WIDTH), lambda b, c: (b * nt + nt - 1 - c, u_blk)),
                  pl.BlockSpec((tt, STATE_LANES), lambda b, c: (b * nt + nt - 1 - c, 0)),
                  pl.BlockSpec((SUBLANES, STATE_LANES), prev_rows),
                  pl.BlockSpec((SSM_CHUNKS, LANES, CHUNK_LANES), lambda b, c: (0, 0, 0)),
                  pl.BlockSpec((SSM_CHUNKS, CHUNK_LANES, LANES), lambda b, c: (0, 0, 0)),
                  pl.BlockSpec((TAB_ROWS, STATE_LANES), lambda b, c: (0, 0)),
                  pl.BlockSpec((1, SSM_WIDTH), lambda b, c: (0, 0))],
        out_specs=[pl.BlockSpec((tt, SSM_WIDTH), lambda b, c: (b * nt + nt - 1 - c, 0)),
                   pl.BlockSpec((SUBLANES, STATE_LANES), lambda b, c: (0, 0)),
                   pl.BlockSpec((SSM_CHUNKS, LANES, CHUNK_LANES), lambda b, c: (0, 0, 0)),
                   pl.BlockSpec((SSM_CHUNKS, CHUNK_LANES, LANES), lambda b, c: (0, 0, 0)),
                   pl.BlockSpec((SUBLANES, SSM_WIDTH), lambda b, c: (0, 0))],
        out_shape=[jax.ShapeDtypeStruct((bsz * seq, SSM_WIDTH), BF16),
                   jax.ShapeDtypeStruct((SUBLANES, STATE_LANES), F32),
                   jax.ShapeDtypeStruct((SSM_CHUNKS, LANES, CHUNK_LANES), F32),
                   jax.ShapeDtypeStruct((SSM_CHUNKS, CHUNK_LANES, LANES), F32),
                   jax.ShapeDtypeStruct((SUBLANES, SSM_WIDTH), F32)],
        scratch_shapes=[pltpu.VMEM((tt, STATE_LANES), F32), pltpu.VMEM((SUBLANES, STATE_LANES), F32)],
        compiler_params=_params(("arbitrary", "arbitrary")),
    )(dys, rest, hs, hs, wb4, wc4, tabs, dskip)


def _gelu(v):
    t = jnp.tanh(GELU_C * (v + GELU_A * v * v * v))
    return 0.5 * v * (1.0 + t), t


def mix_fwd(ya, ys, rest, x0, wglu, bglu, wba, wbb, wout):
    t, d = x0.shape
    tm = _rows(t, 256)

    def body(ya_ref, ys_ref, ga_ref, gb_ref, x_ref, wg_ref, bg_ref, wa_ref, wb_ref, wo_ref,
             x1_ref, z_ref, pa_ref, pb_ref, yb_ref, yb2_ref, mx_ref):
        yb, _ = _gelu(ys_ref[...])
        ybb = yb.astype(BF16)
        z = _dot(ybb, wg_ref[...]) + bg_ref[...]
        yb2 = (yb * _sigmoid(z)).astype(BF16)
        pa = _dot(ya_ref[...], wa_ref[...])
        pb = _dot(yb2, wb_ref[...])
        mixed = (_sigmoid(ga_ref[...]) * pa + _sigmoid(gb_ref[...]) * pb).astype(BF16)
        x1_ref[...] = x_ref[...] + _dot(mixed, wo_ref[...])
        z_ref[...] = z.astype(z_ref.dtype)
        pa_ref[...] = pa.astype(pa_ref.dtype)
        pb_ref[...] = pb.astype(pb_ref.dtype)
        yb_ref[...] = ybb
        yb2_ref[...] = yb2
        mx_ref[...] = mixed

    row = lambda w: pl.BlockSpec((tm, w), lambda i: (i, 0))
    full = lambda a: pl.BlockSpec(a.shape, lambda i: (0,) * a.ndim)
    return pl.pallas_call(
        body, name="mix_fwd", grid=(t // tm,),
        in_specs=[row(ATTN_WIDTH), row(SSM_WIDTH),
                  pl.BlockSpec((tm, d), lambda i: (i, 0)), pl.BlockSpec((tm, d), lambda i: (i, 1)),
                  row(d), full(wglu), full(bglu), full(wba), full(wbb), full(wout)],
        out_specs=[row(d), row(SSM_WIDTH), row(d), row(d), row(SSM_WIDTH), row(SSM_WIDTH), row(d)],
        out_shape=[jax.ShapeDtypeStruct((t, d), F32), jax.ShapeDtypeStruct((t, SSM_WIDTH), BF16),
                   jax.ShapeDtypeStruct((t, d), BF16), jax.ShapeDtypeStruct((t, d), BF16),
                   jax.ShapeDtypeStruct((t, SSM_WIDTH), BF16), jax.ShapeDtypeStruct((t, SSM_WIDTH), BF16),
                   jax.ShapeDtypeStruct((t, d), BF16)],
        compiler_params=_params(("parallel",)),
    )(ya, ys, rest, rest, x0, wglu, bglu, wba, wbb, wout)


def mix_bwd(dx1, rest, pa, pb, z, ys, wglu, wba, wbb, wout):
    t, d = dx1.shape
    tm = _rows(t, 256)

    def body(dx_ref, ga_ref, gb_ref, pa_ref, pb_ref, z_ref, ys_ref, wg_ref, wa_ref, wb_ref, wo_ref,
             dya_ref, dys_ref, dg_ref, dpa_ref, dpb_ref, dz_ref, dbg_ref):
        @pl.when(pl.program_id(0) == 0)
        def _():
            dbg_ref[...] = jnp.zeros_like(dbg_ref)

        dmix = _dot(dx_ref[...].astype(BF16), wo_ref[...], NT)
        sa = _sigmoid(ga_ref[...])
        sb = _sigmoid(gb_ref[...])
        dpa = (dmix * sa).astype(BF16)
        dpb = (dmix * sb).astype(BF16)
        dg_ref[:, 0:d] = (dmix * pa_ref[...].astype(F32) * sa * (1.0 - sa)).astype(dg_ref.dtype)
        dg_ref[:, d:2 * d] = (dmix * pb_ref[...].astype(F32) * sb * (1.0 - sb)).astype(dg_ref.dtype)
        dpa_ref[...] = dpa
        dpb_ref[...] = dpb
        dya_ref[...] = _dot(dpa, wa_ref[...], NT).astype(dya_ref.dtype)
        dyb2 = _dot(dpb, wb_ref[...], NT)
        ys = ys_ref[...]
        yb, th = _gelu(ys)
        sg = _sigmoid(z_ref[...].astype(F32))
        dz = dyb2 * yb * sg * (1.0 - sg)
        dzb = dz.astype(BF16)
        dz_ref[...] = dzb
        dbg_ref[...] += _fold8(dz)
        dyb = dyb2 * sg + _dot(dzb, wg_ref[...], NT)
        dgelu = 0.5 * (1.0 + th) + 0.5 * ys * (1.0 - th * th) * GELU_C * (1.0 + 3.0 * GELU_A * ys * ys)
        dys_ref[...] = (dyb * dgelu).astype(dys_ref.dtype)

    row = lambda w: pl.BlockSpec((tm, w), lambda i: (i, 0))
    full = lambda a: pl.BlockSpec(a.shape, lambda i: (0,) * a.ndim)
    return pl.pallas_call(
        body, name="mix_bwd", grid=(t // tm,),
        in_specs=[row(d), pl.BlockSpec((tm, d), lambda i: (i, 0)), pl.BlockSpec((tm, d), lambda i: (i, 1)),
                  row(d), row(d), row(SSM_WIDTH), row(SSM_WIDTH), full(wglu), full(wba), full(wbb), full(wout)],
        out_specs=[row(ATTN_WIDTH), row(SSM_WIDTH), row(2 * d), row(d), row(d), row(SSM_WIDTH),
                   pl.BlockSpec((SUBLANES, SSM_WIDTH), lambda i: (0, 0))],
        out_shape=[jax.ShapeDtypeStruct((t, ATTN_WIDTH), BF16), jax.ShapeDtypeStruct((t, SSM_WIDTH), BF16),
                   jax.ShapeDtypeStruct((t, 2 * d), BF16), jax.ShapeDtypeStruct((t, d), BF16),
                   jax.ShapeDtypeStruct((t, d), BF16), jax.ShapeDtypeStruct((t, SSM_WIDTH), BF16),
                   jax.ShapeDtypeStruct((SUBLANES, SSM_WIDTH), F32)],
        compiler_params=_params(("arbitrary",)),
    )(dx1, rest, rest, pa, pb, z, ys, wglu, wba, wbb, wout)


def mlp_fwd(x1, g, wup, wdown):
    t, d = x1.shape
    ff = wup.shape[1]
    tm, tf = _rows(t, 512), _pick(ff, 1024)
    nf = ff // tf

    def body(x_ref, g_ref, wu_ref, wd_ref, x2_ref, up_ref, h_s, acc_s):
        f = pl.program_id(1)

        @pl.when(f == 0)
        def _():
            xv = x_ref[...]
            h_s[...] = (xv * _rms_scale(xv) * g_ref[...]).astype(BF16)
            acc_s[...] = jnp.zeros_like(acc_s)

        up = _dot(h_s[...], wu_ref[...])
        up_ref[...] = up.astype(up_ref.dtype)
        act = jnp.square(jnp.maximum(up, 0.0)).astype(BF16)
        acc_s[...] += _dot(act, wd_ref[...])

        @pl.when(f == nf - 1)
        def _():
            x2_ref[...] = x_ref[...] + acc_s[...]

    return pl.pallas_call(
        body, name="mlp_fwd", grid=(t // tm, nf),
        in_specs=[pl.BlockSpec((tm, d), lambda i, f: (i, 0)), pl.BlockSpec((1, d), lambda i, f: (0, 0)),
                  pl.BlockSpec((d, tf), lambda i, f: (0, f)), pl.BlockSpec((tf, d), lambda i, f: (f, 0))],
        out_specs=[pl.BlockSpec((tm, d), lambda i, f: (i, 0)), pl.BlockSpec((tm, tf), lambda i, f: (i, f))],
        out_shape=[jax.ShapeDtypeStruct((t, d), F32), jax.ShapeDtypeStruct((t, ff), BF16)],
        scratch_shapes=[pltpu.VMEM((tm, d), BF16), pltpu.VMEM((tm, d), F32)],
        compiler_params=_params(("parallel", "arbitrary")),
    )(x1, g, wup, wdown)


def mlp_bwd(dx2, up, x1, g, wup, wdown):
    t, d = x1.shape
    ff = wup.shape[1]
    tm, tf = _rows(t, 512), _pick(ff, 1024)
    nf = ff // tf

    def body(dx_ref, up_ref, x_ref, g_ref, wu_ref, wd_ref, dup_ref, dx1_ref, dg_ref, dxb_s, acc_s):
        i = pl.program_id(0)
        f = pl.program_id(1)

        @pl.when((i == 0) & (f == 0))
        def _():
            dg_ref[...] = jnp.zeros_like(dg_ref)

        @pl.when(f == 0)
        def _():
            dxb_s[...] = dx_ref[...].astype(BF16)
            acc_s[...] = jnp.zeros_like(acc_s)

        dact = _dot(dxb_s[...], wd_ref[...], NT)
        dup = (dact * 2.0 * jnp.maximum(up_ref[...].astype(F32), 0.0)).astype(BF16)
        dup_ref[...] = dup
        acc_s[...] += _dot(dup, wu_ref[...], NT)

        @pl.when(f == nf - 1)
        def _():
            dxn, dgain = _rms_bwd(x_ref[...], g_ref[...], acc_s[...])
            dx1_ref[...] = dx_ref[...] + dxn
            dg_ref[...] += _fold8(dgain)

    return pl.pallas_call(
        body, name="mlp_bwd", grid=(t // tm, nf),
        in_specs=[pl.BlockSpec((tm, d), lambda i, f: (i, 0)), pl.BlockSpec((tm, tf), lambda i, f: (i, f)),
                  pl.BlockSpec((tm, d), lambda i, f: (i, 0)), pl.BlockSpec((1, d), lambda i, f: (0, 0)),
                  pl.BlockSpec((d, tf), lambda i, f: (0, f)), pl.BlockSpec((tf, d), lambda i, f: (f, 0))],
        out_specs=[pl.BlockSpec((tm, tf), lambda i, f: (i, f)), pl.BlockSpec((tm, d), lambda i, f: (i, 0)),
                   pl.BlockSpec((SUBLANES, d), lambda i, f: (0, 0))],
        out_shape=[jax.ShapeDtypeStruct((t, ff), BF16), jax.ShapeDtypeStruct((t, d), F32),
                   jax.ShapeDtypeStruct((SUBLANES, d), F32)],
        scratch_shapes=[pltpu.VMEM((tm, d), BF16), pltpu.VMEM((tm, d), F32)],
        compiler_params=_params(("arbitrary", "arbitrary")),
    )(dx2, up, x1, g, wup, wdown)


def proj_bwd(dproj, wpad, x0, dx1, g):
    t, d = x0.shape
    m = wpad.shape[1]
    tm = _rows(t, 256)

    def body(dp_ref, w_ref, x_ref, dx1_ref, g_ref, dx0_ref, dg_ref):
        @pl.when(pl.program_id(0) == 0)
        def _():
            dg_ref[...] = jnp.zeros_like(dg_ref)

        dh = _dot(dp_ref[...], w_ref[...], NT)
        dxn, dgain = _rms_bwd(x_ref[...], g_ref[...], dh)
        dx0_ref[...] = dx1_ref[...] + dxn
        dg_ref[...] += _fold8(dgain)

    return pl.pallas_call(
        body, name="proj_bwd", grid=(t // tm,),
        in_specs=[pl.BlockSpec((tm, m), lambda i: (i, 0)), pl.BlockSpec((d, m), lambda i: (0, 0)),
                  pl.BlockSpec((tm, d), lambda i: (i, 0)), pl.BlockSpec((tm, d), lambda i: (i, 0)),
                  pl.BlockSpec((1, d), lambda i: (0, 0))],
        out_specs=[pl.BlockSpec((tm, d), lambda i: (i, 0)), pl.BlockSpec((SUBLANES, d), lambda i: (0, 0))],
        out_shape=[jax.ShapeDtypeStruct((t, d), F32), jax.ShapeDtypeStruct((SUBLANES, d), F32)],
        compiler_params=_params(("arbitrary",)),
    )(dproj, wpad, x0, dx1, g)


def final_loss(x, g, target):
    t, d = x.shape
    tm = _rows(t, 512)

    def body(x_ref, g_ref, t_ref, dx_ref, ls_ref, dg_ref):
        @pl.when(pl.program_id(0) == 0)
        def _():
            ls_ref[...] = jnp.zeros_like(ls_ref)
            dg_ref[...] = jnp.zeros_like(dg_ref)

        xv = x_ref[...]
        gv = g_ref[...]
        err = xv * _rms_scale(xv) * gv - t_ref[...]
        ls_ref[...] += _fold8(err * err) * (0.5 / d)
        dxn, dgain = _rms_bwd(xv, gv, err * (1.0 / d))
        dx_ref[...] = dxn
        dg_ref[...] += _fold8(dgain)

    return pl.pallas_call(
        body, name="final_loss", grid=(t // tm,),
        in_specs=[pl.BlockSpec((tm, d), lambda i: (i, 0)), pl.BlockSpec((1, d), lambda i: (0, 0)),
                  pl.BlockSpec((tm, d), lambda i: (i, 0))],
        out_specs=[pl.BlockSpec((tm, d), lambda i: (i, 0)), pl.BlockSpec((SUBLANES, d), lambda i: (0, 0)),
                   pl.BlockSpec((SUBLANES, d), lambda i: (0, 0))],
        out_shape=[jax.ShapeDtypeStruct((t, d), F32), jax.ShapeDtypeStruct((SUBLANES, d), F32),
                   jax.ShapeDtypeStruct((SUBLANES, d), F32)],
        compiler_params=_params(("arbitrary",)),
    )(x, g, target)


def _place():
    x, y, c = lax.axis_index("x"), lax.axis_index("y"), lax.axis_index("c")
    chips = [(1 - x, y), (x, 1 - y), (1 - x, 1 - y)]
    return x, y, c, chips


def _exchange_call(body, name, ins, out_shapes, copies_per_tensor, local_copies):
    n = len(ins)
    scratch = [pltpu.SemaphoreType.DMA((copies_per_tensor * n,)), pltpu.SemaphoreType.DMA((copies_per_tensor * n,))]
    if local_copies:
        scratch.append(pltpu.SemaphoreType.DMA((n,)))
    return pl.pallas_call(
        body, name=name, in_specs=[ANY] * n, out_specs=[ANY] * n, out_shape=out_shapes, scratch_shapes=scratch,
    )(*ins)


def gather_weights(ws):
    n = len(ws)
    lh = ws[0].shape[0] // 2

    def body(*refs):
        w_refs, out_refs = refs[:n], refs[n:2 * n]
        send_sems, recv_sems, local_sems = refs[2 * n:]
        x, y, c, chips = _place()
        me = 2 * x + y
        sibling = (x, y, 1 - c)

        def half(t, chip_idx, core):
            return out_refs[t].at[chip_idx, pl.ds(core * lh, lh)]

        def copy(t, k, src, dst, to):
            return pltpu.make_async_remote_copy(src_ref=src, dst_ref=dst, send_sem=send_sems.at[6 * t + k],
                                                recv_sem=recv_sems.at[6 * t + k], device_id=to, device_id_type=MESH)

        mine = [pltpu.make_async_copy(w_refs[t], out_refs[t].at[me], local_sems.at[t]) for t in range(n)]
        for cp in mine:
            cp.start()
        first = [copy(t, j, w_refs[t].at[pl.ds(c * lh, lh)], half(t, me, c), (*chip, c))
                 for j, chip in enumerate(chips) for t in range(n)]
        for cp in first:
            cp.start()
        passed = []
        for j, (cx, cy) in enumerate(chips):
            idx = 2 * cx + cy
            for t in range(n):
                copy(t, j, half(t, idx, c), half(t, idx, c), (x, y, c)).wait_recv()
                fwd = copy(t, 3 + j, half(t, idx, c), half(t, idx, c), sibling)
                fwd.start()
                passed.append(fwd)
        for j, (cx, cy) in enumerate(chips):
            idx = 2 * cx + cy
            for t in range(n):
                copy(t, 3 + j, half(t, idx, 1 - c), half(t, idx, 1 - c), (x, y, c)).wait_recv()
        for cp in first + passed:
            cp.wait_send()
        for cp in mine:
            cp.wait()

    return _exchange_call(body, "gather_weights", ws,
                          [jax.ShapeDtypeStruct((4,) + w.shape, w.dtype) for w in ws], 6, True)


def exchange_sibling_half(gs):
    n = len(gs)
    lh = gs[0].shape[1] // 2

    def body(*refs):
        g_refs, out_refs = refs[:n], refs[n:2 * n]
        send_sems, recv_sems = refs[2 * n:]
        x, y, c, _ = _place()
        cps = [pltpu.make_async_remote_copy(src_ref=g_refs[t].at[:, pl.ds((1 - c) * lh, lh)], dst_ref=out_refs[t],
                                            send_sem=send_sems.at[t], recv_sem=recv_sems.at[t],
                                            device_id=(x, y, 1 - c), device_id_type=MESH) for t in range(n)]
        for cp in cps:
            cp.start()
        for cp in cps:
            cp.wait()

    return _exchange_call(body, "exchange_sibling_half", gs,
                          [jax.ShapeDtypeStruct((4, lh) + g.shape[2:], g.dtype) for g in gs], 1, False)


def scatter_to_owners(ps):
    n = len(ps)

    def body(*refs):
        a_refs, out_refs = refs[:n], refs[n:2 * n]
        send_sems, recv_sems, local_sems = refs[2 * n:]
        x, y, c, chips = _place()
        me = 2 * x + y
        mine = [pltpu.make_async_copy(a_refs[t].at[me], out_refs[t].at[3], local_sems.at[t]) for t in range(n)]
        for cp in mine:
            cp.start()
        sends = [pltpu.make_async_remote_copy(src_ref=a_refs[t].at[2 * cx + cy], dst_ref=out_refs[t].at[j],
                                              send_sem=send_sems.at[3 * t + j], recv_sem=recv_sems.at[3 * t + j],
                                              device_id=(cx, cy, c), device_id_type=MESH)
                 for j, (cx, cy) in enumerate(chips) for t in range(n)]
        for cp in sends:
            cp.start()
        for cp in sends:
            cp.wait()
        for cp in mine:
            cp.wait()

    return _exchange_call(body, "scatter_to_owners", ps,
                          [jax.ShapeDtypeStruct(p.shape, p.dtype) for p in ps], 3, True)


def share_with_sibling(rs):
    n = len(rs)
    lh = rs[0].shape[0]

    def body(*refs):
        r_refs, out_refs = refs[:n], refs[n:2 * n]
        send_sems, recv_sems, local_sems = refs[2 * n:]
        x, y, c, _ = _place()
        mine = [pltpu.make_async_copy(r_refs[t], out_refs[t].at[pl.ds(c * lh, lh)], local_sems.at[t]) for t in range(n)]
        for cp in mine:
            cp.start()
        cps = [pltpu.make_async_remote_copy(src_ref=r_refs[t], dst_ref=out_refs[t].at[pl.ds(c * lh, lh)],
                                            send_sem=send_sems.at[t], recv_sem=recv_sems.at[t],
                                            device_id=(x, y, 1 - c), device_id_type=MESH) for t in range(n)]
        for cp in cps:
            cp.start()
        for cp in cps:
            cp.wait()
        for cp in mine:
            cp.wait()

    return _exchange_call(body, "share_with_sibling", rs,
                          [jax.ShapeDtypeStruct((2 * lh,) + r.shape[1:], r.dtype) for r in rs], 1, True)


def exchange_all(sp):
    def body(s_ref, out_ref, send_sems, recv_sems, local_sem):
        x, y, c, _ = _place()
        me = 4 * x + 2 * y + c
        mine = pltpu.make_async_copy(s_ref, out_ref.at[me], local_sem)
        mine.start()
        sends = []
        for m in range(1, 8):
            fx, fy, fc = (m >> 2) & 1, (m >> 1) & 1, m & 1
            peer = ((1 - x) if fx else x, (1 - y) if fy else y, (1 - c) if fc else c)
            cp = pltpu.make_async_remote_copy(src_ref=s_ref, dst_ref=out_ref.at[me],
                                              send_sem=send_sems.at[m - 1], recv_sem=recv_sems.at[m - 1],
                                              device_id=peer, device_id_type=MESH)
            cp.start()
            sends.append(cp)
        for cp in sends:
            cp.wait()
        mine.wait()

    return pl.pallas_call(
        body, name="exchange_all",
        in_specs=[ANY], out_specs=ANY,
        out_shape=jax.ShapeDtypeStruct((8,) + sp.shape, sp.dtype),
        scratch_shapes=[pltpu.SemaphoreType.DMA((7,)), pltpu.SemaphoreType.DMA((7,)), pltpu.SemaphoreType.DMA],
    )(sp)


def add_sibling(gp, s1, core, name):
    _, nl, r, cdim = gp.shape
    lh = nl // 2
    tr = _rows(r, PACK_ROW_TILE)

    def body(c_ref, g_ref, s_ref, o_ref):
        o_ref[...] = (g_ref[...] + s_ref[...]).astype(o_ref.dtype)

    return pl.pallas_call(
        body, name=name,
        grid_spec=pltpu.PrefetchScalarGridSpec(
            num_scalar_prefetch=1, grid=(4, lh, r // tr),
            in_specs=[pl.BlockSpec((1, 1, tr, cdim), lambda k, l, i, c_ref: (k, c_ref[0] * lh + l, i, 0)),
                      pl.BlockSpec((1, 1, tr, cdim), lambda k, l, i, c_ref: (k, l, i, 0))],
            out_specs=pl.BlockSpec((1, 1, tr, cdim), lambda k, l, i, c_ref: (k, l, i, 0))),
        out_shape=jax.ShapeDtypeStruct(s1.shape, BF16),
        compiler_params=_params(("parallel", "parallel", "parallel")),
    )(core, gp, s1)


def add_chips(s2, name):
    _, lh, r, cdim = s2.shape
    tr = _rows(r, PACK_ROW_TILE)

    def body(s_ref, o_ref):
        o_ref[0] = ((s_ref[0, 0].astype(F32) + s_ref[1, 0].astype(F32)) + s_ref[2, 0].astype(F32)) + s_ref[3, 0].astype(F32)

    return pl.pallas_call(
        body, name=name, grid=(lh, r // tr),
        in_specs=[pl.BlockSpec((4, 1, tr, cdim), lambda l, i: (0, l, i, 0))],
        out_specs=pl.BlockSpec((1, tr, cdim), lambda l, i: (l, i, 0)),
        out_shape=jax.ShapeDtypeStruct(s2.shape[1:], F32),
        compiler_params=_params(("parallel", "parallel")),
    )(s2)


def _adamw_math(w, g, m, v):
    m = ADAM_B1 * m + (1.0 - ADAM_B1) * g
    v = ADAM_B2 * v + (1.0 - ADAM_B2) * (g * g)
    m_hat = m / (1.0 - ADAM_B1 ** ADAM_STEP)
    v_hat = v / (1.0 - ADAM_B2 ** ADAM_STEP)
    delta = -ADAM_LR * (m_hat / (jnp.sqrt(v_hat) + ADAM_EPS) + ADAM_WD * w)
    return delta, m, v


def adamw(w, g, m, v, name):
    shape = w.shape
    rows, cdim = shape[0] * shape[1], shape[2]
    tr = _rows(rows, PACK_ROW_TILE)

    def body(w_ref, g_ref, m_ref, v_ref, d_ref, nm_ref, nv_ref):
        d, nm, nv = _adamw_math(w_ref[...], g_ref[...], m_ref[...], v_ref[...])
        d_ref[...] = d
        nm_ref[...] = nm
        nv_ref[...] = nv

    spec = pl.BlockSpec((tr, cdim), lambda i: (i, 0))
    outs = pl.pallas_call(
        body, name=name, grid=(rows // tr,),
        in_specs=[spec] * 4, out_specs=[spec] * 3,
        out_shape=[jax.ShapeDtypeStruct((rows, cdim), F32)] * 3,
        compiler_params=_params(("parallel",)),
    )(*[a.reshape(rows, cdim) for a in (w, g, m, v)])
    return [o.reshape(shape) for o in outs]


def sum_and_adamw(parts, w, m, v):
    _, r, cdim = parts.shape
    tr = _rows(r, PACK_ROW_TILE)

    def body(p_ref, w_ref, m_ref, v_ref, g_ref, d_ref, nm_ref, nv_ref):
        g = p_ref[0]
        for k in range(1, 8):
            g = g + p_ref[k]
        d, nm, nv = _adamw_math(w_ref[...], g, m_ref[...], v_ref[...])
        g_ref[...] = g
        d_ref[...] = d
        nm_ref[...] = nm
        nv_ref[...] = nv

    spec = pl.BlockSpec((tr, cdim), lambda i: (i, 0))
    return pl.pallas_call(
        body, name="sum_and_adamw", grid=(r // tr,),
        in_specs=[pl.BlockSpec((8, tr, cdim), lambda i: (0, i, 0)), spec, spec, spec],
        out_specs=[spec] * 4,
        out_shape=[jax.ShapeDtypeStruct((r, cdim), F32)] * 4,
        compiler_params=_params(("parallel",)),
    )(parts, w, m, v)


SHARDED = ("w_in", "w_glu", "w_branch_a", "w_branch_b", "w_out", "w_mlp_up", "w_mlp_down")
SHARD_AXIS = {"w_in": 2, "w_glu": 1, "w_branch_a": 2, "w_branch_b": 2, "w_out": 1, "w_mlp_up": 2, "w_mlp_down": 1}
SMALL = ("norm_mix", "b_forget", "ssm_lambda_re", "ssm_lambda_im", "ssm_log_dt", "ssm_b_re", "ssm_b_im",
         "ssm_c_re", "ssm_c_im", "ssm_d", "b_glu", "norm_mlp", "norm_final")


def pack_flat(arrs):
    flat = jnp.concatenate([a.reshape(-1).astype(F32) for a in arrs])
    unit = PACK_COLS * PACK_ROW_TILE
    rows = (flat.shape[0] + unit - 1) // unit * PACK_ROW_TILE
    return jnp.pad(flat, (0, rows * PACK_COLS - flat.shape[0])).reshape(rows, PACK_COLS)


def unpack_flat(packed, shapes):
    flat = packed.reshape(-1)
    out, off = [], 0
    for s in shapes:
        n = math.prod(s)
        out.append(flat[off:off + n].reshape(tuple(s)))
        off += n
    return out


def _discretise(lam_re, lam_im, log_dt, b_re, b_im):
    dt = jnp.exp(log_dt)[:, None]
    mag = jnp.exp(lam_re * dt)
    ar = mag * jnp.cos(lam_im * dt)
    ai = mag * jnp.sin(lam_im * dt)
    den = lam_re * lam_re + lam_im * lam_im
    cr = ((ar - 1.0) * lam_re + ai * lam_im) / den
    ci = (ai * lam_re - (ar - 1.0) * lam_im) / den
    bbr = cr[:, :, None] * b_re - ci[:, :, None] * b_im
    bbi = cr[:, :, None] * b_im + ci[:, :, None] * b_re
    return ar, ai, bbr, bbi


def _state_lanes(re, im):
    return jnp.concatenate([re.reshape(SSM_CHUNKS, CHUNK_STATES), im.reshape(SSM_CHUNKS, CHUNK_STATES)],
                           axis=1).reshape(STATE_LANES)


def _ssm_inputs(ar, ai, bbr, bbi, c_re, c_im):
    pr, pi = [ar], [ai]
    for _ in range(SUBLANES - 1):
        pr, pi = pr + [pr[-1] * ar - pi[-1] * ai], pi + [pr[-1] * ai + pi[-1] * ar]
    power = lambda n: _state_lanes(pr[n - 1], pi[n - 1])
    zero = jnp.zeros((STATE_LANES,), F32)
    rows = [power(1), power(2), power(4)] + [zero] * 5
    rows += [power(i + 1) for i in range(SUBLANES)]
    rows += [power(SUBLANES - i) for i in range(SUBLANES)]
    tabs = jnp.stack(rows)
    eye = jnp.eye(CHUNK_GROUPS, dtype=F32)

    def to_wb(bb):
        t = bb.reshape(SSM_CHUNKS, CHUNK_GROUPS, SSM_STATE, SSM_GROUP_CH).transpose(0, 1, 3, 2)
        return jnp.einsum("jgcp,gh->jgchp", t, eye).reshape(SSM_CHUNKS, LANES, CHUNK_STATES)

    def to_wc(cc):
        t = cc.reshape(SSM_CHUNKS, CHUNK_GROUPS, SSM_GROUP_CH, SSM_STATE)
        return jnp.einsum("jgcp,gh->jhpgc", t, eye).reshape(SSM_CHUNKS, CHUNK_STATES, LANES)

    wb4 = jnp.concatenate([to_wb(bbr), to_wb(bbi)], axis=2).astype(BF16)
    wc4 = jnp.concatenate([to_wc(c_re), -to_wc(c_im)], axis=1).astype(BF16)
    return tabs, wb4, wc4


def _ssm_param_grads(ga8, gwb, gwc):
    eye = jnp.eye(CHUNK_GROUPS, dtype=F32)
    ga = jnp.sum(ga8, axis=0).reshape(SSM_CHUNKS, 2, CHUNK_STATES)
    gar = ga[:, 0].reshape(SSM_GROUPS, SSM_STATE)
    gai = ga[:, 1].reshape(SSM_GROUPS, SSM_STATE)

    def from_wb(g):
        t = g.reshape(SSM_CHUNKS, CHUNK_GROUPS, SSM_GROUP_CH, CHUNK_GROUPS, SSM_STATE)
        return jnp.einsum("jgchp,gh->jgpc", t, eye).reshape(SSM_GROUPS, SSM_STATE, SSM_GROUP_CH)

    def from_wc(g):
        t = g.reshape(SSM_CHUNKS, CHUNK_GROUPS, SSM_STATE, CHUNK_GROUPS, SSM_GROUP_CH)
        return jnp.einsum("jhpgc,gh->jgcp", t, eye).reshape(SSM_GROUPS, SSM_GROUP_CH, SSM_STATE)

    return (gar, gai, from_wb(gwb[:, :, :CHUNK_STATES]), from_wb(gwb[:, :, CHUNK_STATES:]),
            from_wc(gwc[:, :CHUNK_STATES]), -from_wc(gwc[:, CHUNK_STATES:]))


def kernel(x, norm_mix, w_in, b_forget, ssm_lambda_re, ssm_lambda_im, ssm_log_dt, ssm_b_re, ssm_b_im, ssm_c_re, ssm_c_im, ssm_d, w_glu, b_glu, w_branch_a, w_branch_b, w_out, norm_mlp, w_mlp_up, w_mlp_down, norm_final, loss_target, m_norm_mix, m_w_in, m_b_forget, m_ssm_lambda_re, m_ssm_lambda_im, m_ssm_log_dt, m_ssm_b_re, m_ssm_b_im, m_ssm_c_re, m_ssm_c_im, m_ssm_d, m_w_glu, m_b_glu, m_w_branch_a, m_w_branch_b, m_w_out, m_norm_mlp, m_w_mlp_up, m_w_mlp_down, m_norm_final, v_norm_mix, v_w_in, v_b_forget, v_ssm_lambda_re, v_ssm_lambda_im, v_ssm_log_dt, v_ssm_b_re, v_ssm_b_im, v_ssm_c_re, v_ssm_c_im, v_ssm_d, v_w_glu, v_b_glu, v_w_branch_a, v_w_branch_b, v_w_out, v_norm_mlp, v_w_mlp_up, v_w_mlp_down, v_norm_final):
    args = dict(locals())
    bsz, seq, d = x.shape
    nl = norm_mix.shape[0]
    tokens = bsz * seq
    aw, sw = ATTN_WIDTH, SSM_WIDTH
    core = lax.axis_index("c").astype(jnp.int32).reshape(1)

    gathered = dict(zip(SHARDED, gather_weights([args[n].astype(BF16) for n in SHARDED])))
    full = {n: jnp.concatenate([gathered[n][k] for k in range(4)], axis=SHARD_AXIS[n]) for n in SHARDED}
    win_f = full["w_in"]
    o_f, o_u, o_ga, o_gb = 3 * aw, 3 * aw + ATTN_HEADS, 3 * aw + ATTN_HEADS + sw, 3 * aw + ATTN_HEADS + sw + d
    w_qkv = win_f[:, :, :o_f]
    w_rest = jnp.concatenate([win_f[:, :, o_ga:o_gb], win_f[:, :, o_gb:], win_f[:, :, o_u:o_ga],
                              jnp.pad(win_f[:, :, o_f:o_u], ((0, 0), (0, 0), (0, F_PAD - ATTN_HEADS)))], axis=2)
    w_pad = jnp.concatenate([w_qkv, w_rest], axis=2)
    u_blk = 2 * d // sw
    f_blk = (2 * d + sw) // F_PAD
    bf_pad = jnp.pad(b_forget, ((0, 0), (0, F_PAD - ATTN_HEADS)))

    disc = [jax.vjp(_discretise, ssm_lambda_re[l], ssm_lambda_im[l], ssm_log_dt[l], ssm_b_re[l], ssm_b_im[l])
            for l in range(nl)]

    xs = x.reshape(tokens, d)
    saved = []
    for l in range(nl):
        g1 = norm_mix[l].reshape(1, d)
        qkv = norm_matmul(xs, g1, w_qkv[l], BF16, "proj_qkv")
        rest = norm_matmul(xs, g1, w_rest[l], F32, "proj_rest")
        cumcol, cumrow = forget_cumsum(rest, bf_pad[l:l + 1], bsz, seq, f_blk)
        ya, lsec, lser = fox_fwd(qkv, cumrow, bsz, seq)
        tabs, wb4, wc4 = _ssm_inputs(*disc[l][0], ssm_c_re[l], ssm_c_im[l])
        dskip = ssm_d[l].reshape(1, sw)
        ys, hs = ssm_fwd(rest, wb4, wc4, tabs, dskip, bsz, seq, u_blk)
        x1, z, pa, pb, yb, yb2, mixed = mix_fwd(ya, ys, rest, xs, full["w_glu"][l], b_glu[l].reshape(1, sw),
                                                 full["w_branch_a"][l], full["w_branch_b"][l], full["w_out"][l])
        x2, up = mlp_fwd(x1, norm_mlp[l].reshape(1, d), full["w_mlp_up"][l], full["w_mlp_down"][l])
        saved.append(dict(x0=xs, qkv=qkv, rest=rest, cumcol=cumcol, cumrow=cumrow, ya=ya, lsec=lsec, lser=lser,
                          tabs=tabs, wb4=wb4, wc4=wc4, dskip=dskip, ys=ys, hs=hs, x1=x1, z=z, pa=pa, pb=pb,
                          yb=yb, yb2=yb2, mixed=mixed, up=up))
        xs = x2
    dx, loss_rows, dgf_rows = final_loss(xs, norm_final.reshape(1, d), loss_target.reshape(tokens, d))
    loss = lax.psum(jnp.sum(loss_rows), ("x", "y", "c"))

    big = {n: [None] * nl for n in SHARDED}
    small = {n: [None] * nl for n in SMALL if n != "norm_final"}
    for l in reversed(range(nl)):
        s = saved[l]
        g2 = norm_mlp[l].reshape(1, d)
        dup, dx1, dg2 = mlp_bwd(dx, s["up"], s["x1"], g2, full["w_mlp_up"][l], full["w_mlp_down"][l])
        big["w_mlp_down"][l] = matmul_tn(s["up"], dx, "grad_w_mlp_down", a_kind="relu2")
        big["w_mlp_up"][l] = matmul_tn(s["x1"], dup, "grad_w_mlp_up", a_kind="norm", gain=g2)
        small["norm_mlp"][l] = jnp.sum(dg2, axis=0)
        dya, dys, dgab, dpa, dpb, dz, dbg = mix_bwd(dx1, s["rest"], s["pa"], s["pb"], s["z"], s["ys"],
                                                    full["w_glu"][l], full["w_branch_a"][l], full["w_branch_b"][l],
                                                    full["w_out"][l])
        big["w_out"][l] = matmul_tn(s["mixed"], dx1, "grad_w_out")
        big["w_branch_a"][l] = matmul_tn(s["ya"], dpa, "grad_w_branch_a")
        big["w_branch_b"][l] = matmul_tn(s["yb2"], dpb, "grad_w_branch_b")
        big["w_glu"][l] = matmul_tn(s["yb"], dz, "grad_w_glu")
        small["b_glu"][l] = jnp.sum(dbg, axis=0)
        du, ga8, gwb, gwc, gd8 = ssm_bwd(dys, s["rest"], s["hs"], s["wb4"], s["wc4"], s["tabs"], s["dskip"],
                                         bsz, seq, u_blk)
        gar, gai, gbbr, gbbi, gcr, gci = _ssm_param_grads(ga8, gwb, gwc)
        glr, gli, gdt, gbr, gbi = disc[l][1]((gar, gai, gbbr, gbbi))
        small["ssm_lambda_re"][l], small["ssm_lambda_im"][l], small["ssm_log_dt"][l] = glr, gli, gdt
        small["ssm_b_re"][l], small["ssm_b_im"][l] = gbr, gbi
        small["ssm_c_re"][l], small["ssm_c_im"][l] = gcr, gci
        small["ssm_d"][l] = jnp.sum(gd8, axis=0)
        dq, dk, dv, df, dbf = fox_bwd(s["qkv"], dya, s["ya"], s["lsec"], s["lser"], s["cumcol"], s["cumrow"],
                                      s["rest"], bf_pad[l:l + 1], bsz, seq, f_blk)
        small["b_forget"][l] = jnp.sum(dbf, axis=0)[:ATTN_HEADS]
        dproj = jnp.concatenate([dq, dk, dv, dgab, du, df], axis=1)
        g1 = norm_mix[l].reshape(1, d)
        dwp = matmul_tn(s["x0"], dproj, "grad_w_in", a_kind="norm", gain=g1)
        big["w_in"][l] = jnp.concatenate(
            [dwp[:, :o_f], dwp[:, o_f + 2 * d + sw:o_f + 2 * d + sw + ATTN_HEADS],
             dwp[:, o_f + 2 * d:o_f + 2 * d + sw], dwp[:, o_f:o_f + 2 * d]], axis=1)
        dx, dg1 = proj_bwd(dproj, w_pad[l], s["x0"], dx1, g1)
        small["norm_mix"][l] = jnp.sum(dg1, axis=0)
    grad_x = dx.reshape(bsz, seq, d)

    gps = [jnp.stack(jnp.split(jnp.stack(big[n]), 4, axis=SHARD_AXIS[n])) for n in SHARDED]
    from_sibling = exchange_sibling_half(gps)
    pair = [add_sibling(g, s1, core, "add_sibling_" + n) for n, g, s1 in zip(SHARDED, gps, from_sibling)]
    slots = scatter_to_owners(pair)
    reduced = [add_chips(s2, "add_chips_" + n) for n, s2 in zip(SHARDED, slots)]
    out_g = dict(zip(SHARDED, share_with_sibling(reduced)))
    out_d, out_m, out_v = {}, {}, {}
    for n in SHARDED:
        out_d[n], out_m[n], out_v[n] = adamw(args[n], out_g[n], args["m_" + n], args["v_" + n], "adamw_" + n)

    small_g = [jnp.stack(small[n]) if n != "norm_final" else jnp.sum(dgf_rows, axis=0) for n in SMALL]
    small_shapes = [args[n].shape for n in SMALL]
    sg, sd, sm, sv = sum_and_adamw(exchange_all(pack_flat(small_g)),
                                   pack_flat([args[n] for n in SMALL]),
                                   pack_flat([args["m_" + n] for n in SMALL]),
                                   pack_flat([args["v_" + n] for n in SMALL]))
    for res, packed in ((out_g, sg), (out_d, sd), (out_m, sm), (out_v, sv)):
        res.update(zip(SMALL, unpack_flat(packed, small_shapes)))

    order = ("norm_mix", "w_in", "b_forget", "ssm_lambda_re", "ssm_lambda_im", "ssm_log_dt", "ssm_b_re",
             "ssm_b_im", "ssm_c_re", "ssm_c_im", "ssm_d", "w_glu", "b_glu", "w_branch_a", "w_branch_b", "w_out",
             "norm_mlp", "w_mlp_up", "w_mlp_down", "norm_final")
    return (loss, grad_x, *[out_g[n] for n in order], *[out_d[n] for n in order],
            *[out_m[n] for n in order], *[out_v[n] for n in order])
```

```python
import math

import jax
import jax.numpy as jnp
from jax import lax
from jax.experimental import pallas as pl
from jax.experimental.pallas import tpu as pltpu

F32 = jnp.float32
BF16 = jnp.bfloat16
MESH = pl.DeviceIdType.MESH
ANY = pl.BlockSpec(memory_space=pl.ANY)

ATTN_HEADS = 8
HEAD_DIM = 64
ATTN_WIDTH = ATTN_HEADS * HEAD_DIM
HEAD_PAIRS = ATTN_HEADS // 2
SSM_GROUPS = 32
SSM_GROUP_CH = 16
SSM_STATE = 64
SSM_WIDTH = SSM_GROUPS * SSM_GROUP_CH
LANES = 128
SUBLANES = 8
SSM_CHUNKS = SSM_WIDTH // LANES
CHUNK_GROUPS = SSM_GROUPS // SSM_CHUNKS
CHUNK_STATES = CHUNK_GROUPS * SSM_STATE
CHUNK_LANES = 2 * CHUNK_STATES
STATE_LANES = SSM_CHUNKS * CHUNK_LANES
F_PAD = LANES
RMS_EPS = 1e-6
ADAM_LR = 0.001
ADAM_B1 = 0.9
ADAM_B2 = 0.999
ADAM_EPS = 1e-08
ADAM_WD = 0.01
ADAM_STEP = 10
PACK_COLS = 1024
PACK_ROW_TILE = 256
VMEM_LIMIT = 52 * 1024 * 1024
NEG_BIG = -1e30
GELU_C = math.sqrt(2.0 / math.pi)
GELU_A = 0.044715

NN = (((1,), (0,)), ((), ()))
NT = (((1,), (1,)), ((), ()))
TN = (((0,), (0,)), ((), ()))


def _pick(n, pref):
    if n <= pref:
        return n
    best = LANES
    for t in range(LANES, pref + 1, LANES):
        if n % t == 0:
            best = t
    assert n % best == 0, (n, pref)
    return best


def _rows(n, pref):
    t = min(n, pref)
    while n % t:
        t //= 2
    assert t % 16 == 0 or t == n, (n, pref)
    return t


def _params(sem):
    return pltpu.CompilerParams(dimension_semantics=sem, vmem_limit_bytes=VMEM_LIMIT)


def _fold8(v):
    r, c = v.shape
    return jnp.sum(v.reshape(r // SUBLANES, SUBLANES, c), axis=0)


def _dot(a, b, dims=None):
    if dims is None:
        return jnp.dot(a, b, preferred_element_type=F32)
    return lax.dot_general(a, b, dims, preferred_element_type=F32)


def _dot_exact(a, b, dims):
    return lax.dot_general(a, b, dims, preferred_element_type=F32, precision=lax.Precision.HIGHEST)


def _sigmoid(v):
    return 1.0 / (1.0 + jnp.exp(-v))


def _rms_scale(x):
    return lax.rsqrt(jnp.mean(x * x, axis=-1, keepdims=True) + RMS_EPS)


def _rms_bwd(x, g, dh):
    r = _rms_scale(x)
    xn = x * r
    dxn = dh * g
    dx = r * (dxn - xn * jnp.mean(dxn * xn, axis=-1, keepdims=True))
    return dx, dh * xn


def norm_matmul(x, g, w, out_dtype, name):
    t, d = x.shape
    m = w.shape[1]
    tm, tn = _rows(t, 512), _pick(m, 1024)

    def body(x_ref, g_ref, w_ref, o_ref):
        xv = x_ref[...]
        h = (xv * _rms_scale(xv) * g_ref[...]).astype(BF16)
        o_ref[...] = _dot(h, w_ref[...]).astype(o_ref.dtype)

    return pl.pallas_call(
        body, name=name, grid=(t // tm, m // tn),
        in_specs=[pl.BlockSpec((tm, d), lambda i, j: (i, 0)),
                  pl.BlockSpec((1, d), lambda i, j: (0, 0)),
                  pl.BlockSpec((d, tn), lambda i, j: (0, j))],
        out_specs=pl.BlockSpec((tm, tn), lambda i, j: (i, j)),
        out_shape=jax.ShapeDtypeStruct((t, m), out_dtype),
        compiler_params=_params(("parallel", "arbitrary")),
    )(x, g, w)


def matmul_tn(a, b, name, a_kind="plain", gain=None, tm_pref=1024, tn_pref=1536, tk_pref=512):
    t, ma = a.shape
    nb = b.shape[1]
    tm = ma if a_kind == "norm" else _pick(ma, tm_pref)
    tn = _pick(nb, tn_pref)
    tk = _rows(t, tk_pref)
    nk = t // tk

    def body(*refs):
        if a_kind == "norm":
            a_ref, g_ref, b_ref, o_ref, acc = refs
        else:
            a_ref, b_ref, o_ref, acc = refs
        k = pl.program_id(2)

        @pl.when(k == 0)
        def _():
            acc[...] = jnp.zeros_like(acc)

        av = a_ref[...]
        if a_kind == "norm":
            av = av * _rms_scale(av) * g_ref[...]
        elif a_kind == "relu2":
            av = jnp.square(jnp.maximum(av.astype(F32), 0.0))
        acc[...] += _dot(av.astype(BF16), b_ref[...].astype(BF16), TN)

        @pl.when(k == nk - 1)
        def _():
            o_ref[...] = acc[...]

    in_specs = [pl.BlockSpec((tk, tm), lambda i, j, k: (k, i))]
    args = [a]
    if a_kind == "norm":
        in_specs.append(pl.BlockSpec((1, ma), lambda i, j, k: (0, 0)))
        args.append(gain)
    in_specs.append(pl.BlockSpec((tk, tn), lambda i, j, k: (k, j)))
    args.append(b)
    return pl.pallas_call(
        body, name=name, grid=(ma // tm, nb // tn, nk),
        in_specs=in_specs,
        out_specs=pl.BlockSpec((tm, tn), lambda i, j, k: (i, j)),
        out_shape=jax.ShapeDtypeStruct((ma, nb), F32),
        scratch_shapes=[pltpu.VMEM((tm, tn), F32)],
        compiler_params=_params(("parallel", "parallel", "arbitrary")),
    )(*args)


def _tri(n, upper):
    r = lax.broadcasted_iota(jnp.int32, (n, n), 0)
    c = lax.broadcasted_iota(jnp.int32, (n, n), 1)
    return jnp.where((c >= r) if upper else (c <= r), 1.0, 0.0).astype(F32)


def _head_rows():
    r = lax.broadcasted_iota(jnp.int32, (SUBLANES, LANES), 0)
    c = lax.broadcasted_iota(jnp.int32, (SUBLANES, LANES), 1)
    return jnp.where(r == c, 1.0, 0.0).astype(F32)


def forget_cumsum(rest, bf, bsz, seq, f_blk):
    tc = _rows(seq, 512)
    nc = seq // tc

    def body(f_ref, b_ref, col_ref, row_ref, carry):
        c = pl.program_id(1)

        @pl.when(c == 0)
        def _():
            carry[...] = jnp.zeros_like(carry)

        z = f_ref[...] + b_ref[...]
        logf = jnp.minimum(z, 0.0) - jnp.log(1.0 + jnp.exp(-jnp.abs(z)))
        cum = _dot_exact(_tri(tc, False), logf, NN) + carry[0:1, :]
        col_ref[0] = cum
        row_ref[0] = _dot_exact(_head_rows(), cum, NT)
        carry[...] = jnp.broadcast_to(cum[tc - 1:tc, :], carry.shape)

    return pl.pallas_call(
        body, name="forget_cumsum", grid=(bsz, nc),
        in_specs=[pl.BlockSpec((tc, F_PAD), lambda b, c: (b * nc + c, f_blk)),
                  pl.BlockSpec((1, F_PAD), lambda b, c: (0, 0))],
        out_specs=[pl.BlockSpec((1, tc, LANES), lambda b, c: (b, c, 0)),
                   pl.BlockSpec((1, SUBLANES, tc), lambda b, c: (b, 0, c))],
        out_shape=[jax.ShapeDtypeStruct((bsz, seq, LANES), F32),
                   jax.ShapeDtypeStruct((bsz, SUBLANES, seq), F32)],
        scratch_shapes=[pltpu.VMEM((SUBLANES, LANES), F32)],
        compiler_params=_params(("parallel", "arbitrary")),
    )(rest, bf)


def forget_bwd(dcq, dcp, rest, bf, bsz, seq, f_blk):
    tc = _attn_tile(seq)
    nc = seq // tc

    def body(dq_ref, dc_ref, f_ref, b_ref, df_ref, db_ref, carry):
        b = pl.program_id(0)
        c = pl.program_id(1)

        @pl.when(c == 0)
        def _():
            carry[...] = jnp.zeros_like(carry)

        @pl.when((b == 0) & (c == 0))
        def _():
            db_ref[...] = jnp.zeros_like(db_ref)

        row = lax.broadcasted_iota(jnp.int32, (SUBLANES, tc), 0)
        heads = jnp.zeros((SUBLANES, tc), F32)
        dc = jnp.zeros((tc, LANES), F32)
        for p in range(HEAD_PAIRS):
            blk = dq_ref[0, p, 0]
            heads = heads + jnp.where(row == 2 * p, blk[0:1], 0.0) + jnp.where(row == 2 * p + 1, blk[1:2], 0.0)
            dc = dc + dc_ref[0, p]
        dc = dc + jnp.concatenate([heads, jnp.zeros((LANES - SUBLANES, tc), F32)], axis=0).T
        dlogf = _dot_exact(_tri(tc, True), dc, NN) + carry[0:1, :]
        carry[...] = jnp.broadcast_to(dlogf[0:1, :], carry.shape)
        z = f_ref[...] + b_ref[...]
        lane = lax.broadcasted_iota(jnp.int32, z.shape, 1)
        df = jnp.where(lane < ATTN_HEADS, dlogf * _sigmoid(-z), 0.0)
        df_ref[...] = df.astype(df_ref.dtype)
        db_ref[...] += _fold8(df)

    return pl.pallas_call(
        body, name="forget_bwd", grid=(bsz, nc),
        in_specs=[pl.BlockSpec((1, HEAD_PAIRS, 1, SUBLANES, tc), lambda b, c: (b, 0, nc - 1 - c, 0, 0)),
                  pl.BlockSpec((1, HEAD_PAIRS, tc, LANES), lambda b, c: (b, 0, nc - 1 - c, 0)),
                  pl.BlockSpec((tc, F_PAD), lambda b, c: (b * nc + nc - 1 - c, f_blk)),
                  pl.BlockSpec((1, F_PAD), lambda b, c: (0, 0))],
        out_specs=[pl.BlockSpec((tc, F_PAD), lambda b, c: (b * nc + nc - 1 - c, 0)),
                   pl.BlockSpec((SUBLANES, F_PAD), lambda b, c: (0, 0))],
        out_shape=[jax.ShapeDtypeStruct((bsz * seq, F_PAD), BF16),
                   jax.ShapeDtypeStruct((SUBLANES, F_PAD), F32)],
        scratch_shapes=[pltpu.VMEM((SUBLANES, LANES), F32)],
        compiler_params=_params(("arbitrary", "arbitrary")),
    )(dcq, dcp, rest, bf)


def _attn_tile(seq):
    return 512 if seq >= 2048 else 128


def _lane_head(shape, par):
    lane = lax.broadcasted_iota(jnp.int32, shape, len(shape) - 1)
    return (lane >= HEAD_DIM) if par else (lane < HEAD_DIM)


def _pick_lane(block, idx):
    lane = lax.broadcasted_iota(jnp.int32, block.shape, 1)
    return jnp.sum(jnp.where(lane == idx, block, 0.0), axis=1, keepdims=True)


def _pair_rows(lo_lane, hi_lane):
    r = lax.broadcasted_iota(jnp.int32, (SUBLANES, LANES), 0)
    c = lax.broadcasted_iota(jnp.int32, (SUBLANES, LANES), 1)
    if lo_lane is None:
        sel = ((r == 0) & (c < HEAD_DIM)) | ((r == 1) & (c >= HEAD_DIM))
    else:
        sel = ((r == 0) & (c == lo_lane)) | ((r == 1) & (c == hi_lane))
    return jnp.where(sel, 1.0, 0.0).astype(F32)


def _causal(s, transposed):
    r = lax.broadcasted_iota(jnp.int32, s.shape, 0)
    c = lax.broadcasted_iota(jnp.int32, s.shape, 1)
    return jnp.where((c >= r) if transposed else (r >= c), s, NEG_BIG)


def fox_fwd(qkv, cumrow, bsz, seq):
    tq = _attn_tile(seq)
    nq = seq // tq
    scale = HEAD_DIM ** -0.5
    kb, vb = ATTN_WIDTH // LANES, 2 * ATTN_WIDTH // LANES

    def body(q_ref, k_ref, v_ref, cr_ref, o_ref, lr_ref, m_s, l_s, acc_s):
        hp = pl.program_id(1)
        qi = pl.program_id(2)
        kj = pl.program_id(3)

        @pl.when(kj == 0)
        def _():
            m_s[...] = jnp.full_like(m_s, NEG_BIG)
            l_s[...] = jnp.zeros_like(l_s)
            acc_s[...] = jnp.zeros_like(acc_s)

        def step(diag):
            q = q_ref[...]
            k = k_ref[...]
            v = v_ref[...]
            ones = jnp.ones((tq, LANES), BF16)
            for par in range(2):
                qh = jnp.where(_lane_head(q.shape, par), q, 0.0) * scale
                s = _dot(qh.astype(BF16), k, NT) - cr_ref[0, pl.ds(2 * hp + par, 1), :]
                if diag:
                    s = _causal(s, False)
                m_prev = m_s[par]
                m_new = jnp.maximum(m_prev, jnp.max(s, axis=1, keepdims=True))
                alpha = jnp.exp(m_prev - m_new)
                p = jnp.exp(s - m_new).astype(BF16)
                l_s[par] = alpha * l_s[par] + _dot(p, ones)
                acc_s[par] = alpha * acc_s[par] + _dot(p, v)
                m_s[par] = m_new

        @pl.when(kj < qi)
        def _():
            step(False)

        @pl.when(kj == qi)
        def _():
            step(True)
            lo = _lane_head((tq, LANES), 0)
            out = jnp.where(lo, acc_s[0] / l_s[0], acc_s[1] / l_s[1])
            o_ref[...] = out.astype(o_ref.dtype)
            lse = jnp.where(lo, m_s[0] + jnp.log(l_s[0]), m_s[1] + jnp.log(l_s[1]))
            lr_ref[0, 0] = _dot_exact(_pair_rows(0, HEAD_DIM), lse, NT)

    return pl.pallas_call(
        body, name="fox_fwd", grid=(bsz, HEAD_PAIRS, nq, nq),
        in_specs=[pl.BlockSpec((tq, LANES), lambda b, h, i, j: (b * nq + i, h)),
                  pl.BlockSpec((tq, LANES), lambda b, h, i, j: (b * nq + jnp.minimum(i, j), kb + h)),
                  pl.BlockSpec((tq, LANES), lambda b, h, i, j: (b * nq + jnp.minimum(i, j), vb + h)),
                  pl.BlockSpec((1, SUBLANES, tq), lambda b, h, i, j: (b, 0, jnp.minimum(i, j)))],
        out_specs=[pl.BlockSpec((tq, LANES), lambda b, h, i, j: (b * nq + i, h)),
                   pl.BlockSpec((1, 1, SUBLANES, tq), lambda b, h, i, j: (b, h, 0, i))],
        out_shape=[jax.ShapeDtypeStruct((bsz * seq, ATTN_WIDTH), BF16),
                   jax.ShapeDtypeStruct((bsz, HEAD_PAIRS, SUBLANES, seq), F32)],
        scratch_shapes=[pltpu.VMEM((2, tq, 1), F32), pltpu.VMEM((2, tq, LANES), F32),
                        pltpu.VMEM((2, tq, LANES), F32)],
        compiler_params=_params(("parallel", "parallel", "parallel", "arbitrary")),
    )(qkv, qkv, qkv, cumrow)


def fox_bwd_kernel(qkv, do, o, lser, cumcol, bsz, seq):
    tk = _attn_tile(seq)
    nk = seq // tk
    scale = HEAD_DIM ** -0.5
    kb, vb = ATTN_WIDTH // LANES, 2 * ATTN_WIDTH // LANES

    def body(q_ref, k_ref, v_ref, do_ref, o_ref, lr_ref, cc_ref, dq_ref, dk_ref, dv_ref, dc_ref, dr_ref,
             dq_s, dk_s, dv_s, dc_s, dr_s):
        hp = pl.program_id(1)
        kj = pl.program_id(2)
        qi = pl.program_id(3)

        @pl.when((kj == 0) & (qi == 0))
        def _():
            dq_s[...] = jnp.zeros_like(dq_s)
            dr_s[...] = jnp.zeros_like(dr_s)

        @pl.when(qi == 0)
        def _():
            dk_s[...] = jnp.zeros_like(dk_s)
            dv_s[...] = jnp.zeros_like(dv_s)
            dc_s[...] = jnp.zeros_like(dc_s)

        def step(diag):
            q = q_ref[...]
            k = k_ref[...]
            v = v_ref[...]
            dov = do_ref[...]
            prod = dov.astype(F32) * o_ref[...].astype(F32)
            drow = _dot_exact(_pair_rows(None, None), prod, NT)
            lrow = lr_ref[0, 0]
            lane = lax.broadcasted_iota(jnp.int32, (tk, LANES), 1)
            for par in range(2):
                head = 2 * hp + par
                sel = _lane_head(k.shape, par)
                kh = (jnp.where(sel, k, 0.0) * scale).astype(BF16)
                st = _dot(kh, q, NT) - _pick_lane(cc_ref[0], head)
                if diag:
                    st = _causal(st, True)
                pt = jnp.exp(st - lrow[par:par + 1, :])
                vh = jnp.where(sel, v, 0.0)
                dpt = _dot(vh.astype(BF16), dov, NT)
                dst = pt * (dpt - drow[par:par + 1, :])
                dsb = dst.astype(BF16)
                dv_s[...] += jnp.where(sel, _dot(pt.astype(BF16), dov), 0.0)
                dk_s[...] += jnp.where(sel, _dot(dsb, q), 0.0)
                dq_s[qi] += _dot(dsb, kh, TN)
                dc_s[...] += jnp.where(lane == head, -jnp.sum(dst, axis=1, keepdims=True), 0.0)
                dr_s[qi, par:par + 1, :] += jnp.sum(dst, axis=0, keepdims=True)

        @pl.when(qi > kj)
        def _():
            step(False)

        @pl.when(qi == kj)
        def _():
            step(True)

        @pl.when(qi == nk - 1)
        def _():
            dk_ref[...] = (dk_s[...] * scale).astype(dk_ref.dtype)
            dv_ref[...] = dv_s[...].astype(dv_ref.dtype)
            dc_ref[0, 0] = dc_s[...]

        @pl.when((kj == nk - 1) & (qi == nk - 1))
        def _():
            for i in range(nk):
                dq_ref[i * tk:(i + 1) * tk, :] = dq_s[i].astype(dq_ref.dtype)
            dr_ref[0, 0] = dr_s[...]

    return pl.pallas_call(
        body, name="fox_bwd", grid=(bsz, HEAD_PAIRS, nk, nk),
        in_specs=[pl.BlockSpec((tk, LANES), lambda b, h, j, i: (b * nk + jnp.maximum(i, j), h)),
                  pl.BlockSpec((tk, LANES), lambda b, h, j, i: (b * nk + j, kb + h)),
                  pl.BlockSpec((tk, LANES), lambda b, h, j, i: (b * nk + j, vb + h)),
                  pl.BlockSpec((tk, LANES), lambda b, h, j, i: (b * nk + jnp.maximum(i, j), h)),
                  pl.BlockSpec((tk, LANES), lambda b, h, j, i: (b * nk + jnp.maximum(i, j), h)),
                  pl.BlockSpec((1, 1, SUBLANES, tk), lambda b, h, j, i: (b, h, 0, jnp.maximum(i, j))),
                  pl.BlockSpec((1, tk, LANES), lambda b, h, j, i: (b, j, 0))],
        out_specs=[pl.BlockSpec((seq, LANES), lambda b, h, j, i: (b, h)),
                   pl.BlockSpec((tk, LANES), lambda b, h, j, i: (b * nk + j, h)),
                   pl.BlockSpec((tk, LANES), lambda b, h, j, i: (b * nk + j, h)),
                   pl.BlockSpec((1, 1, tk, LANES), lambda b, h, j, i: (b, h, j, 0)),
                   pl.BlockSpec((1, 1, nk, SUBLANES, tk), lambda b, h, j, i: (b, h, 0, 0, 0))],
        out_shape=[jax.ShapeDtypeStruct((bsz * seq, ATTN_WIDTH), BF16),
                   jax.ShapeDtypeStruct((bsz * seq, ATTN_WIDTH), BF16),
                   jax.ShapeDtypeStruct((bsz * seq, ATTN_WIDTH), BF16),
                   jax.ShapeDtypeStruct((bsz, HEAD_PAIRS, seq, LANES), F32),
                   jax.ShapeDtypeStruct((bsz, HEAD_PAIRS, nk, SUBLANES, tk), F32)],
        scratch_shapes=[pltpu.VMEM((nk, tk, LANES), F32), pltpu.VMEM((tk, LANES), F32), pltpu.VMEM((tk, LANES), F32),
                        pltpu.VMEM((tk, LANES), F32), pltpu.VMEM((nk, SUBLANES, tk), F32)],
        compiler_params=_params(("parallel", "parallel", "arbitrary", "arbitrary")),
    )(qkv, qkv, qkv, do, o, lser, cumcol)


def fox_bwd(qkv, do, o, lser, cumcol, rest, bf, bsz, seq, f_blk):
    dq, dk, dv, dcp, dcq = fox_bwd_kernel(qkv, do, o, lser, cumcol, bsz, seq)
    df, dbf = forget_bwd(dcq, dcp, rest, bf, bsz, seq, f_blk)
    return dq, dk, dv, df, dbf


TAB_STEP = 0
TAB_FWD = 8
TAB_BWD = 16
TAB_ROWS = 24


def _ssm_tile(seq):
    return 256 if seq >= 1024 else 64


def _scan_block(xr, xi, tab_ref, re, im, cr, ci, reverse):
    row = lax.broadcasted_iota(jnp.int32, xr.shape, 0)
    sign = -1.0 if reverse else 1.0
    for n, s in enumerate((1, 2, 4)):
        ar = tab_ref[TAB_STEP + n:TAB_STEP + n + 1, re]
        ai = tab_ref[TAB_STEP + n:TAB_STEP + n + 1, im] * sign
        if reverse:
            keep = row < SUBLANES - s
            sr = jnp.where(keep, pltpu.roll(xr, SUBLANES - s, 0), 0.0)
            si = jnp.where(keep, pltpu.roll(xi, SUBLANES - s, 0), 0.0)
        else:
            keep = row >= s
            sr = jnp.where(keep, pltpu.roll(xr, s, 0), 0.0)
            si = jnp.where(keep, pltpu.roll(xi, s, 0), 0.0)
        xr, xi = xr + ar * sr - ai * si, xi + ar * si + ai * sr
    base = TAB_BWD if reverse else TAB_FWD
    pr = tab_ref[base:base + SUBLANES, re]
    pi = tab_ref[base:base + SUBLANES, im] * sign
    xr, xi = xr + pr * cr - pi * ci, xi + pr * ci + pi * cr
    return xr, xi


def ssm_fwd(rest, wb4, wc4, tabs, dskip, bsz, seq, u_blk):
    tt = _ssm_tile(seq)
    nt = seq // tt

    def body(u_ref, wb_ref, wc_ref, tab_ref, d_ref, y_ref, h_ref, carry):
        c = pl.program_id(1)

        @pl.when(c == 0)
        def _():
            carry[...] = jnp.zeros_like(carry)

        u = u_ref[...]
        ub = u.astype(BF16)
        for j in range(SSM_CHUNKS):
            h_ref[:, j * CHUNK_LANES:(j + 1) * CHUNK_LANES] = _dot(ub[:, j * LANES:(j + 1) * LANES], wb_ref[j])
        for j in range(SSM_CHUNKS):
            re = slice(j * CHUNK_LANES, j * CHUNK_LANES + CHUNK_STATES)
            im = slice(j * CHUNK_LANES + CHUNK_STATES, (j + 1) * CHUNK_LANES)

            def blk(bi, car):
                r0 = pl.multiple_of(bi * SUBLANES, SUBLANES)
                xr, xi = _scan_block(h_ref[pl.ds(r0, SUBLANES), re], h_ref[pl.ds(r0, SUBLANES), im],
                                     tab_ref, re, im, car[0], car[1], False)
                h_ref[pl.ds(r0, SUBLANES), re] = xr
                h_ref[pl.ds(r0, SUBLANES), im] = xi
                return xr[SUBLANES - 1:SUBLANES], xi[SUBLANES - 1:SUBLANES]

            cr, ci = lax.fori_loop(0, tt // SUBLANES, blk, (carry[0:1, re], carry[0:1, im]))
            carry[0:1, re] = cr
            carry[0:1, im] = ci
        for j in range(SSM_CHUNKS):
            hj = h_ref[:, j * CHUNK_LANES:(j + 1) * CHUNK_LANES].astype(BF16)
            cols = slice(j * LANES, (j + 1) * LANES)
            y_ref[:, cols] = _dot(hj, wc_ref[j]) + d_ref[:, cols] * u[:, cols]

    return pl.pallas_call(
        body, name="ssm_fwd", grid=(bsz, nt),
        in_specs=[pl.BlockSpec((tt, SSM_WIDTH), lambda b, c: (b * nt + c, u_blk)),
                  pl.BlockSpec((SSM_CHUNKS, LANES, CHUNK_LANES), lambda b, c: (0, 0, 0)),
                  pl.BlockSpec((SSM_CHUNKS, CHUNK_LANES, LANES), lambda b, c: (0, 0, 0)),
                  pl.BlockSpec((TAB_ROWS, STATE_LANES), lambda b, c: (0, 0)),
                  pl.BlockSpec((1, SSM_WIDTH), lambda b, c: (0, 0))],
        out_specs=[pl.BlockSpec((tt, SSM_WIDTH), lambda b, c: (b * nt + c, 0)),
                   pl.BlockSpec((tt, STATE_LANES), lambda b, c: (b * nt + c, 0))],
        out_shape=[jax.ShapeDtypeStruct((bsz * seq, SSM_WIDTH), F32),
                   jax.ShapeDtypeStruct((bsz * seq, STATE_LANES), F32)],
        scratch_shapes=[pltpu.VMEM((SUBLANES, STATE_LANES), F32)],
        compiler_params=_params(("parallel", "arbitrary")),
    )(rest, wb4, wc4, tabs, dskip)


def ssm_bwd(dys, rest, hs, wb4, wc4, tabs, dskip, bsz, seq, u_blk):
    tt = _ssm_tile(seq)
    nt = seq // tt
    nb = tt // SUBLANES

    def body(dy_ref, u_ref, h_ref, hp_ref, wb_ref, wc_ref, tab_ref, d_ref,
             du_ref, ga_ref, gwb_ref, gwc_ref, gd_ref, g_s, carry):
        b = pl.program_id(0)
        c = pl.program_id(1)

        @pl.when(c == 0)
        def _():
            carry[...] = jnp.zeros_like(carry)

        @pl.when((b == 0) & (c == 0))
        def _():
            ga_ref[...] = jnp.zeros_like(ga_ref)
            gwb_ref[...] = jnp.zeros_like(gwb_ref)
            gwc_ref[...] = jnp.zeros_like(gwc_ref)
            gd_ref[...] = jnp.zeros_like(gd_ref)

        dy = dy_ref[...].astype(F32)
        dyb = dy.astype(BF16)
        u = u_ref[...]
        ub = u.astype(BF16)
        first_chunk = c == nt - 1
        for j in range(SSM_CHUNKS):
            g_s[:, j * CHUNK_LANES:(j + 1) * CHUNK_LANES] = _dot(dyb[:, j * LANES:(j + 1) * LANES], wc_ref[j], NT)
        for j in range(SSM_CHUNKS):
            re = slice(j * CHUNK_LANES, j * CHUNK_LANES + CHUNK_STATES)
            im = slice(j * CHUNK_LANES + CHUNK_STATES, (j + 1) * CHUNK_LANES)
            row = lax.broadcasted_iota(jnp.int32, (SUBLANES, CHUNK_STATES), 0)

            def blk(n, car):
                bi = nb - 1 - n
                r0 = pl.multiple_of(bi * SUBLANES, SUBLANES)
                gr, gi = _scan_block(g_s[pl.ds(r0, SUBLANES), re], g_s[pl.ds(r0, SUBLANES), im],
                                     tab_ref, re, im, car[0], car[1], True)
                g_s[pl.ds(r0, SUBLANES), re] = gr
                g_s[pl.ds(r0, SUBLANES), im] = gi
                rp = pl.multiple_of(jnp.maximum(bi - 1, 0) * SUBLANES, SUBLANES)
                inside = bi > 0
                live = jnp.where(jnp.logical_or(inside, jnp.logical_not(first_chunk)), 1.0, 0.0)
                pr = jnp.where(inside, h_ref[pl.ds(rp, SUBLANES), re], hp_ref[:, re])[SUBLANES - 1:SUBLANES] * live
                pi = jnp.where(inside, h_ref[pl.ds(rp, SUBLANES), im], hp_ref[:, im])[SUBLANES - 1:SUBLANES] * live
                hr = jnp.where(row >= 1, pltpu.roll(h_ref[pl.ds(r0, SUBLANES), re], 1, 0), pr)
                hi = jnp.where(row >= 1, pltpu.roll(h_ref[pl.ds(r0, SUBLANES), im], 1, 0), pi)
                return (gr[0:1], gi[0:1], car[2] + gr * hr + gi * hi, car[3] + gi * hr - gr * hi)

            zero = jnp.zeros((SUBLANES, CHUNK_STATES), F32)
            cr, ci, sr, si = lax.fori_loop(0, nb, blk, (carry[0:1, re], carry[0:1, im], zero, zero))
            carry[0:1, re] = cr
            carry[0:1, im] = ci
            ga_ref[:, re] += sr
            ga_ref[:, im] += si
        for j in range(SSM_CHUNKS):
            cols = slice(j * LANES, (j + 1) * LANES)
            lanes = slice(j * CHUNK_LANES, (j + 1) * CHUNK_LANES)
            gj = g_s[:, lanes].astype(BF16)
            du_ref[:, cols] = (_dot(gj, wb_ref[j], NT) + d_ref[:, cols] * dy[:, cols]).astype(du_ref.dtype)
            gwb_ref[j] += _dot(ub[:, cols], gj, TN)
            gwc_ref[j] += _dot(h_ref[:, lanes].astype(BF16), dyb[:, cols], TN)
        gd_ref[...] += _fold8(dy * u)

    def prev_rows(b, c):
        chunk = nt - 1 - c
        return (jnp.maximum((b * nt + chunk) * nb - 1, 0), 0)

    return pl.pallas_call(
        body, name="ssm_bwd", grid=(bsz, nt),
        in_specs=[pl.BlockSpec((tt, SSM_WIDTH), lambda b, c: (b * nt + nt - 1 - c, 0)),
                  pl.BlockSpec((tt, SSM_WIDTH), lambda b, c: (b * nt + nt - 1 - c, u_blk)),
                  pl.BlockSpec((tt, STATE_LANES), lambda b, c: (b * nt + nt - 1 - c, 0)),
                  pl.BlockSpec((SUBLANES, STATE_LANES), prev_rows),
                  pl.BlockSpec((SSM_CHUNKS, LANES, CHUNK_LANES), lambda b, c: (0, 0, 0)),
                  pl.BlockSpec((SSM_CHUNKS, CHUNK_LANES, LANES), lambda b, c: (0, 0, 0)),
                  pl.BlockSpec((TAB_ROWS, STATE_LANES), lambda b, c: (0, 0)),
                  pl.BlockSpec((1, SSM_WIDTH), lambda b, c: (0, 0))],
        out_specs=[pl.BlockSpec((tt, SSM_WIDTH), lambda b, c: (b * nt + nt - 1 - c, 0)),
                   pl.BlockSpec((SUBLANES, STATE_LANES), lambda b, c: (0, 0)),
                   pl.BlockSpec((SSM_CHUNKS, LANES, CHUNK_LANES), lambda b, c: (0, 0, 0)),
                   pl.BlockSpec((SSM_CHUNKS, CHUNK_LANES, LANES), lambda b, c: (0, 0, 0)),
                   pl.BlockSpec((SUBLANES, SSM_WIDTH), lambda b, c: (0, 0))],
        out_shape=[jax.ShapeDtypeStruct((bsz * seq, SSM_WIDTH), BF16),
                   jax.ShapeDtypeStruct((SUBLANES, STATE_LANES), F32),
                   jax.ShapeDtypeStruct((SSM_CHUNKS, LANES, CHUNK_LANES), F32),
                   jax.ShapeDtypeStruct((SSM_CHUNKS, CHUNK_LANES, LANES), F32),
                   jax.ShapeDtypeStruct((SUBLANES, SSM_WIDTH), F32)],
        scratch_shapes=[pltpu.VMEM((tt, STATE_LANES), F32), pltpu.VMEM((SUBLANES, STATE_LANES), F32)],
        compiler_params=_params(("arbitrary", "arbitrary")),
    )(dys, rest, hs, hs, wb4, wc4, tabs, dskip)


def _gelu(v):
    t = jnp.tanh(GELU_C * (v + GELU_A * v * v * v))
    return 0.5 * v * (1.0 + t), t


def mix_fwd(ya, ys, rest, x0, wglu, bglu, wba, wbb, wout):
    t, d = x0.shape
    tm = _rows(t, 256)

    def body(ya_ref, ys_ref, ga_ref, gb_ref, x_ref, wg_ref, bg_ref, wa_ref, wb_ref, wo_ref,
             x1_ref, z_ref, pa_ref, pb_ref, yb_ref, yb2_ref, mx_ref):
        yb, _ = _gelu(ys_ref[...])
        ybb = yb.astype(BF16)
        z = _dot(ybb, wg_ref[...]) + bg_ref[...]
        yb2 = (yb * _sigmoid(z)).astype(BF16)
        pa = _dot(ya_ref[...], wa_ref[...])
        pb = _dot(yb2, wb_ref[...])
        mixed = (_sigmoid(ga_ref[...]) * pa + _sigmoid(gb_ref[...]) * pb).astype(BF16)
        x1_ref[...] = x_ref[...] + _dot(mixed, wo_ref[...])
        z_ref[...] = z.astype(z_ref.dtype)
        pa_ref[...] = pa.astype(pa_ref.dtype)
        pb_ref[...] = pb.astype(pb_ref.dtype)
        yb_ref[...] = ybb
        yb2_ref[...] = yb2
        mx_ref[...] = mixed

    row = lambda w: pl.BlockSpec((tm, w), lambda i: (i, 0))
    full = lambda a: pl.BlockSpec(a.shape, lambda i: (0,) * a.ndim)
    return pl.pallas_call(
        body, name="mix_fwd", grid=(t // tm,),
        in_specs=[row(ATTN_WIDTH), row(SSM_WIDTH),
                  pl.BlockSpec((tm, d), lambda i: (i, 0)), pl.BlockSpec((tm, d), lambda i: (i, 1)),
                  row(d), full(wglu), full(bglu), full(wba), full(wbb), full(wout)],
        out_specs=[row(d), row(SSM_WIDTH), row(d), row(d), row(SSM_WIDTH), row(SSM_WIDTH), row(d)],
        out_shape=[jax.ShapeDtypeStruct((t, d), F32), jax.ShapeDtypeStruct((t, SSM_WIDTH), BF16),
                   jax.ShapeDtypeStruct((t, d), BF16), jax.ShapeDtypeStruct((t, d), BF16),
                   jax.ShapeDtypeStruct((t, SSM_WIDTH), BF16), jax.ShapeDtypeStruct((t, SSM_WIDTH), BF16),
                   jax.ShapeDtypeStruct((t, d), BF16)],
        compiler_params=_params(("parallel",)),
    )(ya, ys, rest, rest, x0, wglu, bglu, wba, wbb, wout)


def mix_bwd(dx1, rest, pa, pb, z, ys, wglu, wba, wbb, wout):
    t, d = dx1.shape
    tm = _rows(t, 256)

    def body(dx_ref, ga_ref, gb_ref, pa_ref, pb_ref, z_ref, ys_ref, wg_ref, wa_ref, wb_ref, wo_ref,
             dya_ref, dys_ref, dg_ref, dpa_ref, dpb_ref, dz_ref, dbg_ref):
        @pl.when(pl.program_id(0) == 0)
        def _():
            dbg_ref[...] = jnp.zeros_like(dbg_ref)

        dmix = _dot(dx_ref[...].astype(BF16), wo_ref[...], NT)
        sa = _sigmoid(ga_ref[...])
        sb = _sigmoid(gb_ref[...])
        dpa = (dmix * sa).astype(BF16)
        dpb = (dmix * sb).astype(BF16)
        dg_ref[:, 0:d] = (dmix * pa_ref[...].astype(F32) * sa * (1.0 - sa)).astype(dg_ref.dtype)
        dg_ref[:, d:2 * d] = (dmix * pb_ref[...].astype(F32) * sb * (1.0 - sb)).astype(dg_ref.dtype)
        dpa_ref[...] = dpa
        dpb_ref[...] = dpb
        dya_ref[...] = _dot(dpa, wa_ref[...], NT).astype(dya_ref.dtype)
        dyb2 = _dot(dpb, wb_ref[...], NT)
        ys = ys_ref[...]
        yb, th = _gelu(ys)
        sg = _sigmoid(z_ref[...].astype(F32))
        dz = dyb2 * yb * sg * (1.0 - sg)
        dzb = dz.astype(BF16)
        dz_ref[...] = dzb
        dbg_ref[...] += _fold8(dz)
        dyb = dyb2 * sg + _dot(dzb, wg_ref[...], NT)
        dgelu = 0.5 * (1.0 + th) + 0.5 * ys * (1.0 - th * th) * GELU_C * (1.0 + 3.0 * GELU_A * ys * ys)
        dys_ref[...] = (dyb * dgelu).astype(dys_ref.dtype)

    row = lambda w: pl.BlockSpec((tm, w), lambda i: (i, 0))
    full = lambda a: pl.BlockSpec(a.shape, lambda i: (0,) * a.ndim)
    return pl.pallas_call(
        body, name="mix_bwd", grid=(t // tm,),
        in_specs=[row(d), pl.BlockSpec((tm, d), lambda i: (i, 0)), pl.BlockSpec((tm, d), lambda i: (i, 1)),
                  row(d), row(d), row(SSM_WIDTH), row(SSM_WIDTH), full(wglu), full(wba), full(wbb), full(wout)],
        out_specs=[row(ATTN_WIDTH), row(SSM_WIDTH), row(2 * d), row(d), row(d), row(SSM_WIDTH),
                   pl.BlockSpec((SUBLANES, SSM_WIDTH), lambda i: (0, 0))],
        out_shape=[jax.ShapeDtypeStruct((t, ATTN_WIDTH), BF16), jax.ShapeDtypeStruct((t, SSM_WIDTH), BF16),
                   jax.ShapeDtypeStruct((t, 2 * d), BF16), jax.ShapeDtypeStruct((t, d), BF16),
                   jax.ShapeDtypeStruct((t, d), BF16), jax.ShapeDtypeStruct((t, SSM_WIDTH), BF16),
                   jax.ShapeDtypeStruct((SUBLANES, SSM_WIDTH), F32)],
        compiler_params=_params(("arbitrary",)),
    )(dx1, rest, rest, pa, pb, z, ys, wglu, wba, wbb, wout)


def mlp_fwd(x1, g, wup, wdown):
    t, d = x1.shape
    ff = wup.shape[1]
    tm, tf = _rows(t, 512), _pick(ff, 1024)
    nf = ff // tf

    def body(x_ref, g_ref, wu_ref, wd_ref, x2_ref, up_ref, h_s, acc_s):
        f = pl.program_id(1)

        @pl.when(f == 0)
        def _():
            xv = x_ref[...]
            h_s[...] = (xv * _rms_scale(xv) * g_ref[...]).astype(BF16)
            acc_s[...] = jnp.zeros_like(acc_s)

        up = _dot(h_s[...], wu_ref[...])
        up_ref[...] = up.astype(up_ref.dtype)
        act = jnp.square(jnp.maximum(up, 0.0)).astype(BF16)
        acc_s[...] += _dot(act, wd_ref[...])

        @pl.when(f == nf - 1)
        def _():
            x2_ref[...] = x_ref[...] + acc_s[...]

    return pl.pallas_call(
        body, name="mlp_fwd", grid=(t // tm, nf),
        in_specs=[pl.BlockSpec((tm, d), lambda i, f: (i, 0)), pl.BlockSpec((1, d), lambda i, f: (0, 0)),
                  pl.BlockSpec((d, tf), lambda i, f: (0, f)), pl.BlockSpec((tf, d), lambda i, f: (f, 0))],
        out_specs=[pl.BlockSpec((tm, d), lambda i, f: (i, 0)), pl.BlockSpec((tm, tf), lambda i, f: (i, f))],
        out_shape=[jax.ShapeDtypeStruct((t, d), F32), jax.ShapeDtypeStruct((t, ff), BF16)],
        scratch_shapes=[pltpu.VMEM((tm, d), BF16), pltpu.VMEM((tm, d), F32)],
        compiler_params=_params(("parallel", "arbitrary")),
    )(x1, g, wup, wdown)


def mlp_bwd(dx2, up, x1, g, wup, wdown):
    t, d = x1.shape
    ff = wup.shape[1]
    tm, tf = _rows(t, 512), _pick(ff, 1024)
    nf = ff // tf

    def body(dx_ref, up_ref, x_ref, g_ref, wu_ref, wd_ref, dup_ref, dx1_ref, dg_ref, dxb_s, acc_s):
        i = pl.program_id(0)
        f = pl.program_id(1)

        @pl.when((i == 0) & (f == 0))
        def _():
            dg_ref[...] = jnp.zeros_like(dg_ref)

        @pl.when(f == 0)
        def _():
            dxb_s[...] = dx_ref[...].astype(BF16)
            acc_s[...] = jnp.zeros_like(acc_s)

        dact = _dot(dxb_s[...], wd_ref[...], NT)
        dup = (dact * 2.0 * jnp.maximum(up_ref[...].astype(F32), 0.0)).astype(BF16)
        dup_ref[...] = dup
        acc_s[...] += _dot(dup, wu_ref[...], NT)

        @pl.when(f == nf - 1)
        def _():
            dxn, dgain = _rms_bwd(x_ref[...], g_ref[...], acc_s[...])
            dx1_ref[...] = dx_ref[...] + dxn
            dg_ref[...] += _fold8(dgain)

    return pl.pallas_call(
        body, name="mlp_bwd", grid=(t // tm, nf),
        in_specs=[pl.BlockSpec((tm, d), lambda i, f: (i, 0)), pl.BlockSpec((tm, tf), lambda i, f: (i, f)),
                  pl.BlockSpec((tm, d), lambda i, f: (i, 0)), pl.BlockSpec((1, d), lambda i, f: (0, 0)),
                  pl.BlockSpec((d, tf), lambda i, f: (0, f)), pl.BlockSpec((tf, d), lambda i, f: (f, 0))],
        out_specs=[pl.BlockSpec((tm, tf), lambda i, f: (i, f)), pl.BlockSpec((tm, d), lambda i, f: (i, 0)),
                   pl.BlockSpec((SUBLANES, d), lambda i, f: (0, 0))],
        out_shape=[jax.ShapeDtypeStruct((t, ff), BF16), jax.ShapeDtypeStruct((t, d), F32),
                   jax.ShapeDtypeStruct((SUBLANES, d), F32)],
        scratch_shapes=[pltpu.VMEM((tm, d), BF16), pltpu.VMEM((tm, d), F32)],
        compiler_params=_params(("arbitrary", "arbitrary")),
    )(dx2, up, x1, g, wup, wdown)


def proj_bwd(dproj, wpad, x0, dx1, g):
    t, d = x0.shape
    m = wpad.shape[1]
    tm = _rows(t, 256)

    def body(dp_ref, w_ref, x_ref, dx1_ref, g_ref, dx0_ref, dg_ref):
        @pl.when(pl.program_id(0) == 0)
        def _():
            dg_ref[...] = jnp.zeros_like(dg_ref)

        dh = _dot(dp_ref[...], w_ref[...], NT)
        dxn, dgain = _rms_bwd(x_ref[...], g_ref[...], dh)
        dx0_ref[...] = dx1_ref[...] + dxn
        dg_ref[...] += _fold8(dgain)

    return pl.pallas_call(
        body, name="proj_bwd", grid=(t // tm,),
        in_specs=[pl.BlockSpec((tm, m), lambda i: (i, 0)), pl.BlockSpec((d, m), lambda i: (0, 0)),
                  pl.BlockSpec((tm, d), lambda i: (i, 0)), pl.BlockSpec((tm, d), lambda i: (i, 0)),
                  pl.BlockSpec((1, d), lambda i: (0, 0))],
        out_specs=[pl.BlockSpec((tm, d), lambda i: (i, 0)), pl.BlockSpec((SUBLANES, d), lambda i: (0, 0))],
        out_shape=[jax.ShapeDtypeStruct((t, d), F32), jax.ShapeDtypeStruct((SUBLANES, d), F32)],
        compiler_params=_params(("arbitrary",)),
    )(dproj, wpad, x0, dx1, g)


def final_loss(x, g, target):
    t, d = x.shape
    tm = _rows(t, 512)

    def body(x_ref, g_ref, t_ref, dx_ref, ls_ref, dg_ref):
        @pl.when(pl.program_id(0) == 0)
        def _():
            ls_ref[...] = jnp.zeros_like(ls_ref)
            dg_ref[...] = jnp.zeros_like(dg_ref)

        xv = x_ref[...]
        gv = g_ref[...]
        err = xv * _rms_scale(xv) * gv - t_ref[...]
        ls_ref[...] += _fold8(err * err) * (0.5 / d)
        dxn, dgain = _rms_bwd(xv, gv, err * (1.0 / d))
        dx_ref[...] = dxn
        dg_ref[...] += _fold8(dgain)

    return pl.pallas_call(
        body, name="final_loss", grid=(t // tm,),
        in_specs=[pl.BlockSpec((tm, d), lambda i: (i, 0)), pl.BlockSpec((1, d), lambda i: (0, 0)),
                  pl.BlockSpec((tm, d), lambda i: (i, 0))],
        out_specs=[pl.BlockSpec((tm, d), lambda i: (i, 0)), pl.BlockSpec((SUBLANES, d), lambda i: (0, 0)),
                   pl.BlockSpec((SUBLANES, d), lambda i: (0, 0))],
        out_shape=[jax.ShapeDtypeStruct((t, d), F32), jax.ShapeDtypeStruct((SUBLANES, d), F32),
                   jax.ShapeDtypeStruct((SUBLANES, d), F32)],
        compiler_params=_params(("arbitrary",)),
    )(x, g, target)


def _place():
    x, y, c = lax.axis_index("x"), lax.axis_index("y"), lax.axis_index("c")
    chips = [(1 - x, y), (x, 1 - y), (1 - x, 1 - y)]
    return x, y, c, chips


def _exchange_call(body, name, ins, out_shapes, copies_per_tensor):
    n = len(ins)
    scratch = [pltpu.SemaphoreType.DMA((copies_per_tensor * n,)), pltpu.SemaphoreType.DMA((copies_per_tensor * n,))]
    return pl.pallas_call(
        body, name=name, in_specs=[ANY] * n, out_specs=[ANY] * n, out_shape=out_shapes, scratch_shapes=scratch,
    )(*ins)


def gather_weights(ws):
    n = len(ws)
    lh = ws[0].shape[0] // 2

    def body(*refs):
        w_refs, out_refs = refs[:n], refs[n:2 * n]
        send_sems, recv_sems = refs[2 * n:]
        x, y, c, chips = _place()
        me = 2 * x + y
        sibling = (x, y, 1 - c)

        def half(t, chip_idx, core):
            return out_refs[t].at[chip_idx, pl.ds(core * lh, lh)]

        def copy(t, k, src, dst, to):
            return pltpu.make_async_remote_copy(src_ref=src, dst_ref=dst, send_sem=send_sems.at[6 * t + k],
                                                recv_sem=recv_sems.at[6 * t + k], device_id=to, device_id_type=MESH)

        first = [copy(t, j, w_refs[t].at[pl.ds(c * lh, lh)], half(t, me, c), (*chip, c))
                 for j, chip in enumerate(chips) for t in range(n)]
        for cp in first:
            cp.start()
        passed = []
        for j, (cx, cy) in enumerate(chips):
            idx = 2 * cx + cy
            for t in range(n):
                copy(t, j, half(t, idx, c), half(t, idx, c), (x, y, c)).wait_recv()
                fwd = copy(t, 3 + j, half(t, idx, c), half(t, idx, c), sibling)
                fwd.start()
                passed.append(fwd)
        for j, (cx, cy) in enumerate(chips):
            idx = 2 * cx + cy
            for t in range(n):
                copy(t, 3 + j, half(t, idx, 1 - c), half(t, idx, 1 - c), (x, y, c)).wait_recv()
        for cp in first + passed:
            cp.wait_send()

    return _exchange_call(body, "gather_weights", ws,
                          [jax.ShapeDtypeStruct((4,) + w.shape, w.dtype) for w in ws], 6)


def exchange_sibling_half(gs):
    n = len(gs)
    lh = gs[0].shape[1] // 2

    def body(*refs):
        g_refs, out_refs = refs[:n], refs[n:2 * n]
        send_sems, recv_sems = refs[2 * n:]
        x, y, c, _ = _place()
        cps = [pltpu.make_async_remote_copy(src_ref=g_refs[t].at[:, pl.ds((1 - c) * lh, lh)], dst_ref=out_refs[t],
                                            send_sem=send_sems.at[t], recv_sem=recv_sems.at[t],
                                            device_id=(x, y, 1 - c), device_id_type=MESH) for t in range(n)]
        for cp in cps:
            cp.start()
        for cp in cps:
            cp.wait()

    return _exchange_call(body, "exchange_sibling_half", gs,
                          [jax.ShapeDtypeStruct((4, lh) + g.shape[2:], g.dtype) for g in gs], 1)


def scatter_to_owners(ps):
    n = len(ps)

    def body(*refs):
        a_refs, out_refs = refs[:n], refs[n:2 * n]
        send_sems, recv_sems = refs[2 * n:]
        x, y, c, chips = _place()
        sends = [pltpu.make_async_remote_copy(src_ref=a_refs[t].at[2 * cx + cy], dst_ref=out_refs[t].at[j],
                                              send_sem=send_sems.at[3 * t + j], recv_sem=recv_sems.at[3 * t + j],
                                              device_id=(cx, cy, c), device_id_type=MESH)
                 for j, (cx, cy) in enumerate(chips) for t in range(n)]
        for cp in sends:
            cp.start()
        for cp in sends:
            cp.wait()

    return _exchange_call(body, "scatter_to_owners", ps,
                          [jax.ShapeDtypeStruct((3,) + p.shape[1:], p.dtype) for p in ps], 3)


def share_with_sibling(rs):
    n = len(rs)

    def body(*refs):
        r_refs, out_refs = refs[:n], refs[n:2 * n]
        send_sems, recv_sems = refs[2 * n:]
        x, y, c, _ = _place()
        cps = [pltpu.make_async_remote_copy(src_ref=r_refs[t], dst_ref=out_refs[t],
                                            send_sem=send_sems.at[t], recv_sem=recv_sems.at[t],
                                            device_id=(x, y, 1 - c), device_id_type=MESH) for t in range(n)]
        for cp in cps:
            cp.start()
        for cp in cps:
            cp.wait()

    return _exchange_call(body, "share_with_sibling", rs,
                          [jax.ShapeDtypeStruct(r.shape, r.dtype) for r in rs], 1)


def exchange_all(sp):
    def body(s_ref, out_ref, send_sems, recv_sems):
        x, y, c, _ = _place()
        me = 4 * x + 2 * y + c
        sends = []
        for m in range(1, 8):
            fx, fy, fc = (m >> 2) & 1, (m >> 1) & 1, m & 1
            peer = ((1 - x) if fx else x, (1 - y) if fy else y, (1 - c) if fc else c)
            cp = pltpu.make_async_remote_copy(src_ref=s_ref, dst_ref=out_ref.at[me],
                                              send_sem=send_sems.at[m - 1], recv_sem=recv_sems.at[m - 1],
                                              device_id=peer, device_id_type=MESH)
            cp.start()
            sends.append(cp)
        for cp in sends:
            cp.wait()

    return pl.pallas_call(
        body, name="exchange_all",
        in_specs=[ANY], out_specs=ANY,
        out_shape=jax.ShapeDtypeStruct((8,) + sp.shape, sp.dtype),
        scratch_shapes=[pltpu.SemaphoreType.DMA((7,)), pltpu.SemaphoreType.DMA((7,))],
    )(sp)


def add_sibling(gp, s1, core, name):
    _, nl, r, cdim = gp.shape
    lh = nl // 2
    tr = _rows(r, PACK_ROW_TILE)

    def body(c_ref, g_ref, s_ref, o_ref):
        o_ref[...] = (g_ref[...] + s_ref[...]).astype(o_ref.dtype)

    return pl.pallas_call(
        body, name=name,
        grid_spec=pltpu.PrefetchScalarGridSpec(
            num_scalar_prefetch=1, grid=(4, lh, r // tr),
            in_specs=[pl.BlockSpec((1, 1, tr, cdim), lambda k, l, i, c_ref: (k, c_ref[0] * lh + l, i, 0)),
                      pl.BlockSpec((1, 1, tr, cdim), lambda k, l, i, c_ref: (k, l, i, 0))],
            out_specs=pl.BlockSpec((1, 1, tr, cdim), lambda k, l, i, c_ref: (k, l, i, 0))),
        out_shape=jax.ShapeDtypeStruct(s1.shape, BF16),
        compiler_params=_params(("parallel", "parallel", "parallel")),
    )(core, gp, s1)


def add_chips(s2, pair, chip, name):
    _, lh, r, cdim = s2.shape
    tr = _rows(r, PACK_ROW_TILE)

    def body(k_ref, s_ref, p_ref, o_ref):
        o_ref[0] = ((s_ref[0, 0].astype(F32) + s_ref[1, 0].astype(F32)) + s_ref[2, 0].astype(F32)) + p_ref[0, 0].astype(F32)

    return pl.pallas_call(
        body, name=name,
        grid_spec=pltpu.PrefetchScalarGridSpec(
            num_scalar_prefetch=1, grid=(lh, r // tr),
            in_specs=[pl.BlockSpec((3, 1, tr, cdim), lambda l, i, k_ref: (0, l, i, 0)),
                      pl.BlockSpec((1, 1, tr, cdim), lambda l, i, k_ref: (k_ref[0], l, i, 0))],
            out_specs=pl.BlockSpec((1, tr, cdim), lambda l, i, k_ref: (l, i, 0))),
        out_shape=jax.ShapeDtypeStruct(s2.shape[1:], F32),
        compiler_params=_params(("parallel", "parallel")),
    )(chip, s2, pair)


def _adamw_math(w, g, m, v):
    m = ADAM_B1 * m + (1.0 - ADAM_B1) * g
    v = ADAM_B2 * v + (1.0 - ADAM_B2) * (g * g)
    m_hat = m / (1.0 - ADAM_B1 ** ADAM_STEP)
    v_hat = v / (1.0 - ADAM_B2 ** ADAM_STEP)
    delta = -ADAM_LR * (m_hat / (jnp.sqrt(v_hat) + ADAM_EPS) + ADAM_WD * w)
    return delta, m, v


def adamw(w, g_own, g_sibling, m, v, core, name):
    nl, r, cdim = w.shape
    lh = nl // 2
    tr = _rows(r, PACK_ROW_TILE)

    def body(c_ref, w_ref, go_ref, gs_ref, m_ref, v_ref, g_ref, d_ref, nm_ref, nv_ref):
        own = pl.program_id(0) // lh == c_ref[0]
        g = jnp.where(own, go_ref[...], gs_ref[...])
        d, nm, nv = _adamw_math(w_ref[...], g, m_ref[...], v_ref[...])
        g_ref[...] = g
        d_ref[...] = d
        nm_ref[...] = nm
        nv_ref[...] = nv

    spec = pl.BlockSpec((1, tr, cdim), lambda l, i, c_ref: (l, i, 0))
    own_spec = pl.BlockSpec((1, tr, cdim), lambda l, i, c_ref: (jnp.clip(l - c_ref[0] * lh, 0, lh - 1), i, 0))
    sib_spec = pl.BlockSpec((1, tr, cdim), lambda l, i, c_ref: (jnp.clip(l - (1 - c_ref[0]) * lh, 0, lh - 1), i, 0))
    return pl.pallas_call(
        body, name=name,
        grid_spec=pltpu.PrefetchScalarGridSpec(
            num_scalar_prefetch=1, grid=(nl, r // tr),
            in_specs=[spec, own_spec, sib_spec, spec, spec], out_specs=[spec] * 4),
        out_shape=[jax.ShapeDtypeStruct(w.shape, F32)] * 4,
        compiler_params=_params(("parallel", "parallel")),
    )(core, w, g_own, g_sibling, m, v)


def sum_and_adamw(parts, own, me, w, m, v):
    _, r, cdim = parts.shape
    tr = _rows(r, PACK_ROW_TILE)

    def body(me_ref, p_ref, o_ref, w_ref, m_ref, v_ref, g_ref, d_ref, nm_ref, nv_ref):
        part = lambda k: jnp.where(me_ref[0] == k, o_ref[...], p_ref[k])
        g = part(0)
        for k in range(1, 8):
            g = g + part(k)
        d, nm, nv = _adamw_math(w_ref[...], g, m_ref[...], v_ref[...])
        g_ref[...] = g
        d_ref[...] = d
        nm_ref[...] = nm
        nv_ref[...] = nv

    spec = pl.BlockSpec((tr, cdim), lambda i, me_ref: (i, 0))
    return pl.pallas_call(
        body, name="sum_and_adamw",
        grid_spec=pltpu.PrefetchScalarGridSpec(
            num_scalar_prefetch=1, grid=(r // tr,),
            in_specs=[pl.BlockSpec((8, tr, cdim), lambda i, me_ref: (0, i, 0)), spec, spec, spec, spec],
            out_specs=[spec] * 4),
        out_shape=[jax.ShapeDtypeStruct((r, cdim), F32)] * 4,
        compiler_params=_params(("parallel",)),
    )(me, parts, own, w, m, v)


SHARDED = ("w_in", "w_glu", "w_branch_a", "w_branch_b", "w_out", "w_mlp_up", "w_mlp_down")
SHARD_AXIS = {"w_in": 2, "w_glu": 1, "w_branch_a": 2, "w_branch_b": 2, "w_out": 1, "w_mlp_up": 2, "w_mlp_down": 1}
SMALL = ("norm_mix", "b_forget", "ssm_lambda_re", "ssm_lambda_im", "ssm_log_dt", "ssm_b_re", "ssm_b_im",
         "ssm_c_re", "ssm_c_im", "ssm_d", "b_glu", "norm_mlp", "norm_final")


def pack_flat(arrs):
    flat = jnp.concatenate([a.reshape(-1).astype(F32) for a in arrs])
    unit = PACK_COLS * PACK_ROW_TILE
    rows = (flat.shape[0] + unit - 1) // unit * PACK_ROW_TILE
    return jnp.pad(flat, (0, rows * PACK_COLS - flat.shape[0])).reshape(rows, PACK_COLS)


def unpack_flat(packed, shapes):
    flat = packed.reshape(-1)
    out, off = [], 0
    for s in shapes:
        n = math.prod(s)
        out.append(flat[off:off + n].reshape(tuple(s)))
        off += n
    return out


def _discretise(lam_re, lam_im, log_dt, b_re, b_im):
    dt = jnp.exp(log_dt)[:, None]
    mag = jnp.exp(lam_re * dt)
    ar = mag * jnp.cos(lam_im * dt)
    ai = mag * jnp.sin(lam_im * dt)
    den = lam_re * lam_re + lam_im * lam_im
    cr = ((ar - 1.0) * lam_re + ai * lam_im) / den
    ci = (ai * lam_re - (ar - 1.0) * lam_im) / den
    bbr = cr[:, :, None] * b_re - ci[:, :, None] * b_im
    bbi = cr[:, :, None] * b_im + ci[:, :, None] * b_re
    return ar, ai, bbr, bbi


def _state_lanes(re, im):
    return jnp.concatenate([re.reshape(SSM_CHUNKS, CHUNK_STATES), im.reshape(SSM_CHUNKS, CHUNK_STATES)],
                           axis=1).reshape(STATE_LANES)


def _ssm_inputs(ar, ai, bbr, bbi, c_re, c_im):
    pr, pi = [ar], [ai]
    for _ in range(SUBLANES - 1):
        pr, pi = pr + [pr[-1] * ar - pi[-1] * ai], pi + [pr[-1] * ai + pi[-1] * ar]
    power = lambda n: _state_lanes(pr[n - 1], pi[n - 1])
    zero = jnp.zeros((STATE_LANES,), F32)
    rows = [power(1), power(2), power(4)] + [zero] * 5
    rows += [power(i + 1) for i in range(SUBLANES)]
    rows += [power(SUBLANES - i) for i in range(SUBLANES)]
    tabs = jnp.stack(rows)
    eye = jnp.eye(CHUNK_GROUPS, dtype=F32)

    def to_wb(bb):
        t = bb.reshape(SSM_CHUNKS, CHUNK_GROUPS, SSM_STATE, SSM_GROUP_CH).transpose(0, 1, 3, 2)
        return jnp.einsum("jgcp,gh->jgchp", t, eye).reshape(SSM_CHUNKS, LANES, CHUNK_STATES)

    def to_wc(cc):
        t = cc.reshape(SSM_CHUNKS, CHUNK_GROUPS, SSM_GROUP_CH, SSM_STATE)
        return jnp.einsum("jgcp,gh->jhpgc", t, eye).reshape(SSM_CHUNKS, CHUNK_STATES, LANES)

    wb4 = jnp.concatenate([to_wb(bbr), to_wb(bbi)], axis=2).astype(BF16)
    wc4 = jnp.concatenate([to_wc(c_re), -to_wc(c_im)], axis=1).astype(BF16)
    return tabs, wb4, wc4


def _ssm_param_grads(ga8, gwb, gwc):
    eye = jnp.eye(CHUNK_GROUPS, dtype=F32)
    ga = jnp.sum(ga8, axis=0).reshape(SSM_CHUNKS, 2, CHUNK_STATES)
    gar = ga[:, 0].reshape(SSM_GROUPS, SSM_STATE)
    gai = ga[:, 1].reshape(SSM_GROUPS, SSM_STATE)

    def from_wb(g):
        t = g.reshape(SSM_CHUNKS, CHUNK_GROUPS, SSM_GROUP_CH, CHUNK_GROUPS, SSM_STATE)
        return jnp.einsum("jgchp,gh->jgpc", t, eye).reshape(SSM_GROUPS, SSM_STATE, SSM_GROUP_CH)

    def from_wc(g):
        t = g.reshape(SSM_CHUNKS, CHUNK_GROUPS, SSM_STATE, CHUNK_GROUPS, SSM_GROUP_CH)
        return jnp.einsum("jhpgc,gh->jgcp", t, eye).reshape(SSM_GROUPS, SSM_GROUP_CH, SSM_STATE)

    return (gar, gai, from_wb(gwb[:, :, :CHUNK_STATES]), from_wb(gwb[:, :, CHUNK_STATES:]),
            from_wc(gwc[:, :CHUNK_STATES]), -from_wc(gwc[:, CHUNK_STATES:]))


def kernel(x, norm_mix, w_in, b_forget, ssm_lambda_re, ssm_lambda_im, ssm_log_dt, ssm_b_re, ssm_b_im, ssm_c_re, ssm_c_im, ssm_d, w_glu, b_glu, w_branch_a, w_branch_b, w_out, norm_mlp, w_mlp_up, w_mlp_down, norm_final, loss_target, m_norm_mix, m_w_in, m_b_forget, m_ssm_lambda_re, m_ssm_lambda_im, m_ssm_log_dt, m_ssm_b_re, m_ssm_b_im, m_ssm_c_re, m_ssm_c_im, m_ssm_d, m_w_glu, m_b_glu, m_w_branch_a, m_w_branch_b, m_w_out, m_norm_mlp, m_w_mlp_up, m_w_mlp_down, m_norm_final, v_norm_mix, v_w_in, v_b_forget, v_ssm_lambda_re, v_ssm_lambda_im, v_ssm_log_dt, v_ssm_b_re, v_ssm_b_im, v_ssm_c_re, v_ssm_c_im, v_ssm_d, v_w_glu, v_b_glu, v_w_branch_a, v_w_branch_b, v_w_out, v_norm_mlp, v_w_mlp_up, v_w_mlp_down, v_norm_final):
    args = dict(locals())
    bsz, seq, d = x.shape
    nl = norm_mix.shape[0]
    tokens = bsz * seq
    aw, sw = ATTN_WIDTH, SSM_WIDTH
    core = lax.axis_index("c").astype(jnp.int32).reshape(1)
    chip = (2 * lax.axis_index("x") + lax.axis_index("y")).astype(jnp.int32).reshape(1)
    device = 2 * chip + core

    own = {n: args[n].astype(BF16) for n in SHARDED}
    gathered = dict(zip(SHARDED, gather_weights([own[n] for n in SHARDED])))
    full = {n: jnp.concatenate([jnp.where(chip[0] == k, own[n], gathered[n][k]) for k in range(4)],
                               axis=SHARD_AXIS[n]) for n in SHARDED}
    win_f = full["w_in"]
    o_f, o_u, o_ga, o_gb = 3 * aw, 3 * aw + ATTN_HEADS, 3 * aw + ATTN_HEADS + sw, 3 * aw + ATTN_HEADS + sw + d
    w_qkv = win_f[:, :, :o_f]
    w_rest = jnp.concatenate([win_f[:, :, o_ga:o_gb], win_f[:, :, o_gb:], win_f[:, :, o_u:o_ga],
                              jnp.pad(win_f[:, :, o_f:o_u], ((0, 0), (0, 0), (0, F_PAD - ATTN_HEADS)))], axis=2)
    w_pad = jnp.concatenate([w_qkv, w_rest], axis=2)
    u_blk = 2 * d // sw
    f_blk = (2 * d + sw) // F_PAD
    bf_pad = jnp.pad(b_forget, ((0, 0), (0, F_PAD - ATTN_HEADS)))

    disc = [jax.vjp(_discretise, ssm_lambda_re[l], ssm_lambda_im[l], ssm_log_dt[l], ssm_b_re[l], ssm_b_im[l])
            for l in range(nl)]

    xs = x.reshape(tokens, d)
    saved = []
    for l in range(nl):
        g1 = norm_mix[l].reshape(1, d)
        qkv = norm_matmul(xs, g1, w_qkv[l], BF16, "proj_qkv")
        rest = norm_matmul(xs, g1, w_rest[l], F32, "proj_rest")
        cumcol, cumrow = forget_cumsum(rest, bf_pad[l:l + 1], bsz, seq, f_blk)
        ya, lser = fox_fwd(qkv, cumrow, bsz, seq)
        tabs, wb4, wc4 = _ssm_inputs(*disc[l][0], ssm_c_re[l], ssm_c_im[l])
        dskip = ssm_d[l].reshape(1, sw)
        ys, hs = ssm_fwd(rest, wb4, wc4, tabs, dskip, bsz, seq, u_blk)
        x1, z, pa, pb, yb, yb2, mixed = mix_fwd(ya, ys, rest, xs, full["w_glu"][l], b_glu[l].reshape(1, sw),
                                                 full["w_branch_a"][l], full["w_branch_b"][l], full["w_out"][l])
        x2, up = mlp_fwd(x1, norm_mlp[l].reshape(1, d), full["w_mlp_up"][l], full["w_mlp_down"][l])
        saved.append(dict(x0=xs, qkv=qkv, rest=rest, cumcol=cumcol, ya=ya, lser=lser,
                          tabs=tabs, wb4=wb4, wc4=wc4, dskip=dskip, ys=ys, hs=hs, x1=x1, z=z, pa=pa, pb=pb,
                          yb=yb, yb2=yb2, mixed=mixed, up=up))
        xs = x2
    dx, loss_rows, dgf_rows = final_loss(xs, norm_final.reshape(1, d), loss_target.reshape(tokens, d))
    loss = lax.psum(jnp.sum(loss_rows), ("x", "y", "c"))

    big = {n: [None] * nl for n in SHARDED}
    small = {n: [None] * nl for n in SMALL if n != "norm_final"}
    for l in reversed(range(nl)):
        s = saved[l]
        g2 = norm_mlp[l].reshape(1, d)
        dup, dx1, dg2 = mlp_bwd(dx, s["up"], s["x1"], g2, full["w_mlp_up"][l], full["w_mlp_down"][l])
        big["w_mlp_down"][l] = matmul_tn(s["up"], dx, "grad_w_mlp_down", a_kind="relu2")
        big["w_mlp_up"][l] = matmul_tn(s["x1"], dup, "grad_w_mlp_up", a_kind="norm", gain=g2)
        small["norm_mlp"][l] = jnp.sum(dg2, axis=0)
        dya, dys, dgab, dpa, dpb, dz, dbg = mix_bwd(dx1, s["rest"], s["pa"], s["pb"], s["z"], s["ys"],
                                                    full["w_glu"][l], full["w_branch_a"][l], full["w_branch_b"][l],
                                                    full["w_out"][l])
        big["w_out"][l] = matmul_tn(s["mixed"], dx1, "grad_w_out")
        big["w_branch_a"][l] = matmul_tn(s["ya"], dpa, "grad_w_branch_a")
        big["w_branch_b"][l] = matmul_tn(s["yb2"], dpb, "grad_w_branch_b")
        big["w_glu"][l] = matmul_tn(s["yb"], dz, "grad_w_glu")
        small["b_glu"][l] = jnp.sum(dbg, axis=0)
        du, ga8, gwb, gwc, gd8 = ssm_bwd(dys, s["rest"], s["hs"], s["wb4"], s["wc4"], s["tabs"], s["dskip"],
                                         bsz, seq, u_blk)
        gar, gai, gbbr, gbbi, gcr, gci = _ssm_param_grads(ga8, gwb, gwc)
        glr, gli, gdt, gbr, gbi = disc[l][1]((gar, gai, gbbr, gbbi))
        small["ssm_lambda_re"][l], small["ssm_lambda_im"][l], small["ssm_log_dt"][l] = glr, gli, gdt
        small["ssm_b_re"][l], small["ssm_b_im"][l] = gbr, gbi
        small["ssm_c_re"][l], small["ssm_c_im"][l] = gcr, gci
        small["ssm_d"][l] = jnp.sum(gd8, axis=0)
        dq, dk, dv, df, dbf = fox_bwd(s["qkv"], dya, s["ya"], s["lser"], s["cumcol"],
                                      s["rest"], bf_pad[l:l + 1], bsz, seq, f_blk)
        small["b_forget"][l] = jnp.sum(dbf, axis=0)[:ATTN_HEADS]
        dproj = jnp.concatenate([dq, dk, dv, dgab, du, df], axis=1)
        g1 = norm_mix[l].reshape(1, d)
        dwp = matmul_tn(s["x0"], dproj, "grad_w_in", a_kind="norm", gain=g1)
        big["w_in"][l] = jnp.concatenate(
            [dwp[:, :o_f], dwp[:, o_f + 2 * d + sw:o_f + 2 * d + sw + ATTN_HEADS],
             dwp[:, o_f + 2 * d:o_f + 2 * d + sw], dwp[:, o_f:o_f + 2 * d]], axis=1)
        dx, dg1 = proj_bwd(dproj, w_pad[l], s["x0"], dx1, g1)
        small["norm_mix"][l] = jnp.sum(dg1, axis=0)
    grad_x = dx.reshape(bsz, seq, d)

    gps = [jnp.stack(jnp.split(jnp.stack(big[n]), 4, axis=SHARD_AXIS[n])) for n in SHARDED]
    from_sibling = exchange_sibling_half(gps)
    pair = [add_sibling(g, s1, core, "add_sibling_" + n) for n, g, s1 in zip(SHARDED, gps, from_sibling)]
    slots = scatter_to_owners(pair)
    reduced = [add_chips(s2, p, chip, "add_chips_" + n) for n, s2, p in zip(SHARDED, slots, pair)]
    from_sibling = share_with_sibling(reduced)
    out_g, out_d, out_m, out_v = {}, {}, {}, {}
    for n, g_own, g_sib in zip(SHARDED, reduced, from_sibling):
        out_g[n], out_d[n], out_m[n], out_v[n] = adamw(args[n], g_own, g_sib, args["m_" + n], args["v_" + n],
                                                       core, "adamw_" + n)

    small_g = [jnp.stack(small[n]) if n != "norm_final" else jnp.sum(dgf_rows, axis=0) for n in SMALL]
    small_shapes = [args[n].shape for n in SMALL]
    small_packed = pack_flat(small_g)
    sg, sd, sm, sv = sum_and_adamw(exchange_all(small_packed), small_packed, device,
                                   pack_flat([args[n] for n in SMALL]),
                                   pack_flat([args["m_" + n] for n in SMALL]),
                                   pack_flat([args["v_" + n] for n in SMALL]))
    for res, packed in ((out_g, sg), (out_d, sd), (out_m, sm), (out_v, sv)):
        res.update(zip(SMALL, unpack_flat(packed, small_shapes)))

    order = ("norm_mix", "w_in", "b_forget", "ssm_lambda_re", "ssm_lambda_im", "ssm_log_dt", "ssm_b_re",
             "ssm_b_im", "ssm_c_re", "ssm_c_im", "ssm_d", "w_glu", "b_glu", "w_branch_a", "w_branch_b", "w_out",
             "norm_mlp", "w_mlp_up", "w_mlp_down", "norm_final")
    return (loss, grad_x, *[out_g[n] for n in order], *[out_d[n] for n in order],
            *[out_m[n] for n in order], *[out_v[n] for n in order])
```

```python
import math

import jax
import jax.numpy as jnp
from jax import lax
from jax.experimental import pallas as pl
from jax.experimental.pallas import tpu as pltpu

F32 = jnp.float32
BF16 = jnp.bfloat16
MESH = pl.DeviceIdType.MESH
ANY = pl.BlockSpec(memory_space=pl.ANY)

ATTN_HEADS = 8
HEAD_DIM = 64
ATTN_WIDTH = ATTN_HEADS * HEAD_DIM
HEAD_PAIRS = ATTN_HEADS // 2
SSM_GROUPS = 32
SSM_GROUP_CH = 16
SSM_STATE = 64
SSM_WIDTH = SSM_GROUPS * SSM_GROUP_CH
LANES = 128
SUBLANES = 8
SSM_CHUNKS = SSM_WIDTH // LANES
CHUNK_GROUPS = SSM_GROUPS // SSM_CHUNKS
CHUNK_STATES = CHUNK_GROUPS * SSM_STATE
CHUNK_LANES = 2 * CHUNK_STATES
STATE_LANES = SSM_CHUNKS * CHUNK_LANES
F_PAD = LANES
RMS_EPS = 1e-6
ADAM_LR = 0.001
ADAM_B1 = 0.9
ADAM_B2 = 0.999
ADAM_EPS = 1e-08
ADAM_WD = 0.01
ADAM_STEP = 10
PACK_COLS = 1024
PACK_ROW_TILE = 256
VMEM_LIMIT = 52 * 1024 * 1024
NEG_BIG = -1e30
GELU_C = math.sqrt(2.0 / math.pi)
GELU_A = 0.044715

NN = (((1,), (0,)), ((), ()))
NT = (((1,), (1,)), ((), ()))
TN = (((0,), (0,)), ((), ()))


def _pick(n, pref):
    if n <= pref:
        return n
    best = LANES
    for t in range(LANES, pref + 1, LANES):
        if n % t == 0:
            best = t
    assert n % best == 0, (n, pref)
    return best


def _rows(n, pref):
    t = min(n, pref)
    while n % t:
        t //= 2
    assert t % 16 == 0 or t == n, (n, pref)
    return t


def _params(sem):
    return pltpu.CompilerParams(dimension_semantics=sem, vmem_limit_bytes=VMEM_LIMIT)


def _fold8(v):
    r, c = v.shape
    return jnp.sum(v.reshape(r // SUBLANES, SUBLANES, c), axis=0)


def _dot(a, b, dims=None):
    if dims is None:
        return jnp.dot(a, b, preferred_element_type=F32)
    return lax.dot_general(a, b, dims, preferred_element_type=F32)


def _dot_exact(a, b, dims):
    return lax.dot_general(a, b, dims, preferred_element_type=F32, precision=lax.Precision.HIGHEST)


def _sigmoid(v):
    return 1.0 / (1.0 + jnp.exp(-v))


def _rms_scale(x):
    return lax.rsqrt(jnp.mean(x * x, axis=-1, keepdims=True) + RMS_EPS)


def _rms_bwd(x, g, dh):
    r = _rms_scale(x)
    xn = x * r
    dxn = dh * g
    dx = r * (dxn - xn * jnp.mean(dxn * xn, axis=-1, keepdims=True))
    return dx, dh * xn


def norm_matmul(x, g, w, out_dtype, name):
    t, d = x.shape
    m = w.shape[1]
    tm, tn = _rows(t, 512), _pick(m, 1024)

    def body(x_ref, g_ref, w_ref, o_ref):
        xv = x_ref[...]
        h = (xv * _rms_scale(xv) * g_ref[...]).astype(BF16)
        o_ref[...] = _dot(h, w_ref[...]).astype(o_ref.dtype)

    return pl.pallas_call(
        body, name=name, grid=(t // tm, m // tn),
        in_specs=[pl.BlockSpec((tm, d), lambda i, j: (i, 0)),
                  pl.BlockSpec((1, d), lambda i, j: (0, 0)),
                  pl.BlockSpec((d, tn), lambda i, j: (0, j))],
        out_specs=pl.BlockSpec((tm, tn), lambda i, j: (i, j)),
        out_shape=jax.ShapeDtypeStruct((t, m), out_dtype),
        compiler_params=_params(("parallel", "arbitrary")),
    )(x, g, w)


def matmul_tn(a, b, name, a_kind="plain", gain=None, tm_pref=1024, tn_pref=1536, tk_pref=512):
    t, ma = a.shape
    nb = b.shape[1]
    tm = ma if a_kind == "norm" else _pick(ma, tm_pref)
    tn = _pick(nb, tn_pref)
    tk = _rows(t, tk_pref)
    nk = t // tk

    def body(*refs):
        if a_kind == "norm":
            a_ref, g_ref, b_ref, o_ref, acc = refs
        else:
            a_ref, b_ref, o_ref, acc = refs
        k = pl.program_id(2)

        @pl.when(k == 0)
        def _():
            acc[...] = jnp.zeros_like(acc)

        av = a_ref[...]
        if a_kind == "norm":
            av = av * _rms_scale(av) * g_ref[...]
        elif a_kind == "relu2":
            av = jnp.square(jnp.maximum(av.astype(F32), 0.0))
        acc[...] += _dot(av.astype(BF16), b_ref[...].astype(BF16), TN)

        @pl.when(k == nk - 1)
        def _():
            o_ref[...] = acc[...]

    in_specs = [pl.BlockSpec((tk, tm), lambda i, j, k: (k, i))]
    args = [a]
    if a_kind == "norm":
        in_specs.append(pl.BlockSpec((1, ma), lambda i, j, k: (0, 0)))
        args.append(gain)
    in_specs.append(pl.BlockSpec((tk, tn), lambda i, j, k: (k, j)))
    args.append(b)
    return pl.pallas_call(
        body, name=name, grid=(ma // tm, nb // tn, nk),
        in_specs=in_specs,
        out_specs=pl.BlockSpec((tm, tn), lambda i, j, k: (i, j)),
        out_shape=jax.ShapeDtypeStruct((ma, nb), F32),
        scratch_shapes=[pltpu.VMEM((tm, tn), F32)],
        compiler_params=_params(("parallel", "parallel", "arbitrary")),
    )(*args)


def _tri(n, upper):
    r = lax.broadcasted_iota(jnp.int32, (n, n), 0)
    c = lax.broadcasted_iota(jnp.int32, (n, n), 1)
    return jnp.where((c >= r) if upper else (c <= r), 1.0, 0.0).astype(F32)


def _head_rows():
    r = lax.broadcasted_iota(jnp.int32, (SUBLANES, LANES), 0)
    c = lax.broadcasted_iota(jnp.int32, (SUBLANES, LANES), 1)
    return jnp.where(r == c, 1.0, 0.0).astype(F32)


def forget_cumsum(rest, bf, bsz, seq, f_blk):
    tc = _rows(seq, 512)
    nc = seq // tc

    def body(f_ref, b_ref, col_ref, row_ref, carry):
        c = pl.program_id(1)

        @pl.when(c == 0)
        def _():
            carry[...] = jnp.zeros_like(carry)

        z = f_ref[...] + b_ref[...]
        logf = jnp.minimum(z, 0.0) - jnp.log(1.0 + jnp.exp(-jnp.abs(z)))
        cum = _dot_exact(_tri(tc, False), logf, NN) + carry[0:1, :]
        col_ref[0] = cum
        row_ref[0] = _dot_exact(_head_rows(), cum, NT)
        carry[...] = jnp.broadcast_to(cum[tc - 1:tc, :], carry.shape)

    return pl.pallas_call(
        body, name="forget_cumsum", grid=(bsz, nc),
        in_specs=[pl.BlockSpec((tc, F_PAD), lambda b, c: (b * nc + c, f_blk)),
                  pl.BlockSpec((1, F_PAD), lambda b, c: (0, 0))],
        out_specs=[pl.BlockSpec((1, tc, LANES), lambda b, c: (b, c, 0)),
                   pl.BlockSpec((1, SUBLANES, tc), lambda b, c: (b, 0, c))],
        out_shape=[jax.ShapeDtypeStruct((bsz, seq, LANES), F32),
                   jax.ShapeDtypeStruct((bsz, SUBLANES, seq), F32)],
        scratch_shapes=[pltpu.VMEM((SUBLANES, LANES), F32)],
        compiler_params=_params(("parallel", "arbitrary")),
    )(rest, bf)


def forget_bwd(dcq, dcp, rest, bf, bsz, seq, f_blk):
    tc = _attn_tile(seq)
    nc = seq // tc

    def body(dq_ref, dc_ref, f_ref, b_ref, df_ref, db_ref, carry):
        b = pl.program_id(0)
        c = pl.program_id(1)

        @pl.when(c == 0)
        def _():
            carry[...] = jnp.zeros_like(carry)

        @pl.when((b == 0) & (c == 0))
        def _():
            db_ref[...] = jnp.zeros_like(db_ref)

        row = lax.broadcasted_iota(jnp.int32, (SUBLANES, tc), 0)
        heads = jnp.zeros((SUBLANES, tc), F32)
        dc = jnp.zeros((tc, LANES), F32)
        for p in range(HEAD_PAIRS):
            blk = dq_ref[0, p, 0]
            heads = heads + jnp.where(row == 2 * p, blk[0:1], 0.0) + jnp.where(row == 2 * p + 1, blk[1:2], 0.0)
            dc = dc + dc_ref[0, p]
        dc = dc + jnp.concatenate([heads, jnp.zeros((LANES - SUBLANES, tc), F32)], axis=0).T
        dlogf = _dot_exact(_tri(tc, True), dc, NN) + carry[0:1, :]
        carry[...] = jnp.broadcast_to(dlogf[0:1, :], carry.shape)
        z = f_ref[...] + b_ref[...]
        lane = lax.broadcasted_iota(jnp.int32, z.shape, 1)
        df = jnp.where(lane < ATTN_HEADS, dlogf * _sigmoid(-z), 0.0)
        df_ref[...] = df.astype(df_ref.dtype)
        db_ref[...] += _fold8(df)

    return pl.pallas_call(
        body, name="forget_bwd", grid=(bsz, nc),
        in_specs=[pl.BlockSpec((1, HEAD_PAIRS, 1, SUBLANES, tc), lambda b, c: (b, 0, nc - 1 - c, 0, 0)),
                  pl.BlockSpec((1, HEAD_PAIRS, tc, LANES), lambda b, c: (b, 0, nc - 1 - c, 0)),
                  pl.BlockSpec((tc, F_PAD), lambda b, c: (b * nc + nc - 1 - c, f_blk)),
                  pl.BlockSpec((1, F_PAD), lambda b, c: (0, 0))],
        out_specs=[pl.BlockSpec((tc, F_PAD), lambda b, c: (b * nc + nc - 1 - c, 0)),
                   pl.BlockSpec((SUBLANES, F_PAD), lambda b, c: (0, 0))],
        out_shape=[jax.ShapeDtypeStruct((bsz * seq, F_PAD), BF16),
                   jax.ShapeDtypeStruct((SUBLANES, F_PAD), F32)],
        scratch_shapes=[pltpu.VMEM((SUBLANES, LANES), F32)],
        compiler_params=_params(("arbitrary", "arbitrary")),
    )(dcq, dcp, rest, bf)


def _attn_tile(seq):
    return 512 if seq >= 2048 else 128


def _lane_head(shape, par):
    lane = lax.broadcasted_iota(jnp.int32, shape, len(shape) - 1)
    return (lane >= HEAD_DIM) if par else (lane < HEAD_DIM)


def _pick_lane(block, idx):
    lane = lax.broadcasted_iota(jnp.int32, block.shape, 1)
    return jnp.sum(jnp.where(lane == idx, block, 0.0), axis=1, keepdims=True)


def _pair_rows(lo_lane, hi_lane):
    r = lax.broadcasted_iota(jnp.int32, (SUBLANES, LANES), 0)
    c = lax.broadcasted_iota(jnp.int32, (SUBLANES, LANES), 1)
    if lo_lane is None:
        sel = ((r == 0) & (c < HEAD_DIM)) | ((r == 1) & (c >= HEAD_DIM))
    else:
        sel = ((r == 0) & (c == lo_lane)) | ((r == 1) & (c == hi_lane))
    return jnp.where(sel, 1.0, 0.0).astype(F32)


def _causal(s, transposed):
    r = lax.broadcasted_iota(jnp.int32, s.shape, 0)
    c = lax.broadcasted_iota(jnp.int32, s.shape, 1)
    return jnp.where((c >= r) if transposed else (r >= c), s, NEG_BIG)


def _causal_pairs(n, key_major):
    if key_major:
        pairs = [(i, j) for j in range(n) for i in range(j, n)]
    else:
        pairs = [(i, j) for i in range(n) for j in range(i + 1)]
    return (jnp.asarray([p[0] for p in pairs], jnp.int32), jnp.asarray([p[1] for p in pairs], jnp.int32))


def fox_fwd(qkv, cumrow, bsz, seq):
    tq = _attn_tile(seq)
    nq = seq // tq
    scale = HEAD_DIM ** -0.5
    kb, vb = ATTN_WIDTH // LANES, 2 * ATTN_WIDTH // LANES
    qi_tab, kj_tab = _causal_pairs(nq, False)

    def body(qi_ref, kj_ref, q_ref, k_ref, v_ref, cr_ref, o_ref, lr_ref, m_s, acc_s):
        hp = pl.program_id(1)
        qi = qi_ref[pl.program_id(2)]
        kj = kj_ref[pl.program_id(2)]

        @pl.when(kj == 0)
        def _():
            m_s[...] = jnp.full_like(m_s, NEG_BIG)
            acc_s[...] = jnp.zeros_like(acc_s)

        def step(diag):
            q = q_ref[...]
            k = k_ref[...]
            v = v_ref[...]
            for par in range(2):
                sel = _lane_head(q.shape, par)
                qh = jnp.where(sel, q, 0.0) * scale
                s = _dot(qh.astype(BF16), k, NT) - cr_ref[0, pl.ds(2 * hp + par, 1), :]
                if diag:
                    s = _causal(s, False)
                m_prev = m_s[par]
                m_new = jnp.maximum(m_prev, jnp.max(s, axis=1, keepdims=True))
                p = jnp.exp(s - m_new).astype(BF16)
                acc_s[par] = jnp.exp(m_prev - m_new) * acc_s[par] + _dot(p, jnp.where(sel, v, 1.0).astype(BF16))
                m_s[par] = m_new

        @pl.when(kj < qi)
        def _():
            step(False)

        @pl.when(kj == qi)
        def _():
            step(True)
            lo = _lane_head((tq, LANES), 0)
            sums = [pltpu.roll(acc_s[par], HEAD_DIM, 1) for par in range(2)]
            out = jnp.where(lo, acc_s[0] / sums[0], acc_s[1] / sums[1])
            o_ref[...] = out.astype(o_ref.dtype)
            lse = jnp.where(lo, m_s[0] + jnp.log(sums[0]), m_s[1] + jnp.log(sums[1]))
            lr_ref[0, 0] = _dot_exact(_pair_rows(0, HEAD_DIM), lse, NT)

    return pl.pallas_call(
        body, name="fox_fwd",
        grid_spec=pltpu.PrefetchScalarGridSpec(
            num_scalar_prefetch=2, grid=(bsz, HEAD_PAIRS, int(qi_tab.shape[0])),
            in_specs=[pl.BlockSpec((tq, LANES), lambda b, h, t, qi, kj: (b * nq + qi[t], h)),
                      pl.BlockSpec((tq, LANES), lambda b, h, t, qi, kj: (b * nq + kj[t], kb + h)),
                      pl.BlockSpec((tq, LANES), lambda b, h, t, qi, kj: (b * nq + kj[t], vb + h)),
                      pl.BlockSpec((1, SUBLANES, tq), lambda b, h, t, qi, kj: (b, 0, kj[t]))],
            out_specs=[pl.BlockSpec((tq, LANES), lambda b, h, t, qi, kj: (b * nq + qi[t], h)),
                       pl.BlockSpec((1, 1, SUBLANES, tq), lambda b, h, t, qi, kj: (b, h, 0, qi[t]))],
            scratch_shapes=[pltpu.VMEM((2, tq, 1), F32), pltpu.VMEM((2, tq, LANES), F32)]),
        out_shape=[jax.ShapeDtypeStruct((bsz * seq, ATTN_WIDTH), BF16),
                   jax.ShapeDtypeStruct((bsz, HEAD_PAIRS, SUBLANES, seq), F32)],
        compiler_params=_params(("parallel", "parallel", "arbitrary")),
    )(qi_tab, kj_tab, qkv, qkv, qkv, cumrow)


def fox_bwd_kernel(qkv, do, o, lser, cumcol, bsz, seq):
    tk = _attn_tile(seq)
    nk = seq // tk
    scale = HEAD_DIM ** -0.5
    kb, vb = ATTN_WIDTH // LANES, 2 * ATTN_WIDTH // LANES
    qi_tab, kj_tab = _causal_pairs(nk, True)
    npairs = int(qi_tab.shape[0])

    def body(qi_ref, kj_ref, q_ref, k_ref, v_ref, do_ref, o_ref, lr_ref, cc_ref,
             dq_ref, dk_ref, dv_ref, dc_ref, dr_ref, dq_s, dk_s, dv_s, dc_s, dr_s):
        hp = pl.program_id(1)
        t = pl.program_id(2)
        qi = qi_ref[t]
        kj = kj_ref[t]

        @pl.when(t == 0)
        def _():
            dq_s[...] = jnp.zeros_like(dq_s)
            dr_s[...] = jnp.zeros_like(dr_s)

        @pl.when(qi == kj)
        def _():
            dk_s[...] = jnp.zeros_like(dk_s)
            dv_s[...] = jnp.zeros_like(dv_s)
            dc_s[...] = jnp.zeros_like(dc_s)

        def step(diag):
            q = q_ref[...]
            k = k_ref[...]
            v = v_ref[...]
            dov = do_ref[...]
            prod = dov.astype(F32) * o_ref[...].astype(F32)
            drow = _dot_exact(_pair_rows(None, None), prod, NT)
            lrow = lr_ref[0, 0]
            lane = lax.broadcasted_iota(jnp.int32, (tk, LANES), 1)
            for par in range(2):
                head = 2 * hp + par
                sel = _lane_head(k.shape, par)
                kh = (jnp.where(sel, k, 0.0) * scale).astype(BF16)
                st = _dot(kh, q, NT) - _pick_lane(cc_ref[0], head)
                if diag:
                    st = _causal(st, True)
                pt = jnp.exp(st - lrow[par:par + 1, :])
                vh = jnp.where(sel, v, 0.0)
                dpt = _dot(vh.astype(BF16), dov, NT)
                dst = pt * (dpt - drow[par:par + 1, :])
                dsb = dst.astype(BF16)
                dv_s[...] += jnp.where(sel, _dot(pt.astype(BF16), dov), 0.0)
                dk_s[...] += jnp.where(sel, _dot(dsb, q), 0.0)
                dq_s[qi] += _dot(dsb, kh, TN)
                dc_s[...] += jnp.where(lane == head, -jnp.sum(dst, axis=1, keepdims=True), 0.0)
                dr_s[qi, par:par + 1, :] += jnp.sum(dst, axis=0, keepdims=True)

        @pl.when(qi > kj)
        def _():
            step(False)

        @pl.when(qi == kj)
        def _():
            step(True)

        @pl.when(qi == nk - 1)
        def _():
            dk_ref[...] = (dk_s[...] * scale).astype(dk_ref.dtype)
            dv_ref[...] = dv_s[...].astype(dv_ref.dtype)
            dc_ref[0, 0] = dc_s[...]

        @pl.when(t == npairs - 1)
        def _():
            for i in range(nk):
                dq_ref[i * tk:(i + 1) * tk, :] = dq_s[i].astype(dq_ref.dtype)
            dr_ref[0, 0] = dr_s[...]

    return pl.pallas_call(
        body, name="fox_bwd",
        grid_spec=pltpu.PrefetchScalarGridSpec(
            num_scalar_prefetch=2, grid=(bsz, HEAD_PAIRS, npairs),
            in_specs=[pl.BlockSpec((tk, LANES), lambda b, h, t, qi, kj: (b * nk + qi[t], h)),
                      pl.BlockSpec((tk, LANES), lambda b, h, t, qi, kj: (b * nk + kj[t], kb + h)),
                      pl.BlockSpec((tk, LANES), lambda b, h, t, qi, kj: (b * nk + kj[t], vb + h)),
                      pl.BlockSpec((tk, LANES), lambda b, h, t, qi, kj: (b * nk + qi[t], h)),
                      pl.BlockSpec((tk, LANES), lambda b, h, t, qi, kj: (b * nk + qi[t], h)),
                      pl.BlockSpec((1, 1, SUBLANES, tk), lambda b, h, t, qi, kj: (b, h, 0, qi[t])),
                      pl.BlockSpec((1, tk, LANES), lambda b, h, t, qi, kj: (b, kj[t], 0))],
            out_specs=[pl.BlockSpec((seq, LANES), lambda b, h, t, qi, kj: (b, h)),
                       pl.BlockSpec((tk, LANES), lambda b, h, t, qi, kj: (b * nk + kj[t], h)),
                       pl.BlockSpec((tk, LANES), lambda b, h, t, qi, kj: (b * nk + kj[t], h)),
                       pl.BlockSpec((1, 1, tk, LANES), lambda b, h, t, qi, kj: (b, h, kj[t], 0)),
                       pl.BlockSpec((1, 1, nk, SUBLANES, tk), lambda b, h, t, qi, kj: (b, h, 0, 0, 0))],
            scratch_shapes=[pltpu.VMEM((nk, tk, LANES), F32), pltpu.VMEM((tk, LANES), F32),
                            pltpu.VMEM((tk, LANES), F32), pltpu.VMEM((tk, LANES), F32),
                            pltpu.VMEM((nk, SUBLANES, tk), F32)]),
        out_shape=[jax.ShapeDtypeStruct((bsz * seq, ATTN_WIDTH), BF16),
                   jax.ShapeDtypeStruct((bsz * seq, ATTN_WIDTH), BF16),
                   jax.ShapeDtypeStruct((bsz * seq, ATTN_WIDTH), BF16),
                   jax.ShapeDtypeStruct((bsz, HEAD_PAIRS, seq, LANES), F32),
                   jax.ShapeDtypeStruct((bsz, HEAD_PAIRS, nk, SUBLANES, tk), F32)],
        compiler_params=_params(("parallel", "parallel", "arbitrary")),
    )(qi_tab, kj_tab, qkv, qkv, qkv, do, o, lser, cumcol)


def fox_bwd(qkv, do, o, lser, cumcol, rest, bf, bsz, seq, f_blk):
    dq, dk, dv, dcp, dcq = fox_bwd_kernel(qkv, do, o, lser, cumcol, bsz, seq)
    df, dbf = forget_bwd(dcq, dcp, rest, bf, bsz, seq, f_blk)
    return dq, dk, dv, df, dbf


TAB_STEP = 0
TAB_FWD = 8
TAB_BWD = 16
TAB_ROWS = 24


def _ssm_tile(seq):
    return 256 if seq >= 1024 else 64


def _scan_block(xr, xi, tab_ref, re, im, cr, ci, reverse):
    row = lax.broadcasted_iota(jnp.int32, xr.shape, 0)
    sign = -1.0 if reverse else 1.0
    for n, s in enumerate((1, 2, 4)):
        ar = tab_ref[TAB_STEP + n:TAB_STEP + n + 1, re]
        ai = tab_ref[TAB_STEP + n:TAB_STEP + n + 1, im] * sign
        if reverse:
            keep = row < SUBLANES - s
            sr = jnp.where(keep, pltpu.roll(xr, SUBLANES - s, 0), 0.0)
            si = jnp.where(keep, pltpu.roll(xi, SUBLANES - s, 0), 0.0)
        else:
            keep = row >= s
            sr = jnp.where(keep, pltpu.roll(xr, s, 0), 0.0)
            si = jnp.where(keep, pltpu.roll(xi, s, 0), 0.0)
        xr, xi = xr + ar * sr - ai * si, xi + ar * si + ai * sr
    base = TAB_BWD if reverse else TAB_FWD
    pr = tab_ref[base:base + SUBLANES, re]
    pi = tab_ref[base:base + SUBLANES, im] * sign
    xr, xi = xr + pr * cr - pi * ci, xi + pr * ci + pi * cr
    return xr, xi


def ssm_fwd(rest, wb4, wc4, tabs, dskip, bsz, seq, u_blk):
    tt = _ssm_tile(seq)
    nt = seq // tt

    def body(u_ref, wb_ref, wc_ref, tab_ref, d_ref, y_ref, h_ref, carry):
        c = pl.program_id(1)

        @pl.when(c == 0)
        def _():
            carry[...] = jnp.zeros_like(carry)

        u = u_ref[...]
        ub = u.astype(BF16)
        for j in range(SSM_CHUNKS):
            h_ref[:, j * CHUNK_LANES:(j + 1) * CHUNK_LANES] = _dot(ub[:, j * LANES:(j + 1) * LANES], wb_ref[j])
        for j in range(SSM_CHUNKS):
            re = slice(j * CHUNK_LANES, j * CHUNK_LANES + CHUNK_STATES)
            im = slice(j * CHUNK_LANES + CHUNK_STATES, (j + 1) * CHUNK_LANES)

            def blk(bi, car):
                r0 = pl.multiple_of(bi * SUBLANES, SUBLANES)
                xr, xi = _scan_block(h_ref[pl.ds(r0, SUBLANES), re], h_ref[pl.ds(r0, SUBLANES), im],
                                     tab_ref, re, im, car[0], car[1], False)
                h_ref[pl.ds(r0, SUBLANES), re] = xr
                h_ref[pl.ds(r0, SUBLANES), im] = xi
                return xr[SUBLANES - 1:SUBLANES], xi[SUBLANES - 1:SUBLANES]

            cr, ci = lax.fori_loop(0, tt // SUBLANES, blk, (carry[0:1, re], carry[0:1, im]), unroll=2)
            carry[0:1, re] = cr
            carry[0:1, im] = ci
        for j in range(SSM_CHUNKS):
            hj = h_ref[:, j * CHUNK_LANES:(j + 1) * CHUNK_LANES].astype(BF16)
            cols = slice(j * LANES, (j + 1) * LANES)
            y_ref[:, cols] = _dot(hj, wc_ref[j]) + d_ref[:, cols] * u[:, cols]

    return pl.pallas_call(
        body, name="ssm_fwd", grid=(bsz, nt),
        in_specs=[pl.BlockSpec((tt, SSM_WIDTH), lambda b, c: (b * nt + c, u_blk)),
                  pl.BlockSpec((SSM_CHUNKS, LANES, CHUNK_LANES), lambda b, c: (0, 0, 0)),
                  pl.BlockSpec((SSM_CHUNKS, CHUNK_LANES, LANES), lambda b, c: (0, 0, 0)),
                  pl.BlockSpec((TAB_ROWS, STATE_LANES), lambda b, c: (0, 0)),
                  pl.BlockSpec((1, SSM_WIDTH), lambda b, c: (0, 0))],
        out_specs=[pl.BlockSpec((tt, SSM_WIDTH), lambda b, c: (b * nt + c, 0)),
                   pl.BlockSpec((tt, STATE_LANES), lambda b, c: (b * nt + c, 0))],
        out_shape=[jax.ShapeDtypeStruct((bsz * seq, SSM_WIDTH), F32),
                   jax.ShapeDtypeStruct((bsz * seq, STATE_LANES), F32)],
        scratch_shapes=[pltpu.VMEM((SUBLANES, STATE_LANES), F32)],
        compiler_params=_params(("parallel", "arbitrary")),
    )(rest, wb4, wc4, tabs, dskip)


def ssm_bwd(dys, rest, hs, wb4, wc4, tabs, dskip, bsz, seq, u_blk):
    tt = _ssm_tile(seq)
    nt = seq // tt
    nb = tt // SUBLANES

    def body(dy_ref, u_ref, h_ref, hp_ref, wb_ref, wc_ref, tab_ref, d_ref,
             du_ref, ga_ref, gwb_ref, gwc_ref, gd_ref, g_s, carry):
        b = pl.program_id(0)
        c = pl.program_id(1)

        @pl.when(c == 0)
        def _():
            carry[...] = jnp.zeros_like(carry)

        @pl.when((b == 0) & (c == 0))
        def _():
            ga_ref[...] = jnp.zeros_like(ga_ref)
            gwb_ref[...] = jnp.zeros_like(gwb_ref)
            gwc_ref[...] = jnp.zeros_like(gwc_ref)
            gd_ref[...] = jnp.zeros_like(gd_ref)

        dy = dy_ref[...].astype(F32)
        dyb = dy.astype(BF16)
        u = u_ref[...]
        ub = u.astype(BF16)
        first_chunk = c == nt - 1
        for j in range(SSM_CHUNKS):
            g_s[:, j * CHUNK_LANES:(j + 1) * CHUNK_LANES] = _dot(dyb[:, j * LANES:(j + 1) * LANES], wc_ref[j], NT)
        for j in range(SSM_CHUNKS):
            re = slice(j * CHUNK_LANES, j * CHUNK_LANES + CHUNK_STATES)
            im = slice(j * CHUNK_LANES + CHUNK_STATES, (j + 1) * CHUNK_LANES)
            row = lax.broadcasted_iota(jnp.int32, (SUBLANES, CHUNK_STATES), 0)

            def blk(n, car):
                bi = nb - 1 - n
                r0 = pl.multiple_of(bi * SUBLANES, SUBLANES)
                gr, gi = _scan_block(g_s[pl.ds(r0, SUBLANES), re], g_s[pl.ds(r0, SUBLANES), im],
                                     tab_ref, re, im, car[0], car[1], True)
                g_s[pl.ds(r0, SUBLANES), re] = gr
                g_s[pl.ds(r0, SUBLANES), im] = gi
                rp = pl.multiple_of(jnp.maximum(bi - 1, 0) * SUBLANES, SUBLANES)
                inside = bi > 0
                live = jnp.where(jnp.logical_or(inside, jnp.logical_not(first_chunk)), 1.0, 0.0)
                pr = jnp.where(inside, h_ref[pl.ds(rp, SUBLANES), re], hp_ref[:, re])[SUBLANES - 1:SUBLANES] * live
                pi = jnp.where(inside, h_ref[pl.ds(rp, SUBLANES), im], hp_ref[:, im])[SUBLANES - 1:SUBLANES] * live
                hr = jnp.where(row >= 1, pltpu.roll(h_ref[pl.ds(r0, SUBLANES), re], 1, 0), pr)
                hi = jnp.where(row >= 1, pltpu.roll(h_ref[pl.ds(r0, SUBLANES), im], 1, 0), pi)
                return (gr[0:1], gi[0:1], car[2] + gr * hr + gi * hi, car[3] + gi * hr - gr * hi)

            zero = jnp.zeros((SUBLANES, CHUNK_STATES), F32)
            cr, ci, sr, si = lax.fori_loop(0, nb, blk, (carry[0:1, re], carry[0:1, im], zero, zero), unroll=2)
            carry[0:1, re] = cr
            carry[0:1, im] = ci
            ga_ref[:, re] += sr
            ga_ref[:, im] += si
        for j in range(SSM_CHUNKS):
            cols = slice(j * LANES, (j + 1) * LANES)
            lanes = slice(j * CHUNK_LANES, (j + 1) * CHUNK_LANES)
            gj = g_s[:, lanes].astype(BF16)
            du_ref[:, cols] = (_dot(gj, wb_ref[j], NT) + d_ref[:, cols] * dy[:, cols]).astype(du_ref.dtype)
            gwb_ref[j] += _dot(ub[:, cols], gj, TN)
            gwc_ref[j] += _dot(h_ref[:, lanes].astype(BF16), dyb[:, cols], TN)
        gd_ref[...] += _fold8(dy * u)

    def prev_rows(b, c):
        chunk = nt - 1 - c
        return (jnp.maximum((b * nt + chunk) * nb - 1, 0), 0)

    return pl.pallas_call(
        body, name="ssm_bwd", grid=(bsz, nt),
        in_specs=[pl.BlockSpec((tt, SSM_WIDTH), lambda b, c: (b * nt + nt - 1 - c, 0)),
                  pl.BlockSpec((tt, SSM_WIDTH), lambda b, c: (b * nt + nt - 1 - c, u_blk)),
                  pl.BlockSpec((tt, STATE_LANES), lambda b, c: (b * nt + nt - 1 - c, 0)),
                  pl.BlockSpec((SUBLANES, STATE_LANES), prev_rows),
                  pl.BlockSpec((SSM_CHUNKS, LANES, CHUNK_LANES), lambda b, c: (0, 0, 0)),
                  pl.BlockSpec((SSM_CHUNKS, CHUNK_LANES, LANES), lambda b, c: (0, 0, 0)),
                  pl.BlockSpec((TAB_ROWS, STATE_LANES), lambda b, c: (0, 0)),
                  pl.BlockSpec((1, SSM_WIDTH), lambda b, c: (0, 0))],
        out_specs=[pl.BlockSpec((tt, SSM_WIDTH), lambda b, c: (b * nt + nt - 1 - c, 0)),
                   pl.BlockSpec((SUBLANES, STATE_LANES), lambda b, c: (0, 0)),
                   pl.BlockSpec((SSM_CHUNKS, LANES, CHUNK_LANES), lambda b, c: (0, 0, 0)),
                   pl.BlockSpec((SSM_CHUNKS, CHUNK_LANES, LANES), lambda b, c: (0, 0, 0)),
                   pl.BlockSpec((SUBLANES, SSM_WIDTH), lambda b, c: (0, 0))],
        out_shape=[jax.ShapeDtypeStruct((bsz * seq, SSM_WIDTH), BF16),
                   jax.ShapeDtypeStruct((SUBLANES, STATE_LANES), F32),
                   jax.ShapeDtypeStruct((SSM_CHUNKS, LANES, CHUNK_LANES), F32),
                   jax.ShapeDtypeStruct((SSM_CHUNKS, CHUNK_LANES, LANES), F32),
                   jax.ShapeDtypeStruct((SUBLANES, SSM_WIDTH), F32)],
        scratch_shapes=[pltpu.VMEM((tt, STATE_LANES), F32), pltpu.VMEM((SUBLANES, STATE_LANES), F32)],
        compiler_params=_params(("arbitrary", "arbitrary")),
    )(dys, rest, hs, hs, wb4, wc4, tabs, dskip)


def _gelu(v):
    t = jnp.tanh(GELU_C * (v + GELU_A * v * v * v))
    return 0.5 * v * (1.0 + t), t


def mix_fwd(ya, ys, rest, x0, wglu, bglu, wba, wbb, wout):
    t, d = x0.shape
    tm = _rows(t, 256)

    def body(ya_ref, ys_ref, ga_ref, gb_ref, x_ref, wg_ref, bg_ref, wa_ref, wb_ref, wo_ref,
             x1_ref, z_ref, pa_ref, pb_ref, yb_ref, yb2_ref, mx_ref):
        yb, _ = _gelu(ys_ref[...])
        ybb = yb.astype(BF16)
        z = _dot(ybb, wg_ref[...]) + bg_ref[...]
        yb2 = (yb * _sigmoid(z)).astype(BF16)
        pa = _dot(ya_ref[...], wa_ref[...])
        pb = _dot(yb2, wb_ref[...])
        mixed = (_sigmoid(ga_ref[...]) * pa + _sigmoid(gb_ref[...]) * pb).astype(BF16)
        x1_ref[...] = x_ref[...] + _dot(mixed, wo_ref[...])
        z_ref[...] = z.astype(z_ref.dtype)
        pa_ref[...] = pa.astype(pa_ref.dtype)
        pb_ref[...] = pb.astype(pb_ref.dtype)
        yb_ref[...] = ybb
        yb2_ref[...] = yb2
        mx_ref[...] = mixed

    row = lambda w: pl.BlockSpec((tm, w), lambda i: (i, 0))
    full = lambda a: pl.BlockSpec(a.shape, lambda i: (0,) * a.ndim)
    return pl.pallas_call(
        body, name="mix_fwd", grid=(t // tm,),
        in_specs=[row(ATTN_WIDTH), row(SSM_WIDTH),
                  pl.BlockSpec((tm, d), lambda i: (i, 0)), pl.BlockSpec((tm, d), lambda i: (i, 1)),
                  row(d), full(wglu), full(bglu), full(wba), full(wbb), full(wout)],
        out_specs=[row(d), row(SSM_WIDTH), row(d), row(d), row(SSM_WIDTH), row(SSM_WIDTH), row(d)],
        out_shape=[jax.ShapeDtypeStruct((t, d), F32), jax.ShapeDtypeStruct((t, SSM_WIDTH), BF16),
                   jax.ShapeDtypeStruct((t, d), BF16), jax.ShapeDtypeStruct((t, d), BF16),
                   jax.ShapeDtypeStruct((t, SSM_WIDTH), BF16), jax.ShapeDtypeStruct((t, SSM_WIDTH), BF16),
                   jax.ShapeDtypeStruct((t, d), BF16)],
        compiler_params=_params(("parallel",)),
    )(ya, ys, rest, rest, x0, wglu, bglu, wba, wbb, wout)


def mix_bwd(dx1, rest, pa, pb, z, ys, wglu, wba, wbb, wout):
    t, d = dx1.shape
    tm = _rows(t, 256)

    def body(dx_ref, ga_ref, gb_ref, pa_ref, pb_ref, z_ref, ys_ref, wg_ref, wa_ref, wb_ref, wo_ref,
             dya_ref, dys_ref, dg_ref, dpa_ref, dpb_ref, dz_ref, dbg_ref):
        @pl.when(pl.program_id(0) == 0)
        def _():
            dbg_ref[...] = jnp.zeros_like(dbg_ref)

        dmix = _dot(dx_ref[...].astype(BF16), wo_ref[...], NT)
        sa = _sigmoid(ga_ref[...])
        sb = _sigmoid(gb_ref[...])
        dpa = (dmix * sa).astype(BF16)
        dpb = (dmix * sb).astype(BF16)
        dg_ref[:, 0:d] = (dmix * pa_ref[...].astype(F32) * sa * (1.0 - sa)).astype(dg_ref.dtype)
        dg_ref[:, d:2 * d] = (dmix * pb_ref[...].astype(F32) * sb * (1.0 - sb)).astype(dg_ref.dtype)
        dpa_ref[...] = dpa
        dpb_ref[...] = dpb
        dya_ref[...] = _dot(dpa, wa_ref[...], NT).astype(dya_ref.dtype)
        dyb2 = _dot(dpb, wb_ref[...], NT)
        ys = ys_ref[...]
        yb, th = _gelu(ys)
        sg = _sigmoid(z_ref[...].astype(F32))
        dz = dyb2 * yb * sg * (1.0 - sg)
        dzb = dz.astype(BF16)
        dz_ref[...] = dzb
        dbg_ref[...] += _fold8(dz)
        dyb = dyb2 * sg + _dot(dzb, wg_ref[...], NT)
        dgelu = 0.5 * (1.0 + th) + 0.5 * ys * (1.0 - th * th) * GELU_C * (1.0 + 3.0 * GELU_A * ys * ys)
        dys_ref[...] = (dyb * dgelu).astype(dys_ref.dtype)

    row = lambda w: pl.BlockSpec((tm, w), lambda i: (i, 0))
    full = lambda a: pl.BlockSpec(a.shape, lambda i: (0,) * a.ndim)
    return pl.pallas_call(
        body, name="mix_bwd", grid=(t // tm,),
        in_specs=[row(d), pl.BlockSpec((tm, d), lambda i: (i, 0)), pl.BlockSpec((tm, d), lambda i: (i, 1)),
                  row(d), row(d), row(SSM_WIDTH), row(SSM_WIDTH), full(wglu), full(wba), full(wbb), full(wout)],
        out_specs=[row(ATTN_WIDTH), row(SSM_WIDTH), row(2 * d), row(d), row(d), row(SSM_WIDTH),
                   pl.BlockSpec((SUBLANES, SSM_WIDTH), lambda i: (0, 0))],
        out_shape=[jax.ShapeDtypeStruct((t, ATTN_WIDTH), BF16), jax.ShapeDtypeStruct((t, SSM_WIDTH), BF16),
                   jax.ShapeDtypeStruct((t, 2 * d), BF16), jax.ShapeDtypeStruct((t, d), BF16),
                   jax.ShapeDtypeStruct((t, d), BF16), jax.ShapeDtypeStruct((t, SSM_WIDTH), BF16),
                   jax.ShapeDtypeStruct((SUBLANES, SSM_WIDTH), F32)],
        compiler_params=_params(("arbitrary",)),
    )(dx1, rest, rest, pa, pb, z, ys, wglu, wba, wbb, wout)


def mlp_fwd(x1, g, wup, wdown):
    t, d = x1.shape
    ff = wup.shape[1]
    tm, tf = _rows(t, 512), _pick(ff, 1024)
    nf = ff // tf

    def body(x_ref, g_ref, wu_ref, wd_ref, x2_ref, up_ref, h_s, acc_s):
        f = pl.program_id(1)

        @pl.when(f == 0)
        def _():
            xv = x_ref[...]
            h_s[...] = (xv * _rms_scale(xv) * g_ref[...]).astype(BF16)
            acc_s[...] = jnp.zeros_like(acc_s)

        up = _dot(h_s[...], wu_ref[...])
        up_ref[...] = up.astype(up_ref.dtype)
        act = jnp.square(jnp.maximum(up, 0.0)).astype(BF16)
        acc_s[...] += _dot(act, wd_ref[...])

        @pl.when(f == nf - 1)
        def _():
            x2_ref[...] = x_ref[...] + acc_s[...]

    return pl.pallas_call(
        body, name="mlp_fwd", grid=(t // tm, nf),
        in_specs=[pl.BlockSpec((tm, d), lambda i, f: (i, 0)), pl.BlockSpec((1, d), lambda i, f: (0, 0)),
                  pl.BlockSpec((d, tf), lambda i, f: (0, f)), pl.BlockSpec((tf, d), lambda i, f: (f, 0))],
        out_specs=[pl.BlockSpec((tm, d), lambda i, f: (i, 0)), pl.BlockSpec((tm, tf), lambda i, f: (i, f))],
        out_shape=[jax.ShapeDtypeStruct((t, d), F32), jax.ShapeDtypeStruct((t, ff), BF16)],
        scratch_shapes=[pltpu.VMEM((tm, d), BF16), pltpu.VMEM((tm, d), F32)],
        compiler_params=_params(("parallel", "arbitrary")),
    )(x1, g, wup, wdown)


def mlp_bwd(dx2, up, x1, g, wup, wdown):
    t, d = x1.shape
    ff = wup.shape[1]
    tm, tf = _rows(t, 512), _pick(ff, 1024)
    nf = ff // tf

    def body(dx_ref, up_ref, x_ref, g_ref, wu_ref, wd_ref, dup_ref, dx1_ref, dg_ref, dxb_s, acc_s):
        i = pl.program_id(0)
        f = pl.program_id(1)

        @pl.when((i == 0) & (f == 0))
        def _():
            dg_ref[...] = jnp.zeros_like(dg_ref)

        @pl.when(f == 0)
        def _():
            dxb_s[...] = dx_ref[...].astype(BF16)
            acc_s[...] = jnp.zeros_like(acc_s)

        dact = _dot(dxb_s[...], wd_ref[...], NT)
        dup = (dact * 2.0 * jnp.maximum(up_ref[...].astype(F32), 0.0)).astype(BF16)
        dup_ref[...] = dup
        acc_s[...] += _dot(dup, wu_ref[...], NT)

        @pl.when(f == nf - 1)
        def _():
            dxn, dgain = _rms_bwd(x_ref[...], g_ref[...], acc_s[...])
            dx1_ref[...] = dx_ref[...] + dxn
            dg_ref[...] += _fold8(dgain)

    return pl.pallas_call(
        body, name="mlp_bwd", grid=(t // tm, nf),
        in_specs=[pl.BlockSpec((tm, d), lambda i, f: (i, 0)), pl.BlockSpec((tm, tf), lambda i, f: (i, f)),
                  pl.BlockSpec((tm, d), lambda i, f: (i, 0)), pl.BlockSpec((1, d), lambda i, f: (0, 0)),
                  pl.BlockSpec((d, tf), lambda i, f: (0, f)), pl.BlockSpec((tf, d), lambda i, f: (f, 0))],
        out_specs=[pl.BlockSpec((tm, tf), lambda i, f: (i, f)), pl.BlockSpec((tm, d), lambda i, f: (i, 0)),
                   pl.BlockSpec((SUBLANES, d), lambda i, f: (0, 0))],
        out_shape=[jax.ShapeDtypeStruct((t, ff), BF16), jax.ShapeDtypeStruct((t, d), F32),
                   jax.ShapeDtypeStruct((SUBLANES, d), F32)],
        scratch_shapes=[pltpu.VMEM((tm, d), BF16), pltpu.VMEM((tm, d), F32)],
        compiler_params=_params(("arbitrary", "arbitrary")),
    )(dx2, up, x1, g, wup, wdown)


def proj_bwd(dproj, wpad, x0, dx1, g):
    t, d = x0.shape
    m = wpad.shape[1]
    tm = _rows(t, 256)

    def body(dp_ref, w_ref, x_ref, dx1_ref, g_ref, dx0_ref, dg_ref):
        @pl.when(pl.program_id(0) == 0)
        def _():
            dg_ref[...] = jnp.zeros_like(dg_ref)

        dh = _dot(dp_ref[...], w_ref[...], NT)
        dxn, dgain = _rms_bwd(x_ref[...], g_ref[...], dh)
        dx0_ref[...] = dx1_ref[...] + dxn
        dg_ref[...] += _fold8(dgain)

    return pl.pallas_call(
        body, name="proj_bwd", grid=(t // tm,),
        in_specs=[pl.BlockSpec((tm, m), lambda i: (i, 0)), pl.BlockSpec((d, m), lambda i: (0, 0)),
                  pl.BlockSpec((tm, d), lambda i: (i, 0)), pl.BlockSpec((tm, d), lambda i: (i, 0)),
                  pl.BlockSpec((1, d), lambda i: (0, 0))],
        out_specs=[pl.BlockSpec((tm, d), lambda i: (i, 0)), pl.BlockSpec((SUBLANES, d), lambda i: (0, 0))],
        out_shape=[jax.ShapeDtypeStruct((t, d), F32), jax.ShapeDtypeStruct((SUBLANES, d), F32)],
        compiler_params=_params(("arbitrary",)),
    )(dproj, wpad, x0, dx1, g)


def final_loss(x, g, target):
    t, d = x.shape
    tm = _rows(t, 512)

    def body(x_ref, g_ref, t_ref, dx_ref, ls_ref, dg_ref):
        @pl.when(pl.program_id(0) == 0)
        def _():
            ls_ref[...] = jnp.zeros_like(ls_ref)
            dg_ref[...] = jnp.zeros_like(dg_ref)

        xv = x_ref[...]
        gv = g_ref[...]
        err = xv * _rms_scale(xv) * gv - t_ref[...]
        ls_ref[...] += _fold8(err * err) * (0.5 / d)
        dxn, dgain = _rms_bwd(xv, gv, err * (1.0 / d))
        dx_ref[...] = dxn
        dg_ref[...] += _fold8(dgain)

    return pl.pallas_call(
        body, name="final_loss", grid=(t // tm,),
        in_specs=[pl.BlockSpec((tm, d), lambda i: (i, 0)), pl.BlockSpec((1, d), lambda i: (0, 0)),
                  pl.BlockSpec((tm, d), lambda i: (i, 0))],
        out_specs=[pl.BlockSpec((tm, d), lambda i: (i, 0)), pl.BlockSpec((SUBLANES, d), lambda i: (0, 0)),
                   pl.BlockSpec((SUBLANES, d), lambda i: (0, 0))],
        out_shape=[jax.ShapeDtypeStruct((t, d), F32), jax.ShapeDtypeStruct((SUBLANES, d), F32),
                   jax.ShapeDtypeStruct((SUBLANES, d), F32)],
        compiler_params=_params(("arbitrary",)),
    )(x, g, target)


def _place():
    x, y, c = lax.axis_index("x"), lax.axis_index("y"), lax.axis_index("c")
    chips = [(1 - x, y), (x, 1 - y), (1 - x, 1 - y)]
    return x, y, c, chips


def _exchange_call(body, name, ins, out_shapes, copies_per_tensor):
    n = len(ins)
    scratch = [pltpu.SemaphoreType.DMA((copies_per_tensor * n,)), pltpu.SemaphoreType.DMA((copies_per_tensor * n,))]
    return pl.pallas_call(
        body, name=name, in_specs=[ANY] * n, out_specs=[ANY] * n, out_shape=out_shapes, scratch_shapes=scratch,
    )(*ins)


def gather_weights(ws):
    n = len(ws)
    lh = ws[0].shape[0] // 2

    def body(*refs):
        w_refs, out_refs = refs[:n], refs[n:2 * n]
        send_sems, recv_sems = refs[2 * n:]
        x, y, c, chips = _place()
        me = 2 * x + y
        sibling = (x, y, 1 - c)

        def half(t, chip_idx, core):
            return out_refs[t].at[chip_idx, pl.ds(core * lh, lh)]

        def copy(t, k, src, dst, to):
            return pltpu.make_async_remote_copy(src_ref=src, dst_ref=dst, send_sem=send_sems.at[6 * t + k],
                                                recv_sem=recv_sems.at[6 * t + k], device_id=to, device_id_type=MESH)

        first = [copy(t, j, w_refs[t].at[pl.ds(c * lh, lh)], half(t, me, c), (*chip, c))
                 for j, chip in enumerate(chips) for t in range(n)]
        for cp in first:
            cp.start()
        passed = []
        for j, (cx, cy) in enumerate(chips):
            idx = 2 * cx + cy
            for t in range(n):
                copy(t, j, half(t, idx, c), half(t, idx, c), (x, y, c)).wait_recv()
                fwd = copy(t, 3 + j, half(t, idx, c), half(t, idx, c), sibling)
                fwd.start()
                passed.append(fwd)
        for j, (cx, cy) in enumerate(chips):
            idx = 2 * cx + cy
            for t in range(n):
                copy(t, 3 + j, half(t, idx, 1 - c), half(t, idx, 1 - c), (x, y, c)).wait_recv()
        for cp in first + passed:
            cp.wait_send()

    return _exchange_call(body, "gather_weights", ws,
                          [jax.ShapeDtypeStruct((4,) + w.shape, w.dtype) for w in ws], 6)


def exchange_sibling_half(gs):
    n = len(gs)
    lh = gs[0].shape[1] // 2

    def body(*refs):
        g_refs, out_refs = refs[:n], refs[n:2 * n]
        send_sems, recv_sems = refs[2 * n:]
        x, y, c, _ = _place()
        cps = [pltpu.make_async_remote_copy(src_ref=g_refs[t].at[:, pl.ds((1 - c) * lh, lh)], dst_ref=out_refs[t],
                                            send_sem=send_sems.at[t], recv_sem=recv_sems.at[t],
                                            device_id=(x, y, 1 - c), device_id_type=MESH) for t in range(n)]
        for cp in cps:
            cp.start()
        for cp in cps:
            cp.wait()

    return _exchange_call(body, "exchange_sibling_half", gs,
                          [jax.ShapeDtypeStruct((4, lh) + g.shape[2:], g.dtype) for g in gs], 1)


def scatter_to_owners(ps):
    n = len(ps)

    def body(*refs):
        a_refs, out_refs = refs[:n], refs[n:2 * n]
        send_sems, recv_sems = refs[2 * n:]
        x, y, c, chips = _place()
        sends = [pltpu.make_async_remote_copy(src_ref=a_refs[t].at[2 * cx + cy], dst_ref=out_refs[t].at[j],
                                              send_sem=send_sems.at[3 * t + j], recv_sem=recv_sems.at[3 * t + j],
                                              device_id=(cx, cy, c), device_id_type=MESH)
                 for j, (cx, cy) in enumerate(chips) for t in range(n)]
        for cp in sends:
            cp.start()
        for cp in sends:
            cp.wait()

    return _exchange_call(body, "scatter_to_owners", ps,
                          [jax.ShapeDtypeStruct((3,) + p.shape[1:], p.dtype) for p in ps], 3)


def share_with_sibling(rs):
    n = len(rs)

    def body(*refs):
        r_refs, out_refs = refs[:n], refs[n:2 * n]
        send_sems, recv_sems = refs[2 * n:]
        x, y, c, _ = _place()
        cps = [pltpu.make_async_remote_copy(src_ref=r_refs[t], dst_ref=out_refs[t],
                                            send_sem=send_sems.at[t], recv_sem=recv_sems.at[t],
                                            device_id=(x, y, 1 - c), device_id_type=MESH) for t in range(n)]
        for cp in cps:
            cp.start()
        for cp in cps:
            cp.wait()

    return _exchange_call(body, "share_with_sibling", rs,
                          [jax.ShapeDtypeStruct(r.shape, r.dtype) for r in rs], 1)


def exchange_all(sp):
    def body(s_ref, out_ref, send_sems, recv_sems):
        x, y, c, _ = _place()
        me = 4 * x + 2 * y + c
        sends = []
        for m in range(1, 8):
            fx, fy, fc = (m >> 2) & 1, (m >> 1) & 1, m & 1
            peer = ((1 - x) if fx else x, (1 - y) if fy else y, (1 - c) if fc else c)
            cp = pltpu.make_async_remote_copy(src_ref=s_ref, dst_ref=out_ref.at[me],
                                              send_sem=send_sems.at[m - 1], recv_sem=recv_sems.at[m - 1],
                                              device_id=peer, device_id_type=MESH)
            cp.start()
            sends.append(cp)
        for cp in sends:
            cp.wait()

    return pl.pallas_call(
        body, name="exchange_all",
        in_specs=[ANY], out_specs=ANY,
        out_shape=jax.ShapeDtypeStruct((8,) + sp.shape, sp.dtype),
        scratch_shapes=[pltpu.SemaphoreType.DMA((7,)), pltpu.SemaphoreType.DMA((7,))],
    )(sp)


def add_sibling(gp, s1, core, name):
    _, nl, r, cdim = gp.shape
    lh = nl // 2
    tr = _rows(r, PACK_ROW_TILE)

    def body(c_ref, g_ref, s_ref, o_ref):
        o_ref[...] = (g_ref[...] + s_ref[...]).astype(o_ref.dtype)

    return pl.pallas_call(
        body, name=name,
        grid_spec=pltpu.PrefetchScalarGridSpec(
            num_scalar_prefetch=1, grid=(4, lh, r // tr),
            in_specs=[pl.BlockSpec((1, 1, tr, cdim), lambda k, l, i, c_ref: (k, c_ref[0] * lh + l, i, 0)),
                      pl.BlockSpec((1, 1, tr, cdim), lambda k, l, i, c_ref: (k, l, i, 0))],
            out_specs=pl.BlockSpec((1, 1, tr, cdim), lambda k, l, i, c_ref: (k, l, i, 0))),
        out_shape=jax.ShapeDtypeStruct(s1.shape, BF16),
        compiler_params=_params(("parallel", "parallel", "parallel")),
    )(core, gp, s1)


def add_chips(s2, pair, chip, name):
    _, lh, r, cdim = s2.shape
    tr = _rows(r, PACK_ROW_TILE)

    def body(k_ref, s_ref, p_ref, o_ref):
        o_ref[0] = ((s_ref[0, 0].astype(F32) + s_ref[1, 0].astype(F32)) + s_ref[2, 0].astype(F32)) + p_ref[0, 0].astype(F32)

    return pl.pallas_call(
        body, name=name,
        grid_spec=pltpu.PrefetchScalarGridSpec(
            num_scalar_prefetch=1, grid=(lh, r // tr),
            in_specs=[pl.BlockSpec((3, 1, tr, cdim), lambda l, i, k_ref: (0, l, i, 0)),
                      pl.BlockSpec((1, 1, tr, cdim), lambda l, i, k_ref: (k_ref[0], l, i, 0))],
            out_specs=pl.BlockSpec((1, tr, cdim), lambda l, i, k_ref: (l, i, 0))),
        out_shape=jax.ShapeDtypeStruct(s2.shape[1:], F32),
        compiler_params=_params(("parallel", "parallel")),
    )(chip, s2, pair)


def _adamw_math(w, g, m, v):
    m = ADAM_B1 * m + (1.0 - ADAM_B1) * g
    v = ADAM_B2 * v + (1.0 - ADAM_B2) * (g * g)
    m_hat = m / (1.0 - ADAM_B1 ** ADAM_STEP)
    v_hat = v / (1.0 - ADAM_B2 ** ADAM_STEP)
    delta = -ADAM_LR * (m_hat / (jnp.sqrt(v_hat) + ADAM_EPS) + ADAM_WD * w)
    return delta, m, v


def adamw(w, g_own, g_sibling, m, v, core, name):
    nl, r, cdim = w.shape
    lh = nl // 2
    tr = _rows(r, PACK_ROW_TILE)

    def body(c_ref, w_ref, go_ref, gs_ref, m_ref, v_ref, g_ref, d_ref, nm_ref, nv_ref):
        own = pl.program_id(0) // lh == c_ref[0]
        g = jnp.where(own, go_ref[...], gs_ref[...])
        d, nm, nv = _adamw_math(w_ref[...], g, m_ref[...], v_ref[...])
        g_ref[...] = g
        d_ref[...] = d
        nm_ref[...] = nm
        nv_ref[...] = nv

    spec = pl.BlockSpec((1, tr, cdim), lambda l, i, c_ref: (l, i, 0))
    own_spec = pl.BlockSpec((1, tr, cdim), lambda l, i, c_ref: (jnp.clip(l - c_ref[0] * lh, 0, lh - 1), i, 0))
    sib_spec = pl.BlockSpec((1, tr, cdim), lambda l, i, c_ref: (jnp.clip(l - (1 - c_ref[0]) * lh, 0, lh - 1), i, 0))
    return pl.pallas_call(
        body, name=name,
        grid_spec=pltpu.PrefetchScalarGridSpec(
            num_scalar_prefetch=1, grid=(nl, r // tr),
            in_specs=[spec, own_spec, sib_spec, spec, spec], out_specs=[spec] * 4),
        out_shape=[jax.ShapeDtypeStruct(w.shape, F32)] * 4,
        compiler_params=_params(("parallel", "parallel")),
    )(core, w, g_own, g_sibling, m, v)


def sum_and_adamw(parts, own, me, w, m, v):
    _, r, cdim = parts.shape
    tr = _rows(r, PACK_ROW_TILE)

    def body(me_ref, p_ref, o_ref, w_ref, m_ref, v_ref, g_ref, d_ref, nm_ref, nv_ref):
        part = lambda k: jnp.where(me_ref[0] == k, o_ref[...], p_ref[k])
        g = part(0)
        for k in range(1, 8):
            g = g + part(k)
        d, nm, nv = _adamw_math(w_ref[...], g, m_ref[...], v_ref[...])
        g_ref[...] = g
        d_ref[...] = d
        nm_ref[...] = nm
        nv_ref[...] = nv

    spec = pl.BlockSpec((tr, cdim), lambda i, me_ref: (i, 0))
    return pl.pallas_call(
        body, name="sum_and_adamw",
        grid_spec=pltpu.PrefetchScalarGridSpec(
            num_scalar_prefetch=1, grid=(r // tr,),
            in_specs=[pl.BlockSpec((8, tr, cdim), lambda i, me_ref: (0, i, 0)), spec, spec, spec, spec],
            out_specs=[spec] * 4),
        out_shape=[jax.ShapeDtypeStruct((r, cdim), F32)] * 4,
        compiler_params=_params(("parallel",)),
    )(me, parts, own, w, m, v)


SHARDED = ("w_in", "w_glu", "w_branch_a", "w_branch_b", "w_out", "w_mlp_up", "w_mlp_down")
SHARD_AXIS = {"w_in": 2, "w_glu": 1, "w_branch_a": 2, "w_branch_b": 2, "w_out": 1, "w_mlp_up": 2, "w_mlp_down": 1}
SMALL = ("norm_mix", "b_forget", "ssm_lambda_re", "ssm_lambda_im", "ssm_log_dt", "ssm_b_re", "ssm_b_im",
         "ssm_c_re", "ssm_c_im", "ssm_d", "b_glu", "norm_mlp", "norm_final")


def pack_flat(arrs):
    flat = jnp.concatenate([a.reshape(-1).astype(F32) for a in arrs])
    unit = PACK_COLS * PACK_ROW_TILE
    rows = (flat.shape[0] + unit - 1) // unit * PACK_ROW_TILE
    return jnp.pad(flat, (0, rows * PACK_COLS - flat.shape[0])).reshape(rows, PACK_COLS)


def unpack_flat(packed, shapes):
    flat = packed.reshape(-1)
    out, off = [], 0
    for s in shapes:
        n = math.prod(s)
        out.append(flat[off:off + n].reshape(tuple(s)))
        off += n
    return out


def _discretise(lam_re, lam_im, log_dt, b_re, b_im):
    dt = jnp.exp(log_dt)[:, None]
    mag = jnp.exp(lam_re * dt)
    ar = mag * jnp.cos(lam_im * dt)
    ai = mag * jnp.sin(lam_im * dt)
    den = lam_re * lam_re + lam_im * lam_im
    cr = ((ar - 1.0) * lam_re + ai * lam_im) / den
    ci = (ai * lam_re - (ar - 1.0) * lam_im) / den
    bbr = cr[:, :, None] * b_re - ci[:, :, None] * b_im
    bbi = cr[:, :, None] * b_im + ci[:, :, None] * b_re
    return ar, ai, bbr, bbi


def _state_lanes(re, im):
    return jnp.concatenate([re.reshape(SSM_CHUNKS, CHUNK_STATES), im.reshape(SSM_CHUNKS, CHUNK_STATES)],
                           axis=1).reshape(STATE_LANES)


def _ssm_inputs(ar, ai, bbr, bbi, c_re, c_im):
    pr, pi = [ar], [ai]
    for _ in range(SUBLANES - 1):
        pr, pi = pr + [pr[-1] * ar - pi[-1] * ai], pi + [pr[-1] * ai + pi[-1] * ar]
    power = lambda n: _state_lanes(pr[n - 1], pi[n - 1])
    zero = jnp.zeros((STATE_LANES,), F32)
    rows = [power(1), power(2), power(4)] + [zero] * 5
    rows += [power(i + 1) for i in range(SUBLANES)]
    rows += [power(SUBLANES - i) for i in range(SUBLANES)]
    tabs = jnp.stack(rows)
    eye = jnp.eye(CHUNK_GROUPS, dtype=F32)

    def to_wb(bb):
        t = bb.reshape(SSM_CHUNKS, CHUNK_GROUPS, SSM_STATE, SSM_GROUP_CH).transpose(0, 1, 3, 2)
        return jnp.einsum("jgcp,gh->jgchp", t, eye).reshape(SSM_CHUNKS, LANES, CHUNK_STATES)

    def to_wc(cc):
        t = cc.reshape(SSM_CHUNKS, CHUNK_GROUPS, SSM_GROUP_CH, SSM_STATE)
        return jnp.einsum("jgcp,gh->jhpgc", t, eye).reshape(SSM_CHUNKS, CHUNK_STATES, LANES)

    wb4 = jnp.concatenate([to_wb(bbr), to_wb(bbi)], axis=2).astype(BF16)
    wc4 = jnp.concatenate([to_wc(c_re), -to_wc(c_im)], axis=1).astype(BF16)
    return tabs, wb4, wc4


def _ssm_param_grads(ga8, gwb, gwc):
    eye = jnp.eye(CHUNK_GROUPS, dtype=F32)
    ga = jnp.sum(ga8, axis=0).reshape(SSM_CHUNKS, 2, CHUNK_STATES)
    gar = ga[:, 0].reshape(SSM_GROUPS, SSM_STATE)
    gai = ga[:, 1].reshape(SSM_GROUPS, SSM_STATE)

    def from_wb(g):
        t = g.reshape(SSM_CHUNKS, CHUNK_GROUPS, SSM_GROUP_CH, CHUNK_GROUPS, SSM_STATE)
        return jnp.einsum("jgchp,gh->jgpc", t, eye).reshape(SSM_GROUPS, SSM_STATE, SSM_GROUP_CH)

    def from_wc(g):
        t = g.reshape(SSM_CHUNKS, CHUNK_GROUPS, SSM_STATE, CHUNK_GROUPS, SSM_GROUP_CH)
        return jnp.einsum("jhpgc,gh->jgcp", t, eye).reshape(SSM_GROUPS, SSM_GROUP_CH, SSM_STATE)

    return (gar, gai, from_wb(gwb[:, :, :CHUNK_STATES]), from_wb(gwb[:, :, CHUNK_STATES:]),
            from_wc(gwc[:, :CHUNK_STATES]), -from_wc(gwc[:, CHUNK_STATES:]))


def kernel(x, norm_mix, w_in, b_forget, ssm_lambda_re, ssm_lambda_im, ssm_log_dt, ssm_b_re, ssm_b_im, ssm_c_re, ssm_c_im, ssm_d, w_glu, b_glu, w_branch_a, w_branch_b, w_out, norm_mlp, w_mlp_up, w_mlp_down, norm_final, loss_target, m_norm_mix, m_w_in, m_b_forget, m_ssm_lambda_re, m_ssm_lambda_im, m_ssm_log_dt, m_ssm_b_re, m_ssm_b_im, m_ssm_c_re, m_ssm_c_im, m_ssm_d, m_w_glu, m_b_glu, m_w_branch_a, m_w_branch_b, m_w_out, m_norm_mlp, m_w_mlp_up, m_w_mlp_down, m_norm_final, v_norm_mix, v_w_in, v_b_forget, v_ssm_lambda_re, v_ssm_lambda_im, v_ssm_log_dt, v_ssm_b_re, v_ssm_b_im, v_ssm_c_re, v_ssm_c_im, v_ssm_d, v_w_glu, v_b_glu, v_w_branch_a, v_w_branch_b, v_w_out, v_norm_mlp, v_w_mlp_up, v_w_mlp_down, v_norm_final):
    args = dict(locals())
    bsz, seq, d = x.shape
    nl = norm_mix.shape[0]
    tokens = bsz * seq
    aw, sw = ATTN_WIDTH, SSM_WIDTH
    core = lax.axis_index("c").astype(jnp.int32).reshape(1)
    chip = (2 * lax.axis_index("x") + lax.axis_index("y")).astype(jnp.int32).reshape(1)
    device = 2 * chip + core

    own = {n: args[n].astype(BF16) for n in SHARDED}
    gathered = dict(zip(SHARDED, gather_weights([own[n] for n in SHARDED])))
    full = {n: jnp.concatenate([jnp.where(chip[0] == k, own[n], gathered[n][k]) for k in range(4)],
                               axis=SHARD_AXIS[n]) for n in SHARDED}
    win_f = full["w_in"]
    o_f, o_u, o_ga, o_gb = 3 * aw, 3 * aw + ATTN_HEADS, 3 * aw + ATTN_HEADS + sw, 3 * aw + ATTN_HEADS + sw + d
    w_qkv = win_f[:, :, :o_f]
    w_rest = jnp.concatenate([win_f[:, :, o_ga:o_gb], win_f[:, :, o_gb:], win_f[:, :, o_u:o_ga],
                              jnp.pad(win_f[:, :, o_f:o_u], ((0, 0), (0, 0), (0, F_PAD - ATTN_HEADS)))], axis=2)
    w_pad = jnp.concatenate([w_qkv, w_rest], axis=2)
    u_blk = 2 * d // sw
    f_blk = (2 * d + sw) // F_PAD
    bf_pad = jnp.pad(b_forget, ((0, 0), (0, F_PAD - ATTN_HEADS)))

    disc = [jax.vjp(_discretise, ssm_lambda_re[l], ssm_lambda_im[l], ssm_log_dt[l], ssm_b_re[l], ssm_b_im[l])
            for l in range(nl)]

    xs = x.reshape(tokens, d)
    saved = []
    for l in range(nl):
        g1 = norm_mix[l].reshape(1, d)
        qkv = norm_matmul(xs, g1, w_qkv[l], BF16, "proj_qkv")
        rest = norm_matmul(xs, g1, w_rest[l], F32, "proj_rest")
        cumcol, cumrow = forget_cumsum(rest, bf_pad[l:l + 1], bsz, seq, f_blk)
        ya, lser = fox_fwd(qkv, cumrow, bsz, seq)
        tabs, wb4, wc4 = _ssm_inputs(*disc[l][0], ssm_c_re[l], ssm_c_im[l])
        dskip = ssm_d[l].reshape(1, sw)
        ys, hs = ssm_fwd(rest, wb4, wc4, tabs, dskip, bsz, seq, u_blk)
        x1, z, pa, pb, yb, yb2, mixed = mix_fwd(ya, ys, rest, xs, full["w_glu"][l], b_glu[l].reshape(1, sw),
                                                 full["w_branch_a"][l], full["w_branch_b"][l], full["w_out"][l])
        x2, up = mlp_fwd(x1, norm_mlp[l].reshape(1, d), full["w_mlp_up"][l], full["w_mlp_down"][l])
        saved.append(dict(x0=xs, qkv=qkv, rest=rest, cumcol=cumcol, ya=ya, lser=lser,
                          tabs=tabs, wb4=wb4, wc4=wc4, dskip=dskip, ys=ys, hs=hs, x1=x1, z=z, pa=pa, pb=pb,
                          yb=yb, yb2=yb2, mixed=mixed, up=up))
        xs = x2
    dx, loss_rows, dgf_rows = final_loss(xs, norm_final.reshape(1, d), loss_target.reshape(tokens, d))
    loss = lax.psum(jnp.sum(loss_rows), ("x", "y", "c"))

    big = {n: [None] * nl for n in SHARDED}
    small = {n: [None] * nl for n in SMALL if n != "norm_final"}
    for l in reversed(range(nl)):
        s = saved[l]
        g2 = norm_mlp[l].reshape(1, d)
        dup, dx1, dg2 = mlp_bwd(dx, s["up"], s["x1"], g2, full["w_mlp_up"][l], full["w_mlp_down"][l])
        big["w_mlp_down"][l] = matmul_tn(s["up"], dx, "grad_w_mlp_down", a_kind="relu2")
        big["w_mlp_up"][l] = matmul_tn(s["x1"], dup, "grad_w_mlp_up", a_kind="norm", gain=g2)
        small["norm_mlp"][l] = jnp.sum(dg2, axis=0)
        dya, dys, dgab, dpa, dpb, dz, dbg = mix_bwd(dx1, s["rest"], s["pa"], s["pb"], s["z"], s["ys"],
                                                    full["w_glu"][l], full["w_branch_a"][l], full["w_branch_b"][l],
                                                    full["w_out"][l])
        big["w_out"][l] = matmul_tn(s["mixed"], dx1, "grad_w_out")
        big["w_branch_a"][l] = matmul_tn(s["ya"], dpa, "grad_w_branch_a")
        big["w_branch_b"][l] = matmul_tn(s["yb2"], dpb, "grad_w_branch_b")
        big["w_glu"][l] = matmul_tn(s["yb"], dz, "grad_w_glu")
        small["b_glu"][l] = jnp.sum(dbg, axis=0)
        du, ga8, gwb, gwc, gd8 = ssm_bwd(dys, s["rest"], s["hs"], s["wb4"], s["wc4"], s["tabs"], s["dskip"],
                                         bsz, seq, u_blk)
        gar, gai, gbbr, gbbi, gcr, gci = _ssm_param_grads(ga8, gwb, gwc)
        glr, gli, gdt, gbr, gbi = disc[l][1]((gar, gai, gbbr, gbbi))
        small["ssm_lambda_re"][l], small["ssm_lambda_im"][l], small["ssm_log_dt"][l] = glr, gli, gdt
        small["ssm_b_re"][l], small["ssm_b_im"][l] = gbr, gbi
        small["ssm_c_re"][l], small["ssm_c_im"][l] = gcr, gci
        small["ssm_d"][l] = jnp.sum(gd8, axis=0)
        dq, dk, dv, df, dbf = fox_bwd(s["qkv"], dya, s["ya"], s["lser"], s["cumcol"],
                                      s["rest"], bf_pad[l:l + 1], bsz, seq, f_blk)
        small["b_forget"][l] = jnp.sum(dbf, axis=0)[:ATTN_HEADS]
        dproj = jnp.concatenate([dq, dk, dv, dgab, du, df], axis=1)
        g1 = norm_mix[l].reshape(1, d)
        dwp = matmul_tn(s["x0"], dproj, "grad_w_in", a_kind="norm", gain=g1)
        big["w_in"][l] = jnp.concatenate(
            [dwp[:, :o_f], dwp[:, o_f + 2 * d + sw:o_f + 2 * d + sw + ATTN_HEADS],
             dwp[:, o_f + 2 * d:o_f + 2 * d + sw], dwp[:, o_f:o_f + 2 * d]], axis=1)
        dx, dg1 = proj_bwd(dproj, w_pad[l], s["x0"], dx1, g1)
        small["norm_mix"][l] = jnp.sum(dg1, axis=0)
    grad_x = dx.reshape(bsz, seq, d)

    gps = [jnp.stack(jnp.split(jnp.stack(big[n]), 4, axis=SHARD_AXIS[n])) for n in SHARDED]
    from_sibling = exchange_sibling_half(gps)
    pair = [add_sibling(g, s1, core, "add_sibling_" + n) for n, g, s1 in zip(SHARDED, gps, from_sibling)]
    slots = scatter_to_owners(pair)
    reduced = [add_chips(s2, p, chip, "add_chips_" + n) for n, s2, p in zip(SHARDED, slots, pair)]
    from_sibling = share_with_sibling(reduced)
    out_g, out_d, out_m, out_v = {}, {}, {}, {}
    for n, g_own, g_sib in zip(SHARDED, reduced, from_sibling):
        out_g[n], out_d[n], out_m[n], out_v[n] = adamw(args[n], g_own, g_sib, args["m_" + n], args["v_" + n],
                                                       core, "adamw_" + n)

    small_g = [jnp.stack(small[n]) if n != "norm_final" else jnp.sum(dgf_rows, axis=0) for n in SMALL]
    small_shapes = [args[n].shape for n in SMALL]
    small_packed = pack_flat(small_g)
    sg, sd, sm, sv = sum_and_adamw(exchange_all(small_packed), small_packed, device,
                                   pack_flat([args[n] for n in SMALL]),
                                   pack_flat([args["m_" + n] for n in SMALL]),
                                   pack_flat([args["v_" + n] for n in SMALL]))
    for res, packed in ((out_g, sg), (out_d, sd), (out_m, sm), (out_v, sv)):
        res.update(zip(SMALL, unpack_flat(packed, small_shapes)))

    order = ("norm_mix", "w_in", "b_forget", "ssm_lambda_re", "ssm_lambda_im", "ssm_log_dt", "ssm_b_re",
             "ssm_b_im", "ssm_c_re", "ssm_c_im", "ssm_d", "w_glu", "b_glu", "w_branch_a", "w_branch_b", "w_out",
             "norm_mlp", "w_mlp_up", "w_mlp_down", "norm_final")
    return (loss, grad_x, *[out_g[n] for n in order], *[out_d[n] for n in order],
            *[out_m[n] for n in order], *[out_v[n] for n in order])
```

```python
import math

import jax
import jax.numpy as jnp
from jax import lax
from jax.experimental import pallas as pl
from jax.experimental.pallas import tpu as pltpu

F32 = jnp.float32
BF16 = jnp.bfloat16
MESH = pl.DeviceIdType.MESH
ANY = pl.BlockSpec(memory_space=pl.ANY)

ATTN_HEADS = 8
HEAD_DIM = 64
ATTN_WIDTH = ATTN_HEADS * HEAD_DIM
HEAD_PAIRS = ATTN_HEADS // 2
SSM_GROUPS = 32
SSM_GROUP_CH = 16
SSM_STATE = 64
SSM_WIDTH = SSM_GROUPS * SSM_GROUP_CH
LANES = 128
SUBLANES = 8
SSM_CHUNKS = SSM_WIDTH // LANES
CHUNK_GROUPS = SSM_GROUPS // SSM_CHUNKS
CHUNK_STATES = CHUNK_GROUPS * SSM_STATE
CHUNK_LANES = 2 * CHUNK_STATES
STATE_LANES = SSM_CHUNKS * CHUNK_LANES
F_PAD = LANES
RMS_EPS = 1e-6
ADAM_LR = 0.001
ADAM_B1 = 0.9
ADAM_B2 = 0.999
ADAM_EPS = 1e-08
ADAM_WD = 0.01
ADAM_STEP = 10
PACK_COLS = 1024
PACK_ROW_TILE = 256
VMEM_LIMIT = 52 * 1024 * 1024
NEG_BIG = -1e30
GELU_C = math.sqrt(2.0 / math.pi)
GELU_A = 0.044715

NN = (((1,), (0,)), ((), ()))
NT = (((1,), (1,)), ((), ()))
TN = (((0,), (0,)), ((), ()))


def _pick(n, pref):
    if n <= pref:
        return n
    best = LANES
    for t in range(LANES, pref + 1, LANES):
        if n % t == 0:
            best = t
    assert n % best == 0, (n, pref)
    return best


def _rows(n, pref):
    t = min(n, pref)
    while n % t:
        t //= 2
    assert t % 16 == 0 or t == n, (n, pref)
    return t


def _params(sem):
    return pltpu.CompilerParams(dimension_semantics=sem, vmem_limit_bytes=VMEM_LIMIT)


def _fold8(v):
    r, c = v.shape
    return jnp.sum(v.reshape(r // SUBLANES, SUBLANES, c), axis=0)


def _dot(a, b, dims=None):
    if dims is None:
        return jnp.dot(a, b, preferred_element_type=F32)
    return lax.dot_general(a, b, dims, preferred_element_type=F32)


def _dot_exact(a, b, dims):
    return lax.dot_general(a, b, dims, preferred_element_type=F32, precision=lax.Precision.HIGHEST)


def _sigmoid(v):
    return 1.0 / (1.0 + jnp.exp(-v))


def _rms_scale(x):
    return lax.rsqrt(jnp.mean(x * x, axis=-1, keepdims=True) + RMS_EPS)


def _rms_bwd(x, g, dh):
    r = _rms_scale(x)
    xn = x * r
    dxn = dh * g
    dx = r * (dxn - xn * jnp.mean(dxn * xn, axis=-1, keepdims=True))
    return dx, dh * xn


def norm_matmul(x, g, w, out_dtype, name):
    t, d = x.shape
    m = w.shape[1]
    tm, tn = _rows(t, 512), _pick(m, 1024)

    def body(x_ref, g_ref, w_ref, o_ref):
        xv = x_ref[...]
        h = (xv * _rms_scale(xv) * g_ref[...]).astype(BF16)
        o_ref[...] = _dot(h, w_ref[...]).astype(o_ref.dtype)

    return pl.pallas_call(
        body, name=name, grid=(t // tm, m // tn),
        in_specs=[pl.BlockSpec((tm, d), lambda i, j: (i, 0)),
                  pl.BlockSpec((1, d), lambda i, j: (0, 0)),
                  pl.BlockSpec((d, tn), lambda i, j: (0, j))],
        out_specs=pl.BlockSpec((tm, tn), lambda i, j: (i, j)),
        out_shape=jax.ShapeDtypeStruct((t, m), out_dtype),
        compiler_params=_params(("parallel", "arbitrary")),
    )(x, g, w)


def matmul_tn(a, b, name, a_kind="plain", gain=None, shard_axis=None, tm_pref=1024, tn_pref=1536, tk_pref=512):
    t, ma = a.shape
    nb = b.shape[1]
    tm = ma if a_kind == "norm" else _pick(ma, tm_pref)
    tn = _pick(nb, tn_pref)
    tk = _rows(t, tk_pref)
    nk = t // tk
    if shard_axis == 0:
        per = tm * 4 // ma
        assert per >= 1 and (ma // 4) * per == tm, (ma, tm)
        out_shape, out_spec = (4, ma // 4, nb), pl.BlockSpec((per, ma // 4, tn), lambda i, j, k: (i, 0, j))
    elif shard_axis == 1:
        per = tn * 4 // nb
        assert per >= 1 and (nb // 4) * per == tn, (nb, tn)
        out_shape, out_spec = (4, ma, nb // 4), pl.BlockSpec((per, tm, nb // 4), lambda i, j, k: (j, i, 0))
    else:
        out_shape, out_spec = (ma, nb), pl.BlockSpec((tm, tn), lambda i, j, k: (i, j))

    def body(*refs):
        if a_kind == "norm":
            a_ref, g_ref, b_ref, o_ref, acc = refs
        else:
            a_ref, b_ref, o_ref, acc = refs
        k = pl.program_id(2)

        @pl.when(k == 0)
        def _():
            acc[...] = jnp.zeros_like(acc)

        av = a_ref[...]
        if a_kind == "norm":
            av = av * _rms_scale(av) * g_ref[...]
        elif a_kind == "relu2":
            av = jnp.square(jnp.maximum(av.astype(F32), 0.0))
        acc[...] += _dot(av.astype(BF16), b_ref[...].astype(BF16), TN)

        @pl.when(k == nk - 1)
        def _():
            if shard_axis == 0:
                o_ref[...] = acc[...].reshape(o_ref.shape)
            elif shard_axis == 1:
                cs = nb // 4
                for n in range(o_ref.shape[0]):
                    o_ref[n] = acc[:, n * cs:(n + 1) * cs]
            else:
                o_ref[...] = acc[...]

    in_specs = [pl.BlockSpec((tk, tm), lambda i, j, k: (k, i))]
    args = [a]
    if a_kind == "norm":
        in_specs.append(pl.BlockSpec((1, ma), lambda i, j, k: (0, 0)))
        args.append(gain)
    in_specs.append(pl.BlockSpec((tk, tn), lambda i, j, k: (k, j)))
    args.append(b)
    return pl.pallas_call(
        body, name=name, grid=(ma // tm, nb // tn, nk),
        in_specs=in_specs,
        out_specs=out_spec,
        out_shape=jax.ShapeDtypeStruct(out_shape, F32),
        scratch_shapes=[pltpu.VMEM((tm, tn), F32)],
        compiler_params=_params(("parallel", "parallel", "arbitrary")),
    )(*args)


def _tri(n, upper):
    r = lax.broadcasted_iota(jnp.int32, (n, n), 0)
    c = lax.broadcasted_iota(jnp.int32, (n, n), 1)
    return jnp.where((c >= r) if upper else (c <= r), 1.0, 0.0).astype(F32)


def _head_rows():
    r = lax.broadcasted_iota(jnp.int32, (SUBLANES, LANES), 0)
    c = lax.broadcasted_iota(jnp.int32, (SUBLANES, LANES), 1)
    return jnp.where(r == c, 1.0, 0.0).astype(F32)


def forget_cumsum(rest, bf, bsz, seq, f_blk):
    tc = _rows(seq, 512)
    nc = seq // tc

    def body(f_ref, b_ref, col_ref, row_ref, carry):
        c = pl.program_id(1)

        @pl.when(c == 0)
        def _():
            carry[...] = jnp.zeros_like(carry)

        z = f_ref[...] + b_ref[...]
        logf = jnp.minimum(z, 0.0) - jnp.log(1.0 + jnp.exp(-jnp.abs(z)))
        cum = _dot_exact(_tri(tc, False), logf, NN) + carry[0:1, :]
        col_ref[0] = cum
        row_ref[0] = _dot_exact(_head_rows(), cum, NT)
        carry[...] = jnp.broadcast_to(cum[tc - 1:tc, :], carry.shape)

    return pl.pallas_call(
        body, name="forget_cumsum", grid=(bsz, nc),
        in_specs=[pl.BlockSpec((tc, F_PAD), lambda b, c: (b * nc + c, f_blk)),
                  pl.BlockSpec((1, F_PAD), lambda b, c: (0, 0))],
        out_specs=[pl.BlockSpec((1, tc, LANES), lambda b, c: (b, c, 0)),
                   pl.BlockSpec((1, SUBLANES, tc), lambda b, c: (b, 0, c))],
        out_shape=[jax.ShapeDtypeStruct((bsz, seq, LANES), F32),
                   jax.ShapeDtypeStruct((bsz, SUBLANES, seq), F32)],
        scratch_shapes=[pltpu.VMEM((SUBLANES, LANES), F32)],
        compiler_params=_params(("parallel", "arbitrary")),
    )(rest, bf)


def forget_bwd(dcq, dcp, rest, bf, bsz, seq, f_blk):
    tc = _attn_tile(seq)
    nc = seq // tc

    def body(dq_ref, dc_ref, f_ref, b_ref, df_ref, db_ref, carry):
        b = pl.program_id(0)
        c = pl.program_id(1)

        @pl.when(c == 0)
        def _():
            carry[...] = jnp.zeros_like(carry)

        @pl.when((b == 0) & (c == 0))
        def _():
            db_ref[...] = jnp.zeros_like(db_ref)

        row = lax.broadcasted_iota(jnp.int32, (SUBLANES, tc), 0)
        heads = jnp.zeros((SUBLANES, tc), F32)
        dc = jnp.zeros((tc, LANES), F32)
        for p in range(HEAD_PAIRS):
            blk = dq_ref[0, p, 0]
            heads = heads + jnp.where(row == 2 * p, blk[0:1], 0.0) + jnp.where(row == 2 * p + 1, blk[1:2], 0.0)
            dc = dc + dc_ref[0, p]
        dc = dc + jnp.concatenate([heads, jnp.zeros((LANES - SUBLANES, tc), F32)], axis=0).T
        dlogf = _dot_exact(_tri(tc, True), dc, NN) + carry[0:1, :]
        carry[...] = jnp.broadcast_to(dlogf[0:1, :], carry.shape)
        z = f_ref[...] + b_ref[...]
        lane = lax.broadcasted_iota(jnp.int32, z.shape, 1)
        df = jnp.where(lane < ATTN_HEADS, dlogf * _sigmoid(-z), 0.0)
        df_ref[...] = df.astype(df_ref.dtype)
        db_ref[...] += _fold8(df)

    return pl.pallas_call(
        body, name="forget_bwd", grid=(bsz, nc),
        in_specs=[pl.BlockSpec((1, HEAD_PAIRS, 1, SUBLANES, tc), lambda b, c: (b, 0, nc - 1 - c, 0, 0)),
                  pl.BlockSpec((1, HEAD_PAIRS, tc, LANES), lambda b, c: (b, 0, nc - 1 - c, 0)),
                  pl.BlockSpec((tc, F_PAD), lambda b, c: (b * nc + nc - 1 - c, f_blk)),
                  pl.BlockSpec((1, F_PAD), lambda b, c: (0, 0))],
        out_specs=[pl.BlockSpec((tc, F_PAD), lambda b, c: (b * nc + nc - 1 - c, 0)),
                   pl.BlockSpec((SUBLANES, F_PAD), lambda b, c: (0, 0))],
        out_shape=[jax.ShapeDtypeStruct((bsz * seq, F_PAD), BF16),
                   jax.ShapeDtypeStruct((SUBLANES, F_PAD), F32)],
        scratch_shapes=[pltpu.VMEM((SUBLANES, LANES), F32)],
        compiler_params=_params(("arbitrary", "arbitrary")),
    )(dcq, dcp, rest, bf)


def _attn_tile(seq):
    return 512 if seq >= 2048 else 128


def _lane_head(shape, par):
    lane = lax.broadcasted_iota(jnp.int32, shape, len(shape) - 1)
    return (lane >= HEAD_DIM) if par else (lane < HEAD_DIM)


def _pick_lane(block, idx):
    lane = lax.broadcasted_iota(jnp.int32, block.shape, 1)
    return jnp.sum(jnp.where(lane == idx, block, 0.0), axis=1, keepdims=True)


def _pair_rows(lo_lane, hi_lane):
    r = lax.broadcasted_iota(jnp.int32, (SUBLANES, LANES), 0)
    c = lax.broadcasted_iota(jnp.int32, (SUBLANES, LANES), 1)
    if lo_lane is None:
        sel = ((r == 0) & (c < HEAD_DIM)) | ((r == 1) & (c >= HEAD_DIM))
    else:
        sel = ((r == 0) & (c == lo_lane)) | ((r == 1) & (c == hi_lane))
    return jnp.where(sel, 1.0, 0.0).astype(F32)


def _causal(s, transposed):
    r = lax.broadcasted_iota(jnp.int32, s.shape, 0)
    c = lax.broadcasted_iota(jnp.int32, s.shape, 1)
    return jnp.where((c >= r) if transposed else (r >= c), s, NEG_BIG)


def _causal_pairs(n, key_major):
    if key_major:
        pairs = [(i, j) for j in range(n) for i in range(j, n)]
    else:
        pairs = [(i, j) for i in range(n) for j in range(i + 1)]
    return (jnp.asarray([p[0] for p in pairs], jnp.int32), jnp.asarray([p[1] for p in pairs], jnp.int32))


def fox_fwd(qkv, cumrow, bsz, seq):
    tq = _attn_tile(seq)
    nq = seq // tq
    scale = HEAD_DIM ** -0.5
    kb, vb = ATTN_WIDTH // LANES, 2 * ATTN_WIDTH // LANES
    qi_tab, kj_tab = _causal_pairs(nq, False)

    def body(qi_ref, kj_ref, q_ref, k_ref, v_ref, cr_ref, o_ref, lr_ref, m_s, acc_s):
        hp = pl.program_id(1)
        qi = qi_ref[pl.program_id(2)]
        kj = kj_ref[pl.program_id(2)]

        @pl.when(kj == 0)
        def _():
            m_s[...] = jnp.full_like(m_s, NEG_BIG)
            acc_s[...] = jnp.zeros_like(acc_s)

        def step(diag):
            q = q_ref[...]
            k = k_ref[...]
            v = v_ref[...]
            for par in range(2):
                sel = _lane_head(q.shape, par)
                qh = jnp.where(sel, q, 0.0) * scale
                s = _dot(qh.astype(BF16), k, NT) - cr_ref[0, pl.ds(2 * hp + par, 1), :]
                if diag:
                    s = _causal(s, False)
                m_prev = m_s[par]
                m_new = jnp.maximum(m_prev, jnp.max(s, axis=1, keepdims=True))
                p = jnp.exp(s - m_new).astype(BF16)
                acc_s[par] = jnp.exp(m_prev - m_new) * acc_s[par] + _dot(p, jnp.where(sel, v, 1.0).astype(BF16))
                m_s[par] = m_new

        @pl.when(kj < qi)
        def _():
            step(False)

        @pl.when(kj == qi)
        def _():
            step(True)
            lo = _lane_head((tq, LANES), 0)
            sums = [pltpu.roll(acc_s[par], HEAD_DIM, 1) for par in range(2)]
            out = jnp.where(lo, acc_s[0] / sums[0], acc_s[1] / sums[1])
            o_ref[...] = out.astype(o_ref.dtype)
            lse = jnp.where(lo, m_s[0] + jnp.log(sums[0]), m_s[1] + jnp.log(sums[1]))
            lr_ref[0, 0] = _dot_exact(_pair_rows(0, HEAD_DIM), lse, NT)

    return pl.pallas_call(
        body, name="fox_fwd",
        grid_spec=pltpu.PrefetchScalarGridSpec(
            num_scalar_prefetch=2, grid=(bsz, HEAD_PAIRS, int(qi_tab.shape[0])),
            in_specs=[pl.BlockSpec((tq, LANES), lambda b, h, t, qi, kj: (b * nq + qi[t], h)),
                      pl.BlockSpec((tq, LANES), lambda b, h, t, qi, kj: (b * nq + kj[t], kb + h)),
                      pl.BlockSpec((tq, LANES), lambda b, h, t, qi, kj: (b * nq + kj[t], vb + h)),
                      pl.BlockSpec((1, SUBLANES, tq), lambda b, h, t, qi, kj: (b, 0, kj[t]))],
            out_specs=[pl.BlockSpec((tq, LANES), lambda b, h, t, qi, kj: (b * nq + qi[t], h)),
                       pl.BlockSpec((1, 1, SUBLANES, tq), lambda b, h, t, qi, kj: (b, h, 0, qi[t]))],
            scratch_shapes=[pltpu.VMEM((2, tq, 1), F32), pltpu.VMEM((2, tq, LANES), F32)]),
        out_shape=[jax.ShapeDtypeStruct((bsz * seq, ATTN_WIDTH), BF16),
                   jax.ShapeDtypeStruct((bsz, HEAD_PAIRS, SUBLANES, seq), F32)],
        compiler_params=_params(("parallel", "parallel", "arbitrary")),
    )(qi_tab, kj_tab, qkv, qkv, qkv, cumrow)


def fox_bwd_kernel(qkv, do, o, lser, cumcol, bsz, seq):
    tk = _attn_tile(seq)
    nk = seq // tk
    scale = HEAD_DIM ** -0.5
    kb, vb = ATTN_WIDTH // LANES, 2 * ATTN_WIDTH // LANES
    qi_tab, kj_tab = _causal_pairs(nk, True)
    npairs = int(qi_tab.shape[0])

    def body(qi_ref, kj_ref, q_ref, k_ref, v_ref, do_ref, o_ref, lr_ref, cc_ref,
             dq_ref, dk_ref, dv_ref, dc_ref, dr_ref, dq_s, dk_s, dv_s, dc_s, dr_s):
        hp = pl.program_id(1)
        t = pl.program_id(2)
        qi = qi_ref[t]
        kj = kj_ref[t]

        @pl.when(t == 0)
        def _():
            dq_s[...] = jnp.zeros_like(dq_s)
            dr_s[...] = jnp.zeros_like(dr_s)

        @pl.when(qi == kj)
        def _():
            dk_s[...] = jnp.zeros_like(dk_s)
            dv_s[...] = jnp.zeros_like(dv_s)
            dc_s[...] = jnp.zeros_like(dc_s)

        def step(diag):
            q = q_ref[...]
            k = k_ref[...]
            v = v_ref[...]
            dov = do_ref[...]
            prod = dov.astype(F32) * o_ref[...].astype(F32)
            drow = _dot_exact(_pair_rows(None, None), prod, NT)
            lrow = lr_ref[0, 0]
            lane = lax.broadcasted_iota(jnp.int32, (tk, LANES), 1)
            for par in range(2):
                head = 2 * hp + par
                sel = _lane_head(k.shape, par)
                kh = (jnp.where(sel, k, 0.0) * scale).astype(BF16)
                st = _dot(kh, q, NT) - _pick_lane(cc_ref[0], head)
                if diag:
                    st = _causal(st, True)
                pt = jnp.exp(st - lrow[par:par + 1, :])
                vh = jnp.where(sel, v, 0.0)
                dpt = _dot(vh.astype(BF16), dov, NT)
                dst = pt * (dpt - drow[par:par + 1, :])
                dsb = dst.astype(BF16)
                dv_s[...] += jnp.where(sel, _dot(pt.astype(BF16), dov), 0.0)
                dk_s[...] += jnp.where(sel, _dot(dsb, q), 0.0)
                dq_s[qi] += _dot(dsb, kh, TN)
                dc_s[...] += jnp.where(lane == head, -jnp.sum(dst, axis=1, keepdims=True), 0.0)
                dr_s[qi, par:par + 1, :] += jnp.sum(dst, axis=0, keepdims=True)

        @pl.when(qi > kj)
        def _():
            step(False)

        @pl.when(qi == kj)
        def _():
            step(True)

        @pl.when(qi == nk - 1)
        def _():
            dk_ref[...] = (dk_s[...] * scale).astype(dk_ref.dtype)
            dv_ref[...] = dv_s[...].astype(dv_ref.dtype)
            dc_ref[0, 0] = dc_s[...]

        @pl.when(t == npairs - 1)
        def _():
            for i in range(nk):
                dq_ref[i * tk:(i + 1) * tk, :] = dq_s[i].astype(dq_ref.dtype)
            dr_ref[0, 0] = dr_s[...]

    return pl.pallas_call(
        body, name="fox_bwd",
        grid_spec=pltpu.PrefetchScalarGridSpec(
            num_scalar_prefetch=2, grid=(bsz, HEAD_PAIRS, npairs),
            in_specs=[pl.BlockSpec((tk, LANES), lambda b, h, t, qi, kj: (b * nk + qi[t], h)),
                      pl.BlockSpec((tk, LANES), lambda b, h, t, qi, kj: (b * nk + kj[t], kb + h)),
                      pl.BlockSpec((tk, LANES), lambda b, h, t, qi, kj: (b * nk + kj[t], vb + h)),
                      pl.BlockSpec((tk, LANES), lambda b, h, t, qi, kj: (b * nk + qi[t], h)),
                      pl.BlockSpec((tk, LANES), lambda b, h, t, qi, kj: (b * nk + qi[t], h)),
                      pl.BlockSpec((1, 1, SUBLANES, tk), lambda b, h, t, qi, kj: (b, h, 0, qi[t])),
                      pl.BlockSpec((1, tk, LANES), lambda b, h, t, qi, kj: (b, kj[t], 0))],
            out_specs=[pl.BlockSpec((seq, LANES), lambda b, h, t, qi, kj: (b, h)),
                       pl.BlockSpec((tk, LANES), lambda b, h, t, qi, kj: (b * nk + kj[t], h)),
                       pl.BlockSpec((tk, LANES), lambda b, h, t, qi, kj: (b * nk + kj[t], h)),
                       pl.BlockSpec((1, 1, tk, LANES), lambda b, h, t, qi, kj: (b, h, kj[t], 0)),
                       pl.BlockSpec((1, 1, nk, SUBLANES, tk), lambda b, h, t, qi, kj: (b, h, 0, 0, 0))],
            scratch_shapes=[pltpu.VMEM((nk, tk, LANES), F32), pltpu.VMEM((tk, LANES), F32),
                            pltpu.VMEM((tk, LANES), F32), pltpu.VMEM((tk, LANES), F32),
                            pltpu.VMEM((nk, SUBLANES, tk), F32)]),
        out_shape=[jax.ShapeDtypeStruct((bsz * seq, ATTN_WIDTH), BF16),
                   jax.ShapeDtypeStruct((bsz * seq, ATTN_WIDTH), BF16),
                   jax.ShapeDtypeStruct((bsz * seq, ATTN_WIDTH), BF16),
                   jax.ShapeDtypeStruct((bsz, HEAD_PAIRS, seq, LANES), F32),
                   jax.ShapeDtypeStruct((bsz, HEAD_PAIRS, nk, SUBLANES, tk), F32)],
        compiler_params=_params(("parallel", "parallel", "arbitrary")),
    )(qi_tab, kj_tab, qkv, qkv, qkv, do, o, lser, cumcol)


def fox_bwd(qkv, do, o, lser, cumcol, rest, bf, bsz, seq, f_blk):
    dq, dk, dv, dcp, dcq = fox_bwd_kernel(qkv, do, o, lser, cumcol, bsz, seq)
    df, dbf = forget_bwd(dcq, dcp, rest, bf, bsz, seq, f_blk)
    return dq, dk, dv, df, dbf


SCAN_STEPS = (1, 2, 4)
TAB_FWD = 0
TAB_BWD = 32
TAB_CARRY = 24
TAB_ROWS = 64


def _ssm_tile(seq):
    return 256 if seq >= 1024 else 64


def _scan_block(xr, xi, tab_ref, re, im, cr, ci, reverse):
    base = TAB_BWD if reverse else TAB_FWD
    for n, s in enumerate(SCAN_STEPS):
        ar = tab_ref[base + n * SUBLANES:base + (n + 1) * SUBLANES, re]
        ai = tab_ref[base + n * SUBLANES:base + (n + 1) * SUBLANES, im]
        shift = SUBLANES - s if reverse else s
        sr = pltpu.roll(xr, shift, 0)
        si = pltpu.roll(xi, shift, 0)
        xr, xi = xr + ar * sr - ai * si, xi + ar * si + ai * sr
    pr = tab_ref[base + TAB_CARRY:base + TAB_CARRY + SUBLANES, re]
    pi = tab_ref[base + TAB_CARRY:base + TAB_CARRY + SUBLANES, im]
    xr, xi = xr + pr * cr - pi * ci, xi + pr * ci + pi * cr
    return xr, xi


def ssm_fwd(rest, wb4, wc4, tabs, dskip, bsz, seq, u_blk):
    tt = _ssm_tile(seq)
    nt = seq // tt

    def body(u_ref, wb_ref, wc_ref, tab_ref, d_ref, y_ref, h_ref, carry):
        c = pl.program_id(1)

        @pl.when(c == 0)
        def _():
            carry[...] = jnp.zeros_like(carry)

        u = u_ref[...]
        ub = u.astype(BF16)
        for j in range(SSM_CHUNKS):
            h_ref[:, j * CHUNK_LANES:(j + 1) * CHUNK_LANES] = _dot(ub[:, j * LANES:(j + 1) * LANES], wb_ref[j])
        for j in range(SSM_CHUNKS):
            re = slice(j * CHUNK_LANES, j * CHUNK_LANES + CHUNK_STATES)
            im = slice(j * CHUNK_LANES + CHUNK_STATES, (j + 1) * CHUNK_LANES)

            def blk(bi, car):
                r0 = pl.multiple_of(bi * SUBLANES, SUBLANES)
                xr, xi = _scan_block(h_ref[pl.ds(r0, SUBLANES), re], h_ref[pl.ds(r0, SUBLANES), im],
                                     tab_ref, re, im, car[0], car[1], False)
                h_ref[pl.ds(r0, SUBLANES), re] = xr
                h_ref[pl.ds(r0, SUBLANES), im] = xi
                return xr[SUBLANES - 1:SUBLANES], xi[SUBLANES - 1:SUBLANES]

            cr, ci = lax.fori_loop(0, tt // SUBLANES, blk, (carry[0:1, re], carry[0:1, im]), unroll=2)
            carry[0:1, re] = cr
            carry[0:1, im] = ci
        for j in range(SSM_CHUNKS):
            hj = h_ref[:, j * CHUNK_LANES:(j + 1) * CHUNK_LANES].astype(BF16)
            cols = slice(j * LANES, (j + 1) * LANES)
            y_ref[:, cols] = _dot(hj, wc_ref[j]) + d_ref[:, cols] * u[:, cols]

    return pl.pallas_call(
        body, name="ssm_fwd", grid=(bsz, nt),
        in_specs=[pl.BlockSpec((tt, SSM_WIDTH), lambda b, c: (b * nt + c, u_blk)),
                  pl.BlockSpec((SSM_CHUNKS, LANES, CHUNK_LANES), lambda b, c: (0, 0, 0)),
                  pl.BlockSpec((SSM_CHUNKS, CHUNK_LANES, LANES), lambda b, c: (0, 0, 0)),
                  pl.BlockSpec((TAB_ROWS, STATE_LANES), lambda b, c: (0, 0)),
                  pl.BlockSpec((1, SSM_WIDTH), lambda b, c: (0, 0))],
        out_specs=[pl.BlockSpec((tt, SSM_WIDTH), lambda b, c: (b * nt + c, 0)),
                   pl.BlockSpec((tt, STATE_LANES), lambda b, c: (b * nt + c, 0))],
        out_shape=[jax.ShapeDtypeStruct((bsz * seq, SSM_WIDTH), F32),
                   jax.ShapeDtypeStruct((bsz * seq, STATE_LANES), F32)],
        scratch_shapes=[pltpu.VMEM((SUBLANES, STATE_LANES), F32)],
        compiler_params=_params(("parallel", "arbitrary")),
    )(rest, wb4, wc4, tabs, dskip)


def ssm_bwd(dys, rest, hs, wb4, wc4, tabs, dskip, bsz, seq, u_blk):
    tt = _ssm_tile(seq)
    nt = seq // tt
    nb = tt // SUBLANES

    def body(dy_ref, u_ref, h_ref, hp_ref, wb_ref, wc_ref, tab_ref, d_ref,
             du_ref, ga_ref, gwb_ref, gwc_ref, gd_ref, g_s, carry):
        b = pl.program_id(0)
        c = pl.program_id(1)

        @pl.when(c == 0)
        def _():
            carry[...] = jnp.zeros_like(carry)

        @pl.when((b == 0) & (c == 0))
        def _():
            ga_ref[...] = jnp.zeros_like(ga_ref)
            gwb_ref[...] = jnp.zeros_like(gwb_ref)
            gwc_ref[...] = jnp.zeros_like(gwc_ref)
            gd_ref[...] = jnp.zeros_like(gd_ref)

        dy = dy_ref[...].astype(F32)
        dyb = dy.astype(BF16)
        u = u_ref[...]
        ub = u.astype(BF16)
        first_chunk = c == nt - 1
        for j in range(SSM_CHUNKS):
            g_s[:, j * CHUNK_LANES:(j + 1) * CHUNK_LANES] = _dot(dyb[:, j * LANES:(j + 1) * LANES], wc_ref[j], NT)
        for j in range(SSM_CHUNKS):
            re = slice(j * CHUNK_LANES, j * CHUNK_LANES + CHUNK_STATES)
            im = slice(j * CHUNK_LANES + CHUNK_STATES, (j + 1) * CHUNK_LANES)
            row = lax.broadcasted_iota(jnp.int32, (SUBLANES, CHUNK_STATES), 0)

            def blk(n, car):
                bi = nb - 1 - n
                r0 = pl.multiple_of(bi * SUBLANES, SUBLANES)
                gr, gi = _scan_block(g_s[pl.ds(r0, SUBLANES), re], g_s[pl.ds(r0, SUBLANES), im],
                                     tab_ref, re, im, car[0], car[1], True)
                g_s[pl.ds(r0, SUBLANES), re] = gr
                g_s[pl.ds(r0, SUBLANES), im] = gi
                rp = pl.multiple_of(jnp.maximum(bi - 1, 0) * SUBLANES, SUBLANES)
                inside = bi > 0
                live = jnp.where(jnp.logical_or(inside, jnp.logical_not(first_chunk)), 1.0, 0.0)
                pr = jnp.where(inside, h_ref[pl.ds(rp, SUBLANES), re], hp_ref[:, re])[SUBLANES - 1:SUBLANES] * live
                pi = jnp.where(inside, h_ref[pl.ds(rp, SUBLANES), im], hp_ref[:, im])[SUBLANES - 1:SUBLANES] * live
                hr = jnp.where(row >= 1, pltpu.roll(h_ref[pl.ds(r0, SUBLANES), re], 1, 0), pr)
                hi = jnp.where(row >= 1, pltpu.roll(h_ref[pl.ds(r0, SUBLANES), im], 1, 0), pi)
                return (gr[0:1], gi[0:1], car[2] + gr * hr + gi * hi, car[3] + gi * hr - gr * hi)

            zero = jnp.zeros((SUBLANES, CHUNK_STATES), F32)
            cr, ci, sr, si = lax.fori_loop(0, nb, blk, (carry[0:1, re], carry[0:1, im], zero, zero), unroll=2)
            carry[0:1, re] = cr
            carry[0:1, im] = ci
            ga_ref[:, re] += sr
            ga_ref[:, im] += si
        for j in range(SSM_CHUNKS):
            cols = slice(j * LANES, (j + 1) * LANES)
            lanes = slice(j * CHUNK_LANES, (j + 1) * CHUNK_LANES)
            gj = g_s[:, lanes].astype(BF16)
            du_ref[:, cols] = (_dot(gj, wb_ref[j], NT) + d_ref[:, cols] * dy[:, cols]).astype(du_ref.dtype)
            gwb_ref[j] += _dot(ub[:, cols], gj, TN)
            gwc_ref[j] += _dot(h_ref[:, lanes].astype(BF16), dyb[:, cols], TN)
        gd_ref[...] += _fold8(dy * u)

    def prev_rows(b, c):
        chunk = nt - 1 - c
        return (jnp.maximum((b * nt + chunk) * nb - 1, 0), 0)

    return pl.pallas_call(
        body, name="ssm_bwd", grid=(bsz, nt),
        in_specs=[pl.BlockSpec((tt, SSM_WIDTH), lambda b, c: (b * nt + nt - 1 - c, 0)),
                  pl.BlockSpec((tt, SSM_WIDTH), lambda b, c: (b * nt + nt - 1 - c, u_blk)),
                  pl.BlockSpec((tt, STATE_LANES), lambda b, c: (b * nt + nt - 1 - c, 0)),
                  pl.BlockSpec((SUBLANES, STATE_LANES), prev_rows),
                  pl.BlockSpec((SSM_CHUNKS, LANES, CHUNK_LANES), lambda b, c: (0, 0, 0)),
                  pl.BlockSpec((SSM_CHUNKS, CHUNK_LANES, LANES), lambda b, c: (0, 0, 0)),
                  pl.BlockSpec((TAB_ROWS, STATE_LANES), lambda b, c: (0, 0)),
                  pl.BlockSpec((1, SSM_WIDTH), lambda b, c: (0, 0))],
        out_specs=[pl.BlockSpec((tt, SSM_WIDTH), lambda b, c: (b * nt + nt - 1 - c, 0)),
                   pl.BlockSpec((SUBLANES, STATE_LANES), lambda b, c: (0, 0)),
                   pl.BlockSpec((SSM_CHUNKS, LANES, CHUNK_LANES), lambda b, c: (0, 0, 0)),
                   pl.BlockSpec((SSM_CHUNKS, CHUNK_LANES, LANES), lambda b, c: (0, 0, 0)),
                   pl.BlockSpec((SUBLANES, SSM_WIDTH), lambda b, c: (0, 0))],
        out_shape=[jax.ShapeDtypeStruct((bsz * seq, SSM_WIDTH), BF16),
                   jax.ShapeDtypeStruct((SUBLANES, STATE_LANES), F32),
                   jax.ShapeDtypeStruct((SSM_CHUNKS, LANES, CHUNK_LANES), F32),
                   jax.ShapeDtypeStruct((SSM_CHUNKS, CHUNK_LANES, LANES), F32),
                   jax.ShapeDtypeStruct((SUBLANES, SSM_WIDTH), F32)],
        scratch_shapes=[pltpu.VMEM((tt, STATE_LANES), F32), pltpu.VMEM((SUBLANES, STATE_LANES), F32)],
        compiler_params=_params(("arbitrary", "arbitrary")),
    )(dys, rest, hs, hs, wb4, wc4, tabs, dskip)


def _gelu(v):
    t = jnp.tanh(GELU_C * (v + GELU_A * v * v * v))
    return 0.5 * v * (1.0 + t), t


def mix_fwd(ya, ys, rest, x0, wglu, bglu, wba, wbb, wout):
    t, d = x0.shape
    tm = _rows(t, 256)

    def body(ya_ref, ys_ref, ga_ref, gb_ref, x_ref, wg_ref, bg_ref, wa_ref, wb_ref, wo_ref,
             x1_ref, z_ref, pa_ref, pb_ref, yb_ref, yb2_ref, mx_ref):
        yb, _ = _gelu(ys_ref[...])
        ybb = yb.astype(BF16)
        z = _dot(ybb, wg_ref[...]) + bg_ref[...]
        yb2 = (yb * _sigmoid(z)).astype(BF16)
        pa = _dot(ya_ref[...], wa_ref[...])
        pb = _dot(yb2, wb_ref[...])
        mixed = (_sigmoid(ga_ref[...]) * pa + _sigmoid(gb_ref[...]) * pb).astype(BF16)
        x1_ref[...] = x_ref[...] + _dot(mixed, wo_ref[...])
        z_ref[...] = z.astype(z_ref.dtype)
        pa_ref[...] = pa.astype(pa_ref.dtype)
        pb_ref[...] = pb.astype(pb_ref.dtype)
        yb_ref[...] = ybb
        yb2_ref[...] = yb2
        mx_ref[...] = mixed

    row = lambda w: pl.BlockSpec((tm, w), lambda i: (i, 0))
    full = lambda a: pl.BlockSpec(a.shape, lambda i: (0,) * a.ndim)
    return pl.pallas_call(
        body, name="mix_fwd", grid=(t // tm,),
        in_specs=[row(ATTN_WIDTH), row(SSM_WIDTH),
                  pl.BlockSpec((tm, d), lambda i: (i, 0)), pl.BlockSpec((tm, d), lambda i: (i, 1)),
                  row(d), full(wglu), full(bglu), full(wba), full(wbb), full(wout)],
        out_specs=[row(d), row(SSM_WIDTH), row(d), row(d), row(SSM_WIDTH), row(SSM_WIDTH), row(d)],
        out_shape=[jax.ShapeDtypeStruct((t, d), F32), jax.ShapeDtypeStruct((t, SSM_WIDTH), BF16),
                   jax.ShapeDtypeStruct((t, d), BF16), jax.ShapeDtypeStruct((t, d), BF16),
                   jax.ShapeDtypeStruct((t, SSM_WIDTH), BF16), jax.ShapeDtypeStruct((t, SSM_WIDTH), BF16),
                   jax.ShapeDtypeStruct((t, d), BF16)],
        compiler_params=_params(("parallel",)),
    )(ya, ys, rest, rest, x0, wglu, bglu, wba, wbb, wout)


def mix_bwd(dx1, rest, pa, pb, z, ys, wglu, wba, wbb, wout):
    t, d = dx1.shape
    tm = _rows(t, 256)

    def body(dx_ref, ga_ref, gb_ref, pa_ref, pb_ref, z_ref, ys_ref, wg_ref, wa_ref, wb_ref, wo_ref,
             dya_ref, dys_ref, dg_ref, dpa_ref, dpb_ref, dz_ref, dbg_ref):
        @pl.when(pl.program_id(0) == 0)
        def _():
            dbg_ref[...] = jnp.zeros_like(dbg_ref)

        dmix = _dot(dx_ref[...].astype(BF16), wo_ref[...], NT)
        sa = _sigmoid(ga_ref[...])
        sb = _sigmoid(gb_ref[...])
        dpa = (dmix * sa).astype(BF16)
        dpb = (dmix * sb).astype(BF16)
        dg_ref[:, 0:d] = (dmix * pa_ref[...].astype(F32) * sa * (1.0 - sa)).astype(dg_ref.dtype)
        dg_ref[:, d:2 * d] = (dmix * pb_ref[...].astype(F32) * sb * (1.0 - sb)).astype(dg_ref.dtype)
        dpa_ref[...] = dpa
        dpb_ref[...] = dpb
        dya_ref[...] = _dot(dpa, wa_ref[...], NT).astype(dya_ref.dtype)
        dyb2 = _dot(dpb, wb_ref[...], NT)
        ys = ys_ref[...]
        yb, th = _gelu(ys)
        sg = _sigmoid(z_ref[...].astype(F32))
        dz = dyb2 * yb * sg * (1.0 - sg)
        dzb = dz.astype(BF16)
        dz_ref[...] = dzb
        dbg_ref[...] += _fold8(dz)
        dyb = dyb2 * sg + _dot(dzb, wg_ref[...], NT)
        dgelu = 0.5 * (1.0 + th) + 0.5 * ys * (1.0 - th * th) * GELU_C * (1.0 + 3.0 * GELU_A * ys * ys)
        dys_ref[...] = (dyb * dgelu).astype(dys_ref.dtype)

    row = lambda w: pl.BlockSpec((tm, w), lambda i: (i, 0))
    full = lambda a: pl.BlockSpec(a.shape, lambda i: (0,) * a.ndim)
    return pl.pallas_call(
        body, name="mix_bwd", grid=(t // tm,),
        in_specs=[row(d), pl.BlockSpec((tm, d), lambda i: (i, 0)), pl.BlockSpec((tm, d), lambda i: (i, 1)),
                  row(d), row(d), row(SSM_WIDTH), row(SSM_WIDTH), full(wglu), full(wba), full(wbb), full(wout)],
        out_specs=[row(ATTN_WIDTH), row(SSM_WIDTH), row(2 * d), row(d), row(d), row(SSM_WIDTH),
                   pl.BlockSpec((SUBLANES, SSM_WIDTH), lambda i: (0, 0))],
        out_shape=[jax.ShapeDtypeStruct((t, ATTN_WIDTH), BF16), jax.ShapeDtypeStruct((t, SSM_WIDTH), BF16),
                   jax.ShapeDtypeStruct((t, 2 * d), BF16), jax.ShapeDtypeStruct((t, d), BF16),
                   jax.ShapeDtypeStruct((t, d), BF16), jax.ShapeDtypeStruct((t, SSM_WIDTH), BF16),
                   jax.ShapeDtypeStruct((SUBLANES, SSM_WIDTH), F32)],
        compiler_params=_params(("arbitrary",)),
    )(dx1, rest, rest, pa, pb, z, ys, wglu, wba, wbb, wout)


def mlp_fwd(x1, g, wup, wdown):
    t, d = x1.shape
    ff = wup.shape[1]
    tm, tf = _rows(t, 512), _pick(ff, 1024)
    nf = ff // tf

    def body(x_ref, g_ref, wu_ref, wd_ref, x2_ref, up_ref, h_s, acc_s):
        f = pl.program_id(1)

        @pl.when(f == 0)
        def _():
            xv = x_ref[...]
            h_s[...] = (xv * _rms_scale(xv) * g_ref[...]).astype(BF16)
            acc_s[...] = jnp.zeros_like(acc_s)

        up = _dot(h_s[...], wu_ref[...])
        up_ref[...] = up.astype(up_ref.dtype)
        act = jnp.square(jnp.maximum(up, 0.0)).astype(BF16)
        acc_s[...] += _dot(act, wd_ref[...])

        @pl.when(f == nf - 1)
        def _():
            x2_ref[...] = x_ref[...] + acc_s[...]

    return pl.pallas_call(
        body, name="mlp_fwd", grid=(t // tm, nf),
        in_specs=[pl.BlockSpec((tm, d), lambda i, f: (i, 0)), pl.BlockSpec((1, d), lambda i, f: (0, 0)),
                  pl.BlockSpec((d, tf), lambda i, f: (0, f)), pl.BlockSpec((tf, d), lambda i, f: (f, 0))],
        out_specs=[pl.BlockSpec((tm, d), lambda i, f: (i, 0)), pl.BlockSpec((tm, tf), lambda i, f: (i, f))],
        out_shape=[jax.ShapeDtypeStruct((t, d), F32), jax.ShapeDtypeStruct((t, ff), BF16)],
        scratch_shapes=[pltpu.VMEM((tm, d), BF16), pltpu.VMEM((tm, d), F32)],
        compiler_params=_params(("parallel", "arbitrary")),
    )(x1, g, wup, wdown)


def mlp_bwd(dx2, up, x1, g, wup, wdown):
    t, d = x1.shape
    ff = wup.shape[1]
    tm, tf = _rows(t, 512), _pick(ff, 1024)
    nf = ff // tf

    def body(dx_ref, up_ref, x_ref, g_ref, wu_ref, wd_ref, dup_ref, dx1_ref, dg_ref, dxb_s, acc_s):
        i = pl.program_id(0)
        f = pl.program_id(1)

        @pl.when((i == 0) & (f == 0))
        def _():
            dg_ref[...] = jnp.zeros_like(dg_ref)

        @pl.when(f == 0)
        def _():
            dxb_s[...] = dx_ref[...].astype(BF16)
            acc_s[...] = jnp.zeros_like(acc_s)

        dact = _dot(dxb_s[...], wd_ref[...], NT)
        dup = (dact * 2.0 * jnp.maximum(up_ref[...].astype(F32), 0.0)).astype(BF16)
        dup_ref[...] = dup
        acc_s[...] += _dot(dup, wu_ref[...], NT)

        @pl.when(f == nf - 1)
        def _():
            dxn, dgain = _rms_bwd(x_ref[...], g_ref[...], acc_s[...])
            dx1_ref[...] = dx_ref[...] + dxn
            dg_ref[...] += _fold8(dgain)

    return pl.pallas_call(
        body, name="mlp_bwd", grid=(t // tm, nf),
        in_specs=[pl.BlockSpec((tm, d), lambda i, f: (i, 0)), pl.BlockSpec((tm, tf), lambda i, f: (i, f)),
                  pl.BlockSpec((tm, d), lambda i, f: (i, 0)), pl.BlockSpec((1, d), lambda i, f: (0, 0)),
                  pl.BlockSpec((d, tf), lambda i, f: (0, f)), pl.BlockSpec((tf, d), lambda i, f: (f, 0))],
        out_specs=[pl.BlockSpec((tm, tf), lambda i, f: (i, f)), pl.BlockSpec((tm, d), lambda i, f: (i, 0)),
                   pl.BlockSpec((SUBLANES, d), lambda i, f: (0, 0))],
        out_shape=[jax.ShapeDtypeStruct((t, ff), BF16), jax.ShapeDtypeStruct((t, d), F32),
                   jax.ShapeDtypeStruct((SUBLANES, d), F32)],
        scratch_shapes=[pltpu.VMEM((tm, d), BF16), pltpu.VMEM((tm, d), F32)],
        compiler_params=_params(("arbitrary", "arbitrary")),
    )(dx2, up, x1, g, wup, wdown)


def proj_bwd(dproj, wpad, x0, dx1, g):
    t, d = x0.shape
    m = wpad.shape[1]
    tm = _rows(t, 256)

    def body(dp_ref, w_ref, x_ref, dx1_ref, g_ref, dx0_ref, dg_ref):
        @pl.when(pl.program_id(0) == 0)
        def _():
            dg_ref[...] = jnp.zeros_like(dg_ref)

        dh = _dot(dp_ref[...], w_ref[...], NT)
        dxn, dgain = _rms_bwd(x_ref[...], g_ref[...], dh)
        dx0_ref[...] = dx1_ref[...] + dxn
        dg_ref[...] += _fold8(dgain)

    return pl.pallas_call(
        body, name="proj_bwd", grid=(t // tm,),
        in_specs=[pl.BlockSpec((tm, m), lambda i: (i, 0)), pl.BlockSpec((d, m), lambda i: (0, 0)),
                  pl.BlockSpec((tm, d), lambda i: (i, 0)), pl.BlockSpec((tm, d), lambda i: (i, 0)),
                  pl.BlockSpec((1, d), lambda i: (0, 0))],
        out_specs=[pl.BlockSpec((tm, d), lambda i: (i, 0)), pl.BlockSpec((SUBLANES, d), lambda i: (0, 0))],
        out_shape=[jax.ShapeDtypeStruct((t, d), F32), jax.ShapeDtypeStruct((SUBLANES, d), F32)],
        compiler_params=_params(("arbitrary",)),
    )(dproj, wpad, x0, dx1, g)


def final_loss(x, g, target):
    t, d = x.shape
    tm = _rows(t, 512)

    def body(x_ref, g_ref, t_ref, dx_ref, ls_ref, dg_ref):
        @pl.when(pl.program_id(0) == 0)
        def _():
            ls_ref[...] = jnp.zeros_like(ls_ref)
            dg_ref[...] = jnp.zeros_like(dg_ref)

        xv = x_ref[...]
        gv = g_ref[...]
        err = xv * _rms_scale(xv) * gv - t_ref[...]
        ls_ref[...] += _fold8(err * err) * (0.5 / d)
        dxn, dgain = _rms_bwd(xv, gv, err * (1.0 / d))
        dx_ref[...] = dxn
        dg_ref[...] += _fold8(dgain)

    return pl.pallas_call(
        body, name="final_loss", grid=(t // tm,),
        in_specs=[pl.BlockSpec((tm, d), lambda i: (i, 0)), pl.BlockSpec((1, d), lambda i: (0, 0)),
                  pl.BlockSpec((tm, d), lambda i: (i, 0))],
        out_specs=[pl.BlockSpec((tm, d), lambda i: (i, 0)), pl.BlockSpec((SUBLANES, d), lambda i: (0, 0)),
                   pl.BlockSpec((SUBLANES, d), lambda i: (0, 0))],
        out_shape=[jax.ShapeDtypeStruct((t, d), F32), jax.ShapeDtypeStruct((SUBLANES, d), F32),
                   jax.ShapeDtypeStruct((SUBLANES, d), F32)],
        compiler_params=_params(("arbitrary",)),
    )(x, g, target)


def _place():
    x, y, c = lax.axis_index("x"), lax.axis_index("y"), lax.axis_index("c")
    chips = [(1 - x, y), (x, 1 - y), (1 - x, 1 - y)]
    return x, y, c, chips


def _exchange_call(body, name, ins, out_shapes, copies_per_tensor):
    n = len(ins)
    scratch = [pltpu.SemaphoreType.DMA((copies_per_tensor * n,)), pltpu.SemaphoreType.DMA((copies_per_tensor * n,))]
    return pl.pallas_call(
        body, name=name, in_specs=[ANY] * n, out_specs=[ANY] * n, out_shape=out_shapes, scratch_shapes=scratch,
    )(*ins)


def gather_weights(ws):
    n = len(ws)
    lh = ws[0].shape[0] // 2

    def body(*refs):
        w_refs, out_refs = refs[:n], refs[n:2 * n]
        send_sems, recv_sems = refs[2 * n:]
        x, y, c, chips = _place()
        me = 2 * x + y
        sibling = (x, y, 1 - c)

        def half(t, chip_idx, core):
            return out_refs[t].at[chip_idx, pl.ds(core * lh, lh)]

        def copy(t, k, src, dst, to):
            return pltpu.make_async_remote_copy(src_ref=src, dst_ref=dst, send_sem=send_sems.at[6 * t + k],
                                                recv_sem=recv_sems.at[6 * t + k], device_id=to, device_id_type=MESH)

        first = [copy(t, j, w_refs[t].at[pl.ds(c * lh, lh)], half(t, me, c), (*chip, c))
                 for j, chip in enumerate(chips) for t in range(n)]
        for cp in first:
            cp.start()
        passed = []
        for j, (cx, cy) in enumerate(chips):
            idx = 2 * cx + cy
            for t in range(n):
                copy(t, j, half(t, idx, c), half(t, idx, c), (x, y, c)).wait_recv()
                fwd = copy(t, 3 + j, half(t, idx, c), half(t, idx, c), sibling)
                fwd.start()
                passed.append(fwd)
        for j, (cx, cy) in enumerate(chips):
            idx = 2 * cx + cy
            for t in range(n):
                copy(t, 3 + j, half(t, idx, 1 - c), half(t, idx, 1 - c), (x, y, c)).wait_recv()
        for cp in first + passed:
            cp.wait_send()

    return _exchange_call(body, "gather_weights", ws,
                          [jax.ShapeDtypeStruct((4,) + w.shape, w.dtype) for w in ws], 6)


def exchange_sibling_half(gs):
    n = len(gs)
    lh = gs[0].shape[1] // 2

    def body(*refs):
        g_refs, out_refs = refs[:n], refs[n:2 * n]
        send_sems, recv_sems = refs[2 * n:]
        x, y, c, _ = _place()
        cps = [pltpu.make_async_remote_copy(src_ref=g_refs[t].at[:, pl.ds((1 - c) * lh, lh)], dst_ref=out_refs[t],
                                            send_sem=send_sems.at[t], recv_sem=recv_sems.at[t],
                                            device_id=(x, y, 1 - c), device_id_type=MESH) for t in range(n)]
        for cp in cps:
            cp.start()
        for cp in cps:
            cp.wait()

    return _exchange_call(body, "exchange_sibling_half", gs,
                          [jax.ShapeDtypeStruct((4, lh) + g.shape[2:], g.dtype) for g in gs], 1)


def scatter_to_owners(ps):
    n = len(ps)

    def body(*refs):
        a_refs, out_refs = refs[:n], refs[n:2 * n]
        send_sems, recv_sems = refs[2 * n:]
        x, y, c, chips = _place()
        sends = [pltpu.make_async_remote_copy(src_ref=a_refs[t].at[2 * cx + cy], dst_ref=out_refs[t].at[j],
                                              send_sem=send_sems.at[3 * t + j], recv_sem=recv_sems.at[3 * t + j],
                                              device_id=(cx, cy, c), device_id_type=MESH)
                 for j, (cx, cy) in enumerate(chips) for t in range(n)]
        for cp in sends:
            cp.start()
        for cp in sends:
            cp.wait()

    return _exchange_call(body, "scatter_to_owners", ps,
                          [jax.ShapeDtypeStruct((3,) + p.shape[1:], p.dtype) for p in ps], 3)


def share_with_sibling(rs):
    n = len(rs)

    def body(*refs):
        r_refs, out_refs = refs[:n], refs[n:2 * n]
        send_sems, recv_sems = refs[2 * n:]
        x, y, c, _ = _place()
        cps = [pltpu.make_async_remote_copy(src_ref=r_refs[t], dst_ref=out_refs[t],
                                            send_sem=send_sems.at[t], recv_sem=recv_sems.at[t],
                                            device_id=(x, y, 1 - c), device_id_type=MESH) for t in range(n)]
        for cp in cps:
            cp.start()
        for cp in cps:
            cp.wait()

    return _exchange_call(body, "share_with_sibling", rs,
                          [jax.ShapeDtypeStruct(r.shape, r.dtype) for r in rs], 1)


def exchange_all(sp):
    def body(s_ref, out_ref, send_sems, recv_sems):
        x, y, c, _ = _place()
        me = 4 * x + 2 * y + c
        sends = []
        for m in range(1, 8):
            fx, fy, fc = (m >> 2) & 1, (m >> 1) & 1, m & 1
            peer = ((1 - x) if fx else x, (1 - y) if fy else y, (1 - c) if fc else c)
            cp = pltpu.make_async_remote_copy(src_ref=s_ref, dst_ref=out_ref.at[me],
                                              send_sem=send_sems.at[m - 1], recv_sem=recv_sems.at[m - 1],
                                              device_id=peer, device_id_type=MESH)
            cp.start()
            sends.append(cp)
        for cp in sends:
            cp.wait()

    return pl.pallas_call(
        body, name="exchange_all",
        in_specs=[ANY], out_specs=ANY,
        out_shape=jax.ShapeDtypeStruct((8,) + sp.shape, sp.dtype),
        scratch_shapes=[pltpu.SemaphoreType.DMA((7,)), pltpu.SemaphoreType.DMA((7,))],
    )(sp)


def add_sibling(gp, s1, core, name):
    _, nl, r, cdim = gp.shape
    lh = nl // 2
    tr = _rows(r, PACK_ROW_TILE)

    def body(c_ref, g_ref, s_ref, o_ref):
        o_ref[...] = (g_ref[...] + s_ref[...]).astype(o_ref.dtype)

    return pl.pallas_call(
        body, name=name,
        grid_spec=pltpu.PrefetchScalarGridSpec(
            num_scalar_prefetch=1, grid=(4, lh, r // tr),
            in_specs=[pl.BlockSpec((1, 1, tr, cdim), lambda k, l, i, c_ref: (k, c_ref[0] * lh + l, i, 0)),
                      pl.BlockSpec((1, 1, tr, cdim), lambda k, l, i, c_ref: (k, l, i, 0))],
            out_specs=pl.BlockSpec((1, 1, tr, cdim), lambda k, l, i, c_ref: (k, l, i, 0))),
        out_shape=jax.ShapeDtypeStruct(s1.shape, BF16),
        compiler_params=_params(("parallel", "parallel", "parallel")),
    )(core, gp, s1)


def add_chips(s2, pair, chip, name):
    _, lh, r, cdim = s2.shape
    tr = _rows(r, PACK_ROW_TILE)

    def body(k_ref, s_ref, p_ref, o_ref):
        o_ref[0] = ((s_ref[0, 0].astype(F32) + s_ref[1, 0].astype(F32)) + s_ref[2, 0].astype(F32)) + p_ref[0, 0].astype(F32)

    return pl.pallas_call(
        body, name=name,
        grid_spec=pltpu.PrefetchScalarGridSpec(
            num_scalar_prefetch=1, grid=(lh, r // tr),
            in_specs=[pl.BlockSpec((3, 1, tr, cdim), lambda l, i, k_ref: (0, l, i, 0)),
                      pl.BlockSpec((1, 1, tr, cdim), lambda l, i, k_ref: (k_ref[0], l, i, 0))],
            out_specs=pl.BlockSpec((1, tr, cdim), lambda l, i, k_ref: (l, i, 0))),
        out_shape=jax.ShapeDtypeStruct(s2.shape[1:], F32),
        compiler_params=_params(("parallel", "parallel")),
    )(chip, s2, pair)


def _adamw_math(w, g, m, v):
    m = ADAM_B1 * m + (1.0 - ADAM_B1) * g
    v = ADAM_B2 * v + (1.0 - ADAM_B2) * (g * g)
    m_hat = m / (1.0 - ADAM_B1 ** ADAM_STEP)
    v_hat = v / (1.0 - ADAM_B2 ** ADAM_STEP)
    delta = -ADAM_LR * (m_hat / (jnp.sqrt(v_hat) + ADAM_EPS) + ADAM_WD * w)
    return delta, m, v


def adamw(w, g_own, g_sibling, m, v, core, name):
    nl, r, cdim = w.shape
    lh = nl // 2
    tr = _rows(r, PACK_ROW_TILE)

    def body(c_ref, w_ref, go_ref, gs_ref, m_ref, v_ref, g_ref, d_ref, nm_ref, nv_ref):
        own = pl.program_id(0) // lh == c_ref[0]
        g = jnp.where(own, go_ref[...], gs_ref[...])
        d, nm, nv = _adamw_math(w_ref[...], g, m_ref[...], v_ref[...])
        g_ref[...] = g
        d_ref[...] = d
        nm_ref[...] = nm
        nv_ref[...] = nv

    spec = pl.BlockSpec((1, tr, cdim), lambda l, i, c_ref: (l, i, 0))
    own_spec = pl.BlockSpec((1, tr, cdim), lambda l, i, c_ref: (jnp.clip(l - c_ref[0] * lh, 0, lh - 1), i, 0))
    sib_spec = pl.BlockSpec((1, tr, cdim), lambda l, i, c_ref: (jnp.clip(l - (1 - c_ref[0]) * lh, 0, lh - 1), i, 0))
    return pl.pallas_call(
        body, name=name,
        grid_spec=pltpu.PrefetchScalarGridSpec(
            num_scalar_prefetch=1, grid=(nl, r // tr),
            in_specs=[spec, own_spec, sib_spec, spec, spec], out_specs=[spec] * 4),
        out_shape=[jax.ShapeDtypeStruct(w.shape, F32)] * 4,
        compiler_params=_params(("parallel", "parallel")),
    )(core, w, g_own, g_sibling, m, v)


def sum_and_adamw(parts, own, me, w, m, v):
    _, r, cdim = parts.shape
    tr = _rows(r, PACK_ROW_TILE)

    def body(me_ref, p_ref, o_ref, w_ref, m_ref, v_ref, g_ref, d_ref, nm_ref, nv_ref):
        part = lambda k: jnp.where(me_ref[0] == k, o_ref[...], p_ref[k])
        g = part(0)
        for k in range(1, 8):
            g = g + part(k)
        d, nm, nv = _adamw_math(w_ref[...], g, m_ref[...], v_ref[...])
        g_ref[...] = g
        d_ref[...] = d
        nm_ref[...] = nm
        nv_ref[...] = nv

    spec = pl.BlockSpec((tr, cdim), lambda i, me_ref: (i, 0))
    return pl.pallas_call(
        body, name="sum_and_adamw",
        grid_spec=pltpu.PrefetchScalarGridSpec(
            num_scalar_prefetch=1, grid=(r // tr,),
            in_specs=[pl.BlockSpec((8, tr, cdim), lambda i, me_ref: (0, i, 0)), spec, spec, spec, spec],
            out_specs=[spec] * 4),
        out_shape=[jax.ShapeDtypeStruct((r, cdim), F32)] * 4,
        compiler_params=_params(("parallel",)),
    )(me, parts, own, w, m, v)


SHARDED = ("w_in", "w_glu", "w_branch_a", "w_branch_b", "w_out", "w_mlp_up", "w_mlp_down")
SHARD_AXIS = {"w_in": 2, "w_glu": 1, "w_branch_a": 2, "w_branch_b": 2, "w_out": 1, "w_mlp_up": 2, "w_mlp_down": 1}
SMALL = ("norm_mix", "b_forget", "ssm_lambda_re", "ssm_lambda_im", "ssm_log_dt", "ssm_b_re", "ssm_b_im",
         "ssm_c_re", "ssm_c_im", "ssm_d", "b_glu", "norm_mlp", "norm_final")


def pack_flat(arrs):
    flat = jnp.concatenate([a.reshape(-1).astype(F32) for a in arrs])
    unit = PACK_COLS * PACK_ROW_TILE
    rows = (flat.shape[0] + unit - 1) // unit * PACK_ROW_TILE
    return jnp.pad(flat, (0, rows * PACK_COLS - flat.shape[0])).reshape(rows, PACK_COLS)


def unpack_flat(packed, shapes):
    flat = packed.reshape(-1)
    out, off = [], 0
    for s in shapes:
        n = math.prod(s)
        out.append(flat[off:off + n].reshape(tuple(s)))
        off += n
    return out


def _discretise(lam_re, lam_im, log_dt, b_re, b_im):
    dt = jnp.exp(log_dt)[:, None]
    mag = jnp.exp(lam_re * dt)
    ar = mag * jnp.cos(lam_im * dt)
    ai = mag * jnp.sin(lam_im * dt)
    den = lam_re * lam_re + lam_im * lam_im
    cr = ((ar - 1.0) * lam_re + ai * lam_im) / den
    ci = (ai * lam_re - (ar - 1.0) * lam_im) / den
    bbr = cr[:, :, None] * b_re - ci[:, :, None] * b_im
    bbi = cr[:, :, None] * b_im + ci[:, :, None] * b_re
    return ar, ai, bbr, bbi


def _state_lanes(re, im):
    return jnp.concatenate([re.reshape(SSM_CHUNKS, CHUNK_STATES), im.reshape(SSM_CHUNKS, CHUNK_STATES)],
                           axis=1).reshape(STATE_LANES)


def _ssm_inputs(ar, ai, bbr, bbi, c_re, c_im):
    pr, pi = [ar], [ai]
    for _ in range(SUBLANES - 1):
        pr, pi = pr + [pr[-1] * ar - pi[-1] * ai], pi + [pr[-1] * ai + pi[-1] * ar]
    power = lambda n, sign: _state_lanes(pr[n - 1], sign * pi[n - 1])
    idx = jnp.arange(SUBLANES)
    masked = lambda n, sign, keep: jnp.where(keep[:, None], power(n, sign)[None, :], 0.0)
    tabs = jnp.concatenate(
        [masked(s, 1.0, idx >= s) for s in SCAN_STEPS]
        + [jnp.stack([power(i + 1, 1.0) for i in range(SUBLANES)])]
        + [masked(s, -1.0, idx < SUBLANES - s) for s in SCAN_STEPS]
        + [jnp.stack([power(SUBLANES - i, -1.0) for i in range(SUBLANES)])], axis=0)
    eye = jnp.eye(CHUNK_GROUPS, dtype=F32)

    def to_wb(bb):
        t = bb.reshape(SSM_CHUNKS, CHUNK_GROUPS, SSM_STATE, SSM_GROUP_CH).transpose(0, 1, 3, 2)
        return jnp.einsum("jgcp,gh->jgchp", t, eye).reshape(SSM_CHUNKS, LANES, CHUNK_STATES)

    def to_wc(cc):
        t = cc.reshape(SSM_CHUNKS, CHUNK_GROUPS, SSM_GROUP_CH, SSM_STATE)
        return jnp.einsum("jgcp,gh->jhpgc", t, eye).reshape(SSM_CHUNKS, CHUNK_STATES, LANES)

    wb4 = jnp.concatenate([to_wb(bbr), to_wb(bbi)], axis=2).astype(BF16)
    wc4 = jnp.concatenate([to_wc(c_re), -to_wc(c_im)], axis=1).astype(BF16)
    return tabs, wb4, wc4


def _ssm_param_grads(ga8, gwb, gwc):
    eye = jnp.eye(CHUNK_GROUPS, dtype=F32)
    ga = jnp.sum(ga8, axis=0).reshape(SSM_CHUNKS, 2, CHUNK_STATES)
    gar = ga[:, 0].reshape(SSM_GROUPS, SSM_STATE)
    gai = ga[:, 1].reshape(SSM_GROUPS, SSM_STATE)

    def from_wb(g):
        t = g.reshape(SSM_CHUNKS, CHUNK_GROUPS, SSM_GROUP_CH, CHUNK_GROUPS, SSM_STATE)
        return jnp.einsum("jgchp,gh->jgpc", t, eye).reshape(SSM_GROUPS, SSM_STATE, SSM_GROUP_CH)

    def from_wc(g):
        t = g.reshape(SSM_CHUNKS, CHUNK_GROUPS, SSM_STATE, CHUNK_GROUPS, SSM_GROUP_CH)
        return jnp.einsum("jhpgc,gh->jgcp", t, eye).reshape(SSM_GROUPS, SSM_GROUP_CH, SSM_STATE)

    return (gar, gai, from_wb(gwb[:, :, :CHUNK_STATES]), from_wb(gwb[:, :, CHUNK_STATES:]),
            from_wc(gwc[:, :CHUNK_STATES]), -from_wc(gwc[:, CHUNK_STATES:]))


def kernel(x, norm_mix, w_in, b_forget, ssm_lambda_re, ssm_lambda_im, ssm_log_dt, ssm_b_re, ssm_b_im, ssm_c_re, ssm_c_im, ssm_d, w_glu, b_glu, w_branch_a, w_branch_b, w_out, norm_mlp, w_mlp_up, w_mlp_down, norm_final, loss_target, m_norm_mix, m_w_in, m_b_forget, m_ssm_lambda_re, m_ssm_lambda_im, m_ssm_log_dt, m_ssm_b_re, m_ssm_b_im, m_ssm_c_re, m_ssm_c_im, m_ssm_d, m_w_glu, m_b_glu, m_w_branch_a, m_w_branch_b, m_w_out, m_norm_mlp, m_w_mlp_up, m_w_mlp_down, m_norm_final, v_norm_mix, v_w_in, v_b_forget, v_ssm_lambda_re, v_ssm_lambda_im, v_ssm_log_dt, v_ssm_b_re, v_ssm_b_im, v_ssm_c_re, v_ssm_c_im, v_ssm_d, v_w_glu, v_b_glu, v_w_branch_a, v_w_branch_b, v_w_out, v_norm_mlp, v_w_mlp_up, v_w_mlp_down, v_norm_final):
    args = dict(locals())
    bsz, seq, d = x.shape
    nl = norm_mix.shape[0]
    tokens = bsz * seq
    aw, sw = ATTN_WIDTH, SSM_WIDTH
    core = lax.axis_index("c").astype(jnp.int32).reshape(1)
    chip = (2 * lax.axis_index("x") + lax.axis_index("y")).astype(jnp.int32).reshape(1)
    device = 2 * chip + core

    own = {n: args[n].astype(BF16) for n in SHARDED}
    gathered = dict(zip(SHARDED, gather_weights([own[n] for n in SHARDED])))
    full = {n: jnp.concatenate([jnp.where(chip[0] == k, own[n], gathered[n][k]) for k in range(4)],
                               axis=SHARD_AXIS[n]) for n in SHARDED}
    win_f = full["w_in"]
    o_f, o_u, o_ga, o_gb = 3 * aw, 3 * aw + ATTN_HEADS, 3 * aw + ATTN_HEADS + sw, 3 * aw + ATTN_HEADS + sw + d
    w_qkv = win_f[:, :, :o_f]
    w_rest = jnp.concatenate([win_f[:, :, o_ga:o_gb], win_f[:, :, o_gb:], win_f[:, :, o_u:o_ga],
                              jnp.pad(win_f[:, :, o_f:o_u], ((0, 0), (0, 0), (0, F_PAD - ATTN_HEADS)))], axis=2)
    w_pad = jnp.concatenate([w_qkv, w_rest], axis=2)
    u_blk = 2 * d // sw
    f_blk = (2 * d + sw) // F_PAD
    bf_pad = jnp.pad(b_forget, ((0, 0), (0, F_PAD - ATTN_HEADS)))

    disc = [jax.vjp(_discretise, ssm_lambda_re[l], ssm_lambda_im[l], ssm_log_dt[l], ssm_b_re[l], ssm_b_im[l])
            for l in range(nl)]

    xs = x.reshape(tokens, d)
    saved = []
    for l in range(nl):
        g1 = norm_mix[l].reshape(1, d)
        qkv = norm_matmul(xs, g1, w_qkv[l], BF16, "proj_qkv")
        rest = norm_matmul(xs, g1, w_rest[l], F32, "proj_rest")
        cumcol, cumrow = forget_cumsum(rest, bf_pad[l:l + 1], bsz, seq, f_blk)
        ya, lser = fox_fwd(qkv, cumrow, bsz, seq)
        tabs, wb4, wc4 = _ssm_inputs(*disc[l][0], ssm_c_re[l], ssm_c_im[l])
        dskip = ssm_d[l].reshape(1, sw)
        ys, hs = ssm_fwd(rest, wb4, wc4, tabs, dskip, bsz, seq, u_blk)
        x1, z, pa, pb, yb, yb2, mixed = mix_fwd(ya, ys, rest, xs, full["w_glu"][l], b_glu[l].reshape(1, sw),
                                                 full["w_branch_a"][l], full["w_branch_b"][l], full["w_out"][l])
        x2, up = mlp_fwd(x1, norm_mlp[l].reshape(1, d), full["w_mlp_up"][l], full["w_mlp_down"][l])
        saved.append(dict(x0=xs, qkv=qkv, rest=rest, cumcol=cumcol, ya=ya, lser=lser,
                          tabs=tabs, wb4=wb4, wc4=wc4, dskip=dskip, ys=ys, hs=hs, x1=x1, z=z, pa=pa, pb=pb,
                          yb=yb, yb2=yb2, mixed=mixed, up=up))
        xs = x2
    dx, loss_rows, dgf_rows = final_loss(xs, norm_final.reshape(1, d), loss_target.reshape(tokens, d))
    loss = lax.psum(jnp.sum(loss_rows), ("x", "y", "c"))

    big = {n: [None] * nl for n in SHARDED}
    small = {n: [None] * nl for n in SMALL if n != "norm_final"}
    for l in reversed(range(nl)):
        s = saved[l]
        g2 = norm_mlp[l].reshape(1, d)
        dup, dx1, dg2 = mlp_bwd(dx, s["up"], s["x1"], g2, full["w_mlp_up"][l], full["w_mlp_down"][l])
        big["w_mlp_down"][l] = matmul_tn(s["up"], dx, "grad_w_mlp_down", a_kind="relu2", shard_axis=0)
        big["w_mlp_up"][l] = matmul_tn(s["x1"], dup, "grad_w_mlp_up", a_kind="norm", gain=g2, shard_axis=1)
        small["norm_mlp"][l] = jnp.sum(dg2, axis=0)
        dya, dys, dgab, dpa, dpb, dz, dbg = mix_bwd(dx1, s["rest"], s["pa"], s["pb"], s["z"], s["ys"],
                                                    full["w_glu"][l], full["w_branch_a"][l], full["w_branch_b"][l],
                                                    full["w_out"][l])
        big["w_out"][l] = matmul_tn(s["mixed"], dx1, "grad_w_out", shard_axis=0)
        big["w_branch_a"][l] = matmul_tn(s["ya"], dpa, "grad_w_branch_a", shard_axis=1)
        big["w_branch_b"][l] = matmul_tn(s["yb2"], dpb, "grad_w_branch_b", shard_axis=1)
        big["w_glu"][l] = matmul_tn(s["yb"], dz, "grad_w_glu", shard_axis=0)
        small["b_glu"][l] = jnp.sum(dbg, axis=0)
        du, ga8, gwb, gwc, gd8 = ssm_bwd(dys, s["rest"], s["hs"], s["wb4"], s["wc4"], s["tabs"], s["dskip"],
                                         bsz, seq, u_blk)
        gar, gai, gbbr, gbbi, gcr, gci = _ssm_param_grads(ga8, gwb, gwc)
        glr, gli, gdt, gbr, gbi = disc[l][1]((gar, gai, gbbr, gbbi))
        small["ssm_lambda_re"][l], small["ssm_lambda_im"][l], small["ssm_log_dt"][l] = glr, gli, gdt
        small["ssm_b_re"][l], small["ssm_b_im"][l] = gbr, gbi
        small["ssm_c_re"][l], small["ssm_c_im"][l] = gcr, gci
        small["ssm_d"][l] = jnp.sum(gd8, axis=0)
        dq, dk, dv, df, dbf = fox_bwd(s["qkv"], dya, s["ya"], s["lser"], s["cumcol"],
                                      s["rest"], bf_pad[l:l + 1], bsz, seq, f_blk)
        small["b_forget"][l] = jnp.sum(dbf, axis=0)[:ATTN_HEADS]
        dproj = jnp.concatenate([dq, dk, dv, dgab, du, df], axis=1)
        g1 = norm_mix[l].reshape(1, d)
        dwp = matmul_tn(s["x0"], dproj, "grad_w_in", a_kind="norm", gain=g1)
        big["w_in"][l] = jnp.stack(jnp.split(jnp.concatenate(
            [dwp[:, :o_f], dwp[:, o_f + 2 * d + sw:o_f + 2 * d + sw + ATTN_HEADS],
             dwp[:, o_f + 2 * d:o_f + 2 * d + sw], dwp[:, o_f:o_f + 2 * d]], axis=1), 4, axis=1))
        dx, dg1 = proj_bwd(dproj, w_pad[l], s["x0"], dx1, g1)
        small["norm_mix"][l] = jnp.sum(dg1, axis=0)
    grad_x = dx.reshape(bsz, seq, d)

    gps = [jnp.stack(big[n], axis=1) for n in SHARDED]
    from_sibling = exchange_sibling_half(gps)
    pair = [add_sibling(g, s1, core, "add_sibling_" + n) for n, g, s1 in zip(SHARDED, gps, from_sibling)]
    slots = scatter_to_owners(pair)
    reduced = [add_chips(s2, p, chip, "add_chips_" + n) for n, s2, p in zip(SHARDED, slots, pair)]
    from_sibling = share_with_sibling(reduced)
    out_g, out_d, out_m, out_v = {}, {}, {}, {}
    for n, g_own, g_sib in zip(SHARDED, reduced, from_sibling):
        out_g[n], out_d[n], out_m[n], out_v[n] = adamw(args[n], g_own, g_sib, args["m_" + n], args["v_" + n],
                                                       core, "adamw_" + n)

    small_g = [jnp.stack(small[n]) if n != "norm_final" else jnp.sum(dgf_rows, axis=0) for n in SMALL]
    small_shapes = [args[n].shape for n in SMALL]
    small_packed = pack_flat(small_g)
    sg, sd, sm, sv = sum_and_adamw(exchange_all(small_packed), small_packed, device,
                                   pack_flat([args[n] for n in SMALL]),
                                   pack_flat([args["m_" + n] for n in SMALL]),
                                   pack_flat([args["v_" + n] for n in SMALL]))
    for res, packed in ((out_g, sg), (out_d, sd), (out_m, sm), (out_v, sv)):
        res.update(zip(SMALL, unpack_flat(packed, small_shapes)))

    order = ("norm_mix", "w_in", "b_forget", "ssm_lambda_re", "ssm_lambda_im", "ssm_log_dt", "ssm_b_re",
             "ssm_b_im", "ssm_c_re", "ssm_c_im", "ssm_d", "w_glu", "b_glu", "w_branch_a", "w_branch_b", "w_out",
             "norm_mlp", "w_mlp_up", "w_mlp_down", "norm_final")
    return (loss, grad_x, *[out_g[n] for n in order], *[out_d[n] for n in order],
            *[out_m[n] for n in order], *[out_v[n] for n in order])
```

```python
import math

import jax
import jax.numpy as jnp
from jax import lax
from jax.experimental import pallas as pl
from jax.experimental.pallas import tpu as pltpu

F32 = jnp.float32
BF16 = jnp.bfloat16
MESH = pl.DeviceIdType.MESH
ANY = pl.BlockSpec(memory_space=pl.ANY)

ATTN_HEADS = 8
HEAD_DIM = 64
ATTN_WIDTH = ATTN_HEADS * HEAD_DIM
HEAD_PAIRS = ATTN_HEADS // 2
SSM_GROUPS = 32
SSM_GROUP_CH = 16
SSM_STATE = 64
SSM_WIDTH = SSM_GROUPS * SSM_GROUP_CH
LANES = 128
SUBLANES = 8
SSM_CHUNKS = SSM_WIDTH // LANES
CHUNK_GROUPS = SSM_GROUPS // SSM_CHUNKS
CHUNK_STATES = CHUNK_GROUPS * SSM_STATE
CHUNK_LANES = 2 * CHUNK_STATES
STATE_LANES = SSM_CHUNKS * CHUNK_LANES
F_PAD = LANES
RMS_EPS = 1e-6
ADAM_LR = 0.001
ADAM_B1 = 0.9
ADAM_B2 = 0.999
ADAM_EPS = 1e-08
ADAM_WD = 0.01
ADAM_STEP = 10
PACK_COLS = 1024
PACK_ROW_TILE = 256
VMEM_LIMIT = 52 * 1024 * 1024
NEG_BIG = -1e30
GELU_C = math.sqrt(2.0 / math.pi)
GELU_A = 0.044715

NN = (((1,), (0,)), ((), ()))
NT = (((1,), (1,)), ((), ()))
TN = (((0,), (0,)), ((), ()))


def _pick(n, pref):
    if n <= pref:
        return n
    best = LANES
    for t in range(LANES, pref + 1, LANES):
        if n % t == 0:
            best = t
    assert n % best == 0, (n, pref)
    return best


def _rows(n, pref):
    t = min(n, pref)
    while n % t:
        t //= 2
    assert t % 16 == 0 or t == n, (n, pref)
    return t


def _params(sem):
    return pltpu.CompilerParams(dimension_semantics=sem, vmem_limit_bytes=VMEM_LIMIT)


def _fold8(v):
    r, c = v.shape
    return jnp.sum(v.reshape(r // SUBLANES, SUBLANES, c), axis=0)


def _dot(a, b, dims=None):
    if dims is None:
        return jnp.dot(a, b, preferred_element_type=F32)
    return lax.dot_general(a, b, dims, preferred_element_type=F32)


def _dot_exact(a, b, dims):
    return lax.dot_general(a, b, dims, preferred_element_type=F32, precision=lax.Precision.HIGHEST)


def _sigmoid(v):
    return 1.0 / (1.0 + jnp.exp(-v))


def _rms_scale(x):
    return lax.rsqrt(jnp.mean(x * x, axis=-1, keepdims=True) + RMS_EPS)


def _rms_bwd(x, g, dh):
    r = _rms_scale(x)
    xn = x * r
    dxn = dh * g
    dx = r * (dxn - xn * jnp.mean(dxn * xn, axis=-1, keepdims=True))
    return dx, dh * xn


def norm_matmul(x, g, w, out_dtype, name):
    t, d = x.shape
    m = w.shape[1]
    tm, tn = _rows(t, 512), _pick(m, 1024)

    def body(x_ref, g_ref, w_ref, o_ref):
        xv = x_ref[...]
        h = (xv * _rms_scale(xv) * g_ref[...]).astype(BF16)
        o_ref[...] = _dot(h, w_ref[...]).astype(o_ref.dtype)

    return pl.pallas_call(
        body, name=name, grid=(t // tm, m // tn),
        in_specs=[pl.BlockSpec((tm, d), lambda i, j: (i, 0)),
                  pl.BlockSpec((1, d), lambda i, j: (0, 0)),
                  pl.BlockSpec((d, tn), lambda i, j: (0, j))],
        out_specs=pl.BlockSpec((tm, tn), lambda i, j: (i, j)),
        out_shape=jax.ShapeDtypeStruct((t, m), out_dtype),
        compiler_params=_params(("parallel", "arbitrary")),
    )(x, g, w)


def matmul_tn(a, b, name, a_kind="plain", gain=None, shard_axis=None, out_dtype=F32, tm_pref=1024, tn_pref=1536,
              tk_pref=512):
    t, ma = a.shape
    nb = b.shape[1]
    tm = ma if a_kind == "norm" else _pick(ma, tm_pref)
    tn = _pick(nb, tn_pref)
    tk = _rows(t, tk_pref)
    nk = t // tk
    if shard_axis == 0:
        per = tm * 4 // ma
        assert per >= 1 and (ma // 4) * per == tm, (ma, tm)
        out_shape, out_spec = (4, ma // 4, nb), pl.BlockSpec((per, ma // 4, tn), lambda i, j, k: (i, 0, j))
    elif shard_axis == 1:
        per = tn * 4 // nb
        assert per >= 1 and (nb // 4) * per == tn, (nb, tn)
        out_shape, out_spec = (4, ma, nb // 4), pl.BlockSpec((per, tm, nb // 4), lambda i, j, k: (j, i, 0))
    else:
        out_shape, out_spec = (ma, nb), pl.BlockSpec((tm, tn), lambda i, j, k: (i, j))

    def body(*refs):
        if a_kind == "norm":
            a_ref, g_ref, b_ref, o_ref, acc = refs
        else:
            a_ref, b_ref, o_ref, acc = refs
        k = pl.program_id(2)

        @pl.when(k == 0)
        def _():
            acc[...] = jnp.zeros_like(acc)

        av = a_ref[...]
        if a_kind == "norm":
            av = av * _rms_scale(av) * g_ref[...]
        elif a_kind == "relu2":
            av = jnp.square(jnp.maximum(av.astype(F32), 0.0))
        acc[...] += _dot(av.astype(BF16), b_ref[...].astype(BF16), TN)

        @pl.when(k == nk - 1)
        def _():
            if shard_axis == 0:
                o_ref[...] = acc[...].reshape(o_ref.shape).astype(o_ref.dtype)
            elif shard_axis == 1:
                cs = nb // 4
                for n in range(o_ref.shape[0]):
                    o_ref[n] = acc[:, n * cs:(n + 1) * cs].astype(o_ref.dtype)
            else:
                o_ref[...] = acc[...].astype(o_ref.dtype)

    in_specs = [pl.BlockSpec((tk, tm), lambda i, j, k: (k, i))]
    args = [a]
    if a_kind == "norm":
        in_specs.append(pl.BlockSpec((1, ma), lambda i, j, k: (0, 0)))
        args.append(gain)
    in_specs.append(pl.BlockSpec((tk, tn), lambda i, j, k: (k, j)))
    args.append(b)
    return pl.pallas_call(
        body, name=name, grid=(ma // tm, nb // tn, nk),
        in_specs=in_specs,
        out_specs=out_spec,
        out_shape=jax.ShapeDtypeStruct(out_shape, out_dtype),
        scratch_shapes=[pltpu.VMEM((tm, tn), F32)],
        compiler_params=_params(("parallel", "parallel", "arbitrary")),
    )(*args)


def _tri(n, upper):
    r = lax.broadcasted_iota(jnp.int32, (n, n), 0)
    c = lax.broadcasted_iota(jnp.int32, (n, n), 1)
    return jnp.where((c >= r) if upper else (c <= r), 1.0, 0.0).astype(F32)


def _head_rows():
    r = lax.broadcasted_iota(jnp.int32, (SUBLANES, LANES), 0)
    c = lax.broadcasted_iota(jnp.int32, (SUBLANES, LANES), 1)
    return jnp.where(r == c, 1.0, 0.0).astype(F32)


def forget_cumsum(rest, bf, bsz, seq, f_blk):
    tc = _rows(seq, 512)
    nc = seq // tc

    def body(f_ref, b_ref, col_ref, row_ref, carry):
        c = pl.program_id(1)

        @pl.when(c == 0)
        def _():
            carry[...] = jnp.zeros_like(carry)

        z = f_ref[...] + b_ref[...]
        logf = jnp.minimum(z, 0.0) - jnp.log(1.0 + jnp.exp(-jnp.abs(z)))
        cum = _dot_exact(_tri(tc, False), logf, NN) + carry[0:1, :]
        col_ref[0] = cum
        row_ref[0] = _dot_exact(_head_rows(), cum, NT)
        carry[...] = jnp.broadcast_to(cum[tc - 1:tc, :], carry.shape)

    return pl.pallas_call(
        body, name="forget_cumsum", grid=(bsz, nc),
        in_specs=[pl.BlockSpec((tc, F_PAD), lambda b, c: (b * nc + c, f_blk)),
                  pl.BlockSpec((1, F_PAD), lambda b, c: (0, 0))],
        out_specs=[pl.BlockSpec((1, tc, LANES), lambda b, c: (b, c, 0)),
                   pl.BlockSpec((1, SUBLANES, tc), lambda b, c: (b, 0, c))],
        out_shape=[jax.ShapeDtypeStruct((bsz, seq, LANES), F32),
                   jax.ShapeDtypeStruct((bsz, SUBLANES, seq), F32)],
        scratch_shapes=[pltpu.VMEM((SUBLANES, LANES), F32)],
        compiler_params=_params(("parallel", "arbitrary")),
    )(rest, bf)


def forget_bwd(dcq, dcp, rest, bf, bsz, seq, f_blk):
    tc = _attn_tile(seq)
    nc = seq // tc

    def body(dq_ref, dc_ref, f_ref, b_ref, df_ref, db_ref, carry):
        b = pl.program_id(0)
        c = pl.program_id(1)

        @pl.when(c == 0)
        def _():
            carry[...] = jnp.zeros_like(carry)

        @pl.when((b == 0) & (c == 0))
        def _():
            db_ref[...] = jnp.zeros_like(db_ref)

        row = lax.broadcasted_iota(jnp.int32, (SUBLANES, tc), 0)
        heads = jnp.zeros((SUBLANES, tc), F32)
        dc = jnp.zeros((tc, LANES), F32)
        for p in range(HEAD_PAIRS):
            blk = dq_ref[0, p, 0]
            heads = heads + jnp.where(row == 2 * p, blk[0:1], 0.0) + jnp.where(row == 2 * p + 1, blk[1:2], 0.0)
            dc = dc + dc_ref[0, p]
        dc = dc + jnp.concatenate([heads, jnp.zeros((LANES - SUBLANES, tc), F32)], axis=0).T
        dlogf = _dot_exact(_tri(tc, True), dc, NN) + carry[0:1, :]
        carry[...] = jnp.broadcast_to(dlogf[0:1, :], carry.shape)
        z = f_ref[...] + b_ref[...]
        lane = lax.broadcasted_iota(jnp.int32, z.shape, 1)
        df = jnp.where(lane < ATTN_HEADS, dlogf * _sigmoid(-z), 0.0)
        df_ref[...] = df.astype(df_ref.dtype)
        db_ref[...] += _fold8(df)

    return pl.pallas_call(
        body, name="forget_bwd", grid=(bsz, nc),
        in_specs=[pl.BlockSpec((1, HEAD_PAIRS, 1, SUBLANES, tc), lambda b, c: (b, 0, nc - 1 - c, 0, 0)),
                  pl.BlockSpec((1, HEAD_PAIRS, tc, LANES), lambda b, c: (b, 0, nc - 1 - c, 0)),
                  pl.BlockSpec((tc, F_PAD), lambda b, c: (b * nc + nc - 1 - c, f_blk)),
                  pl.BlockSpec((1, F_PAD), lambda b, c: (0, 0))],
        out_specs=[pl.BlockSpec((tc, F_PAD), lambda b, c: (b * nc + nc - 1 - c, 0)),
                   pl.BlockSpec((SUBLANES, F_PAD), lambda b, c: (0, 0))],
        out_shape=[jax.ShapeDtypeStruct((bsz * seq, F_PAD), BF16),
                   jax.ShapeDtypeStruct((SUBLANES, F_PAD), F32)],
        scratch_shapes=[pltpu.VMEM((SUBLANES, LANES), F32)],
        compiler_params=_params(("arbitrary", "arbitrary")),
    )(dcq, dcp, rest, bf)


def _attn_tile(seq):
    return 512 if seq >= 2048 else 128


def _lane_head(shape, par):
    lane = lax.broadcasted_iota(jnp.int32, shape, len(shape) - 1)
    return (lane >= HEAD_DIM) if par else (lane < HEAD_DIM)


def _pick_lane(block, idx):
    lane = lax.broadcasted_iota(jnp.int32, block.shape, 1)
    return jnp.sum(jnp.where(lane == idx, block, 0.0), axis=1, keepdims=True)


def _pair_rows(lo_lane, hi_lane):
    r = lax.broadcasted_iota(jnp.int32, (SUBLANES, LANES), 0)
    c = lax.broadcasted_iota(jnp.int32, (SUBLANES, LANES), 1)
    if lo_lane is None:
        sel = ((r == 0) & (c < HEAD_DIM)) | ((r == 1) & (c >= HEAD_DIM))
    else:
        sel = ((r == 0) & (c == lo_lane)) | ((r == 1) & (c == hi_lane))
    return jnp.where(sel, 1.0, 0.0).astype(F32)


def _causal(s, transposed):
    r = lax.broadcasted_iota(jnp.int32, s.shape, 0)
    c = lax.broadcasted_iota(jnp.int32, s.shape, 1)
    return jnp.where((c >= r) if transposed else (r >= c), s, NEG_BIG)


def _causal_pairs(n, key_major):
    if key_major:
        pairs = [(i, j) for j in range(n) for i in range(j, n)]
    else:
        pairs = [(i, j) for i in range(n) for j in range(i + 1)]
    return (jnp.asarray([p[0] for p in pairs], jnp.int32), jnp.asarray([p[1] for p in pairs], jnp.int32))


def fox_fwd(qkv, cumrow, bsz, seq, carries=()):
    tq = _attn_tile(seq)
    nq = seq // tq
    scale = HEAD_DIM ** -0.5
    kb, vb = ATTN_WIDTH // LANES, 2 * ATTN_WIDTH // LANES
    qi_tab, kj_tab = _causal_pairs(nq, False)
    npairs = int(qi_tab.shape[0])
    sends, landings, n_sems, build = _carry_plan(carries)
    nc = len(sends)

    def body(qi_ref, kj_ref, q_ref, k_ref, v_ref, cr_ref, *rest):
        ins, (o_ref, lr_ref), lands = rest[:nc], rest[nc:nc + 2], rest[nc + 2:2 * nc + 2]
        (m_s, acc_s), sems = rest[2 * nc + 2:2 * nc + 4], rest[2 * nc + 4:]
        hp = pl.program_id(1)
        qi = qi_ref[pl.program_id(2)]
        kj = kj_ref[pl.program_id(2)]
        if nc:
            @pl.when((pl.program_id(0) == 0) & (hp == 0) & (pl.program_id(2) == 0))
            def _():
                for cp in build(ins, lands, sems):
                    cp.start()

        @pl.when(kj == 0)
        def _():
            m_s[...] = jnp.full_like(m_s, NEG_BIG)
            acc_s[...] = jnp.zeros_like(acc_s)

        def step(diag):
            q = q_ref[...]
            k = k_ref[...]
            v = v_ref[...]
            for par in range(2):
                sel = _lane_head(q.shape, par)
                qh = jnp.where(sel, q, 0.0) * scale
                s = _dot(qh.astype(BF16), k, NT) - cr_ref[0, pl.ds(2 * hp + par, 1), :]
                if diag:
                    s = _causal(s, False)
                m_prev = m_s[par]
                m_new = jnp.maximum(m_prev, jnp.max(s, axis=1, keepdims=True))
                p = jnp.exp(s - m_new).astype(BF16)
                acc_s[par] = jnp.exp(m_prev - m_new) * acc_s[par] + _dot(p, jnp.where(sel, v, 1.0).astype(BF16))
                m_s[par] = m_new

        @pl.when(kj < qi)
        def _():
            step(False)

        @pl.when(kj == qi)
        def _():
            step(True)
            lo = _lane_head((tq, LANES), 0)
            sums = [pltpu.roll(acc_s[par], HEAD_DIM, 1) for par in range(2)]
            out = jnp.where(lo, acc_s[0] / sums[0], acc_s[1] / sums[1])
            o_ref[...] = out.astype(o_ref.dtype)
            lse = jnp.where(lo, m_s[0] + jnp.log(sums[0]), m_s[1] + jnp.log(sums[1]))
            lr_ref[0, 0] = _dot_exact(_pair_rows(0, HEAD_DIM), lse, NT)

        if nc:
            @pl.when((pl.program_id(0) == bsz - 1) & (hp == HEAD_PAIRS - 1) & (pl.program_id(2) == npairs - 1))
            def _():
                for cp in build(ins, lands, sems):
                    cp.wait()

    sem_shapes = [pltpu.SemaphoreType.DMA((n_sems,)), pltpu.SemaphoreType.DMA((n_sems,))] if nc else []
    outs = pl.pallas_call(
        body, name="fox_fwd",
        grid_spec=pltpu.PrefetchScalarGridSpec(
            num_scalar_prefetch=2, grid=(bsz, HEAD_PAIRS, npairs),
            in_specs=[pl.BlockSpec((tq, LANES), lambda b, h, t, qi, kj: (b * nq + qi[t], h)),
                      pl.BlockSpec((tq, LANES), lambda b, h, t, qi, kj: (b * nq + kj[t], kb + h)),
                      pl.BlockSpec((tq, LANES), lambda b, h, t, qi, kj: (b * nq + kj[t], vb + h)),
                      pl.BlockSpec((1, SUBLANES, tq), lambda b, h, t, qi, kj: (b, 0, kj[t]))] + [ANY] * nc,
            out_specs=[pl.BlockSpec((tq, LANES), lambda b, h, t, qi, kj: (b * nq + qi[t], h)),
                       pl.BlockSpec((1, 1, SUBLANES, tq), lambda b, h, t, qi, kj: (b, h, 0, qi[t]))] + [ANY] * nc,
            scratch_shapes=[pltpu.VMEM((2, tq, 1), F32), pltpu.VMEM((2, tq, LANES), F32)] + sem_shapes),
        out_shape=[jax.ShapeDtypeStruct((bsz * seq, ATTN_WIDTH), BF16),
                   jax.ShapeDtypeStruct((bsz, HEAD_PAIRS, SUBLANES, seq), F32)] + landings,
        compiler_params=_params(("arbitrary", "arbitrary", "arbitrary")),
    )(qi_tab, kj_tab, qkv, qkv, qkv, cumrow, *sends)
    return outs[0], outs[1], list(outs[2:])


def fox_bwd_kernel(qkv, do, o, lser, cumcol, bsz, seq, carries=()):
    tk = _attn_tile(seq)
    nk = seq // tk
    scale = HEAD_DIM ** -0.5
    kb, vb = ATTN_WIDTH // LANES, 2 * ATTN_WIDTH // LANES
    qi_tab, kj_tab = _causal_pairs(nk, True)
    npairs = int(qi_tab.shape[0])
    sends, landings, n_sems, build = _carry_plan(carries)
    nc = len(sends)

    def body(qi_ref, kj_ref, q_ref, k_ref, v_ref, do_ref, o_ref, lr_ref, cc_ref, *rest):
        ins, lands = rest[:nc], rest[nc + 5:2 * nc + 5]
        dq_ref, dk_ref, dv_ref, dc_ref, dr_ref = rest[nc:nc + 5]
        (dq_s, dk_s, dv_s, dc_s, dr_s), sems = rest[2 * nc + 5:2 * nc + 10], rest[2 * nc + 10:]
        hp = pl.program_id(1)
        t = pl.program_id(2)
        qi = qi_ref[t]
        kj = kj_ref[t]
        if nc:
            @pl.when((pl.program_id(0) == 0) & (hp == 0) & (t == 0))
            def _():
                for cp in build(ins, lands, sems):
                    cp.start()

        @pl.when(t == 0)
        def _():
            dq_s[...] = jnp.zeros_like(dq_s)
            dr_s[...] = jnp.zeros_like(dr_s)

        @pl.when(qi == kj)
        def _():
            dk_s[...] = jnp.zeros_like(dk_s)
            dv_s[...] = jnp.zeros_like(dv_s)
            dc_s[...] = jnp.zeros_like(dc_s)

        def step(diag):
            q = q_ref[...]
            k = k_ref[...]
            v = v_ref[...]
            dov = do_ref[...]
            prod = dov.astype(F32) * o_ref[...].astype(F32)
            drow = _dot_exact(_pair_rows(None, None), prod, NT)
            lrow = lr_ref[0, 0]
            lane = lax.broadcasted_iota(jnp.int32, (tk, LANES), 1)
            for par in range(2):
                head = 2 * hp + par
                sel = _lane_head(k.shape, par)
                kh = (jnp.where(sel, k, 0.0) * scale).astype(BF16)
                st = _dot(kh, q, NT) - _pick_lane(cc_ref[0], head)
                if diag:
                    st = _causal(st, True)
                pt = jnp.exp(st - lrow[par:par + 1, :])
                vh = jnp.where(sel, v, 0.0)
                dpt = _dot(vh.astype(BF16), dov, NT)
                dst = pt * (dpt - drow[par:par + 1, :])
                dsb = dst.astype(BF16)
                dv_s[...] += jnp.where(sel, _dot(pt.astype(BF16), dov), 0.0)
                dk_s[...] += jnp.where(sel, _dot(dsb, q), 0.0)
                dq_s[qi] += _dot(dsb, kh, TN)
                dc_s[...] += jnp.where(lane == head, -jnp.sum(dst, axis=1, keepdims=True), 0.0)
                dr_s[qi, par:par + 1, :] += jnp.sum(dst, axis=0, keepdims=True)

        @pl.when(qi > kj)
        def _():
            step(False)

        @pl.when(qi == kj)
        def _():
            step(True)

        @pl.when(qi == nk - 1)
        def _():
            dk_ref[...] = (dk_s[...] * scale).astype(dk_ref.dtype)
            dv_ref[...] = dv_s[...].astype(dv_ref.dtype)
            dc_ref[0, 0] = dc_s[...]

        @pl.when(t == npairs - 1)
        def _():
            for i in range(nk):
                dq_ref[i * tk:(i + 1) * tk, :] = dq_s[i].astype(dq_ref.dtype)
            dr_ref[0, 0] = dr_s[...]

        if nc:
            @pl.when((pl.program_id(0) == bsz - 1) & (hp == HEAD_PAIRS - 1) & (t == npairs - 1))
            def _():
                for cp in build(ins, lands, sems):
                    cp.wait()

    sem_shapes = [pltpu.SemaphoreType.DMA((n_sems,)), pltpu.SemaphoreType.DMA((n_sems,))] if nc else []
    outs = pl.pallas_call(
        body, name="fox_bwd",
        grid_spec=pltpu.PrefetchScalarGridSpec(
            num_scalar_prefetch=2, grid=(bsz, HEAD_PAIRS, npairs),
            in_specs=[pl.BlockSpec((tk, LANES), lambda b, h, t, qi, kj: (b * nk + qi[t], h)),
                      pl.BlockSpec((tk, LANES), lambda b, h, t, qi, kj: (b * nk + kj[t], kb + h)),
                      pl.BlockSpec((tk, LANES), lambda b, h, t, qi, kj: (b * nk + kj[t], vb + h)),
                      pl.BlockSpec((tk, LANES), lambda b, h, t, qi, kj: (b * nk + qi[t], h)),
                      pl.BlockSpec((tk, LANES), lambda b, h, t, qi, kj: (b * nk + qi[t], h)),
                      pl.BlockSpec((1, 1, SUBLANES, tk), lambda b, h, t, qi, kj: (b, h, 0, qi[t])),
                      pl.BlockSpec((1, tk, LANES), lambda b, h, t, qi, kj: (b, kj[t], 0))] + [ANY] * nc,
            out_specs=[pl.BlockSpec((seq, LANES), lambda b, h, t, qi, kj: (b, h)),
                       pl.BlockSpec((tk, LANES), lambda b, h, t, qi, kj: (b * nk + kj[t], h)),
                       pl.BlockSpec((tk, LANES), lambda b, h, t, qi, kj: (b * nk + kj[t], h)),
                       pl.BlockSpec((1, 1, tk, LANES), lambda b, h, t, qi, kj: (b, h, kj[t], 0)),
                       pl.BlockSpec((1, 1, nk, SUBLANES, tk), lambda b, h, t, qi, kj: (b, h, 0, 0, 0))] + [ANY] * nc,
            scratch_shapes=[pltpu.VMEM((nk, tk, LANES), F32), pltpu.VMEM((tk, LANES), F32),
                            pltpu.VMEM((tk, LANES), F32), pltpu.VMEM((tk, LANES), F32),
                            pltpu.VMEM((nk, SUBLANES, tk), F32)] + sem_shapes),
        out_shape=[jax.ShapeDtypeStruct((bsz * seq, ATTN_WIDTH), BF16),
                   jax.ShapeDtypeStruct((bsz * seq, ATTN_WIDTH), BF16),
                   jax.ShapeDtypeStruct((bsz * seq, ATTN_WIDTH), BF16),
                   jax.ShapeDtypeStruct((bsz, HEAD_PAIRS, seq, LANES), F32),
                   jax.ShapeDtypeStruct((bsz, HEAD_PAIRS, nk, SUBLANES, tk), F32)] + landings,
        compiler_params=_params(("arbitrary", "arbitrary", "arbitrary")),
    )(qi_tab, kj_tab, qkv, qkv, qkv, do, o, lser, cumcol, *sends)
    return outs[:5], list(outs[5:])


def fox_bwd(qkv, do, o, lser, cumcol, rest, bf, bsz, seq, f_blk, carries=()):
    (dq, dk, dv, dcp, dcq), lands = fox_bwd_kernel(qkv, do, o, lser, cumcol, bsz, seq, carries)
    df, dbf = forget_bwd(dcq, dcp, rest, bf, bsz, seq, f_blk)
    return dq, dk, dv, df, dbf, lands


SCAN_STEPS = (1, 2, 4)
TAB_FWD = 0
TAB_BWD = 32
TAB_CARRY = 24
TAB_ROWS = 64


def _ssm_tile(seq):
    return 256 if seq >= 1024 else 64


def _scan_block(xr, xi, tab_ref, re, im, cr, ci, reverse):
    base = TAB_BWD if reverse else TAB_FWD
    for n, s in enumerate(SCAN_STEPS):
        ar = tab_ref[base + n * SUBLANES:base + (n + 1) * SUBLANES, re]
        ai = tab_ref[base + n * SUBLANES:base + (n + 1) * SUBLANES, im]
        shift = SUBLANES - s if reverse else s
        sr = pltpu.roll(xr, shift, 0)
        si = pltpu.roll(xi, shift, 0)
        xr, xi = xr + ar * sr - ai * si, xi + ar * si + ai * sr
    pr = tab_ref[base + TAB_CARRY:base + TAB_CARRY + SUBLANES, re]
    pi = tab_ref[base + TAB_CARRY:base + TAB_CARRY + SUBLANES, im]
    xr, xi = xr + pr * cr - pi * ci, xi + pr * ci + pi * cr
    return xr, xi


def ssm_fwd(rest, wb4, wc4, tabs, dskip, bsz, seq, u_blk):
    tt = _ssm_tile(seq)
    nt = seq // tt

    def body(u_ref, wb_ref, wc_ref, tab_ref, d_ref, y_ref, h_ref, carry):
        c = pl.program_id(1)

        @pl.when(c == 0)
        def _():
            carry[...] = jnp.zeros_like(carry)

        u = u_ref[...]
        ub = u.astype(BF16)
        for j in range(SSM_CHUNKS):
            h_ref[:, j * CHUNK_LANES:(j + 1) * CHUNK_LANES] = _dot(ub[:, j * LANES:(j + 1) * LANES], wb_ref[j])
        for j in range(SSM_CHUNKS):
            re = slice(j * CHUNK_LANES, j * CHUNK_LANES + CHUNK_STATES)
            im = slice(j * CHUNK_LANES + CHUNK_STATES, (j + 1) * CHUNK_LANES)

            def blk(bi, car):
                r0 = pl.multiple_of(bi * SUBLANES, SUBLANES)
                xr, xi = _scan_block(h_ref[pl.ds(r0, SUBLANES), re], h_ref[pl.ds(r0, SUBLANES), im],
                                     tab_ref, re, im, car[0], car[1], False)
                h_ref[pl.ds(r0, SUBLANES), re] = xr
                h_ref[pl.ds(r0, SUBLANES), im] = xi
                return xr[SUBLANES - 1:SUBLANES], xi[SUBLANES - 1:SUBLANES]

            cr, ci = lax.fori_loop(0, tt // SUBLANES, blk, (carry[0:1, re], carry[0:1, im]), unroll=2)
            carry[0:1, re] = cr
            carry[0:1, im] = ci
        for j in range(SSM_CHUNKS):
            hj = h_ref[:, j * CHUNK_LANES:(j + 1) * CHUNK_LANES].astype(BF16)
            cols = slice(j * LANES, (j + 1) * LANES)
            y_ref[:, cols] = _dot(hj, wc_ref[j]) + d_ref[:, cols] * u[:, cols]

    return pl.pallas_call(
        body, name="ssm_fwd", grid=(bsz, nt),
        in_specs=[pl.BlockSpec((tt, SSM_WIDTH), lambda b, c: (b * nt + c, u_blk)),
                  pl.BlockSpec((SSM_CHUNKS, LANES, CHUNK_LANES), lambda b, c: (0, 0, 0)),
                  pl.BlockSpec((SSM_CHUNKS, CHUNK_LANES, LANES), lambda b, c: (0, 0, 0)),
                  pl.BlockSpec((TAB_ROWS, STATE_LANES), lambda b, c: (0, 0)),
                  pl.BlockSpec((1, SSM_WIDTH), lambda b, c: (0, 0))],
        out_specs=[pl.BlockSpec((tt, SSM_WIDTH), lambda b, c: (b * nt + c, 0)),
                   pl.BlockSpec((tt, STATE_LANES), lambda b, c: (b * nt + c, 0))],
        out_shape=[jax.ShapeDtypeStruct((bsz * seq, SSM_WIDTH), F32),
                   jax.ShapeDtypeStruct((bsz * seq, STATE_LANES), F32)],
        scratch_shapes=[pltpu.VMEM((SUBLANES, STATE_LANES), F32)],
        compiler_params=_params(("parallel", "arbitrary")),
    )(rest, wb4, wc4, tabs, dskip)


def ssm_bwd(dys, rest, hs, wb4, wc4, tabs, dskip, bsz, seq, u_blk):
    tt = _ssm_tile(seq)
    nt = seq // tt
    nb = tt // SUBLANES

    def body(dy_ref, u_ref, h_ref, hp_ref, wb_ref, wc_ref, tab_ref, d_ref,
             du_ref, ga_ref, gwb_ref, gwc_ref, gd_ref, g_s, carry):
        b = pl.program_id(0)
        c = pl.program_id(1)

        @pl.when(c == 0)
        def _():
            carry[...] = jnp.zeros_like(carry)

        @pl.when((b == 0) & (c == 0))
        def _():
            ga_ref[...] = jnp.zeros_like(ga_ref)
            gwb_ref[...] = jnp.zeros_like(gwb_ref)
            gwc_ref[...] = jnp.zeros_like(gwc_ref)
            gd_ref[...] = jnp.zeros_like(gd_ref)

        dy = dy_ref[...].astype(F32)
        dyb = dy.astype(BF16)
        u = u_ref[...]
        ub = u.astype(BF16)
        first_chunk = c == nt - 1
        for j in range(SSM_CHUNKS):
            g_s[:, j * CHUNK_LANES:(j + 1) * CHUNK_LANES] = _dot(dyb[:, j * LANES:(j + 1) * LANES], wc_ref[j], NT)
        for j in range(SSM_CHUNKS):
            re = slice(j * CHUNK_LANES, j * CHUNK_LANES + CHUNK_STATES)
            im = slice(j * CHUNK_LANES + CHUNK_STATES, (j + 1) * CHUNK_LANES)
            row = lax.broadcasted_iota(jnp.int32, (SUBLANES, CHUNK_STATES), 0)

            def blk(n, car):
                bi = nb - 1 - n
                r0 = pl.multiple_of(bi * SUBLANES, SUBLANES)
                gr, gi = _scan_block(g_s[pl.ds(r0, SUBLANES), re], g_s[pl.ds(r0, SUBLANES), im],
                                     tab_ref, re, im, car[0], car[1], True)
                g_s[pl.ds(r0, SUBLANES), re] = gr
                g_s[pl.ds(r0, SUBLANES), im] = gi
                rp = pl.multiple_of(jnp.maximum(bi - 1, 0) * SUBLANES, SUBLANES)
                inside = bi > 0
                live = jnp.where(jnp.logical_or(inside, jnp.logical_not(first_chunk)), 1.0, 0.0)
                pr = jnp.where(inside, h_ref[pl.ds(rp, SUBLANES), re], hp_ref[:, re])[SUBLANES - 1:SUBLANES] * live
                pi = jnp.where(inside, h_ref[pl.ds(rp, SUBLANES), im], hp_ref[:, im])[SUBLANES - 1:SUBLANES] * live
                hr = jnp.where(row >= 1, pltpu.roll(h_ref[pl.ds(r0, SUBLANES), re], 1, 0), pr)
                hi = jnp.where(row >= 1, pltpu.roll(h_ref[pl.ds(r0, SUBLANES), im], 1, 0), pi)
                return (gr[0:1], gi[0:1], car[2] + gr * hr + gi * hi, car[3] + gi * hr - gr * hi)

            zero = jnp.zeros((SUBLANES, CHUNK_STATES), F32)
            cr, ci, sr, si = lax.fori_loop(0, nb, blk, (carry[0:1, re], carry[0:1, im], zero, zero), unroll=2)
            carry[0:1, re] = cr
            carry[0:1, im] = ci
            ga_ref[:, re] += sr
            ga_ref[:, im] += si
        for j in range(SSM_CHUNKS):
            cols = slice(j * LANES, (j + 1) * LANES)
            lanes = slice(j * CHUNK_LANES, (j + 1) * CHUNK_LANES)
            gj = g_s[:, lanes].astype(BF16)
            du_ref[:, cols] = (_dot(gj, wb_ref[j], NT) + d_ref[:, cols] * dy[:, cols]).astype(du_ref.dtype)
            gwb_ref[j] += _dot(ub[:, cols], gj, TN)
            gwc_ref[j] += _dot(h_ref[:, lanes].astype(BF16), dyb[:, cols], TN)
        gd_ref[...] += _fold8(dy * u)

    def prev_rows(b, c):
        chunk = nt - 1 - c
        return (jnp.maximum((b * nt + chunk) * nb - 1, 0), 0)

    return pl.pallas_call(
        body, name="ssm_bwd", grid=(bsz, nt),
        in_specs=[pl.BlockSpec((tt, SSM_WIDTH), lambda b, c: (b * nt + nt - 1 - c, 0)),
                  pl.BlockSpec((tt, SSM_WIDTH), lambda b, c: (b * nt + nt - 1 - c, u_blk)),
                  pl.BlockSpec((tt, STATE_LANES), lambda b, c: (b * nt + nt - 1 - c, 0)),
                  pl.BlockSpec((SUBLANES, STATE_LANES), prev_rows),
                  pl.BlockSpec((SSM_CHUNKS, LANES, CHUNK_LANES), lambda b, c: (0, 0, 0)),
                  pl.BlockSpec((SSM_CHUNKS, CHUNK_LANES, LANES), lambda b, c: (0, 0, 0)),
                  pl.BlockSpec((TAB_ROWS, STATE_LANES), lambda b, c: (0, 0)),
                  pl.BlockSpec((1, SSM_WIDTH), lambda b, c: (0, 0))],
        out_specs=[pl.BlockSpec((tt, SSM_WIDTH), lambda b, c: (b * nt + nt - 1 - c, 0)),
                   pl.BlockSpec((SUBLANES, STATE_LANES), lambda b, c: (0, 0)),
                   pl.BlockSpec((SSM_CHUNKS, LANES, CHUNK_LANES), lambda b, c: (0, 0, 0)),
                   pl.BlockSpec((SSM_CHUNKS, CHUNK_LANES, LANES), lambda b, c: (0, 0, 0)),
                   pl.BlockSpec((SUBLANES, SSM_WIDTH), lambda b, c: (0, 0))],
        out_shape=[jax.ShapeDtypeStruct((bsz * seq, SSM_WIDTH), BF16),
                   jax.ShapeDtypeStruct((SUBLANES, STATE_LANES), F32),
                   jax.ShapeDtypeStruct((SSM_CHUNKS, LANES, CHUNK_LANES), F32),
                   jax.ShapeDtypeStruct((SSM_CHUNKS, CHUNK_LANES, LANES), F32),
                   jax.ShapeDtypeStruct((SUBLANES, SSM_WIDTH), F32)],
        scratch_shapes=[pltpu.VMEM((tt, STATE_LANES), F32), pltpu.VMEM((SUBLANES, STATE_LANES), F32)],
        compiler_params=_params(("arbitrary", "arbitrary")),
    )(dys, rest, hs, hs, wb4, wc4, tabs, dskip)


def _gelu(v):
    t = jnp.tanh(GELU_C * (v + GELU_A * v * v * v))
    return 0.5 * v * (1.0 + t), t


def mix_fwd(ya, ys, rest, x0, wglu, bglu, wba, wbb, wout):
    t, d = x0.shape
    tm = _rows(t, 256)

    def body(ya_ref, ys_ref, ga_ref, gb_ref, x_ref, wg_ref, bg_ref, wa_ref, wb_ref, wo_ref,
             x1_ref, z_ref, pa_ref, pb_ref, yb_ref, yb2_ref, mx_ref):
        yb, _ = _gelu(ys_ref[...])
        ybb = yb.astype(BF16)
        z = _dot(ybb, wg_ref[...]) + bg_ref[...]
        yb2 = (yb * _sigmoid(z)).astype(BF16)
        pa = _dot(ya_ref[...], wa_ref[...])
        pb = _dot(yb2, wb_ref[...])
        mixed = (_sigmoid(ga_ref[...]) * pa + _sigmoid(gb_ref[...]) * pb).astype(BF16)
        x1_ref[...] = x_ref[...] + _dot(mixed, wo_ref[...])
        z_ref[...] = z.astype(z_ref.dtype)
        pa_ref[...] = pa.astype(pa_ref.dtype)
        pb_ref[...] = pb.astype(pb_ref.dtype)
        yb_ref[...] = ybb
        yb2_ref[...] = yb2
        mx_ref[...] = mixed

    row = lambda w: pl.BlockSpec((tm, w), lambda i: (i, 0))
    full = lambda a: pl.BlockSpec(a.shape, lambda i: (0,) * a.ndim)
    return pl.pallas_call(
        body, name="mix_fwd", grid=(t // tm,),
        in_specs=[row(ATTN_WIDTH), row(SSM_WIDTH),
                  pl.BlockSpec((tm, d), lambda i: (i, 0)), pl.BlockSpec((tm, d), lambda i: (i, 1)),
                  row(d), full(wglu), full(bglu), full(wba), full(wbb), full(wout)],
        out_specs=[row(d), row(SSM_WIDTH), row(d), row(d), row(SSM_WIDTH), row(SSM_WIDTH), row(d)],
        out_shape=[jax.ShapeDtypeStruct((t, d), F32), jax.ShapeDtypeStruct((t, SSM_WIDTH), BF16),
                   jax.ShapeDtypeStruct((t, d), BF16), jax.ShapeDtypeStruct((t, d), BF16),
                   jax.ShapeDtypeStruct((t, SSM_WIDTH), BF16), jax.ShapeDtypeStruct((t, SSM_WIDTH), BF16),
                   jax.ShapeDtypeStruct((t, d), BF16)],
        compiler_params=_params(("parallel",)),
    )(ya, ys, rest, rest, x0, wglu, bglu, wba, wbb, wout)


def mix_bwd(dx1, rest, pa, pb, z, ys, wglu, wba, wbb, wout):
    t, d = dx1.shape
    tm = _rows(t, 256)

    def body(dx_ref, ga_ref, gb_ref, pa_ref, pb_ref, z_ref, ys_ref, wg_ref, wa_ref, wb_ref, wo_ref,
             dya_ref, dys_ref, dg_ref, dpa_ref, dpb_ref, dz_ref, dbg_ref):
        @pl.when(pl.program_id(0) == 0)
        def _():
            dbg_ref[...] = jnp.zeros_like(dbg_ref)

        dmix = _dot(dx_ref[...].astype(BF16), wo_ref[...], NT)
        sa = _sigmoid(ga_ref[...])
        sb = _sigmoid(gb_ref[...])
        dpa = (dmix * sa).astype(BF16)
        dpb = (dmix * sb).astype(BF16)
        dg_ref[:, 0:d] = (dmix * pa_ref[...].astype(F32) * sa * (1.0 - sa)).astype(dg_ref.dtype)
        dg_ref[:, d:2 * d] = (dmix * pb_ref[...].astype(F32) * sb * (1.0 - sb)).astype(dg_ref.dtype)
        dpa_ref[...] = dpa
        dpb_ref[...] = dpb
        dya_ref[...] = _dot(dpa, wa_ref[...], NT).astype(dya_ref.dtype)
        dyb2 = _dot(dpb, wb_ref[...], NT)
        ys = ys_ref[...]
        yb, th = _gelu(ys)
        sg = _sigmoid(z_ref[...].astype(F32))
        dz = dyb2 * yb * sg * (1.0 - sg)
        dzb = dz.astype(BF16)
        dz_ref[...] = dzb
        dbg_ref[...] += _fold8(dz)
        dyb = dyb2 * sg + _dot(dzb, wg_ref[...], NT)
        dgelu = 0.5 * (1.0 + th) + 0.5 * ys * (1.0 - th * th) * GELU_C * (1.0 + 3.0 * GELU_A * ys * ys)
        dys_ref[...] = (dyb * dgelu).astype(dys_ref.dtype)

    row = lambda w: pl.BlockSpec((tm, w), lambda i: (i, 0))
    full = lambda a: pl.BlockSpec(a.shape, lambda i: (0,) * a.ndim)
    return pl.pallas_call(
        body, name="mix_bwd", grid=(t // tm,),
        in_specs=[row(d), pl.BlockSpec((tm, d), lambda i: (i, 0)), pl.BlockSpec((tm, d), lambda i: (i, 1)),
                  row(d), row(d), row(SSM_WIDTH), row(SSM_WIDTH), full(wglu), full(wba), full(wbb), full(wout)],
        out_specs=[row(ATTN_WIDTH), row(SSM_WIDTH), row(2 * d), row(d), row(d), row(SSM_WIDTH),
                   pl.BlockSpec((SUBLANES, SSM_WIDTH), lambda i: (0, 0))],
        out_shape=[jax.ShapeDtypeStruct((t, ATTN_WIDTH), BF16), jax.ShapeDtypeStruct((t, SSM_WIDTH), BF16),
                   jax.ShapeDtypeStruct((t, 2 * d), BF16), jax.ShapeDtypeStruct((t, d), BF16),
                   jax.ShapeDtypeStruct((t, d), BF16), jax.ShapeDtypeStruct((t, SSM_WIDTH), BF16),
                   jax.ShapeDtypeStruct((SUBLANES, SSM_WIDTH), F32)],
        compiler_params=_params(("arbitrary",)),
    )(dx1, rest, rest, pa, pb, z, ys, wglu, wba, wbb, wout)


def mlp_fwd(x1, g, wup, wdown):
    t, d = x1.shape
    ff = wup.shape[1]
    tm, tf = _rows(t, 512), _pick(ff, 1024)
    nf = ff // tf

    def body(x_ref, g_ref, wu_ref, wd_ref, x2_ref, up_ref, h_s, acc_s):
        f = pl.program_id(1)

        @pl.when(f == 0)
        def _():
            xv = x_ref[...]
            h_s[...] = (xv * _rms_scale(xv) * g_ref[...]).astype(BF16)
            acc_s[...] = jnp.zeros_like(acc_s)

        up = _dot(h_s[...], wu_ref[...])
        up_ref[...] = up.astype(up_ref.dtype)
        act = jnp.square(jnp.maximum(up, 0.0)).astype(BF16)
        acc_s[...] += _dot(act, wd_ref[...])

        @pl.when(f == nf - 1)
        def _():
            x2_ref[...] = x_ref[...] + acc_s[...]

    return pl.pallas_call(
        body, name="mlp_fwd", grid=(t // tm, nf),
        in_specs=[pl.BlockSpec((tm, d), lambda i, f: (i, 0)), pl.BlockSpec((1, d), lambda i, f: (0, 0)),
                  pl.BlockSpec((d, tf), lambda i, f: (0, f)), pl.BlockSpec((tf, d), lambda i, f: (f, 0))],
        out_specs=[pl.BlockSpec((tm, d), lambda i, f: (i, 0)), pl.BlockSpec((tm, tf), lambda i, f: (i, f))],
        out_shape=[jax.ShapeDtypeStruct((t, d), F32), jax.ShapeDtypeStruct((t, ff), BF16)],
        scratch_shapes=[pltpu.VMEM((tm, d), BF16), pltpu.VMEM((tm, d), F32)],
        compiler_params=_params(("parallel", "arbitrary")),
    )(x1, g, wup, wdown)


def mlp_bwd(dx2, up, x1, g, wup, wdown):
    t, d = x1.shape
    ff = wup.shape[1]
    tm, tf = _rows(t, 512), _pick(ff, 1024)
    nf = ff // tf

    def body(dx_ref, up_ref, x_ref, g_ref, wu_ref, wd_ref, dup_ref, dx1_ref, dg_ref, dxb_s, acc_s):
        i = pl.program_id(0)
        f = pl.program_id(1)

        @pl.when((i == 0) & (f == 0))
        def _():
            dg_ref[...] = jnp.zeros_like(dg_ref)

        @pl.when(f == 0)
        def _():
            dxb_s[...] = dx_ref[...].astype(BF16)
            acc_s[...] = jnp.zeros_like(acc_s)

        dact = _dot(dxb_s[...], wd_ref[...], NT)
        dup = (dact * 2.0 * jnp.maximum(up_ref[...].astype(F32), 0.0)).astype(BF16)
        dup_ref[...] = dup
        acc_s[...] += _dot(dup, wu_ref[...], NT)

        @pl.when(f == nf - 1)
        def _():
            dxn, dgain = _rms_bwd(x_ref[...], g_ref[...], acc_s[...])
            dx1_ref[...] = dx_ref[...] + dxn
            dg_ref[...] += _fold8(dgain)

    return pl.pallas_call(
        body, name="mlp_bwd", grid=(t // tm, nf),
        in_specs=[pl.BlockSpec((tm, d), lambda i, f: (i, 0)), pl.BlockSpec((tm, tf), lambda i, f: (i, f)),
                  pl.BlockSpec((tm, d), lambda i, f: (i, 0)), pl.BlockSpec((1, d), lambda i, f: (0, 0)),
                  pl.BlockSpec((d, tf), lambda i, f: (0, f)), pl.BlockSpec((tf, d), lambda i, f: (f, 0))],
        out_specs=[pl.BlockSpec((tm, tf), lambda i, f: (i, f)), pl.BlockSpec((tm, d), lambda i, f: (i, 0)),
                   pl.BlockSpec((SUBLANES, d), lambda i, f: (0, 0))],
        out_shape=[jax.ShapeDtypeStruct((t, ff), BF16), jax.ShapeDtypeStruct((t, d), F32),
                   jax.ShapeDtypeStruct((SUBLANES, d), F32)],
        scratch_shapes=[pltpu.VMEM((tm, d), BF16), pltpu.VMEM((tm, d), F32)],
        compiler_params=_params(("arbitrary", "arbitrary")),
    )(dx2, up, x1, g, wup, wdown)


def proj_bwd(dproj, wpad, x0, dx1, g):
    t, d = x0.shape
    m = wpad.shape[1]
    tm = _rows(t, 256)

    def body(dp_ref, w_ref, x_ref, dx1_ref, g_ref, dx0_ref, dg_ref):
        @pl.when(pl.program_id(0) == 0)
        def _():
            dg_ref[...] = jnp.zeros_like(dg_ref)

        dh = _dot(dp_ref[...], w_ref[...], NT)
        dxn, dgain = _rms_bwd(x_ref[...], g_ref[...], dh)
        dx0_ref[...] = dx1_ref[...] + dxn
        dg_ref[...] += _fold8(dgain)

    return pl.pallas_call(
        body, name="proj_bwd", grid=(t // tm,),
        in_specs=[pl.BlockSpec((tm, m), lambda i: (i, 0)), pl.BlockSpec((d, m), lambda i: (0, 0)),
                  pl.BlockSpec((tm, d), lambda i: (i, 0)), pl.BlockSpec((tm, d), lambda i: (i, 0)),
                  pl.BlockSpec((1, d), lambda i: (0, 0))],
        out_specs=[pl.BlockSpec((tm, d), lambda i: (i, 0)), pl.BlockSpec((SUBLANES, d), lambda i: (0, 0))],
        out_shape=[jax.ShapeDtypeStruct((t, d), F32), jax.ShapeDtypeStruct((SUBLANES, d), F32)],
        compiler_params=_params(("arbitrary",)),
    )(dproj, wpad, x0, dx1, g)


def final_loss(x, g, target):
    t, d = x.shape
    tm = _rows(t, 512)

    def body(x_ref, g_ref, t_ref, dx_ref, ls_ref, dg_ref):
        @pl.when(pl.program_id(0) == 0)
        def _():
            ls_ref[...] = jnp.zeros_like(ls_ref)
            dg_ref[...] = jnp.zeros_like(dg_ref)

        xv = x_ref[...]
        gv = g_ref[...]
        err = xv * _rms_scale(xv) * gv - t_ref[...]
        ls_ref[...] += _fold8(err * err) * (0.5 / d)
        dxn, dgain = _rms_bwd(xv, gv, err * (1.0 / d))
        dx_ref[...] = dxn
        dg_ref[...] += _fold8(dgain)

    return pl.pallas_call(
        body, name="final_loss", grid=(t // tm,),
        in_specs=[pl.BlockSpec((tm, d), lambda i: (i, 0)), pl.BlockSpec((1, d), lambda i: (0, 0)),
                  pl.BlockSpec((tm, d), lambda i: (i, 0))],
        out_specs=[pl.BlockSpec((tm, d), lambda i: (i, 0)), pl.BlockSpec((SUBLANES, d), lambda i: (0, 0)),
                   pl.BlockSpec((SUBLANES, d), lambda i: (0, 0))],
        out_shape=[jax.ShapeDtypeStruct((t, d), F32), jax.ShapeDtypeStruct((SUBLANES, d), F32),
                   jax.ShapeDtypeStruct((SUBLANES, d), F32)],
        compiler_params=_params(("arbitrary",)),
    )(x, g, target)


def _peers():
    x, y, c = lax.axis_index("x"), lax.axis_index("y"), lax.axis_index("c")
    peers = []
    for m in range(1, 8):
        fx, fy, fc = (m >> 2) & 1, (m >> 1) & 1, m & 1
        peers.append((m, ((1 - x) if fx else x, (1 - y) if fy else y, (1 - c) if fc else c)))
    return x, y, c, peers


def _remote(src, dst, sems, k, peer):
    return pltpu.make_async_remote_copy(src_ref=src, dst_ref=dst, send_sem=sems[0].at[k], recv_sem=sems[1].at[k],
                                        device_id=peer, device_id_type=MESH)


def _gather_copies(w_refs, out_refs, sems, first):
    x, y, c, peers = _peers()
    chip = 2 * x + y
    cps = []
    for t, (w, o) in enumerate(zip(w_refs, out_refs)):
        half = w.shape[0] // 2
        rows = pl.ds(c * half, half)
        for m, peer in peers:
            if m >> 1:
                cps.append(_remote(w.at[rows], o.at[chip, rows], sems, first + 7 * t + m - 1, peer))
    return cps


def _reduce_copies(g_refs, out_refs, sems, first):
    x, y, c, peers = _peers()
    me = 4 * x + 2 * y + c
    cps = []
    for t, (g, o) in enumerate(zip(g_refs, out_refs)):
        for m, (px, py, pc) in peers:
            cps.append(_remote(g.at[2 * px + py], o.at[me], sems, first + 7 * t + m - 1, (px, py, pc)))
    return cps


def _carry_plan(carries):
    inputs, shapes, spans = [], [], []
    for kind, arrs in carries:
        for a in arrs:
            inputs.append(a)
            shapes.append(jax.ShapeDtypeStruct(((4,) + a.shape) if kind == "gather" else ((8,) + a.shape[1:]), a.dtype))
        spans.append((kind, len(arrs)))

    def build(in_refs, out_refs, sems):
        cps, pos = [], 0
        for kind, cnt in spans:
            fn = _gather_copies if kind == "gather" else _reduce_copies
            cps += fn(in_refs[pos:pos + cnt], out_refs[pos:pos + cnt], sems, 7 * pos)
            pos += cnt
        return cps

    return inputs, shapes, 7 * len(inputs), build


def exchange(carries, name):
    inputs, shapes, n_sems, build = _carry_plan(carries)
    n = len(inputs)

    def body(*refs):
        cps = build(refs[:n], refs[n:2 * n], refs[2 * n:])
        for cp in cps:
            cp.start()
        for cp in cps:
            cp.wait()

    return pl.pallas_call(
        body, name=name, in_specs=[ANY] * n, out_specs=[ANY] * n, out_shape=shapes,
        scratch_shapes=[pltpu.SemaphoreType.DMA((n_sems,)), pltpu.SemaphoreType.DMA((n_sems,))],
    )(*inputs)


def exchange_all(sp):
    def body(s_ref, out_ref, send_sems, recv_sems):
        x, y, c, _ = _peers()
        me = 4 * x + 2 * y + c
        sends = []
        for m in range(1, 8):
            fx, fy, fc = (m >> 2) & 1, (m >> 1) & 1, m & 1
            peer = ((1 - x) if fx else x, (1 - y) if fy else y, (1 - c) if fc else c)
            cp = pltpu.make_async_remote_copy(src_ref=s_ref, dst_ref=out_ref.at[me],
                                              send_sem=send_sems.at[m - 1], recv_sem=recv_sems.at[m - 1],
                                              device_id=peer, device_id_type=MESH)
            cp.start()
            sends.append(cp)
        for cp in sends:
            cp.wait()

    return pl.pallas_call(
        body, name="exchange_all",
        in_specs=[ANY], out_specs=ANY,
        out_shape=jax.ShapeDtypeStruct((8,) + sp.shape, sp.dtype),
        scratch_shapes=[pltpu.SemaphoreType.DMA((7,)), pltpu.SemaphoreType.DMA((7,))],
    )(sp)


def _adamw_math(w, g, m, v):
    m = ADAM_B1 * m + (1.0 - ADAM_B1) * g
    v = ADAM_B2 * v + (1.0 - ADAM_B2) * (g * g)
    m_hat = m / (1.0 - ADAM_B1 ** ADAM_STEP)
    v_hat = v / (1.0 - ADAM_B2 ** ADAM_STEP)
    delta = -ADAM_LR * (m_hat / (jnp.sqrt(v_hat) + ADAM_EPS) + ADAM_WD * w)
    return delta, m, v


def adamw_layer(l, w, m, v, parts, own, device, chip, bufs, name):
    _, r, cdim = w.shape
    tr = _rows(r, PACK_ROW_TILE)

    def body(dev_ref, chip_ref, w_ref, m_ref, v_ref, p_ref, o_ref, *rest):
        g_ref, d_ref, nm_ref, nv_ref = rest[-4:]
        g = None
        for dev in range(8):
            part = jnp.where(dev_ref[0] == dev, o_ref[0], p_ref[dev]).astype(F32)
            g = part if g is None else g + part
        d, nm, nv = _adamw_math(w_ref[0], g, m_ref[0], v_ref[0])
        g_ref[0] = g
        d_ref[0] = d
        nm_ref[0] = nm
        nv_ref[0] = nv

    lay = pl.BlockSpec((1, tr, cdim), lambda i, dev_ref, chip_ref: (l, i, 0))
    in_specs = [lay, lay, lay, pl.BlockSpec((8, tr, cdim), lambda i, dev_ref, chip_ref: (0, i, 0)),
                pl.BlockSpec((1, tr, cdim), lambda i, dev_ref, chip_ref: (chip_ref[0], i, 0))]
    args = [device, chip, w, m, v, parts, own]
    aliases = {}
    if bufs is not None:
        in_specs += [ANY] * 4
        aliases = {len(args) + k: k for k in range(4)}
        args += list(bufs)
    return pl.pallas_call(
        body, name=name,
        grid_spec=pltpu.PrefetchScalarGridSpec(num_scalar_prefetch=2, grid=(r // tr,), in_specs=in_specs,
                                               out_specs=[lay] * 4),
        out_shape=[jax.ShapeDtypeStruct(w.shape, F32)] * 4,
        input_output_aliases=aliases,
        compiler_params=_params(("parallel",)),
    )(*args)


def sum_and_adamw(parts, own, me, w, m, v):
    _, r, cdim = parts.shape
    tr = _rows(r, PACK_ROW_TILE)

    def body(me_ref, p_ref, o_ref, w_ref, m_ref, v_ref, g_ref, d_ref, nm_ref, nv_ref):
        part = lambda k: jnp.where(me_ref[0] == k, o_ref[...], p_ref[k])
        g = part(0)
        for k in range(1, 8):
            g = g + part(k)
        d, nm, nv = _adamw_math(w_ref[...], g, m_ref[...], v_ref[...])
        g_ref[...] = g
        d_ref[...] = d
        nm_ref[...] = nm
        nv_ref[...] = nv

    spec = pl.BlockSpec((tr, cdim), lambda i, me_ref: (i, 0))
    return pl.pallas_call(
        body, name="sum_and_adamw",
        grid_spec=pltpu.PrefetchScalarGridSpec(
            num_scalar_prefetch=1, grid=(r // tr,),
            in_specs=[pl.BlockSpec((8, tr, cdim), lambda i, me_ref: (0, i, 0)), spec, spec, spec, spec],
            out_specs=[spec] * 4),
        out_shape=[jax.ShapeDtypeStruct((r, cdim), F32)] * 4,
        compiler_params=_params(("parallel",)),
    )(me, parts, own, w, m, v)


SHARDED = ("w_in", "w_glu", "w_branch_a", "w_branch_b", "w_out", "w_mlp_up", "w_mlp_down")
SHARD_AXIS = {"w_in": 2, "w_glu": 1, "w_branch_a": 2, "w_branch_b": 2, "w_out": 1, "w_mlp_up": 2, "w_mlp_down": 1}
SMALL = ("norm_mix", "b_forget", "ssm_lambda_re", "ssm_lambda_im", "ssm_log_dt", "ssm_b_re", "ssm_b_im",
         "ssm_c_re", "ssm_c_im", "ssm_d", "b_glu", "norm_mlp", "norm_final")


def pack_flat(arrs):
    flat = jnp.concatenate([a.reshape(-1).astype(F32) for a in arrs])
    unit = PACK_COLS * PACK_ROW_TILE
    rows = (flat.shape[0] + unit - 1) // unit * PACK_ROW_TILE
    return jnp.pad(flat, (0, rows * PACK_COLS - flat.shape[0])).reshape(rows, PACK_COLS)


def unpack_flat(packed, shapes):
    flat = packed.reshape(-1)
    out, off = [], 0
    for s in shapes:
        n = math.prod(s)
        out.append(flat[off:off + n].reshape(tuple(s)))
        off += n
    return out


def _discretise(lam_re, lam_im, log_dt, b_re, b_im):
    dt = jnp.exp(log_dt)[:, None]
    mag = jnp.exp(lam_re * dt)
    ar = mag * jnp.cos(lam_im * dt)
    ai = mag * jnp.sin(lam_im * dt)
    den = lam_re * lam_re + lam_im * lam_im
    cr = ((ar - 1.0) * lam_re + ai * lam_im) / den
    ci = (ai * lam_re - (ar - 1.0) * lam_im) / den
    bbr = cr[:, :, None] * b_re - ci[:, :, None] * b_im
    bbi = cr[:, :, None] * b_im + ci[:, :, None] * b_re
    return ar, ai, bbr, bbi


def _state_lanes(re, im):
    return jnp.concatenate([re.reshape(SSM_CHUNKS, CHUNK_STATES), im.reshape(SSM_CHUNKS, CHUNK_STATES)],
                           axis=1).reshape(STATE_LANES)


def _ssm_inputs(ar, ai, bbr, bbi, c_re, c_im):
    pr, pi = [ar], [ai]
    for _ in range(SUBLANES - 1):
        pr, pi = pr + [pr[-1] * ar - pi[-1] * ai], pi + [pr[-1] * ai + pi[-1] * ar]
    power = lambda n, sign: _state_lanes(pr[n - 1], sign * pi[n - 1])
    idx = jnp.arange(SUBLANES)
    masked = lambda n, sign, keep: jnp.where(keep[:, None], power(n, sign)[None, :], 0.0)
    tabs = jnp.concatenate(
        [masked(s, 1.0, idx >= s) for s in SCAN_STEPS]
        + [jnp.stack([power(i + 1, 1.0) for i in range(SUBLANES)])]
        + [masked(s, -1.0, idx < SUBLANES - s) for s in SCAN_STEPS]
        + [jnp.stack([power(SUBLANES - i, -1.0) for i in range(SUBLANES)])], axis=0)
    eye = jnp.eye(CHUNK_GROUPS, dtype=F32)

    def to_wb(bb):
        t = bb.reshape(SSM_CHUNKS, CHUNK_GROUPS, SSM_STATE, SSM_GROUP_CH).transpose(0, 1, 3, 2)
        return jnp.einsum("jgcp,gh->jgchp", t, eye).reshape(SSM_CHUNKS, LANES, CHUNK_STATES)

    def to_wc(cc):
        t = cc.reshape(SSM_CHUNKS, CHUNK_GROUPS, SSM_GROUP_CH, SSM_STATE)
        return jnp.einsum("jgcp,gh->jhpgc", t, eye).reshape(SSM_CHUNKS, CHUNK_STATES, LANES)

    wb4 = jnp.concatenate([to_wb(bbr), to_wb(bbi)], axis=2).astype(BF16)
    wc4 = jnp.concatenate([to_wc(c_re), -to_wc(c_im)], axis=1).astype(BF16)
    return tabs, wb4, wc4


def _ssm_param_grads(ga8, gwb, gwc):
    eye = jnp.eye(CHUNK_GROUPS, dtype=F32)
    ga = jnp.sum(ga8, axis=0).reshape(SSM_CHUNKS, 2, CHUNK_STATES)
    gar = ga[:, 0].reshape(SSM_GROUPS, SSM_STATE)
    gai = ga[:, 1].reshape(SSM_GROUPS, SSM_STATE)

    def from_wb(g):
        t = g.reshape(SSM_CHUNKS, CHUNK_GROUPS, SSM_GROUP_CH, CHUNK_GROUPS, SSM_STATE)
        return jnp.einsum("jgchp,gh->jgpc", t, eye).reshape(SSM_GROUPS, SSM_STATE, SSM_GROUP_CH)

    def from_wc(g):
        t = g.reshape(SSM_CHUNKS, CHUNK_GROUPS, SSM_STATE, CHUNK_GROUPS, SSM_GROUP_CH)
        return jnp.einsum("jhpgc,gh->jgcp", t, eye).reshape(SSM_GROUPS, SSM_GROUP_CH, SSM_STATE)

    return (gar, gai, from_wb(gwb[:, :, :CHUNK_STATES]), from_wb(gwb[:, :, CHUNK_STATES:]),
            from_wc(gwc[:, :CHUNK_STATES]), -from_wc(gwc[:, CHUNK_STATES:]))


def kernel(x, norm_mix, w_in, b_forget, ssm_lambda_re, ssm_lambda_im, ssm_log_dt, ssm_b_re, ssm_b_im, ssm_c_re, ssm_c_im, ssm_d, w_glu, b_glu, w_branch_a, w_branch_b, w_out, norm_mlp, w_mlp_up, w_mlp_down, norm_final, loss_target, m_norm_mix, m_w_in, m_b_forget, m_ssm_lambda_re, m_ssm_lambda_im, m_ssm_log_dt, m_ssm_b_re, m_ssm_b_im, m_ssm_c_re, m_ssm_c_im, m_ssm_d, m_w_glu, m_b_glu, m_w_branch_a, m_w_branch_b, m_w_out, m_norm_mlp, m_w_mlp_up, m_w_mlp_down, m_norm_final, v_norm_mix, v_w_in, v_b_forget, v_ssm_lambda_re, v_ssm_lambda_im, v_ssm_log_dt, v_ssm_b_re, v_ssm_b_im, v_ssm_c_re, v_ssm_c_im, v_ssm_d, v_w_glu, v_b_glu, v_w_branch_a, v_w_branch_b, v_w_out, v_norm_mlp, v_w_mlp_up, v_w_mlp_down, v_norm_final):
    args = dict(locals())
    bsz, seq, d = x.shape
    nl = norm_mix.shape[0]
    tokens = bsz * seq
    aw, sw = ATTN_WIDTH, SSM_WIDTH
    chip = (2 * lax.axis_index("x") + lax.axis_index("y")).astype(jnp.int32)
    chip_id = chip.reshape(1)
    device_id = (2 * chip + lax.axis_index("c").astype(jnp.int32)).reshape(1)

    own = {n: args[n].astype(BF16) for n in SHARDED}
    shards = lambda l: [own[n][l] for n in SHARDED]
    o_f, o_u, o_ga, o_gb = 3 * aw, 3 * aw + ATTN_HEADS, 3 * aw + ATTN_HEADS + sw, 3 * aw + ATTN_HEADS + sw + d
    u_blk = 2 * d // sw
    f_blk = (2 * d + sw) // F_PAD
    bf_pad = jnp.pad(b_forget, ((0, 0), (0, F_PAD - ATTN_HEADS)))

    def assemble(l, gathered):
        full = {n: jnp.concatenate([jnp.where(chip == k, own[n][l], g[k]) for k in range(4)],
                                   axis=SHARD_AXIS[n] - 1) for n, g in zip(SHARDED, gathered)}
        win = full["w_in"]
        full["w_qkv"] = win[:, :o_f]
        full["w_rest"] = jnp.concatenate([win[:, o_ga:o_gb], win[:, o_gb:], win[:, o_u:o_ga],
                                          jnp.pad(win[:, o_f:o_u], ((0, 0), (0, F_PAD - ATTN_HEADS)))], axis=1)
        full["w_pad"] = jnp.concatenate([full["w_qkv"], full["w_rest"]], axis=1)
        return full

    disc = [jax.vjp(_discretise, ssm_lambda_re[l], ssm_lambda_im[l], ssm_log_dt[l], ssm_b_re[l], ssm_b_im[l])
            for l in range(nl)]

    gathered = exchange([("gather", shards(0))], "gather_first")
    xs = x.reshape(tokens, d)
    saved, weights = [], []
    for l in range(nl):
        w = assemble(l, gathered)
        g1 = norm_mix[l].reshape(1, d)
        qkv = norm_matmul(xs, g1, w["w_qkv"], BF16, "proj_qkv")
        rest = norm_matmul(xs, g1, w["w_rest"], F32, "proj_rest")
        cumcol, cumrow = forget_cumsum(rest, bf_pad[l:l + 1], bsz, seq, f_blk)
        ya, lser, gathered = fox_fwd(qkv, cumrow, bsz, seq, [("gather", shards(l + 1))] if l + 1 < nl else [])
        tabs, wb4, wc4 = _ssm_inputs(*disc[l][0], ssm_c_re[l], ssm_c_im[l])
        dskip = ssm_d[l].reshape(1, sw)
        ys, hs = ssm_fwd(rest, wb4, wc4, tabs, dskip, bsz, seq, u_blk)
        x1, z, pa, pb, yb, yb2, mixed = mix_fwd(ya, ys, rest, xs, w["w_glu"], b_glu[l].reshape(1, sw),
                                                 w["w_branch_a"], w["w_branch_b"], w["w_out"])
        x2, up = mlp_fwd(x1, norm_mlp[l].reshape(1, d), w["w_mlp_up"], w["w_mlp_down"])
        saved.append(dict(x0=xs, qkv=qkv, rest=rest, cumcol=cumcol, ya=ya, lser=lser,
                          tabs=tabs, wb4=wb4, wc4=wc4, dskip=dskip, ys=ys, hs=hs, x1=x1, z=z, pa=pa, pb=pb,
                          yb=yb, yb2=yb2, mixed=mixed, up=up))
        weights.append(w)
        xs = x2
    dx, loss_rows, dgf_rows = final_loss(xs, norm_final.reshape(1, d), loss_target.reshape(tokens, d))
    loss = lax.psum(jnp.sum(loss_rows), ("x", "y", "c"))

    early = [n for n in SHARDED if n != "w_in"]
    big = {n: [None] * nl for n in SHARDED}
    parts = {n: [None] * nl for n in SHARDED}
    small = {n: [None] * nl for n in SMALL if n != "norm_final"}
    for l in reversed(range(nl)):
        s, w = saved[l], weights[l]
        g2 = norm_mlp[l].reshape(1, d)
        dup, dx1, dg2 = mlp_bwd(dx, s["up"], s["x1"], g2, w["w_mlp_up"], w["w_mlp_down"])
        big["w_mlp_down"][l] = matmul_tn(s["up"], dx, "grad_w_mlp_down", a_kind="relu2", shard_axis=0, out_dtype=BF16)
        big["w_mlp_up"][l] = matmul_tn(s["x1"], dup, "grad_w_mlp_up", a_kind="norm", gain=g2, shard_axis=1,
                                       out_dtype=BF16)
        small["norm_mlp"][l] = jnp.sum(dg2, axis=0)
        dya, dys, dgab, dpa, dpb, dz, dbg = mix_bwd(dx1, s["rest"], s["pa"], s["pb"], s["z"], s["ys"],
                                                    w["w_glu"], w["w_branch_a"], w["w_branch_b"], w["w_out"])
        big["w_out"][l] = matmul_tn(s["mixed"], dx1, "grad_w_out", shard_axis=0, out_dtype=BF16)
        big["w_branch_a"][l] = matmul_tn(s["ya"], dpa, "grad_w_branch_a", shard_axis=1, out_dtype=BF16)
        big["w_branch_b"][l] = matmul_tn(s["yb2"], dpb, "grad_w_branch_b", shard_axis=1, out_dtype=BF16)
        big["w_glu"][l] = matmul_tn(s["yb"], dz, "grad_w_glu", shard_axis=0, out_dtype=BF16)
        small["b_glu"][l] = jnp.sum(dbg, axis=0)
        du, ga8, gwb, gwc, gd8 = ssm_bwd(dys, s["rest"], s["hs"], s["wb4"], s["wc4"], s["tabs"], s["dskip"],
                                         bsz, seq, u_blk)
        gar, gai, gbbr, gbbi, gcr, gci = _ssm_param_grads(ga8, gwb, gwc)
        glr, gli, gdt, gbr, gbi = disc[l][1]((gar, gai, gbbr, gbbi))
        small["ssm_lambda_re"][l], small["ssm_lambda_im"][l], small["ssm_log_dt"][l] = glr, gli, gdt
        small["ssm_b_re"][l], small["ssm_b_im"][l] = gbr, gbi
        small["ssm_c_re"][l], small["ssm_c_im"][l] = gcr, gci
        small["ssm_d"][l] = jnp.sum(gd8, axis=0)
        carries = [("reduce", [big[n][l] for n in early])]
        if l + 1 < nl:
            carries.append(("reduce", [big["w_in"][l + 1]]))
        dq, dk, dv, df, dbf, lands = fox_bwd(s["qkv"], dya, s["ya"], s["lser"], s["cumcol"],
                                             s["rest"], bf_pad[l:l + 1], bsz, seq, f_blk, carries)
        for n, p in zip(early, lands):
            parts[n][l] = p
        if l + 1 < nl:
            parts["w_in"][l + 1] = lands[len(early)]
        small["b_forget"][l] = jnp.sum(dbf, axis=0)[:ATTN_HEADS]
        dproj = jnp.concatenate([dq, dk, dv, dgab, du, df], axis=1)
        g1 = norm_mix[l].reshape(1, d)
        dwp = matmul_tn(s["x0"], dproj, "grad_w_in", a_kind="norm", gain=g1)
        big["w_in"][l] = jnp.stack(jnp.split(jnp.concatenate(
            [dwp[:, :o_f], dwp[:, o_f + 2 * d + sw:o_f + 2 * d + sw + ATTN_HEADS],
             dwp[:, o_f + 2 * d:o_f + 2 * d + sw], dwp[:, o_f:o_f + 2 * d]], axis=1), 4, axis=1)).astype(BF16)
        dx, dg1 = proj_bwd(dproj, w["w_pad"], s["x0"], dx1, g1)
        small["norm_mix"][l] = jnp.sum(dg1, axis=0)
    grad_x = dx.reshape(bsz, seq, d)
    parts["w_in"][0] = exchange([("reduce", [big["w_in"][0]])], "reduce_last")[0]

    out_g, out_d, out_m, out_v = {}, {}, {}, {}
    for n in SHARDED:
        bufs = None
        for l in range(nl):
            bufs = adamw_layer(l, args[n], args["m_" + n], args["v_" + n], parts[n][l], big[n][l], device_id, chip_id,
                               bufs, "adamw_" + n)
        out_g[n], out_d[n], out_m[n], out_v[n] = bufs

    small_g = [jnp.stack(small[n]) if n != "norm_final" else jnp.sum(dgf_rows, axis=0) for n in SMALL]
    small_shapes = [args[n].shape for n in SMALL]
    small_packed = pack_flat(small_g)
    sg, sd, sm, sv = sum_and_adamw(exchange_all(small_packed), small_packed, device_id,
                                   pack_flat([args[n] for n in SMALL]),
                                   pack_flat([args["m_" + n] for n in SMALL]),
                                   pack_flat([args["v_" + n] for n in SMALL]))
    for res, packed in ((out_g, sg), (out_d, sd), (out_m, sm), (out_v, sv)):
        res.update(zip(SMALL, unpack_flat(packed, small_shapes)))

    order = ("norm_mix", "w_in", "b_forget", "ssm_lambda_re", "ssm_lambda_im", "ssm_log_dt", "ssm_b_re",
             "ssm_b_im", "ssm_c_re", "ssm_c_im", "ssm_d", "w_glu", "b_glu", "w_branch_a", "w_branch_b", "w_out",
             "norm_mlp", "w_mlp_up", "w_mlp_down", "norm_final")
    return (loss, grad_x, *[out_g[n] for n in order], *[out_d[n] for n in order],
            *[out_m[n] for n in order], *[out_v[n] for n in order])
```

```python
import math

import jax
import jax.numpy as jnp
from jax import lax
from jax.experimental import pallas as pl
from jax.experimental.pallas import tpu as pltpu

F32 = jnp.float32
BF16 = jnp.bfloat16
MESH = pl.DeviceIdType.MESH
ANY = pl.BlockSpec(memory_space=pl.ANY)

ATTN_HEADS = 8
HEAD_DIM = 64
ATTN_WIDTH = ATTN_HEADS * HEAD_DIM
HEAD_PAIRS = ATTN_HEADS // 2
SSM_GROUPS = 32
SSM_GROUP_CH = 16
SSM_STATE = 64
SSM_WIDTH = SSM_GROUPS * SSM_GROUP_CH
LANES = 128
SUBLANES = 8
SSM_CHUNKS = SSM_WIDTH // LANES
CHUNK_GROUPS = SSM_GROUPS // SSM_CHUNKS
CHUNK_STATES = CHUNK_GROUPS * SSM_STATE
CHUNK_LANES = 2 * CHUNK_STATES
STATE_LANES = SSM_CHUNKS * CHUNK_LANES
F_PAD = LANES
RMS_EPS = 1e-6
ADAM_LR = 0.001
ADAM_B1 = 0.9
ADAM_B2 = 0.999
ADAM_EPS = 1e-08
ADAM_WD = 0.01
ADAM_STEP = 10
PACK_COLS = 1024
PACK_ROW_TILE = 256
VMEM_LIMIT = 52 * 1024 * 1024
NEG_BIG = -1e30
GELU_C = math.sqrt(2.0 / math.pi)
GELU_A = 0.044715

NN = (((1,), (0,)), ((), ()))
NT = (((1,), (1,)), ((), ()))
TN = (((0,), (0,)), ((), ()))


def _pick(n, pref):
    if n <= pref:
        return n
    best = LANES
    for t in range(LANES, pref + 1, LANES):
        if n % t == 0:
            best = t
    assert n % best == 0, (n, pref)
    return best


def _rows(n, pref):
    t = min(n, pref)
    while n % t:
        t //= 2
    assert t % 16 == 0 or t == n, (n, pref)
    return t


def _params(sem):
    return pltpu.CompilerParams(dimension_semantics=sem, vmem_limit_bytes=VMEM_LIMIT)


def _fold8(v):
    r, c = v.shape
    return jnp.sum(v.reshape(r // SUBLANES, SUBLANES, c), axis=0)


def _dot(a, b, dims=None):
    if dims is None:
        return jnp.dot(a, b, preferred_element_type=F32)
    return lax.dot_general(a, b, dims, preferred_element_type=F32)


def _dot_exact(a, b, dims):
    return lax.dot_general(a, b, dims, preferred_element_type=F32, precision=lax.Precision.HIGHEST)


def _sigmoid(v):
    return 1.0 / (1.0 + jnp.exp(-v))


def _rms_scale(x):
    return lax.rsqrt(jnp.mean(x * x, axis=-1, keepdims=True) + RMS_EPS)


def _rms_bwd(x, g, dh):
    r = _rms_scale(x)
    xn = x * r
    dxn = dh * g
    dx = r * (dxn - xn * jnp.mean(dxn * xn, axis=-1, keepdims=True))
    return dx, dh * xn


def norm_matmul(x, g, w, out_dtype, name):
    t, d = x.shape
    m = w.shape[1]
    tm, tn = _rows(t, 1024), _pick(m, 1024)

    def body(x_ref, g_ref, w_ref, o_ref):
        xv = x_ref[...]
        h = (xv * _rms_scale(xv) * g_ref[...]).astype(BF16)
        o_ref[...] = _dot(h, w_ref[...]).astype(o_ref.dtype)

    return pl.pallas_call(
        body, name=name, grid=(t // tm, m // tn),
        in_specs=[pl.BlockSpec((tm, d), lambda i, j: (i, 0)),
                  pl.BlockSpec((1, d), lambda i, j: (0, 0)),
                  pl.BlockSpec((d, tn), lambda i, j: (0, j))],
        out_specs=pl.BlockSpec((tm, tn), lambda i, j: (i, j)),
        out_shape=jax.ShapeDtypeStruct((t, m), out_dtype),
        compiler_params=_params(("parallel", "arbitrary")),
    )(x, g, w)


def matmul_tn(a, b, name, a_kind="plain", gain=None, shard_axis=None, out_dtype=F32, tm_pref=1024, tn_pref=1536,
              tk_pref=1024):
    t, ma = a.shape
    nb = b.shape[1]
    tm = ma if a_kind == "norm" else _pick(ma, tm_pref)
    tn = _pick(nb, tn_pref)
    tk = _rows(t, tk_pref)
    nk = t // tk
    if shard_axis == 0:
        per = tm * 4 // ma
        assert per >= 1 and (ma // 4) * per == tm, (ma, tm)
        out_shape, out_spec = (4, ma // 4, nb), pl.BlockSpec((per, ma // 4, tn), lambda i, j, k: (i, 0, j))
    elif shard_axis == 1:
        per = tn * 4 // nb
        assert per >= 1 and (nb // 4) * per == tn, (nb, tn)
        out_shape, out_spec = (4, ma, nb // 4), pl.BlockSpec((per, tm, nb // 4), lambda i, j, k: (j, i, 0))
    else:
        out_shape, out_spec = (ma, nb), pl.BlockSpec((tm, tn), lambda i, j, k: (i, j))

    def body(*refs):
        if a_kind == "norm":
            a_ref, g_ref, b_ref, o_ref, acc = refs
        else:
            a_ref, b_ref, o_ref, acc = refs
        k = pl.program_id(2)

        @pl.when(k == 0)
        def _():
            acc[...] = jnp.zeros_like(acc)

        av = a_ref[...]
        if a_kind == "norm":
            av = av * _rms_scale(av) * g_ref[...]
        elif a_kind == "relu2":
            av = jnp.square(jnp.maximum(av.astype(F32), 0.0))
        acc[...] += _dot(av.astype(BF16), b_ref[...].astype(BF16), TN)

        @pl.when(k == nk - 1)
        def _():
            if shard_axis == 0:
                o_ref[...] = acc[...].reshape(o_ref.shape).astype(o_ref.dtype)
            elif shard_axis == 1:
                cs = nb // 4
                for n in range(o_ref.shape[0]):
                    o_ref[n] = acc[:, n * cs:(n + 1) * cs].astype(o_ref.dtype)
            else:
                o_ref[...] = acc[...].astype(o_ref.dtype)

    in_specs = [pl.BlockSpec((tk, tm), lambda i, j, k: (k, i))]
    args = [a]
    if a_kind == "norm":
        in_specs.append(pl.BlockSpec((1, ma), lambda i, j, k: (0, 0)))
        args.append(gain)
    in_specs.append(pl.BlockSpec((tk, tn), lambda i, j, k: (k, j)))
    args.append(b)
    return pl.pallas_call(
        body, name=name, grid=(ma // tm, nb // tn, nk),
        in_specs=in_specs,
        out_specs=out_spec,
        out_shape=jax.ShapeDtypeStruct(out_shape, out_dtype),
        scratch_shapes=[pltpu.VMEM((tm, tn), F32)],
        compiler_params=_params(("parallel", "parallel", "arbitrary")),
    )(*args)


def _tri(n, upper):
    r = lax.broadcasted_iota(jnp.int32, (n, n), 0)
    c = lax.broadcasted_iota(jnp.int32, (n, n), 1)
    return jnp.where((c >= r) if upper else (c <= r), 1.0, 0.0).astype(F32)


def _head_rows():
    r = lax.broadcasted_iota(jnp.int32, (SUBLANES, LANES), 0)
    c = lax.broadcasted_iota(jnp.int32, (SUBLANES, LANES), 1)
    return jnp.where(r == c, 1.0, 0.0).astype(F32)


def forget_cumsum(rest, bf, bsz, seq, f_blk):
    tc = _rows(seq, 512)
    nc = seq // tc

    def body(f_ref, b_ref, col_ref, row_ref, carry):
        c = pl.program_id(1)

        @pl.when(c == 0)
        def _():
            carry[...] = jnp.zeros_like(carry)

        z = f_ref[...] + b_ref[...]
        logf = jnp.minimum(z, 0.0) - jnp.log(1.0 + jnp.exp(-jnp.abs(z)))
        cum = _dot_exact(_tri(tc, False), logf, NN) + carry[0:1, :]
        col_ref[0] = cum
        row_ref[0] = _dot_exact(_head_rows(), cum, NT)
        carry[...] = jnp.broadcast_to(cum[tc - 1:tc, :], carry.shape)

    return pl.pallas_call(
        body, name="forget_cumsum", grid=(bsz, nc),
        in_specs=[pl.BlockSpec((tc, F_PAD), lambda b, c: (b * nc + c, f_blk)),
                  pl.BlockSpec((1, F_PAD), lambda b, c: (0, 0))],
        out_specs=[pl.BlockSpec((1, tc, LANES), lambda b, c: (b, c, 0)),
                   pl.BlockSpec((1, SUBLANES, tc), lambda b, c: (b, 0, c))],
        out_shape=[jax.ShapeDtypeStruct((bsz, seq, LANES), F32),
                   jax.ShapeDtypeStruct((bsz, SUBLANES, seq), F32)],
        scratch_shapes=[pltpu.VMEM((SUBLANES, LANES), F32)],
        compiler_params=_params(("parallel", "arbitrary")),
    )(rest, bf)


def forget_bwd(dcq, dcp, rest, bf, bsz, seq, f_blk):
    tc = _attn_tile(seq)
    nc = seq // tc

    def body(dq_ref, dc_ref, f_ref, b_ref, df_ref, db_ref, carry):
        b = pl.program_id(0)
        c = pl.program_id(1)

        @pl.when(c == 0)
        def _():
            carry[...] = jnp.zeros_like(carry)

        @pl.when((b == 0) & (c == 0))
        def _():
            db_ref[...] = jnp.zeros_like(db_ref)

        row = lax.broadcasted_iota(jnp.int32, (SUBLANES, tc), 0)
        heads = jnp.zeros((SUBLANES, tc), F32)
        dc = jnp.zeros((tc, LANES), F32)
        for p in range(HEAD_PAIRS):
            blk = dq_ref[0, p, 0]
            heads = heads + jnp.where(row == 2 * p, blk[0:1], 0.0) + jnp.where(row == 2 * p + 1, blk[1:2], 0.0)
            dc = dc + dc_ref[0, p]
        dc = dc + jnp.concatenate([heads, jnp.zeros((LANES - SUBLANES, tc), F32)], axis=0).T
        dlogf = _dot_exact(_tri(tc, True), dc, NN) + carry[0:1, :]
        carry[...] = jnp.broadcast_to(dlogf[0:1, :], carry.shape)
        z = f_ref[...] + b_ref[...]
        lane = lax.broadcasted_iota(jnp.int32, z.shape, 1)
        df = jnp.where(lane < ATTN_HEADS, dlogf * _sigmoid(-z), 0.0)
        df_ref[...] = df.astype(df_ref.dtype)
        db_ref[...] += _fold8(df)

    return pl.pallas_call(
        body, name="forget_bwd", grid=(bsz, nc),
        in_specs=[pl.BlockSpec((1, HEAD_PAIRS, 1, SUBLANES, tc), lambda b, c: (b, 0, nc - 1 - c, 0, 0)),
                  pl.BlockSpec((1, HEAD_PAIRS, tc, LANES), lambda b, c: (b, 0, nc - 1 - c, 0)),
                  pl.BlockSpec((tc, F_PAD), lambda b, c: (b * nc + nc - 1 - c, f_blk)),
                  pl.BlockSpec((1, F_PAD), lambda b, c: (0, 0))],
        out_specs=[pl.BlockSpec((tc, F_PAD), lambda b, c: (b * nc + nc - 1 - c, 0)),
                   pl.BlockSpec((SUBLANES, F_PAD), lambda b, c: (0, 0))],
        out_shape=[jax.ShapeDtypeStruct((bsz * seq, F_PAD), BF16),
                   jax.ShapeDtypeStruct((SUBLANES, F_PAD), F32)],
        scratch_shapes=[pltpu.VMEM((SUBLANES, LANES), F32)],
        compiler_params=_params(("arbitrary", "arbitrary")),
    )(dcq, dcp, rest, bf)


def _attn_tile(seq):
    return 512 if seq >= 2048 else 128


def _lane_head(shape, par):
    lane = lax.broadcasted_iota(jnp.int32, shape, len(shape) - 1)
    return (lane >= HEAD_DIM) if par else (lane < HEAD_DIM)


def _pick_lane(block, idx):
    lane = lax.broadcasted_iota(jnp.int32, block.shape, 1)
    return jnp.sum(jnp.where(lane == idx, block, 0.0), axis=1, keepdims=True)


def _pair_rows(lo_lane, hi_lane):
    r = lax.broadcasted_iota(jnp.int32, (SUBLANES, LANES), 0)
    c = lax.broadcasted_iota(jnp.int32, (SUBLANES, LANES), 1)
    if lo_lane is None:
        sel = ((r == 0) & (c < HEAD_DIM)) | ((r == 1) & (c >= HEAD_DIM))
    else:
        sel = ((r == 0) & (c == lo_lane)) | ((r == 1) & (c == hi_lane))
    return jnp.where(sel, 1.0, 0.0).astype(F32)


def _causal(s, transposed):
    r = lax.broadcasted_iota(jnp.int32, s.shape, 0)
    c = lax.broadcasted_iota(jnp.int32, s.shape, 1)
    return jnp.where((c >= r) if transposed else (r >= c), s, NEG_BIG)


def _causal_pairs(n, key_major):
    if key_major:
        pairs = [(i, j) for j in range(n) for i in range(j, n)]
    else:
        pairs = [(i, j) for i in range(n) for j in range(i + 1)]
    return (jnp.asarray([p[0] for p in pairs], jnp.int32), jnp.asarray([p[1] for p in pairs], jnp.int32))


def fox_fwd(qkv, cumrow, bsz, seq, carries=()):
    tq = _attn_tile(seq)
    nq = seq // tq
    scale = HEAD_DIM ** -0.5
    kb, vb = ATTN_WIDTH // LANES, 2 * ATTN_WIDTH // LANES
    qi_tab, kj_tab = _causal_pairs(nq, False)
    npairs = int(qi_tab.shape[0])
    sends, landings, n_sems, build = _carry_plan(carries)
    nc = len(sends)

    def body(qi_ref, kj_ref, q_ref, k_ref, v_ref, cr_ref, *rest):
        ins, (o_ref, lr_ref), lands = rest[:nc], rest[nc:nc + 2], rest[nc + 2:2 * nc + 2]
        (m_s, acc_s), sems = rest[2 * nc + 2:2 * nc + 4], rest[2 * nc + 4:]
        hp = pl.program_id(1)
        qi = qi_ref[pl.program_id(2)]
        kj = kj_ref[pl.program_id(2)]
        if nc:
            @pl.when((pl.program_id(0) == 0) & (hp == 0) & (pl.program_id(2) == 0))
            def _():
                for cp in build(ins, lands, sems):
                    cp.start()

        @pl.when(kj == 0)
        def _():
            m_s[...] = jnp.full_like(m_s, NEG_BIG)
            acc_s[...] = jnp.zeros_like(acc_s)

        def step(diag):
            q = q_ref[...]
            k = k_ref[...]
            v = v_ref[...]
            for par in range(2):
                sel = _lane_head(q.shape, par)
                qh = jnp.where(sel, q, 0.0) * scale
                s = _dot(qh.astype(BF16), k, NT) - cr_ref[0, pl.ds(2 * hp + par, 1), :]
                if diag:
                    s = _causal(s, False)
                m_prev = m_s[par]
                m_new = jnp.maximum(m_prev, jnp.max(s, axis=1, keepdims=True))
                p = jnp.exp(s - m_new).astype(BF16)
                acc_s[par] = jnp.exp(m_prev - m_new) * acc_s[par] + _dot(p, jnp.where(sel, v, 1.0).astype(BF16))
                m_s[par] = m_new

        @pl.when(kj < qi)
        def _():
            step(False)

        @pl.when(kj == qi)
        def _():
            step(True)
            lo = _lane_head((tq, LANES), 0)
            sums = [pltpu.roll(acc_s[par], HEAD_DIM, 1) for par in range(2)]
            out = jnp.where(lo, acc_s[0] / sums[0], acc_s[1] / sums[1])
            o_ref[...] = out.astype(o_ref.dtype)
            lse = jnp.where(lo, m_s[0] + jnp.log(sums[0]), m_s[1] + jnp.log(sums[1]))
            lr_ref[0, 0] = _dot_exact(_pair_rows(0, HEAD_DIM), lse, NT)

        if nc:
            @pl.when((pl.program_id(0) == bsz - 1) & (hp == HEAD_PAIRS - 1) & (pl.program_id(2) == npairs - 1))
            def _():
                for cp in build(ins, lands, sems):
                    cp.wait()

    sem_shapes = [pltpu.SemaphoreType.DMA((n_sems,)), pltpu.SemaphoreType.DMA((n_sems,))] if nc else []
    outs = pl.pallas_call(
        body, name="fox_fwd",
        grid_spec=pltpu.PrefetchScalarGridSpec(
            num_scalar_prefetch=2, grid=(bsz, HEAD_PAIRS, npairs),
            in_specs=[pl.BlockSpec((tq, LANES), lambda b, h, t, qi, kj: (b * nq + qi[t], h)),
                      pl.BlockSpec((tq, LANES), lambda b, h, t, qi, kj: (b * nq + kj[t], kb + h)),
                      pl.BlockSpec((tq, LANES), lambda b, h, t, qi, kj: (b * nq + kj[t], vb + h)),
                      pl.BlockSpec((1, SUBLANES, tq), lambda b, h, t, qi, kj: (b, 0, kj[t]))] + [ANY] * nc,
            out_specs=[pl.BlockSpec((tq, LANES), lambda b, h, t, qi, kj: (b * nq + qi[t], h)),
                       pl.BlockSpec((1, 1, SUBLANES, tq), lambda b, h, t, qi, kj: (b, h, 0, qi[t]))] + [ANY] * nc,
            scratch_shapes=[pltpu.VMEM((2, tq, 1), F32), pltpu.VMEM((2, tq, LANES), F32)] + sem_shapes),
        out_shape=[jax.ShapeDtypeStruct((bsz * seq, ATTN_WIDTH), BF16),
                   jax.ShapeDtypeStruct((bsz, HEAD_PAIRS, SUBLANES, seq), F32)] + landings,
        compiler_params=_params(("arbitrary", "arbitrary", "arbitrary")),
    )(qi_tab, kj_tab, qkv, qkv, qkv, cumrow, *sends)
    return outs[0], outs[1], list(outs[2:])


def fox_bwd_kernel(qkv, do, o, lser, cumcol, bsz, seq, carries=()):
    tk = _attn_tile(seq)
    nk = seq // tk
    scale = HEAD_DIM ** -0.5
    kb, vb = ATTN_WIDTH // LANES, 2 * ATTN_WIDTH // LANES
    qi_tab, kj_tab = _causal_pairs(nk, True)
    npairs = int(qi_tab.shape[0])
    sends, landings, n_sems, build = _carry_plan(carries)
    nc = len(sends)

    def body(qi_ref, kj_ref, q_ref, k_ref, v_ref, do_ref, o_ref, lr_ref, cc_ref, *rest):
        ins, lands = rest[:nc], rest[nc + 5:2 * nc + 5]
        dq_ref, dk_ref, dv_ref, dc_ref, dr_ref = rest[nc:nc + 5]
        (dq_s, dk_s, dv_s, dc_s, dr_s), sems = rest[2 * nc + 5:2 * nc + 10], rest[2 * nc + 10:]
        hp = pl.program_id(1)
        t = pl.program_id(2)
        qi = qi_ref[t]
        kj = kj_ref[t]
        if nc:
            @pl.when((pl.program_id(0) == 0) & (hp == 0) & (t == 0))
            def _():
                for cp in build(ins, lands, sems):
                    cp.start()

        @pl.when(t == 0)
        def _():
            dq_s[...] = jnp.zeros_like(dq_s)
            dr_s[...] = jnp.zeros_like(dr_s)

        @pl.when(qi == kj)
        def _():
            dk_s[...] = jnp.zeros_like(dk_s)
            dv_s[...] = jnp.zeros_like(dv_s)
            dc_s[...] = jnp.zeros_like(dc_s)

        def step(diag):
            q = q_ref[...]
            k = k_ref[...]
            v = v_ref[...]
            dov = do_ref[...]
            prod = dov.astype(F32) * o_ref[...].astype(F32)
            drow = _dot_exact(_pair_rows(None, None), prod, NT)
            lrow = lr_ref[0, 0]
            lane = lax.broadcasted_iota(jnp.int32, (tk, LANES), 1)
            for par in range(2):
                head = 2 * hp + par
                sel = _lane_head(k.shape, par)
                kh = (jnp.where(sel, k, 0.0) * scale).astype(BF16)
                st = _dot(kh, q, NT) - _pick_lane(cc_ref[0], head)
                if diag:
                    st = _causal(st, True)
                pt = jnp.exp(st - lrow[par:par + 1, :])
                vh = jnp.where(sel, v, 0.0)
                dpt = _dot(vh.astype(BF16), dov, NT)
                dst = pt * (dpt - drow[par:par + 1, :])
                dsb = dst.astype(BF16)
                dv_s[...] += jnp.where(sel, _dot(pt.astype(BF16), dov), 0.0)
                dk_s[...] += jnp.where(sel, _dot(dsb, q), 0.0)
                dq_s[qi] += _dot(dsb, kh, TN)
                dc_s[...] += jnp.where(lane == head, -jnp.sum(dst, axis=1, keepdims=True), 0.0)
                dr_s[qi, par:par + 1, :] += jnp.sum(dst, axis=0, keepdims=True)

        @pl.when(qi > kj)
        def _():
            step(False)

        @pl.when(qi == kj)
        def _():
            step(True)

        @pl.when(qi == nk - 1)
        def _():
            dk_ref[...] = (dk_s[...] * scale).astype(dk_ref.dtype)
            dv_ref[...] = dv_s[...].astype(dv_ref.dtype)
            dc_ref[0, 0] = dc_s[...]

        @pl.when(t == npairs - 1)
        def _():
            for i in range(nk):
                dq_ref[i * tk:(i + 1) * tk, :] = dq_s[i].astype(dq_ref.dtype)
            dr_ref[0, 0] = dr_s[...]

        if nc:
            @pl.when((pl.program_id(0) == bsz - 1) & (hp == HEAD_PAIRS - 1) & (t == npairs - 1))
            def _():
                for cp in build(ins, lands, sems):
                    cp.wait()

    sem_shapes = [pltpu.SemaphoreType.DMA((n_sems,)), pltpu.SemaphoreType.DMA((n_sems,))] if nc else []
    outs = pl.pallas_call(
        body, name="fox_bwd",
        grid_spec=pltpu.PrefetchScalarGridSpec(
            num_scalar_prefetch=2, grid=(bsz, HEAD_PAIRS, npairs),
            in_specs=[pl.BlockSpec((tk, LANES), lambda b, h, t, qi, kj: (b * nk + qi[t], h)),
                      pl.BlockSpec((tk, LANES), lambda b, h, t, qi, kj: (b * nk + kj[t], kb + h)),
                      pl.BlockSpec((tk, LANES), lambda b, h, t, qi, kj: (b * nk + kj[t], vb + h)),
                      pl.BlockSpec((tk, LANES), lambda b, h, t, qi, kj: (b * nk + qi[t], h)),
                      pl.BlockSpec((tk, LANES), lambda b, h, t, qi, kj: (b * nk + qi[t], h)),
                      pl.BlockSpec((1, 1, SUBLANES, tk), lambda b, h, t, qi, kj: (b, h, 0, qi[t])),
                      pl.BlockSpec((1, tk, LANES), lambda b, h, t, qi, kj: (b, kj[t], 0))] + [ANY] * nc,
            out_specs=[pl.BlockSpec((seq, LANES), lambda b, h, t, qi, kj: (b, h)),
                       pl.BlockSpec((tk, LANES), lambda b, h, t, qi, kj: (b * nk + kj[t], h)),
                       pl.BlockSpec((tk, LANES), lambda b, h, t, qi, kj: (b * nk + kj[t], h)),
                       pl.BlockSpec((1, 1, tk, LANES), lambda b, h, t, qi, kj: (b, h, kj[t], 0)),
                       pl.BlockSpec((1, 1, nk, SUBLANES, tk), lambda b, h, t, qi, kj: (b, h, 0, 0, 0))] + [ANY] * nc,
            scratch_shapes=[pltpu.VMEM((nk, tk, LANES), F32), pltpu.VMEM((tk, LANES), F32),
                            pltpu.VMEM((tk, LANES), F32), pltpu.VMEM((tk, LANES), F32),
                            pltpu.VMEM((nk, SUBLANES, tk), F32)] + sem_shapes),
        out_shape=[jax.ShapeDtypeStruct((bsz * seq, ATTN_WIDTH), BF16),
                   jax.ShapeDtypeStruct((bsz * seq, ATTN_WIDTH), BF16),
                   jax.ShapeDtypeStruct((bsz * seq, ATTN_WIDTH), BF16),
                   jax.ShapeDtypeStruct((bsz, HEAD_PAIRS, seq, LANES), F32),
                   jax.ShapeDtypeStruct((bsz, HEAD_PAIRS, nk, SUBLANES, tk), F32)] + landings,
        compiler_params=_params(("arbitrary", "arbitrary", "arbitrary")),
    )(qi_tab, kj_tab, qkv, qkv, qkv, do, o, lser, cumcol, *sends)
    return outs[:5], list(outs[5:])


def fox_bwd(qkv, do, o, lser, cumcol, rest, bf, bsz, seq, f_blk, carries=()):
    (dq, dk, dv, dcp, dcq), lands = fox_bwd_kernel(qkv, do, o, lser, cumcol, bsz, seq, carries)
    df, dbf = forget_bwd(dcq, dcp, rest, bf, bsz, seq, f_blk)
    return dq, dk, dv, df, dbf, lands


SCAN_STEPS = (1, 2, 4)
TAB_FWD = 0
TAB_BWD = 32
TAB_CARRY = 24
TAB_ROWS = 64


def _ssm_tile(seq):
    return 256 if seq >= 1024 else 64


def _scan_block(xr, xi, tab_ref, re, im, cr, ci, reverse):
    base = TAB_BWD if reverse else TAB_FWD
    for n, s in enumerate(SCAN_STEPS):
        ar = tab_ref[base + n * SUBLANES:base + (n + 1) * SUBLANES, re]
        ai = tab_ref[base + n * SUBLANES:base + (n + 1) * SUBLANES, im]
        shift = SUBLANES - s if reverse else s
        sr = pltpu.roll(xr, shift, 0)
        si = pltpu.roll(xi, shift, 0)
        xr, xi = xr + ar * sr - ai * si, xi + ar * si + ai * sr
    pr = tab_ref[base + TAB_CARRY:base + TAB_CARRY + SUBLANES, re]
    pi = tab_ref[base + TAB_CARRY:base + TAB_CARRY + SUBLANES, im]
    xr, xi = xr + pr * cr - pi * ci, xi + pr * ci + pi * cr
    return xr, xi


def ssm_fwd(rest, wb4, wc4, tabs, dskip, bsz, seq, u_blk):
    tt = _ssm_tile(seq)
    nt = seq // tt

    def body(u_ref, wb_ref, wc_ref, tab_ref, d_ref, y_ref, h_ref, carry):
        c = pl.program_id(1)

        @pl.when(c == 0)
        def _():
            carry[...] = jnp.zeros_like(carry)

        u = u_ref[...]
        ub = u.astype(BF16)
        for j in range(SSM_CHUNKS):
            h_ref[:, j * CHUNK_LANES:(j + 1) * CHUNK_LANES] = _dot(ub[:, j * LANES:(j + 1) * LANES], wb_ref[j])
        for j in range(SSM_CHUNKS):
            re = slice(j * CHUNK_LANES, j * CHUNK_LANES + CHUNK_STATES)
            im = slice(j * CHUNK_LANES + CHUNK_STATES, (j + 1) * CHUNK_LANES)

            def blk(bi, car):
                r0 = pl.multiple_of(bi * SUBLANES, SUBLANES)
                xr, xi = _scan_block(h_ref[pl.ds(r0, SUBLANES), re], h_ref[pl.ds(r0, SUBLANES), im],
                                     tab_ref, re, im, car[0], car[1], False)
                h_ref[pl.ds(r0, SUBLANES), re] = xr
                h_ref[pl.ds(r0, SUBLANES), im] = xi
                return xr[SUBLANES - 1:SUBLANES], xi[SUBLANES - 1:SUBLANES]

            cr, ci = lax.fori_loop(0, tt // SUBLANES, blk, (carry[0:1, re], carry[0:1, im]), unroll=2)
            carry[0:1, re] = cr
            carry[0:1, im] = ci
        for j in range(SSM_CHUNKS):
            hj = h_ref[:, j * CHUNK_LANES:(j + 1) * CHUNK_LANES].astype(BF16)
            cols = slice(j * LANES, (j + 1) * LANES)
            y_ref[:, cols] = _dot(hj, wc_ref[j]) + d_ref[:, cols] * u[:, cols]

    return pl.pallas_call(
        body, name="ssm_fwd", grid=(bsz, nt),
        in_specs=[pl.BlockSpec((tt, SSM_WIDTH), lambda b, c: (b * nt + c, u_blk)),
                  pl.BlockSpec((SSM_CHUNKS, LANES, CHUNK_LANES), lambda b, c: (0, 0, 0)),
                  pl.BlockSpec((SSM_CHUNKS, CHUNK_LANES, LANES), lambda b, c: (0, 0, 0)),
                  pl.BlockSpec((TAB_ROWS, STATE_LANES), lambda b, c: (0, 0)),
                  pl.BlockSpec((1, SSM_WIDTH), lambda b, c: (0, 0))],
        out_specs=[pl.BlockSpec((tt, SSM_WIDTH), lambda b, c: (b * nt + c, 0)),
                   pl.BlockSpec((tt, STATE_LANES), lambda b, c: (b * nt + c, 0))],
        out_shape=[jax.ShapeDtypeStruct((bsz * seq, SSM_WIDTH), F32),
                   jax.ShapeDtypeStruct((bsz * seq, STATE_LANES), F32)],
        scratch_shapes=[pltpu.VMEM((SUBLANES, STATE_LANES), F32)],
        compiler_params=_params(("parallel", "arbitrary")),
    )(rest, wb4, wc4, tabs, dskip)


def ssm_bwd(dys, rest, hs, wb4, wc4, tabs, dskip, bsz, seq, u_blk):
    tt = _ssm_tile(seq)
    nt = seq // tt
    nb = tt // SUBLANES

    def body(dy_ref, u_ref, h_ref, hp_ref, wb_ref, wc_ref, tab_ref, d_ref,
             du_ref, ga_ref, gwb_ref, gwc_ref, gd_ref, g_s, carry):
        b = pl.program_id(0)
        c = pl.program_id(1)

        @pl.when(c == 0)
        def _():
            carry[...] = jnp.zeros_like(carry)

        @pl.when((b == 0) & (c == 0))
        def _():
            ga_ref[...] = jnp.zeros_like(ga_ref)
            gwb_ref[...] = jnp.zeros_like(gwb_ref)
            gwc_ref[...] = jnp.zeros_like(gwc_ref)
            gd_ref[...] = jnp.zeros_like(gd_ref)

        dy = dy_ref[...].astype(F32)
        dyb = dy.astype(BF16)
        u = u_ref[...]
        ub = u.astype(BF16)
        first_chunk = c == nt - 1
        for j in range(SSM_CHUNKS):
            g_s[:, j * CHUNK_LANES:(j + 1) * CHUNK_LANES] = _dot(dyb[:, j * LANES:(j + 1) * LANES], wc_ref[j], NT)
        for j in range(SSM_CHUNKS):
            re = slice(j * CHUNK_LANES, j * CHUNK_LANES + CHUNK_STATES)
            im = slice(j * CHUNK_LANES + CHUNK_STATES, (j + 1) * CHUNK_LANES)
            row = lax.broadcasted_iota(jnp.int32, (SUBLANES, CHUNK_STATES), 0)

            def blk(n, car):
                bi = nb - 1 - n
                r0 = pl.multiple_of(bi * SUBLANES, SUBLANES)
                gr, gi = _scan_block(g_s[pl.ds(r0, SUBLANES), re], g_s[pl.ds(r0, SUBLANES), im],
                                     tab_ref, re, im, car[0], car[1], True)
                g_s[pl.ds(r0, SUBLANES), re] = gr
                g_s[pl.ds(r0, SUBLANES), im] = gi
                rp = pl.multiple_of(jnp.maximum(bi - 1, 0) * SUBLANES, SUBLANES)
                inside = bi > 0
                live = jnp.where(jnp.logical_or(inside, jnp.logical_not(first_chunk)), 1.0, 0.0)
                pr = jnp.where(inside, h_ref[pl.ds(rp, SUBLANES), re], hp_ref[:, re])[SUBLANES - 1:SUBLANES] * live
                pi = jnp.where(inside, h_ref[pl.ds(rp, SUBLANES), im], hp_ref[:, im])[SUBLANES - 1:SUBLANES] * live
                hr = jnp.where(row >= 1, pltpu.roll(h_ref[pl.ds(r0, SUBLANES), re], 1, 0), pr)
                hi = jnp.where(row >= 1, pltpu.roll(h_ref[pl.ds(r0, SUBLANES), im], 1, 0), pi)
                return (gr[0:1], gi[0:1], car[2] + gr * hr + gi * hi, car[3] + gi * hr - gr * hi)

            zero = jnp.zeros((SUBLANES, CHUNK_STATES), F32)
            cr, ci, sr, si = lax.fori_loop(0, nb, blk, (carry[0:1, re], carry[0:1, im], zero, zero), unroll=2)
            carry[0:1, re] = cr
            carry[0:1, im] = ci
            ga_ref[:, re] += sr
            ga_ref[:, im] += si
        for j in range(SSM_CHUNKS):
            cols = slice(j * LANES, (j + 1) * LANES)
            lanes = slice(j * CHUNK_LANES, (j + 1) * CHUNK_LANES)
            gj = g_s[:, lanes].astype(BF16)
            du_ref[:, cols] = (_dot(gj, wb_ref[j], NT) + d_ref[:, cols] * dy[:, cols]).astype(du_ref.dtype)
            gwb_ref[j] += _dot(ub[:, cols], gj, TN)
            gwc_ref[j] += _dot(h_ref[:, lanes].astype(BF16), dyb[:, cols], TN)
        gd_ref[...] += _fold8(dy * u)

    def prev_rows(b, c):
        chunk = nt - 1 - c
        return (jnp.maximum((b * nt + chunk) * nb - 1, 0), 0)

    return pl.pallas_call(
        body, name="ssm_bwd", grid=(bsz, nt),
        in_specs=[pl.BlockSpec((tt, SSM_WIDTH), lambda b, c: (b * nt + nt - 1 - c, 0)),
                  pl.BlockSpec((tt, SSM_WIDTH), lambda b, c: (b * nt + nt - 1 - c, u_blk)),
                  pl.BlockSpec((tt, STATE_LANES), lambda b, c: (b * nt + nt - 1 - c, 0)),
                  pl.BlockSpec((SUBLANES, STATE_LANES), prev_rows),
                  pl.BlockSpec((SSM_CHUNKS, LANES, CHUNK_LANES), lambda b, c: (0, 0, 0)),
                  pl.BlockSpec((SSM_CHUNKS, CHUNK_LANES, LANES), lambda b, c: (0, 0, 0)),
                  pl.BlockSpec((TAB_ROWS, STATE_LANES), lambda b, c: (0, 0)),
                  pl.BlockSpec((1, SSM_WIDTH), lambda b, c: (0, 0))],
        out_specs=[pl.BlockSpec((tt, SSM_WIDTH), lambda b, c: (b * nt + nt - 1 - c, 0)),
                   pl.BlockSpec((SUBLANES, STATE_LANES), lambda b, c: (0, 0)),
                   pl.BlockSpec((SSM_CHUNKS, LANES, CHUNK_LANES), lambda b, c: (0, 0, 0)),
                   pl.BlockSpec((SSM_CHUNKS, CHUNK_LANES, LANES), lambda b, c: (0, 0, 0)),
                   pl.BlockSpec((SUBLANES, SSM_WIDTH), lambda b, c: (0, 0))],
        out_shape=[jax.ShapeDtypeStruct((bsz * seq, SSM_WIDTH), BF16),
                   jax.ShapeDtypeStruct((SUBLANES, STATE_LANES), F32),
                   jax.ShapeDtypeStruct((SSM_CHUNKS, LANES, CHUNK_LANES), F32),
                   jax.ShapeDtypeStruct((SSM_CHUNKS, CHUNK_LANES, LANES), F32),
                   jax.ShapeDtypeStruct((SUBLANES, SSM_WIDTH), F32)],
        scratch_shapes=[pltpu.VMEM((tt, STATE_LANES), F32), pltpu.VMEM((SUBLANES, STATE_LANES), F32)],
        compiler_params=_params(("arbitrary", "arbitrary")),
    )(dys, rest, hs, hs, wb4, wc4, tabs, dskip)


def _gelu(v):
    t = jnp.tanh(GELU_C * (v + GELU_A * v * v * v))
    return 0.5 * v * (1.0 + t), t


def mix_fwd(ya, ys, rest, x0, wglu, bglu, wba, wbb, wout):
    t, d = x0.shape
    tm = _rows(t, 256)

    def body(ya_ref, ys_ref, ga_ref, gb_ref, x_ref, wg_ref, bg_ref, wa_ref, wb_ref, wo_ref,
             x1_ref, z_ref, pa_ref, pb_ref, yb_ref, yb2_ref, mx_ref):
        yb, _ = _gelu(ys_ref[...])
        ybb = yb.astype(BF16)
        z = _dot(ybb, wg_ref[...]) + bg_ref[...]
        yb2 = (yb * _sigmoid(z)).astype(BF16)
        pa = _dot(ya_ref[...], wa_ref[...])
        pb = _dot(yb2, wb_ref[...])
        mixed = (_sigmoid(ga_ref[...]) * pa + _sigmoid(gb_ref[...]) * pb).astype(BF16)
        x1_ref[...] = x_ref[...] + _dot(mixed, wo_ref[...])
        z_ref[...] = z.astype(z_ref.dtype)
        pa_ref[...] = pa.astype(pa_ref.dtype)
        pb_ref[...] = pb.astype(pb_ref.dtype)
        yb_ref[...] = ybb
        yb2_ref[...] = yb2
        mx_ref[...] = mixed

    row = lambda w: pl.BlockSpec((tm, w), lambda i: (i, 0))
    full = lambda a: pl.BlockSpec(a.shape, lambda i: (0,) * a.ndim)
    return pl.pallas_call(
        body, name="mix_fwd", grid=(t // tm,),
        in_specs=[row(ATTN_WIDTH), row(SSM_WIDTH),
                  pl.BlockSpec((tm, d), lambda i: (i, 0)), pl.BlockSpec((tm, d), lambda i: (i, 1)),
                  row(d), full(wglu), full(bglu), full(wba), full(wbb), full(wout)],
        out_specs=[row(d), row(SSM_WIDTH), row(d), row(d), row(SSM_WIDTH), row(SSM_WIDTH), row(d)],
        out_shape=[jax.ShapeDtypeStruct((t, d), F32), jax.ShapeDtypeStruct((t, SSM_WIDTH), BF16),
                   jax.ShapeDtypeStruct((t, d), BF16), jax.ShapeDtypeStruct((t, d), BF16),
                   jax.ShapeDtypeStruct((t, SSM_WIDTH), BF16), jax.ShapeDtypeStruct((t, SSM_WIDTH), BF16),
                   jax.ShapeDtypeStruct((t, d), BF16)],
        compiler_params=_params(("parallel",)),
    )(ya, ys, rest, rest, x0, wglu, bglu, wba, wbb, wout)


def mix_bwd(dx1, rest, pa, pb, z, ys, wglu, wba, wbb, wout):
    t, d = dx1.shape
    tm = _rows(t, 256)

    def body(dx_ref, ga_ref, gb_ref, pa_ref, pb_ref, z_ref, ys_ref, wg_ref, wa_ref, wb_ref, wo_ref,
             dya_ref, dys_ref, dg_ref, dpa_ref, dpb_ref, dz_ref, dbg_ref):
        @pl.when(pl.program_id(0) == 0)
        def _():
            dbg_ref[...] = jnp.zeros_like(dbg_ref)

        dmix = _dot(dx_ref[...].astype(BF16), wo_ref[...], NT)
        sa = _sigmoid(ga_ref[...])
        sb = _sigmoid(gb_ref[...])
        dpa = (dmix * sa).astype(BF16)
        dpb = (dmix * sb).astype(BF16)
        dg_ref[:, 0:d] = (dmix * pa_ref[...].astype(F32) * sa * (1.0 - sa)).astype(dg_ref.dtype)
        dg_ref[:, d:2 * d] = (dmix * pb_ref[...].astype(F32) * sb * (1.0 - sb)).astype(dg_ref.dtype)
        dpa_ref[...] = dpa
        dpb_ref[...] = dpb
        dya_ref[...] = _dot(dpa, wa_ref[...], NT).astype(dya_ref.dtype)
        dyb2 = _dot(dpb, wb_ref[...], NT)
        ys = ys_ref[...]
        yb, th = _gelu(ys)
        sg = _sigmoid(z_ref[...].astype(F32))
        dz = dyb2 * yb * sg * (1.0 - sg)
        dzb = dz.astype(BF16)
        dz_ref[...] = dzb
        dbg_ref[...] += _fold8(dz)
        dyb = dyb2 * sg + _dot(dzb, wg_ref[...], NT)
        dgelu = 0.5 * (1.0 + th) + 0.5 * ys * (1.0 - th * th) * GELU_C * (1.0 + 3.0 * GELU_A * ys * ys)
        dys_ref[...] = (dyb * dgelu).astype(dys_ref.dtype)

    row = lambda w: pl.BlockSpec((tm, w), lambda i: (i, 0))
    full = lambda a: pl.BlockSpec(a.shape, lambda i: (0,) * a.ndim)
    return pl.pallas_call(
        body, name="mix_bwd", grid=(t // tm,),
        in_specs=[row(d), pl.BlockSpec((tm, d), lambda i: (i, 0)), pl.BlockSpec((tm, d), lambda i: (i, 1)),
                  row(d), row(d), row(SSM_WIDTH), row(SSM_WIDTH), full(wglu), full(wba), full(wbb), full(wout)],
        out_specs=[row(ATTN_WIDTH), row(SSM_WIDTH), row(2 * d), row(d), row(d), row(SSM_WIDTH),
                   pl.BlockSpec((SUBLANES, SSM_WIDTH), lambda i: (0, 0))],
        out_shape=[jax.ShapeDtypeStruct((t, ATTN_WIDTH), BF16), jax.ShapeDtypeStruct((t, SSM_WIDTH), BF16),
                   jax.ShapeDtypeStruct((t, 2 * d), BF16), jax.ShapeDtypeStruct((t, d), BF16),
                   jax.ShapeDtypeStruct((t, d), BF16), jax.ShapeDtypeStruct((t, SSM_WIDTH), BF16),
                   jax.ShapeDtypeStruct((SUBLANES, SSM_WIDTH), F32)],
        compiler_params=_params(("arbitrary",)),
    )(dx1, rest, rest, pa, pb, z, ys, wglu, wba, wbb, wout)


def mlp_fwd(x1, g, wup, wdown):
    t, d = x1.shape
    ff = wup.shape[1]
    tm, tf = _rows(t, 1024), _pick(ff, 1024)
    nf = ff // tf

    def body(x_ref, g_ref, wu_ref, wd_ref, x2_ref, up_ref, h_s, acc_s):
        f = pl.program_id(1)

        @pl.when(f == 0)
        def _():
            xv = x_ref[...]
            h_s[...] = (xv * _rms_scale(xv) * g_ref[...]).astype(BF16)
            acc_s[...] = jnp.zeros_like(acc_s)

        up = _dot(h_s[...], wu_ref[...])
        up_ref[...] = up.astype(up_ref.dtype)
        act = jnp.square(jnp.maximum(up, 0.0)).astype(BF16)
        acc_s[...] += _dot(act, wd_ref[...])

        @pl.when(f == nf - 1)
        def _():
            x2_ref[...] = x_ref[...] + acc_s[...]

    return pl.pallas_call(
        body, name="mlp_fwd", grid=(t // tm, nf),
        in_specs=[pl.BlockSpec((tm, d), lambda i, f: (i, 0)), pl.BlockSpec((1, d), lambda i, f: (0, 0)),
                  pl.BlockSpec((d, tf), lambda i, f: (0, f)), pl.BlockSpec((tf, d), lambda i, f: (f, 0))],
        out_specs=[pl.BlockSpec((tm, d), lambda i, f: (i, 0)), pl.BlockSpec((tm, tf), lambda i, f: (i, f))],
        out_shape=[jax.ShapeDtypeStruct((t, d), F32), jax.ShapeDtypeStruct((t, ff), BF16)],
        scratch_shapes=[pltpu.VMEM((tm, d), BF16), pltpu.VMEM((tm, d), F32)],
        compiler_params=_params(("parallel", "arbitrary")),
    )(x1, g, wup, wdown)


def mlp_bwd(dx2, up, x1, g, wup, wdown):
    t, d = x1.shape
    ff = wup.shape[1]
    tm, tf = _rows(t, 512), _pick(ff, 1024)
    nf = ff // tf

    def body(dx_ref, up_ref, x_ref, g_ref, wu_ref, wd_ref, dup_ref, dx1_ref, dg_ref, dxb_s, acc_s):
        i = pl.program_id(0)
        f = pl.program_id(1)

        @pl.when((i == 0) & (f == 0))
        def _():
            dg_ref[...] = jnp.zeros_like(dg_ref)

        @pl.when(f == 0)
        def _():
            dxb_s[...] = dx_ref[...].astype(BF16)
            acc_s[...] = jnp.zeros_like(acc_s)

        dact = _dot(dxb_s[...], wd_ref[...], NT)
        dup = (dact * 2.0 * jnp.maximum(up_ref[...].astype(F32), 0.0)).astype(BF16)
        dup_ref[...] = dup
        acc_s[...] += _dot(dup, wu_ref[...], NT)

        @pl.when(f == nf - 1)
        def _():
            dxn, dgain = _rms_bwd(x_ref[...], g_ref[...], acc_s[...])
            dx1_ref[...] = dx_ref[...] + dxn
            dg_ref[...] += _fold8(dgain)

    return pl.pallas_call(
        body, name="mlp_bwd", grid=(t // tm, nf),
        in_specs=[pl.BlockSpec((tm, d), lambda i, f: (i, 0)), pl.BlockSpec((tm, tf), lambda i, f: (i, f)),
                  pl.BlockSpec((tm, d), lambda i, f: (i, 0)), pl.BlockSpec((1, d), lambda i, f: (0, 0)),
                  pl.BlockSpec((d, tf), lambda i, f: (0, f)), pl.BlockSpec((tf, d), lambda i, f: (f, 0))],
        out_specs=[pl.BlockSpec((tm, tf), lambda i, f: (i, f)), pl.BlockSpec((tm, d), lambda i, f: (i, 0)),
                   pl.BlockSpec((SUBLANES, d), lambda i, f: (0, 0))],
        out_shape=[jax.ShapeDtypeStruct((t, ff), BF16), jax.ShapeDtypeStruct((t, d), F32),
                   jax.ShapeDtypeStruct((SUBLANES, d), F32)],
        scratch_shapes=[pltpu.VMEM((tm, d), BF16), pltpu.VMEM((tm, d), F32)],
        compiler_params=_params(("arbitrary", "arbitrary")),
    )(dx2, up, x1, g, wup, wdown)


def proj_bwd(dproj, wpad, x0, dx1, g):
    t, d = x0.shape
    m = wpad.shape[1]
    tm = _rows(t, 256)

    def body(dp_ref, w_ref, x_ref, dx1_ref, g_ref, dx0_ref, dg_ref):
        @pl.when(pl.program_id(0) == 0)
        def _():
            dg_ref[...] = jnp.zeros_like(dg_ref)

        dh = _dot(dp_ref[...], w_ref[...], NT)
        dxn, dgain = _rms_bwd(x_ref[...], g_ref[...], dh)
        dx0_ref[...] = dx1_ref[...] + dxn
        dg_ref[...] += _fold8(dgain)

    return pl.pallas_call(
        body, name="proj_bwd", grid=(t // tm,),
        in_specs=[pl.BlockSpec((tm, m), lambda i: (i, 0)), pl.BlockSpec((d, m), lambda i: (0, 0)),
                  pl.BlockSpec((tm, d), lambda i: (i, 0)), pl.BlockSpec((tm, d), lambda i: (i, 0)),
                  pl.BlockSpec((1, d), lambda i: (0, 0))],
        out_specs=[pl.BlockSpec((tm, d), lambda i: (i, 0)), pl.BlockSpec((SUBLANES, d), lambda i: (0, 0))],
        out_shape=[jax.ShapeDtypeStruct((t, d), F32), jax.ShapeDtypeStruct((SUBLANES, d), F32)],
        compiler_params=_params(("arbitrary",)),
    )(dproj, wpad, x0, dx1, g)


def final_loss(x, g, target):
    t, d = x.shape
    tm = _rows(t, 512)

    def body(x_ref, g_ref, t_ref, dx_ref, ls_ref, dg_ref):
        @pl.when(pl.program_id(0) == 0)
        def _():
            ls_ref[...] = jnp.zeros_like(ls_ref)
            dg_ref[...] = jnp.zeros_like(dg_ref)

        xv = x_ref[...]
        gv = g_ref[...]
        err = xv * _rms_scale(xv) * gv - t_ref[...]
        ls_ref[...] += _fold8(err * err) * (0.5 / d)
        dxn, dgain = _rms_bwd(xv, gv, err * (1.0 / d))
        dx_ref[...] = dxn
        dg_ref[...] += _fold8(dgain)

    return pl.pallas_call(
        body, name="final_loss", grid=(t // tm,),
        in_specs=[pl.BlockSpec((tm, d), lambda i: (i, 0)), pl.BlockSpec((1, d), lambda i: (0, 0)),
                  pl.BlockSpec((tm, d), lambda i: (i, 0))],
        out_specs=[pl.BlockSpec((tm, d), lambda i: (i, 0)), pl.BlockSpec((SUBLANES, d), lambda i: (0, 0)),
                   pl.BlockSpec((SUBLANES, d), lambda i: (0, 0))],
        out_shape=[jax.ShapeDtypeStruct((t, d), F32), jax.ShapeDtypeStruct((SUBLANES, d), F32),
                   jax.ShapeDtypeStruct((SUBLANES, d), F32)],
        compiler_params=_params(("arbitrary",)),
    )(x, g, target)


def _peers():
    x, y, c = lax.axis_index("x"), lax.axis_index("y"), lax.axis_index("c")
    peers = []
    for m in range(1, 8):
        fx, fy, fc = (m >> 2) & 1, (m >> 1) & 1, m & 1
        peers.append((m, ((1 - x) if fx else x, (1 - y) if fy else y, (1 - c) if fc else c)))
    return x, y, c, peers


def _remote(src, dst, sems, k, peer):
    return pltpu.make_async_remote_copy(src_ref=src, dst_ref=dst, send_sem=sems[0].at[k], recv_sem=sems[1].at[k],
                                        device_id=peer, device_id_type=MESH)


def _gather_copies(w_refs, out_refs, sems, first):
    x, y, c, peers = _peers()
    chip = 2 * x + y
    cps = []
    for t, (w, o) in enumerate(zip(w_refs, out_refs)):
        half = w.shape[0] // 2
        rows = pl.ds(c * half, half)
        for m, peer in peers:
            if m >> 1:
                cps.append(_remote(w.at[rows], o.at[chip, rows], sems, first + 7 * t + m - 1, peer))
    return cps


def _reduce_copies(g_refs, out_refs, sems, first):
    x, y, c, peers = _peers()
    me = 4 * x + 2 * y + c
    cps = []
    for t, (g, o) in enumerate(zip(g_refs, out_refs)):
        for m, (px, py, pc) in peers:
            cps.append(_remote(g.at[2 * px + py], o.at[me], sems, first + 7 * t + m - 1, (px, py, pc)))
    return cps


def _all_copies(s_refs, out_refs, sems, first):
    x, y, c, peers = _peers()
    me = 4 * x + 2 * y + c
    return [_remote(s, o.at[me], sems, first + 7 * t + m - 1, peer)
            for t, (s, o) in enumerate(zip(s_refs, out_refs)) for m, peer in peers]


EXCHANGES = {"gather": (_gather_copies, lambda a: (4,) + a.shape),
             "reduce": (_reduce_copies, lambda a: (8,) + a.shape[1:]),
             "all": (_all_copies, lambda a: (8,) + a.shape)}


def _carry_plan(carries):
    inputs, shapes, spans = [], [], []
    for kind, arrs in carries:
        for a in arrs:
            inputs.append(a)
            shapes.append(jax.ShapeDtypeStruct(EXCHANGES[kind][1](a), a.dtype))
        spans.append((kind, len(arrs)))

    def build(in_refs, out_refs, sems):
        cps, pos = [], 0
        for kind, cnt in spans:
            cps += EXCHANGES[kind][0](in_refs[pos:pos + cnt], out_refs[pos:pos + cnt], sems, 7 * pos)
            pos += cnt
        return cps

    return inputs, shapes, 7 * len(inputs), build


def exchange(carries, name):
    inputs, shapes, n_sems, build = _carry_plan(carries)
    n = len(inputs)

    def body(*refs):
        cps = build(refs[:n], refs[n:2 * n], refs[2 * n:])
        for cp in cps:
            cp.start()
        for cp in cps:
            cp.wait()

    return pl.pallas_call(
        body, name=name, in_specs=[ANY] * n, out_specs=[ANY] * n, out_shape=shapes,
        scratch_shapes=[pltpu.SemaphoreType.DMA((n_sems,)), pltpu.SemaphoreType.DMA((n_sems,))],
    )(*inputs)


def _adamw_math(w, g, m, v):
    m = ADAM_B1 * m + (1.0 - ADAM_B1) * g
    v = ADAM_B2 * v + (1.0 - ADAM_B2) * (g * g)
    m_hat = m / (1.0 - ADAM_B1 ** ADAM_STEP)
    v_hat = v / (1.0 - ADAM_B2 ** ADAM_STEP)
    delta = -ADAM_LR * (m_hat / (jnp.sqrt(v_hat) + ADAM_EPS) + ADAM_WD * w)
    return delta, m, v


def adamw_layer(l, w, m, v, parts, own, device, chip, bufs, name):
    _, r, cdim = w.shape
    tr = _rows(r, PACK_ROW_TILE)

    def body(dev_ref, chip_ref, w_ref, m_ref, v_ref, p_ref, o_ref, *rest):
        g_ref, d_ref, nm_ref, nv_ref = rest[-4:]
        g = None
        for dev in range(8):
            part = jnp.where(dev_ref[0] == dev, o_ref[0], p_ref[dev]).astype(F32)
            g = part if g is None else g + part
        d, nm, nv = _adamw_math(w_ref[0], g, m_ref[0], v_ref[0])
        g_ref[0] = g
        d_ref[0] = d
        nm_ref[0] = nm
        nv_ref[0] = nv

    lay = pl.BlockSpec((1, tr, cdim), lambda i, dev_ref, chip_ref: (l, i, 0))
    in_specs = [lay, lay, lay, pl.BlockSpec((8, tr, cdim), lambda i, dev_ref, chip_ref: (0, i, 0)),
                pl.BlockSpec((1, tr, cdim), lambda i, dev_ref, chip_ref: (chip_ref[0], i, 0))]
    args = [device, chip, w, m, v, parts, own]
    aliases = {}
    if bufs is not None:
        in_specs += [ANY] * 4
        aliases = {len(args) + k: k for k in range(4)}
        args += list(bufs)
    return pl.pallas_call(
        body, name=name,
        grid_spec=pltpu.PrefetchScalarGridSpec(num_scalar_prefetch=2, grid=(r // tr,), in_specs=in_specs,
                                               out_specs=[lay] * 4),
        out_shape=[jax.ShapeDtypeStruct(w.shape, F32)] * 4,
        input_output_aliases=aliases,
        compiler_params=_params(("parallel",)),
    )(*args)


def sum_and_adamw(parts, own, me, w, m, v):
    _, r, cdim = parts.shape
    tr = _rows(r, PACK_ROW_TILE)

    def body(me_ref, p_ref, o_ref, w_ref, m_ref, v_ref, g_ref, d_ref, nm_ref, nv_ref):
        part = lambda k: jnp.where(me_ref[0] == k, o_ref[...], p_ref[k])
        g = part(0)
        for k in range(1, 8):
            g = g + part(k)
        d, nm, nv = _adamw_math(w_ref[...], g, m_ref[...], v_ref[...])
        g_ref[...] = g
        d_ref[...] = d
        nm_ref[...] = nm
        nv_ref[...] = nv

    spec = pl.BlockSpec((tr, cdim), lambda i, me_ref: (i, 0))
    return pl.pallas_call(
        body, name="sum_and_adamw",
        grid_spec=pltpu.PrefetchScalarGridSpec(
            num_scalar_prefetch=1, grid=(r // tr,),
            in_specs=[pl.BlockSpec((8, tr, cdim), lambda i, me_ref: (0, i, 0)), spec, spec, spec, spec],
            out_specs=[spec] * 4),
        out_shape=[jax.ShapeDtypeStruct((r, cdim), F32)] * 4,
        compiler_params=_params(("parallel",)),
    )(me, parts, own, w, m, v)


SHARDED = ("w_in", "w_glu", "w_branch_a", "w_branch_b", "w_out", "w_mlp_up", "w_mlp_down")
SHARD_AXIS = {"w_in": 2, "w_glu": 1, "w_branch_a": 2, "w_branch_b": 2, "w_out": 1, "w_mlp_up": 2, "w_mlp_down": 1}
SMALL = ("norm_mix", "b_forget", "ssm_lambda_re", "ssm_lambda_im", "ssm_log_dt", "ssm_b_re", "ssm_b_im",
         "ssm_c_re", "ssm_c_im", "ssm_d", "b_glu", "norm_mlp", "norm_final")


def pack_flat(arrs):
    flat = jnp.concatenate([a.reshape(-1).astype(F32) for a in arrs])
    unit = PACK_COLS * PACK_ROW_TILE
    rows = (flat.shape[0] + unit - 1) // unit * PACK_ROW_TILE
    return jnp.pad(flat, (0, rows * PACK_COLS - flat.shape[0])).reshape(rows, PACK_COLS)


def unpack_flat(packed, shapes):
    flat = packed.reshape(-1)
    out, off = [], 0
    for s in shapes:
        n = math.prod(s)
        out.append(flat[off:off + n].reshape(tuple(s)))
        off += n
    return out


def _discretise(lam_re, lam_im, log_dt, b_re, b_im):
    dt = jnp.exp(log_dt)[:, None]
    mag = jnp.exp(lam_re * dt)
    ar = mag * jnp.cos(lam_im * dt)
    ai = mag * jnp.sin(lam_im * dt)
    den = lam_re * lam_re + lam_im * lam_im
    cr = ((ar - 1.0) * lam_re + ai * lam_im) / den
    ci = (ai * lam_re - (ar - 1.0) * lam_im) / den
    bbr = cr[:, :, None] * b_re - ci[:, :, None] * b_im
    bbi = cr[:, :, None] * b_im + ci[:, :, None] * b_re
    return ar, ai, bbr, bbi


def _state_lanes(re, im):
    return jnp.concatenate([re.reshape(SSM_CHUNKS, CHUNK_STATES), im.reshape(SSM_CHUNKS, CHUNK_STATES)],
                           axis=1).reshape(STATE_LANES)


def _ssm_inputs(ar, ai, bbr, bbi, c_re, c_im):
    pr, pi = [ar], [ai]
    for _ in range(SUBLANES - 1):
        pr, pi = pr + [pr[-1] * ar - pi[-1] * ai], pi + [pr[-1] * ai + pi[-1] * ar]
    power = lambda n, sign: _state_lanes(pr[n - 1], sign * pi[n - 1])
    idx = jnp.arange(SUBLANES)
    masked = lambda n, sign, keep: jnp.where(keep[:, None], power(n, sign)[None, :], 0.0)
    tabs = jnp.concatenate(
        [masked(s, 1.0, idx >= s) for s in SCAN_STEPS]
        + [jnp.stack([power(i + 1, 1.0) for i in range(SUBLANES)])]
        + [masked(s, -1.0, idx < SUBLANES - s) for s in SCAN_STEPS]
        + [jnp.stack([power(SUBLANES - i, -1.0) for i in range(SUBLANES)])], axis=0)
    eye = jnp.eye(CHUNK_GROUPS, dtype=F32)

    def to_wb(bb):
        t = bb.reshape(SSM_CHUNKS, CHUNK_GROUPS, SSM_STATE, SSM_GROUP_CH).transpose(0, 1, 3, 2)
        return jnp.einsum("jgcp,gh->jgchp", t, eye).reshape(SSM_CHUNKS, LANES, CHUNK_STATES)

    def to_wc(cc):
        t = cc.reshape(SSM_CHUNKS, CHUNK_GROUPS, SSM_GROUP_CH, SSM_STATE)
        return jnp.einsum("jgcp,gh->jhpgc", t, eye).reshape(SSM_CHUNKS, CHUNK_STATES, LANES)

    wb4 = jnp.concatenate([to_wb(bbr), to_wb(bbi)], axis=2).astype(BF16)
    wc4 = jnp.concatenate([to_wc(c_re), -to_wc(c_im)], axis=1).astype(BF16)
    return tabs, wb4, wc4


def _ssm_param_grads(ga8, gwb, gwc):
    eye = jnp.eye(CHUNK_GROUPS, dtype=F32)
    ga = jnp.sum(ga8, axis=0).reshape(SSM_CHUNKS, 2, CHUNK_STATES)
    gar = ga[:, 0].reshape(SSM_GROUPS, SSM_STATE)
    gai = ga[:, 1].reshape(SSM_GROUPS, SSM_STATE)

    def from_wb(g):
        t = g.reshape(SSM_CHUNKS, CHUNK_GROUPS, SSM_GROUP_CH, CHUNK_GROUPS, SSM_STATE)
        return jnp.einsum("jgchp,gh->jgpc", t, eye).reshape(SSM_GROUPS, SSM_STATE, SSM_GROUP_CH)

    def from_wc(g):
        t = g.reshape(SSM_CHUNKS, CHUNK_GROUPS, SSM_STATE, CHUNK_GROUPS, SSM_GROUP_CH)
        return jnp.einsum("jhpgc,gh->jgcp", t, eye).reshape(SSM_GROUPS, SSM_GROUP_CH, SSM_STATE)

    return (gar, gai, from_wb(gwb[:, :, :CHUNK_STATES]), from_wb(gwb[:, :, CHUNK_STATES:]),
            from_wc(gwc[:, :CHUNK_STATES]), -from_wc(gwc[:, CHUNK_STATES:]))


def kernel(x, norm_mix, w_in, b_forget, ssm_lambda_re, ssm_lambda_im, ssm_log_dt, ssm_b_re, ssm_b_im, ssm_c_re, ssm_c_im, ssm_d, w_glu, b_glu, w_branch_a, w_branch_b, w_out, norm_mlp, w_mlp_up, w_mlp_down, norm_final, loss_target, m_norm_mix, m_w_in, m_b_forget, m_ssm_lambda_re, m_ssm_lambda_im, m_ssm_log_dt, m_ssm_b_re, m_ssm_b_im, m_ssm_c_re, m_ssm_c_im, m_ssm_d, m_w_glu, m_b_glu, m_w_branch_a, m_w_branch_b, m_w_out, m_norm_mlp, m_w_mlp_up, m_w_mlp_down, m_norm_final, v_norm_mix, v_w_in, v_b_forget, v_ssm_lambda_re, v_ssm_lambda_im, v_ssm_log_dt, v_ssm_b_re, v_ssm_b_im, v_ssm_c_re, v_ssm_c_im, v_ssm_d, v_w_glu, v_b_glu, v_w_branch_a, v_w_branch_b, v_w_out, v_norm_mlp, v_w_mlp_up, v_w_mlp_down, v_norm_final):
    args = dict(locals())
    bsz, seq, d = x.shape
    nl = norm_mix.shape[0]
    tokens = bsz * seq
    aw, sw = ATTN_WIDTH, SSM_WIDTH
    chip = (2 * lax.axis_index("x") + lax.axis_index("y")).astype(jnp.int32)
    chip_id = chip.reshape(1)
    device_id = (2 * chip + lax.axis_index("c").astype(jnp.int32)).reshape(1)

    own = {n: args[n].astype(BF16) for n in SHARDED}
    late = [n for n in SHARDED if n != "w_in"]
    o_f, o_u, o_ga, o_gb = 3 * aw, 3 * aw + ATTN_HEADS, 3 * aw + ATTN_HEADS + sw, 3 * aw + ATTN_HEADS + sw + d
    u_blk = 2 * d // sw
    f_blk = (2 * d + sw) // F_PAD
    bf_pad = jnp.pad(b_forget, ((0, 0), (0, F_PAD - ATTN_HEADS)))

    def assemble(l, names, gathered):
        return {n: jnp.concatenate([jnp.where(chip == k, own[n][l], g[k]) for k in range(4)],
                                   axis=SHARD_AXIS[n] - 1) for n, g in zip(names, gathered)}

    def split_w_in(win):
        w = {"w_qkv": win[:, :o_f],
             "w_rest": jnp.concatenate([win[:, o_ga:o_gb], win[:, o_gb:], win[:, o_u:o_ga],
                                        jnp.pad(win[:, o_f:o_u], ((0, 0), (0, F_PAD - ATTN_HEADS)))], axis=1)}
        w["w_pad"] = jnp.concatenate([w["w_qkv"], w["w_rest"]], axis=1)
        return w

    disc = [jax.vjp(_discretise, ssm_lambda_re[l], ssm_lambda_im[l], ssm_log_dt[l], ssm_b_re[l], ssm_b_im[l])
            for l in range(nl)]

    g_in = exchange([("gather", [own["w_in"][0]])], "gather_first")
    xs = x.reshape(tokens, d)
    saved, weights = [], []
    for l in range(nl):
        w = split_w_in(assemble(l, ["w_in"], g_in)["w_in"])
        g1 = norm_mix[l].reshape(1, d)
        qkv = norm_matmul(xs, g1, w["w_qkv"], BF16, "proj_qkv")
        rest = norm_matmul(xs, g1, w["w_rest"], F32, "proj_rest")
        cumcol, cumrow = forget_cumsum(rest, bf_pad[l:l + 1], bsz, seq, f_blk)
        carries = [("gather", [own[n][l] for n in late] + ([own["w_in"][l + 1]] if l + 1 < nl else []))]
        ya, lser, lands = fox_fwd(qkv, cumrow, bsz, seq, carries)
        w.update(assemble(l, late, lands))
        g_in = lands[len(late):]
        tabs, wb4, wc4 = _ssm_inputs(*disc[l][0], ssm_c_re[l], ssm_c_im[l])
        dskip = ssm_d[l].reshape(1, sw)
        ys, hs = ssm_fwd(rest, wb4, wc4, tabs, dskip, bsz, seq, u_blk)
        x1, z, pa, pb, yb, yb2, mixed = mix_fwd(ya, ys, rest, xs, w["w_glu"], b_glu[l].reshape(1, sw),
                                                 w["w_branch_a"], w["w_branch_b"], w["w_out"])
        x2, up = mlp_fwd(x1, norm_mlp[l].reshape(1, d), w["w_mlp_up"], w["w_mlp_down"])
        saved.append(dict(x0=xs, qkv=qkv, rest=rest, cumcol=cumcol, ya=ya, lser=lser,
                          tabs=tabs, wb4=wb4, wc4=wc4, dskip=dskip, ys=ys, hs=hs, x1=x1, z=z, pa=pa, pb=pb,
                          yb=yb, yb2=yb2, mixed=mixed, up=up))
        weights.append(w)
        xs = x2
    dx, loss_rows, dgf_rows = final_loss(xs, norm_final.reshape(1, d), loss_target.reshape(tokens, d))
    loss = lax.psum(jnp.sum(loss_rows), ("x", "y", "c"))

    early = [n for n in SHARDED if n != "w_in"]
    big = {n: [None] * nl for n in SHARDED}
    parts = {n: [None] * nl for n in SHARDED}
    small = {n: [None] * nl for n in SMALL if n != "norm_final"}
    for l in reversed(range(nl)):
        s, w = saved[l], weights[l]
        g2 = norm_mlp[l].reshape(1, d)
        dup, dx1, dg2 = mlp_bwd(dx, s["up"], s["x1"], g2, w["w_mlp_up"], w["w_mlp_down"])
        big["w_mlp_down"][l] = matmul_tn(s["up"], dx, "grad_w_mlp_down", a_kind="relu2", shard_axis=0, out_dtype=BF16)
        big["w_mlp_up"][l] = matmul_tn(s["x1"], dup, "grad_w_mlp_up", a_kind="norm", gain=g2, shard_axis=1,
                                       out_dtype=BF16)
        small["norm_mlp"][l] = jnp.sum(dg2, axis=0)
        dya, dys, dgab, dpa, dpb, dz, dbg = mix_bwd(dx1, s["rest"], s["pa"], s["pb"], s["z"], s["ys"],
                                                    w["w_glu"], w["w_branch_a"], w["w_branch_b"], w["w_out"])
        big["w_out"][l] = matmul_tn(s["mixed"], dx1, "grad_w_out", shard_axis=0, out_dtype=BF16)
        big["w_branch_a"][l] = matmul_tn(s["ya"], dpa, "grad_w_branch_a", shard_axis=1, out_dtype=BF16)
        big["w_branch_b"][l] = matmul_tn(s["yb2"], dpb, "grad_w_branch_b", shard_axis=1, out_dtype=BF16)
        big["w_glu"][l] = matmul_tn(s["yb"], dz, "grad_w_glu", shard_axis=0, out_dtype=BF16)
        small["b_glu"][l] = jnp.sum(dbg, axis=0)
        du, ga8, gwb, gwc, gd8 = ssm_bwd(dys, s["rest"], s["hs"], s["wb4"], s["wc4"], s["tabs"], s["dskip"],
                                         bsz, seq, u_blk)
        gar, gai, gbbr, gbbi, gcr, gci = _ssm_param_grads(ga8, gwb, gwc)
        glr, gli, gdt, gbr, gbi = disc[l][1]((gar, gai, gbbr, gbbi))
        small["ssm_lambda_re"][l], small["ssm_lambda_im"][l], small["ssm_log_dt"][l] = glr, gli, gdt
        small["ssm_b_re"][l], small["ssm_b_im"][l] = gbr, gbi
        small["ssm_c_re"][l], small["ssm_c_im"][l] = gcr, gci
        small["ssm_d"][l] = jnp.sum(gd8, axis=0)
        carries = [("reduce", [big[n][l] for n in early])]
        if l + 1 < nl:
            carries.append(("reduce", [big["w_in"][l + 1]]))
        dq, dk, dv, df, dbf, lands = fox_bwd(s["qkv"], dya, s["ya"], s["lser"], s["cumcol"],
                                             s["rest"], bf_pad[l:l + 1], bsz, seq, f_blk, carries)
        for n, p in zip(early, lands):
            parts[n][l] = p
        if l + 1 < nl:
            parts["w_in"][l + 1] = lands[len(early)]
        small["b_forget"][l] = jnp.sum(dbf, axis=0)[:ATTN_HEADS]
        dproj = jnp.concatenate([dq, dk, dv, dgab, du, df], axis=1)
        g1 = norm_mix[l].reshape(1, d)
        dwp = matmul_tn(s["x0"], dproj, "grad_w_in", a_kind="norm", gain=g1)
        big["w_in"][l] = jnp.stack(jnp.split(jnp.concatenate(
            [dwp[:, :o_f], dwp[:, o_f + 2 * d + sw:o_f + 2 * d + sw + ATTN_HEADS],
             dwp[:, o_f + 2 * d:o_f + 2 * d + sw], dwp[:, o_f:o_f + 2 * d]], axis=1), 4, axis=1)).astype(BF16)
        dx, dg1 = proj_bwd(dproj, w["w_pad"], s["x0"], dx1, g1)
        small["norm_mix"][l] = jnp.sum(dg1, axis=0)
    grad_x = dx.reshape(bsz, seq, d)

    small_g = [jnp.stack(small[n]) if n != "norm_final" else jnp.sum(dgf_rows, axis=0) for n in SMALL]
    small_packed = pack_flat(small_g)
    parts["w_in"][0], small_parts = exchange([("reduce", [big["w_in"][0]]), ("all", [small_packed])], "exchange_last")

    out_g, out_d, out_m, out_v = {}, {}, {}, {}
    for n in SHARDED:
        bufs = None
        for l in range(nl):
            bufs = adamw_layer(l, args[n], args["m_" + n], args["v_" + n], parts[n][l], big[n][l], device_id, chip_id,
                               bufs, "adamw_" + n)
        out_g[n], out_d[n], out_m[n], out_v[n] = bufs

    small_shapes = [args[n].shape for n in SMALL]
    sg, sd, sm, sv = sum_and_adamw(small_parts, small_packed, device_id,
                                   pack_flat([args[n] for n in SMALL]),
                                   pack_flat([args["m_" + n] for n in SMALL]),
                                   pack_flat([args["v_" + n] for n in SMALL]))
    for res, packed in ((out_g, sg), (out_d, sd), (out_m, sm), (out_v, sv)):
        res.update(zip(SMALL, unpack_flat(packed, small_shapes)))

    order = ("norm_mix", "w_in", "b_forget", "ssm_lambda_re", "ssm_lambda_im", "ssm_log_dt", "ssm_b_re",
             "ssm_b_im", "ssm_c_re", "ssm_c_im", "ssm_d", "w_glu", "b_glu", "w_branch_a", "w_branch_b", "w_out",
             "norm_mlp", "w_mlp_up", "w_mlp_down", "norm_final")
    return (loss, grad_x, *[out_g[n] for n in order], *[out_d[n] for n in order],
            *[out_m[n] for n in order], *[out_v[n] for n in order])
```

```python
import math

import jax
import jax.numpy as jnp
from jax import lax
from jax.experimental import pallas as pl
from jax.experimental.pallas import tpu as pltpu

F32 = jnp.float32
BF16 = jnp.bfloat16
MESH = pl.DeviceIdType.MESH
ANY = pl.BlockSpec(memory_space=pl.ANY)

ATTN_HEADS = 8
HEAD_DIM = 64
ATTN_WIDTH = ATTN_HEADS * HEAD_DIM
HEAD_PAIRS = ATTN_HEADS // 2
SSM_GROUPS = 32
SSM_GROUP_CH = 16
SSM_STATE = 64
SSM_WIDTH = SSM_GROUPS * SSM_GROUP_CH
LANES = 128
SUBLANES = 8
SSM_CHUNKS = SSM_WIDTH // LANES
CHUNK_GROUPS = SSM_GROUPS // SSM_CHUNKS
CHUNK_STATES = CHUNK_GROUPS * SSM_STATE
CHUNK_LANES = 2 * CHUNK_STATES
STATE_LANES = SSM_CHUNKS * CHUNK_LANES
F_PAD = LANES
RMS_EPS = 1e-6
ADAM_LR = 0.001
ADAM_B1 = 0.9
ADAM_B2 = 0.999
ADAM_EPS = 1e-08
ADAM_WD = 0.01
ADAM_STEP = 10
PACK_COLS = 1024
PACK_ROW_TILE = 256
VMEM_LIMIT = 52 * 1024 * 1024
NEG_BIG = -1e30
GELU_C = math.sqrt(2.0 / math.pi)
GELU_A = 0.044715

NN = (((1,), (0,)), ((), ()))
NT = (((1,), (1,)), ((), ()))
TN = (((0,), (0,)), ((), ()))


def _pick(n, pref):
    if n <= pref:
        return n
    best = LANES
    for t in range(LANES, pref + 1, LANES):
        if n % t == 0:
            best = t
    assert n % best == 0, (n, pref)
    return best


def _rows(n, pref):
    t = min(n, pref)
    while n % t:
        t //= 2
    assert t % 16 == 0 or t == n, (n, pref)
    return t


def _params(sem):
    return pltpu.CompilerParams(dimension_semantics=sem, vmem_limit_bytes=VMEM_LIMIT)


def _fold8(v):
    r, c = v.shape
    return jnp.sum(v.reshape(r // SUBLANES, SUBLANES, c), axis=0)


def _dot(a, b, dims=None):
    if dims is None:
        return jnp.dot(a, b, preferred_element_type=F32)
    return lax.dot_general(a, b, dims, preferred_element_type=F32)


def _dot_exact(a, b, dims):
    return lax.dot_general(a, b, dims, preferred_element_type=F32, precision=lax.Precision.HIGHEST)


def _sigmoid(v):
    return 1.0 / (1.0 + jnp.exp(-v))


def _rms_scale(x):
    return lax.rsqrt(jnp.mean(x * x, axis=-1, keepdims=True) + RMS_EPS)


def _rms_bwd(x, g, dh):
    r = _rms_scale(x)
    xn = x * r
    dxn = dh * g
    dx = r * (dxn - xn * jnp.mean(dxn * xn, axis=-1, keepdims=True))
    return dx, dh * xn


def norm_matmul(x, g, w, out_dtype, name):
    t, d = x.shape
    m = w.shape[1]
    tm, tn = _rows(t, 1024), _pick(m, 1024)

    def body(x_ref, g_ref, w_ref, o_ref):
        xv = x_ref[...]
        h = (xv * _rms_scale(xv) * g_ref[...]).astype(BF16)
        o_ref[...] = _dot(h, w_ref[...]).astype(o_ref.dtype)

    return pl.pallas_call(
        body, name=name, grid=(t // tm, m // tn),
        in_specs=[pl.BlockSpec((tm, d), lambda i, j: (i, 0)),
                  pl.BlockSpec((1, d), lambda i, j: (0, 0)),
                  pl.BlockSpec((d, tn), lambda i, j: (0, j))],
        out_specs=pl.BlockSpec((tm, tn), lambda i, j: (i, j)),
        out_shape=jax.ShapeDtypeStruct((t, m), out_dtype),
        compiler_params=_params(("parallel", "arbitrary")),
    )(x, g, w)


def matmul_tn(a, b, name, a_kind="plain", gain=None, shard_axis=None, out_dtype=F32, tm_pref=1024, tn_pref=1536,
              tk_pref=1024):
    t, ma = a.shape
    nb = b.shape[1]
    tm = ma if a_kind == "norm" else _pick(ma, tm_pref)
    tn = _pick(nb, tn_pref)
    tk = _rows(t, tk_pref)
    nk = t // tk
    if shard_axis == 0:
        per = tm * 4 // ma
        assert per >= 1 and (ma // 4) * per == tm, (ma, tm)
        out_shape, out_spec = (4, ma // 4, nb), pl.BlockSpec((per, ma // 4, tn), lambda i, j, k: (i, 0, j))
    elif shard_axis == 1:
        per = tn * 4 // nb
        assert per >= 1 and (nb // 4) * per == tn, (nb, tn)
        out_shape, out_spec = (4, ma, nb // 4), pl.BlockSpec((per, tm, nb // 4), lambda i, j, k: (j, i, 0))
    else:
        out_shape, out_spec = (ma, nb), pl.BlockSpec((tm, tn), lambda i, j, k: (i, j))

    def body(*refs):
        if a_kind == "norm":
            a_ref, g_ref, b_ref, o_ref, acc = refs
        else:
            a_ref, b_ref, o_ref, acc = refs
        k = pl.program_id(2)

        @pl.when(k == 0)
        def _():
            acc[...] = jnp.zeros_like(acc)

        av = a_ref[...]
        if a_kind == "norm":
            av = av * _rms_scale(av) * g_ref[...]
        elif a_kind == "relu2":
            av = jnp.square(jnp.maximum(av.astype(F32), 0.0))
        acc[...] += _dot(av.astype(BF16), b_ref[...].astype(BF16), TN)

        @pl.when(k == nk - 1)
        def _():
            if shard_axis == 0:
                o_ref[...] = acc[...].reshape(o_ref.shape).astype(o_ref.dtype)
            elif shard_axis == 1:
                cs = nb // 4
                for n in range(o_ref.shape[0]):
                    o_ref[n] = acc[:, n * cs:(n + 1) * cs].astype(o_ref.dtype)
            else:
                o_ref[...] = acc[...].astype(o_ref.dtype)

    in_specs = [pl.BlockSpec((tk, tm), lambda i, j, k: (k, i))]
    args = [a]
    if a_kind == "norm":
        in_specs.append(pl.BlockSpec((1, ma), lambda i, j, k: (0, 0)))
        args.append(gain)
    in_specs.append(pl.BlockSpec((tk, tn), lambda i, j, k: (k, j)))
    args.append(b)
    return pl.pallas_call(
        body, name=name, grid=(ma // tm, nb // tn, nk),
        in_specs=in_specs,
        out_specs=out_spec,
        out_shape=jax.ShapeDtypeStruct(out_shape, out_dtype),
        scratch_shapes=[pltpu.VMEM((tm, tn), F32)],
        compiler_params=_params(("parallel", "parallel", "arbitrary")),
    )(*args)


def _tri(n, upper):
    r = lax.broadcasted_iota(jnp.int32, (n, n), 0)
    c = lax.broadcasted_iota(jnp.int32, (n, n), 1)
    return jnp.where((c >= r) if upper else (c <= r), 1.0, 0.0).astype(F32)


def _head_rows():
    r = lax.broadcasted_iota(jnp.int32, (SUBLANES, LANES), 0)
    c = lax.broadcasted_iota(jnp.int32, (SUBLANES, LANES), 1)
    return jnp.where(r == c, 1.0, 0.0).astype(F32)


def forget_cumsum(rest, bf, bsz, seq, f_blk):
    tc = _rows(seq, 512)
    nc = seq // tc

    def body(f_ref, b_ref, col_ref, row_ref, carry):
        c = pl.program_id(1)

        @pl.when(c == 0)
        def _():
            carry[...] = jnp.zeros_like(carry)

        z = f_ref[...] + b_ref[...]
        logf = jnp.minimum(z, 0.0) - jnp.log(1.0 + jnp.exp(-jnp.abs(z)))
        cum = _dot_exact(_tri(tc, False), logf, NN) + carry[0:1, :]
        col_ref[0] = cum
        row_ref[0] = _dot_exact(_head_rows(), cum, NT)
        carry[...] = jnp.broadcast_to(cum[tc - 1:tc, :], carry.shape)

    return pl.pallas_call(
        body, name="forget_cumsum", grid=(bsz, nc),
        in_specs=[pl.BlockSpec((tc, F_PAD), lambda b, c: (b * nc + c, f_blk)),
                  pl.BlockSpec((1, F_PAD), lambda b, c: (0, 0))],
        out_specs=[pl.BlockSpec((1, tc, LANES), lambda b, c: (b, c, 0)),
                   pl.BlockSpec((1, SUBLANES, tc), lambda b, c: (b, 0, c))],
        out_shape=[jax.ShapeDtypeStruct((bsz, seq, LANES), F32),
                   jax.ShapeDtypeStruct((bsz, SUBLANES, seq), F32)],
        scratch_shapes=[pltpu.VMEM((SUBLANES, LANES), F32)],
        compiler_params=_params(("parallel", "arbitrary")),
    )(rest, bf)


def forget_bwd(dcq, dcp, rest, bf, bsz, seq, f_blk):
    tc = _attn_tile(seq)
    nc = seq // tc

    def body(dq_ref, dc_ref, f_ref, b_ref, df_ref, db_ref, carry):
        b = pl.program_id(0)
        c = pl.program_id(1)

        @pl.when(c == 0)
        def _():
            carry[...] = jnp.zeros_like(carry)

        @pl.when((b == 0) & (c == 0))
        def _():
            db_ref[...] = jnp.zeros_like(db_ref)

        row = lax.broadcasted_iota(jnp.int32, (SUBLANES, tc), 0)
        heads = jnp.zeros((SUBLANES, tc), F32)
        dc = jnp.zeros((tc, LANES), F32)
        for p in range(HEAD_PAIRS):
            blk = dq_ref[0, p, 0]
            heads = heads + jnp.where(row == 2 * p, blk[0:1], 0.0) + jnp.where(row == 2 * p + 1, blk[1:2], 0.0)
            dc = dc + dc_ref[0, p]
        dc = dc + jnp.concatenate([heads, jnp.zeros((LANES - SUBLANES, tc), F32)], axis=0).T
        dlogf = _dot_exact(_tri(tc, True), dc, NN) + carry[0:1, :]
        carry[...] = jnp.broadcast_to(dlogf[0:1, :], carry.shape)
        z = f_ref[...] + b_ref[...]
        lane = lax.broadcasted_iota(jnp.int32, z.shape, 1)
        df = jnp.where(lane < ATTN_HEADS, dlogf * _sigmoid(-z), 0.0)
        df_ref[...] = df.astype(df_ref.dtype)
        db_ref[...] += _fold8(df)

    return pl.pallas_call(
        body, name="forget_bwd", grid=(bsz, nc),
        in_specs=[pl.BlockSpec((1, HEAD_PAIRS, 1, SUBLANES, tc), lambda b, c: (b, 0, nc - 1 - c, 0, 0)),
                  pl.BlockSpec((1, HEAD_PAIRS, tc, LANES), lambda b, c: (b, 0, nc - 1 - c, 0)),
                  pl.BlockSpec((tc, F_PAD), lambda b, c: (b * nc + nc - 1 - c, f_blk)),
                  pl.BlockSpec((1, F_PAD), lambda b, c: (0, 0))],
        out_specs=[pl.BlockSpec((tc, F_PAD), lambda b, c: (b * nc + nc - 1 - c, 0)),
                   pl.BlockSpec((SUBLANES, F_PAD), lambda b, c: (0, 0))],
        out_shape=[jax.ShapeDtypeStruct((bsz * seq, F_PAD), BF16),
                   jax.ShapeDtypeStruct((SUBLANES, F_PAD), F32)],
        scratch_shapes=[pltpu.VMEM((SUBLANES, LANES), F32)],
        compiler_params=_params(("arbitrary", "arbitrary")),
    )(dcq, dcp, rest, bf)


def _attn_tile(seq):
    return 512 if seq >= 2048 else 128


def _lane_head(shape, par):
    lane = lax.broadcasted_iota(jnp.int32, shape, len(shape) - 1)
    return (lane >= HEAD_DIM) if par else (lane < HEAD_DIM)


def _pick_lane(block, idx):
    lane = lax.broadcasted_iota(jnp.int32, block.shape, 1)
    return jnp.sum(jnp.where(lane == idx, block, 0.0), axis=1, keepdims=True)


def _pair_rows(lo_lane, hi_lane):
    r = lax.broadcasted_iota(jnp.int32, (SUBLANES, LANES), 0)
    c = lax.broadcasted_iota(jnp.int32, (SUBLANES, LANES), 1)
    if lo_lane is None:
        sel = ((r == 0) & (c < HEAD_DIM)) | ((r == 1) & (c >= HEAD_DIM))
    else:
        sel = ((r == 0) & (c == lo_lane)) | ((r == 1) & (c == hi_lane))
    return jnp.where(sel, 1.0, 0.0).astype(F32)


def _causal(s, transposed):
    r = lax.broadcasted_iota(jnp.int32, s.shape, 0)
    c = lax.broadcasted_iota(jnp.int32, s.shape, 1)
    return jnp.where((c >= r) if transposed else (r >= c), s, NEG_BIG)


def _causal_pairs(n, key_major):
    if key_major:
        pairs = [(i, j) for j in range(n) for i in range(j, n)]
    else:
        pairs = [(i, j) for i in range(n) for j in range(i + 1)]
    return (jnp.asarray([p[0] for p in pairs], jnp.int32), jnp.asarray([p[1] for p in pairs], jnp.int32))


def fox_fwd(qkv, cumrow, bsz, seq, carries=()):
    tq = _attn_tile(seq)
    nq = seq // tq
    scale = HEAD_DIM ** -0.5
    kb, vb = ATTN_WIDTH // LANES, 2 * ATTN_WIDTH // LANES
    qi_tab, kj_tab = _causal_pairs(nq, False)
    npairs = int(qi_tab.shape[0])
    sends, landings, n_sems, build = _carry_plan(carries)
    nc = len(sends)

    def body(qi_ref, kj_ref, q_ref, k_ref, v_ref, cr_ref, *rest):
        ins, (o_ref, lr_ref), lands = rest[:nc], rest[nc:nc + 2], rest[nc + 2:2 * nc + 2]
        (m_s, acc_s), sems = rest[2 * nc + 2:2 * nc + 4], rest[2 * nc + 4:]
        hp = pl.program_id(1)
        qi = qi_ref[pl.program_id(2)]
        kj = kj_ref[pl.program_id(2)]
        if nc:
            @pl.when((pl.program_id(0) == 0) & (hp == 0) & (pl.program_id(2) == 0))
            def _():
                for cp in build(ins, lands, sems):
                    cp.start()

        @pl.when(kj == 0)
        def _():
            m_s[...] = jnp.full_like(m_s, NEG_BIG)
            acc_s[...] = jnp.zeros_like(acc_s)

        def step(diag):
            q = q_ref[...]
            k = k_ref[...]
            v = v_ref[...]
            for par in range(2):
                sel = _lane_head(q.shape, par)
                qh = jnp.where(sel, q, 0.0) * scale
                s = _dot(qh.astype(BF16), k, NT) - cr_ref[0, pl.ds(2 * hp + par, 1), :]
                if diag:
                    s = _causal(s, False)
                m_prev = m_s[par]
                m_new = jnp.maximum(m_prev, jnp.max(s, axis=1, keepdims=True))
                p = jnp.exp(s - m_new).astype(BF16)
                acc_s[par] = jnp.exp(m_prev - m_new) * acc_s[par] + _dot(p, jnp.where(sel, v, 1.0).astype(BF16))
                m_s[par] = m_new

        @pl.when(kj < qi)
        def _():
            step(False)

        @pl.when(kj == qi)
        def _():
            step(True)
            lo = _lane_head((tq, LANES), 0)
            sums = [pltpu.roll(acc_s[par], HEAD_DIM, 1) for par in range(2)]
            out = jnp.where(lo, acc_s[0] / sums[0], acc_s[1] / sums[1])
            o_ref[...] = out.astype(o_ref.dtype)
            lse = jnp.where(lo, m_s[0] + jnp.log(sums[0]), m_s[1] + jnp.log(sums[1]))
            lr_ref[0, 0] = _dot_exact(_pair_rows(0, HEAD_DIM), lse, NT)

        if nc:
            @pl.when((pl.program_id(0) == bsz - 1) & (hp == HEAD_PAIRS - 1) & (pl.program_id(2) == npairs - 1))
            def _():
                for cp in build(ins, lands, sems):
                    cp.wait()

    sem_shapes = [pltpu.SemaphoreType.DMA((n_sems,)), pltpu.SemaphoreType.DMA((n_sems,))] if nc else []
    outs = pl.pallas_call(
        body, name="fox_fwd",
        grid_spec=pltpu.PrefetchScalarGridSpec(
            num_scalar_prefetch=2, grid=(bsz, HEAD_PAIRS, npairs),
            in_specs=[pl.BlockSpec((tq, LANES), lambda b, h, t, qi, kj: (b * nq + qi[t], h)),
                      pl.BlockSpec((tq, LANES), lambda b, h, t, qi, kj: (b * nq + kj[t], kb + h)),
                      pl.BlockSpec((tq, LANES), lambda b, h, t, qi, kj: (b * nq + kj[t], vb + h)),
                      pl.BlockSpec((1, SUBLANES, tq), lambda b, h, t, qi, kj: (b, 0, kj[t]))] + [ANY] * nc,
            out_specs=[pl.BlockSpec((tq, LANES), lambda b, h, t, qi, kj: (b * nq + qi[t], h)),
                       pl.BlockSpec((1, 1, SUBLANES, tq), lambda b, h, t, qi, kj: (b, h, 0, qi[t]))] + [ANY] * nc,
            scratch_shapes=[pltpu.VMEM((2, tq, 1), F32), pltpu.VMEM((2, tq, LANES), F32)] + sem_shapes),
        out_shape=[jax.ShapeDtypeStruct((bsz * seq, ATTN_WIDTH), BF16),
                   jax.ShapeDtypeStruct((bsz, HEAD_PAIRS, SUBLANES, seq), F32)] + landings,
        compiler_params=_params(("arbitrary", "arbitrary", "arbitrary")),
    )(qi_tab, kj_tab, qkv, qkv, qkv, cumrow, *sends)
    return outs[0], outs[1], list(outs[2:])


def fox_bwd_kernel(qkv, do, o, lser, cumcol, bsz, seq, carries=()):
    tk = _attn_tile(seq)
    nk = seq // tk
    scale = HEAD_DIM ** -0.5
    kb, vb = ATTN_WIDTH // LANES, 2 * ATTN_WIDTH // LANES
    qi_tab, kj_tab = _causal_pairs(nk, True)
    npairs = int(qi_tab.shape[0])
    sends, landings, n_sems, build = _carry_plan(carries)
    nc = len(sends)

    def body(qi_ref, kj_ref, q_ref, k_ref, v_ref, do_ref, o_ref, lr_ref, cc_ref, *rest):
        ins, lands = rest[:nc], rest[nc + 5:2 * nc + 5]
        dq_ref, dk_ref, dv_ref, dc_ref, dr_ref = rest[nc:nc + 5]
        (dq_s, dk_s, dv_s, dc_s, dr_s), sems = rest[2 * nc + 5:2 * nc + 10], rest[2 * nc + 10:]
        hp = pl.program_id(1)
        t = pl.program_id(2)
        qi = qi_ref[t]
        kj = kj_ref[t]
        if nc:
            @pl.when((pl.program_id(0) == 0) & (hp == 0) & (t == 0))
            def _():
                for cp in build(ins, lands, sems):
                    cp.start()

        @pl.when(t == 0)
        def _():
            dq_s[...] = jnp.zeros_like(dq_s)
            dr_s[...] = jnp.zeros_like(dr_s)

        @pl.when(qi == kj)
        def _():
            dk_s[...] = jnp.zeros_like(dk_s)
            dv_s[...] = jnp.zeros_like(dv_s)
            dc_s[...] = jnp.zeros_like(dc_s)

        def step(diag):
            q = q_ref[...]
            k = k_ref[...]
            v = v_ref[...]
            dov = do_ref[...]
            prod = dov.astype(F32) * o_ref[...].astype(F32)
            drow = _dot_exact(_pair_rows(None, None), prod, NT)
            lrow = lr_ref[0, 0]
            lane = lax.broadcasted_iota(jnp.int32, (tk, LANES), 1)
            for par in range(2):
                head = 2 * hp + par
                sel = _lane_head(k.shape, par)
                kh = (jnp.where(sel, k, 0.0) * scale).astype(BF16)
                st = _dot(kh, q, NT) - _pick_lane(cc_ref[0], head)
                if diag:
                    st = _causal(st, True)
                pt = jnp.exp(st - lrow[par:par + 1, :])
                vh = jnp.where(sel, v, 0.0)
                dpt = _dot(vh.astype(BF16), dov, NT)
                dst = pt * (dpt - drow[par:par + 1, :])
                dsb = dst.astype(BF16)
                dv_s[...] += jnp.where(sel, _dot(pt.astype(BF16), dov), 0.0)
                dk_s[...] += jnp.where(sel, _dot(dsb, q), 0.0)
                dq_s[qi] += _dot(dsb, kh, TN)
                dc_s[...] += jnp.where(lane == head, -jnp.sum(dst, axis=1, keepdims=True), 0.0)
                dr_s[qi, par:par + 1, :] += jnp.sum(dst, axis=0, keepdims=True)

        @pl.when(qi > kj)
        def _():
            step(False)

        @pl.when(qi == kj)
        def _():
            step(True)

        @pl.when(qi == nk - 1)
        def _():
            dk_ref[...] = (dk_s[...] * scale).astype(dk_ref.dtype)
            dv_ref[...] = dv_s[...].astype(dv_ref.dtype)
            dc_ref[0, 0] = dc_s[...]

        @pl.when(t == npairs - 1)
        def _():
            for i in range(nk):
                dq_ref[i * tk:(i + 1) * tk, :] = dq_s[i].astype(dq_ref.dtype)
            dr_ref[0, 0] = dr_s[...]

        if nc:
            @pl.when((pl.program_id(0) == bsz - 1) & (hp == HEAD_PAIRS - 1) & (t == npairs - 1))
            def _():
                for cp in build(ins, lands, sems):
                    cp.wait()

    sem_shapes = [pltpu.SemaphoreType.DMA((n_sems,)), pltpu.SemaphoreType.DMA((n_sems,))] if nc else []
    outs = pl.pallas_call(
        body, name="fox_bwd",
        grid_spec=pltpu.PrefetchScalarGridSpec(
            num_scalar_prefetch=2, grid=(bsz, HEAD_PAIRS, npairs),
            in_specs=[pl.BlockSpec((tk, LANES), lambda b, h, t, qi, kj: (b * nk + qi[t], h)),
                      pl.BlockSpec((tk, LANES), lambda b, h, t, qi, kj: (b * nk + kj[t], kb + h)),
                      pl.BlockSpec((tk, LANES), lambda b, h, t, qi, kj: (b * nk + kj[t], vb + h)),
                      pl.BlockSpec((tk, LANES), lambda b, h, t, qi, kj: (b * nk + qi[t], h)),
                      pl.BlockSpec((tk, LANES), lambda b, h, t, qi, kj: (b * nk + qi[t], h)),
                      pl.BlockSpec((1, 1, SUBLANES, tk), lambda b, h, t, qi, kj: (b, h, 0, qi[t])),
                      pl.BlockSpec((1, tk, LANES), lambda b, h, t, qi, kj: (b, kj[t], 0))] + [ANY] * nc,
            out_specs=[pl.BlockSpec((seq, LANES), lambda b, h, t, qi, kj: (b, h)),
                       pl.BlockSpec((tk, LANES), lambda b, h, t, qi, kj: (b * nk + kj[t], h)),
                       pl.BlockSpec((tk, LANES), lambda b, h, t, qi, kj: (b * nk + kj[t], h)),
                       pl.BlockSpec((1, 1, tk, LANES), lambda b, h, t, qi, kj: (b, h, kj[t], 0)),
                       pl.BlockSpec((1, 1, nk, SUBLANES, tk), lambda b, h, t, qi, kj: (b, h, 0, 0, 0))] + [ANY] * nc,
            scratch_shapes=[pltpu.VMEM((nk, tk, LANES), F32), pltpu.VMEM((tk, LANES), F32),
                            pltpu.VMEM((tk, LANES), F32), pltpu.VMEM((tk, LANES), F32),
                            pltpu.VMEM((nk, SUBLANES, tk), F32)] + sem_shapes),
        out_shape=[jax.ShapeDtypeStruct((bsz * seq, ATTN_WIDTH), BF16),
                   jax.ShapeDtypeStruct((bsz * seq, ATTN_WIDTH), BF16),
                   jax.ShapeDtypeStruct((bsz * seq, ATTN_WIDTH), BF16),
                   jax.ShapeDtypeStruct((bsz, HEAD_PAIRS, seq, LANES), F32),
                   jax.ShapeDtypeStruct((bsz, HEAD_PAIRS, nk, SUBLANES, tk), F32)] + landings,
        compiler_params=_params(("arbitrary", "arbitrary", "arbitrary")),
    )(qi_tab, kj_tab, qkv, qkv, qkv, do, o, lser, cumcol, *sends)
    return outs[:5], list(outs[5:])


def fox_bwd(qkv, do, o, lser, cumcol, rest, bf, bsz, seq, f_blk, carries=()):
    (dq, dk, dv, dcp, dcq), lands = fox_bwd_kernel(qkv, do, o, lser, cumcol, bsz, seq, carries)
    df, dbf = forget_bwd(dcq, dcp, rest, bf, bsz, seq, f_blk)
    return dq, dk, dv, df, dbf, lands


SCAN_STEPS = (1, 2, 4)
TAB_FWD = 0
TAB_BWD = 32
TAB_CARRY = 24
TAB_ROWS = 64


def _ssm_tile(seq):
    return 256 if seq >= 1024 else 64


def _scan_block(xr, xi, tab_ref, re, im, cr, ci, reverse):
    base = TAB_BWD if reverse else TAB_FWD
    for n, s in enumerate(SCAN_STEPS):
        ar = tab_ref[base + n * SUBLANES:base + (n + 1) * SUBLANES, re]
        ai = tab_ref[base + n * SUBLANES:base + (n + 1) * SUBLANES, im]
        shift = SUBLANES - s if reverse else s
        sr = pltpu.roll(xr, shift, 0)
        si = pltpu.roll(xi, shift, 0)
        xr, xi = xr + ar * sr - ai * si, xi + ar * si + ai * sr
    pr = tab_ref[base + TAB_CARRY:base + TAB_CARRY + SUBLANES, re]
    pi = tab_ref[base + TAB_CARRY:base + TAB_CARRY + SUBLANES, im]
    xr, xi = xr + pr * cr - pi * ci, xi + pr * ci + pi * cr
    return xr, xi


def ssm_fwd(rest, wb4, wc4, tabs, dskip, bsz, seq, u_blk):
    tt = _ssm_tile(seq)
    nt = seq // tt

    def body(u_ref, wb_ref, wc_ref, tab_ref, d_ref, y_ref, h_ref, carry):
        c = pl.program_id(1)

        @pl.when(c == 0)
        def _():
            carry[...] = jnp.zeros_like(carry)

        u = u_ref[...]
        ub = u.astype(BF16)
        for j in range(SSM_CHUNKS):
            h_ref[:, j * CHUNK_LANES:(j + 1) * CHUNK_LANES] = _dot(ub[:, j * LANES:(j + 1) * LANES], wb_ref[j])
        for j in range(SSM_CHUNKS):
            re = slice(j * CHUNK_LANES, j * CHUNK_LANES + CHUNK_STATES)
            im = slice(j * CHUNK_LANES + CHUNK_STATES, (j + 1) * CHUNK_LANES)

            def blk(bi, car):
                r0 = pl.multiple_of(bi * SUBLANES, SUBLANES)
                xr, xi = _scan_block(h_ref[pl.ds(r0, SUBLANES), re], h_ref[pl.ds(r0, SUBLANES), im],
                                     tab_ref, re, im, car[0], car[1], False)
                h_ref[pl.ds(r0, SUBLANES), re] = xr
                h_ref[pl.ds(r0, SUBLANES), im] = xi
                return xr[SUBLANES - 1:SUBLANES], xi[SUBLANES - 1:SUBLANES]

            cr, ci = lax.fori_loop(0, tt // SUBLANES, blk, (carry[0:1, re], carry[0:1, im]), unroll=2)
            carry[0:1, re] = cr
            carry[0:1, im] = ci
        for j in range(SSM_CHUNKS):
            hj = h_ref[:, j * CHUNK_LANES:(j + 1) * CHUNK_LANES].astype(BF16)
            cols = slice(j * LANES, (j + 1) * LANES)
            y_ref[:, cols] = _dot(hj, wc_ref[j]) + d_ref[:, cols] * u[:, cols]

    return pl.pallas_call(
        body, name="ssm_fwd", grid=(bsz, nt),
        in_specs=[pl.BlockSpec((tt, SSM_WIDTH), lambda b, c: (b * nt + c, u_blk)),
                  pl.BlockSpec((SSM_CHUNKS, LANES, CHUNK_LANES), lambda b, c: (0, 0, 0)),
                  pl.BlockSpec((SSM_CHUNKS, CHUNK_LANES, LANES), lambda b, c: (0, 0, 0)),
                  pl.BlockSpec((TAB_ROWS, STATE_LANES), lambda b, c: (0, 0)),
                  pl.BlockSpec((1, SSM_WIDTH), lambda b, c: (0, 0))],
        out_specs=[pl.BlockSpec((tt, SSM_WIDTH), lambda b, c: (b * nt + c, 0)),
                   pl.BlockSpec((tt, STATE_LANES), lambda b, c: (b * nt + c, 0))],
        out_shape=[jax.ShapeDtypeStruct((bsz * seq, SSM_WIDTH), F32),
                   jax.ShapeDtypeStruct((bsz * seq, STATE_LANES), F32)],
        scratch_shapes=[pltpu.VMEM((SUBLANES, STATE_LANES), F32)],
        compiler_params=_params(("parallel", "arbitrary")),
    )(rest, wb4, wc4, tabs, dskip)


def ssm_bwd(dys, rest, hs, wb4, wc4, tabs, dskip, bsz, seq, u_blk):
    tt = _ssm_tile(seq)
    nt = seq // tt
    nb = tt // SUBLANES

    def body(dy_ref, u_ref, h_ref, hp_ref, wb_ref, wc_ref, tab_ref, d_ref,
             du_ref, ga_ref, gwb_ref, gwc_ref, gd_ref, g_s, carry):
        b = pl.program_id(0)
        c = pl.program_id(1)

        @pl.when(c == 0)
        def _():
            carry[...] = jnp.zeros_like(carry)

        @pl.when((b == 0) & (c == 0))
        def _():
            ga_ref[...] = jnp.zeros_like(ga_ref)
            gwb_ref[...] = jnp.zeros_like(gwb_ref)
            gwc_ref[...] = jnp.zeros_like(gwc_ref)
            gd_ref[...] = jnp.zeros_like(gd_ref)

        dy = dy_ref[...].astype(F32)
        dyb = dy.astype(BF16)
        u = u_ref[...]
        ub = u.astype(BF16)
        first_chunk = c == nt - 1
        for j in range(SSM_CHUNKS):
            g_s[:, j * CHUNK_LANES:(j + 1) * CHUNK_LANES] = _dot(dyb[:, j * LANES:(j + 1) * LANES], wc_ref[j], NT)
        for j in range(SSM_CHUNKS):
            re = slice(j * CHUNK_LANES, j * CHUNK_LANES + CHUNK_STATES)
            im = slice(j * CHUNK_LANES + CHUNK_STATES, (j + 1) * CHUNK_LANES)
            row = lax.broadcasted_iota(jnp.int32, (SUBLANES, CHUNK_STATES), 0)

            def blk(n, car):
                bi = nb - 1 - n
                r0 = pl.multiple_of(bi * SUBLANES, SUBLANES)
                gr, gi = _scan_block(g_s[pl.ds(r0, SUBLANES), re], g_s[pl.ds(r0, SUBLANES), im],
                                     tab_ref, re, im, car[0], car[1], True)
                g_s[pl.ds(r0, SUBLANES), re] = gr
                g_s[pl.ds(r0, SUBLANES), im] = gi
                rp = pl.multiple_of(jnp.maximum(bi - 1, 0) * SUBLANES, SUBLANES)
                inside = bi > 0
                live = jnp.where(jnp.logical_or(inside, jnp.logical_not(first_chunk)), 1.0, 0.0)
                pr = jnp.where(inside, h_ref[pl.ds(rp, SUBLANES), re], hp_ref[:, re])[SUBLANES - 1:SUBLANES] * live
                pi = jnp.where(inside, h_ref[pl.ds(rp, SUBLANES), im], hp_ref[:, im])[SUBLANES - 1:SUBLANES] * live
                hr = jnp.where(row >= 1, pltpu.roll(h_ref[pl.ds(r0, SUBLANES), re], 1, 0), pr)
                hi = jnp.where(row >= 1, pltpu.roll(h_ref[pl.ds(r0, SUBLANES), im], 1, 0), pi)
                return (gr[0:1], gi[0:1], car[2] + gr * hr + gi * hi, car[3] + gi * hr - gr * hi)

            zero = jnp.zeros((SUBLANES, CHUNK_STATES), F32)
            cr, ci, sr, si = lax.fori_loop(0, nb, blk, (carry[0:1, re], carry[0:1, im], zero, zero), unroll=2)
            carry[0:1, re] = cr
            carry[0:1, im] = ci
            ga_ref[:, re] += sr
            ga_ref[:, im] += si
        for j in range(SSM_CHUNKS):
            cols = slice(j * LANES, (j + 1) * LANES)
            lanes = slice(j * CHUNK_LANES, (j + 1) * CHUNK_LANES)
            gj = g_s[:, lanes].astype(BF16)
            du_ref[:, cols] = (_dot(gj, wb_ref[j], NT) + d_ref[:, cols] * dy[:, cols]).astype(du_ref.dtype)
            gwb_ref[j] += _dot(ub[:, cols], gj, TN)
            gwc_ref[j] += _dot(h_ref[:, lanes].astype(BF16), dyb[:, cols], TN)
        gd_ref[...] += _fold8(dy * u)

    def prev_rows(b, c):
        chunk = nt - 1 - c
        return (jnp.maximum((b * nt + chunk) * nb - 1, 0), 0)

    return pl.pallas_call(
        body, name="ssm_bwd", grid=(bsz, nt),
        in_specs=[pl.BlockSpec((tt, SSM_WIDTH), lambda b, c: (b * nt + nt - 1 - c, 0)),
                  pl.BlockSpec((tt, SSM_WIDTH), lambda b, c: (b * nt + nt - 1 - c, u_blk)),
                  pl.BlockSpec((tt, STATE_LANES), lambda b, c: (b * nt + nt - 1 - c, 0)),
                  pl.BlockSpec((SUBLANES, STATE_LANES), prev_rows),
                  pl.BlockSpec((SSM_CHUNKS, LANES, CHUNK_LANES), lambda b, c: (0, 0, 0)),
                  pl.BlockSpec((SSM_CHUNKS, CHUNK_LANES, LANES), lambda b, c: (0, 0, 0)),
                  pl.BlockSpec((TAB_ROWS, STATE_LANES), lambda b, c: (0, 0)),
                  pl.BlockSpec((1, SSM_WIDTH), lambda b, c: (0, 0))],
        out_specs=[pl.BlockSpec((tt, SSM_WIDTH), lambda b, c: (b * nt + nt - 1 - c, 0)),
                   pl.BlockSpec((SUBLANES, STATE_LANES), lambda b, c: (0, 0)),
                   pl.BlockSpec((SSM_CHUNKS, LANES, CHUNK_LANES), lambda b, c: (0, 0, 0)),
                   pl.BlockSpec((SSM_CHUNKS, CHUNK_LANES, LANES), lambda b, c: (0, 0, 0)),
                   pl.BlockSpec((SUBLANES, SSM_WIDTH), lambda b, c: (0, 0))],
        out_shape=[jax.ShapeDtypeStruct((bsz * seq, SSM_WIDTH), BF16),
                   jax.ShapeDtypeStruct((SUBLANES, STATE_LANES), F32),
                   jax.ShapeDtypeStruct((SSM_CHUNKS, LANES, CHUNK_LANES), F32),
                   jax.ShapeDtypeStruct((SSM_CHUNKS, CHUNK_LANES, LANES), F32),
                   jax.ShapeDtypeStruct((SUBLANES, SSM_WIDTH), F32)],
        scratch_shapes=[pltpu.VMEM((tt, STATE_LANES), F32), pltpu.VMEM((SUBLANES, STATE_LANES), F32)],
        compiler_params=_params(("arbitrary", "arbitrary")),
    )(dys, rest, hs, hs, wb4, wc4, tabs, dskip)


def _gelu(v):
    t = jnp.tanh(GELU_C * (v + GELU_A * v * v * v))
    return 0.5 * v * (1.0 + t), t


def mix_fwd(ya, ys, rest, x0, wglu, bglu, wba, wbb, wout):
    t, d = x0.shape
    tm = _rows(t, 256)

    def body(ya_ref, ys_ref, ga_ref, gb_ref, x_ref, wg_ref, bg_ref, wa_ref, wb_ref, wo_ref,
             x1_ref, z_ref, pa_ref, pb_ref, yb_ref, yb2_ref, mx_ref):
        yb, _ = _gelu(ys_ref[...])
        ybb = yb.astype(BF16)
        z = _dot(ybb, wg_ref[...]) + bg_ref[...]
        yb2 = (yb * _sigmoid(z)).astype(BF16)
        pa = _dot(ya_ref[...], wa_ref[...])
        pb = _dot(yb2, wb_ref[...])
        mixed = (_sigmoid(ga_ref[...]) * pa + _sigmoid(gb_ref[...]) * pb).astype(BF16)
        x1_ref[...] = x_ref[...] + _dot(mixed, wo_ref[...])
        z_ref[...] = z.astype(z_ref.dtype)
        pa_ref[...] = pa.astype(pa_ref.dtype)
        pb_ref[...] = pb.astype(pb_ref.dtype)
        yb_ref[...] = ybb
        yb2_ref[...] = yb2
        mx_ref[...] = mixed

    row = lambda w: pl.BlockSpec((tm, w), lambda i: (i, 0))
    full = lambda a: pl.BlockSpec(a.shape, lambda i: (0,) * a.ndim)
    return pl.pallas_call(
        body, name="mix_fwd", grid=(t // tm,),
        in_specs=[row(ATTN_WIDTH), row(SSM_WIDTH),
                  pl.BlockSpec((tm, d), lambda i: (i, 0)), pl.BlockSpec((tm, d), lambda i: (i, 1)),
                  row(d), full(wglu), full(bglu), full(wba), full(wbb), full(wout)],
        out_specs=[row(d), row(SSM_WIDTH), row(d), row(d), row(SSM_WIDTH), row(SSM_WIDTH), row(d)],
        out_shape=[jax.ShapeDtypeStruct((t, d), F32), jax.ShapeDtypeStruct((t, SSM_WIDTH), BF16),
                   jax.ShapeDtypeStruct((t, d), BF16), jax.ShapeDtypeStruct((t, d), BF16),
                   jax.ShapeDtypeStruct((t, SSM_WIDTH), BF16), jax.ShapeDtypeStruct((t, SSM_WIDTH), BF16),
                   jax.ShapeDtypeStruct((t, d), BF16)],
        compiler_params=_params(("parallel",)),
    )(ya, ys, rest, rest, x0, wglu, bglu, wba, wbb, wout)


def mix_bwd(dx1, rest, pa, pb, z, ys, wglu, wba, wbb, wout):
    t, d = dx1.shape
    tm = _rows(t, 256)

    def body(dx_ref, ga_ref, gb_ref, pa_ref, pb_ref, z_ref, ys_ref, wg_ref, wa_ref, wb_ref, wo_ref,
             dya_ref, dys_ref, dg_ref, dpa_ref, dpb_ref, dz_ref, dbg_ref):
        @pl.when(pl.program_id(0) == 0)
        def _():
            dbg_ref[...] = jnp.zeros_like(dbg_ref)

        dmix = _dot(dx_ref[...].astype(BF16), wo_ref[...], NT)
        sa = _sigmoid(ga_ref[...])
        sb = _sigmoid(gb_ref[...])
        dpa = (dmix * sa).astype(BF16)
        dpb = (dmix * sb).astype(BF16)
        dg_ref[:, 0:d] = (dmix * pa_ref[...].astype(F32) * sa * (1.0 - sa)).astype(dg_ref.dtype)
        dg_ref[:, d:2 * d] = (dmix * pb_ref[...].astype(F32) * sb * (1.0 - sb)).astype(dg_ref.dtype)
        dpa_ref[...] = dpa
        dpb_ref[...] = dpb
        dya_ref[...] = _dot(dpa, wa_ref[...], NT).astype(dya_ref.dtype)
        dyb2 = _dot(dpb, wb_ref[...], NT)
        ys = ys_ref[...]
        yb, th = _gelu(ys)
        sg = _sigmoid(z_ref[...].astype(F32))
        dz = dyb2 * yb * sg * (1.0 - sg)
        dzb = dz.astype(BF16)
        dz_ref[...] = dzb
        dbg_ref[...] += _fold8(dz)
        dyb = dyb2 * sg + _dot(dzb, wg_ref[...], NT)
        dgelu = 0.5 * (1.0 + th) + 0.5 * ys * (1.0 - th * th) * GELU_C * (1.0 + 3.0 * GELU_A * ys * ys)
        dys_ref[...] = (dyb * dgelu).astype(dys_ref.dtype)

    row = lambda w: pl.BlockSpec((tm, w), lambda i: (i, 0))
    full = lambda a: pl.BlockSpec(a.shape, lambda i: (0,) * a.ndim)
    return pl.pallas_call(
        body, name="mix_bwd", grid=(t // tm,),
        in_specs=[row(d), pl.BlockSpec((tm, d), lambda i: (i, 0)), pl.BlockSpec((tm, d), lambda i: (i, 1)),
                  row(d), row(d), row(SSM_WIDTH), row(SSM_WIDTH), full(wglu), full(wba), full(wbb), full(wout)],
        out_specs=[row(ATTN_WIDTH), row(SSM_WIDTH), row(2 * d), row(d), row(d), row(SSM_WIDTH),
                   pl.BlockSpec((SUBLANES, SSM_WIDTH), lambda i: (0, 0))],
        out_shape=[jax.ShapeDtypeStruct((t, ATTN_WIDTH), BF16), jax.ShapeDtypeStruct((t, SSM_WIDTH), BF16),
                   jax.ShapeDtypeStruct((t, 2 * d), BF16), jax.ShapeDtypeStruct((t, d), BF16),
                   jax.ShapeDtypeStruct((t, d), BF16), jax.ShapeDtypeStruct((t, SSM_WIDTH), BF16),
                   jax.ShapeDtypeStruct((SUBLANES, SSM_WIDTH), F32)],
        compiler_params=_params(("arbitrary",)),
    )(dx1, rest, rest, pa, pb, z, ys, wglu, wba, wbb, wout)


def mlp_fwd(x1, g, wup, wdown):
    t, d = x1.shape
    ff = wup.shape[1]
    tm, tf = _rows(t, 1024), _pick(ff, 1024)
    nf = ff // tf

    def body(x_ref, g_ref, wu_ref, wd_ref, x2_ref, up_ref, h_s, acc_s):
        f = pl.program_id(1)

        @pl.when(f == 0)
        def _():
            xv = x_ref[...]
            h_s[...] = (xv * _rms_scale(xv) * g_ref[...]).astype(BF16)
            acc_s[...] = jnp.zeros_like(acc_s)

        up = _dot(h_s[...], wu_ref[...])
        up_ref[...] = up.astype(up_ref.dtype)
        act = jnp.square(jnp.maximum(up, 0.0)).astype(BF16)
        acc_s[...] += _dot(act, wd_ref[...])

        @pl.when(f == nf - 1)
        def _():
            x2_ref[...] = x_ref[...] + acc_s[...]

    return pl.pallas_call(
        body, name="mlp_fwd", grid=(t // tm, nf),
        in_specs=[pl.BlockSpec((tm, d), lambda i, f: (i, 0)), pl.BlockSpec((1, d), lambda i, f: (0, 0)),
                  pl.BlockSpec((d, tf), lambda i, f: (0, f)), pl.BlockSpec((tf, d), lambda i, f: (f, 0))],
        out_specs=[pl.BlockSpec((tm, d), lambda i, f: (i, 0)), pl.BlockSpec((tm, tf), lambda i, f: (i, f))],
        out_shape=[jax.ShapeDtypeStruct((t, d), F32), jax.ShapeDtypeStruct((t, ff), BF16)],
        scratch_shapes=[pltpu.VMEM((tm, d), BF16), pltpu.VMEM((tm, d), F32)],
        compiler_params=_params(("parallel", "arbitrary")),
    )(x1, g, wup, wdown)


def mlp_bwd(dx2, up, x1, g, wup, wdown):
    t, d = x1.shape
    ff = wup.shape[1]
    tm, tf = _rows(t, 512), _pick(ff, 1024)
    nf = ff // tf

    def body(dx_ref, up_ref, x_ref, g_ref, wu_ref, wd_ref, dup_ref, dx1_ref, dg_ref, dxb_s, acc_s):
        i = pl.program_id(0)
        f = pl.program_id(1)

        @pl.when((i == 0) & (f == 0))
        def _():
            dg_ref[...] = jnp.zeros_like(dg_ref)

        @pl.when(f == 0)
        def _():
            dxb_s[...] = dx_ref[...].astype(BF16)
            acc_s[...] = jnp.zeros_like(acc_s)

        dact = _dot(dxb_s[...], wd_ref[...], NT)
        dup = (dact * 2.0 * jnp.maximum(up_ref[...].astype(F32), 0.0)).astype(BF16)
        dup_ref[...] = dup
        acc_s[...] += _dot(dup, wu_ref[...], NT)

        @pl.when(f == nf - 1)
        def _():
            dxn, dgain = _rms_bwd(x_ref[...], g_ref[...], acc_s[...])
            dx1_ref[...] = dx_ref[...] + dxn
            dg_ref[...] += _fold8(dgain)

    return pl.pallas_call(
        body, name="mlp_bwd", grid=(t // tm, nf),
        in_specs=[pl.BlockSpec((tm, d), lambda i, f: (i, 0)), pl.BlockSpec((tm, tf), lambda i, f: (i, f)),
                  pl.BlockSpec((tm, d), lambda i, f: (i, 0)), pl.BlockSpec((1, d), lambda i, f: (0, 0)),
                  pl.BlockSpec((d, tf), lambda i, f: (0, f)), pl.BlockSpec((tf, d), lambda i, f: (f, 0))],
        out_specs=[pl.BlockSpec((tm, tf), lambda i, f: (i, f)), pl.BlockSpec((tm, d), lambda i, f: (i, 0)),
                   pl.BlockSpec((SUBLANES, d), lambda i, f: (0, 0))],
        out_shape=[jax.ShapeDtypeStruct((t, ff), BF16), jax.ShapeDtypeStruct((t, d), F32),
                   jax.ShapeDtypeStruct((SUBLANES, d), F32)],
        scratch_shapes=[pltpu.VMEM((tm, d), BF16), pltpu.VMEM((tm, d), F32)],
        compiler_params=_params(("arbitrary", "arbitrary")),
    )(dx2, up, x1, g, wup, wdown)


def proj_bwd(dproj, wpad, x0, dx1, g):
    t, d = x0.shape
    m = wpad.shape[1]
    tm = _rows(t, 256)

    def body(dp_ref, w_ref, x_ref, dx1_ref, g_ref, dx0_ref, dg_ref):
        @pl.when(pl.program_id(0) == 0)
        def _():
            dg_ref[...] = jnp.zeros_like(dg_ref)

        dh = _dot(dp_ref[...], w_ref[...], NT)
        dxn, dgain = _rms_bwd(x_ref[...], g_ref[...], dh)
        dx0_ref[...] = dx1_ref[...] + dxn
        dg_ref[...] += _fold8(dgain)

    return pl.pallas_call(
        body, name="proj_bwd", grid=(t // tm,),
        in_specs=[pl.BlockSpec((tm, m), lambda i: (i, 0)), pl.BlockSpec((d, m), lambda i: (0, 0)),
                  pl.BlockSpec((tm, d), lambda i: (i, 0)), pl.BlockSpec((tm, d), lambda i: (i, 0)),
                  pl.BlockSpec((1, d), lambda i: (0, 0))],
        out_specs=[pl.BlockSpec((tm, d), lambda i: (i, 0)), pl.BlockSpec((SUBLANES, d), lambda i: (0, 0))],
        out_shape=[jax.ShapeDtypeStruct((t, d), F32), jax.ShapeDtypeStruct((SUBLANES, d), F32)],
        compiler_params=_params(("arbitrary",)),
    )(dproj, wpad, x0, dx1, g)


def final_loss(x, g, target):
    t, d = x.shape
    tm = _rows(t, 512)

    def body(x_ref, g_ref, t_ref, dx_ref, ls_ref, dg_ref):
        @pl.when(pl.program_id(0) == 0)
        def _():
            ls_ref[...] = jnp.zeros_like(ls_ref)
            dg_ref[...] = jnp.zeros_like(dg_ref)

        xv = x_ref[...]
        gv = g_ref[...]
        err = xv * _rms_scale(xv) * gv - t_ref[...]
        ls_ref[...] += _fold8(err * err) * (0.5 / d)
        dxn, dgain = _rms_bwd(xv, gv, err * (1.0 / d))
        dx_ref[...] = dxn
        dg_ref[...] += _fold8(dgain)

    return pl.pallas_call(
        body, name="final_loss", grid=(t // tm,),
        in_specs=[pl.BlockSpec((tm, d), lambda i: (i, 0)), pl.BlockSpec((1, d), lambda i: (0, 0)),
                  pl.BlockSpec((tm, d), lambda i: (i, 0))],
        out_specs=[pl.BlockSpec((tm, d), lambda i: (i, 0)), pl.BlockSpec((SUBLANES, d), lambda i: (0, 0)),
                   pl.BlockSpec((SUBLANES, d), lambda i: (0, 0))],
        out_shape=[jax.ShapeDtypeStruct((t, d), F32), jax.ShapeDtypeStruct((SUBLANES, d), F32),
                   jax.ShapeDtypeStruct((SUBLANES, d), F32)],
        compiler_params=_params(("arbitrary",)),
    )(x, g, target)


def _peers():
    x, y, c = lax.axis_index("x"), lax.axis_index("y"), lax.axis_index("c")
    peers = []
    for m in range(1, 8):
        fx, fy, fc = (m >> 2) & 1, (m >> 1) & 1, m & 1
        peers.append((m, ((1 - x) if fx else x, (1 - y) if fy else y, (1 - c) if fc else c)))
    return x, y, c, peers


def _remote(src, dst, sems, k, peer):
    return pltpu.make_async_remote_copy(src_ref=src, dst_ref=dst, send_sem=sems[0].at[k], recv_sem=sems[1].at[k],
                                        device_id=peer, device_id_type=MESH)


def _gather_copies(w_refs, out_refs, sems, first):
    x, y, c, peers = _peers()
    chip = 2 * x + y
    cps = []
    for t, (w, o) in enumerate(zip(w_refs, out_refs)):
        half = w.shape[0] // 2
        rows = pl.ds(c * half, half)
        for m, peer in peers:
            if m >> 1:
                cps.append(_remote(w.at[rows], o.at[chip, rows], sems, first + 7 * t + m - 1, peer))
    return cps


def _reduce_copies(g_refs, out_refs, sems, first):
    x, y, c, peers = _peers()
    me = 4 * x + 2 * y + c
    cps = []
    for t, (g, o) in enumerate(zip(g_refs, out_refs)):
        for m, (px, py, pc) in peers:
            cps.append(_remote(g.at[2 * px + py], o.at[me], sems, first + 7 * t + m - 1, (px, py, pc)))
    return cps


def _all_copies(s_refs, out_refs, sems, first):
    x, y, c, peers = _peers()
    me = 4 * x + 2 * y + c
    return [_remote(s, o.at[me], sems, first + 7 * t + m - 1, peer)
            for t, (s, o) in enumerate(zip(s_refs, out_refs)) for m, peer in peers]


EXCHANGES = {"gather": (_gather_copies, lambda a: (4,) + a.shape),
             "reduce": (_reduce_copies, lambda a: (8,) + a.shape[1:]),
             "all": (_all_copies, lambda a: (8,) + a.shape)}


def _carry_plan(carries):
    inputs, shapes, spans = [], [], []
    for kind, arrs in carries:
        for a in arrs:
            inputs.append(a)
            shapes.append(jax.ShapeDtypeStruct(EXCHANGES[kind][1](a), a.dtype))
        spans.append((kind, len(arrs)))

    def build(in_refs, out_refs, sems):
        cps, pos = [], 0
        for kind, cnt in spans:
            cps += EXCHANGES[kind][0](in_refs[pos:pos + cnt], out_refs[pos:pos + cnt], sems, 7 * pos)
            pos += cnt
        return cps

    return inputs, shapes, 7 * len(inputs), build


def exchange(carries, name):
    inputs, shapes, n_sems, build = _carry_plan(carries)
    n = len(inputs)

    def body(*refs):
        cps = build(refs[:n], refs[n:2 * n], refs[2 * n:])
        for cp in cps:
            cp.start()
        for cp in cps:
            cp.wait()

    return pl.pallas_call(
        body, name=name, in_specs=[ANY] * n, out_specs=[ANY] * n, out_shape=shapes,
        scratch_shapes=[pltpu.SemaphoreType.DMA((n_sems,)), pltpu.SemaphoreType.DMA((n_sems,))],
    )(*inputs)


def _adamw_math(w, g, m, v):
    m = ADAM_B1 * m + (1.0 - ADAM_B1) * g
    v = ADAM_B2 * v + (1.0 - ADAM_B2) * (g * g)
    m_hat = m / (1.0 - ADAM_B1 ** ADAM_STEP)
    v_hat = v / (1.0 - ADAM_B2 ** ADAM_STEP)
    delta = -ADAM_LR * (m_hat / (jnp.sqrt(v_hat) + ADAM_EPS) + ADAM_WD * w)
    return delta, m, v


def adamw_layer(l, w, m, v, parts, own, device, chip, bufs, name):
    _, r, cdim = w.shape
    tr = _rows(r, PACK_ROW_TILE)

    def body(dev_ref, chip_ref, w_ref, m_ref, v_ref, p_ref, o_ref, *rest):
        g_ref, d_ref, nm_ref, nv_ref = rest[-4:]
        g = None
        for dev in range(8):
            part = jnp.where(dev_ref[0] == dev, o_ref[0], p_ref[dev]).astype(F32)
            g = part if g is None else g + part
        d, nm, nv = _adamw_math(w_ref[0], g, m_ref[0], v_ref[0])
        g_ref[0] = g
        d_ref[0] = d
        nm_ref[0] = nm
        nv_ref[0] = nv

    lay = pl.BlockSpec((1, tr, cdim), lambda i, dev_ref, chip_ref: (l, i, 0))
    in_specs = [lay, lay, lay, pl.BlockSpec((8, tr, cdim), lambda i, dev_ref, chip_ref: (0, i, 0)),
                pl.BlockSpec((1, tr, cdim), lambda i, dev_ref, chip_ref: (chip_ref[0], i, 0))]
    args = [device, chip, w, m, v, parts, own]
    aliases = {}
    if bufs is not None:
        in_specs += [ANY] * 4
        aliases = {len(args) + k: k for k in range(4)}
        args += list(bufs)
    return pl.pallas_call(
        body, name=name,
        grid_spec=pltpu.PrefetchScalarGridSpec(num_scalar_prefetch=2, grid=(r // tr,), in_specs=in_specs,
                                               out_specs=[lay] * 4),
        out_shape=[jax.ShapeDtypeStruct(w.shape, F32)] * 4,
        input_output_aliases=aliases,
        compiler_params=_params(("parallel",)),
    )(*args)


def sum_and_adamw(parts, own, me, w, m, v):
    _, r, cdim = parts.shape
    tr = _rows(r, PACK_ROW_TILE)

    def body(me_ref, p_ref, o_ref, w_ref, m_ref, v_ref, g_ref, d_ref, nm_ref, nv_ref):
        part = lambda k: jnp.where(me_ref[0] == k, o_ref[...], p_ref[k]).astype(F32)
        g = part(0)
        for k in range(1, 8):
            g = g + part(k)
        d, nm, nv = _adamw_math(w_ref[...], g, m_ref[...], v_ref[...])
        g_ref[...] = g
        d_ref[...] = d
        nm_ref[...] = nm
        nv_ref[...] = nv

    spec = pl.BlockSpec((tr, cdim), lambda i, me_ref: (i, 0))
    return pl.pallas_call(
        body, name="sum_and_adamw",
        grid_spec=pltpu.PrefetchScalarGridSpec(
            num_scalar_prefetch=1, grid=(r // tr,),
            in_specs=[pl.BlockSpec((8, tr, cdim), lambda i, me_ref: (0, i, 0)), spec, spec, spec, spec],
            out_specs=[spec] * 4),
        out_shape=[jax.ShapeDtypeStruct((r, cdim), F32)] * 4,
        compiler_params=_params(("parallel",)),
    )(me, parts, own, w, m, v)


SHARDED = ("w_in", "w_glu", "w_branch_a", "w_branch_b", "w_out", "w_mlp_up", "w_mlp_down")
SHARD_AXIS = {"w_in": 2, "w_glu": 1, "w_branch_a": 2, "w_branch_b": 2, "w_out": 1, "w_mlp_up": 2, "w_mlp_down": 1}
SMALL = ("norm_mix", "b_forget", "ssm_lambda_re", "ssm_lambda_im", "ssm_log_dt", "ssm_b_re", "ssm_b_im",
         "ssm_c_re", "ssm_c_im", "ssm_d", "b_glu", "norm_mlp", "norm_final")
SMALL_WIDE = ("ssm_b_re", "ssm_b_im", "ssm_c_re", "ssm_c_im")


def pack_flat(arrs):
    flat = jnp.concatenate([a.reshape(-1).astype(F32) for a in arrs])
    unit = PACK_COLS * PACK_ROW_TILE
    rows = (flat.shape[0] + unit - 1) // unit * PACK_ROW_TILE
    return jnp.pad(flat, (0, rows * PACK_COLS - flat.shape[0])).reshape(rows, PACK_COLS)


def unpack_flat(packed, shapes):
    flat = packed.reshape(-1)
    out, off = [], 0
    for s in shapes:
        n = math.prod(s)
        out.append(flat[off:off + n].reshape(tuple(s)))
        off += n
    return out


def _discretise(lam_re, lam_im, log_dt, b_re, b_im):
    dt = jnp.exp(log_dt)[:, None]
    mag = jnp.exp(lam_re * dt)
    ar = mag * jnp.cos(lam_im * dt)
    ai = mag * jnp.sin(lam_im * dt)
    den = lam_re * lam_re + lam_im * lam_im
    cr = ((ar - 1.0) * lam_re + ai * lam_im) / den
    ci = (ai * lam_re - (ar - 1.0) * lam_im) / den
    bbr = cr[:, :, None] * b_re - ci[:, :, None] * b_im
    bbi = cr[:, :, None] * b_im + ci[:, :, None] * b_re
    return ar, ai, bbr, bbi


def _ssm_inputs(ar, ai, bbr, bbi, c_re, c_im):
    pr, pi = [ar], [ai]
    for _ in range(SUBLANES - 1):
        pr, pi = pr + [pr[-1] * ar - pi[-1] * ai], pi + [pr[-1] * ai + pi[-1] * ar]
    powers = jnp.stack([jnp.stack(pr), jnp.stack(pi)], axis=1)
    powers = powers.reshape(SUBLANES, 2, SSM_CHUNKS, CHUNK_STATES).transpose(0, 2, 1, 3).reshape(SUBLANES, STATE_LANES)
    conj = powers * jnp.tile(jnp.repeat(jnp.asarray([1.0, -1.0], F32), CHUNK_STATES), SSM_CHUNKS)
    idx = jnp.arange(SUBLANES)[:, None]
    tabs = jnp.concatenate(
        [jnp.where(idx >= s, powers[s - 1][None, :], 0.0) for s in SCAN_STEPS] + [powers]
        + [jnp.where(idx < SUBLANES - s, conj[s - 1][None, :], 0.0) for s in SCAN_STEPS] + [conj[::-1]], axis=0)
    eye = jnp.eye(CHUNK_GROUPS, dtype=F32)
    wb = jnp.stack([bbr, bbi]).reshape(2, SSM_CHUNKS, CHUNK_GROUPS, SSM_STATE, SSM_GROUP_CH).transpose(1, 2, 4, 0, 3)
    wb4 = (wb[:, :, :, :, None, :] * eye[None, :, None, None, :, None]).reshape(SSM_CHUNKS, LANES, CHUNK_LANES)
    wc = jnp.stack([c_re, -c_im]).reshape(2, SSM_CHUNKS, CHUNK_GROUPS, SSM_GROUP_CH, SSM_STATE).transpose(1, 0, 4, 2, 3)
    wc4 = (wc[:, :, None, :, :, :] * eye[None, None, :, None, :, None]).reshape(SSM_CHUNKS, CHUNK_LANES, LANES)
    return tabs, wb4.astype(BF16), wc4.astype(BF16)


def _ssm_param_grads(ga8, gwb, gwc):
    eye = jnp.eye(CHUNK_GROUPS, dtype=F32)
    ga = jnp.sum(ga8, axis=0).reshape(SSM_CHUNKS, 2, CHUNK_STATES)
    gar = ga[:, 0].reshape(SSM_GROUPS, SSM_STATE)
    gai = ga[:, 1].reshape(SSM_GROUPS, SSM_STATE)

    def from_wb(g):
        t = g.reshape(SSM_CHUNKS, CHUNK_GROUPS, SSM_GROUP_CH, CHUNK_GROUPS, SSM_STATE)
        return jnp.einsum("jgchp,gh->jgpc", t, eye).reshape(SSM_GROUPS, SSM_STATE, SSM_GROUP_CH)

    def from_wc(g):
        t = g.reshape(SSM_CHUNKS, CHUNK_GROUPS, SSM_STATE, CHUNK_GROUPS, SSM_GROUP_CH)
        return jnp.einsum("jhpgc,gh->jgcp", t, eye).reshape(SSM_GROUPS, SSM_GROUP_CH, SSM_STATE)

    return (gar, gai, from_wb(gwb[:, :, :CHUNK_STATES]), from_wb(gwb[:, :, CHUNK_STATES:]),
            from_wc(gwc[:, :CHUNK_STATES]), -from_wc(gwc[:, CHUNK_STATES:]))


def kernel(x, norm_mix, w_in, b_forget, ssm_lambda_re, ssm_lambda_im, ssm_log_dt, ssm_b_re, ssm_b_im, ssm_c_re, ssm_c_im, ssm_d, w_glu, b_glu, w_branch_a, w_branch_b, w_out, norm_mlp, w_mlp_up, w_mlp_down, norm_final, loss_target, m_norm_mix, m_w_in, m_b_forget, m_ssm_lambda_re, m_ssm_lambda_im, m_ssm_log_dt, m_ssm_b_re, m_ssm_b_im, m_ssm_c_re, m_ssm_c_im, m_ssm_d, m_w_glu, m_b_glu, m_w_branch_a, m_w_branch_b, m_w_out, m_norm_mlp, m_w_mlp_up, m_w_mlp_down, m_norm_final, v_norm_mix, v_w_in, v_b_forget, v_ssm_lambda_re, v_ssm_lambda_im, v_ssm_log_dt, v_ssm_b_re, v_ssm_b_im, v_ssm_c_re, v_ssm_c_im, v_ssm_d, v_w_glu, v_b_glu, v_w_branch_a, v_w_branch_b, v_w_out, v_norm_mlp, v_w_mlp_up, v_w_mlp_down, v_norm_final):
    args = dict(locals())
    bsz, seq, d = x.shape
    nl = norm_mix.shape[0]
    tokens = bsz * seq
    aw, sw = ATTN_WIDTH, SSM_WIDTH
    chip = (2 * lax.axis_index("x") + lax.axis_index("y")).astype(jnp.int32)
    chip_id = chip.reshape(1)
    device_id = (2 * chip + lax.axis_index("c").astype(jnp.int32)).reshape(1)

    own = {n: args[n].astype(BF16) for n in SHARDED}
    late = [n for n in SHARDED if n != "w_in"]
    o_f, o_u, o_ga, o_gb = 3 * aw, 3 * aw + ATTN_HEADS, 3 * aw + ATTN_HEADS + sw, 3 * aw + ATTN_HEADS + sw + d
    u_blk = 2 * d // sw
    f_blk = (2 * d + sw) // F_PAD
    bf_pad = jnp.pad(b_forget, ((0, 0), (0, F_PAD - ATTN_HEADS)))

    def assemble(l, names, gathered):
        return {n: jnp.concatenate([jnp.where(chip == k, own[n][l], g[k]) for k in range(4)],
                                   axis=SHARD_AXIS[n] - 1) for n, g in zip(names, gathered)}

    def split_w_in(win):
        w = {"w_qkv": win[:, :o_f],
             "w_rest": jnp.concatenate([win[:, o_ga:o_gb], win[:, o_gb:], win[:, o_u:o_ga],
                                        jnp.pad(win[:, o_f:o_u], ((0, 0), (0, F_PAD - ATTN_HEADS)))], axis=1)}
        w["w_pad"] = jnp.concatenate([w["w_qkv"], w["w_rest"]], axis=1)
        return w

    disc = [jax.vjp(_discretise, ssm_lambda_re[l], ssm_lambda_im[l], ssm_log_dt[l], ssm_b_re[l], ssm_b_im[l])
            for l in range(nl)]

    g_in = exchange([("gather", [own["w_in"][0]])], "gather_first")
    xs = x.reshape(tokens, d)
    saved, weights = [], []
    for l in range(nl):
        w = split_w_in(assemble(l, ["w_in"], g_in)["w_in"])
        g1 = norm_mix[l].reshape(1, d)
        qkv = norm_matmul(xs, g1, w["w_qkv"], BF16, "proj_qkv")
        rest = norm_matmul(xs, g1, w["w_rest"], F32, "proj_rest")
        cumcol, cumrow = forget_cumsum(rest, bf_pad[l:l + 1], bsz, seq, f_blk)
        carries = [("gather", [own[n][l] for n in late] + ([own["w_in"][l + 1]] if l + 1 < nl else []))]
        ya, lser, lands = fox_fwd(qkv, cumrow, bsz, seq, carries)
        w.update(assemble(l, late, lands))
        g_in = lands[len(late):]
        tabs, wb4, wc4 = _ssm_inputs(*disc[l][0], ssm_c_re[l], ssm_c_im[l])
        dskip = ssm_d[l].reshape(1, sw)
        ys, hs = ssm_fwd(rest, wb4, wc4, tabs, dskip, bsz, seq, u_blk)
        x1, z, pa, pb, yb, yb2, mixed = mix_fwd(ya, ys, rest, xs, w["w_glu"], b_glu[l].reshape(1, sw),
                                                 w["w_branch_a"], w["w_branch_b"], w["w_out"])
        x2, up = mlp_fwd(x1, norm_mlp[l].reshape(1, d), w["w_mlp_up"], w["w_mlp_down"])
        saved.append(dict(x0=xs, qkv=qkv, rest=rest, cumcol=cumcol, ya=ya, lser=lser,
                          tabs=tabs, wb4=wb4, wc4=wc4, dskip=dskip, ys=ys, hs=hs, x1=x1, z=z, pa=pa, pb=pb,
                          yb=yb, yb2=yb2, mixed=mixed, up=up))
        weights.append(w)
        xs = x2
    dx, loss_rows, dgf_rows = final_loss(xs, norm_final.reshape(1, d), loss_target.reshape(tokens, d))
    loss = lax.psum(jnp.sum(loss_rows), ("x", "y", "c"))

    early = [n for n in SHARDED if n != "w_in"]
    big = {n: [None] * nl for n in SHARDED}
    parts = {n: [None] * nl for n in SHARDED}
    small = {n: [None] * nl for n in SMALL if n != "norm_final"}
    for l in reversed(range(nl)):
        s, w = saved[l], weights[l]
        g2 = norm_mlp[l].reshape(1, d)
        dup, dx1, dg2 = mlp_bwd(dx, s["up"], s["x1"], g2, w["w_mlp_up"], w["w_mlp_down"])
        big["w_mlp_down"][l] = matmul_tn(s["up"], dx, "grad_w_mlp_down", a_kind="relu2", shard_axis=0, out_dtype=BF16)
        big["w_mlp_up"][l] = matmul_tn(s["x1"], dup, "grad_w_mlp_up", a_kind="norm", gain=g2, shard_axis=1,
                                       out_dtype=BF16)
        small["norm_mlp"][l] = jnp.sum(dg2, axis=0)
        dya, dys, dgab, dpa, dpb, dz, dbg = mix_bwd(dx1, s["rest"], s["pa"], s["pb"], s["z"], s["ys"],
                                                    w["w_glu"], w["w_branch_a"], w["w_branch_b"], w["w_out"])
        big["w_out"][l] = matmul_tn(s["mixed"], dx1, "grad_w_out", shard_axis=0, out_dtype=BF16)
        big["w_branch_a"][l] = matmul_tn(s["ya"], dpa, "grad_w_branch_a", shard_axis=1, out_dtype=BF16)
        big["w_branch_b"][l] = matmul_tn(s["yb2"], dpb, "grad_w_branch_b", shard_axis=1, out_dtype=BF16)
        big["w_glu"][l] = matmul_tn(s["yb"], dz, "grad_w_glu", shard_axis=0, out_dtype=BF16)
        small["b_glu"][l] = jnp.sum(dbg, axis=0)
        du, ga8, gwb, gwc, gd8 = ssm_bwd(dys, s["rest"], s["hs"], s["wb4"], s["wc4"], s["tabs"], s["dskip"],
                                         bsz, seq, u_blk)
        gar, gai, gbbr, gbbi, gcr, gci = _ssm_param_grads(ga8, gwb, gwc)
        glr, gli, gdt, gbr, gbi = disc[l][1]((gar, gai, gbbr, gbbi))
        small["ssm_lambda_re"][l], small["ssm_lambda_im"][l], small["ssm_log_dt"][l] = glr, gli, gdt
        small["ssm_b_re"][l], small["ssm_b_im"][l] = gbr, gbi
        small["ssm_c_re"][l], small["ssm_c_im"][l] = gcr, gci
        small["ssm_d"][l] = jnp.sum(gd8, axis=0)
        carries = [("reduce", [big[n][l] for n in early])]
        if l + 1 < nl:
            carries.append(("reduce", [big["w_in"][l + 1]]))
        dq, dk, dv, df, dbf, lands = fox_bwd(s["qkv"], dya, s["ya"], s["lser"], s["cumcol"],
                                             s["rest"], bf_pad[l:l + 1], bsz, seq, f_blk, carries)
        for n, p in zip(early, lands):
            parts[n][l] = p
        if l + 1 < nl:
            parts["w_in"][l + 1] = lands[len(early)]
        small["b_forget"][l] = jnp.sum(dbf, axis=0)[:ATTN_HEADS]
        dproj = jnp.concatenate([dq, dk, dv, dgab, du, df], axis=1)
        g1 = norm_mix[l].reshape(1, d)
        dwp = matmul_tn(s["x0"], dproj, "grad_w_in", a_kind="norm", gain=g1)
        big["w_in"][l] = jnp.stack(jnp.split(jnp.concatenate(
            [dwp[:, :o_f], dwp[:, o_f + 2 * d + sw:o_f + 2 * d + sw + ATTN_HEADS],
             dwp[:, o_f + 2 * d:o_f + 2 * d + sw], dwp[:, o_f:o_f + 2 * d]], axis=1), 4, axis=1)).astype(BF16)
        dx, dg1 = proj_bwd(dproj, w["w_pad"], s["x0"], dx1, g1)
        small["norm_mix"][l] = jnp.sum(dg1, axis=0)
    grad_x = dx.reshape(bsz, seq, d)

    small_g = {n: jnp.stack(small[n]) if n != "norm_final" else jnp.sum(dgf_rows, axis=0) for n in SMALL}
    groups = {BF16: SMALL_WIDE, F32: [n for n in SMALL if n not in SMALL_WIDE]}
    packed = {dt: pack_flat([small_g[n] for n in names]).astype(dt) for dt, names in groups.items()}
    parts["w_in"][0], *small_parts = exchange(
        [("reduce", [big["w_in"][0]]), ("all", [packed[dt] for dt in groups])], "exchange_last")

    out_g, out_d, out_m, out_v = {}, {}, {}, {}
    for n in SHARDED:
        bufs = None
        for l in range(nl):
            bufs = adamw_layer(l, args[n], args["m_" + n], args["v_" + n], parts[n][l], big[n][l], device_id, chip_id,
                               bufs, "adamw_" + n)
        out_g[n], out_d[n], out_m[n], out_v[n] = bufs

    for (dt, names), landed in zip(groups.items(), small_parts):
        results = sum_and_adamw(landed, packed[dt], device_id, pack_flat([args[n] for n in names]),
                                pack_flat([args["m_" + n] for n in names]), pack_flat([args["v_" + n] for n in names]))
        for res, flat in zip((out_g, out_d, out_m, out_v), results):
            res.update(zip(names, unpack_flat(flat, [args[n].shape for n in names])))

    order = ("norm_mix", "w_in", "b_forget", "ssm_lambda_re", "ssm_lambda_im", "ssm_log_dt", "ssm_b_re",
             "ssm_b_im", "ssm_c_re", "ssm_c_im", "ssm_d", "w_glu", "b_glu", "w_branch_a", "w_branch_b", "w_out",
             "norm_mlp", "w_mlp_up", "w_mlp_down", "norm_final")
    return (loss, grad_x, *[out_g[n] for n in order], *[out_d[n] for n in order],
            *[out_m[n] for n in order], *[out_v[n] for n in order])
```

```python
import math

import jax
import jax.numpy as jnp
from jax import lax
from jax.experimental import pallas as pl
from jax.experimental.pallas import tpu as pltpu

F32 = jnp.float32
BF16 = jnp.bfloat16
MESH = pl.DeviceIdType.MESH
ANY = pl.BlockSpec(memory_space=pl.ANY)

ATTN_HEADS = 8
HEAD_DIM = 64
ATTN_WIDTH = ATTN_HEADS * HEAD_DIM
HEAD_PAIRS = ATTN_HEADS // 2
SSM_GROUPS = 32
SSM_GROUP_CH = 16
SSM_STATE = 64
SSM_WIDTH = SSM_GROUPS * SSM_GROUP_CH
LANES = 128
SUBLANES = 8
SSM_CHUNKS = SSM_WIDTH // LANES
CHUNK_GROUPS = SSM_GROUPS // SSM_CHUNKS
CHUNK_STATES = CHUNK_GROUPS * SSM_STATE
CHUNK_LANES = 2 * CHUNK_STATES
STATE_LANES = SSM_CHUNKS * CHUNK_LANES
F_PAD = LANES
RMS_EPS = 1e-6
ADAM_LR = 0.001
ADAM_B1 = 0.9
ADAM_B2 = 0.999
ADAM_EPS = 1e-08
ADAM_WD = 0.01
ADAM_STEP = 10
PACK_COLS = 1024
PACK_ROW_TILE = 256
VMEM_LIMIT = 52 * 1024 * 1024
NEG_BIG = -1e30
GELU_C = math.sqrt(2.0 / math.pi)
GELU_A = 0.044715

NN = (((1,), (0,)), ((), ()))
NT = (((1,), (1,)), ((), ()))
TN = (((0,), (0,)), ((), ()))


def _pick(n, pref):
    if n <= pref:
        return n
    best = LANES
    for t in range(LANES, pref + 1, LANES):
        if n % t == 0:
            best = t
    assert n % best == 0, (n, pref)
    return best


def _rows(n, pref):
    t = min(n, pref)
    while n % t:
        t //= 2
    assert t % 16 == 0 or t == n, (n, pref)
    return t


def _params(sem):
    return pltpu.CompilerParams(dimension_semantics=sem, vmem_limit_bytes=VMEM_LIMIT)


def _fold8(v):
    r, c = v.shape
    return jnp.sum(v.reshape(r // SUBLANES, SUBLANES, c), axis=0)


def _dot(a, b, dims=None):
    if dims is None:
        return jnp.dot(a, b, preferred_element_type=F32)
    return lax.dot_general(a, b, dims, preferred_element_type=F32)


def _dot_exact(a, b, dims):
    return lax.dot_general(a, b, dims, preferred_element_type=F32, precision=lax.Precision.HIGHEST)


def _sigmoid(v):
    return 1.0 / (1.0 + jnp.exp(-v))


def _rms_scale(x):
    return lax.rsqrt(jnp.mean(x * x, axis=-1, keepdims=True) + RMS_EPS)


def _rms_bwd(x, g, dh):
    r = _rms_scale(x)
    xn = x * r
    dxn = dh * g
    dx = r * (dxn - xn * jnp.mean(dxn * xn, axis=-1, keepdims=True))
    return dx, dh * xn


def norm_matmul(x, g, w, out_dtype, name):
    t, d = x.shape
    m = w.shape[1]
    tm, tn = _rows(t, 1024), _pick(m, 1024)

    def body(x_ref, g_ref, w_ref, o_ref):
        xv = x_ref[...]
        h = (xv * _rms_scale(xv) * g_ref[...]).astype(BF16)
        o_ref[...] = _dot(h, w_ref[...]).astype(o_ref.dtype)

    return pl.pallas_call(
        body, name=name, grid=(t // tm, m // tn),
        in_specs=[pl.BlockSpec((tm, d), lambda i, j: (i, 0)),
                  pl.BlockSpec((1, d), lambda i, j: (0, 0)),
                  pl.BlockSpec((d, tn), lambda i, j: (0, j))],
        out_specs=pl.BlockSpec((tm, tn), lambda i, j: (i, j)),
        out_shape=jax.ShapeDtypeStruct((t, m), out_dtype),
        compiler_params=_params(("parallel", "arbitrary")),
    )(x, g, w)


def matmul_tn(a, b, name, a_kind="plain", gain=None, shard_axis=None, out_dtype=F32, tm_pref=1024, tn_pref=1536,
              tk_pref=1024):
    t, ma = a.shape
    nb = b.shape[1]
    tm = ma if a_kind == "norm" else _pick(ma, tm_pref)
    tn = _pick(nb, tn_pref)
    tk = _rows(t, tk_pref)
    nk = t // tk
    if shard_axis == 0:
        per = tm * 4 // ma
        assert per >= 1 and (ma // 4) * per == tm, (ma, tm)
        out_shape, out_spec = (4, ma // 4, nb), pl.BlockSpec((per, ma // 4, tn), lambda i, j, k: (i, 0, j))
    elif shard_axis == 1:
        per = tn * 4 // nb
        assert per >= 1 and (nb // 4) * per == tn, (nb, tn)
        out_shape, out_spec = (4, ma, nb // 4), pl.BlockSpec((per, tm, nb // 4), lambda i, j, k: (j, i, 0))
    else:
        out_shape, out_spec = (ma, nb), pl.BlockSpec((tm, tn), lambda i, j, k: (i, j))

    def body(*refs):
        if a_kind == "norm":
            a_ref, g_ref, b_ref, o_ref, acc = refs
        else:
            a_ref, b_ref, o_ref, acc = refs
        k = pl.program_id(2)

        @pl.when(k == 0)
        def _():
            acc[...] = jnp.zeros_like(acc)

        av = a_ref[...]
        if a_kind == "norm":
            av = av * _rms_scale(av) * g_ref[...]
        elif a_kind == "relu2":
            av = jnp.square(jnp.maximum(av.astype(F32), 0.0))
        acc[...] += _dot(av.astype(BF16), b_ref[...].astype(BF16), TN)

        @pl.when(k == nk - 1)
        def _():
            if shard_axis == 0:
                o_ref[...] = acc[...].reshape(o_ref.shape).astype(o_ref.dtype)
            elif shard_axis == 1:
                cs = nb // 4
                for n in range(o_ref.shape[0]):
                    o_ref[n] = acc[:, n * cs:(n + 1) * cs].astype(o_ref.dtype)
            else:
                o_ref[...] = acc[...].astype(o_ref.dtype)

    in_specs = [pl.BlockSpec((tk, tm), lambda i, j, k: (k, i))]
    args = [a]
    if a_kind == "norm":
        in_specs.append(pl.BlockSpec((1, ma), lambda i, j, k: (0, 0)))
        args.append(gain)
    in_specs.append(pl.BlockSpec((tk, tn), lambda i, j, k: (k, j)))
    args.append(b)
    return pl.pallas_call(
        body, name=name, grid=(ma // tm, nb // tn, nk),
        in_specs=in_specs,
        out_specs=out_spec,
        out_shape=jax.ShapeDtypeStruct(out_shape, out_dtype),
        scratch_shapes=[pltpu.VMEM((tm, tn), F32)],
        compiler_params=_params(("parallel", "parallel", "arbitrary")),
    )(*args)


def _tri(n, upper):
    r = lax.broadcasted_iota(jnp.int32, (n, n), 0)
    c = lax.broadcasted_iota(jnp.int32, (n, n), 1)
    return jnp.where((c >= r) if upper else (c <= r), 1.0, 0.0).astype(F32)


def _head_rows():
    r = lax.broadcasted_iota(jnp.int32, (SUBLANES, LANES), 0)
    c = lax.broadcasted_iota(jnp.int32, (SUBLANES, LANES), 1)
    return jnp.where(r == c, 1.0, 0.0).astype(F32)


def forget_cumsum(rest, bf, bsz, seq, f_blk):
    tc = _rows(seq, 512)
    nc = seq // tc

    def body(f_ref, b_ref, col_ref, row_ref, carry):
        c = pl.program_id(1)

        @pl.when(c == 0)
        def _():
            carry[...] = jnp.zeros_like(carry)

        z = f_ref[...] + b_ref[...]
        logf = jnp.minimum(z, 0.0) - jnp.log(1.0 + jnp.exp(-jnp.abs(z)))
        cum = _dot_exact(_tri(tc, False), logf, NN) + carry[0:1, :]
        col_ref[0] = cum
        row_ref[0] = _dot_exact(_head_rows(), cum, NT)
        carry[...] = jnp.broadcast_to(cum[tc - 1:tc, :], carry.shape)

    return pl.pallas_call(
        body, name="forget_cumsum", grid=(bsz, nc),
        in_specs=[pl.BlockSpec((tc, F_PAD), lambda b, c: (b * nc + c, f_blk)),
                  pl.BlockSpec((1, F_PAD), lambda b, c: (0, 0))],
        out_specs=[pl.BlockSpec((1, tc, LANES), lambda b, c: (b, c, 0)),
                   pl.BlockSpec((1, SUBLANES, tc), lambda b, c: (b, 0, c))],
        out_shape=[jax.ShapeDtypeStruct((bsz, seq, LANES), F32),
                   jax.ShapeDtypeStruct((bsz, SUBLANES, seq), F32)],
        scratch_shapes=[pltpu.VMEM((SUBLANES, LANES), F32)],
        compiler_params=_params(("parallel", "arbitrary")),
    )(rest, bf)


def forget_bwd(dcq, dcp, rest, bf, bsz, seq, f_blk):
    tc = _attn_tile(seq)
    nc = seq // tc

    def body(dq_ref, dc_ref, f_ref, b_ref, df_ref, db_ref, carry):
        b = pl.program_id(0)
        c = pl.program_id(1)

        @pl.when(c == 0)
        def _():
            carry[...] = jnp.zeros_like(carry)

        @pl.when((b == 0) & (c == 0))
        def _():
            db_ref[...] = jnp.zeros_like(db_ref)

        row = lax.broadcasted_iota(jnp.int32, (SUBLANES, tc), 0)
        heads = jnp.zeros((SUBLANES, tc), F32)
        dc = jnp.zeros((tc, LANES), F32)
        for p in range(HEAD_PAIRS):
            blk = dq_ref[0, p, 0]
            heads = heads + jnp.where(row == 2 * p, blk[0:1], 0.0) + jnp.where(row == 2 * p + 1, blk[1:2], 0.0)
            dc = dc + dc_ref[0, p]
        dc = dc + jnp.concatenate([heads, jnp.zeros((LANES - SUBLANES, tc), F32)], axis=0).T
        dlogf = _dot_exact(_tri(tc, True), dc, NN) + carry[0:1, :]
        carry[...] = jnp.broadcast_to(dlogf[0:1, :], carry.shape)
        z = f_ref[...] + b_ref[...]
        lane = lax.broadcasted_iota(jnp.int32, z.shape, 1)
        df = jnp.where(lane < ATTN_HEADS, dlogf * _sigmoid(-z), 0.0)
        df_ref[...] = df.astype(df_ref.dtype)
        db_ref[...] += _fold8(df)

    return pl.pallas_call(
        body, name="forget_bwd", grid=(bsz, nc),
        in_specs=[pl.BlockSpec((1, HEAD_PAIRS, 1, SUBLANES, tc), lambda b, c: (b, 0, nc - 1 - c, 0, 0)),
                  pl.BlockSpec((1, HEAD_PAIRS, tc, LANES), lambda b, c: (b, 0, nc - 1 - c, 0)),
                  pl.BlockSpec((tc, F_PAD), lambda b, c: (b * nc + nc - 1 - c, f_blk)),
                  pl.BlockSpec((1, F_PAD), lambda b, c: (0, 0))],
        out_specs=[pl.BlockSpec((tc, F_PAD), lambda b, c: (b * nc + nc - 1 - c, 0)),
                   pl.BlockSpec((SUBLANES, F_PAD), lambda b, c: (0, 0))],
        out_shape=[jax.ShapeDtypeStruct((bsz * seq, F_PAD), BF16),
                   jax.ShapeDtypeStruct((SUBLANES, F_PAD), F32)],
        scratch_shapes=[pltpu.VMEM((SUBLANES, LANES), F32)],
        compiler_params=_params(("arbitrary", "arbitrary")),
    )(dcq, dcp, rest, bf)


def _attn_tile(seq):
    return 512 if seq >= 2048 else 128


def _lane_head(shape, par):
    lane = lax.broadcasted_iota(jnp.int32, shape, len(shape) - 1)
    return (lane >= HEAD_DIM) if par else (lane < HEAD_DIM)


def _pick_lane(block, idx):
    lane = lax.broadcasted_iota(jnp.int32, block.shape, 1)
    return jnp.sum(jnp.where(lane == idx, block, 0.0), axis=1, keepdims=True)


def _pair_rows(lo_lane, hi_lane):
    r = lax.broadcasted_iota(jnp.int32, (SUBLANES, LANES), 0)
    c = lax.broadcasted_iota(jnp.int32, (SUBLANES, LANES), 1)
    if lo_lane is None:
        sel = ((r == 0) & (c < HEAD_DIM)) | ((r == 1) & (c >= HEAD_DIM))
    else:
        sel = ((r == 0) & (c == lo_lane)) | ((r == 1) & (c == hi_lane))
    return jnp.where(sel, 1.0, 0.0).astype(F32)


def _causal(s, transposed):
    r = lax.broadcasted_iota(jnp.int32, s.shape, 0)
    c = lax.broadcasted_iota(jnp.int32, s.shape, 1)
    return jnp.where((c >= r) if transposed else (r >= c), s, NEG_BIG)


def _causal_pairs(n, key_major):
    if key_major:
        pairs = [(i, j) for j in range(n) for i in range(j, n)]
    else:
        pairs = [(i, j) for i in range(n) for j in range(i + 1)]
    return (jnp.asarray([p[0] for p in pairs], jnp.int32), jnp.asarray([p[1] for p in pairs], jnp.int32))


def fox_fwd(qkv, cumrow, bsz, seq, carries=()):
    tq = _attn_tile(seq)
    nq = seq // tq
    scale = HEAD_DIM ** -0.5
    kb, vb = ATTN_WIDTH // LANES, 2 * ATTN_WIDTH // LANES
    qi_tab, kj_tab = _causal_pairs(nq, False)
    npairs = int(qi_tab.shape[0])
    sends, landings, n_sems, build = _carry_plan(carries)
    nc = len(sends)

    def body(qi_ref, kj_ref, q_ref, k_ref, v_ref, cr_ref, *rest):
        ins, (o_ref, lr_ref), lands = rest[:nc], rest[nc:nc + 2], rest[nc + 2:2 * nc + 2]
        (m_s, acc_s), sems = rest[2 * nc + 2:2 * nc + 4], rest[2 * nc + 4:]
        hp = pl.program_id(1)
        qi = qi_ref[pl.program_id(2)]
        kj = kj_ref[pl.program_id(2)]
        if nc:
            @pl.when((pl.program_id(0) == 0) & (hp == 0) & (pl.program_id(2) == 0))
            def _():
                for cp in build(ins, lands, sems):
                    cp.start()

        @pl.when(kj == 0)
        def _():
            m_s[...] = jnp.full_like(m_s, NEG_BIG)
            acc_s[...] = jnp.zeros_like(acc_s)

        def step(diag):
            q = q_ref[...]
            k = k_ref[...]
            v = v_ref[...]
            for par in range(2):
                sel = _lane_head(q.shape, par)
                qh = jnp.where(sel, q, 0.0) * scale
                s = _dot(qh.astype(BF16), k, NT) - cr_ref[0, pl.ds(2 * hp + par, 1), :]
                if diag:
                    s = _causal(s, False)
                m_prev = m_s[par]
                m_new = jnp.maximum(m_prev, jnp.max(s, axis=1, keepdims=True))
                p = jnp.exp(s - m_new).astype(BF16)
                acc_s[par] = jnp.exp(m_prev - m_new) * acc_s[par] + _dot(p, jnp.where(sel, v, 1.0).astype(BF16))
                m_s[par] = m_new

        @pl.when(kj < qi)
        def _():
            step(False)

        @pl.when(kj == qi)
        def _():
            step(True)
            lo = _lane_head((tq, LANES), 0)
            sums = [pltpu.roll(acc_s[par], HEAD_DIM, 1) for par in range(2)]
            out = jnp.where(lo, acc_s[0] / sums[0], acc_s[1] / sums[1])
            o_ref[...] = out.astype(o_ref.dtype)
            lse = jnp.where(lo, m_s[0] + jnp.log(sums[0]), m_s[1] + jnp.log(sums[1]))
            lr_ref[0, 0] = _dot_exact(_pair_rows(0, HEAD_DIM), lse, NT)

        if nc:
            @pl.when((pl.program_id(0) == bsz - 1) & (hp == HEAD_PAIRS - 1) & (pl.program_id(2) == npairs - 1))
            def _():
                for cp in build(ins, lands, sems):
                    cp.wait()

    sem_shapes = [pltpu.SemaphoreType.DMA((n_sems,)), pltpu.SemaphoreType.DMA((n_sems,))] if nc else []
    outs = pl.pallas_call(
        body, name="fox_fwd",
        grid_spec=pltpu.PrefetchScalarGridSpec(
            num_scalar_prefetch=2, grid=(bsz, HEAD_PAIRS, npairs),
            in_specs=[pl.BlockSpec((tq, LANES), lambda b, h, t, qi, kj: (b * nq + qi[t], h)),
                      pl.BlockSpec((tq, LANES), lambda b, h, t, qi, kj: (b * nq + kj[t], kb + h)),
                      pl.BlockSpec((tq, LANES), lambda b, h, t, qi, kj: (b * nq + kj[t], vb + h)),
                      pl.BlockSpec((1, SUBLANES, tq), lambda b, h, t, qi, kj: (b, 0, kj[t]))] + [ANY] * nc,
            out_specs=[pl.BlockSpec((tq, LANES), lambda b, h, t, qi, kj: (b * nq + qi[t], h)),
                       pl.BlockSpec((1, 1, SUBLANES, tq), lambda b, h, t, qi, kj: (b, h, 0, qi[t]))] + [ANY] * nc,
            scratch_shapes=[pltpu.VMEM((2, tq, 1), F32), pltpu.VMEM((2, tq, LANES), F32)] + sem_shapes),
        out_shape=[jax.ShapeDtypeStruct((bsz * seq, ATTN_WIDTH), BF16),
                   jax.ShapeDtypeStruct((bsz, HEAD_PAIRS, SUBLANES, seq), F32)] + landings,
        compiler_params=_params(("arbitrary", "arbitrary", "arbitrary")),
    )(qi_tab, kj_tab, qkv, qkv, qkv, cumrow, *sends)
    return outs[0], outs[1], list(outs[2:])


def fox_bwd_kernel(qkv, do, o, lser, cumcol, bsz, seq, carries=()):
    tk = _attn_tile(seq)
    nk = seq // tk
    scale = HEAD_DIM ** -0.5
    kb, vb = ATTN_WIDTH // LANES, 2 * ATTN_WIDTH // LANES
    qi_tab, kj_tab = _causal_pairs(nk, True)
    npairs = int(qi_tab.shape[0])
    sends, landings, n_sems, build = _carry_plan(carries)
    nc = len(sends)

    def body(qi_ref, kj_ref, q_ref, k_ref, v_ref, do_ref, o_ref, lr_ref, cc_ref, *rest):
        ins, lands = rest[:nc], rest[nc + 5:2 * nc + 5]
        dq_ref, dk_ref, dv_ref, dc_ref, dr_ref = rest[nc:nc + 5]
        (dq_s, dk_s, dv_s, dc_s, dr_s), sems = rest[2 * nc + 5:2 * nc + 10], rest[2 * nc + 10:]
        hp = pl.program_id(1)
        t = pl.program_id(2)
        qi = qi_ref[t]
        kj = kj_ref[t]
        if nc:
            @pl.when((pl.program_id(0) == 0) & (hp == 0) & (t == 0))
            def _():
                for cp in build(ins, lands, sems):
                    cp.start()

        @pl.when(t == 0)
        def _():
            dq_s[...] = jnp.zeros_like(dq_s)
            dr_s[...] = jnp.zeros_like(dr_s)

        @pl.when(qi == kj)
        def _():
            dk_s[...] = jnp.zeros_like(dk_s)
            dv_s[...] = jnp.zeros_like(dv_s)
            dc_s[...] = jnp.zeros_like(dc_s)

        def step(diag):
            q = q_ref[...]
            k = k_ref[...]
            v = v_ref[...]
            dov = do_ref[...]
            prod = dov.astype(F32) * o_ref[...].astype(F32)
            drow = _dot_exact(_pair_rows(None, None), prod, NT)
            lrow = lr_ref[0, 0]
            lane = lax.broadcasted_iota(jnp.int32, (tk, LANES), 1)
            for par in range(2):
                head = 2 * hp + par
                sel = _lane_head(k.shape, par)
                kh = (jnp.where(sel, k, 0.0) * scale).astype(BF16)
                st = _dot(kh, q, NT) - _pick_lane(cc_ref[0], head)
                if diag:
                    st = _causal(st, True)
                pt = jnp.exp(st - lrow[par:par + 1, :])
                vh = jnp.where(sel, v, 0.0)
                dpt = _dot(vh.astype(BF16), dov, NT)
                dst = pt * (dpt - drow[par:par + 1, :])
                dsb = dst.astype(BF16)
                dv_s[...] += jnp.where(sel, _dot(pt.astype(BF16), dov), 0.0)
                dk_s[...] += jnp.where(sel, _dot(dsb, q), 0.0)
                dq_s[qi] += _dot(dsb, kh, TN)
                dc_s[...] += jnp.where(lane == head, -jnp.sum(dst, axis=1, keepdims=True), 0.0)
                dr_s[qi, par:par + 1, :] += jnp.sum(dst, axis=0, keepdims=True)

        @pl.when(qi > kj)
        def _():
            step(False)

        @pl.when(qi == kj)
        def _():
            step(True)

        @pl.when(qi == nk - 1)
        def _():
            dk_ref[...] = (dk_s[...] * scale).astype(dk_ref.dtype)
            dv_ref[...] = dv_s[...].astype(dv_ref.dtype)
            dc_ref[0, 0] = dc_s[...]

        @pl.when(t == npairs - 1)
        def _():
            for i in range(nk):
                dq_ref[i * tk:(i + 1) * tk, :] = dq_s[i].astype(dq_ref.dtype)
            dr_ref[0, 0] = dr_s[...]

        if nc:
            @pl.when((pl.program_id(0) == bsz - 1) & (hp == HEAD_PAIRS - 1) & (t == npairs - 1))
            def _():
                for cp in build(ins, lands, sems):
                    cp.wait()

    sem_shapes = [pltpu.SemaphoreType.DMA((n_sems,)), pltpu.SemaphoreType.DMA((n_sems,))] if nc else []
    outs = pl.pallas_call(
        body, name="fox_bwd",
        grid_spec=pltpu.PrefetchScalarGridSpec(
            num_scalar_prefetch=2, grid=(bsz, HEAD_PAIRS, npairs),
            in_specs=[pl.BlockSpec((tk, LANES), lambda b, h, t, qi, kj: (b * nk + qi[t], h)),
                      pl.BlockSpec((tk, LANES), lambda b, h, t, qi, kj: (b * nk + kj[t], kb + h)),
                      pl.BlockSpec((tk, LANES), lambda b, h, t, qi, kj: (b * nk + kj[t], vb + h)),
                      pl.BlockSpec((tk, LANES), lambda b, h, t, qi, kj: (b * nk + qi[t], h)),
                      pl.BlockSpec((tk, LANES), lambda b, h, t, qi, kj: (b * nk + qi[t], h)),
                      pl.BlockSpec((1, 1, SUBLANES, tk), lambda b, h, t, qi, kj: (b, h, 0, qi[t])),
                      pl.BlockSpec((1, tk, LANES), lambda b, h, t, qi, kj: (b, kj[t], 0))] + [ANY] * nc,
            out_specs=[pl.BlockSpec((seq, LANES), lambda b, h, t, qi, kj: (b, h)),
                       pl.BlockSpec((tk, LANES), lambda b, h, t, qi, kj: (b * nk + kj[t], h)),
                       pl.BlockSpec((tk, LANES), lambda b, h, t, qi, kj: (b * nk + kj[t], h)),
                       pl.BlockSpec((1, 1, tk, LANES), lambda b, h, t, qi, kj: (b, h, kj[t], 0)),
                       pl.BlockSpec((1, 1, nk, SUBLANES, tk), lambda b, h, t, qi, kj: (b, h, 0, 0, 0))] + [ANY] * nc,
            scratch_shapes=[pltpu.VMEM((nk, tk, LANES), F32), pltpu.VMEM((tk, LANES), F32),
                            pltpu.VMEM((tk, LANES), F32), pltpu.VMEM((tk, LANES), F32),
                            pltpu.VMEM((nk, SUBLANES, tk), F32)] + sem_shapes),
        out_shape=[jax.ShapeDtypeStruct((bsz * seq, ATTN_WIDTH), BF16),
                   jax.ShapeDtypeStruct((bsz * seq, ATTN_WIDTH), BF16),
                   jax.ShapeDtypeStruct((bsz * seq, ATTN_WIDTH), BF16),
                   jax.ShapeDtypeStruct((bsz, HEAD_PAIRS, seq, LANES), F32),
                   jax.ShapeDtypeStruct((bsz, HEAD_PAIRS, nk, SUBLANES, tk), F32)] + landings,
        compiler_params=_params(("arbitrary", "arbitrary", "arbitrary")),
    )(qi_tab, kj_tab, qkv, qkv, qkv, do, o, lser, cumcol, *sends)
    return outs[:5], list(outs[5:])


def fox_bwd(qkv, do, o, lser, cumcol, rest, bf, bsz, seq, f_blk, carries=()):
    (dq, dk, dv, dcp, dcq), lands = fox_bwd_kernel(qkv, do, o, lser, cumcol, bsz, seq, carries)
    df, dbf = forget_bwd(dcq, dcp, rest, bf, bsz, seq, f_blk)
    return dq, dk, dv, df, dbf, lands


SCAN_STEPS = (1, 2, 4)
TAB_FWD = 0
TAB_BWD = 32
TAB_CARRY = 24
TAB_ROWS = 64


def _ssm_tile(seq):
    return 256 if seq >= 1024 else 64


def _scan_block(xr, xi, tab_ref, re, im, cr, ci, reverse):
    base = TAB_BWD if reverse else TAB_FWD
    for n, s in enumerate(SCAN_STEPS):
        ar = tab_ref[base + n * SUBLANES:base + (n + 1) * SUBLANES, re]
        ai = tab_ref[base + n * SUBLANES:base + (n + 1) * SUBLANES, im]
        shift = SUBLANES - s if reverse else s
        sr = pltpu.roll(xr, shift, 0)
        si = pltpu.roll(xi, shift, 0)
        xr, xi = xr + ar * sr - ai * si, xi + ar * si + ai * sr
    pr = tab_ref[base + TAB_CARRY:base + TAB_CARRY + SUBLANES, re]
    pi = tab_ref[base + TAB_CARRY:base + TAB_CARRY + SUBLANES, im]
    xr, xi = xr + pr * cr - pi * ci, xi + pr * ci + pi * cr
    return xr, xi


def ssm_fwd(rest, wb4, wc4, tabs, dskip, bsz, seq, u_blk):
    tt = _ssm_tile(seq)
    nt = seq // tt

    def body(u_ref, wb_ref, wc_ref, tab_ref, d_ref, y_ref, h_ref, carry):
        c = pl.program_id(1)

        @pl.when(c == 0)
        def _():
            carry[...] = jnp.zeros_like(carry)

        u = u_ref[...]
        ub = u.astype(BF16)
        for j in range(SSM_CHUNKS):
            h_ref[:, j * CHUNK_LANES:(j + 1) * CHUNK_LANES] = _dot(ub[:, j * LANES:(j + 1) * LANES], wb_ref[j])
        for j in range(SSM_CHUNKS):
            re = slice(j * CHUNK_LANES, j * CHUNK_LANES + CHUNK_STATES)
            im = slice(j * CHUNK_LANES + CHUNK_STATES, (j + 1) * CHUNK_LANES)

            def blk(bi, car):
                r0 = pl.multiple_of(bi * SUBLANES, SUBLANES)
                xr, xi = _scan_block(h_ref[pl.ds(r0, SUBLANES), re], h_ref[pl.ds(r0, SUBLANES), im],
                                     tab_ref, re, im, car[0], car[1], False)
                h_ref[pl.ds(r0, SUBLANES), re] = xr
                h_ref[pl.ds(r0, SUBLANES), im] = xi
                return xr[SUBLANES - 1:SUBLANES], xi[SUBLANES - 1:SUBLANES]

            cr, ci = lax.fori_loop(0, tt // SUBLANES, blk, (carry[0:1, re], carry[0:1, im]), unroll=2)
            carry[0:1, re] = cr
            carry[0:1, im] = ci
        for j in range(SSM_CHUNKS):
            hj = h_ref[:, j * CHUNK_LANES:(j + 1) * CHUNK_LANES].astype(BF16)
            cols = slice(j * LANES, (j + 1) * LANES)
            y_ref[:, cols] = _dot(hj, wc_ref[j]) + d_ref[:, cols] * u[:, cols]

    return pl.pallas_call(
        body, name="ssm_fwd", grid=(bsz, nt),
        in_specs=[pl.BlockSpec((tt, SSM_WIDTH), lambda b, c: (b * nt + c, u_blk)),
                  pl.BlockSpec((SSM_CHUNKS, LANES, CHUNK_LANES), lambda b, c: (0, 0, 0)),
                  pl.BlockSpec((SSM_CHUNKS, CHUNK_LANES, LANES), lambda b, c: (0, 0, 0)),
                  pl.BlockSpec((TAB_ROWS, STATE_LANES), lambda b, c: (0, 0)),
                  pl.BlockSpec((1, SSM_WIDTH), lambda b, c: (0, 0))],
        out_specs=[pl.BlockSpec((tt, SSM_WIDTH), lambda b, c: (b * nt + c, 0)),
                   pl.BlockSpec((tt, STATE_LANES), lambda b, c: (b * nt + c, 0))],
        out_shape=[jax.ShapeDtypeStruct((bsz * seq, SSM_WIDTH), F32),
                   jax.ShapeDtypeStruct((bsz * seq, STATE_LANES), F32)],
        scratch_shapes=[pltpu.VMEM((SUBLANES, STATE_LANES), F32)],
        compiler_params=_params(("parallel", "arbitrary")),
    )(rest, wb4, wc4, tabs, dskip)


def ssm_bwd(dys, rest, hs, wb4, wc4, tabs, dskip, bsz, seq, u_blk):
    tt = _ssm_tile(seq)
    nt = seq // tt
    nb = tt // SUBLANES

    def body(dy_ref, u_ref, h_ref, hp_ref, wb_ref, wc_ref, tab_ref, d_ref,
             du_ref, ga_ref, gwb_ref, gwc_ref, gd_ref, g_s, carry):
        b = pl.program_id(0)
        c = pl.program_id(1)

        @pl.when(c == 0)
        def _():
            carry[...] = jnp.zeros_like(carry)

        @pl.when((b == 0) & (c == 0))
        def _():
            ga_ref[...] = jnp.zeros_like(ga_ref)
            gwb_ref[...] = jnp.zeros_like(gwb_ref)
            gwc_ref[...] = jnp.zeros_like(gwc_ref)
            gd_ref[...] = jnp.zeros_like(gd_ref)

        dy = dy_ref[...].astype(F32)
        dyb = dy.astype(BF16)
        u = u_ref[...]
        ub = u.astype(BF16)
        first_chunk = c == nt - 1
        for j in range(SSM_CHUNKS):
            g_s[:, j * CHUNK_LANES:(j + 1) * CHUNK_LANES] = _dot(dyb[:, j * LANES:(j + 1) * LANES], wc_ref[j], NT)
        for j in range(SSM_CHUNKS):
            re = slice(j * CHUNK_LANES, j * CHUNK_LANES + CHUNK_STATES)
            im = slice(j * CHUNK_LANES + CHUNK_STATES, (j + 1) * CHUNK_LANES)
            row = lax.broadcasted_iota(jnp.int32, (SUBLANES, CHUNK_STATES), 0)

            def blk(n, car):
                bi = nb - 1 - n
                r0 = pl.multiple_of(bi * SUBLANES, SUBLANES)
                gr, gi = _scan_block(g_s[pl.ds(r0, SUBLANES), re], g_s[pl.ds(r0, SUBLANES), im],
                                     tab_ref, re, im, car[0], car[1], True)
                g_s[pl.ds(r0, SUBLANES), re] = gr
                g_s[pl.ds(r0, SUBLANES), im] = gi
                rp = pl.multiple_of(jnp.maximum(bi - 1, 0) * SUBLANES, SUBLANES)
                inside = bi > 0
                live = jnp.where(jnp.logical_or(inside, jnp.logical_not(first_chunk)), 1.0, 0.0)
                pr = jnp.where(inside, h_ref[pl.ds(rp, SUBLANES), re], hp_ref[:, re])[SUBLANES - 1:SUBLANES] * live
                pi = jnp.where(inside, h_ref[pl.ds(rp, SUBLANES), im], hp_ref[:, im])[SUBLANES - 1:SUBLANES] * live
                hr = jnp.where(row >= 1, pltpu.roll(h_ref[pl.ds(r0, SUBLANES), re], 1, 0), pr)
                hi = jnp.where(row >= 1, pltpu.roll(h_ref[pl.ds(r0, SUBLANES), im], 1, 0), pi)
                return (gr[0:1], gi[0:1], car[2] + gr * hr + gi * hi, car[3] + gi * hr - gr * hi)

            zero = jnp.zeros((SUBLANES, CHUNK_STATES), F32)
            cr, ci, sr, si = lax.fori_loop(0, nb, blk, (carry[0:1, re], carry[0:1, im], zero, zero), unroll=2)
            carry[0:1, re] = cr
            carry[0:1, im] = ci
            ga_ref[:, re] += sr
            ga_ref[:, im] += si
        for j in range(SSM_CHUNKS):
            cols = slice(j * LANES, (j + 1) * LANES)
            lanes = slice(j * CHUNK_LANES, (j + 1) * CHUNK_LANES)
            gj = g_s[:, lanes].astype(BF16)
            du_ref[:, cols] = (_dot(gj, wb_ref[j], NT) + d_ref[:, cols] * dy[:, cols]).astype(du_ref.dtype)
            gwb_ref[j] += _dot(ub[:, cols], gj, TN)
            gwc_ref[j] += _dot(h_ref[:, lanes].astype(BF16), dyb[:, cols], TN)
        gd_ref[...] += _fold8(dy * u)

    def prev_rows(b, c):
        chunk = nt - 1 - c
        return (jnp.maximum((b * nt + chunk) * nb - 1, 0), 0)

    return pl.pallas_call(
        body, name="ssm_bwd", grid=(bsz, nt),
        in_specs=[pl.BlockSpec((tt, SSM_WIDTH), lambda b, c: (b * nt + nt - 1 - c, 0)),
                  pl.BlockSpec((tt, SSM_WIDTH), lambda b, c: (b * nt + nt - 1 - c, u_blk)),
                  pl.BlockSpec((tt, STATE_LANES), lambda b, c: (b * nt + nt - 1 - c, 0)),
                  pl.BlockSpec((SUBLANES, STATE_LANES), prev_rows),
                  pl.BlockSpec((SSM_CHUNKS, LANES, CHUNK_LANES), lambda b, c: (0, 0, 0)),
                  pl.BlockSpec((SSM_CHUNKS, CHUNK_LANES, LANES), lambda b, c: (0, 0, 0)),
                  pl.BlockSpec((TAB_ROWS, STATE_LANES), lambda b, c: (0, 0)),
                  pl.BlockSpec((1, SSM_WIDTH), lambda b, c: (0, 0))],
        out_specs=[pl.BlockSpec((tt, SSM_WIDTH), lambda b, c: (b * nt + nt - 1 - c, 0)),
                   pl.BlockSpec((SUBLANES, STATE_LANES), lambda b, c: (0, 0)),
                   pl.BlockSpec((SSM_CHUNKS, LANES, CHUNK_LANES), lambda b, c: (0, 0, 0)),
                   pl.BlockSpec((SSM_CHUNKS, CHUNK_LANES, LANES), lambda b, c: (0, 0, 0)),
                   pl.BlockSpec((SUBLANES, SSM_WIDTH), lambda b, c: (0, 0))],
        out_shape=[jax.ShapeDtypeStruct((bsz * seq, SSM_WIDTH), BF16),
                   jax.ShapeDtypeStruct((SUBLANES, STATE_LANES), F32),
                   jax.ShapeDtypeStruct((SSM_CHUNKS, LANES, CHUNK_LANES), F32),
                   jax.ShapeDtypeStruct((SSM_CHUNKS, CHUNK_LANES, LANES), F32),
                   jax.ShapeDtypeStruct((SUBLANES, SSM_WIDTH), F32)],
        scratch_shapes=[pltpu.VMEM((tt, STATE_LANES), F32), pltpu.VMEM((SUBLANES, STATE_LANES), F32)],
        compiler_params=_params(("arbitrary", "arbitrary")),
    )(dys, rest, hs, hs, wb4, wc4, tabs, dskip)


def _gelu(v):
    t = jnp.tanh(GELU_C * (v + GELU_A * v * v * v))
    return 0.5 * v * (1.0 + t), t


def mix_fwd(ya, ys, rest, x0, wglu, bglu, wba, wbb, wout):
    t, d = x0.shape
    tm = _rows(t, 512)

    def body(ya_ref, ys_ref, ga_ref, gb_ref, x_ref, wg_ref, bg_ref, wa_ref, wb_ref, wo_ref,
             x1_ref, z_ref, pa_ref, pb_ref, yb_ref, yb2_ref, mx_ref):
        yb, _ = _gelu(ys_ref[...])
        ybb = yb.astype(BF16)
        z = _dot(ybb, wg_ref[...]) + bg_ref[...]
        yb2 = (yb * _sigmoid(z)).astype(BF16)
        pa = _dot(ya_ref[...], wa_ref[...])
        pb = _dot(yb2, wb_ref[...])
        mixed = (_sigmoid(ga_ref[...]) * pa + _sigmoid(gb_ref[...]) * pb).astype(BF16)
        x1_ref[...] = x_ref[...] + _dot(mixed, wo_ref[...])
        z_ref[...] = z.astype(z_ref.dtype)
        pa_ref[...] = pa.astype(pa_ref.dtype)
        pb_ref[...] = pb.astype(pb_ref.dtype)
        yb_ref[...] = ybb
        yb2_ref[...] = yb2
        mx_ref[...] = mixed

    row = lambda w: pl.BlockSpec((tm, w), lambda i: (i, 0))
    full = lambda a: pl.BlockSpec(a.shape, lambda i: (0,) * a.ndim)
    return pl.pallas_call(
        body, name="mix_fwd", grid=(t // tm,),
        in_specs=[row(ATTN_WIDTH), row(SSM_WIDTH),
                  pl.BlockSpec((tm, d), lambda i: (i, 0)), pl.BlockSpec((tm, d), lambda i: (i, 1)),
                  row(d), full(wglu), full(bglu), full(wba), full(wbb), full(wout)],
        out_specs=[row(d), row(SSM_WIDTH), row(d), row(d), row(SSM_WIDTH), row(SSM_WIDTH), row(d)],
        out_shape=[jax.ShapeDtypeStruct((t, d), F32), jax.ShapeDtypeStruct((t, SSM_WIDTH), BF16),
                   jax.ShapeDtypeStruct((t, d), BF16), jax.ShapeDtypeStruct((t, d), BF16),
                   jax.ShapeDtypeStruct((t, SSM_WIDTH), BF16), jax.ShapeDtypeStruct((t, SSM_WIDTH), BF16),
                   jax.ShapeDtypeStruct((t, d), BF16)],
        compiler_params=_params(("parallel",)),
    )(ya, ys, rest, rest, x0, wglu, bglu, wba, wbb, wout)


def mix_bwd(dx1, rest, pa, pb, z, ys, wglu, wba, wbb, wout):
    t, d = dx1.shape
    tm = _rows(t, 512)

    def body(dx_ref, ga_ref, gb_ref, pa_ref, pb_ref, z_ref, ys_ref, wg_ref, wa_ref, wb_ref, wo_ref,
             dya_ref, dys_ref, dg_ref, dpa_ref, dpb_ref, dz_ref, dbg_ref):
        @pl.when(pl.program_id(0) == 0)
        def _():
            dbg_ref[...] = jnp.zeros_like(dbg_ref)

        dmix = _dot(dx_ref[...].astype(BF16), wo_ref[...], NT)
        sa = _sigmoid(ga_ref[...])
        sb = _sigmoid(gb_ref[...])
        dpa = (dmix * sa).astype(BF16)
        dpb = (dmix * sb).astype(BF16)
        dg_ref[:, 0:d] = (dmix * pa_ref[...].astype(F32) * sa * (1.0 - sa)).astype(dg_ref.dtype)
        dg_ref[:, d:2 * d] = (dmix * pb_ref[...].astype(F32) * sb * (1.0 - sb)).astype(dg_ref.dtype)
        dpa_ref[...] = dpa
        dpb_ref[...] = dpb
        dya_ref[...] = _dot(dpa, wa_ref[...], NT).astype(dya_ref.dtype)
        dyb2 = _dot(dpb, wb_ref[...], NT)
        ys = ys_ref[...]
        yb, th = _gelu(ys)
        sg = _sigmoid(z_ref[...].astype(F32))
        dz = dyb2 * yb * sg * (1.0 - sg)
        dzb = dz.astype(BF16)
        dz_ref[...] = dzb
        dbg_ref[...] += _fold8(dz)
        dyb = dyb2 * sg + _dot(dzb, wg_ref[...], NT)
        dgelu = 0.5 * (1.0 + th) + 0.5 * ys * (1.0 - th * th) * GELU_C * (1.0 + 3.0 * GELU_A * ys * ys)
        dys_ref[...] = (dyb * dgelu).astype(dys_ref.dtype)

    row = lambda w: pl.BlockSpec((tm, w), lambda i: (i, 0))
    full = lambda a: pl.BlockSpec(a.shape, lambda i: (0,) * a.ndim)
    return pl.pallas_call(
        body, name="mix_bwd", grid=(t // tm,),
        in_specs=[row(d), pl.BlockSpec((tm, d), lambda i: (i, 0)), pl.BlockSpec((tm, d), lambda i: (i, 1)),
                  row(d), row(d), row(SSM_WIDTH), row(SSM_WIDTH), full(wglu), full(wba), full(wbb), full(wout)],
        out_specs=[row(ATTN_WIDTH), row(SSM_WIDTH), row(2 * d), row(d), row(d), row(SSM_WIDTH),
                   pl.BlockSpec((SUBLANES, SSM_WIDTH), lambda i: (0, 0))],
        out_shape=[jax.ShapeDtypeStruct((t, ATTN_WIDTH), BF16), jax.ShapeDtypeStruct((t, SSM_WIDTH), BF16),
                   jax.ShapeDtypeStruct((t, 2 * d), BF16), jax.ShapeDtypeStruct((t, d), BF16),
                   jax.ShapeDtypeStruct((t, d), BF16), jax.ShapeDtypeStruct((t, SSM_WIDTH), BF16),
                   jax.ShapeDtypeStruct((SUBLANES, SSM_WIDTH), F32)],
        compiler_params=_params(("arbitrary",)),
    )(dx1, rest, rest, pa, pb, z, ys, wglu, wba, wbb, wout)


def mlp_fwd(x1, g, wup, wdown):
    t, d = x1.shape
    ff = wup.shape[1]
    tm, tf = _rows(t, 1024), _pick(ff, 1024)
    nf = ff // tf

    def body(x_ref, g_ref, wu_ref, wd_ref, x2_ref, up_ref, h_s, acc_s):
        f = pl.program_id(1)

        @pl.when(f == 0)
        def _():
            xv = x_ref[...]
            h_s[...] = (xv * _rms_scale(xv) * g_ref[...]).astype(BF16)
            acc_s[...] = jnp.zeros_like(acc_s)

        up = _dot(h_s[...], wu_ref[...])
        up_ref[...] = up.astype(up_ref.dtype)
        act = jnp.square(jnp.maximum(up, 0.0)).astype(BF16)
        acc_s[...] += _dot(act, wd_ref[...])

        @pl.when(f == nf - 1)
        def _():
            x2_ref[...] = x_ref[...] + acc_s[...]

    return pl.pallas_call(
        body, name="mlp_fwd", grid=(t // tm, nf),
        in_specs=[pl.BlockSpec((tm, d), lambda i, f: (i, 0)), pl.BlockSpec((1, d), lambda i, f: (0, 0)),
                  pl.BlockSpec((d, tf), lambda i, f: (0, f)), pl.BlockSpec((tf, d), lambda i, f: (f, 0))],
        out_specs=[pl.BlockSpec((tm, d), lambda i, f: (i, 0)), pl.BlockSpec((tm, tf), lambda i, f: (i, f))],
        out_shape=[jax.ShapeDtypeStruct((t, d), F32), jax.ShapeDtypeStruct((t, ff), BF16)],
        scratch_shapes=[pltpu.VMEM((tm, d), BF16), pltpu.VMEM((tm, d), F32)],
        compiler_params=_params(("parallel", "arbitrary")),
    )(x1, g, wup, wdown)


def mlp_bwd(dx2, up, x1, g, wup, wdown):
    t, d = x1.shape
    ff = wup.shape[1]
    tm, tf = _rows(t, 512), _pick(ff, 1024)
    nf = ff // tf

    def body(dx_ref, up_ref, x_ref, g_ref, wu_ref, wd_ref, dup_ref, dx1_ref, dg_ref, dxb_s, acc_s):
        i = pl.program_id(0)
        f = pl.program_id(1)

        @pl.when((i == 0) & (f == 0))
        def _():
            dg_ref[...] = jnp.zeros_like(dg_ref)

        @pl.when(f == 0)
        def _():
            dxb_s[...] = dx_ref[...].astype(BF16)
            acc_s[...] = jnp.zeros_like(acc_s)

        dact = _dot(dxb_s[...], wd_ref[...], NT)
        dup = (dact * 2.0 * jnp.maximum(up_ref[...].astype(F32), 0.0)).astype(BF16)
        dup_ref[...] = dup
        acc_s[...] += _dot(dup, wu_ref[...], NT)

        @pl.when(f == nf - 1)
        def _():
            dxn, dgain = _rms_bwd(x_ref[...], g_ref[...], acc_s[...])
            dx1_ref[...] = dx_ref[...] + dxn
            dg_ref[...] += _fold8(dgain)

    return pl.pallas_call(
        body, name="mlp_bwd", grid=(t // tm, nf),
        in_specs=[pl.BlockSpec((tm, d), lambda i, f: (i, 0)), pl.BlockSpec((tm, tf), lambda i, f: (i, f)),
                  pl.BlockSpec((tm, d), lambda i, f: (i, 0)), pl.BlockSpec((1, d), lambda i, f: (0, 0)),
                  pl.BlockSpec((d, tf), lambda i, f: (0, f)), pl.BlockSpec((tf, d), lambda i, f: (f, 0))],
        out_specs=[pl.BlockSpec((tm, tf), lambda i, f: (i, f)), pl.BlockSpec((tm, d), lambda i, f: (i, 0)),
                   pl.BlockSpec((SUBLANES, d), lambda i, f: (0, 0))],
        out_shape=[jax.ShapeDtypeStruct((t, ff), BF16), jax.ShapeDtypeStruct((t, d), F32),
                   jax.ShapeDtypeStruct((SUBLANES, d), F32)],
        scratch_shapes=[pltpu.VMEM((tm, d), BF16), pltpu.VMEM((tm, d), F32)],
        compiler_params=_params(("arbitrary", "arbitrary")),
    )(dx2, up, x1, g, wup, wdown)


def proj_bwd(dproj, wpad, x0, dx1, g):
    t, d = x0.shape
    m = wpad.shape[1]
    tm = _rows(t, 512)

    def body(dp_ref, w_ref, x_ref, dx1_ref, g_ref, dx0_ref, dg_ref):
        @pl.when(pl.program_id(0) == 0)
        def _():
            dg_ref[...] = jnp.zeros_like(dg_ref)

        dh = _dot(dp_ref[...], w_ref[...], NT)
        dxn, dgain = _rms_bwd(x_ref[...], g_ref[...], dh)
        dx0_ref[...] = dx1_ref[...] + dxn
        dg_ref[...] += _fold8(dgain)

    return pl.pallas_call(
        body, name="proj_bwd", grid=(t // tm,),
        in_specs=[pl.BlockSpec((tm, m), lambda i: (i, 0)), pl.BlockSpec((d, m), lambda i: (0, 0)),
                  pl.BlockSpec((tm, d), lambda i: (i, 0)), pl.BlockSpec((tm, d), lambda i: (i, 0)),
                  pl.BlockSpec((1, d), lambda i: (0, 0))],
        out_specs=[pl.BlockSpec((tm, d), lambda i: (i, 0)), pl.BlockSpec((SUBLANES, d), lambda i: (0, 0))],
        out_shape=[jax.ShapeDtypeStruct((t, d), F32), jax.ShapeDtypeStruct((SUBLANES, d), F32)],
        compiler_params=_params(("arbitrary",)),
    )(dproj, wpad, x0, dx1, g)


def final_loss(x, g, target):
    t, d = x.shape
    tm = _rows(t, 512)

    def body(x_ref, g_ref, t_ref, dx_ref, ls_ref, dg_ref):
        @pl.when(pl.program_id(0) == 0)
        def _():
            ls_ref[...] = jnp.zeros_like(ls_ref)
            dg_ref[...] = jnp.zeros_like(dg_ref)

        xv = x_ref[...]
        gv = g_ref[...]
        err = xv * _rms_scale(xv) * gv - t_ref[...]
        ls_ref[...] += _fold8(err * err) * (0.5 / d)
        dxn, dgain = _rms_bwd(xv, gv, err * (1.0 / d))
        dx_ref[...] = dxn
        dg_ref[...] += _fold8(dgain)

    return pl.pallas_call(
        body, name="final_loss", grid=(t // tm,),
        in_specs=[pl.BlockSpec((tm, d), lambda i: (i, 0)), pl.BlockSpec((1, d), lambda i: (0, 0)),
                  pl.BlockSpec((tm, d), lambda i: (i, 0))],
        out_specs=[pl.BlockSpec((tm, d), lambda i: (i, 0)), pl.BlockSpec((SUBLANES, d), lambda i: (0, 0)),
                   pl.BlockSpec((SUBLANES, d), lambda i: (0, 0))],
        out_shape=[jax.ShapeDtypeStruct((t, d), F32), jax.ShapeDtypeStruct((SUBLANES, d), F32),
                   jax.ShapeDtypeStruct((SUBLANES, d), F32)],
        compiler_params=_params(("arbitrary",)),
    )(x, g, target)


def _peers():
    x, y, c = lax.axis_index("x"), lax.axis_index("y"), lax.axis_index("c")
    peers = []
    for m in range(1, 8):
        fx, fy, fc = (m >> 2) & 1, (m >> 1) & 1, m & 1
        peers.append((m, ((1 - x) if fx else x, (1 - y) if fy else y, (1 - c) if fc else c)))
    return x, y, c, peers


def _remote(src, dst, sems, k, peer):
    return pltpu.make_async_remote_copy(src_ref=src, dst_ref=dst, send_sem=sems[0].at[k], recv_sem=sems[1].at[k],
                                        device_id=peer, device_id_type=MESH)


def _gather_copies(w_refs, out_refs, sems, first):
    x, y, c, peers = _peers()
    chip = 2 * x + y
    cps = []
    for t, (w, o) in enumerate(zip(w_refs, out_refs)):
        half = w.shape[0] // 2
        rows = pl.ds(c * half, half)
        for m, peer in peers:
            if m >> 1:
                cps.append(_remote(w.at[rows], o.at[chip, rows], sems, first + 7 * t + m - 1, peer))
    return cps


def _reduce_copies(g_refs, out_refs, sems, first):
    x, y, c, peers = _peers()
    me = 4 * x + 2 * y + c
    cps = []
    for t, (g, o) in enumerate(zip(g_refs, out_refs)):
        for m, (px, py, pc) in peers:
            cps.append(_remote(g.at[2 * px + py], o.at[me], sems, first + 7 * t + m - 1, (px, py, pc)))
    return cps


def _all_copies(s_refs, out_refs, sems, first):
    x, y, c, peers = _peers()
    me = 4 * x + 2 * y + c
    return [_remote(s, o.at[me], sems, first + 7 * t + m - 1, peer)
            for t, (s, o) in enumerate(zip(s_refs, out_refs)) for m, peer in peers]


EXCHANGES = {"gather": (_gather_copies, lambda a: (4,) + a.shape),
             "reduce": (_reduce_copies, lambda a: (8,) + a.shape[1:]),
             "all": (_all_copies, lambda a: (8,) + a.shape)}


def _carry_plan(carries):
    inputs, shapes, spans = [], [], []
    for kind, arrs in carries:
        for a in arrs:
            inputs.append(a)
            shapes.append(jax.ShapeDtypeStruct(EXCHANGES[kind][1](a), a.dtype))
        spans.append((kind, len(arrs)))

    def build(in_refs, out_refs, sems):
        cps, pos = [], 0
        for kind, cnt in spans:
            cps += EXCHANGES[kind][0](in_refs[pos:pos + cnt], out_refs[pos:pos + cnt], sems, 7 * pos)
            pos += cnt
        return cps

    return inputs, shapes, 7 * len(inputs), build


def exchange(carries, name):
    inputs, shapes, n_sems, build = _carry_plan(carries)
    n = len(inputs)

    def body(*refs):
        cps = build(refs[:n], refs[n:2 * n], refs[2 * n:])
        for cp in cps:
            cp.start()
        for cp in cps:
            cp.wait()

    return pl.pallas_call(
        body, name=name, in_specs=[ANY] * n, out_specs=[ANY] * n, out_shape=shapes,
        scratch_shapes=[pltpu.SemaphoreType.DMA((n_sems,)), pltpu.SemaphoreType.DMA((n_sems,))],
    )(*inputs)


def _adamw_math(w, g, m, v):
    m = ADAM_B1 * m + (1.0 - ADAM_B1) * g
    v = ADAM_B2 * v + (1.0 - ADAM_B2) * (g * g)
    m_hat = m / (1.0 - ADAM_B1 ** ADAM_STEP)
    v_hat = v / (1.0 - ADAM_B2 ** ADAM_STEP)
    delta = -ADAM_LR * (m_hat / (jnp.sqrt(v_hat) + ADAM_EPS) + ADAM_WD * w)
    return delta, m, v


def adamw_layer(l, w, m, v, parts, own, device, chip, bufs, name):
    _, r, cdim = w.shape
    tr = _rows(r, PACK_ROW_TILE)

    def body(dev_ref, chip_ref, w_ref, m_ref, v_ref, p_ref, o_ref, *rest):
        g_ref, d_ref, nm_ref, nv_ref = rest[-4:]
        g = None
        for dev in range(8):
            part = jnp.where(dev_ref[0] == dev, o_ref[0], p_ref[dev]).astype(F32)
            g = part if g is None else g + part
        d, nm, nv = _adamw_math(w_ref[0], g, m_ref[0], v_ref[0])
        g_ref[0] = g
        d_ref[0] = d
        nm_ref[0] = nm
        nv_ref[0] = nv

    lay = pl.BlockSpec((1, tr, cdim), lambda i, dev_ref, chip_ref: (l, i, 0))
    in_specs = [lay, lay, lay, pl.BlockSpec((8, tr, cdim), lambda i, dev_ref, chip_ref: (0, i, 0)),
                pl.BlockSpec((1, tr, cdim), lambda i, dev_ref, chip_ref: (chip_ref[0], i, 0))]
    args = [device, chip, w, m, v, parts, own]
    aliases = {}
    if bufs is not None:
        in_specs += [ANY] * 4
        aliases = {len(args) + k: k for k in range(4)}
        args += list(bufs)
    return pl.pallas_call(
        body, name=name,
        grid_spec=pltpu.PrefetchScalarGridSpec(num_scalar_prefetch=2, grid=(r // tr,), in_specs=in_specs,
                                               out_specs=[lay] * 4),
        out_shape=[jax.ShapeDtypeStruct(w.shape, F32)] * 4,
        input_output_aliases=aliases,
        compiler_params=_params(("parallel",)),
    )(*args)


def sum_and_adamw(parts, own, me, w, m, v):
    _, r, cdim = parts.shape
    tr = _rows(r, PACK_ROW_TILE)

    def body(me_ref, p_ref, o_ref, w_ref, m_ref, v_ref, g_ref, d_ref, nm_ref, nv_ref):
        part = lambda k: jnp.where(me_ref[0] == k, o_ref[...], p_ref[k]).astype(F32)
        g = part(0)
        for k in range(1, 8):
            g = g + part(k)
        d, nm, nv = _adamw_math(w_ref[...], g, m_ref[...], v_ref[...])
        g_ref[...] = g
        d_ref[...] = d
        nm_ref[...] = nm
        nv_ref[...] = nv

    spec = pl.BlockSpec((tr, cdim), lambda i, me_ref: (i, 0))
    return pl.pallas_call(
        body, name="sum_and_adamw",
        grid_spec=pltpu.PrefetchScalarGridSpec(
            num_scalar_prefetch=1, grid=(r // tr,),
            in_specs=[pl.BlockSpec((8, tr, cdim), lambda i, me_ref: (0, i, 0)), spec, spec, spec, spec],
            out_specs=[spec] * 4),
        out_shape=[jax.ShapeDtypeStruct((r, cdim), F32)] * 4,
        compiler_params=_params(("parallel",)),
    )(me, parts, own, w, m, v)


SHARDED = ("w_in", "w_glu", "w_branch_a", "w_branch_b", "w_out", "w_mlp_up", "w_mlp_down")
SHARD_AXIS = {"w_in": 2, "w_glu": 1, "w_branch_a": 2, "w_branch_b": 2, "w_out": 1, "w_mlp_up": 2, "w_mlp_down": 1}
SMALL = ("norm_mix", "b_forget", "ssm_lambda_re", "ssm_lambda_im", "ssm_log_dt", "ssm_b_re", "ssm_b_im",
         "ssm_c_re", "ssm_c_im", "ssm_d", "b_glu", "norm_mlp", "norm_final")
SMALL_WIDE = ("ssm_b_re", "ssm_b_im", "ssm_c_re", "ssm_c_im")


def pack_flat(arrs):
    flat = jnp.concatenate([a.reshape(-1).astype(F32) for a in arrs])
    unit = PACK_COLS * PACK_ROW_TILE
    rows = (flat.shape[0] + unit - 1) // unit * PACK_ROW_TILE
    return jnp.pad(flat, (0, rows * PACK_COLS - flat.shape[0])).reshape(rows, PACK_COLS)


def unpack_flat(packed, shapes):
    flat = packed.reshape(-1)
    out, off = [], 0
    for s in shapes:
        n = math.prod(s)
        out.append(flat[off:off + n].reshape(tuple(s)))
        off += n
    return out


def _discretise(lam_re, lam_im, log_dt, b_re, b_im):
    dt = jnp.exp(log_dt)[:, None]
    mag = jnp.exp(lam_re * dt)
    ar = mag * jnp.cos(lam_im * dt)
    ai = mag * jnp.sin(lam_im * dt)
    den = lam_re * lam_re + lam_im * lam_im
    cr = ((ar - 1.0) * lam_re + ai * lam_im) / den
    ci = (ai * lam_re - (ar - 1.0) * lam_im) / den
    bbr = cr[:, :, None] * b_re - ci[:, :, None] * b_im
    bbi = cr[:, :, None] * b_im + ci[:, :, None] * b_re
    return ar, ai, bbr, bbi


def _ssm_inputs(ar, ai, bbr, bbi, c_re, c_im):
    pr, pi = [ar], [ai]
    for _ in range(SUBLANES - 1):
        pr, pi = pr + [pr[-1] * ar - pi[-1] * ai], pi + [pr[-1] * ai + pi[-1] * ar]
    powers = jnp.stack([jnp.stack(pr), jnp.stack(pi)], axis=1)
    powers = powers.reshape(SUBLANES, 2, SSM_CHUNKS, CHUNK_STATES).transpose(0, 2, 1, 3).reshape(SUBLANES, STATE_LANES)
    conj = powers * jnp.tile(jnp.repeat(jnp.asarray([1.0, -1.0], F32), CHUNK_STATES), SSM_CHUNKS)
    idx = jnp.arange(SUBLANES)[:, None]
    tabs = jnp.concatenate(
        [jnp.where(idx >= s, powers[s - 1][None, :], 0.0) for s in SCAN_STEPS] + [powers]
        + [jnp.where(idx < SUBLANES - s, conj[s - 1][None, :], 0.0) for s in SCAN_STEPS] + [conj[::-1]], axis=0)
    eye = jnp.eye(CHUNK_GROUPS, dtype=F32)
    wb = jnp.stack([bbr, bbi]).reshape(2, SSM_CHUNKS, CHUNK_GROUPS, SSM_STATE, SSM_GROUP_CH).transpose(1, 2, 4, 0, 3)
    wb4 = (wb[:, :, :, :, None, :] * eye[None, :, None, None, :, None]).reshape(SSM_CHUNKS, LANES, CHUNK_LANES)
    wc = jnp.stack([c_re, -c_im]).reshape(2, SSM_CHUNKS, CHUNK_GROUPS, SSM_GROUP_CH, SSM_STATE).transpose(1, 0, 4, 2, 3)
    wc4 = (wc[:, :, None, :, :, :] * eye[None, None, :, None, :, None]).reshape(SSM_CHUNKS, CHUNK_LANES, LANES)
    return tabs, wb4.astype(BF16), wc4.astype(BF16)


def _ssm_param_grads(ga8, gwb, gwc):
    eye = jnp.eye(CHUNK_GROUPS, dtype=F32)
    ga = jnp.sum(ga8, axis=0).reshape(SSM_CHUNKS, 2, CHUNK_STATES)
    gar = ga[:, 0].reshape(SSM_GROUPS, SSM_STATE)
    gai = ga[:, 1].reshape(SSM_GROUPS, SSM_STATE)

    def from_wb(g):
        t = g.reshape(SSM_CHUNKS, CHUNK_GROUPS, SSM_GROUP_CH, CHUNK_GROUPS, SSM_STATE)
        return jnp.einsum("jgchp,gh->jgpc", t, eye).reshape(SSM_GROUPS, SSM_STATE, SSM_GROUP_CH)

    def from_wc(g):
        t = g.reshape(SSM_CHUNKS, CHUNK_GROUPS, SSM_STATE, CHUNK_GROUPS, SSM_GROUP_CH)
        return jnp.einsum("jhpgc,gh->jgcp", t, eye).reshape(SSM_GROUPS, SSM_GROUP_CH, SSM_STATE)

    return (gar, gai, from_wb(gwb[:, :, :CHUNK_STATES]), from_wb(gwb[:, :, CHUNK_STATES:]),
            from_wc(gwc[:, :CHUNK_STATES]), -from_wc(gwc[:, CHUNK_STATES:]))


def kernel(x, norm_mix, w_in, b_forget, ssm_lambda_re, ssm_lambda_im, ssm_log_dt, ssm_b_re, ssm_b_im, ssm_c_re, ssm_c_im, ssm_d, w_glu, b_glu, w_branch_a, w_branch_b, w_out, norm_mlp, w_mlp_up, w_mlp_down, norm_final, loss_target, m_norm_mix, m_w_in, m_b_forget, m_ssm_lambda_re, m_ssm_lambda_im, m_ssm_log_dt, m_ssm_b_re, m_ssm_b_im, m_ssm_c_re, m_ssm_c_im, m_ssm_d, m_w_glu, m_b_glu, m_w_branch_a, m_w_branch_b, m_w_out, m_norm_mlp, m_w_mlp_up, m_w_mlp_down, m_norm_final, v_norm_mix, v_w_in, v_b_forget, v_ssm_lambda_re, v_ssm_lambda_im, v_ssm_log_dt, v_ssm_b_re, v_ssm_b_im, v_ssm_c_re, v_ssm_c_im, v_ssm_d, v_w_glu, v_b_glu, v_w_branch_a, v_w_branch_b, v_w_out, v_norm_mlp, v_w_mlp_up, v_w_mlp_down, v_norm_final):
    args = dict(locals())
    bsz, seq, d = x.shape
    nl = norm_mix.shape[0]
    tokens = bsz * seq
    aw, sw = ATTN_WIDTH, SSM_WIDTH
    chip = (2 * lax.axis_index("x") + lax.axis_index("y")).astype(jnp.int32)
    chip_id = chip.reshape(1)
    device_id = (2 * chip + lax.axis_index("c").astype(jnp.int32)).reshape(1)

    own = {n: args[n].astype(BF16) for n in SHARDED}
    late = [n for n in SHARDED if n != "w_in"]
    o_f, o_u, o_ga, o_gb = 3 * aw, 3 * aw + ATTN_HEADS, 3 * aw + ATTN_HEADS + sw, 3 * aw + ATTN_HEADS + sw + d
    u_blk = 2 * d // sw
    f_blk = (2 * d + sw) // F_PAD
    bf_pad = jnp.pad(b_forget, ((0, 0), (0, F_PAD - ATTN_HEADS)))

    def assemble(l, names, gathered):
        return {n: jnp.concatenate([jnp.where(chip == k, own[n][l], g[k]) for k in range(4)],
                                   axis=SHARD_AXIS[n] - 1) for n, g in zip(names, gathered)}

    def split_w_in(win):
        w = {"w_qkv": win[:, :o_f],
             "w_rest": jnp.concatenate([win[:, o_ga:o_gb], win[:, o_gb:], win[:, o_u:o_ga],
                                        jnp.pad(win[:, o_f:o_u], ((0, 0), (0, F_PAD - ATTN_HEADS)))], axis=1)}
        w["w_pad"] = jnp.concatenate([w["w_qkv"], w["w_rest"]], axis=1)
        return w

    disc = [jax.vjp(_discretise, ssm_lambda_re[l], ssm_lambda_im[l], ssm_log_dt[l], ssm_b_re[l], ssm_b_im[l])
            for l in range(nl)]

    g_in = exchange([("gather", [own["w_in"][0]])], "gather_first")
    xs = x.reshape(tokens, d)
    saved, weights = [], []
    for l in range(nl):
        w = split_w_in(assemble(l, ["w_in"], g_in)["w_in"])
        g1 = norm_mix[l].reshape(1, d)
        qkv = norm_matmul(xs, g1, w["w_qkv"], BF16, "proj_qkv")
        rest = norm_matmul(xs, g1, w["w_rest"], F32, "proj_rest")
        cumcol, cumrow = forget_cumsum(rest, bf_pad[l:l + 1], bsz, seq, f_blk)
        carries = [("gather", [own[n][l] for n in late] + ([own["w_in"][l + 1]] if l + 1 < nl else []))]
        ya, lser, lands = fox_fwd(qkv, cumrow, bsz, seq, carries)
        w.update(assemble(l, late, lands))
        g_in = lands[len(late):]
        tabs, wb4, wc4 = _ssm_inputs(*disc[l][0], ssm_c_re[l], ssm_c_im[l])
        dskip = ssm_d[l].reshape(1, sw)
        ys, hs = ssm_fwd(rest, wb4, wc4, tabs, dskip, bsz, seq, u_blk)
        x1, z, pa, pb, yb, yb2, mixed = mix_fwd(ya, ys, rest, xs, w["w_glu"], b_glu[l].reshape(1, sw),
                                                 w["w_branch_a"], w["w_branch_b"], w["w_out"])
        x2, up = mlp_fwd(x1, norm_mlp[l].reshape(1, d), w["w_mlp_up"], w["w_mlp_down"])
        saved.append(dict(x0=xs, qkv=qkv, rest=rest, cumcol=cumcol, ya=ya, lser=lser,
                          tabs=tabs, wb4=wb4, wc4=wc4, dskip=dskip, ys=ys, hs=hs, x1=x1, z=z, pa=pa, pb=pb,
                          yb=yb, yb2=yb2, mixed=mixed, up=up))
        weights.append(w)
        xs = x2
    dx, loss_rows, dgf_rows = final_loss(xs, norm_final.reshape(1, d), loss_target.reshape(tokens, d))
    loss = lax.psum(jnp.sum(loss_rows), ("x", "y", "c"))

    early = [n for n in SHARDED if n != "w_in"]
    big = {n: [None] * nl for n in SHARDED}
    parts = {n: [None] * nl for n in SHARDED}
    small = {n: [None] * nl for n in SMALL if n != "norm_final"}
    for l in reversed(range(nl)):
        s, w = saved[l], weights[l]
        g2 = norm_mlp[l].reshape(1, d)
        dup, dx1, dg2 = mlp_bwd(dx, s["up"], s["x1"], g2, w["w_mlp_up"], w["w_mlp_down"])
        big["w_mlp_down"][l] = matmul_tn(s["up"], dx, "grad_w_mlp_down", a_kind="relu2", shard_axis=0, out_dtype=BF16)
        big["w_mlp_up"][l] = matmul_tn(s["x1"], dup, "grad_w_mlp_up", a_kind="norm", gain=g2, shard_axis=1,
                                       out_dtype=BF16)
        small["norm_mlp"][l] = jnp.sum(dg2, axis=0)
        dya, dys, dgab, dpa, dpb, dz, dbg = mix_bwd(dx1, s["rest"], s["pa"], s["pb"], s["z"], s["ys"],
                                                    w["w_glu"], w["w_branch_a"], w["w_branch_b"], w["w_out"])
        big["w_out"][l] = matmul_tn(s["mixed"], dx1, "grad_w_out", shard_axis=0, out_dtype=BF16)
        big["w_branch_a"][l] = matmul_tn(s["ya"], dpa, "grad_w_branch_a", shard_axis=1, out_dtype=BF16)
        big["w_branch_b"][l] = matmul_tn(s["yb2"], dpb, "grad_w_branch_b", shard_axis=1, out_dtype=BF16)
        big["w_glu"][l] = matmul_tn(s["yb"], dz, "grad_w_glu", shard_axis=0, out_dtype=BF16)
        small["b_glu"][l] = jnp.sum(dbg, axis=0)
        du, ga8, gwb, gwc, gd8 = ssm_bwd(dys, s["rest"], s["hs"], s["wb4"], s["wc4"], s["tabs"], s["dskip"],
                                         bsz, seq, u_blk)
        gar, gai, gbbr, gbbi, gcr, gci = _ssm_param_grads(ga8, gwb, gwc)
        glr, gli, gdt, gbr, gbi = disc[l][1]((gar, gai, gbbr, gbbi))
        small["ssm_lambda_re"][l], small["ssm_lambda_im"][l], small["ssm_log_dt"][l] = glr, gli, gdt
        small["ssm_b_re"][l], small["ssm_b_im"][l] = gbr, gbi
        small["ssm_c_re"][l], small["ssm_c_im"][l] = gcr, gci
        small["ssm_d"][l] = jnp.sum(gd8, axis=0)
        carries = [("reduce", [big[n][l] for n in early])]
        if l + 1 < nl:
            carries.append(("reduce", [big["w_in"][l + 1]]))
        dq, dk, dv, df, dbf, lands = fox_bwd(s["qkv"], dya, s["ya"], s["lser"], s["cumcol"],
                                             s["rest"], bf_pad[l:l + 1], bsz, seq, f_blk, carries)
        for n, p in zip(early, lands):
            parts[n][l] = p
        if l + 1 < nl:
            parts["w_in"][l + 1] = lands[len(early)]
        small["b_forget"][l] = jnp.sum(dbf, axis=0)[:ATTN_HEADS]
        dproj = jnp.concatenate([dq, dk, dv, dgab, du, df], axis=1)
        g1 = norm_mix[l].reshape(1, d)
        dwp = matmul_tn(s["x0"], dproj, "grad_w_in", a_kind="norm", gain=g1)
        big["w_in"][l] = jnp.stack(jnp.split(jnp.concatenate(
            [dwp[:, :o_f], dwp[:, o_f + 2 * d + sw:o_f + 2 * d + sw + ATTN_HEADS],
             dwp[:, o_f + 2 * d:o_f + 2 * d + sw], dwp[:, o_f:o_f + 2 * d]], axis=1), 4, axis=1)).astype(BF16)
        dx, dg1 = proj_bwd(dproj, w["w_pad"], s["x0"], dx1, g1)
        small["norm_mix"][l] = jnp.sum(dg1, axis=0)
    grad_x = dx.reshape(bsz, seq, d)

    small_g = {n: jnp.stack(small[n]) if n != "norm_final" else jnp.sum(dgf_rows, axis=0) for n in SMALL}
    groups = {BF16: SMALL_WIDE, F32: [n for n in SMALL if n not in SMALL_WIDE]}
    packed = {dt: pack_flat([small_g[n] for n in names]).astype(dt) for dt, names in groups.items()}
    parts["w_in"][0], *small_parts = exchange(
        [("reduce", [big["w_in"][0]]), ("all", [packed[dt] for dt in groups])], "exchange_last")

    out_g, out_d, out_m, out_v = {}, {}, {}, {}
    for n in SHARDED:
        bufs = None
        for l in range(nl):
            bufs = adamw_layer(l, args[n], args["m_" + n], args["v_" + n], parts[n][l], big[n][l], device_id, chip_id,
                               bufs, "adamw_" + n)
        out_g[n], out_d[n], out_m[n], out_v[n] = bufs

    for (dt, names), landed in zip(groups.items(), small_parts):
        results = sum_and_adamw(landed, packed[dt], device_id, pack_flat([args[n] for n in names]),
                                pack_flat([args["m_" + n] for n in names]), pack_flat([args["v_" + n] for n in names]))
        for res, flat in zip((out_g, out_d, out_m, out_v), results):
            res.update(zip(names, unpack_flat(flat, [args[n].shape for n in names])))

    order = ("norm_mix", "w_in", "b_forget", "ssm_lambda_re", "ssm_lambda_im", "ssm_log_dt", "ssm_b_re",
             "ssm_b_im", "ssm_c_re", "ssm_c_im", "ssm_d", "w_glu", "b_glu", "w_branch_a", "w_branch_b", "w_out",
             "norm_mlp", "w_mlp_up", "w_mlp_down", "norm_final")
    return (loss, grad_x, *[out_g[n] for n in order], *[out_d[n] for n in order],
            *[out_m[n] for n in order], *[out_v[n] for n in order])
```

```python
import math

import jax
import jax.numpy as jnp
from jax import lax
from jax.experimental import pallas as pl
from jax.experimental.pallas import tpu as pltpu

F32 = jnp.float32
BF16 = jnp.bfloat16
MESH = pl.DeviceIdType.MESH
ANY = pl.BlockSpec(memory_space=pl.ANY)

ATTN_HEADS = 8
HEAD_DIM = 64
ATTN_WIDTH = ATTN_HEADS * HEAD_DIM
HEAD_PAIRS = ATTN_HEADS // 2
SSM_GROUPS = 32
SSM_GROUP_CH = 16
SSM_STATE = 64
SSM_WIDTH = SSM_GROUPS * SSM_GROUP_CH
LANES = 128
SUBLANES = 8
SSM_CHUNKS = SSM_WIDTH // LANES
CHUNK_GROUPS = SSM_GROUPS // SSM_CHUNKS
CHUNK_STATES = CHUNK_GROUPS * SSM_STATE
CHUNK_LANES = 2 * CHUNK_STATES
STATE_LANES = SSM_CHUNKS * CHUNK_LANES
F_PAD = LANES
RMS_EPS = 1e-6
ADAM_LR = 0.001
ADAM_B1 = 0.9
ADAM_B2 = 0.999
ADAM_EPS = 1e-08
ADAM_WD = 0.01
ADAM_STEP = 10
PACK_COLS = 1024
PACK_ROW_TILE = 256
VMEM_LIMIT = 52 * 1024 * 1024
NEG_BIG = -1e30
LOG2E = math.log2(math.e)
GELU_C = math.sqrt(2.0 / math.pi)
GELU_A = 0.044715

NN = (((1,), (0,)), ((), ()))
NT = (((1,), (1,)), ((), ()))
TN = (((0,), (0,)), ((), ()))


def _pick(n, pref):
    if n <= pref:
        return n
    best = LANES
    for t in range(LANES, pref + 1, LANES):
        if n % t == 0:
            best = t
    assert n % best == 0, (n, pref)
    return best


def _rows(n, pref):
    t = min(n, pref)
    while n % t:
        t //= 2
    assert t % 16 == 0 or t == n, (n, pref)
    return t


def _params(sem):
    return pltpu.CompilerParams(dimension_semantics=sem, vmem_limit_bytes=VMEM_LIMIT)


def _fold8(v):
    r, c = v.shape
    return jnp.sum(v.reshape(r // SUBLANES, SUBLANES, c), axis=0)


def _dot(a, b, dims=None):
    if dims is None:
        return jnp.dot(a, b, preferred_element_type=F32)
    return lax.dot_general(a, b, dims, preferred_element_type=F32)


def _dot_exact(a, b, dims):
    return lax.dot_general(a, b, dims, preferred_element_type=F32, precision=lax.Precision.HIGHEST)


def _sigmoid(v):
    return 1.0 / (1.0 + jnp.exp(-v))


def _rms_scale(x):
    return lax.rsqrt(jnp.mean(x * x, axis=-1, keepdims=True) + RMS_EPS)


def _rms_bwd(x, g, dh):
    r = _rms_scale(x)
    xn = x * r
    dxn = dh * g
    dx = r * (dxn - xn * jnp.mean(dxn * xn, axis=-1, keepdims=True))
    return dx, dh * xn


def norm_matmul(x, g, w, out_dtype, name, col_scale=None):
    t, d = x.shape
    m = w.shape[1]
    tm, tn = _rows(t, 1024), _pick(m, 1024)

    def body(x_ref, g_ref, w_ref, *rest):
        xv = x_ref[...]
        h = (xv * _rms_scale(xv) * g_ref[...]).astype(BF16)
        out = _dot(h, w_ref[...])
        if col_scale is not None:
            out = out * rest[0][...]
        rest[-1][...] = out.astype(rest[-1].dtype)

    scale_spec = [] if col_scale is None else [pl.BlockSpec((1, tn), lambda i, j: (0, j))]
    return pl.pallas_call(
        body, name=name, grid=(t // tm, m // tn),
        in_specs=[pl.BlockSpec((tm, d), lambda i, j: (i, 0)),
                  pl.BlockSpec((1, d), lambda i, j: (0, 0)),
                  pl.BlockSpec((d, tn), lambda i, j: (0, j))] + scale_spec,
        out_specs=pl.BlockSpec((tm, tn), lambda i, j: (i, j)),
        out_shape=jax.ShapeDtypeStruct((t, m), out_dtype),
        compiler_params=_params(("parallel", "arbitrary")),
    )(x, g, w, *([] if col_scale is None else [col_scale]))


def matmul_tn(a, b, name, a_kind="plain", gain=None, shard_axis=None, out_dtype=F32, tm_pref=1024, tn_pref=1536,
              tk_pref=1024):
    t, ma = a.shape
    nb = b.shape[1]
    tm = ma if a_kind == "norm" else _pick(ma, tm_pref)
    tn = _pick(nb, tn_pref)
    tk = _rows(t, tk_pref)
    nk = t // tk
    if shard_axis == 0:
        per = tm * 4 // ma
        assert per >= 1 and (ma // 4) * per == tm, (ma, tm)
        out_shape, out_spec = (4, ma // 4, nb), pl.BlockSpec((per, ma // 4, tn), lambda i, j, k: (i, 0, j))
    elif shard_axis == 1:
        per = tn * 4 // nb
        assert per >= 1 and (nb // 4) * per == tn, (nb, tn)
        out_shape, out_spec = (4, ma, nb // 4), pl.BlockSpec((per, tm, nb // 4), lambda i, j, k: (j, i, 0))
    else:
        out_shape, out_spec = (ma, nb), pl.BlockSpec((tm, tn), lambda i, j, k: (i, j))

    def body(*refs):
        if a_kind == "norm":
            a_ref, g_ref, b_ref, o_ref, acc = refs
        else:
            a_ref, b_ref, o_ref, acc = refs
        k = pl.program_id(2)

        @pl.when(k == 0)
        def _():
            acc[...] = jnp.zeros_like(acc)

        av = a_ref[...]
        if a_kind == "norm":
            av = av * _rms_scale(av) * g_ref[...]
        elif a_kind == "relu2":
            av = jnp.square(jnp.maximum(av.astype(F32), 0.0))
        acc[...] += _dot(av.astype(BF16), b_ref[...].astype(BF16), TN)

        @pl.when(k == nk - 1)
        def _():
            if shard_axis == 0:
                o_ref[...] = acc[...].reshape(o_ref.shape).astype(o_ref.dtype)
            elif shard_axis == 1:
                cs = nb // 4
                for n in range(o_ref.shape[0]):
                    o_ref[n] = acc[:, n * cs:(n + 1) * cs].astype(o_ref.dtype)
            else:
                o_ref[...] = acc[...].astype(o_ref.dtype)

    in_specs = [pl.BlockSpec((tk, tm), lambda i, j, k: (k, i))]
    args = [a]
    if a_kind == "norm":
        in_specs.append(pl.BlockSpec((1, ma), lambda i, j, k: (0, 0)))
        args.append(gain)
    in_specs.append(pl.BlockSpec((tk, tn), lambda i, j, k: (k, j)))
    args.append(b)
    return pl.pallas_call(
        body, name=name, grid=(ma // tm, nb // tn, nk),
        in_specs=in_specs,
        out_specs=out_spec,
        out_shape=jax.ShapeDtypeStruct(out_shape, out_dtype),
        scratch_shapes=[pltpu.VMEM((tm, tn), F32)],
        compiler_params=_params(("parallel", "parallel", "arbitrary")),
    )(*args)


def _tri(n, upper):
    r = lax.broadcasted_iota(jnp.int32, (n, n), 0)
    c = lax.broadcasted_iota(jnp.int32, (n, n), 1)
    return jnp.where((c >= r) if upper else (c <= r), 1.0, 0.0).astype(F32)


def _head_rows():
    r = lax.broadcasted_iota(jnp.int32, (SUBLANES, LANES), 0)
    c = lax.broadcasted_iota(jnp.int32, (SUBLANES, LANES), 1)
    return jnp.where(r == c, 1.0, 0.0).astype(F32)


def forget_cumsum(rest, bf, bsz, seq, f_blk):
    tc = _rows(seq, 512)
    nc = seq // tc

    def body(f_ref, b_ref, col_ref, row_ref, carry):
        c = pl.program_id(1)

        @pl.when(c == 0)
        def _():
            carry[...] = jnp.zeros_like(carry)

        z = f_ref[...] + b_ref[...]
        logf = jnp.minimum(z, 0.0) - jnp.log(1.0 + jnp.exp(-jnp.abs(z)))
        cum = _dot_exact(_tri(tc, False), logf, NN) + carry[0:1, :]
        col_ref[0] = cum
        row_ref[0] = _dot_exact(_head_rows(), cum, NT)
        carry[...] = jnp.broadcast_to(cum[tc - 1:tc, :], carry.shape)

    return pl.pallas_call(
        body, name="forget_cumsum", grid=(bsz, nc),
        in_specs=[pl.BlockSpec((tc, F_PAD), lambda b, c: (b * nc + c, f_blk)),
                  pl.BlockSpec((1, F_PAD), lambda b, c: (0, 0))],
        out_specs=[pl.BlockSpec((1, tc, LANES), lambda b, c: (b, c, 0)),
                   pl.BlockSpec((1, SUBLANES, tc), lambda b, c: (b, 0, c))],
        out_shape=[jax.ShapeDtypeStruct((bsz, seq, LANES), F32),
                   jax.ShapeDtypeStruct((bsz, SUBLANES, seq), F32)],
        scratch_shapes=[pltpu.VMEM((SUBLANES, LANES), F32)],
        compiler_params=_params(("parallel", "arbitrary")),
    )(rest, bf)


def forget_bwd(dcq, dcp, rest, bf, bsz, seq, f_blk):
    tc = _attn_tile(seq)
    nc = seq // tc

    def body(dq_ref, dc_ref, f_ref, b_ref, df_ref, db_ref, carry):
        b = pl.program_id(0)
        c = pl.program_id(1)

        @pl.when(c == 0)
        def _():
            carry[...] = jnp.zeros_like(carry)

        @pl.when((b == 0) & (c == 0))
        def _():
            db_ref[...] = jnp.zeros_like(db_ref)

        row = lax.broadcasted_iota(jnp.int32, (SUBLANES, tc), 0)
        heads = jnp.zeros((SUBLANES, tc), F32)
        dc = jnp.zeros((tc, LANES), F32)
        for p in range(HEAD_PAIRS):
            blk = dq_ref[0, p, 0]
            heads = heads + jnp.where(row == 2 * p, blk[0:1], 0.0) + jnp.where(row == 2 * p + 1, blk[1:2], 0.0)
            dc = dc + dc_ref[0, p]
        dc = dc + jnp.concatenate([heads, jnp.zeros((LANES - SUBLANES, tc), F32)], axis=0).T
        dlogf = _dot_exact(_tri(tc, True), dc, NN) + carry[0:1, :]
        carry[...] = jnp.broadcast_to(dlogf[0:1, :], carry.shape)
        z = f_ref[...] + b_ref[...]
        lane = lax.broadcasted_iota(jnp.int32, z.shape, 1)
        df = jnp.where(lane < ATTN_HEADS, dlogf * _sigmoid(-z), 0.0)
        df_ref[...] = df.astype(df_ref.dtype)
        db_ref[...] += _fold8(df)

    return pl.pallas_call(
        body, name="forget_bwd", grid=(bsz, nc),
        in_specs=[pl.BlockSpec((1, HEAD_PAIRS, 1, SUBLANES, tc), lambda b, c: (b, 0, nc - 1 - c, 0, 0)),
                  pl.BlockSpec((1, HEAD_PAIRS, tc, LANES), lambda b, c: (b, 0, nc - 1 - c, 0)),
                  pl.BlockSpec((tc, F_PAD), lambda b, c: (b * nc + nc - 1 - c, f_blk)),
                  pl.BlockSpec((1, F_PAD), lambda b, c: (0, 0))],
        out_specs=[pl.BlockSpec((tc, F_PAD), lambda b, c: (b * nc + nc - 1 - c, 0)),
                   pl.BlockSpec((SUBLANES, F_PAD), lambda b, c: (0, 0))],
        out_shape=[jax.ShapeDtypeStruct((bsz * seq, F_PAD), BF16),
                   jax.ShapeDtypeStruct((SUBLANES, F_PAD), F32)],
        scratch_shapes=[pltpu.VMEM((SUBLANES, LANES), F32)],
        compiler_params=_params(("arbitrary", "arbitrary")),
    )(dcq, dcp, rest, bf)


def _attn_tile(seq):
    return 512 if seq >= 2048 else 128


def _lane_head(shape, par):
    lane = lax.broadcasted_iota(jnp.int32, shape, len(shape) - 1)
    return (lane >= HEAD_DIM) if par else (lane < HEAD_DIM)


def _pick_lane(block, idx):
    lane = lax.broadcasted_iota(jnp.int32, block.shape, 1)
    return jnp.sum(jnp.where(lane == idx, block, 0.0), axis=1, keepdims=True)


def _pair_rows(lo_lane, hi_lane):
    r = lax.broadcasted_iota(jnp.int32, (SUBLANES, LANES), 0)
    c = lax.broadcasted_iota(jnp.int32, (SUBLANES, LANES), 1)
    if lo_lane is None:
        sel = ((r == 0) & (c < HEAD_DIM)) | ((r == 1) & (c >= HEAD_DIM))
    else:
        sel = ((r == 0) & (c == lo_lane)) | ((r == 1) & (c == hi_lane))
    return jnp.where(sel, 1.0, 0.0).astype(F32)


def _causal(s, transposed):
    r = lax.broadcasted_iota(jnp.int32, s.shape, 0)
    c = lax.broadcasted_iota(jnp.int32, s.shape, 1)
    return jnp.where((c >= r) if transposed else (r >= c), s, NEG_BIG)


def _causal_pairs(n, key_major):
    if key_major:
        pairs = [(i, j) for j in range(n) for i in range(j, n)]
    else:
        pairs = [(i, j) for i in range(n) for j in range(i + 1)]
    return (jnp.asarray([p[0] for p in pairs], jnp.int32), jnp.asarray([p[1] for p in pairs], jnp.int32))


def fox_fwd(qkv, cumrow, bsz, seq, carries=()):
    tq = _attn_tile(seq)
    nq = seq // tq
    kb, vb = ATTN_WIDTH // LANES, 2 * ATTN_WIDTH // LANES
    qi_tab, kj_tab = _causal_pairs(nq, False)
    npairs = int(qi_tab.shape[0])
    sends, landings, n_sems, build = _carry_plan(carries)
    nc = len(sends)

    def body(qi_ref, kj_ref, q_ref, k_ref, v_ref, cr_ref, *rest):
        ins, (o_ref, lr_ref), lands = rest[:nc], rest[nc:nc + 2], rest[nc + 2:2 * nc + 2]
        (m_s, acc_s), sems = rest[2 * nc + 2:2 * nc + 4], rest[2 * nc + 4:]
        hp = pl.program_id(1)
        qi = qi_ref[pl.program_id(2)]
        kj = kj_ref[pl.program_id(2)]
        if nc:
            @pl.when((pl.program_id(0) == 0) & (hp == 0) & (pl.program_id(2) == 0))
            def _():
                for cp in build(ins, lands, sems):
                    cp.start()

        @pl.when(kj == 0)
        def _():
            m_s[...] = jnp.full_like(m_s, NEG_BIG)
            acc_s[...] = jnp.zeros_like(acc_s)

        def step(diag):
            q = q_ref[...]
            k = k_ref[...]
            v = v_ref[...]
            for par in range(2):
                sel = _lane_head(q.shape, par)
                s = _dot(jnp.where(sel, q, 0.0), k, NT) - cr_ref[0, pl.ds(2 * hp + par, 1), :] * LOG2E
                if diag:
                    s = _causal(s, False)
                m_prev = m_s[par]
                m_new = jnp.maximum(m_prev, jnp.max(s, axis=1, keepdims=True))
                p = jnp.exp2(s - m_new).astype(BF16)
                acc_s[par] = jnp.exp2(m_prev - m_new) * acc_s[par] + _dot(p, jnp.where(sel, v, 1.0).astype(BF16))
                m_s[par] = m_new

        @pl.when(kj < qi)
        def _():
            step(False)

        @pl.when(kj == qi)
        def _():
            step(True)
            lo = _lane_head((tq, LANES), 0)
            sums = [pltpu.roll(acc_s[par], HEAD_DIM, 1) for par in range(2)]
            out = jnp.where(lo, acc_s[0] / sums[0], acc_s[1] / sums[1])
            o_ref[...] = out.astype(o_ref.dtype)
            lse = jnp.where(lo, m_s[0] + jnp.log(sums[0]) * LOG2E, m_s[1] + jnp.log(sums[1]) * LOG2E)
            lr_ref[0, 0] = _dot_exact(_pair_rows(0, HEAD_DIM), lse, NT)

        if nc:
            @pl.when((pl.program_id(0) == bsz - 1) & (hp == HEAD_PAIRS - 1) & (pl.program_id(2) == npairs - 1))
            def _():
                for cp in build(ins, lands, sems):
                    cp.wait()

    sem_shapes = [pltpu.SemaphoreType.DMA((n_sems,)), pltpu.SemaphoreType.DMA((n_sems,))] if nc else []
    outs = pl.pallas_call(
        body, name="fox_fwd",
        grid_spec=pltpu.PrefetchScalarGridSpec(
            num_scalar_prefetch=2, grid=(bsz, HEAD_PAIRS, npairs),
            in_specs=[pl.BlockSpec((tq, LANES), lambda b, h, t, qi, kj: (b * nq + qi[t], h)),
                      pl.BlockSpec((tq, LANES), lambda b, h, t, qi, kj: (b * nq + kj[t], kb + h)),
                      pl.BlockSpec((tq, LANES), lambda b, h, t, qi, kj: (b * nq + kj[t], vb + h)),
                      pl.BlockSpec((1, SUBLANES, tq), lambda b, h, t, qi, kj: (b, 0, kj[t]))] + [ANY] * nc,
            out_specs=[pl.BlockSpec((tq, LANES), lambda b, h, t, qi, kj: (b * nq + qi[t], h)),
                       pl.BlockSpec((1, 1, SUBLANES, tq), lambda b, h, t, qi, kj: (b, h, 0, qi[t]))] + [ANY] * nc,
            scratch_shapes=[pltpu.VMEM((2, tq, 1), F32), pltpu.VMEM((2, tq, LANES), F32)] + sem_shapes),
        out_shape=[jax.ShapeDtypeStruct((bsz * seq, ATTN_WIDTH), BF16),
                   jax.ShapeDtypeStruct((bsz, HEAD_PAIRS, SUBLANES, seq), F32)] + landings,
        compiler_params=_params(("arbitrary", "arbitrary", "arbitrary")),
    )(qi_tab, kj_tab, qkv, qkv, qkv, cumrow, *sends)
    return outs[0], outs[1], list(outs[2:])


def fox_bwd_kernel(qkv, do, o, lser, cumcol, bsz, seq, carries=()):
    tk = _attn_tile(seq)
    nk = seq // tk
    scale = HEAD_DIM ** -0.5
    kb, vb = ATTN_WIDTH // LANES, 2 * ATTN_WIDTH // LANES
    qi_tab, kj_tab = _causal_pairs(nk, True)
    npairs = int(qi_tab.shape[0])
    sends, landings, n_sems, build = _carry_plan(carries)
    nc = len(sends)

    def body(qi_ref, kj_ref, q_ref, k_ref, v_ref, do_ref, o_ref, lr_ref, cc_ref, *rest):
        ins, lands = rest[:nc], rest[nc + 5:2 * nc + 5]
        dq_ref, dk_ref, dv_ref, dc_ref, dr_ref = rest[nc:nc + 5]
        (dq_s, dk_s, dv_s, dc_s, dr_s), sems = rest[2 * nc + 5:2 * nc + 10], rest[2 * nc + 10:]
        hp = pl.program_id(1)
        t = pl.program_id(2)
        qi = qi_ref[t]
        kj = kj_ref[t]
        if nc:
            @pl.when((pl.program_id(0) == 0) & (hp == 0) & (t == 0))
            def _():
                for cp in build(ins, lands, sems):
                    cp.start()

        @pl.when(t == 0)
        def _():
            dq_s[...] = jnp.zeros_like(dq_s)
            dr_s[...] = jnp.zeros_like(dr_s)

        @pl.when(qi == kj)
        def _():
            dk_s[...] = jnp.zeros_like(dk_s)
            dv_s[...] = jnp.zeros_like(dv_s)
            dc_s[...] = jnp.zeros_like(dc_s)

        def step(diag):
            q = q_ref[...]
            k = k_ref[...]
            v = v_ref[...]
            dov = do_ref[...]
            prod = dov.astype(F32) * o_ref[...].astype(F32)
            drow = _dot_exact(_pair_rows(None, None), prod, NT)
            lrow = lr_ref[0, 0]
            lane = lax.broadcasted_iota(jnp.int32, (tk, LANES), 1)
            for par in range(2):
                head = 2 * hp + par
                sel = _lane_head(k.shape, par)
                kh = jnp.where(sel, k, 0.0)
                st = _dot(kh, q, NT) - _pick_lane(cc_ref[0], head) * LOG2E
                if diag:
                    st = _causal(st, True)
                pt = jnp.exp2(st - lrow[par:par + 1, :])
                vh = jnp.where(sel, v, 0.0)
                dpt = _dot(vh.astype(BF16), dov, NT)
                dst = pt * (dpt - drow[par:par + 1, :])
                dsb = dst.astype(BF16)
                dv_s[...] += jnp.where(sel, _dot(pt.astype(BF16), dov), 0.0)
                dk_s[...] += jnp.where(sel, _dot(dsb, q), 0.0)
                dq_s[qi] += _dot(dsb, kh, TN)
                dc_s[...] += jnp.where(lane == head, -jnp.sum(dst, axis=1, keepdims=True), 0.0)
                dr_s[qi, par:par + 1, :] += jnp.sum(dst, axis=0, keepdims=True)

        @pl.when(qi > kj)
        def _():
            step(False)

        @pl.when(qi == kj)
        def _():
            step(True)

        @pl.when(qi == nk - 1)
        def _():
            dk_ref[...] = (dk_s[...] * (1.0 / LOG2E)).astype(dk_ref.dtype)
            dv_ref[...] = dv_s[...].astype(dv_ref.dtype)
            dc_ref[0, 0] = dc_s[...]

        @pl.when(t == npairs - 1)
        def _():
            for i in range(nk):
                dq_ref[i * tk:(i + 1) * tk, :] = (dq_s[i] * scale).astype(dq_ref.dtype)
            dr_ref[0, 0] = dr_s[...]

        if nc:
            @pl.when((pl.program_id(0) == bsz - 1) & (hp == HEAD_PAIRS - 1) & (t == npairs - 1))
            def _():
                for cp in build(ins, lands, sems):
                    cp.wait()

    sem_shapes = [pltpu.SemaphoreType.DMA((n_sems,)), pltpu.SemaphoreType.DMA((n_sems,))] if nc else []
    outs = pl.pallas_call(
        body, name="fox_bwd",
        grid_spec=pltpu.PrefetchScalarGridSpec(
            num_scalar_prefetch=2, grid=(bsz, HEAD_PAIRS, npairs),
            in_specs=[pl.BlockSpec((tk, LANES), lambda b, h, t, qi, kj: (b * nk + qi[t], h)),
                      pl.BlockSpec((tk, LANES), lambda b, h, t, qi, kj: (b * nk + kj[t], kb + h)),
                      pl.BlockSpec((tk, LANES), lambda b, h, t, qi, kj: (b * nk + kj[t], vb + h)),
                      pl.BlockSpec((tk, LANES), lambda b, h, t, qi, kj: (b * nk + qi[t], h)),
                      pl.BlockSpec((tk, LANES), lambda b, h, t, qi, kj: (b * nk + qi[t], h)),
                      pl.BlockSpec((1, 1, SUBLANES, tk), lambda b, h, t, qi, kj: (b, h, 0, qi[t])),
                      pl.BlockSpec((1, tk, LANES), lambda b, h, t, qi, kj: (b, kj[t], 0))] + [ANY] * nc,
            out_specs=[pl.BlockSpec((seq, LANES), lambda b, h, t, qi, kj: (b, h)),
                       pl.BlockSpec((tk, LANES), lambda b, h, t, qi, kj: (b * nk + kj[t], h)),
                       pl.BlockSpec((tk, LANES), lambda b, h, t, qi, kj: (b * nk + kj[t], h)),
                       pl.BlockSpec((1, 1, tk, LANES), lambda b, h, t, qi, kj: (b, h, kj[t], 0)),
                       pl.BlockSpec((1, 1, nk, SUBLANES, tk), lambda b, h, t, qi, kj: (b, h, 0, 0, 0))] + [ANY] * nc,
            scratch_shapes=[pltpu.VMEM((nk, tk, LANES), F32), pltpu.VMEM((tk, LANES), F32),
                            pltpu.VMEM((tk, LANES), F32), pltpu.VMEM((tk, LANES), F32),
                            pltpu.VMEM((nk, SUBLANES, tk), F32)] + sem_shapes),
        out_shape=[jax.ShapeDtypeStruct((bsz * seq, ATTN_WIDTH), BF16),
                   jax.ShapeDtypeStruct((bsz * seq, ATTN_WIDTH), BF16),
                   jax.ShapeDtypeStruct((bsz * seq, ATTN_WIDTH), BF16),
                   jax.ShapeDtypeStruct((bsz, HEAD_PAIRS, seq, LANES), F32),
                   jax.ShapeDtypeStruct((bsz, HEAD_PAIRS, nk, SUBLANES, tk), F32)] + landings,
        compiler_params=_params(("arbitrary", "arbitrary", "arbitrary")),
    )(qi_tab, kj_tab, qkv, qkv, qkv, do, o, lser, cumcol, *sends)
    return outs[:5], list(outs[5:])


def fox_bwd(qkv, do, o, lser, cumcol, rest, bf, bsz, seq, f_blk, carries=()):
    (dq, dk, dv, dcp, dcq), lands = fox_bwd_kernel(qkv, do, o, lser, cumcol, bsz, seq, carries)
    df, dbf = forget_bwd(dcq, dcp, rest, bf, bsz, seq, f_blk)
    return dq, dk, dv, df, dbf, lands


SCAN_STEPS = (1, 2, 4)
TAB_FWD = 0
TAB_BWD = 32
TAB_CARRY = 24
TAB_ROWS = 64


def _ssm_tile(seq):
    return 256 if seq >= 1024 else 64


def _scan_block(xr, xi, tab_ref, re, im, cr, ci, reverse):
    base = TAB_BWD if reverse else TAB_FWD
    for n, s in enumerate(SCAN_STEPS):
        ar = tab_ref[base + n * SUBLANES:base + (n + 1) * SUBLANES, re]
        ai = tab_ref[base + n * SUBLANES:base + (n + 1) * SUBLANES, im]
        shift = SUBLANES - s if reverse else s
        sr = pltpu.roll(xr, shift, 0)
        si = pltpu.roll(xi, shift, 0)
        xr, xi = xr + ar * sr - ai * si, xi + ar * si + ai * sr
    pr = tab_ref[base + TAB_CARRY:base + TAB_CARRY + SUBLANES, re]
    pi = tab_ref[base + TAB_CARRY:base + TAB_CARRY + SUBLANES, im]
    xr, xi = xr + pr * cr - pi * ci, xi + pr * ci + pi * cr
    return xr, xi


def ssm_fwd(rest, wb4, wc4, tabs, dskip, bsz, seq, u_blk):
    tt = _ssm_tile(seq)
    nt = seq // tt

    def body(u_ref, wb_ref, wc_ref, tab_ref, d_ref, y_ref, h_ref, carry):
        c = pl.program_id(1)

        @pl.when(c == 0)
        def _():
            carry[...] = jnp.zeros_like(carry)

        u = u_ref[...]
        ub = u.astype(BF16)
        for j in range(SSM_CHUNKS):
            h_ref[:, j * CHUNK_LANES:(j + 1) * CHUNK_LANES] = _dot(ub[:, j * LANES:(j + 1) * LANES], wb_ref[j])
        for j in range(SSM_CHUNKS):
            re = slice(j * CHUNK_LANES, j * CHUNK_LANES + CHUNK_STATES)
            im = slice(j * CHUNK_LANES + CHUNK_STATES, (j + 1) * CHUNK_LANES)

            def blk(bi, car):
                r0 = pl.multiple_of(bi * SUBLANES, SUBLANES)
                xr, xi = _scan_block(h_ref[pl.ds(r0, SUBLANES), re], h_ref[pl.ds(r0, SUBLANES), im],
                                     tab_ref, re, im, car[0], car[1], False)
                h_ref[pl.ds(r0, SUBLANES), re] = xr
                h_ref[pl.ds(r0, SUBLANES), im] = xi
                return xr[SUBLANES - 1:SUBLANES], xi[SUBLANES - 1:SUBLANES]

            cr, ci = lax.fori_loop(0, tt // SUBLANES, blk, (carry[0:1, re], carry[0:1, im]), unroll=2)
            carry[0:1, re] = cr
            carry[0:1, im] = ci
        for j in range(SSM_CHUNKS):
            hj = h_ref[:, j * CHUNK_LANES:(j + 1) * CHUNK_LANES].astype(BF16)
            cols = slice(j * LANES, (j + 1) * LANES)
            y_ref[:, cols] = _dot(hj, wc_ref[j]) + d_ref[:, cols] * u[:, cols]

    return pl.pallas_call(
        body, name="ssm_fwd", grid=(bsz, nt),
        in_specs=[pl.BlockSpec((tt, SSM_WIDTH), lambda b, c: (b * nt + c, u_blk)),
                  pl.BlockSpec((SSM_CHUNKS, LANES, CHUNK_LANES), lambda b, c: (0, 0, 0)),
                  pl.BlockSpec((SSM_CHUNKS, CHUNK_LANES, LANES), lambda b, c: (0, 0, 0)),
                  pl.BlockSpec((TAB_ROWS, STATE_LANES), lambda b, c: (0, 0)),
                  pl.BlockSpec((1, SSM_WIDTH), lambda b, c: (0, 0))],
        out_specs=[pl.BlockSpec((tt, SSM_WIDTH), lambda b, c: (b * nt + c, 0)),
                   pl.BlockSpec((tt, STATE_LANES), lambda b, c: (b * nt + c, 0))],
        out_shape=[jax.ShapeDtypeStruct((bsz * seq, SSM_WIDTH), F32),
                   jax.ShapeDtypeStruct((bsz * seq, STATE_LANES), F32)],
        scratch_shapes=[pltpu.VMEM((SUBLANES, STATE_LANES), F32)],
        compiler_params=_params(("parallel", "arbitrary")),
    )(rest, wb4, wc4, tabs, dskip)


def ssm_bwd(dys, rest, hs, wb4, wc4, tabs, dskip, bsz, seq, u_blk):
    tt = _ssm_tile(seq)
    nt = seq // tt
    nb = tt // SUBLANES

    def body(dy_ref, u_ref, h_ref, hp_ref, wb_ref, wc_ref, tab_ref, d_ref,
             du_ref, ga_ref, gwb_ref, gwc_ref, gd_ref, g_s, carry):
        b = pl.program_id(0)
        c = pl.program_id(1)

        @pl.when(c == 0)
        def _():
            carry[...] = jnp.zeros_like(carry)

        @pl.when((b == 0) & (c == 0))
        def _():
            ga_ref[...] = jnp.zeros_like(ga_ref)
            gwb_ref[...] = jnp.zeros_like(gwb_ref)
            gwc_ref[...] = jnp.zeros_like(gwc_ref)
            gd_ref[...] = jnp.zeros_like(gd_ref)

        dy = dy_ref[...].astype(F32)
        dyb = dy.astype(BF16)
        u = u_ref[...]
        ub = u.astype(BF16)
        first_chunk = c == nt - 1
        for j in range(SSM_CHUNKS):
            g_s[:, j * CHUNK_LANES:(j + 1) * CHUNK_LANES] = _dot(dyb[:, j * LANES:(j + 1) * LANES], wc_ref[j], NT)
        for j in range(SSM_CHUNKS):
            re = slice(j * CHUNK_LANES, j * CHUNK_LANES + CHUNK_STATES)
            im = slice(j * CHUNK_LANES + CHUNK_STATES, (j + 1) * CHUNK_LANES)
            row = lax.broadcasted_iota(jnp.int32, (SUBLANES, CHUNK_STATES), 0)

            def blk(n, car):
                bi = nb - 1 - n
                r0 = pl.multiple_of(bi * SUBLANES, SUBLANES)
                gr, gi = _scan_block(g_s[pl.ds(r0, SUBLANES), re], g_s[pl.ds(r0, SUBLANES), im],
                                     tab_ref, re, im, car[0], car[1], True)
                g_s[pl.ds(r0, SUBLANES), re] = gr
                g_s[pl.ds(r0, SUBLANES), im] = gi
                rp = pl.multiple_of(jnp.maximum(bi - 1, 0) * SUBLANES, SUBLANES)
                inside = bi > 0
                live = jnp.where(jnp.logical_or(inside, jnp.logical_not(first_chunk)), 1.0, 0.0)
                pr = jnp.where(inside, h_ref[pl.ds(rp, SUBLANES), re], hp_ref[:, re])[SUBLANES - 1:SUBLANES] * live
                pi = jnp.where(inside, h_ref[pl.ds(rp, SUBLANES), im], hp_ref[:, im])[SUBLANES - 1:SUBLANES] * live
                hr = jnp.where(row >= 1, pltpu.roll(h_ref[pl.ds(r0, SUBLANES), re], 1, 0), pr)
                hi = jnp.where(row >= 1, pltpu.roll(h_ref[pl.ds(r0, SUBLANES), im], 1, 0), pi)
                return (gr[0:1], gi[0:1], car[2] + gr * hr + gi * hi, car[3] + gi * hr - gr * hi)

            zero = jnp.zeros((SUBLANES, CHUNK_STATES), F32)
            cr, ci, sr, si = lax.fori_loop(0, nb, blk, (carry[0:1, re], carry[0:1, im], zero, zero), unroll=2)
            carry[0:1, re] = cr
            carry[0:1, im] = ci
            ga_ref[:, re] += sr
            ga_ref[:, im] += si
        for j in range(SSM_CHUNKS):
            cols = slice(j * LANES, (j + 1) * LANES)
            lanes = slice(j * CHUNK_LANES, (j + 1) * CHUNK_LANES)
            gj = g_s[:, lanes].astype(BF16)
            du_ref[:, cols] = (_dot(gj, wb_ref[j], NT) + d_ref[:, cols] * dy[:, cols]).astype(du_ref.dtype)
            gwb_ref[j] += _dot(ub[:, cols], gj, TN)
            gwc_ref[j] += _dot(h_ref[:, lanes].astype(BF16), dyb[:, cols], TN)
        gd_ref[...] += _fold8(dy * u)

    def prev_rows(b, c):
        chunk = nt - 1 - c
        return (jnp.maximum((b * nt + chunk) * nb - 1, 0), 0)

    return pl.pallas_call(
        body, name="ssm_bwd", grid=(bsz, nt),
        in_specs=[pl.BlockSpec((tt, SSM_WIDTH), lambda b, c: (b * nt + nt - 1 - c, 0)),
                  pl.BlockSpec((tt, SSM_WIDTH), lambda b, c: (b * nt + nt - 1 - c, u_blk)),
                  pl.BlockSpec((tt, STATE_LANES), lambda b, c: (b * nt + nt - 1 - c, 0)),
                  pl.BlockSpec((SUBLANES, STATE_LANES), prev_rows),
                  pl.BlockSpec((SSM_CHUNKS, LANES, CHUNK_LANES), lambda b, c: (0, 0, 0)),
                  pl.BlockSpec((SSM_CHUNKS, CHUNK_LANES, LANES), lambda b, c: (0, 0, 0)),
                  pl.BlockSpec((TAB_ROWS, STATE_LANES), lambda b, c: (0, 0)),
                  pl.BlockSpec((1, SSM_WIDTH), lambda b, c: (0, 0))],
        out_specs=[pl.BlockSpec((tt, SSM_WIDTH), lambda b, c: (b * nt + nt - 1 - c, 0)),
                   pl.BlockSpec((SUBLANES, STATE_LANES), lambda b, c: (0, 0)),
                   pl.BlockSpec((SSM_CHUNKS, LANES, CHUNK_LANES), lambda b, c: (0, 0, 0)),
                   pl.BlockSpec((SSM_CHUNKS, CHUNK_LANES, LANES), lambda b, c: (0, 0, 0)),
                   pl.BlockSpec((SUBLANES, SSM_WIDTH), lambda b, c: (0, 0))],
        out_shape=[jax.ShapeDtypeStruct((bsz * seq, SSM_WIDTH), BF16),
                   jax.ShapeDtypeStruct((SUBLANES, STATE_LANES), F32),
                   jax.ShapeDtypeStruct((SSM_CHUNKS, LANES, CHUNK_LANES), F32),
                   jax.ShapeDtypeStruct((SSM_CHUNKS, CHUNK_LANES, LANES), F32),
                   jax.ShapeDtypeStruct((SUBLANES, SSM_WIDTH), F32)],
        scratch_shapes=[pltpu.VMEM((tt, STATE_LANES), F32), pltpu.VMEM((SUBLANES, STATE_LANES), F32)],
        compiler_params=_params(("arbitrary", "arbitrary")),
    )(dys, rest, hs, hs, wb4, wc4, tabs, dskip)


def _gelu(v):
    t = jnp.tanh(GELU_C * (v + GELU_A * v * v * v))
    return 0.5 * v * (1.0 + t), t


def mix_fwd(ya, ys, rest, x0, wglu, bglu, wba, wbb, wout):
    t, d = x0.shape
    tm = _rows(t, 512)

    def body(ya_ref, ys_ref, ga_ref, gb_ref, x_ref, wg_ref, bg_ref, wa_ref, wb_ref, wo_ref,
             x1_ref, z_ref, pa_ref, pb_ref, yb_ref, yb2_ref, mx_ref):
        yb, _ = _gelu(ys_ref[...])
        ybb = yb.astype(BF16)
        z = _dot(ybb, wg_ref[...]) + bg_ref[...]
        yb2 = (yb * _sigmoid(z)).astype(BF16)
        pa = _dot(ya_ref[...], wa_ref[...])
        pb = _dot(yb2, wb_ref[...])
        mixed = (_sigmoid(ga_ref[...]) * pa + _sigmoid(gb_ref[...]) * pb).astype(BF16)
        x1_ref[...] = x_ref[...] + _dot(mixed, wo_ref[...])
        z_ref[...] = z.astype(z_ref.dtype)
        pa_ref[...] = pa.astype(pa_ref.dtype)
        pb_ref[...] = pb.astype(pb_ref.dtype)
        yb_ref[...] = ybb
        yb2_ref[...] = yb2
        mx_ref[...] = mixed

    row = lambda w: pl.BlockSpec((tm, w), lambda i: (i, 0))
    full = lambda a: pl.BlockSpec(a.shape, lambda i: (0,) * a.ndim)
    return pl.pallas_call(
        body, name="mix_fwd", grid=(t // tm,),
        in_specs=[row(ATTN_WIDTH), row(SSM_WIDTH),
                  pl.BlockSpec((tm, d), lambda i: (i, 0)), pl.BlockSpec((tm, d), lambda i: (i, 1)),
                  row(d), full(wglu), full(bglu), full(wba), full(wbb), full(wout)],
        out_specs=[row(d), row(SSM_WIDTH), row(d), row(d), row(SSM_WIDTH), row(SSM_WIDTH), row(d)],
        out_shape=[jax.ShapeDtypeStruct((t, d), F32), jax.ShapeDtypeStruct((t, SSM_WIDTH), BF16),
                   jax.ShapeDtypeStruct((t, d), BF16), jax.ShapeDtypeStruct((t, d), BF16),
                   jax.ShapeDtypeStruct((t, SSM_WIDTH), BF16), jax.ShapeDtypeStruct((t, SSM_WIDTH), BF16),
                   jax.ShapeDtypeStruct((t, d), BF16)],
        compiler_params=_params(("parallel",)),
    )(ya, ys, rest, rest, x0, wglu, bglu, wba, wbb, wout)


def mix_bwd(dx1, rest, pa, pb, z, ys, wglu, wba, wbb, wout):
    t, d = dx1.shape
    tm = _rows(t, 512)

    def body(dx_ref, ga_ref, gb_ref, pa_ref, pb_ref, z_ref, ys_ref, wg_ref, wa_ref, wb_ref, wo_ref,
             dya_ref, dys_ref, dg_ref, dpa_ref, dpb_ref, dz_ref, dbg_ref):
        @pl.when(pl.program_id(0) == 0)
        def _():
            dbg_ref[...] = jnp.zeros_like(dbg_ref)

        dmix = _dot(dx_ref[...].astype(BF16), wo_ref[...], NT)
        sa = _sigmoid(ga_ref[...])
        sb = _sigmoid(gb_ref[...])
        dpa = (dmix * sa).astype(BF16)
        dpb = (dmix * sb).astype(BF16)
        dg_ref[:, 0:d] = (dmix * pa_ref[...].astype(F32) * sa * (1.0 - sa)).astype(dg_ref.dtype)
        dg_ref[:, d:2 * d] = (dmix * pb_ref[...].astype(F32) * sb * (1.0 - sb)).astype(dg_ref.dtype)
        dpa_ref[...] = dpa
        dpb_ref[...] = dpb
        dya_ref[...] = _dot(dpa, wa_ref[...], NT).astype(dya_ref.dtype)
        dyb2 = _dot(dpb, wb_ref[...], NT)
        ys = ys_ref[...]
        yb, th = _gelu(ys)
        sg = _sigmoid(z_ref[...].astype(F32))
        dz = dyb2 * yb * sg * (1.0 - sg)
        dzb = dz.astype(BF16)
        dz_ref[...] = dzb
        dbg_ref[...] += _fold8(dz)
        dyb = dyb2 * sg + _dot(dzb, wg_ref[...], NT)
        dgelu = 0.5 * (1.0 + th) + 0.5 * ys * (1.0 - th * th) * GELU_C * (1.0 + 3.0 * GELU_A * ys * ys)
        dys_ref[...] = (dyb * dgelu).astype(dys_ref.dtype)

    row = lambda w: pl.BlockSpec((tm, w), lambda i: (i, 0))
    full = lambda a: pl.BlockSpec(a.shape, lambda i: (0,) * a.ndim)
    return pl.pallas_call(
        body, name="mix_bwd", grid=(t // tm,),
        in_specs=[row(d), pl.BlockSpec((tm, d), lambda i: (i, 0)), pl.BlockSpec((tm, d), lambda i: (i, 1)),
                  row(d), row(d), row(SSM_WIDTH), row(SSM_WIDTH), full(wglu), full(wba), full(wbb), full(wout)],
        out_specs=[row(ATTN_WIDTH), row(SSM_WIDTH), row(2 * d), row(d), row(d), row(SSM_WIDTH),
                   pl.BlockSpec((SUBLANES, SSM_WIDTH), lambda i: (0, 0))],
        out_shape=[jax.ShapeDtypeStruct((t, ATTN_WIDTH), BF16), jax.ShapeDtypeStruct((t, SSM_WIDTH), BF16),
                   jax.ShapeDtypeStruct((t, 2 * d), BF16), jax.ShapeDtypeStruct((t, d), BF16),
                   jax.ShapeDtypeStruct((t, d), BF16), jax.ShapeDtypeStruct((t, SSM_WIDTH), BF16),
                   jax.ShapeDtypeStruct((SUBLANES, SSM_WIDTH), F32)],
        compiler_params=_params(("arbitrary",)),
    )(dx1, rest, rest, pa, pb, z, ys, wglu, wba, wbb, wout)


def mlp_fwd(x1, g, wup, wdown):
    t, d = x1.shape
    ff = wup.shape[1]
    tm, tf = _rows(t, 1024), _pick(ff, 1024)
    nf = ff // tf

    def body(x_ref, g_ref, wu_ref, wd_ref, x2_ref, up_ref, h_s, acc_s):
        f = pl.program_id(1)

        @pl.when(f == 0)
        def _():
            xv = x_ref[...]
            h_s[...] = (xv * _rms_scale(xv) * g_ref[...]).astype(BF16)
            acc_s[...] = jnp.zeros_like(acc_s)

        up = _dot(h_s[...], wu_ref[...])
        up_ref[...] = up.astype(up_ref.dtype)
        act = jnp.square(jnp.maximum(up, 0.0)).astype(BF16)
        acc_s[...] += _dot(act, wd_ref[...])

        @pl.when(f == nf - 1)
        def _():
            x2_ref[...] = x_ref[...] + acc_s[...]

    return pl.pallas_call(
        body, name="mlp_fwd", grid=(t // tm, nf),
        in_specs=[pl.BlockSpec((tm, d), lambda i, f: (i, 0)), pl.BlockSpec((1, d), lambda i, f: (0, 0)),
                  pl.BlockSpec((d, tf), lambda i, f: (0, f)), pl.BlockSpec((tf, d), lambda i, f: (f, 0))],
        out_specs=[pl.BlockSpec((tm, d), lambda i, f: (i, 0)), pl.BlockSpec((tm, tf), lambda i, f: (i, f))],
        out_shape=[jax.ShapeDtypeStruct((t, d), F32), jax.ShapeDtypeStruct((t, ff), BF16)],
        scratch_shapes=[pltpu.VMEM((tm, d), BF16), pltpu.VMEM((tm, d), F32)],
        compiler_params=_params(("parallel", "arbitrary")),
    )(x1, g, wup, wdown)


def mlp_bwd(dx2, up, x1, g, wup, wdown):
    t, d = x1.shape
    ff = wup.shape[1]
    tm, tf = _rows(t, 512), _pick(ff, 1024)
    nf = ff // tf

    def body(dx_ref, up_ref, x_ref, g_ref, wu_ref, wd_ref, dup_ref, dx1_ref, dg_ref, dxb_s, acc_s):
        i = pl.program_id(0)
        f = pl.program_id(1)

        @pl.when((i == 0) & (f == 0))
        def _():
            dg_ref[...] = jnp.zeros_like(dg_ref)

        @pl.when(f == 0)
        def _():
            dxb_s[...] = dx_ref[...].astype(BF16)
            acc_s[...] = jnp.zeros_like(acc_s)

        dact = _dot(dxb_s[...], wd_ref[...], NT)
        dup = (dact * 2.0 * jnp.maximum(up_ref[...].astype(F32), 0.0)).astype(BF16)
        dup_ref[...] = dup
        acc_s[...] += _dot(dup, wu_ref[...], NT)

        @pl.when(f == nf - 1)
        def _():
            dxn, dgain = _rms_bwd(x_ref[...], g_ref[...], acc_s[...])
            dx1_ref[...] = dx_ref[...] + dxn
            dg_ref[...] += _fold8(dgain)

    return pl.pallas_call(
        body, name="mlp_bwd", grid=(t // tm, nf),
        in_specs=[pl.BlockSpec((tm, d), lambda i, f: (i, 0)), pl.BlockSpec((tm, tf), lambda i, f: (i, f)),
                  pl.BlockSpec((tm, d), lambda i, f: (i, 0)), pl.BlockSpec((1, d), lambda i, f: (0, 0)),
                  pl.BlockSpec((d, tf), lambda i, f: (0, f)), pl.BlockSpec((tf, d), lambda i, f: (f, 0))],
        out_specs=[pl.BlockSpec((tm, tf), lambda i, f: (i, f)), pl.BlockSpec((tm, d), lambda i, f: (i, 0)),
                   pl.BlockSpec((SUBLANES, d), lambda i, f: (0, 0))],
        out_shape=[jax.ShapeDtypeStruct((t, ff), BF16), jax.ShapeDtypeStruct((t, d), F32),
                   jax.ShapeDtypeStruct((SUBLANES, d), F32)],
        scratch_shapes=[pltpu.VMEM((tm, d), BF16), pltpu.VMEM((tm, d), F32)],
        compiler_params=_params(("arbitrary", "arbitrary")),
    )(dx2, up, x1, g, wup, wdown)


def proj_bwd(dproj, wpad, x0, dx1, g):
    t, d = x0.shape
    m = wpad.shape[1]
    tm = _rows(t, 512)

    def body(dp_ref, w_ref, x_ref, dx1_ref, g_ref, dx0_ref, dg_ref):
        @pl.when(pl.program_id(0) == 0)
        def _():
            dg_ref[...] = jnp.zeros_like(dg_ref)

        dh = _dot(dp_ref[...], w_ref[...], NT)
        dxn, dgain = _rms_bwd(x_ref[...], g_ref[...], dh)
        dx0_ref[...] = dx1_ref[...] + dxn
        dg_ref[...] += _fold8(dgain)

    return pl.pallas_call(
        body, name="proj_bwd", grid=(t // tm,),
        in_specs=[pl.BlockSpec((tm, m), lambda i: (i, 0)), pl.BlockSpec((d, m), lambda i: (0, 0)),
                  pl.BlockSpec((tm, d), lambda i: (i, 0)), pl.BlockSpec((tm, d), lambda i: (i, 0)),
                  pl.BlockSpec((1, d), lambda i: (0, 0))],
        out_specs=[pl.BlockSpec((tm, d), lambda i: (i, 0)), pl.BlockSpec((SUBLANES, d), lambda i: (0, 0))],
        out_shape=[jax.ShapeDtypeStruct((t, d), F32), jax.ShapeDtypeStruct((SUBLANES, d), F32)],
        compiler_params=_params(("arbitrary",)),
    )(dproj, wpad, x0, dx1, g)


def final_loss(x, g, target):
    t, d = x.shape
    tm = _rows(t, 512)

    def body(x_ref, g_ref, t_ref, dx_ref, ls_ref, dg_ref):
        @pl.when(pl.program_id(0) == 0)
        def _():
            ls_ref[...] = jnp.zeros_like(ls_ref)
            dg_ref[...] = jnp.zeros_like(dg_ref)

        xv = x_ref[...]
        gv = g_ref[...]
        err = xv * _rms_scale(xv) * gv - t_ref[...]
        ls_ref[...] += _fold8(err * err) * (0.5 / d)
        dxn, dgain = _rms_bwd(xv, gv, err * (1.0 / d))
        dx_ref[...] = dxn
        dg_ref[...] += _fold8(dgain)

    return pl.pallas_call(
        body, name="final_loss", grid=(t // tm,),
        in_specs=[pl.BlockSpec((tm, d), lambda i: (i, 0)), pl.BlockSpec((1, d), lambda i: (0, 0)),
                  pl.BlockSpec((tm, d), lambda i: (i, 0))],
        out_specs=[pl.BlockSpec((tm, d), lambda i: (i, 0)), pl.BlockSpec((SUBLANES, d), lambda i: (0, 0)),
                   pl.BlockSpec((SUBLANES, d), lambda i: (0, 0))],
        out_shape=[jax.ShapeDtypeStruct((t, d), F32), jax.ShapeDtypeStruct((SUBLANES, d), F32),
                   jax.ShapeDtypeStruct((SUBLANES, d), F32)],
        compiler_params=_params(("arbitrary",)),
    )(x, g, target)


def _peers():
    x, y, c = lax.axis_index("x"), lax.axis_index("y"), lax.axis_index("c")
    peers = []
    for m in range(1, 8):
        fx, fy, fc = (m >> 2) & 1, (m >> 1) & 1, m & 1
        peers.append((m, ((1 - x) if fx else x, (1 - y) if fy else y, (1 - c) if fc else c)))
    return x, y, c, peers


def _remote(src, dst, sems, k, peer):
    return pltpu.make_async_remote_copy(src_ref=src, dst_ref=dst, send_sem=sems[0].at[k], recv_sem=sems[1].at[k],
                                        device_id=peer, device_id_type=MESH)


def _gather_copies(w_refs, out_refs, sems, first):
    x, y, c, peers = _peers()
    chip = 2 * x + y
    cps = []
    for t, (w, o) in enumerate(zip(w_refs, out_refs)):
        half = w.shape[0] // 2
        rows = pl.ds(c * half, half)
        for m, peer in peers:
            if m >> 1:
                cps.append(_remote(w.at[rows], o.at[chip, rows], sems, first + 7 * t + m - 1, peer))
    return cps


def _reduce_copies(g_refs, out_refs, sems, first):
    x, y, c, peers = _peers()
    me = 4 * x + 2 * y + c
    cps = []
    for t, (g, o) in enumerate(zip(g_refs, out_refs)):
        for m, (px, py, pc) in peers:
            cps.append(_remote(g.at[2 * px + py], o.at[me], sems, first + 7 * t + m - 1, (px, py, pc)))
    return cps


def _all_copies(s_refs, out_refs, sems, first):
    x, y, c, peers = _peers()
    me = 4 * x + 2 * y + c
    return [_remote(s, o.at[me], sems, first + 7 * t + m - 1, peer)
            for t, (s, o) in enumerate(zip(s_refs, out_refs)) for m, peer in peers]


EXCHANGES = {"gather": (_gather_copies, lambda a: (4,) + a.shape),
             "reduce": (_reduce_copies, lambda a: (8,) + a.shape[1:]),
             "all": (_all_copies, lambda a: (8,) + a.shape)}


def _carry_plan(carries):
    inputs, shapes, spans = [], [], []
    for kind, arrs in carries:
        for a in arrs:
            inputs.append(a)
            shapes.append(jax.ShapeDtypeStruct(EXCHANGES[kind][1](a), a.dtype))
        spans.append((kind, len(arrs)))

    def build(in_refs, out_refs, sems):
        cps, pos = [], 0
        for kind, cnt in spans:
            cps += EXCHANGES[kind][0](in_refs[pos:pos + cnt], out_refs[pos:pos + cnt], sems, 7 * pos)
            pos += cnt
        return cps

    return inputs, shapes, 7 * len(inputs), build


def exchange(carries, name):
    inputs, shapes, n_sems, build = _carry_plan(carries)
    n = len(inputs)

    def body(*refs):
        cps = build(refs[:n], refs[n:2 * n], refs[2 * n:])
        for cp in cps:
            cp.start()
        for cp in cps:
            cp.wait()

    return pl.pallas_call(
        body, name=name, in_specs=[ANY] * n, out_specs=[ANY] * n, out_shape=shapes,
        scratch_shapes=[pltpu.SemaphoreType.DMA((n_sems,)), pltpu.SemaphoreType.DMA((n_sems,))],
    )(*inputs)


def _adamw_math(w, g, m, v):
    m = ADAM_B1 * m + (1.0 - ADAM_B1) * g
    v = ADAM_B2 * v + (1.0 - ADAM_B2) * (g * g)
    m_hat = m / (1.0 - ADAM_B1 ** ADAM_STEP)
    v_hat = v / (1.0 - ADAM_B2 ** ADAM_STEP)
    delta = -ADAM_LR * (m_hat / (jnp.sqrt(v_hat) + ADAM_EPS) + ADAM_WD * w)
    return delta, m, v


def adamw_layer(l, w, m, v, parts, own, device, chip, bufs, name):
    _, r, cdim = w.shape
    tr = _rows(r, PACK_ROW_TILE)

    def body(dev_ref, chip_ref, w_ref, m_ref, v_ref, p_ref, o_ref, *rest):
        g_ref, d_ref, nm_ref, nv_ref = rest[-4:]
        g = None
        for dev in range(8):
            part = jnp.where(dev_ref[0] == dev, o_ref[0], p_ref[dev]).astype(F32)
            g = part if g is None else g + part
        d, nm, nv = _adamw_math(w_ref[0], g, m_ref[0], v_ref[0])
        g_ref[0] = g
        d_ref[0] = d
        nm_ref[0] = nm
        nv_ref[0] = nv

    lay = pl.BlockSpec((1, tr, cdim), lambda i, dev_ref, chip_ref: (l, i, 0))
    in_specs = [lay, lay, lay, pl.BlockSpec((8, tr, cdim), lambda i, dev_ref, chip_ref: (0, i, 0)),
                pl.BlockSpec((1, tr, cdim), lambda i, dev_ref, chip_ref: (chip_ref[0], i, 0))]
    args = [device, chip, w, m, v, parts, own]
    aliases = {}
    if bufs is not None:
        in_specs += [ANY] * 4
        aliases = {len(args) + k: k for k in range(4)}
        args += list(bufs)
    return pl.pallas_call(
        body, name=name,
        grid_spec=pltpu.PrefetchScalarGridSpec(num_scalar_prefetch=2, grid=(r // tr,), in_specs=in_specs,
                                               out_specs=[lay] * 4),
        out_shape=[jax.ShapeDtypeStruct(w.shape, F32)] * 4,
        input_output_aliases=aliases,
        compiler_params=_params(("parallel",)),
    )(*args)


def sum_and_adamw(parts, own, me, w, m, v):
    _, r, cdim = parts.shape
    tr = _rows(r, PACK_ROW_TILE)

    def body(me_ref, p_ref, o_ref, w_ref, m_ref, v_ref, g_ref, d_ref, nm_ref, nv_ref):
        part = lambda k: jnp.where(me_ref[0] == k, o_ref[...], p_ref[k]).astype(F32)
        g = part(0)
        for k in range(1, 8):
            g = g + part(k)
        d, nm, nv = _adamw_math(w_ref[...], g, m_ref[...], v_ref[...])
        g_ref[...] = g
        d_ref[...] = d
        nm_ref[...] = nm
        nv_ref[...] = nv

    spec = pl.BlockSpec((tr, cdim), lambda i, me_ref: (i, 0))
    return pl.pallas_call(
        body, name="sum_and_adamw",
        grid_spec=pltpu.PrefetchScalarGridSpec(
            num_scalar_prefetch=1, grid=(r // tr,),
            in_specs=[pl.BlockSpec((8, tr, cdim), lambda i, me_ref: (0, i, 0)), spec, spec, spec, spec],
            out_specs=[spec] * 4),
        out_shape=[jax.ShapeDtypeStruct((r, cdim), F32)] * 4,
        compiler_params=_params(("parallel",)),
    )(me, parts, own, w, m, v)


SHARDED = ("w_in", "w_glu", "w_branch_a", "w_branch_b", "w_out", "w_mlp_up", "w_mlp_down")
SHARD_AXIS = {"w_in": 2, "w_glu": 1, "w_branch_a": 2, "w_branch_b": 2, "w_out": 1, "w_mlp_up": 2, "w_mlp_down": 1}
SMALL = ("norm_mix", "b_forget", "ssm_lambda_re", "ssm_lambda_im", "ssm_log_dt", "ssm_b_re", "ssm_b_im",
         "ssm_c_re", "ssm_c_im", "ssm_d", "b_glu", "norm_mlp", "norm_final")
SMALL_WIDE = ("ssm_b_re", "ssm_b_im", "ssm_c_re", "ssm_c_im")


def pack_flat(arrs):
    flat = jnp.concatenate([a.reshape(-1).astype(F32) for a in arrs])
    unit = PACK_COLS * PACK_ROW_TILE
    rows = (flat.shape[0] + unit - 1) // unit * PACK_ROW_TILE
    return jnp.pad(flat, (0, rows * PACK_COLS - flat.shape[0])).reshape(rows, PACK_COLS)


def unpack_flat(packed, shapes):
    flat = packed.reshape(-1)
    out, off = [], 0
    for s in shapes:
        n = math.prod(s)
        out.append(flat[off:off + n].reshape(tuple(s)))
        off += n
    return out


def _discretise(lam_re, lam_im, log_dt, b_re, b_im):
    dt = jnp.exp(log_dt)[:, None]
    mag = jnp.exp(lam_re * dt)
    ar = mag * jnp.cos(lam_im * dt)
    ai = mag * jnp.sin(lam_im * dt)
    den = lam_re * lam_re + lam_im * lam_im
    cr = ((ar - 1.0) * lam_re + ai * lam_im) / den
    ci = (ai * lam_re - (ar - 1.0) * lam_im) / den
    bbr = cr[:, :, None] * b_re - ci[:, :, None] * b_im
    bbi = cr[:, :, None] * b_im + ci[:, :, None] * b_re
    return ar, ai, bbr, bbi


def _ssm_inputs(ar, ai, bbr, bbi, c_re, c_im):
    pr, pi = [ar], [ai]
    for _ in range(SUBLANES - 1):
        pr, pi = pr + [pr[-1] * ar - pi[-1] * ai], pi + [pr[-1] * ai + pi[-1] * ar]
    powers = jnp.stack([jnp.stack(pr), jnp.stack(pi)], axis=1)
    powers = powers.reshape(SUBLANES, 2, SSM_CHUNKS, CHUNK_STATES).transpose(0, 2, 1, 3).reshape(SUBLANES, STATE_LANES)
    conj = powers * jnp.tile(jnp.repeat(jnp.asarray([1.0, -1.0], F32), CHUNK_STATES), SSM_CHUNKS)
    idx = jnp.arange(SUBLANES)[:, None]
    tabs = jnp.concatenate(
        [jnp.where(idx >= s, powers[s - 1][None, :], 0.0) for s in SCAN_STEPS] + [powers]
        + [jnp.where(idx < SUBLANES - s, conj[s - 1][None, :], 0.0) for s in SCAN_STEPS] + [conj[::-1]], axis=0)
    eye = jnp.eye(CHUNK_GROUPS, dtype=F32)
    wb = jnp.stack([bbr, bbi]).reshape(2, SSM_CHUNKS, CHUNK_GROUPS, SSM_STATE, SSM_GROUP_CH).transpose(1, 2, 4, 0, 3)
    wb4 = (wb[:, :, :, :, None, :] * eye[None, :, None, None, :, None]).reshape(SSM_CHUNKS, LANES, CHUNK_LANES)
    wc = jnp.stack([c_re, -c_im]).reshape(2, SSM_CHUNKS, CHUNK_GROUPS, SSM_GROUP_CH, SSM_STATE).transpose(1, 0, 4, 2, 3)
    wc4 = (wc[:, :, None, :, :, :] * eye[None, None, :, None, :, None]).reshape(SSM_CHUNKS, CHUNK_LANES, LANES)
    return tabs, wb4.astype(BF16), wc4.astype(BF16)


def _ssm_param_grads(ga8, gwb, gwc):
    eye = jnp.eye(CHUNK_GROUPS, dtype=F32)
    ga = jnp.sum(ga8, axis=0).reshape(SSM_CHUNKS, 2, CHUNK_STATES)
    gar = ga[:, 0].reshape(SSM_GROUPS, SSM_STATE)
    gai = ga[:, 1].reshape(SSM_GROUPS, SSM_STATE)

    def from_wb(g):
        t = g.reshape(SSM_CHUNKS, CHUNK_GROUPS, SSM_GROUP_CH, CHUNK_GROUPS, SSM_STATE)
        return jnp.einsum("jgchp,gh->jgpc", t, eye).reshape(SSM_GROUPS, SSM_STATE, SSM_GROUP_CH)

    def from_wc(g):
        t = g.reshape(SSM_CHUNKS, CHUNK_GROUPS, SSM_STATE, CHUNK_GROUPS, SSM_GROUP_CH)
        return jnp.einsum("jhpgc,gh->jgcp", t, eye).reshape(SSM_GROUPS, SSM_GROUP_CH, SSM_STATE)

    return (gar, gai, from_wb(gwb[:, :, :CHUNK_STATES]), from_wb(gwb[:, :, CHUNK_STATES:]),
            from_wc(gwc[:, :CHUNK_STATES]), -from_wc(gwc[:, CHUNK_STATES:]))


def kernel(x, norm_mix, w_in, b_forget, ssm_lambda_re, ssm_lambda_im, ssm_log_dt, ssm_b_re, ssm_b_im, ssm_c_re, ssm_c_im, ssm_d, w_glu, b_glu, w_branch_a, w_branch_b, w_out, norm_mlp, w_mlp_up, w_mlp_down, norm_final, loss_target, m_norm_mix, m_w_in, m_b_forget, m_ssm_lambda_re, m_ssm_lambda_im, m_ssm_log_dt, m_ssm_b_re, m_ssm_b_im, m_ssm_c_re, m_ssm_c_im, m_ssm_d, m_w_glu, m_b_glu, m_w_branch_a, m_w_branch_b, m_w_out, m_norm_mlp, m_w_mlp_up, m_w_mlp_down, m_norm_final, v_norm_mix, v_w_in, v_b_forget, v_ssm_lambda_re, v_ssm_lambda_im, v_ssm_log_dt, v_ssm_b_re, v_ssm_b_im, v_ssm_c_re, v_ssm_c_im, v_ssm_d, v_w_glu, v_b_glu, v_w_branch_a, v_w_branch_b, v_w_out, v_norm_mlp, v_w_mlp_up, v_w_mlp_down, v_norm_final):
    args = dict(locals())
    bsz, seq, d = x.shape
    nl = norm_mix.shape[0]
    tokens = bsz * seq
    aw, sw = ATTN_WIDTH, SSM_WIDTH
    chip = (2 * lax.axis_index("x") + lax.axis_index("y")).astype(jnp.int32)
    chip_id = chip.reshape(1)
    device_id = (2 * chip + lax.axis_index("c").astype(jnp.int32)).reshape(1)

    own = {n: args[n].astype(BF16) for n in SHARDED}
    late = [n for n in SHARDED if n != "w_in"]
    o_f, o_u, o_ga, o_gb = 3 * aw, 3 * aw + ATTN_HEADS, 3 * aw + ATTN_HEADS + sw, 3 * aw + ATTN_HEADS + sw + d
    u_blk = 2 * d // sw
    f_blk = (2 * d + sw) // F_PAD
    bf_pad = jnp.pad(b_forget, ((0, 0), (0, F_PAD - ATTN_HEADS)))
    q_scale = jnp.concatenate([jnp.full((1, aw), HEAD_DIM ** -0.5 * LOG2E, F32), jnp.ones((1, 2 * aw), F32)], axis=1)

    def assemble(l, names, gathered):
        return {n: jnp.concatenate([jnp.where(chip == k, own[n][l], g[k]) for k in range(4)],
                                   axis=SHARD_AXIS[n] - 1) for n, g in zip(names, gathered)}

    def split_w_in(win):
        w = {"w_qkv": win[:, :o_f],
             "w_rest": jnp.concatenate([win[:, o_ga:o_gb], win[:, o_gb:], win[:, o_u:o_ga],
                                        jnp.pad(win[:, o_f:o_u], ((0, 0), (0, F_PAD - ATTN_HEADS)))], axis=1)}
        w["w_pad"] = jnp.concatenate([w["w_qkv"], w["w_rest"]], axis=1)
        return w

    disc = [jax.vjp(_discretise, ssm_lambda_re[l], ssm_lambda_im[l], ssm_log_dt[l], ssm_b_re[l], ssm_b_im[l])
            for l in range(nl)]

    g_in = exchange([("gather", [own["w_in"][0]])], "gather_first")
    xs = x.reshape(tokens, d)
    saved, weights = [], []
    for l in range(nl):
        w = split_w_in(assemble(l, ["w_in"], g_in)["w_in"])
        g1 = norm_mix[l].reshape(1, d)
        qkv = norm_matmul(xs, g1, w["w_qkv"], BF16, "proj_qkv", col_scale=q_scale)
        rest = norm_matmul(xs, g1, w["w_rest"], F32, "proj_rest")
        cumcol, cumrow = forget_cumsum(rest, bf_pad[l:l + 1], bsz, seq, f_blk)
        carries = [("gather", [own[n][l] for n in late] + ([own["w_in"][l + 1]] if l + 1 < nl else []))]
        ya, lser, lands = fox_fwd(qkv, cumrow, bsz, seq, carries)
        w.update(assemble(l, late, lands))
        g_in = lands[len(late):]
        tabs, wb4, wc4 = _ssm_inputs(*disc[l][0], ssm_c_re[l], ssm_c_im[l])
        dskip = ssm_d[l].reshape(1, sw)
        ys, hs = ssm_fwd(rest, wb4, wc4, tabs, dskip, bsz, seq, u_blk)
        x1, z, pa, pb, yb, yb2, mixed = mix_fwd(ya, ys, rest, xs, w["w_glu"], b_glu[l].reshape(1, sw),
                                                 w["w_branch_a"], w["w_branch_b"], w["w_out"])
        x2, up = mlp_fwd(x1, norm_mlp[l].reshape(1, d), w["w_mlp_up"], w["w_mlp_down"])
        saved.append(dict(x0=xs, qkv=qkv, rest=rest, cumcol=cumcol, ya=ya, lser=lser,
                          tabs=tabs, wb4=wb4, wc4=wc4, dskip=dskip, ys=ys, hs=hs, x1=x1, z=z, pa=pa, pb=pb,
                          yb=yb, yb2=yb2, mixed=mixed, up=up))
        weights.append(w)
        xs = x2
    dx, loss_rows, dgf_rows = final_loss(xs, norm_final.reshape(1, d), loss_target.reshape(tokens, d))
    loss = lax.psum(jnp.sum(loss_rows), ("x", "y", "c"))

    early = [n for n in SHARDED if n != "w_in"]
    big = {n: [None] * nl for n in SHARDED}
    parts = {n: [None] * nl for n in SHARDED}
    small = {n: [None] * nl for n in SMALL if n != "norm_final"}
    for l in reversed(range(nl)):
        s, w = saved[l], weights[l]
        g2 = norm_mlp[l].reshape(1, d)
        dup, dx1, dg2 = mlp_bwd(dx, s["up"], s["x1"], g2, w["w_mlp_up"], w["w_mlp_down"])
        big["w_mlp_down"][l] = matmul_tn(s["up"], dx, "grad_w_mlp_down", a_kind="relu2", shard_axis=0, out_dtype=BF16)
        big["w_mlp_up"][l] = matmul_tn(s["x1"], dup, "grad_w_mlp_up", a_kind="norm", gain=g2, shard_axis=1,
                                       out_dtype=BF16)
        small["norm_mlp"][l] = jnp.sum(dg2, axis=0)
        dya, dys, dgab, dpa, dpb, dz, dbg = mix_bwd(dx1, s["rest"], s["pa"], s["pb"], s["z"], s["ys"],
                                                    w["w_glu"], w["w_branch_a"], w["w_branch_b"], w["w_out"])
        big["w_out"][l] = matmul_tn(s["mixed"], dx1, "grad_w_out", shard_axis=0, out_dtype=BF16)
        big["w_branch_a"][l] = matmul_tn(s["ya"], dpa, "grad_w_branch_a", shard_axis=1, out_dtype=BF16)
        big["w_branch_b"][l] = matmul_tn(s["yb2"], dpb, "grad_w_branch_b", shard_axis=1, out_dtype=BF16)
        big["w_glu"][l] = matmul_tn(s["yb"], dz, "grad_w_glu", shard_axis=0, out_dtype=BF16)
        small["b_glu"][l] = jnp.sum(dbg, axis=0)
        du, ga8, gwb, gwc, gd8 = ssm_bwd(dys, s["rest"], s["hs"], s["wb4"], s["wc4"], s["tabs"], s["dskip"],
                                         bsz, seq, u_blk)
        gar, gai, gbbr, gbbi, gcr, gci = _ssm_param_grads(ga8, gwb, gwc)
        glr, gli, gdt, gbr, gbi = disc[l][1]((gar, gai, gbbr, gbbi))
        small["ssm_lambda_re"][l], small["ssm_lambda_im"][l], small["ssm_log_dt"][l] = glr, gli, gdt
        small["ssm_b_re"][l], small["ssm_b_im"][l] = gbr, gbi
        small["ssm_c_re"][l], small["ssm_c_im"][l] = gcr, gci
        small["ssm_d"][l] = jnp.sum(gd8, axis=0)
        carries = [("reduce", [big[n][l] for n in early])]
        if l + 1 < nl:
            carries.append(("reduce", [big["w_in"][l + 1]]))
        dq, dk, dv, df, dbf, lands = fox_bwd(s["qkv"], dya, s["ya"], s["lser"], s["cumcol"],
                                             s["rest"], bf_pad[l:l + 1], bsz, seq, f_blk, carries)
        for n, p in zip(early, lands):
            parts[n][l] = p
        if l + 1 < nl:
            parts["w_in"][l + 1] = lands[len(early)]
        small["b_forget"][l] = jnp.sum(dbf, axis=0)[:ATTN_HEADS]
        dproj = jnp.concatenate([dq, dk, dv, dgab, du, df], axis=1)
        g1 = norm_mix[l].reshape(1, d)
        dwp = matmul_tn(s["x0"], dproj, "grad_w_in", a_kind="norm", gain=g1)
        big["w_in"][l] = jnp.stack(jnp.split(jnp.concatenate(
            [dwp[:, :o_f], dwp[:, o_f + 2 * d + sw:o_f + 2 * d + sw + ATTN_HEADS],
             dwp[:, o_f + 2 * d:o_f + 2 * d + sw], dwp[:, o_f:o_f + 2 * d]], axis=1), 4, axis=1)).astype(BF16)
        dx, dg1 = proj_bwd(dproj, w["w_pad"], s["x0"], dx1, g1)
        small["norm_mix"][l] = jnp.sum(dg1, axis=0)
    grad_x = dx.reshape(bsz, seq, d)

    small_g = {n: jnp.stack(small[n]) if n != "norm_final" else jnp.sum(dgf_rows, axis=0) for n in SMALL}
    groups = {BF16: SMALL_WIDE, F32: [n for n in SMALL if n not in SMALL_WIDE]}
    packed = {dt: pack_flat([small_g[n] for n in names]).astype(dt) for dt, names in groups.items()}
    parts["w_in"][0], *small_parts = exchange(
        [("reduce", [big["w_in"][0]]), ("all", [packed[dt] for dt in groups])], "exchange_last")

    out_g, out_d, out_m, out_v = {}, {}, {}, {}
    for n in SHARDED:
        bufs = None
        for l in range(nl):
            bufs = adamw_layer(l, args[n], args["m_" + n], args["v_" + n], parts[n][l], big[n][l], device_id, chip_id,
                               bufs, "adamw_" + n)
        out_g[n], out_d[n], out_m[n], out_v[n] = bufs

    for (dt, names), landed in zip(groups.items(), small_parts):
        results = sum_and_adamw(landed, packed[dt], device_id, pack_flat([args[n] for n in names]),
                                pack_flat([args["m_" + n] for n in names]), pack_flat([args["v_" + n] for n in names]))
        for res, flat in zip((out_g, out_d, out_m, out_v), results):
            res.update(zip(names, unpack_flat(flat, [args[n].shape for n in names])))

    order = ("norm_mix", "w_in", "b_forget", "ssm_lambda_re", "ssm_lambda_im", "ssm_log_dt", "ssm_b_re",
             "ssm_b_im", "ssm_c_re", "ssm_c_im", "ssm_d", "w_glu", "b_glu", "w_branch_a", "w_branch_b", "w_out",
             "norm_mlp", "w_mlp_up", "w_mlp_down", "norm_final")
    return (loss, grad_x, *[out_g[n] for n in order], *[out_d[n] for n in order],
            *[out_m[n] for n in order], *[out_v[n] for n in order])
```

```python
import math

import jax
import jax.numpy as jnp
from jax import lax
from jax.experimental import pallas as pl
from jax.experimental.pallas import tpu as pltpu

F32 = jnp.float32
BF16 = jnp.bfloat16
MESH = pl.DeviceIdType.MESH
ANY = pl.BlockSpec(memory_space=pl.ANY)

ATTN_HEADS = 8
HEAD_DIM = 64
ATTN_WIDTH = ATTN_HEADS * HEAD_DIM
HEAD_PAIRS = ATTN_HEADS // 2
SSM_GROUPS = 32
SSM_GROUP_CH = 16
SSM_STATE = 64
SSM_WIDTH = SSM_GROUPS * SSM_GROUP_CH
LANES = 128
SUBLANES = 8
SSM_CHUNKS = SSM_WIDTH // LANES
CHUNK_GROUPS = SSM_GROUPS // SSM_CHUNKS
CHUNK_STATES = CHUNK_GROUPS * SSM_STATE
CHUNK_LANES = 2 * CHUNK_STATES
STATE_LANES = SSM_CHUNKS * CHUNK_LANES
F_PAD = LANES
RMS_EPS = 1e-6
ADAM_LR = 0.001
ADAM_B1 = 0.9
ADAM_B2 = 0.999
ADAM_EPS = 1e-08
ADAM_WD = 0.01
ADAM_STEP = 10
PACK_COLS = 1024
PACK_ROW_TILE = 256
VMEM_LIMIT = 52 * 1024 * 1024
NEG_BIG = -1e30
LOG2E = math.log2(math.e)
GELU_C = math.sqrt(2.0 / math.pi)
GELU_A = 0.044715

NN = (((1,), (0,)), ((), ()))
NT = (((1,), (1,)), ((), ()))
TN = (((0,), (0,)), ((), ()))


def _pick(n, pref):
    if n <= pref:
        return n
    best = LANES
    for t in range(LANES, pref + 1, LANES):
        if n % t == 0:
            best = t
    assert n % best == 0, (n, pref)
    return best


def _rows(n, pref):
    t = min(n, pref)
    while n % t:
        t //= 2
    assert t % 16 == 0 or t == n, (n, pref)
    return t


def _params(sem):
    return pltpu.CompilerParams(dimension_semantics=sem, vmem_limit_bytes=VMEM_LIMIT)


def _fold8(v):
    r, c = v.shape
    return jnp.sum(v.reshape(r // SUBLANES, SUBLANES, c), axis=0)


def _dot(a, b, dims=None):
    if dims is None:
        return jnp.dot(a, b, preferred_element_type=F32)
    return lax.dot_general(a, b, dims, preferred_element_type=F32)


def _dot_exact(a, b, dims):
    return lax.dot_general(a, b, dims, preferred_element_type=F32, precision=lax.Precision.HIGHEST)


def _sigmoid(v):
    return 1.0 / (1.0 + jnp.exp(-v))


def _rms_scale(x):
    return lax.rsqrt(jnp.mean(x * x, axis=-1, keepdims=True) + RMS_EPS)


def _rms_bwd(x, g, dh):
    r = _rms_scale(x)
    xn = x * r
    dxn = dh * g
    dx = r * (dxn - xn * jnp.mean(dxn * xn, axis=-1, keepdims=True))
    return dx, dh * xn


def norm_matmul(x, g, w, out_dtype, name, col_scale=None):
    t, d = x.shape
    m = w.shape[1]
    tm, tn = _rows(t, 1024), _pick(m, 1024)

    def body(x_ref, g_ref, w_ref, *rest):
        xv = x_ref[...]
        h = (xv * _rms_scale(xv) * g_ref[...]).astype(BF16)
        out = _dot(h, w_ref[...])
        if col_scale is not None:
            out = out * rest[0][...]
        rest[-1][...] = out.astype(rest[-1].dtype)

    scale_spec = [] if col_scale is None else [pl.BlockSpec((1, tn), lambda i, j: (0, j))]
    return pl.pallas_call(
        body, name=name, grid=(t // tm, m // tn),
        in_specs=[pl.BlockSpec((tm, d), lambda i, j: (i, 0)),
                  pl.BlockSpec((1, d), lambda i, j: (0, 0)),
                  pl.BlockSpec((d, tn), lambda i, j: (0, j))] + scale_spec,
        out_specs=pl.BlockSpec((tm, tn), lambda i, j: (i, j)),
        out_shape=jax.ShapeDtypeStruct((t, m), out_dtype),
        compiler_params=_params(("parallel", "arbitrary")),
    )(x, g, w, *([] if col_scale is None else [col_scale]))


def matmul_tn(a, b, name, a_kind="plain", gain=None, shard_axis=None, out_dtype=F32, tm_pref=1024, tn_pref=1536,
              tk_pref=1024):
    t, ma = a.shape
    nb = b.shape[1]
    tm = ma if a_kind == "norm" else _pick(ma, tm_pref)
    tn = _pick(nb, tn_pref)
    tk = _rows(t, tk_pref)
    nk = t // tk
    if shard_axis == 0:
        per = tm * 4 // ma
        assert per >= 1 and (ma // 4) * per == tm, (ma, tm)
        out_shape, out_spec = (4, ma // 4, nb), pl.BlockSpec((per, ma // 4, tn), lambda i, j, k: (i, 0, j))
    elif shard_axis == 1:
        per = tn * 4 // nb
        assert per >= 1 and (nb // 4) * per == tn, (nb, tn)
        out_shape, out_spec = (4, ma, nb // 4), pl.BlockSpec((per, tm, nb // 4), lambda i, j, k: (j, i, 0))
    else:
        out_shape, out_spec = (ma, nb), pl.BlockSpec((tm, tn), lambda i, j, k: (i, j))

    def body(*refs):
        if a_kind == "norm":
            a_ref, g_ref, b_ref, o_ref, acc = refs
        else:
            a_ref, b_ref, o_ref, acc = refs
        k = pl.program_id(2)

        @pl.when(k == 0)
        def _():
            acc[...] = jnp.zeros_like(acc)

        av = a_ref[...]
        if a_kind == "norm":
            av = av * _rms_scale(av) * g_ref[...]
        elif a_kind == "relu2":
            av = jnp.square(jnp.maximum(av.astype(F32), 0.0))
        acc[...] += _dot(av.astype(BF16), b_ref[...].astype(BF16), TN)

        @pl.when(k == nk - 1)
        def _():
            if shard_axis == 0:
                o_ref[...] = acc[...].reshape(o_ref.shape).astype(o_ref.dtype)
            elif shard_axis == 1:
                cs = nb // 4
                for n in range(o_ref.shape[0]):
                    o_ref[n] = acc[:, n * cs:(n + 1) * cs].astype(o_ref.dtype)
            else:
                o_ref[...] = acc[...].astype(o_ref.dtype)

    in_specs = [pl.BlockSpec((tk, tm), lambda i, j, k: (k, i))]
    args = [a]
    if a_kind == "norm":
        in_specs.append(pl.BlockSpec((1, ma), lambda i, j, k: (0, 0)))
        args.append(gain)
    in_specs.append(pl.BlockSpec((tk, tn), lambda i, j, k: (k, j)))
    args.append(b)
    return pl.pallas_call(
        body, name=name, grid=(ma // tm, nb // tn, nk),
        in_specs=in_specs,
        out_specs=out_spec,
        out_shape=jax.ShapeDtypeStruct(out_shape, out_dtype),
        scratch_shapes=[pltpu.VMEM((tm, tn), F32)],
        compiler_params=_params(("parallel", "parallel", "arbitrary")),
    )(*args)


def _tri(n, upper):
    r = lax.broadcasted_iota(jnp.int32, (n, n), 0)
    c = lax.broadcasted_iota(jnp.int32, (n, n), 1)
    return jnp.where((c >= r) if upper else (c <= r), 1.0, 0.0).astype(F32)


def _head_rows():
    r = lax.broadcasted_iota(jnp.int32, (SUBLANES, LANES), 0)
    c = lax.broadcasted_iota(jnp.int32, (SUBLANES, LANES), 1)
    return jnp.where(r == c, 1.0, 0.0).astype(F32)


def forget_cumsum(rest, bf, bsz, seq, f_blk):
    tc = _rows(seq, 512)
    nc = seq // tc

    def body(f_ref, b_ref, col_ref, row_ref, carry):
        c = pl.program_id(1)

        @pl.when(c == 0)
        def _():
            carry[...] = jnp.zeros_like(carry)

        z = f_ref[...] + b_ref[...]
        logf = jnp.minimum(z, 0.0) - jnp.log(1.0 + jnp.exp(-jnp.abs(z)))
        cum = _dot_exact(_tri(tc, False), logf, NN) + carry[0:1, :]
        col_ref[0] = cum
        row_ref[0] = _dot_exact(_head_rows(), cum, NT)
        carry[...] = jnp.broadcast_to(cum[tc - 1:tc, :], carry.shape)

    return pl.pallas_call(
        body, name="forget_cumsum", grid=(bsz, nc),
        in_specs=[pl.BlockSpec((tc, F_PAD), lambda b, c: (b * nc + c, f_blk)),
                  pl.BlockSpec((1, F_PAD), lambda b, c: (0, 0))],
        out_specs=[pl.BlockSpec((1, tc, LANES), lambda b, c: (b, c, 0)),
                   pl.BlockSpec((1, SUBLANES, tc), lambda b, c: (b, 0, c))],
        out_shape=[jax.ShapeDtypeStruct((bsz, seq, LANES), F32),
                   jax.ShapeDtypeStruct((bsz, SUBLANES, seq), F32)],
        scratch_shapes=[pltpu.VMEM((SUBLANES, LANES), F32)],
        compiler_params=_params(("parallel", "arbitrary")),
    )(rest, bf)


def forget_bwd(dcq, dcp, rest, bf, bsz, seq, f_blk):
    tc = _attn_tile(seq)
    nc = seq // tc

    def body(dq_ref, dc_ref, f_ref, b_ref, df_ref, db_ref, carry):
        b = pl.program_id(0)
        c = pl.program_id(1)

        @pl.when(c == 0)
        def _():
            carry[...] = jnp.zeros_like(carry)

        @pl.when((b == 0) & (c == 0))
        def _():
            db_ref[...] = jnp.zeros_like(db_ref)

        row = lax.broadcasted_iota(jnp.int32, (SUBLANES, tc), 0)
        heads = jnp.zeros((SUBLANES, tc), F32)
        dc = jnp.zeros((tc, LANES), F32)
        for p in range(HEAD_PAIRS):
            blk = dq_ref[0, p, 0]
            heads = heads + jnp.where(row == 2 * p, blk[0:1], 0.0) + jnp.where(row == 2 * p + 1, blk[1:2], 0.0)
            dc = dc + dc_ref[0, p]
        dc = dc + jnp.concatenate([heads, jnp.zeros((LANES - SUBLANES, tc), F32)], axis=0).T
        dlogf = _dot_exact(_tri(tc, True), dc, NN) + carry[0:1, :]
        carry[...] = jnp.broadcast_to(dlogf[0:1, :], carry.shape)
        z = f_ref[...] + b_ref[...]
        lane = lax.broadcasted_iota(jnp.int32, z.shape, 1)
        df = jnp.where(lane < ATTN_HEADS, dlogf * _sigmoid(-z), 0.0)
        df_ref[...] = df.astype(df_ref.dtype)
        db_ref[...] += _fold8(df)

    return pl.pallas_call(
        body, name="forget_bwd", grid=(bsz, nc),
        in_specs=[pl.BlockSpec((1, HEAD_PAIRS, 1, SUBLANES, tc), lambda b, c: (b, 0, nc - 1 - c, 0, 0)),
                  pl.BlockSpec((1, HEAD_PAIRS, tc, LANES), lambda b, c: (b, 0, nc - 1 - c, 0)),
                  pl.BlockSpec((tc, F_PAD), lambda b, c: (b * nc + nc - 1 - c, f_blk)),
                  pl.BlockSpec((1, F_PAD), lambda b, c: (0, 0))],
        out_specs=[pl.BlockSpec((tc, F_PAD), lambda b, c: (b * nc + nc - 1 - c, 0)),
                   pl.BlockSpec((SUBLANES, F_PAD), lambda b, c: (0, 0))],
        out_shape=[jax.ShapeDtypeStruct((bsz * seq, F_PAD), BF16),
                   jax.ShapeDtypeStruct((SUBLANES, F_PAD), F32)],
        scratch_shapes=[pltpu.VMEM((SUBLANES, LANES), F32)],
        compiler_params=_params(("arbitrary", "arbitrary")),
    )(dcq, dcp, rest, bf)


def _attn_tile(seq):
    return 512 if seq >= 2048 else 128


def _lane_head(shape, par):
    lane = lax.broadcasted_iota(jnp.int32, shape, len(shape) - 1)
    return (lane >= HEAD_DIM) if par else (lane < HEAD_DIM)


def _pick_lane(block, idx):
    lane = lax.broadcasted_iota(jnp.int32, block.shape, 1)
    return jnp.sum(jnp.where(lane == idx, block, 0.0), axis=1, keepdims=True)


def _pair_rows(lo_lane, hi_lane):
    r = lax.broadcasted_iota(jnp.int32, (SUBLANES, LANES), 0)
    c = lax.broadcasted_iota(jnp.int32, (SUBLANES, LANES), 1)
    if lo_lane is None:
        sel = ((r == 0) & (c < HEAD_DIM)) | ((r == 1) & (c >= HEAD_DIM))
    else:
        sel = ((r == 0) & (c == lo_lane)) | ((r == 1) & (c == hi_lane))
    return jnp.where(sel, 1.0, 0.0).astype(F32)


def _causal(s, transposed):
    r = lax.broadcasted_iota(jnp.int32, s.shape, 0)
    c = lax.broadcasted_iota(jnp.int32, s.shape, 1)
    return jnp.where((c >= r) if transposed else (r >= c), s, NEG_BIG)


def _causal_pairs(n, key_major):
    if key_major:
        pairs = [(i, j) for j in range(n) for i in range(j, n)]
    else:
        pairs = [(i, j) for i in range(n) for j in range(i + 1)]
    return (jnp.asarray([p[0] for p in pairs], jnp.int32), jnp.asarray([p[1] for p in pairs], jnp.int32))


def fox_fwd(qkv, cumrow, bsz, seq, carries=()):
    tq = _attn_tile(seq)
    nq = seq // tq
    kb, vb = ATTN_WIDTH // LANES, 2 * ATTN_WIDTH // LANES
    qi_tab, kj_tab = _causal_pairs(nq, False)
    npairs = int(qi_tab.shape[0])
    sends, landings, n_sems, build = _carry_plan(carries)
    nc = len(sends)

    def body(qi_ref, kj_ref, q_ref, k_ref, v_ref, cr_ref, *rest):
        ins, (o_ref, lr_ref), lands = rest[:nc], rest[nc:nc + 2], rest[nc + 2:2 * nc + 2]
        (m_s, acc_s), sems = rest[2 * nc + 2:2 * nc + 4], rest[2 * nc + 4:]
        hp = pl.program_id(1)
        qi = qi_ref[pl.program_id(2)]
        kj = kj_ref[pl.program_id(2)]
        if nc:
            @pl.when((pl.program_id(0) == 0) & (hp == 0) & (pl.program_id(2) == 0))
            def _():
                for cp in build(ins, lands, sems):
                    cp.start()

        @pl.when(kj == 0)
        def _():
            m_s[...] = jnp.full_like(m_s, NEG_BIG)
            acc_s[...] = jnp.zeros_like(acc_s)

        def step(diag):
            q = q_ref[...]
            k = k_ref[...]
            v = v_ref[...]
            for par in range(2):
                sel = _lane_head(q.shape, par)
                s = _dot(jnp.where(sel, q, 0.0), k, NT) - cr_ref[0, pl.ds(2 * hp + par, 1), :] * LOG2E
                if diag:
                    s = _causal(s, False)
                m_prev = m_s[par]
                m_new = jnp.maximum(m_prev, jnp.max(s, axis=1, keepdims=True))
                p = jnp.exp2(s - m_new).astype(BF16)
                acc_s[par] = jnp.exp2(m_prev - m_new) * acc_s[par] + _dot(p, jnp.where(sel, v, 1.0).astype(BF16))
                m_s[par] = m_new

        @pl.when(kj < qi)
        def _():
            step(False)

        @pl.when(kj == qi)
        def _():
            step(True)
            lo = _lane_head((tq, LANES), 0)
            sums = [pltpu.roll(acc_s[par], HEAD_DIM, 1) for par in range(2)]
            out = jnp.where(lo, acc_s[0] / sums[0], acc_s[1] / sums[1])
            o_ref[...] = out.astype(o_ref.dtype)
            lse = jnp.where(lo, m_s[0] + jnp.log(sums[0]) * LOG2E, m_s[1] + jnp.log(sums[1]) * LOG2E)
            lr_ref[0, 0] = _dot_exact(_pair_rows(0, HEAD_DIM), lse, NT)

        if nc:
            @pl.when((pl.program_id(0) == bsz - 1) & (hp == HEAD_PAIRS - 1) & (pl.program_id(2) == npairs - 1))
            def _():
                for cp in build(ins, lands, sems):
                    cp.wait()

    sem_shapes = [pltpu.SemaphoreType.DMA((n_sems,)), pltpu.SemaphoreType.DMA((n_sems,))] if nc else []
    outs = pl.pallas_call(
        body, name="fox_fwd",
        grid_spec=pltpu.PrefetchScalarGridSpec(
            num_scalar_prefetch=2, grid=(bsz, HEAD_PAIRS, npairs),
            in_specs=[pl.BlockSpec((tq, LANES), lambda b, h, t, qi, kj: (b * nq + qi[t], h)),
                      pl.BlockSpec((tq, LANES), lambda b, h, t, qi, kj: (b * nq + kj[t], kb + h)),
                      pl.BlockSpec((tq, LANES), lambda b, h, t, qi, kj: (b * nq + kj[t], vb + h)),
                      pl.BlockSpec((1, SUBLANES, tq), lambda b, h, t, qi, kj: (b, 0, kj[t]))] + [ANY] * nc,
            out_specs=[pl.BlockSpec((tq, LANES), lambda b, h, t, qi, kj: (b * nq + qi[t], h)),
                       pl.BlockSpec((1, 1, SUBLANES, tq), lambda b, h, t, qi, kj: (b, h, 0, qi[t]))] + [ANY] * nc,
            scratch_shapes=[pltpu.VMEM((2, tq, 1), F32), pltpu.VMEM((2, tq, LANES), F32)] + sem_shapes),
        out_shape=[jax.ShapeDtypeStruct((bsz * seq, ATTN_WIDTH), BF16),
                   jax.ShapeDtypeStruct((bsz, HEAD_PAIRS, SUBLANES, seq), F32)] + landings,
        compiler_params=_params(("arbitrary", "arbitrary", "arbitrary")),
    )(qi_tab, kj_tab, qkv, qkv, qkv, cumrow, *sends)
    return outs[0], outs[1], list(outs[2:])


def fox_bwd_kernel(qkv, do, o, lser, cumcol, bsz, seq, carries=()):
    tk = _attn_tile(seq)
    nk = seq // tk
    scale = HEAD_DIM ** -0.5
    kb, vb = ATTN_WIDTH // LANES, 2 * ATTN_WIDTH // LANES
    qi_tab, kj_tab = _causal_pairs(nk, True)
    npairs = int(qi_tab.shape[0])
    sends, landings, n_sems, build = _carry_plan(carries)
    nc = len(sends)

    def body(qi_ref, kj_ref, q_ref, k_ref, v_ref, do_ref, o_ref, lr_ref, cc_ref, *rest):
        ins, lands = rest[:nc], rest[nc + 5:2 * nc + 5]
        dq_ref, dk_ref, dv_ref, dc_ref, dr_ref = rest[nc:nc + 5]
        (dq_s, dk_s, dv_s, dc_s, dr_s), sems = rest[2 * nc + 5:2 * nc + 10], rest[2 * nc + 10:]
        hp = pl.program_id(1)
        t = pl.program_id(2)
        qi = qi_ref[t]
        kj = kj_ref[t]
        if nc:
            @pl.when((pl.program_id(0) == 0) & (hp == 0) & (t == 0))
            def _():
                for cp in build(ins, lands, sems):
                    cp.start()

        @pl.when(t == 0)
        def _():
            dq_s[...] = jnp.zeros_like(dq_s)
            dr_s[...] = jnp.zeros_like(dr_s)

        @pl.when(qi == kj)
        def _():
            dk_s[...] = jnp.zeros_like(dk_s)
            dv_s[...] = jnp.zeros_like(dv_s)
            dc_s[...] = jnp.zeros_like(dc_s)

        def step(diag):
            q = q_ref[...]
            k = k_ref[...]
            v = v_ref[...]
            dov = do_ref[...]
            prod = dov.astype(F32) * o_ref[...].astype(F32)
            drow = _dot_exact(_pair_rows(None, None), prod, NT)
            lrow = lr_ref[0, 0]
            lane = lax.broadcasted_iota(jnp.int32, (tk, LANES), 1)
            for par in range(2):
                head = 2 * hp + par
                sel = _lane_head(k.shape, par)
                kh = jnp.where(sel, k, 0.0)
                st = _dot(kh, q, NT) - _pick_lane(cc_ref[0], head) * LOG2E
                if diag:
                    st = _causal(st, True)
                pt = jnp.exp2(st - lrow[par:par + 1, :])
                vh = jnp.where(sel, v, 0.0)
                dpt = _dot(vh.astype(BF16), dov, NT)
                dst = pt * (dpt - drow[par:par + 1, :])
                dsb = dst.astype(BF16)
                dv_s[...] += jnp.where(sel, _dot(pt.astype(BF16), dov), 0.0)
                dk_s[...] += jnp.where(sel, _dot(dsb, q), 0.0)
                dq_s[qi] += _dot(dsb, kh, TN)
                dc_s[...] += jnp.where(lane == head, -jnp.sum(dst, axis=1, keepdims=True), 0.0)
                dr_s[qi, par:par + 1, :] += jnp.sum(dst, axis=0, keepdims=True)

        @pl.when(qi > kj)
        def _():
            step(False)

        @pl.when(qi == kj)
        def _():
            step(True)

        @pl.when(qi == nk - 1)
        def _():
            dk_ref[...] = (dk_s[...] * (1.0 / LOG2E)).astype(dk_ref.dtype)
            dv_ref[...] = dv_s[...].astype(dv_ref.dtype)
            dc_ref[0, 0] = dc_s[...]

        @pl.when(t == npairs - 1)
        def _():
            for i in range(nk):
                dq_ref[i * tk:(i + 1) * tk, :] = (dq_s[i] * scale).astype(dq_ref.dtype)
            dr_ref[0, 0] = dr_s[...]

        if nc:
            @pl.when((pl.program_id(0) == bsz - 1) & (hp == HEAD_PAIRS - 1) & (t == npairs - 1))
            def _():
                for cp in build(ins, lands, sems):
                    cp.wait()

    sem_shapes = [pltpu.SemaphoreType.DMA((n_sems,)), pltpu.SemaphoreType.DMA((n_sems,))] if nc else []
    outs = pl.pallas_call(
        body, name="fox_bwd",
        grid_spec=pltpu.PrefetchScalarGridSpec(
            num_scalar_prefetch=2, grid=(bsz, HEAD_PAIRS, npairs),
            in_specs=[pl.BlockSpec((tk, LANES), lambda b, h, t, qi, kj: (b * nk + qi[t], h)),
                      pl.BlockSpec((tk, LANES), lambda b, h, t, qi, kj: (b * nk + kj[t], kb + h)),
                      pl.BlockSpec((tk, LANES), lambda b, h, t, qi, kj: (b * nk + kj[t], vb + h)),
                      pl.BlockSpec((tk, LANES), lambda b, h, t, qi, kj: (b * nk + qi[t], h)),
                      pl.BlockSpec((tk, LANES), lambda b, h, t, qi, kj: (b * nk + qi[t], h)),
                      pl.BlockSpec((1, 1, SUBLANES, tk), lambda b, h, t, qi, kj: (b, h, 0, qi[t])),
                      pl.BlockSpec((1, tk, LANES), lambda b, h, t, qi, kj: (b, kj[t], 0))] + [ANY] * nc,
            out_specs=[pl.BlockSpec((seq, LANES), lambda b, h, t, qi, kj: (b, h)),
                       pl.BlockSpec((tk, LANES), lambda b, h, t, qi, kj: (b * nk + kj[t], h)),
                       pl.BlockSpec((tk, LANES), lambda b, h, t, qi, kj: (b * nk + kj[t], h)),
                       pl.BlockSpec((1, 1, tk, LANES), lambda b, h, t, qi, kj: (b, h, kj[t], 0)),
                       pl.BlockSpec((1, 1, nk, SUBLANES, tk), lambda b, h, t, qi, kj: (b, h, 0, 0, 0))] + [ANY] * nc,
            scratch_shapes=[pltpu.VMEM((nk, tk, LANES), F32), pltpu.VMEM((tk, LANES), F32),
                            pltpu.VMEM((tk, LANES), F32), pltpu.VMEM((tk, LANES), F32),
                            pltpu.VMEM((nk, SUBLANES, tk), F32)] + sem_shapes),
        out_shape=[jax.ShapeDtypeStruct((bsz * seq, ATTN_WIDTH), BF16),
                   jax.ShapeDtypeStruct((bsz * seq, ATTN_WIDTH), BF16),
                   jax.ShapeDtypeStruct((bsz * seq, ATTN_WIDTH), BF16),
                   jax.ShapeDtypeStruct((bsz, HEAD_PAIRS, seq, LANES), F32),
                   jax.ShapeDtypeStruct((bsz, HEAD_PAIRS, nk, SUBLANES, tk), F32)] + landings,
        compiler_params=_params(("arbitrary", "arbitrary", "arbitrary")),
    )(qi_tab, kj_tab, qkv, qkv, qkv, do, o, lser, cumcol, *sends)
    return outs[:5], list(outs[5:])


def fox_bwd(qkv, do, o, lser, cumcol, rest, bf, bsz, seq, f_blk, carries=()):
    (dq, dk, dv, dcp, dcq), lands = fox_bwd_kernel(qkv, do, o, lser, cumcol, bsz, seq, carries)
    df, dbf = forget_bwd(dcq, dcp, rest, bf, bsz, seq, f_blk)
    return dq, dk, dv, df, dbf, lands


SCAN_STEPS = (1, 2, 4)
TAB_FWD = 0
TAB_BWD = 32
TAB_CARRY = 24
TAB_ROWS = 64


def _ssm_tile(seq):
    return 256 if seq >= 1024 else 64


def _scan_block(xr, xi, tab_ref, re, im, cr, ci, reverse):
    base = TAB_BWD if reverse else TAB_FWD
    for n, s in enumerate(SCAN_STEPS):
        ar = tab_ref[base + n * SUBLANES:base + (n + 1) * SUBLANES, re]
        ai = tab_ref[base + n * SUBLANES:base + (n + 1) * SUBLANES, im]
        shift = SUBLANES - s if reverse else s
        sr = pltpu.roll(xr, shift, 0)
        si = pltpu.roll(xi, shift, 0)
        xr, xi = xr + ar * sr - ai * si, xi + ar * si + ai * sr
    pr = tab_ref[base + TAB_CARRY:base + TAB_CARRY + SUBLANES, re]
    pi = tab_ref[base + TAB_CARRY:base + TAB_CARRY + SUBLANES, im]
    xr, xi = xr + pr * cr - pi * ci, xi + pr * ci + pi * cr
    return xr, xi


def ssm_fwd(rest, wb4, wc4, tabs, dskip, bsz, seq, u_blk):
    tt = _ssm_tile(seq)
    nt = seq // tt

    def body(u_ref, wb_ref, wc_ref, tab_ref, d_ref, y_ref, h_ref, carry):
        c = pl.program_id(1)

        @pl.when(c == 0)
        def _():
            carry[...] = jnp.zeros_like(carry)

        u = u_ref[...]
        ub = u.astype(BF16)
        for j in range(SSM_CHUNKS):
            h_ref[:, j * CHUNK_LANES:(j + 1) * CHUNK_LANES] = _dot(ub[:, j * LANES:(j + 1) * LANES], wb_ref[j])
        for j in range(SSM_CHUNKS):
            re = slice(j * CHUNK_LANES, j * CHUNK_LANES + CHUNK_STATES)
            im = slice(j * CHUNK_LANES + CHUNK_STATES, (j + 1) * CHUNK_LANES)

            def blk(bi, car):
                r0 = pl.multiple_of(bi * SUBLANES, SUBLANES)
                xr, xi = _scan_block(h_ref[pl.ds(r0, SUBLANES), re], h_ref[pl.ds(r0, SUBLANES), im],
                                     tab_ref, re, im, car[0], car[1], False)
                h_ref[pl.ds(r0, SUBLANES), re] = xr
                h_ref[pl.ds(r0, SUBLANES), im] = xi
                return xr[SUBLANES - 1:SUBLANES], xi[SUBLANES - 1:SUBLANES]

            cr, ci = lax.fori_loop(0, tt // SUBLANES, blk, (carry[0:1, re], carry[0:1, im]), unroll=2)
            carry[0:1, re] = cr
            carry[0:1, im] = ci
        for j in range(SSM_CHUNKS):
            hj = h_ref[:, j * CHUNK_LANES:(j + 1) * CHUNK_LANES].astype(BF16)
            cols = slice(j * LANES, (j + 1) * LANES)
            y_ref[:, cols] = _dot(hj, wc_ref[j]) + d_ref[:, cols] * u[:, cols]

    return pl.pallas_call(
        body, name="ssm_fwd", grid=(bsz, nt),
        in_specs=[pl.BlockSpec((tt, SSM_WIDTH), lambda b, c: (b * nt + c, u_blk)),
                  pl.BlockSpec((SSM_CHUNKS, LANES, CHUNK_LANES), lambda b, c: (0, 0, 0)),
                  pl.BlockSpec((SSM_CHUNKS, CHUNK_LANES, LANES), lambda b, c: (0, 0, 0)),
                  pl.BlockSpec((TAB_ROWS, STATE_LANES), lambda b, c: (0, 0)),
                  pl.BlockSpec((1, SSM_WIDTH), lambda b, c: (0, 0))],
        out_specs=[pl.BlockSpec((tt, SSM_WIDTH), lambda b, c: (b * nt + c, 0)),
                   pl.BlockSpec((tt, STATE_LANES), lambda b, c: (b * nt + c, 0))],
        out_shape=[jax.ShapeDtypeStruct((bsz * seq, SSM_WIDTH), F32),
                   jax.ShapeDtypeStruct((bsz * seq, STATE_LANES), F32)],
        scratch_shapes=[pltpu.VMEM((SUBLANES, STATE_LANES), F32)],
        compiler_params=_params(("parallel", "arbitrary")),
    )(rest, wb4, wc4, tabs, dskip)


def ssm_bwd(dys, rest, hs, wb4, wc4, tabs, dskip, bsz, seq, u_blk):
    tt = _ssm_tile(seq)
    nt = seq // tt
    nb = tt // SUBLANES

    def body(dy_ref, u_ref, h_ref, hp_ref, wb_ref, wc_ref, tab_ref, d_ref,
             du_ref, ga_ref, gwb_ref, gwc_ref, gd_ref, g_s, carry):
        b = pl.program_id(0)
        c = pl.program_id(1)

        @pl.when(c == 0)
        def _():
            carry[...] = jnp.zeros_like(carry)

        @pl.when((b == 0) & (c == 0))
        def _():
            ga_ref[...] = jnp.zeros_like(ga_ref)
            gwb_ref[...] = jnp.zeros_like(gwb_ref)
            gwc_ref[...] = jnp.zeros_like(gwc_ref)
            gd_ref[...] = jnp.zeros_like(gd_ref)

        dy = dy_ref[...].astype(F32)
        dyb = dy.astype(BF16)
        u = u_ref[...]
        ub = u.astype(BF16)
        first_chunk = c == nt - 1
        for j in range(SSM_CHUNKS):
            g_s[:, j * CHUNK_LANES:(j + 1) * CHUNK_LANES] = _dot(dyb[:, j * LANES:(j + 1) * LANES], wc_ref[j], NT)
        for j in range(SSM_CHUNKS):
            re = slice(j * CHUNK_LANES, j * CHUNK_LANES + CHUNK_STATES)
            im = slice(j * CHUNK_LANES + CHUNK_STATES, (j + 1) * CHUNK_LANES)
            row = lax.broadcasted_iota(jnp.int32, (SUBLANES, CHUNK_STATES), 0)

            def blk(n, car):
                bi = nb - 1 - n
                r0 = pl.multiple_of(bi * SUBLANES, SUBLANES)
                gr, gi = _scan_block(g_s[pl.ds(r0, SUBLANES), re], g_s[pl.ds(r0, SUBLANES), im],
                                     tab_ref, re, im, car[0], car[1], True)
                g_s[pl.ds(r0, SUBLANES), re] = gr
                g_s[pl.ds(r0, SUBLANES), im] = gi
                rp = pl.multiple_of(jnp.maximum(bi - 1, 0) * SUBLANES, SUBLANES)
                inside = bi > 0
                live = jnp.where(jnp.logical_or(inside, jnp.logical_not(first_chunk)), 1.0, 0.0)
                pr = jnp.where(inside, h_ref[pl.ds(rp, SUBLANES), re], hp_ref[:, re])[SUBLANES - 1:SUBLANES] * live
                pi = jnp.where(inside, h_ref[pl.ds(rp, SUBLANES), im], hp_ref[:, im])[SUBLANES - 1:SUBLANES] * live
                hr = jnp.where(row >= 1, pltpu.roll(h_ref[pl.ds(r0, SUBLANES), re], 1, 0), pr)
                hi = jnp.where(row >= 1, pltpu.roll(h_ref[pl.ds(r0, SUBLANES), im], 1, 0), pi)
                return (gr[0:1], gi[0:1], car[2] + gr * hr + gi * hi, car[3] + gi * hr - gr * hi)

            zero = jnp.zeros((SUBLANES, CHUNK_STATES), F32)
            cr, ci, sr, si = lax.fori_loop(0, nb, blk, (carry[0:1, re], carry[0:1, im], zero, zero), unroll=2)
            carry[0:1, re] = cr
            carry[0:1, im] = ci
            ga_ref[:, re] += sr
            ga_ref[:, im] += si
        for j in range(SSM_CHUNKS):
            cols = slice(j * LANES, (j + 1) * LANES)
            lanes = slice(j * CHUNK_LANES, (j + 1) * CHUNK_LANES)
            gj = g_s[:, lanes].astype(BF16)
            du_ref[:, cols] = (_dot(gj, wb_ref[j], NT) + d_ref[:, cols] * dy[:, cols]).astype(du_ref.dtype)
            gwb_ref[j] += _dot(ub[:, cols], gj, TN)
            gwc_ref[j] += _dot(h_ref[:, lanes].astype(BF16), dyb[:, cols], TN)
        gd_ref[...] += _fold8(dy * u)

    def prev_rows(b, c):
        chunk = nt - 1 - c
        return (jnp.maximum((b * nt + chunk) * nb - 1, 0), 0)

    return pl.pallas_call(
        body, name="ssm_bwd", grid=(bsz, nt),
        in_specs=[pl.BlockSpec((tt, SSM_WIDTH), lambda b, c: (b * nt + nt - 1 - c, 0)),
                  pl.BlockSpec((tt, SSM_WIDTH), lambda b, c: (b * nt + nt - 1 - c, u_blk)),
                  pl.BlockSpec((tt, STATE_LANES), lambda b, c: (b * nt + nt - 1 - c, 0)),
                  pl.BlockSpec((SUBLANES, STATE_LANES), prev_rows),
                  pl.BlockSpec((SSM_CHUNKS, LANES, CHUNK_LANES), lambda b, c: (0, 0, 0)),
                  pl.BlockSpec((SSM_CHUNKS, CHUNK_LANES, LANES), lambda b, c: (0, 0, 0)),
                  pl.BlockSpec((TAB_ROWS, STATE_LANES), lambda b, c: (0, 0)),
                  pl.BlockSpec((1, SSM_WIDTH), lambda b, c: (0, 0))],
        out_specs=[pl.BlockSpec((tt, SSM_WIDTH), lambda b, c: (b * nt + nt - 1 - c, 0)),
                   pl.BlockSpec((SUBLANES, STATE_LANES), lambda b, c: (0, 0)),
                   pl.BlockSpec((SSM_CHUNKS, LANES, CHUNK_LANES), lambda b, c: (0, 0, 0)),
                   pl.BlockSpec((SSM_CHUNKS, CHUNK_LANES, LANES), lambda b, c: (0, 0, 0)),
                   pl.BlockSpec((SUBLANES, SSM_WIDTH), lambda b, c: (0, 0))],
        out_shape=[jax.ShapeDtypeStruct((bsz * seq, SSM_WIDTH), BF16),
                   jax.ShapeDtypeStruct((SUBLANES, STATE_LANES), F32),
                   jax.ShapeDtypeStruct((SSM_CHUNKS, LANES, CHUNK_LANES), F32),
                   jax.ShapeDtypeStruct((SSM_CHUNKS, CHUNK_LANES, LANES), F32),
                   jax.ShapeDtypeStruct((SUBLANES, SSM_WIDTH), F32)],
        scratch_shapes=[pltpu.VMEM((tt, STATE_LANES), F32), pltpu.VMEM((SUBLANES, STATE_LANES), F32)],
        compiler_params=_params(("arbitrary", "arbitrary")),
    )(dys, rest, hs, hs, wb4, wc4, tabs, dskip)


def _gelu(v):
    t = jnp.tanh(GELU_C * (v + GELU_A * v * v * v))
    return 0.5 * v * (1.0 + t), t


def mix_fwd(ya, ys, rest, x0, wglu, bglu, wba, wbb, wout):
    t, d = x0.shape
    tm = _rows(t, 512)

    def body(ya_ref, ys_ref, ga_ref, gb_ref, x_ref, wg_ref, bg_ref, wa_ref, wb_ref, wo_ref,
             x1_ref, z_ref, pa_ref, pb_ref, yb_ref, yb2_ref, mx_ref):
        yb, _ = _gelu(ys_ref[...])
        ybb = yb.astype(BF16)
        z = _dot(ybb, wg_ref[...]) + bg_ref[...]
        yb2 = (yb * _sigmoid(z)).astype(BF16)
        pa = _dot(ya_ref[...], wa_ref[...])
        pb = _dot(yb2, wb_ref[...])
        mixed = (_sigmoid(ga_ref[...]) * pa + _sigmoid(gb_ref[...]) * pb).astype(BF16)
        x1_ref[...] = x_ref[...] + _dot(mixed, wo_ref[...])
        z_ref[...] = z.astype(z_ref.dtype)
        pa_ref[...] = pa.astype(pa_ref.dtype)
        pb_ref[...] = pb.astype(pb_ref.dtype)
        yb_ref[...] = ybb
        yb2_ref[...] = yb2
        mx_ref[...] = mixed

    row = lambda w: pl.BlockSpec((tm, w), lambda i: (i, 0))
    full = lambda a: pl.BlockSpec(a.shape, lambda i: (0,) * a.ndim)
    return pl.pallas_call(
        body, name="mix_fwd", grid=(t // tm,),
        in_specs=[row(ATTN_WIDTH), row(SSM_WIDTH),
                  pl.BlockSpec((tm, d), lambda i: (i, 0)), pl.BlockSpec((tm, d), lambda i: (i, 1)),
                  row(d), full(wglu), full(bglu), full(wba), full(wbb), full(wout)],
        out_specs=[row(d), row(SSM_WIDTH), row(d), row(d), row(SSM_WIDTH), row(SSM_WIDTH), row(d)],
        out_shape=[jax.ShapeDtypeStruct((t, d), F32), jax.ShapeDtypeStruct((t, SSM_WIDTH), BF16),
                   jax.ShapeDtypeStruct((t, d), BF16), jax.ShapeDtypeStruct((t, d), BF16),
                   jax.ShapeDtypeStruct((t, SSM_WIDTH), BF16), jax.ShapeDtypeStruct((t, SSM_WIDTH), BF16),
                   jax.ShapeDtypeStruct((t, d), BF16)],
        compiler_params=_params(("parallel",)),
    )(ya, ys, rest, rest, x0, wglu, bglu, wba, wbb, wout)


def mix_bwd(dx1, rest, pa, pb, z, ys, wglu, wba, wbb, wout):
    t, d = dx1.shape
    tm = _rows(t, 512)

    def body(dx_ref, ga_ref, gb_ref, pa_ref, pb_ref, z_ref, ys_ref, wg_ref, wa_ref, wb_ref, wo_ref,
             dya_ref, dys_ref, dg_ref, dpa_ref, dpb_ref, dz_ref, dbg_ref):
        @pl.when(pl.program_id(0) == 0)
        def _():
            dbg_ref[...] = jnp.zeros_like(dbg_ref)

        dmix = _dot(dx_ref[...].astype(BF16), wo_ref[...], NT)
        sa = _sigmoid(ga_ref[...])
        sb = _sigmoid(gb_ref[...])
        dpa = (dmix * sa).astype(BF16)
        dpb = (dmix * sb).astype(BF16)
        dg_ref[:, 0:d] = (dmix * pa_ref[...].astype(F32) * sa * (1.0 - sa)).astype(dg_ref.dtype)
        dg_ref[:, d:2 * d] = (dmix * pb_ref[...].astype(F32) * sb * (1.0 - sb)).astype(dg_ref.dtype)
        dpa_ref[...] = dpa
        dpb_ref[...] = dpb
        dya_ref[...] = _dot(dpa, wa_ref[...], NT).astype(dya_ref.dtype)
        dyb2 = _dot(dpb, wb_ref[...], NT)
        ys = ys_ref[...]
        yb, th = _gelu(ys)
        sg = _sigmoid(z_ref[...].astype(F32))
        dz = dyb2 * yb * sg * (1.0 - sg)
        dzb = dz.astype(BF16)
        dz_ref[...] = dzb
        dbg_ref[...] += _fold8(dz)
        dyb = dyb2 * sg + _dot(dzb, wg_ref[...], NT)
        dgelu = 0.5 * (1.0 + th) + 0.5 * ys * (1.0 - th * th) * GELU_C * (1.0 + 3.0 * GELU_A * ys * ys)
        dys_ref[...] = (dyb * dgelu).astype(dys_ref.dtype)

    row = lambda w: pl.BlockSpec((tm, w), lambda i: (i, 0))
    full = lambda a: pl.BlockSpec(a.shape, lambda i: (0,) * a.ndim)
    return pl.pallas_call(
        body, name="mix_bwd", grid=(t // tm,),
        in_specs=[row(d), pl.BlockSpec((tm, d), lambda i: (i, 0)), pl.BlockSpec((tm, d), lambda i: (i, 1)),
                  row(d), row(d), row(SSM_WIDTH), row(SSM_WIDTH), full(wglu), full(wba), full(wbb), full(wout)],
        out_specs=[row(ATTN_WIDTH), row(SSM_WIDTH), row(2 * d), row(d), row(d), row(SSM_WIDTH),
                   pl.BlockSpec((SUBLANES, SSM_WIDTH), lambda i: (0, 0))],
        out_shape=[jax.ShapeDtypeStruct((t, ATTN_WIDTH), BF16), jax.ShapeDtypeStruct((t, SSM_WIDTH), BF16),
                   jax.ShapeDtypeStruct((t, 2 * d), BF16), jax.ShapeDtypeStruct((t, d), BF16),
                   jax.ShapeDtypeStruct((t, d), BF16), jax.ShapeDtypeStruct((t, SSM_WIDTH), BF16),
                   jax.ShapeDtypeStruct((SUBLANES, SSM_WIDTH), F32)],
        compiler_params=_params(("arbitrary",)),
    )(dx1, rest, rest, pa, pb, z, ys, wglu, wba, wbb, wout)


def mlp_fwd(x1, g, wup, wdown):
    t, d = x1.shape
    ff = wup.shape[1]
    tm, tf = _rows(t, 1024), _pick(ff, 1024)
    nf = ff // tf

    def body(x_ref, g_ref, wu_ref, wd_ref, x2_ref, up_ref, h_s, acc_s):
        f = pl.program_id(1)

        @pl.when(f == 0)
        def _():
            xv = x_ref[...]
            h_s[...] = (xv * _rms_scale(xv) * g_ref[...]).astype(BF16)
            acc_s[...] = jnp.zeros_like(acc_s)

        up = _dot(h_s[...], wu_ref[...])
        up_ref[...] = up.astype(up_ref.dtype)
        act = jnp.square(jnp.maximum(up, 0.0)).astype(BF16)
        acc_s[...] += _dot(act, wd_ref[...])

        @pl.when(f == nf - 1)
        def _():
            x2_ref[...] = x_ref[...] + acc_s[...]

    return pl.pallas_call(
        body, name="mlp_fwd", grid=(t // tm, nf),
        in_specs=[pl.BlockSpec((tm, d), lambda i, f: (i, 0)), pl.BlockSpec((1, d), lambda i, f: (0, 0)),
                  pl.BlockSpec((d, tf), lambda i, f: (0, f)), pl.BlockSpec((tf, d), lambda i, f: (f, 0))],
        out_specs=[pl.BlockSpec((tm, d), lambda i, f: (i, 0)), pl.BlockSpec((tm, tf), lambda i, f: (i, f))],
        out_shape=[jax.ShapeDtypeStruct((t, d), F32), jax.ShapeDtypeStruct((t, ff), BF16)],
        scratch_shapes=[pltpu.VMEM((tm, d), BF16), pltpu.VMEM((tm, d), F32)],
        compiler_params=_params(("parallel", "arbitrary")),
    )(x1, g, wup, wdown)


def mlp_bwd(dx2, up, x1, g, wup, wdown, carries=()):
    t, d = x1.shape
    ff = wup.shape[1]
    tm, tf = _rows(t, 512), _pick(ff, 1024)
    nf = ff // tf
    ni = t // tm
    sends, landings, n_sems, build = _carry_plan(carries)
    nc = len(sends)

    def body(dx_ref, up_ref, x_ref, g_ref, wu_ref, wd_ref, *rest):
        ins, (dup_ref, dx1_ref, dg_ref), lands = rest[:nc], rest[nc:nc + 3], rest[nc + 3:2 * nc + 3]
        (dxb_s, acc_s), sems = rest[2 * nc + 3:2 * nc + 5], rest[2 * nc + 5:]
        i = pl.program_id(0)
        f = pl.program_id(1)
        if nc:
            @pl.when((i == 0) & (f == 0))
            def _():
                for cp in build(ins, lands, sems):
                    cp.start()

        @pl.when((i == 0) & (f == 0))
        def _():
            dg_ref[...] = jnp.zeros_like(dg_ref)

        @pl.when(f == 0)
        def _():
            dxb_s[...] = dx_ref[...].astype(BF16)
            acc_s[...] = jnp.zeros_like(acc_s)

        dact = _dot(dxb_s[...], wd_ref[...], NT)
        dup = (dact * 2.0 * jnp.maximum(up_ref[...].astype(F32), 0.0)).astype(BF16)
        dup_ref[...] = dup
        acc_s[...] += _dot(dup, wu_ref[...], NT)

        @pl.when(f == nf - 1)
        def _():
            dxn, dgain = _rms_bwd(x_ref[...], g_ref[...], acc_s[...])
            dx1_ref[...] = dx_ref[...] + dxn
            dg_ref[...] += _fold8(dgain)

        if nc:
            @pl.when((i == ni - 1) & (f == nf - 1))
            def _():
                for cp in build(ins, lands, sems):
                    cp.wait()

    sem_shapes = [pltpu.SemaphoreType.DMA((n_sems,)), pltpu.SemaphoreType.DMA((n_sems,))] if nc else []
    outs = pl.pallas_call(
        body, name="mlp_bwd", grid=(ni, nf),
        in_specs=[pl.BlockSpec((tm, d), lambda i, f: (i, 0)), pl.BlockSpec((tm, tf), lambda i, f: (i, f)),
                  pl.BlockSpec((tm, d), lambda i, f: (i, 0)), pl.BlockSpec((1, d), lambda i, f: (0, 0)),
                  pl.BlockSpec((d, tf), lambda i, f: (0, f)), pl.BlockSpec((tf, d), lambda i, f: (f, 0))] + [ANY] * nc,
        out_specs=[pl.BlockSpec((tm, tf), lambda i, f: (i, f)), pl.BlockSpec((tm, d), lambda i, f: (i, 0)),
                   pl.BlockSpec((SUBLANES, d), lambda i, f: (0, 0))] + [ANY] * nc,
        out_shape=[jax.ShapeDtypeStruct((t, ff), BF16), jax.ShapeDtypeStruct((t, d), F32),
                   jax.ShapeDtypeStruct((SUBLANES, d), F32)] + landings,
        scratch_shapes=[pltpu.VMEM((tm, d), BF16), pltpu.VMEM((tm, d), F32)] + sem_shapes,
        compiler_params=_params(("arbitrary", "arbitrary")),
    )(dx2, up, x1, g, wup, wdown, *sends)
    return outs[0], outs[1], outs[2], list(outs[3:])


def proj_bwd(dproj, wpad, x0, dx1, g, carries=()):
    t, d = x0.shape
    m = wpad.shape[1]
    tm = _rows(t, 512)
    ni = t // tm
    sends, landings, n_sems, build = _carry_plan(carries)
    nc = len(sends)

    def body(dp_ref, w_ref, x_ref, dx1_ref, g_ref, *rest):
        ins, (dx0_ref, dg_ref), lands, sems = rest[:nc], rest[nc:nc + 2], rest[nc + 2:2 * nc + 2], rest[2 * nc + 2:]

        @pl.when(pl.program_id(0) == 0)
        def _():
            dg_ref[...] = jnp.zeros_like(dg_ref)
            for cp in build(ins, lands, sems):
                cp.start()

        dh = _dot(dp_ref[...], w_ref[...], NT)
        dxn, dgain = _rms_bwd(x_ref[...], g_ref[...], dh)
        dx0_ref[...] = dx1_ref[...] + dxn
        dg_ref[...] += _fold8(dgain)

        if nc:
            @pl.when(pl.program_id(0) == ni - 1)
            def _():
                for cp in build(ins, lands, sems):
                    cp.wait()

    sem_shapes = [pltpu.SemaphoreType.DMA((n_sems,)), pltpu.SemaphoreType.DMA((n_sems,))] if nc else []
    outs = pl.pallas_call(
        body, name="proj_bwd", grid=(ni,),
        in_specs=[pl.BlockSpec((tm, m), lambda i: (i, 0)), pl.BlockSpec((d, m), lambda i: (0, 0)),
                  pl.BlockSpec((tm, d), lambda i: (i, 0)), pl.BlockSpec((tm, d), lambda i: (i, 0)),
                  pl.BlockSpec((1, d), lambda i: (0, 0))] + [ANY] * nc,
        out_specs=[pl.BlockSpec((tm, d), lambda i: (i, 0)), pl.BlockSpec((SUBLANES, d), lambda i: (0, 0))] + [ANY] * nc,
        out_shape=[jax.ShapeDtypeStruct((t, d), F32), jax.ShapeDtypeStruct((SUBLANES, d), F32)] + landings,
        scratch_shapes=sem_shapes,
        compiler_params=_params(("arbitrary",)),
    )(dproj, wpad, x0, dx1, g, *sends)
    return outs[0], outs[1], list(outs[2:])


def final_loss(x, g, target):
    t, d = x.shape
    tm = _rows(t, 512)

    def body(x_ref, g_ref, t_ref, dx_ref, ls_ref, dg_ref):
        @pl.when(pl.program_id(0) == 0)
        def _():
            ls_ref[...] = jnp.zeros_like(ls_ref)
            dg_ref[...] = jnp.zeros_like(dg_ref)

        xv = x_ref[...]
        gv = g_ref[...]
        err = xv * _rms_scale(xv) * gv - t_ref[...]
        ls_ref[...] += _fold8(err * err) * (0.5 / d)
        dxn, dgain = _rms_bwd(xv, gv, err * (1.0 / d))
        dx_ref[...] = dxn
        dg_ref[...] += _fold8(dgain)

    return pl.pallas_call(
        body, name="final_loss", grid=(t // tm,),
        in_specs=[pl.BlockSpec((tm, d), lambda i: (i, 0)), pl.BlockSpec((1, d), lambda i: (0, 0)),
                  pl.BlockSpec((tm, d), lambda i: (i, 0))],
        out_specs=[pl.BlockSpec((tm, d), lambda i: (i, 0)), pl.BlockSpec((SUBLANES, d), lambda i: (0, 0)),
                   pl.BlockSpec((SUBLANES, d), lambda i: (0, 0))],
        out_shape=[jax.ShapeDtypeStruct((t, d), F32), jax.ShapeDtypeStruct((SUBLANES, d), F32),
                   jax.ShapeDtypeStruct((SUBLANES, d), F32)],
        compiler_params=_params(("arbitrary",)),
    )(x, g, target)


def _peers():
    x, y, c = lax.axis_index("x"), lax.axis_index("y"), lax.axis_index("c")
    peers = []
    for m in range(1, 8):
        fx, fy, fc = (m >> 2) & 1, (m >> 1) & 1, m & 1
        peers.append((m, ((1 - x) if fx else x, (1 - y) if fy else y, (1 - c) if fc else c)))
    return x, y, c, peers


def _remote(src, dst, sems, k, peer):
    return pltpu.make_async_remote_copy(src_ref=src, dst_ref=dst, send_sem=sems[0].at[k], recv_sem=sems[1].at[k],
                                        device_id=peer, device_id_type=MESH)


def _gather_copies(w_refs, out_refs, sems, first):
    x, y, c, peers = _peers()
    chip = 2 * x + y
    cps = []
    for t, (w, o) in enumerate(zip(w_refs, out_refs)):
        half = w.shape[0] // 2
        rows = pl.ds(c * half, half)
        for m, peer in peers:
            if m >> 1:
                cps.append(_remote(w.at[rows], o.at[chip, rows], sems, first + 7 * t + m - 1, peer))
    return cps


def _reduce_copies(g_refs, out_refs, sems, first):
    x, y, c, peers = _peers()
    me = 4 * x + 2 * y + c
    cps = []
    for t, (g, o) in enumerate(zip(g_refs, out_refs)):
        for m, (px, py, pc) in peers:
            cps.append(_remote(g.at[2 * px + py], o.at[me], sems, first + 7 * t + m - 1, (px, py, pc)))
    return cps


def _all_copies(s_refs, out_refs, sems, first):
    x, y, c, peers = _peers()
    me = 4 * x + 2 * y + c
    return [_remote(s, o.at[me], sems, first + 7 * t + m - 1, peer)
            for t, (s, o) in enumerate(zip(s_refs, out_refs)) for m, peer in peers]


EXCHANGES = {"gather": (_gather_copies, lambda a: (4,) + a.shape),
             "reduce": (_reduce_copies, lambda a: (8,) + a.shape[1:]),
             "all": (_all_copies, lambda a: (8,) + a.shape)}


def _carry_plan(carries):
    inputs, shapes, spans = [], [], []
    for kind, arrs in carries:
        for a in arrs:
            inputs.append(a)
            shapes.append(jax.ShapeDtypeStruct(EXCHANGES[kind][1](a), a.dtype))
        spans.append((kind, len(arrs)))

    def build(in_refs, out_refs, sems):
        cps, pos = [], 0
        for kind, cnt in spans:
            cps += EXCHANGES[kind][0](in_refs[pos:pos + cnt], out_refs[pos:pos + cnt], sems, 7 * pos)
            pos += cnt
        return cps

    return inputs, shapes, 7 * len(inputs), build


def exchange(carries, name):
    inputs, shapes, n_sems, build = _carry_plan(carries)
    n = len(inputs)

    def body(*refs):
        cps = build(refs[:n], refs[n:2 * n], refs[2 * n:])
        for cp in cps:
            cp.start()
        for cp in cps:
            cp.wait()

    return pl.pallas_call(
        body, name=name, in_specs=[ANY] * n, out_specs=[ANY] * n, out_shape=shapes,
        scratch_shapes=[pltpu.SemaphoreType.DMA((n_sems,)), pltpu.SemaphoreType.DMA((n_sems,))],
    )(*inputs)


def _adamw_math(w, g, m, v):
    m = ADAM_B1 * m + (1.0 - ADAM_B1) * g
    v = ADAM_B2 * v + (1.0 - ADAM_B2) * (g * g)
    m_hat = m / (1.0 - ADAM_B1 ** ADAM_STEP)
    v_hat = v / (1.0 - ADAM_B2 ** ADAM_STEP)
    delta = -ADAM_LR * (m_hat / (jnp.sqrt(v_hat) + ADAM_EPS) + ADAM_WD * w)
    return delta, m, v


def adamw_layer(l, w, m, v, parts, own, device, chip, bufs, name):
    _, r, cdim = w.shape
    tr = _rows(r, PACK_ROW_TILE)

    def body(dev_ref, chip_ref, w_ref, m_ref, v_ref, p_ref, o_ref, *rest):
        g_ref, d_ref, nm_ref, nv_ref = rest[-4:]
        g = None
        for dev in range(8):
            part = jnp.where(dev_ref[0] == dev, o_ref[0], p_ref[dev]).astype(F32)
            g = part if g is None else g + part
        d, nm, nv = _adamw_math(w_ref[0], g, m_ref[0], v_ref[0])
        g_ref[0] = g
        d_ref[0] = d
        nm_ref[0] = nm
        nv_ref[0] = nv

    lay = pl.BlockSpec((1, tr, cdim), lambda i, dev_ref, chip_ref: (l, i, 0))
    in_specs = [lay, lay, lay, pl.BlockSpec((8, tr, cdim), lambda i, dev_ref, chip_ref: (0, i, 0)),
                pl.BlockSpec((1, tr, cdim), lambda i, dev_ref, chip_ref: (chip_ref[0], i, 0))]
    args = [device, chip, w, m, v, parts, own]
    aliases = {}
    if bufs is not None:
        in_specs += [ANY] * 4
        aliases = {len(args) + k: k for k in range(4)}
        args += list(bufs)
    return pl.pallas_call(
        body, name=name,
        grid_spec=pltpu.PrefetchScalarGridSpec(num_scalar_prefetch=2, grid=(r // tr,), in_specs=in_specs,
                                               out_specs=[lay] * 4),
        out_shape=[jax.ShapeDtypeStruct(w.shape, F32)] * 4,
        input_output_aliases=aliases,
        compiler_params=_params(("parallel",)),
    )(*args)


def sum_and_adamw(parts, own, me, w, m, v):
    _, r, cdim = parts.shape
    tr = _rows(r, PACK_ROW_TILE)

    def body(me_ref, p_ref, o_ref, w_ref, m_ref, v_ref, g_ref, d_ref, nm_ref, nv_ref):
        part = lambda k: jnp.where(me_ref[0] == k, o_ref[...], p_ref[k]).astype(F32)
        g = part(0)
        for k in range(1, 8):
            g = g + part(k)
        d, nm, nv = _adamw_math(w_ref[...], g, m_ref[...], v_ref[...])
        g_ref[...] = g
        d_ref[...] = d
        nm_ref[...] = nm
        nv_ref[...] = nv

    spec = pl.BlockSpec((tr, cdim), lambda i, me_ref: (i, 0))
    return pl.pallas_call(
        body, name="sum_and_adamw",
        grid_spec=pltpu.PrefetchScalarGridSpec(
            num_scalar_prefetch=1, grid=(r // tr,),
            in_specs=[pl.BlockSpec((8, tr, cdim), lambda i, me_ref: (0, i, 0)), spec, spec, spec, spec],
            out_specs=[spec] * 4),
        out_shape=[jax.ShapeDtypeStruct((r, cdim), F32)] * 4,
        compiler_params=_params(("parallel",)),
    )(me, parts, own, w, m, v)


SHARDED = ("w_in", "w_glu", "w_branch_a", "w_branch_b", "w_out", "w_mlp_up", "w_mlp_down")
SHARD_AXIS = {"w_in": 2, "w_glu": 1, "w_branch_a": 2, "w_branch_b": 2, "w_out": 1, "w_mlp_up": 2, "w_mlp_down": 1}
SMALL = ("norm_mix", "b_forget", "ssm_lambda_re", "ssm_lambda_im", "ssm_log_dt", "ssm_b_re", "ssm_b_im",
         "ssm_c_re", "ssm_c_im", "ssm_d", "b_glu", "norm_mlp", "norm_final")
SMALL_WIDE = ("ssm_b_re", "ssm_b_im", "ssm_c_re", "ssm_c_im")


def pack_flat(arrs):
    flat = jnp.concatenate([a.reshape(-1).astype(F32) for a in arrs])
    unit = PACK_COLS * PACK_ROW_TILE
    rows = (flat.shape[0] + unit - 1) // unit * PACK_ROW_TILE
    return jnp.pad(flat, (0, rows * PACK_COLS - flat.shape[0])).reshape(rows, PACK_COLS)


def unpack_flat(packed, shapes):
    flat = packed.reshape(-1)
    out, off = [], 0
    for s in shapes:
        n = math.prod(s)
        out.append(flat[off:off + n].reshape(tuple(s)))
        off += n
    return out


def _discretise(lam_re, lam_im, log_dt, b_re, b_im):
    dt = jnp.exp(log_dt)[:, None]
    mag = jnp.exp(lam_re * dt)
    ar = mag * jnp.cos(lam_im * dt)
    ai = mag * jnp.sin(lam_im * dt)
    den = lam_re * lam_re + lam_im * lam_im
    cr = ((ar - 1.0) * lam_re + ai * lam_im) / den
    ci = (ai * lam_re - (ar - 1.0) * lam_im) / den
    bbr = cr[:, :, None] * b_re - ci[:, :, None] * b_im
    bbi = cr[:, :, None] * b_im + ci[:, :, None] * b_re
    return ar, ai, bbr, bbi


def _ssm_inputs(ar, ai, bbr, bbi, c_re, c_im):
    pr, pi = [ar], [ai]
    for _ in range(SUBLANES - 1):
        pr, pi = pr + [pr[-1] * ar - pi[-1] * ai], pi + [pr[-1] * ai + pi[-1] * ar]
    powers = jnp.stack([jnp.stack(pr), jnp.stack(pi)], axis=1)
    powers = powers.reshape(SUBLANES, 2, SSM_CHUNKS, CHUNK_STATES).transpose(0, 2, 1, 3).reshape(SUBLANES, STATE_LANES)
    conj = powers * jnp.tile(jnp.repeat(jnp.asarray([1.0, -1.0], F32), CHUNK_STATES), SSM_CHUNKS)
    idx = jnp.arange(SUBLANES)[:, None]
    tabs = jnp.concatenate(
        [jnp.where(idx >= s, powers[s - 1][None, :], 0.0) for s in SCAN_STEPS] + [powers]
        + [jnp.where(idx < SUBLANES - s, conj[s - 1][None, :], 0.0) for s in SCAN_STEPS] + [conj[::-1]], axis=0)
    eye = jnp.eye(CHUNK_GROUPS, dtype=F32)
    wb = jnp.stack([bbr, bbi]).reshape(2, SSM_CHUNKS, CHUNK_GROUPS, SSM_STATE, SSM_GROUP_CH).transpose(1, 2, 4, 0, 3)
    wb4 = (wb[:, :, :, :, None, :] * eye[None, :, None, None, :, None]).reshape(SSM_CHUNKS, LANES, CHUNK_LANES)
    wc = jnp.stack([c_re, -c_im]).reshape(2, SSM_CHUNKS, CHUNK_GROUPS, SSM_GROUP_CH, SSM_STATE).transpose(1, 0, 4, 2, 3)
    wc4 = (wc[:, :, None, :, :, :] * eye[None, None, :, None, :, None]).reshape(SSM_CHUNKS, CHUNK_LANES, LANES)
    return tabs, wb4.astype(BF16), wc4.astype(BF16)


def _ssm_param_grads(ga8, gwb, gwc):
    eye = jnp.eye(CHUNK_GROUPS, dtype=F32)
    ga = jnp.sum(ga8, axis=0).reshape(SSM_CHUNKS, 2, CHUNK_STATES)
    gar = ga[:, 0].reshape(SSM_GROUPS, SSM_STATE)
    gai = ga[:, 1].reshape(SSM_GROUPS, SSM_STATE)

    def from_wb(g):
        t = g.reshape(SSM_CHUNKS, CHUNK_GROUPS, SSM_GROUP_CH, CHUNK_GROUPS, SSM_STATE)
        return jnp.einsum("jgchp,gh->jgpc", t, eye).reshape(SSM_GROUPS, SSM_STATE, SSM_GROUP_CH)

    def from_wc(g):
        t = g.reshape(SSM_CHUNKS, CHUNK_GROUPS, SSM_STATE, CHUNK_GROUPS, SSM_GROUP_CH)
        return jnp.einsum("jhpgc,gh->jgcp", t, eye).reshape(SSM_GROUPS, SSM_GROUP_CH, SSM_STATE)

    return (gar, gai, from_wb(gwb[:, :, :CHUNK_STATES]), from_wb(gwb[:, :, CHUNK_STATES:]),
            from_wc(gwc[:, :CHUNK_STATES]), -from_wc(gwc[:, CHUNK_STATES:]))


def kernel(x, norm_mix, w_in, b_forget, ssm_lambda_re, ssm_lambda_im, ssm_log_dt, ssm_b_re, ssm_b_im, ssm_c_re, ssm_c_im, ssm_d, w_glu, b_glu, w_branch_a, w_branch_b, w_out, norm_mlp, w_mlp_up, w_mlp_down, norm_final, loss_target, m_norm_mix, m_w_in, m_b_forget, m_ssm_lambda_re, m_ssm_lambda_im, m_ssm_log_dt, m_ssm_b_re, m_ssm_b_im, m_ssm_c_re, m_ssm_c_im, m_ssm_d, m_w_glu, m_b_glu, m_w_branch_a, m_w_branch_b, m_w_out, m_norm_mlp, m_w_mlp_up, m_w_mlp_down, m_norm_final, v_norm_mix, v_w_in, v_b_forget, v_ssm_lambda_re, v_ssm_lambda_im, v_ssm_log_dt, v_ssm_b_re, v_ssm_b_im, v_ssm_c_re, v_ssm_c_im, v_ssm_d, v_w_glu, v_b_glu, v_w_branch_a, v_w_branch_b, v_w_out, v_norm_mlp, v_w_mlp_up, v_w_mlp_down, v_norm_final):
    args = dict(locals())
    bsz, seq, d = x.shape
    nl = norm_mix.shape[0]
    tokens = bsz * seq
    aw, sw = ATTN_WIDTH, SSM_WIDTH
    chip = (2 * lax.axis_index("x") + lax.axis_index("y")).astype(jnp.int32)
    chip_id = chip.reshape(1)
    device_id = (2 * chip + lax.axis_index("c").astype(jnp.int32)).reshape(1)

    own = {n: args[n].astype(BF16) for n in SHARDED}
    late = [n for n in SHARDED if n != "w_in"]
    o_f, o_u, o_ga, o_gb = 3 * aw, 3 * aw + ATTN_HEADS, 3 * aw + ATTN_HEADS + sw, 3 * aw + ATTN_HEADS + sw + d
    u_blk = 2 * d // sw
    f_blk = (2 * d + sw) // F_PAD
    bf_pad = jnp.pad(b_forget, ((0, 0), (0, F_PAD - ATTN_HEADS)))
    q_scale = jnp.concatenate([jnp.full((1, aw), HEAD_DIM ** -0.5 * LOG2E, F32), jnp.ones((1, 2 * aw), F32)], axis=1)

    def assemble(l, names, gathered):
        return {n: jnp.concatenate([jnp.where(chip == k, own[n][l], g[k]) for k in range(4)],
                                   axis=SHARD_AXIS[n] - 1) for n, g in zip(names, gathered)}

    def split_w_in(win):
        w = {"w_qkv": win[:, :o_f],
             "w_rest": jnp.concatenate([win[:, o_ga:o_gb], win[:, o_gb:], win[:, o_u:o_ga],
                                        jnp.pad(win[:, o_f:o_u], ((0, 0), (0, F_PAD - ATTN_HEADS)))], axis=1)}
        w["w_pad"] = jnp.concatenate([w["w_qkv"], w["w_rest"]], axis=1)
        return w

    disc = [jax.vjp(_discretise, ssm_lambda_re[l], ssm_lambda_im[l], ssm_log_dt[l], ssm_b_re[l], ssm_b_im[l])
            for l in range(nl)]

    g_in = exchange([("gather", [own["w_in"][0]])], "gather_first")
    xs = x.reshape(tokens, d)
    saved, weights = [], []
    for l in range(nl):
        w = split_w_in(assemble(l, ["w_in"], g_in)["w_in"])
        g1 = norm_mix[l].reshape(1, d)
        qkv = norm_matmul(xs, g1, w["w_qkv"], BF16, "proj_qkv", col_scale=q_scale)
        rest = norm_matmul(xs, g1, w["w_rest"], F32, "proj_rest")
        cumcol, cumrow = forget_cumsum(rest, bf_pad[l:l + 1], bsz, seq, f_blk)
        carries = [("gather", [own[n][l] for n in late] + ([own["w_in"][l + 1]] if l + 1 < nl else []))]
        ya, lser, lands = fox_fwd(qkv, cumrow, bsz, seq, carries)
        w.update(assemble(l, late, lands))
        g_in = lands[len(late):]
        tabs, wb4, wc4 = _ssm_inputs(*disc[l][0], ssm_c_re[l], ssm_c_im[l])
        dskip = ssm_d[l].reshape(1, sw)
        ys, hs = ssm_fwd(rest, wb4, wc4, tabs, dskip, bsz, seq, u_blk)
        x1, z, pa, pb, yb, yb2, mixed = mix_fwd(ya, ys, rest, xs, w["w_glu"], b_glu[l].reshape(1, sw),
                                                 w["w_branch_a"], w["w_branch_b"], w["w_out"])
        x2, up = mlp_fwd(x1, norm_mlp[l].reshape(1, d), w["w_mlp_up"], w["w_mlp_down"])
        saved.append(dict(x0=xs, qkv=qkv, rest=rest, cumcol=cumcol, ya=ya, lser=lser,
                          tabs=tabs, wb4=wb4, wc4=wc4, dskip=dskip, ys=ys, hs=hs, x1=x1, z=z, pa=pa, pb=pb,
                          yb=yb, yb2=yb2, mixed=mixed, up=up))
        weights.append(w)
        xs = x2
    dx, loss_rows, dgf_rows = final_loss(xs, norm_final.reshape(1, d), loss_target.reshape(tokens, d))
    loss = lax.psum(jnp.sum(loss_rows), ("x", "y", "c"))

    early = [n for n in SHARDED if n != "w_in"]
    big = {n: [None] * nl for n in SHARDED}
    parts = {n: [None] * nl for n in SHARDED}
    small = {n: [None] * nl for n in SMALL if n != "norm_final"}
    for l in reversed(range(nl)):
        s, w = saved[l], weights[l]
        g2 = norm_mlp[l].reshape(1, d)
        dup, dx1, dg2, lands = mlp_bwd(dx, s["up"], s["x1"], g2, w["w_mlp_up"], w["w_mlp_down"],
                                       [("reduce", [big["w_in"][l + 1]])] if l + 1 < nl else [])
        if l + 1 < nl:
            parts["w_in"][l + 1] = lands[0]
        big["w_mlp_down"][l] = matmul_tn(s["up"], dx, "grad_w_mlp_down", a_kind="relu2", shard_axis=0, out_dtype=BF16)
        big["w_mlp_up"][l] = matmul_tn(s["x1"], dup, "grad_w_mlp_up", a_kind="norm", gain=g2, shard_axis=1,
                                       out_dtype=BF16)
        small["norm_mlp"][l] = jnp.sum(dg2, axis=0)
        dya, dys, dgab, dpa, dpb, dz, dbg = mix_bwd(dx1, s["rest"], s["pa"], s["pb"], s["z"], s["ys"],
                                                    w["w_glu"], w["w_branch_a"], w["w_branch_b"], w["w_out"])
        big["w_out"][l] = matmul_tn(s["mixed"], dx1, "grad_w_out", shard_axis=0, out_dtype=BF16)
        big["w_branch_a"][l] = matmul_tn(s["ya"], dpa, "grad_w_branch_a", shard_axis=1, out_dtype=BF16)
        big["w_branch_b"][l] = matmul_tn(s["yb2"], dpb, "grad_w_branch_b", shard_axis=1, out_dtype=BF16)
        big["w_glu"][l] = matmul_tn(s["yb"], dz, "grad_w_glu", shard_axis=0, out_dtype=BF16)
        small["b_glu"][l] = jnp.sum(dbg, axis=0)
        du, ga8, gwb, gwc, gd8 = ssm_bwd(dys, s["rest"], s["hs"], s["wb4"], s["wc4"], s["tabs"], s["dskip"],
                                         bsz, seq, u_blk)
        gar, gai, gbbr, gbbi, gcr, gci = _ssm_param_grads(ga8, gwb, gwc)
        glr, gli, gdt, gbr, gbi = disc[l][1]((gar, gai, gbbr, gbbi))
        small["ssm_lambda_re"][l], small["ssm_lambda_im"][l], small["ssm_log_dt"][l] = glr, gli, gdt
        small["ssm_b_re"][l], small["ssm_b_im"][l] = gbr, gbi
        small["ssm_c_re"][l], small["ssm_c_im"][l] = gcr, gci
        small["ssm_d"][l] = jnp.sum(gd8, axis=0)
        dq, dk, dv, df, dbf, lands = fox_bwd(s["qkv"], dya, s["ya"], s["lser"], s["cumcol"], s["rest"],
                                             bf_pad[l:l + 1], bsz, seq, f_blk, [("reduce", [big[n][l] for n in early])])
        for n, p in zip(early, lands):
            parts[n][l] = p
        small["b_forget"][l] = jnp.sum(dbf, axis=0)[:ATTN_HEADS]
        dproj = jnp.concatenate([dq, dk, dv, dgab, du, df], axis=1)
        g1 = norm_mix[l].reshape(1, d)
        dwp = matmul_tn(s["x0"], dproj, "grad_w_in", a_kind="norm", gain=g1)
        big["w_in"][l] = jnp.stack(jnp.split(jnp.concatenate(
            [dwp[:, :o_f], dwp[:, o_f + 2 * d + sw:o_f + 2 * d + sw + ATTN_HEADS],
             dwp[:, o_f + 2 * d:o_f + 2 * d + sw], dwp[:, o_f:o_f + 2 * d]], axis=1), 4, axis=1)).astype(BF16)
        dx, dg1, lands = proj_bwd(dproj, w["w_pad"], s["x0"], dx1, g1,
                                  [("reduce", [big["w_in"][0]])] if l == 0 else [])
        if l == 0:
            parts["w_in"][0] = lands[0]
        small["norm_mix"][l] = jnp.sum(dg1, axis=0)
    grad_x = dx.reshape(bsz, seq, d)

    small_g = {n: jnp.stack(small[n]) if n != "norm_final" else jnp.sum(dgf_rows, axis=0) for n in SMALL}
    groups = {BF16: SMALL_WIDE, F32: [n for n in SMALL if n not in SMALL_WIDE]}
    packed = {dt: pack_flat([small_g[n] for n in names]).astype(dt) for dt, names in groups.items()}
    small_parts = exchange([("all", [packed[dt] for dt in groups])], "exchange_last")

    out_g, out_d, out_m, out_v = {}, {}, {}, {}
    for n in SHARDED:
        bufs = None
        for l in range(nl):
            bufs = adamw_layer(l, args[n], args["m_" + n], args["v_" + n], parts[n][l], big[n][l], device_id, chip_id,
                               bufs, "adamw_" + n)
        out_g[n], out_d[n], out_m[n], out_v[n] = bufs

    for (dt, names), landed in zip(groups.items(), small_parts):
        results = sum_and_adamw(landed, packed[dt], device_id, pack_flat([args[n] for n in names]),
                                pack_flat([args["m_" + n] for n in names]), pack_flat([args["v_" + n] for n in names]))
        for res, flat in zip((out_g, out_d, out_m, out_v), results):
            res.update(zip(names, unpack_flat(flat, [args[n].shape for n in names])))

    order = ("norm_mix", "w_in", "b_forget", "ssm_lambda_re", "ssm_lambda_im", "ssm_log_dt", "ssm_b_re",
             "ssm_b_im", "ssm_c_re", "ssm_c_im", "ssm_d", "w_glu", "b_glu", "w_branch_a", "w_branch_b", "w_out",
             "norm_mlp", "w_mlp_up", "w_mlp_down", "norm_final")
    return (loss, grad_x, *[out_g[n] for n in order], *[out_d[n] for n in order],
            *[out_m[n] for n in order], *[out_v[n] for n in order])
```

```python
import math

import jax
import jax.numpy as jnp
from jax import lax
from jax.experimental import pallas as pl
from jax.experimental.pallas import tpu as pltpu

F32 = jnp.float32
BF16 = jnp.bfloat16
MESH = pl.DeviceIdType.MESH
ANY = pl.BlockSpec(memory_space=pl.ANY)

ATTN_HEADS = 8
HEAD_DIM = 64
ATTN_WIDTH = ATTN_HEADS * HEAD_DIM
HEAD_PAIRS = ATTN_HEADS // 2
SSM_GROUPS = 32
SSM_GROUP_CH = 16
SSM_STATE = 64
SSM_WIDTH = SSM_GROUPS * SSM_GROUP_CH
LANES = 128
SUBLANES = 8
SSM_CHUNKS = SSM_WIDTH // LANES
CHUNK_GROUPS = SSM_GROUPS // SSM_CHUNKS
CHUNK_STATES = CHUNK_GROUPS * SSM_STATE
CHUNK_LANES = 2 * CHUNK_STATES
STATE_LANES = SSM_CHUNKS * CHUNK_LANES
F_PAD = LANES
RMS_EPS = 1e-6
ADAM_LR = 0.001
ADAM_B1 = 0.9
ADAM_B2 = 0.999
ADAM_EPS = 1e-08
ADAM_WD = 0.01
ADAM_STEP = 10
PACK_COLS = 1024
PACK_ROW_TILE = 256
VMEM_LIMIT = 52 * 1024 * 1024
NEG_BIG = -1e30
GELU_C = math.sqrt(2.0 / math.pi)
GELU_A = 0.044715

NN = (((1,), (0,)), ((), ()))
NT = (((1,), (1,)), ((), ()))
TN = (((0,), (0,)), ((), ()))


def _pick(n, pref):
    if n <= pref:
        return n
    best = LANES
    for t in range(LANES, pref + 1, LANES):
        if n % t == 0:
            best = t
    assert n % best == 0, (n, pref)
    return best


def _rows(n, pref):
    t = min(n, pref)
    while n % t:
        t //= 2
    assert t % 16 == 0 or t == n, (n, pref)
    return t


def _params(sem):
    return pltpu.CompilerParams(dimension_semantics=sem, vmem_limit_bytes=VMEM_LIMIT)


def _fold8(v):
    r, c = v.shape
    return jnp.sum(v.reshape(r // SUBLANES, SUBLANES, c), axis=0)


def _dot(a, b, dims=None):
    if dims is None:
        return jnp.dot(a, b, preferred_element_type=F32)
    return lax.dot_general(a, b, dims, preferred_element_type=F32)


def _dot_exact(a, b, dims):
    return lax.dot_general(a, b, dims, preferred_element_type=F32, precision=lax.Precision.HIGHEST)


def _sigmoid(v):
    return 1.0 / (1.0 + jnp.exp(-v))


def _rms_scale(x):
    return lax.rsqrt(jnp.mean(x * x, axis=-1, keepdims=True) + RMS_EPS)


def _rms_bwd(x, g, dh):
    r = _rms_scale(x)
    xn = x * r
    dxn = dh * g
    dx = r * (dxn - xn * jnp.mean(dxn * xn, axis=-1, keepdims=True))
    return dx, dh * xn


def norm_matmul(x, g, w, out_dtype, name):
    t, d = x.shape
    m = w.shape[1]
    tm, tn = _rows(t, 1024), _pick(m, 1024)

    def body(x_ref, g_ref, w_ref, o_ref):
        xv = x_ref[...]
        h = (xv * _rms_scale(xv) * g_ref[...]).astype(BF16)
        o_ref[...] = _dot(h, w_ref[...]).astype(o_ref.dtype)

    return pl.pallas_call(
        body, name=name, grid=(t // tm, m // tn),
        in_specs=[pl.BlockSpec((tm, d), lambda i, j: (i, 0)),
                  pl.BlockSpec((1, d), lambda i, j: (0, 0)),
                  pl.BlockSpec((d, tn), lambda i, j: (0, j))],
        out_specs=pl.BlockSpec((tm, tn), lambda i, j: (i, j)),
        out_shape=jax.ShapeDtypeStruct((t, m), out_dtype),
        compiler_params=_params(("parallel", "arbitrary")),
    )(x, g, w)


def matmul_tn(a, b, name, a_kind="plain", gain=None, shard_axis=None, out_dtype=F32, tm_pref=1024, tn_pref=1536,
              tk_pref=1024):
    t, ma = a.shape
    nb = b.shape[1]
    tm = ma if a_kind == "norm" else _pick(ma, tm_pref)
    tn = _pick(nb, tn_pref)
    tk = _rows(t, tk_pref)
    nk = t // tk
    if shard_axis == 0:
        per = tm * 4 // ma
        assert per >= 1 and (ma // 4) * per == tm, (ma, tm)
        out_shape, out_spec = (4, ma // 4, nb), pl.BlockSpec((per, ma // 4, tn), lambda i, j, k: (i, 0, j))
    elif shard_axis == 1:
        per = tn * 4 // nb
        assert per >= 1 and (nb // 4) * per == tn, (nb, tn)
        out_shape, out_spec = (4, ma, nb // 4), pl.BlockSpec((per, tm, nb // 4), lambda i, j, k: (j, i, 0))
    else:
        out_shape, out_spec = (ma, nb), pl.BlockSpec((tm, tn), lambda i, j, k: (i, j))

    def body(*refs):
        if a_kind == "norm":
            a_ref, g_ref, b_ref, o_ref, acc = refs
        else:
            a_ref, b_ref, o_ref, acc = refs
        k = pl.program_id(2)

        @pl.when(k == 0)
        def _():
            acc[...] = jnp.zeros_like(acc)

        av = a_ref[...]
        if a_kind == "norm":
            av = av * _rms_scale(av) * g_ref[...]
        elif a_kind == "relu2":
            av = jnp.square(jnp.maximum(av.astype(F32), 0.0))
        acc[...] += _dot(av.astype(BF16), b_ref[...].astype(BF16), TN)

        @pl.when(k == nk - 1)
        def _():
            if shard_axis == 0:
                o_ref[...] = acc[...].reshape(o_ref.shape).astype(o_ref.dtype)
            elif shard_axis == 1:
                cs = nb // 4
                for n in range(o_ref.shape[0]):
                    o_ref[n] = acc[:, n * cs:(n + 1) * cs].astype(o_ref.dtype)
            else:
                o_ref[...] = acc[...].astype(o_ref.dtype)

    in_specs = [pl.BlockSpec((tk, tm), lambda i, j, k: (k, i))]
    args = [a]
    if a_kind == "norm":
        in_specs.append(pl.BlockSpec((1, ma), lambda i, j, k: (0, 0)))
        args.append(gain)
    in_specs.append(pl.BlockSpec((tk, tn), lambda i, j, k: (k, j)))
    args.append(b)
    return pl.pallas_call(
        body, name=name, grid=(ma // tm, nb // tn, nk),
        in_specs=in_specs,
        out_specs=out_spec,
        out_shape=jax.ShapeDtypeStruct(out_shape, out_dtype),
        scratch_shapes=[pltpu.VMEM((tm, tn), F32)],
        compiler_params=_params(("parallel", "parallel", "arbitrary")),
    )(*args)


def _tri(n, upper):
    r = lax.broadcasted_iota(jnp.int32, (n, n), 0)
    c = lax.broadcasted_iota(jnp.int32, (n, n), 1)
    return jnp.where((c >= r) if upper else (c <= r), 1.0, 0.0).astype(F32)


def _head_rows():
    r = lax.broadcasted_iota(jnp.int32, (SUBLANES, LANES), 0)
    c = lax.broadcasted_iota(jnp.int32, (SUBLANES, LANES), 1)
    return jnp.where(r == c, 1.0, 0.0).astype(F32)


def forget_cumsum(rest, bf, bsz, seq, f_blk):
    tc = _rows(seq, 512)
    nc = seq // tc

    def body(f_ref, b_ref, col_ref, row_ref, carry):
        c = pl.program_id(1)

        @pl.when(c == 0)
        def _():
            carry[...] = jnp.zeros_like(carry)

        z = f_ref[...] + b_ref[...]
        logf = jnp.minimum(z, 0.0) - jnp.log(1.0 + jnp.exp(-jnp.abs(z)))
        cum = _dot_exact(_tri(tc, False), logf, NN) + carry[0:1, :]
        col_ref[0] = cum
        row_ref[0] = _dot_exact(_head_rows(), cum, NT)
        carry[...] = jnp.broadcast_to(cum[tc - 1:tc, :], carry.shape)

    return pl.pallas_call(
        body, name="forget_cumsum", grid=(bsz, nc),
        in_specs=[pl.BlockSpec((tc, F_PAD), lambda b, c: (b * nc + c, f_blk)),
                  pl.BlockSpec((1, F_PAD), lambda b, c: (0, 0))],
        out_specs=[pl.BlockSpec((1, tc, LANES), lambda b, c: (b, c, 0)),
                   pl.BlockSpec((1, SUBLANES, tc), lambda b, c: (b, 0, c))],
        out_shape=[jax.ShapeDtypeStruct((bsz, seq, LANES), F32),
                   jax.ShapeDtypeStruct((bsz, SUBLANES, seq), F32)],
        scratch_shapes=[pltpu.VMEM((SUBLANES, LANES), F32)],
        compiler_params=_params(("parallel", "arbitrary")),
    )(rest, bf)


def forget_bwd(dcq, dcp, rest, bf, bsz, seq, f_blk):
    tc = _attn_tile(seq)
    nc = seq // tc

    def body(dq_ref, dc_ref, f_ref, b_ref, df_ref, db_ref, carry):
        b = pl.program_id(0)
        c = pl.program_id(1)

        @pl.when(c == 0)
        def _():
            carry[...] = jnp.zeros_like(carry)

        @pl.when((b == 0) & (c == 0))
        def _():
            db_ref[...] = jnp.zeros_like(db_ref)

        row = lax.broadcasted_iota(jnp.int32, (SUBLANES, tc), 0)
        heads = jnp.zeros((SUBLANES, tc), F32)
        dc = jnp.zeros((tc, LANES), F32)
        for p in range(HEAD_PAIRS):
            blk = dq_ref[0, p, 0]
            heads = heads + jnp.where(row == 2 * p, blk[0:1], 0.0) + jnp.where(row == 2 * p + 1, blk[1:2], 0.0)
            dc = dc + dc_ref[0, p]
        dc = dc + jnp.concatenate([heads, jnp.zeros((LANES - SUBLANES, tc), F32)], axis=0).T
        dlogf = _dot_exact(_tri(tc, True), dc, NN) + carry[0:1, :]
        carry[...] = jnp.broadcast_to(dlogf[0:1, :], carry.shape)
        z = f_ref[...] + b_ref[...]
        lane = lax.broadcasted_iota(jnp.int32, z.shape, 1)
        df = jnp.where(lane < ATTN_HEADS, dlogf * _sigmoid(-z), 0.0)
        df_ref[...] = df.astype(df_ref.dtype)
        db_ref[...] += _fold8(df)

    return pl.pallas_call(
        body, name="forget_bwd", grid=(bsz, nc),
        in_specs=[pl.BlockSpec((1, HEAD_PAIRS, 1, SUBLANES, tc), lambda b, c: (b, 0, nc - 1 - c, 0, 0)),
                  pl.BlockSpec((1, HEAD_PAIRS, tc, LANES), lambda b, c: (b, 0, nc - 1 - c, 0)),
                  pl.BlockSpec((tc, F_PAD), lambda b, c: (b * nc + nc - 1 - c, f_blk)),
                  pl.BlockSpec((1, F_PAD), lambda b, c: (0, 0))],
        out_specs=[pl.BlockSpec((tc, F_PAD), lambda b, c: (b * nc + nc - 1 - c, 0)),
                   pl.BlockSpec((SUBLANES, F_PAD), lambda b, c: (0, 0))],
        out_shape=[jax.ShapeDtypeStruct((bsz * seq, F_PAD), BF16),
                   jax.ShapeDtypeStruct((SUBLANES, F_PAD), F32)],
        scratch_shapes=[pltpu.VMEM((SUBLANES, LANES), F32)],
        compiler_params=_params(("arbitrary", "arbitrary")),
    )(dcq, dcp, rest, bf)


def _attn_tile(seq):
    return 512 if seq >= 2048 else 128


def _lane_head(shape, par):
    lane = lax.broadcasted_iota(jnp.int32, shape, len(shape) - 1)
    return (lane >= HEAD_DIM) if par else (lane < HEAD_DIM)


def _pick_lane(block, idx):
    lane = lax.broadcasted_iota(jnp.int32, block.shape, 1)
    return jnp.sum(jnp.where(lane == idx, block, 0.0), axis=1, keepdims=True)


def _pair_rows(lo_lane, hi_lane):
    r = lax.broadcasted_iota(jnp.int32, (SUBLANES, LANES), 0)
    c = lax.broadcasted_iota(jnp.int32, (SUBLANES, LANES), 1)
    if lo_lane is None:
        sel = ((r == 0) & (c < HEAD_DIM)) | ((r == 1) & (c >= HEAD_DIM))
    else:
        sel = ((r == 0) & (c == lo_lane)) | ((r == 1) & (c == hi_lane))
    return jnp.where(sel, 1.0, 0.0).astype(F32)


def _causal(s, transposed):
    r = lax.broadcasted_iota(jnp.int32, s.shape, 0)
    c = lax.broadcasted_iota(jnp.int32, s.shape, 1)
    return jnp.where((c >= r) if transposed else (r >= c), s, NEG_BIG)


def _causal_pairs(n, key_major):
    if key_major:
        pairs = [(i, j) for j in range(n) for i in range(j, n)]
    else:
        pairs = [(i, j) for i in range(n) for j in range(i + 1)]
    return (jnp.asarray([p[0] for p in pairs], jnp.int32), jnp.asarray([p[1] for p in pairs], jnp.int32))


def fox_fwd(qkv, cumrow, bsz, seq, carries=()):
    tq = _attn_tile(seq)
    nq = seq // tq
    scale = HEAD_DIM ** -0.5
    kb, vb = ATTN_WIDTH // LANES, 2 * ATTN_WIDTH // LANES
    qi_tab, kj_tab = _causal_pairs(nq, False)
    npairs = int(qi_tab.shape[0])
    sends, landings, n_sems, build = _carry_plan(carries)
    nc = len(sends)

    def body(qi_ref, kj_ref, q_ref, k_ref, v_ref, cr_ref, *rest):
        ins, (o_ref, lr_ref), lands = rest[:nc], rest[nc:nc + 2], rest[nc + 2:2 * nc + 2]
        (m_s, acc_s), sems = rest[2 * nc + 2:2 * nc + 4], rest[2 * nc + 4:]
        hp = pl.program_id(1)
        qi = qi_ref[pl.program_id(2)]
        kj = kj_ref[pl.program_id(2)]
        if nc:
            @pl.when((pl.program_id(0) == 0) & (hp == 0) & (pl.program_id(2) == 0))
            def _():
                for cp in build(ins, lands, sems):
                    cp.start()

        @pl.when(kj == 0)
        def _():
            m_s[...] = jnp.full_like(m_s, NEG_BIG)
            acc_s[...] = jnp.zeros_like(acc_s)

        def step(diag):
            q = q_ref[...]
            k = k_ref[...]
            v = v_ref[...]
            for par in range(2):
                sel = _lane_head(q.shape, par)
                qh = jnp.where(sel, q, 0.0) * scale
                s = _dot(qh.astype(BF16), k, NT) - cr_ref[0, pl.ds(2 * hp + par, 1), :]
                if diag:
                    s = _causal(s, False)
                m_prev = m_s[par]
                m_new = jnp.maximum(m_prev, jnp.max(s, axis=1, keepdims=True))
                p = jnp.exp(s - m_new).astype(BF16)
                acc_s[par] = jnp.exp(m_prev - m_new) * acc_s[par] + _dot(p, jnp.where(sel, v, 1.0).astype(BF16))
                m_s[par] = m_new

        @pl.when(kj < qi)
        def _():
            step(False)

        @pl.when(kj == qi)
        def _():
            step(True)
            lo = _lane_head((tq, LANES), 0)
            sums = [pltpu.roll(acc_s[par], HEAD_DIM, 1) for par in range(2)]
            out = jnp.where(lo, acc_s[0] / sums[0], acc_s[1] / sums[1])
            o_ref[...] = out.astype(o_ref.dtype)
            lse = jnp.where(lo, m_s[0] + jnp.log(sums[0]), m_s[1] + jnp.log(sums[1]))
            lr_ref[0, 0] = _dot_exact(_pair_rows(0, HEAD_DIM), lse, NT)

        if nc:
            @pl.when((pl.program_id(0) == bsz - 1) & (hp == HEAD_PAIRS - 1) & (pl.program_id(2) == npairs - 1))
            def _():
                for cp in build(ins, lands, sems):
                    cp.wait()

    sem_shapes = [pltpu.SemaphoreType.DMA((n_sems,)), pltpu.SemaphoreType.DMA((n_sems,))] if nc else []
    outs = pl.pallas_call(
        body, name="fox_fwd",
        grid_spec=pltpu.PrefetchScalarGridSpec(
            num_scalar_prefetch=2, grid=(bsz, HEAD_PAIRS, npairs),
            in_specs=[pl.BlockSpec((tq, LANES), lambda b, h, t, qi, kj: (b * nq + qi[t], h)),
                      pl.BlockSpec((tq, LANES), lambda b, h, t, qi, kj: (b * nq + kj[t], kb + h)),
                      pl.BlockSpec((tq, LANES), lambda b, h, t, qi, kj: (b * nq + kj[t], vb + h)),
                      pl.BlockSpec((1, SUBLANES, tq), lambda b, h, t, qi, kj: (b, 0, kj[t]))] + [ANY] * nc,
            out_specs=[pl.BlockSpec((tq, LANES), lambda b, h, t, qi, kj: (b * nq + qi[t], h)),
                       pl.BlockSpec((1, 1, SUBLANES, tq), lambda b, h, t, qi, kj: (b, h, 0, qi[t]))] + [ANY] * nc,
            scratch_shapes=[pltpu.VMEM((2, tq, 1), F32), pltpu.VMEM((2, tq, LANES), F32)] + sem_shapes),
        out_shape=[jax.ShapeDtypeStruct((bsz * seq, ATTN_WIDTH), BF16),
                   jax.ShapeDtypeStruct((bsz, HEAD_PAIRS, SUBLANES, seq), F32)] + landings,
        compiler_params=_params(("arbitrary", "arbitrary", "arbitrary")),
    )(qi_tab, kj_tab, qkv, qkv, qkv, cumrow, *sends)
    return outs[0], outs[1], list(outs[2:])


def fox_bwd_kernel(qkv, do, o, lser, cumcol, bsz, seq, carries=()):
    tk = _attn_tile(seq)
    nk = seq // tk
    scale = HEAD_DIM ** -0.5
    kb, vb = ATTN_WIDTH // LANES, 2 * ATTN_WIDTH // LANES
    qi_tab, kj_tab = _causal_pairs(nk, True)
    npairs = int(qi_tab.shape[0])
    sends, landings, n_sems, build = _carry_plan(carries)
    nc = len(sends)

    def body(qi_ref, kj_ref, q_ref, k_ref, v_ref, do_ref, o_ref, lr_ref, cc_ref, *rest):
        ins, lands = rest[:nc], rest[nc + 5:2 * nc + 5]
        dq_ref, dk_ref, dv_ref, dc_ref, dr_ref = rest[nc:nc + 5]
        (dq_s, dk_s, dv_s, dc_s, dr_s), sems = rest[2 * nc + 5:2 * nc + 10], rest[2 * nc + 10:]
        hp = pl.program_id(1)
        t = pl.program_id(2)
        qi = qi_ref[t]
        kj = kj_ref[t]
        if nc:
            @pl.when((pl.program_id(0) == 0) & (hp == 0) & (t == 0))
            def _():
                for cp in build(ins, lands, sems):
                    cp.start()

        @pl.when(t == 0)
        def _():
            dq_s[...] = jnp.zeros_like(dq_s)
            dr_s[...] = jnp.zeros_like(dr_s)

        @pl.when(qi == kj)
        def _():
            dk_s[...] = jnp.zeros_like(dk_s)
            dv_s[...] = jnp.zeros_like(dv_s)
            dc_s[...] = jnp.zeros_like(dc_s)

        def step(diag):
            q = q_ref[...]
            k = k_ref[...]
            v = v_ref[...]
            dov = do_ref[...]
            prod = dov.astype(F32) * o_ref[...].astype(F32)
            drow = _dot_exact(_pair_rows(None, None), prod, NT)
            lrow = lr_ref[0, 0]
            lane = lax.broadcasted_iota(jnp.int32, (tk, LANES), 1)
            for par in range(2):
                head = 2 * hp + par
                sel = _lane_head(k.shape, par)
                kh = (jnp.where(sel, k, 0.0) * scale).astype(BF16)
                st = _dot(kh, q, NT) - _pick_lane(cc_ref[0], head)
                if diag:
                    st = _causal(st, True)
                pt = jnp.exp(st - lrow[par:par + 1, :])
                vh = jnp.where(sel, v, 0.0)
                dpt = _dot(vh.astype(BF16), dov, NT)
                dst = pt * (dpt - drow[par:par + 1, :])
                dsb = dst.astype(BF16)
                dv_s[...] += jnp.where(sel, _dot(pt.astype(BF16), dov), 0.0)
                dk_s[...] += jnp.where(sel, _dot(dsb, q), 0.0)
                dq_s[qi] += _dot(dsb, kh, TN)
                dc_s[...] += jnp.where(lane == head, -jnp.sum(dst, axis=1, keepdims=True), 0.0)
                dr_s[qi, par:par + 1, :] += jnp.sum(dst, axis=0, keepdims=True)

        @pl.when(qi > kj)
        def _():
            step(False)

        @pl.when(qi == kj)
        def _():
            step(True)

        @pl.when(qi == nk - 1)
        def _():
            dk_ref[...] = (dk_s[...] * scale).astype(dk_ref.dtype)
            dv_ref[...] = dv_s[...].astype(dv_ref.dtype)
            dc_ref[0, 0] = dc_s[...]

        @pl.when(t == npairs - 1)
        def _():
            for i in range(nk):
                dq_ref[i * tk:(i + 1) * tk, :] = dq_s[i].astype(dq_ref.dtype)
            dr_ref[0, 0] = dr_s[...]

        if nc:
            @pl.when((pl.program_id(0) == bsz - 1) & (hp == HEAD_PAIRS - 1) & (t == npairs - 1))
            def _():
                for cp in build(ins, lands, sems):
                    cp.wait()

    sem_shapes = [pltpu.SemaphoreType.DMA((n_sems,)), pltpu.SemaphoreType.DMA((n_sems,))] if nc else []
    outs = pl.pallas_call(
        body, name="fox_bwd",
        grid_spec=pltpu.PrefetchScalarGridSpec(
            num_scalar_prefetch=2, grid=(bsz, HEAD_PAIRS, npairs),
            in_specs=[pl.BlockSpec((tk, LANES), lambda b, h, t, qi, kj: (b * nk + qi[t], h)),
                      pl.BlockSpec((tk, LANES), lambda b, h, t, qi, kj: (b * nk + kj[t], kb + h)),
                      pl.BlockSpec((tk, LANES), lambda b, h, t, qi, kj: (b * nk + kj[t], vb + h)),
                      pl.BlockSpec((tk, LANES), lambda b, h, t, qi, kj: (b * nk + qi[t], h)),
                      pl.BlockSpec((tk, LANES), lambda b, h, t, qi, kj: (b * nk + qi[t], h)),
                      pl.BlockSpec((1, 1, SUBLANES, tk), lambda b, h, t, qi, kj: (b, h, 0, qi[t])),
                      pl.BlockSpec((1, tk, LANES), lambda b, h, t, qi, kj: (b, kj[t], 0))] + [ANY] * nc,
            out_specs=[pl.BlockSpec((seq, LANES), lambda b, h, t, qi, kj: (b, h)),
                       pl.BlockSpec((tk, LANES), lambda b, h, t, qi, kj: (b * nk + kj[t], h)),
                       pl.BlockSpec((tk, LANES), lambda b, h, t, qi, kj: (b * nk + kj[t], h)),
                       pl.BlockSpec((1, 1, tk, LANES), lambda b, h, t, qi, kj: (b, h, kj[t], 0)),
                       pl.BlockSpec((1, 1, nk, SUBLANES, tk), lambda b, h, t, qi, kj: (b, h, 0, 0, 0))] + [ANY] * nc,
            scratch_shapes=[pltpu.VMEM((nk, tk, LANES), F32), pltpu.VMEM((tk, LANES), F32),
                            pltpu.VMEM((tk, LANES), F32), pltpu.VMEM((tk, LANES), F32),
                            pltpu.VMEM((nk, SUBLANES, tk), F32)] + sem_shapes),
        out_shape=[jax.ShapeDtypeStruct((bsz * seq, ATTN_WIDTH), BF16),
                   jax.ShapeDtypeStruct((bsz * seq, ATTN_WIDTH), BF16),
                   jax.ShapeDtypeStruct((bsz * seq, ATTN_WIDTH), BF16),
                   jax.ShapeDtypeStruct((bsz, HEAD_PAIRS, seq, LANES), F32),
                   jax.ShapeDtypeStruct((bsz, HEAD_PAIRS, nk, SUBLANES, tk), F32)] + landings,
        compiler_params=_params(("arbitrary", "arbitrary", "arbitrary")),
    )(qi_tab, kj_tab, qkv, qkv, qkv, do, o, lser, cumcol, *sends)
    return outs[:5], list(outs[5:])


def fox_bwd(qkv, do, o, lser, cumcol, rest, bf, bsz, seq, f_blk, carries=()):
    (dq, dk, dv, dcp, dcq), lands = fox_bwd_kernel(qkv, do, o, lser, cumcol, bsz, seq, carries)
    df, dbf = forget_bwd(dcq, dcp, rest, bf, bsz, seq, f_blk)
    return dq, dk, dv, df, dbf, lands


SCAN_STEPS = (1, 2, 4)
TAB_FWD = 0
TAB_BWD = 32
TAB_CARRY = 24
TAB_ROWS = 64


def _ssm_tile(seq):
    return 256 if seq >= 1024 else 64


def _scan_block(xr, xi, tab_ref, re, im, cr, ci, reverse):
    base = TAB_BWD if reverse else TAB_FWD
    for n, s in enumerate(SCAN_STEPS):
        ar = tab_ref[base + n * SUBLANES:base + (n + 1) * SUBLANES, re]
        ai = tab_ref[base + n * SUBLANES:base + (n + 1) * SUBLANES, im]
        shift = SUBLANES - s if reverse else s
        sr = pltpu.roll(xr, shift, 0)
        si = pltpu.roll(xi, shift, 0)
        xr, xi = xr + ar * sr - ai * si, xi + ar * si + ai * sr
    pr = tab_ref[base + TAB_CARRY:base + TAB_CARRY + SUBLANES, re]
    pi = tab_ref[base + TAB_CARRY:base + TAB_CARRY + SUBLANES, im]
    xr, xi = xr + pr * cr - pi * ci, xi + pr * ci + pi * cr
    return xr, xi


def ssm_fwd(rest, wb4, wc4, tabs, dskip, bsz, seq, u_blk):
    tt = _ssm_tile(seq)
    nt = seq // tt

    def body(u_ref, wb_ref, wc_ref, tab_ref, d_ref, y_ref, h_ref, carry):
        c = pl.program_id(1)

        @pl.when(c == 0)
        def _():
            carry[...] = jnp.zeros_like(carry)

        u = u_ref[...]
        ub = u.astype(BF16)
        for j in range(SSM_CHUNKS):
            h_ref[:, j * CHUNK_LANES:(j + 1) * CHUNK_LANES] = _dot(ub[:, j * LANES:(j + 1) * LANES], wb_ref[j])
        for j in range(SSM_CHUNKS):
            re = slice(j * CHUNK_LANES, j * CHUNK_LANES + CHUNK_STATES)
            im = slice(j * CHUNK_LANES + CHUNK_STATES, (j + 1) * CHUNK_LANES)

            def blk(bi, car):
                r0 = pl.multiple_of(bi * SUBLANES, SUBLANES)
                xr, xi = _scan_block(h_ref[pl.ds(r0, SUBLANES), re], h_ref[pl.ds(r0, SUBLANES), im],
                                     tab_ref, re, im, car[0], car[1], False)
                h_ref[pl.ds(r0, SUBLANES), re] = xr
                h_ref[pl.ds(r0, SUBLANES), im] = xi
                return xr[SUBLANES - 1:SUBLANES], xi[SUBLANES - 1:SUBLANES]

            cr, ci = lax.fori_loop(0, tt // SUBLANES, blk, (carry[0:1, re], carry[0:1, im]), unroll=2)
            carry[0:1, re] = cr
            carry[0:1, im] = ci
        for j in range(SSM_CHUNKS):
            hj = h_ref[:, j * CHUNK_LANES:(j + 1) * CHUNK_LANES].astype(BF16)
            cols = slice(j * LANES, (j + 1) * LANES)
            y_ref[:, cols] = _dot(hj, wc_ref[j]) + d_ref[:, cols] * u[:, cols]

    return pl.pallas_call(
        body, name="ssm_fwd", grid=(bsz, nt),
        in_specs=[pl.BlockSpec((tt, SSM_WIDTH), lambda b, c: (b * nt + c, u_blk)),
                  pl.BlockSpec((SSM_CHUNKS, LANES, CHUNK_LANES), lambda b, c: (0, 0, 0)),
                  pl.BlockSpec((SSM_CHUNKS, CHUNK_LANES, LANES), lambda b, c: (0, 0, 0)),
                  pl.BlockSpec((TAB_ROWS, STATE_LANES), lambda b, c: (0, 0)),
                  pl.BlockSpec((1, SSM_WIDTH), lambda b, c: (0, 0))],
        out_specs=[pl.BlockSpec((tt, SSM_WIDTH), lambda b, c: (b * nt + c, 0)),
                   pl.BlockSpec((tt, STATE_LANES), lambda b, c: (b * nt + c, 0))],
        out_shape=[jax.ShapeDtypeStruct((bsz * seq, SSM_WIDTH), F32),
                   jax.ShapeDtypeStruct((bsz * seq, STATE_LANES), F32)],
        scratch_shapes=[pltpu.VMEM((SUBLANES, STATE_LANES), F32)],
        compiler_params=_params(("parallel", "arbitrary")),
    )(rest, wb4, wc4, tabs, dskip)


def ssm_bwd(dys, rest, hs, wb4, wc4, tabs, dskip, bsz, seq, u_blk):
    tt = _ssm_tile(seq)
    nt = seq // tt
    nb = tt // SUBLANES

    def body(dy_ref, u_ref, h_ref, hp_ref, wb_ref, wc_ref, tab_ref, d_ref,
             du_ref, ga_ref, gwb_ref, gwc_ref, gd_ref, g_s, carry):
        b = pl.program_id(0)
        c = pl.program_id(1)

        @pl.when(c == 0)
        def _():
            carry[...] = jnp.zeros_like(carry)

        @pl.when((b == 0) & (c == 0))
        def _():
            ga_ref[...] = jnp.zeros_like(ga_ref)
            gwb_ref[...] = jnp.zeros_like(gwb_ref)
            gwc_ref[...] = jnp.zeros_like(gwc_ref)
            gd_ref[...] = jnp.zeros_like(gd_ref)

        dy = dy_ref[...].astype(F32)
        dyb = dy.astype(BF16)
        u = u_ref[...]
        ub = u.astype(BF16)
        first_chunk = c == nt - 1
        for j in range(SSM_CHUNKS):
            g_s[:, j * CHUNK_LANES:(j + 1) * CHUNK_LANES] = _dot(dyb[:, j * LANES:(j + 1) * LANES], wc_ref[j], NT)
        for j in range(SSM_CHUNKS):
            re = slice(j * CHUNK_LANES, j * CHUNK_LANES + CHUNK_STATES)
            im = slice(j * CHUNK_LANES + CHUNK_STATES, (j + 1) * CHUNK_LANES)
            row = lax.broadcasted_iota(jnp.int32, (SUBLANES, CHUNK_STATES), 0)

            def blk(n, car):
                bi = nb - 1 - n
                r0 = pl.multiple_of(bi * SUBLANES, SUBLANES)
                gr, gi = _scan_block(g_s[pl.ds(r0, SUBLANES), re], g_s[pl.ds(r0, SUBLANES), im],
                                     tab_ref, re, im, car[0], car[1], True)
                g_s[pl.ds(r0, SUBLANES), re] = gr
                g_s[pl.ds(r0, SUBLANES), im] = gi
                rp = pl.multiple_of(jnp.maximum(bi - 1, 0) * SUBLANES, SUBLANES)
                inside = bi > 0
                live = jnp.where(jnp.logical_or(inside, jnp.logical_not(first_chunk)), 1.0, 0.0)
                pr = jnp.where(inside, h_ref[pl.ds(rp, SUBLANES), re], hp_ref[:, re])[SUBLANES - 1:SUBLANES] * live
                pi = jnp.where(inside, h_ref[pl.ds(rp, SUBLANES), im], hp_ref[:, im])[SUBLANES - 1:SUBLANES] * live
                hr = jnp.where(row >= 1, pltpu.roll(h_ref[pl.ds(r0, SUBLANES), re], 1, 0), pr)
                hi = jnp.where(row >= 1, pltpu.roll(h_ref[pl.ds(r0, SUBLANES), im], 1, 0), pi)
                return (gr[0:1], gi[0:1], car[2] + gr * hr + gi * hi, car[3] + gi * hr - gr * hi)

            zero = jnp.zeros((SUBLANES, CHUNK_STATES), F32)
            cr, ci, sr, si = lax.fori_loop(0, nb, blk, (carry[0:1, re], carry[0:1, im], zero, zero), unroll=2)
            carry[0:1, re] = cr
            carry[0:1, im] = ci
            ga_ref[:, re] += sr
            ga_ref[:, im] += si
        for j in range(SSM_CHUNKS):
            cols = slice(j * LANES, (j + 1) * LANES)
            lanes = slice(j * CHUNK_LANES, (j + 1) * CHUNK_LANES)
            gj = g_s[:, lanes].astype(BF16)
            du_ref[:, cols] = (_dot(gj, wb_ref[j], NT) + d_ref[:, cols] * dy[:, cols]).astype(du_ref.dtype)
            gwb_ref[j] += _dot(ub[:, cols], gj, TN)
            gwc_ref[j] += _dot(h_ref[:, lanes].astype(BF16), dyb[:, cols], TN)
        gd_ref[...] += _fold8(dy * u)

    def prev_rows(b, c):
        chunk = nt - 1 - c
        return (jnp.maximum((b * nt + chunk) * nb - 1, 0), 0)

    return pl.pallas_call(
        body, name="ssm_bwd", grid=(bsz, nt),
        in_specs=[pl.BlockSpec((tt, SSM_WIDTH), lambda b, c: (b * nt + nt - 1 - c, 0)),
                  pl.BlockSpec((tt, SSM_WIDTH), lambda b, c: (b * nt + nt - 1 - c, u_blk)),
                  pl.BlockSpec((tt, STATE_LANES), lambda b, c: (b * nt + nt - 1 - c, 0)),
                  pl.BlockSpec((SUBLANES, STATE_LANES), prev_rows),
                  pl.BlockSpec((SSM_CHUNKS, LANES, CHUNK_LANES), lambda b, c: (0, 0, 0)),
                  pl.BlockSpec((SSM_CHUNKS, CHUNK_LANES, LANES), lambda b, c: (0, 0, 0)),
                  pl.BlockSpec((TAB_ROWS, STATE_LANES), lambda b, c: (0, 0)),
                  pl.BlockSpec((1, SSM_WIDTH), lambda b, c: (0, 0))],
        out_specs=[pl.BlockSpec((tt, SSM_WIDTH), lambda b, c: (b * nt + nt - 1 - c, 0)),
                   pl.BlockSpec((SUBLANES, STATE_LANES), lambda b, c: (0, 0)),
                   pl.BlockSpec((SSM_CHUNKS, LANES, CHUNK_LANES), lambda b, c: (0, 0, 0)),
                   pl.BlockSpec((SSM_CHUNKS, CHUNK_LANES, LANES), lambda b, c: (0, 0, 0)),
                   pl.BlockSpec((SUBLANES, SSM_WIDTH), lambda b, c: (0, 0))],
        out_shape=[jax.ShapeDtypeStruct((bsz * seq, SSM_WIDTH), BF16),
                   jax.ShapeDtypeStruct((SUBLANES, STATE_LANES), F32),
                   jax.ShapeDtypeStruct((SSM_CHUNKS, LANES, CHUNK_LANES), F32),
                   jax.ShapeDtypeStruct((SSM_CHUNKS, CHUNK_LANES, LANES), F32),
                   jax.ShapeDtypeStruct((SUBLANES, SSM_WIDTH), F32)],
        scratch_shapes=[pltpu.VMEM((tt, STATE_LANES), F32), pltpu.VMEM((SUBLANES, STATE_LANES), F32)],
        compiler_params=_params(("arbitrary", "arbitrary")),
    )(dys, rest, hs, hs, wb4, wc4, tabs, dskip)


def _gelu(v):
    t = jnp.tanh(GELU_C * (v + GELU_A * v * v * v))
    return 0.5 * v * (1.0 + t), t


def mix_fwd(ya, ys, rest, x0, wglu, bglu, wba, wbb, wout):
    t, d = x0.shape
    tm = _rows(t, 512)

    def body(ya_ref, ys_ref, ga_ref, gb_ref, x_ref, wg_ref, bg_ref, wa_ref, wb_ref, wo_ref,
             x1_ref, z_ref, pa_ref, pb_ref, yb_ref, yb2_ref, mx_ref):
        yb, _ = _gelu(ys_ref[...])
        ybb = yb.astype(BF16)
        z = _dot(ybb, wg_ref[...]) + bg_ref[...]
        yb2 = (yb * _sigmoid(z)).astype(BF16)
        pa = _dot(ya_ref[...], wa_ref[...])
        pb = _dot(yb2, wb_ref[...])
        mixed = (_sigmoid(ga_ref[...]) * pa + _sigmoid(gb_ref[...]) * pb).astype(BF16)
        x1_ref[...] = x_ref[...] + _dot(mixed, wo_ref[...])
        z_ref[...] = z.astype(z_ref.dtype)
        pa_ref[...] = pa.astype(pa_ref.dtype)
        pb_ref[...] = pb.astype(pb_ref.dtype)
        yb_ref[...] = ybb
        yb2_ref[...] = yb2
        mx_ref[...] = mixed

    row = lambda w: pl.BlockSpec((tm, w), lambda i: (i, 0))
    full = lambda a: pl.BlockSpec(a.shape, lambda i: (0,) * a.ndim)
    return pl.pallas_call(
        body, name="mix_fwd", grid=(t // tm,),
        in_specs=[row(ATTN_WIDTH), row(SSM_WIDTH),
                  pl.BlockSpec((tm, d), lambda i: (i, 0)), pl.BlockSpec((tm, d), lambda i: (i, 1)),
                  row(d), full(wglu), full(bglu), full(wba), full(wbb), full(wout)],
        out_specs=[row(d), row(SSM_WIDTH), row(d), row(d), row(SSM_WIDTH), row(SSM_WIDTH), row(d)],
        out_shape=[jax.ShapeDtypeStruct((t, d), F32), jax.ShapeDtypeStruct((t, SSM_WIDTH), BF16),
                   jax.ShapeDtypeStruct((t, d), BF16), jax.ShapeDtypeStruct((t, d), BF16),
                   jax.ShapeDtypeStruct((t, SSM_WIDTH), BF16), jax.ShapeDtypeStruct((t, SSM_WIDTH), BF16),
                   jax.ShapeDtypeStruct((t, d), BF16)],
        compiler_params=_params(("parallel",)),
    )(ya, ys, rest, rest, x0, wglu, bglu, wba, wbb, wout)


def mix_bwd(dx1, rest, pa, pb, z, ys, wglu, wba, wbb, wout):
    t, d = dx1.shape
    tm = _rows(t, 512)

    def body(dx_ref, ga_ref, gb_ref, pa_ref, pb_ref, z_ref, ys_ref, wg_ref, wa_ref, wb_ref, wo_ref,
             dya_ref, dys_ref, dg_ref, dpa_ref, dpb_ref, dz_ref, dbg_ref):
        @pl.when(pl.program_id(0) == 0)
        def _():
            dbg_ref[...] = jnp.zeros_like(dbg_ref)

        dmix = _dot(dx_ref[...].astype(BF16), wo_ref[...], NT)
        sa = _sigmoid(ga_ref[...])
        sb = _sigmoid(gb_ref[...])
        dpa = (dmix * sa).astype(BF16)
        dpb = (dmix * sb).astype(BF16)
        dg_ref[:, 0:d] = (dmix * pa_ref[...].astype(F32) * sa * (1.0 - sa)).astype(dg_ref.dtype)
        dg_ref[:, d:2 * d] = (dmix * pb_ref[...].astype(F32) * sb * (1.0 - sb)).astype(dg_ref.dtype)
        dpa_ref[...] = dpa
        dpb_ref[...] = dpb
        dya_ref[...] = _dot(dpa, wa_ref[...], NT).astype(dya_ref.dtype)
        dyb2 = _dot(dpb, wb_ref[...], NT)
        ys = ys_ref[...]
        yb, th = _gelu(ys)
        sg = _sigmoid(z_ref[...].astype(F32))
        dz = dyb2 * yb * sg * (1.0 - sg)
        dzb = dz.astype(BF16)
        dz_ref[...] = dzb
        dbg_ref[...] += _fold8(dz)
        dyb = dyb2 * sg + _dot(dzb, wg_ref[...], NT)
        dgelu = 0.5 * (1.0 + th) + 0.5 * ys * (1.0 - th * th) * GELU_C * (1.0 + 3.0 * GELU_A * ys * ys)
        dys_ref[...] = (dyb * dgelu).astype(dys_ref.dtype)

    row = lambda w: pl.BlockSpec((tm, w), lambda i: (i, 0))
    full = lambda a: pl.BlockSpec(a.shape, lambda i: (0,) * a.ndim)
    return pl.pallas_call(
        body, name="mix_bwd", grid=(t // tm,),
        in_specs=[row(d), pl.BlockSpec((tm, d), lambda i: (i, 0)), pl.BlockSpec((tm, d), lambda i: (i, 1)),
                  row(d), row(d), row(SSM_WIDTH), row(SSM_WIDTH), full(wglu), full(wba), full(wbb), full(wout)],
        out_specs=[row(ATTN_WIDTH), row(SSM_WIDTH), row(2 * d), row(d), row(d), row(SSM_WIDTH),
                   pl.BlockSpec((SUBLANES, SSM_WIDTH), lambda i: (0, 0))],
        out_shape=[jax.ShapeDtypeStruct((t, ATTN_WIDTH), BF16), jax.ShapeDtypeStruct((t, SSM_WIDTH), BF16),
                   jax.ShapeDtypeStruct((t, 2 * d), BF16), jax.ShapeDtypeStruct((t, d), BF16),
                   jax.ShapeDtypeStruct((t, d), BF16), jax.ShapeDtypeStruct((t, SSM_WIDTH), BF16),
                   jax.ShapeDtypeStruct((SUBLANES, SSM_WIDTH), F32)],
        compiler_params=_params(("arbitrary",)),
    )(dx1, rest, rest, pa, pb, z, ys, wglu, wba, wbb, wout)


def mlp_fwd(x1, g, wup, wdown):
    t, d = x1.shape
    ff = wup.shape[1]
    tm, tf = _rows(t, 1024), _pick(ff, 1024)
    nf = ff // tf

    def body(x_ref, g_ref, wu_ref, wd_ref, x2_ref, up_ref, h_s, acc_s):
        f = pl.program_id(1)

        @pl.when(f == 0)
        def _():
            xv = x_ref[...]
            h_s[...] = (xv * _rms_scale(xv) * g_ref[...]).astype(BF16)
            acc_s[...] = jnp.zeros_like(acc_s)

        up = _dot(h_s[...], wu_ref[...])
        up_ref[...] = up.astype(up_ref.dtype)
        act = jnp.square(jnp.maximum(up, 0.0)).astype(BF16)
        acc_s[...] += _dot(act, wd_ref[...])

        @pl.when(f == nf - 1)
        def _():
            x2_ref[...] = x_ref[...] + acc_s[...]

    return pl.pallas_call(
        body, name="mlp_fwd", grid=(t // tm, nf),
        in_specs=[pl.BlockSpec((tm, d), lambda i, f: (i, 0)), pl.BlockSpec((1, d), lambda i, f: (0, 0)),
                  pl.BlockSpec((d, tf), lambda i, f: (0, f)), pl.BlockSpec((tf, d), lambda i, f: (f, 0))],
        out_specs=[pl.BlockSpec((tm, d), lambda i, f: (i, 0)), pl.BlockSpec((tm, tf), lambda i, f: (i, f))],
        out_shape=[jax.ShapeDtypeStruct((t, d), F32), jax.ShapeDtypeStruct((t, ff), BF16)],
        scratch_shapes=[pltpu.VMEM((tm, d), BF16), pltpu.VMEM((tm, d), F32)],
        compiler_params=_params(("parallel", "arbitrary")),
    )(x1, g, wup, wdown)


def mlp_bwd(dx2, up, x1, g, wup, wdown, carries=()):
    t, d = x1.shape
    ff = wup.shape[1]
    tm, tf = _rows(t, 512), _pick(ff, 1024)
    nf = ff // tf
    ni = t // tm
    sends, landings, n_sems, build = _carry_plan(carries)
    nc = len(sends)

    def body(dx_ref, up_ref, x_ref, g_ref, wu_ref, wd_ref, *rest):
        ins, (dup_ref, dx1_ref, dg_ref), lands = rest[:nc], rest[nc:nc + 3], rest[nc + 3:2 * nc + 3]
        (dxb_s, acc_s), sems = rest[2 * nc + 3:2 * nc + 5], rest[2 * nc + 5:]
        i = pl.program_id(0)
        f = pl.program_id(1)
        if nc:
            @pl.when((i == 0) & (f == 0))
            def _():
                for cp in build(ins, lands, sems):
                    cp.start()

        @pl.when((i == 0) & (f == 0))
        def _():
            dg_ref[...] = jnp.zeros_like(dg_ref)

        @pl.when(f == 0)
        def _():
            dxb_s[...] = dx_ref[...].astype(BF16)
            acc_s[...] = jnp.zeros_like(acc_s)

        dact = _dot(dxb_s[...], wd_ref[...], NT)
        dup = (dact * 2.0 * jnp.maximum(up_ref[...].astype(F32), 0.0)).astype(BF16)
        dup_ref[...] = dup
        acc_s[...] += _dot(dup, wu_ref[...], NT)

        @pl.when(f == nf - 1)
        def _():
            dxn, dgain = _rms_bwd(x_ref[...], g_ref[...], acc_s[...])
            dx1_ref[...] = dx_ref[...] + dxn
            dg_ref[...] += _fold8(dgain)

        if nc:
            @pl.when((i == ni - 1) & (f == nf - 1))
            def _():
                for cp in build(ins, lands, sems):
                    cp.wait()

    sem_shapes = [pltpu.SemaphoreType.DMA((n_sems,)), pltpu.SemaphoreType.DMA((n_sems,))] if nc else []
    outs = pl.pallas_call(
        body, name="mlp_bwd", grid=(ni, nf),
        in_specs=[pl.BlockSpec((tm, d), lambda i, f: (i, 0)), pl.BlockSpec((tm, tf), lambda i, f: (i, f)),
                  pl.BlockSpec((tm, d), lambda i, f: (i, 0)), pl.BlockSpec((1, d), lambda i, f: (0, 0)),
                  pl.BlockSpec((d, tf), lambda i, f: (0, f)), pl.BlockSpec((tf, d), lambda i, f: (f, 0))] + [ANY] * nc,
        out_specs=[pl.BlockSpec((tm, tf), lambda i, f: (i, f)), pl.BlockSpec((tm, d), lambda i, f: (i, 0)),
                   pl.BlockSpec((SUBLANES, d), lambda i, f: (0, 0))] + [ANY] * nc,
        out_shape=[jax.ShapeDtypeStruct((t, ff), BF16), jax.ShapeDtypeStruct((t, d), F32),
                   jax.ShapeDtypeStruct((SUBLANES, d), F32)] + landings,
        scratch_shapes=[pltpu.VMEM((tm, d), BF16), pltpu.VMEM((tm, d), F32)] + sem_shapes,
        compiler_params=_params(("arbitrary", "arbitrary")),
    )(dx2, up, x1, g, wup, wdown, *sends)
    return outs[0], outs[1], outs[2], list(outs[3:])


def proj_bwd(dproj, wpad, x0, dx1, g, carries=()):
    t, d = x0.shape
    m = wpad.shape[1]
    tm = _rows(t, 512)
    ni = t // tm
    sends, landings, n_sems, build = _carry_plan(carries)
    nc = len(sends)

    def body(dp_ref, w_ref, x_ref, dx1_ref, g_ref, *rest):
        ins, (dx0_ref, dg_ref), lands, sems = rest[:nc], rest[nc:nc + 2], rest[nc + 2:2 * nc + 2], rest[2 * nc + 2:]

        @pl.when(pl.program_id(0) == 0)
        def _():
            dg_ref[...] = jnp.zeros_like(dg_ref)
            for cp in build(ins, lands, sems):
                cp.start()

        dh = _dot(dp_ref[...], w_ref[...], NT)
        dxn, dgain = _rms_bwd(x_ref[...], g_ref[...], dh)
        dx0_ref[...] = dx1_ref[...] + dxn
        dg_ref[...] += _fold8(dgain)

        if nc:
            @pl.when(pl.program_id(0) == ni - 1)
            def _():
                for cp in build(ins, lands, sems):
                    cp.wait()

    sem_shapes = [pltpu.SemaphoreType.DMA((n_sems,)), pltpu.SemaphoreType.DMA((n_sems,))] if nc else []
    outs = pl.pallas_call(
        body, name="proj_bwd", grid=(ni,),
        in_specs=[pl.BlockSpec((tm, m), lambda i: (i, 0)), pl.BlockSpec((d, m), lambda i: (0, 0)),
                  pl.BlockSpec((tm, d), lambda i: (i, 0)), pl.BlockSpec((tm, d), lambda i: (i, 0)),
                  pl.BlockSpec((1, d), lambda i: (0, 0))] + [ANY] * nc,
        out_specs=[pl.BlockSpec((tm, d), lambda i: (i, 0)), pl.BlockSpec((SUBLANES, d), lambda i: (0, 0))] + [ANY] * nc,
        out_shape=[jax.ShapeDtypeStruct((t, d), F32), jax.ShapeDtypeStruct((SUBLANES, d), F32)] + landings,
        scratch_shapes=sem_shapes,
        compiler_params=_params(("arbitrary",)),
    )(dproj, wpad, x0, dx1, g, *sends)
    return outs[0], outs[1], list(outs[2:])


def final_loss(x, g, target):
    t, d = x.shape
    tm = _rows(t, 512)

    def body(x_ref, g_ref, t_ref, dx_ref, ls_ref, dg_ref):
        @pl.when(pl.program_id(0) == 0)
        def _():
            ls_ref[...] = jnp.zeros_like(ls_ref)
            dg_ref[...] = jnp.zeros_like(dg_ref)

        xv = x_ref[...]
        gv = g_ref[...]
        err = xv * _rms_scale(xv) * gv - t_ref[...]
        ls_ref[...] += _fold8(err * err) * (0.5 / d)
        dxn, dgain = _rms_bwd(xv, gv, err * (1.0 / d))
        dx_ref[...] = dxn
        dg_ref[...] += _fold8(dgain)

    return pl.pallas_call(
        body, name="final_loss", grid=(t // tm,),
        in_specs=[pl.BlockSpec((tm, d), lambda i: (i, 0)), pl.BlockSpec((1, d), lambda i: (0, 0)),
                  pl.BlockSpec((tm, d), lambda i: (i, 0))],
        out_specs=[pl.BlockSpec((tm, d), lambda i: (i, 0)), pl.BlockSpec((SUBLANES, d), lambda i: (0, 0)),
                   pl.BlockSpec((SUBLANES, d), lambda i: (0, 0))],
        out_shape=[jax.ShapeDtypeStruct((t, d), F32), jax.ShapeDtypeStruct((SUBLANES, d), F32),
                   jax.ShapeDtypeStruct((SUBLANES, d), F32)],
        compiler_params=_params(("arbitrary",)),
    )(x, g, target)


def _peers():
    x, y, c = lax.axis_index("x"), lax.axis_index("y"), lax.axis_index("c")
    peers = []
    for m in range(1, 8):
        fx, fy, fc = (m >> 2) & 1, (m >> 1) & 1, m & 1
        peers.append((m, ((1 - x) if fx else x, (1 - y) if fy else y, (1 - c) if fc else c)))
    return x, y, c, peers


def _remote(src, dst, sems, k, peer):
    return pltpu.make_async_remote_copy(src_ref=src, dst_ref=dst, send_sem=sems[0].at[k], recv_sem=sems[1].at[k],
                                        device_id=peer, device_id_type=MESH)


def _gather_copies(w_refs, out_refs, sems, first):
    x, y, c, peers = _peers()
    chip = 2 * x + y
    cps = []
    for t, (w, o) in enumerate(zip(w_refs, out_refs)):
        half = w.shape[0] // 2
        rows = pl.ds(c * half, half)
        for m, peer in peers:
            if m >> 1:
                cps.append(_remote(w.at[rows], o.at[chip, rows], sems, first + 7 * t + m - 1, peer))
    return cps


def _reduce_copies(g_refs, out_refs, sems, first):
    x, y, c, peers = _peers()
    me = 4 * x + 2 * y + c
    cps = []
    for t, (g, o) in enumerate(zip(g_refs, out_refs)):
        for m, (px, py, pc) in peers:
            cps.append(_remote(g.at[2 * px + py], o.at[me], sems, first + 7 * t + m - 1, (px, py, pc)))
    return cps


def _all_copies(s_refs, out_refs, sems, first):
    x, y, c, peers = _peers()
    me = 4 * x + 2 * y + c
    return [_remote(s, o.at[me], sems, first + 7 * t + m - 1, peer)
            for t, (s, o) in enumerate(zip(s_refs, out_refs)) for m, peer in peers]


EXCHANGES = {"gather": (_gather_copies, lambda a: (4,) + a.shape),
             "reduce": (_reduce_copies, lambda a: (8,) + a.shape[1:]),
             "all": (_all_copies, lambda a: (8,) + a.shape)}


def _carry_plan(carries):
    inputs, shapes, spans = [], [], []
    for kind, arrs in carries:
        for a in arrs:
            inputs.append(a)
            shapes.append(jax.ShapeDtypeStruct(EXCHANGES[kind][1](a), a.dtype))
        spans.append((kind, len(arrs)))

    def build(in_refs, out_refs, sems):
        cps, pos = [], 0
        for kind, cnt in spans:
            cps += EXCHANGES[kind][0](in_refs[pos:pos + cnt], out_refs[pos:pos + cnt], sems, 7 * pos)
            pos += cnt
        return cps

    return inputs, shapes, 7 * len(inputs), build


def exchange(carries, name):
    inputs, shapes, n_sems, build = _carry_plan(carries)
    n = len(inputs)

    def body(*refs):
        cps = build(refs[:n], refs[n:2 * n], refs[2 * n:])
        for cp in cps:
            cp.start()
        for cp in cps:
            cp.wait()

    return pl.pallas_call(
        body, name=name, in_specs=[ANY] * n, out_specs=[ANY] * n, out_shape=shapes,
        scratch_shapes=[pltpu.SemaphoreType.DMA((n_sems,)), pltpu.SemaphoreType.DMA((n_sems,))],
    )(*inputs)


def _adamw_math(w, g, m, v):
    m = ADAM_B1 * m + (1.0 - ADAM_B1) * g
    v = ADAM_B2 * v + (1.0 - ADAM_B2) * (g * g)
    m_hat = m / (1.0 - ADAM_B1 ** ADAM_STEP)
    v_hat = v / (1.0 - ADAM_B2 ** ADAM_STEP)
    delta = -ADAM_LR * (m_hat / (jnp.sqrt(v_hat) + ADAM_EPS) + ADAM_WD * w)
    return delta, m, v


def adamw_layer(l, w, m, v, parts, own, device, chip, bufs, name):
    _, r, cdim = w.shape
    tr = _rows(r, PACK_ROW_TILE)

    def body(dev_ref, chip_ref, w_ref, m_ref, v_ref, p_ref, o_ref, *rest):
        g_ref, d_ref, nm_ref, nv_ref = rest[-4:]
        g = None
        for dev in range(8):
            part = jnp.where(dev_ref[0] == dev, o_ref[0], p_ref[dev]).astype(F32)
            g = part if g is None else g + part
        d, nm, nv = _adamw_math(w_ref[0], g, m_ref[0], v_ref[0])
        g_ref[0] = g
        d_ref[0] = d
        nm_ref[0] = nm
        nv_ref[0] = nv

    lay = pl.BlockSpec((1, tr, cdim), lambda i, dev_ref, chip_ref: (l, i, 0))
    in_specs = [lay, lay, lay, pl.BlockSpec((8, tr, cdim), lambda i, dev_ref, chip_ref: (0, i, 0)),
                pl.BlockSpec((1, tr, cdim), lambda i, dev_ref, chip_ref: (chip_ref[0], i, 0))]
    args = [device, chip, w, m, v, parts, own]
    aliases = {}
    if bufs is not None:
        in_specs += [ANY] * 4
        aliases = {len(args) + k: k for k in range(4)}
        args += list(bufs)
    return pl.pallas_call(
        body, name=name,
        grid_spec=pltpu.PrefetchScalarGridSpec(num_scalar_prefetch=2, grid=(r // tr,), in_specs=in_specs,
                                               out_specs=[lay] * 4),
        out_shape=[jax.ShapeDtypeStruct(w.shape, F32)] * 4,
        input_output_aliases=aliases,
        compiler_params=_params(("parallel",)),
    )(*args)


def sum_and_adamw(parts, own, me, w, m, v):
    _, r, cdim = parts.shape
    tr = _rows(r, PACK_ROW_TILE)

    def body(me_ref, p_ref, o_ref, w_ref, m_ref, v_ref, g_ref, d_ref, nm_ref, nv_ref):
        part = lambda k: jnp.where(me_ref[0] == k, o_ref[...], p_ref[k]).astype(F32)
        g = part(0)
        for k in range(1, 8):
            g = g + part(k)
        d, nm, nv = _adamw_math(w_ref[...], g, m_ref[...], v_ref[...])
        g_ref[...] = g
        d_ref[...] = d
        nm_ref[...] = nm
        nv_ref[...] = nv

    spec = pl.BlockSpec((tr, cdim), lambda i, me_ref: (i, 0))
    return pl.pallas_call(
        body, name="sum_and_adamw",
        grid_spec=pltpu.PrefetchScalarGridSpec(
            num_scalar_prefetch=1, grid=(r // tr,),
            in_specs=[pl.BlockSpec((8, tr, cdim), lambda i, me_ref: (0, i, 0)), spec, spec, spec, spec],
            out_specs=[spec] * 4),
        out_shape=[jax.ShapeDtypeStruct((r, cdim), F32)] * 4,
        compiler_params=_params(("parallel",)),
    )(me, parts, own, w, m, v)


SHARDED = ("w_in", "w_glu", "w_branch_a", "w_branch_b", "w_out", "w_mlp_up", "w_mlp_down")
SHARD_AXIS = {"w_in": 2, "w_glu": 1, "w_branch_a": 2, "w_branch_b": 2, "w_out": 1, "w_mlp_up": 2, "w_mlp_down": 1}
SMALL = ("norm_mix", "b_forget", "ssm_lambda_re", "ssm_lambda_im", "ssm_log_dt", "ssm_b_re", "ssm_b_im",
         "ssm_c_re", "ssm_c_im", "ssm_d", "b_glu", "norm_mlp", "norm_final")
SMALL_WIDE = ("ssm_b_re", "ssm_b_im", "ssm_c_re", "ssm_c_im")


def pack_flat(arrs):
    flat = jnp.concatenate([a.reshape(-1).astype(F32) for a in arrs])
    unit = PACK_COLS * PACK_ROW_TILE
    rows = (flat.shape[0] + unit - 1) // unit * PACK_ROW_TILE
    return jnp.pad(flat, (0, rows * PACK_COLS - flat.shape[0])).reshape(rows, PACK_COLS)


def unpack_flat(packed, shapes):
    flat = packed.reshape(-1)
    out, off = [], 0
    for s in shapes:
        n = math.prod(s)
        out.append(flat[off:off + n].reshape(tuple(s)))
        off += n
    return out


def _discretise(lam_re, lam_im, log_dt, b_re, b_im):
    dt = jnp.exp(log_dt)[:, None]
    mag = jnp.exp(lam_re * dt)
    ar = mag * jnp.cos(lam_im * dt)
    ai = mag * jnp.sin(lam_im * dt)
    den = lam_re * lam_re + lam_im * lam_im
    cr = ((ar - 1.0) * lam_re + ai * lam_im) / den
    ci = (ai * lam_re - (ar - 1.0) * lam_im) / den
    bbr = cr[:, :, None] * b_re - ci[:, :, None] * b_im
    bbi = cr[:, :, None] * b_im + ci[:, :, None] * b_re
    return ar, ai, bbr, bbi


def _ssm_inputs(ar, ai, bbr, bbi, c_re, c_im):
    pr, pi = [ar], [ai]
    for _ in range(SUBLANES - 1):
        pr, pi = pr + [pr[-1] * ar - pi[-1] * ai], pi + [pr[-1] * ai + pi[-1] * ar]
    powers = jnp.stack([jnp.stack(pr), jnp.stack(pi)], axis=1)
    powers = powers.reshape(SUBLANES, 2, SSM_CHUNKS, CHUNK_STATES).transpose(0, 2, 1, 3).reshape(SUBLANES, STATE_LANES)
    conj = powers * jnp.tile(jnp.repeat(jnp.asarray([1.0, -1.0], F32), CHUNK_STATES), SSM_CHUNKS)
    idx = jnp.arange(SUBLANES)[:, None]
    tabs = jnp.concatenate(
        [jnp.where(idx >= s, powers[s - 1][None, :], 0.0) for s in SCAN_STEPS] + [powers]
        + [jnp.where(idx < SUBLANES - s, conj[s - 1][None, :], 0.0) for s in SCAN_STEPS] + [conj[::-1]], axis=0)
    eye = jnp.eye(CHUNK_GROUPS, dtype=F32)
    wb = jnp.stack([bbr, bbi]).reshape(2, SSM_CHUNKS, CHUNK_GROUPS, SSM_STATE, SSM_GROUP_CH).transpose(1, 2, 4, 0, 3)
    wb4 = (wb[:, :, :, :, None, :] * eye[None, :, None, None, :, None]).reshape(SSM_CHUNKS, LANES, CHUNK_LANES)
    wc = jnp.stack([c_re, -c_im]).reshape(2, SSM_CHUNKS, CHUNK_GROUPS, SSM_GROUP_CH, SSM_STATE).transpose(1, 0, 4, 2, 3)
    wc4 = (wc[:, :, None, :, :, :] * eye[None, None, :, None, :, None]).reshape(SSM_CHUNKS, CHUNK_LANES, LANES)
    return tabs, wb4.astype(BF16), wc4.astype(BF16)


def _ssm_param_grads(ga8, gwb, gwc):
    eye = jnp.eye(CHUNK_GROUPS, dtype=F32)
    ga = jnp.sum(ga8, axis=0).reshape(SSM_CHUNKS, 2, CHUNK_STATES)
    gar = ga[:, 0].reshape(SSM_GROUPS, SSM_STATE)
    gai = ga[:, 1].reshape(SSM_GROUPS, SSM_STATE)

    def from_wb(g):
        t = g.reshape(SSM_CHUNKS, CHUNK_GROUPS, SSM_GROUP_CH, CHUNK_GROUPS, SSM_STATE)
        return jnp.einsum("jgchp,gh->jgpc", t, eye).reshape(SSM_GROUPS, SSM_STATE, SSM_GROUP_CH)

    def from_wc(g):
        t = g.reshape(SSM_CHUNKS, CHUNK_GROUPS, SSM_STATE, CHUNK_GROUPS, SSM_GROUP_CH)
        return jnp.einsum("jhpgc,gh->jgcp", t, eye).reshape(SSM_GROUPS, SSM_GROUP_CH, SSM_STATE)

    return (gar, gai, from_wb(gwb[:, :, :CHUNK_STATES]), from_wb(gwb[:, :, CHUNK_STATES:]),
            from_wc(gwc[:, :CHUNK_STATES]), -from_wc(gwc[:, CHUNK_STATES:]))


def kernel(x, norm_mix, w_in, b_forget, ssm_lambda_re, ssm_lambda_im, ssm_log_dt, ssm_b_re, ssm_b_im, ssm_c_re, ssm_c_im, ssm_d, w_glu, b_glu, w_branch_a, w_branch_b, w_out, norm_mlp, w_mlp_up, w_mlp_down, norm_final, loss_target, m_norm_mix, m_w_in, m_b_forget, m_ssm_lambda_re, m_ssm_lambda_im, m_ssm_log_dt, m_ssm_b_re, m_ssm_b_im, m_ssm_c_re, m_ssm_c_im, m_ssm_d, m_w_glu, m_b_glu, m_w_branch_a, m_w_branch_b, m_w_out, m_norm_mlp, m_w_mlp_up, m_w_mlp_down, m_norm_final, v_norm_mix, v_w_in, v_b_forget, v_ssm_lambda_re, v_ssm_lambda_im, v_ssm_log_dt, v_ssm_b_re, v_ssm_b_im, v_ssm_c_re, v_ssm_c_im, v_ssm_d, v_w_glu, v_b_glu, v_w_branch_a, v_w_branch_b, v_w_out, v_norm_mlp, v_w_mlp_up, v_w_mlp_down, v_norm_final):
    args = dict(locals())
    bsz, seq, d = x.shape
    nl = norm_mix.shape[0]
    tokens = bsz * seq
    aw, sw = ATTN_WIDTH, SSM_WIDTH
    chip = (2 * lax.axis_index("x") + lax.axis_index("y")).astype(jnp.int32)
    chip_id = chip.reshape(1)
    device_id = (2 * chip + lax.axis_index("c").astype(jnp.int32)).reshape(1)

    own = {n: args[n].astype(BF16) for n in SHARDED}
    late = [n for n in SHARDED if n != "w_in"]
    o_f, o_u, o_ga, o_gb = 3 * aw, 3 * aw + ATTN_HEADS, 3 * aw + ATTN_HEADS + sw, 3 * aw + ATTN_HEADS + sw + d
    u_blk = 2 * d // sw
    f_blk = (2 * d + sw) // F_PAD
    bf_pad = jnp.pad(b_forget, ((0, 0), (0, F_PAD - ATTN_HEADS)))

    def assemble(l, names, gathered):
        return {n: jnp.concatenate([jnp.where(chip == k, own[n][l], g[k]) for k in range(4)],
                                   axis=SHARD_AXIS[n] - 1) for n, g in zip(names, gathered)}

    def split_w_in(win):
        w = {"w_qkv": win[:, :o_f],
             "w_rest": jnp.concatenate([win[:, o_ga:o_gb], win[:, o_gb:], win[:, o_u:o_ga],
                                        jnp.pad(win[:, o_f:o_u], ((0, 0), (0, F_PAD - ATTN_HEADS)))], axis=1)}
        w["w_pad"] = jnp.concatenate([w["w_qkv"], w["w_rest"]], axis=1)
        return w

    disc = [jax.vjp(_discretise, ssm_lambda_re[l], ssm_lambda_im[l], ssm_log_dt[l], ssm_b_re[l], ssm_b_im[l])
            for l in range(nl)]

    g_in = exchange([("gather", [own["w_in"][0]])], "gather_first")
    xs = x.reshape(tokens, d)
    saved, weights = [], []
    for l in range(nl):
        w = split_w_in(assemble(l, ["w_in"], g_in)["w_in"])
        g1 = norm_mix[l].reshape(1, d)
        qkv = norm_matmul(xs, g1, w["w_qkv"], BF16, "proj_qkv")
        rest = norm_matmul(xs, g1, w["w_rest"], F32, "proj_rest")
        cumcol, cumrow = forget_cumsum(rest, bf_pad[l:l + 1], bsz, seq, f_blk)
        carries = [("gather", [own[n][l] for n in late] + ([own["w_in"][l + 1]] if l + 1 < nl else []))]
        ya, lser, lands = fox_fwd(qkv, cumrow, bsz, seq, carries)
        w.update(assemble(l, late, lands))
        g_in = lands[len(late):]
        tabs, wb4, wc4 = _ssm_inputs(*disc[l][0], ssm_c_re[l], ssm_c_im[l])
        dskip = ssm_d[l].reshape(1, sw)
        ys, hs = ssm_fwd(rest, wb4, wc4, tabs, dskip, bsz, seq, u_blk)
        x1, z, pa, pb, yb, yb2, mixed = mix_fwd(ya, ys, rest, xs, w["w_glu"], b_glu[l].reshape(1, sw),
                                                 w["w_branch_a"], w["w_branch_b"], w["w_out"])
        x2, up = mlp_fwd(x1, norm_mlp[l].reshape(1, d), w["w_mlp_up"], w["w_mlp_down"])
        saved.append(dict(x0=xs, qkv=qkv, rest=rest, cumcol=cumcol, ya=ya, lser=lser,
                          tabs=tabs, wb4=wb4, wc4=wc4, dskip=dskip, ys=ys, hs=hs, x1=x1, z=z, pa=pa, pb=pb,
                          yb=yb, yb2=yb2, mixed=mixed, up=up))
        weights.append(w)
        xs = x2
    dx, loss_rows, dgf_rows = final_loss(xs, norm_final.reshape(1, d), loss_target.reshape(tokens, d))
    loss = lax.psum(jnp.sum(loss_rows), ("x", "y", "c"))

    early = [n for n in SHARDED if n != "w_in"]
    big = {n: [None] * nl for n in SHARDED}
    parts = {n: [None] * nl for n in SHARDED}
    small = {n: [None] * nl for n in SMALL if n != "norm_final"}
    for l in reversed(range(nl)):
        s, w = saved[l], weights[l]
        g2 = norm_mlp[l].reshape(1, d)
        dup, dx1, dg2, lands = mlp_bwd(dx, s["up"], s["x1"], g2, w["w_mlp_up"], w["w_mlp_down"],
                                       [("reduce", [big["w_in"][l + 1]])] if l + 1 < nl else [])
        if l + 1 < nl:
            parts["w_in"][l + 1] = lands[0]
        big["w_mlp_down"][l] = matmul_tn(s["up"], dx, "grad_w_mlp_down", a_kind="relu2", shard_axis=0, out_dtype=BF16)
        big["w_mlp_up"][l] = matmul_tn(s["x1"], dup, "grad_w_mlp_up", a_kind="norm", gain=g2, shard_axis=1,
                                       out_dtype=BF16)
        small["norm_mlp"][l] = jnp.sum(dg2, axis=0)
        dya, dys, dgab, dpa, dpb, dz, dbg = mix_bwd(dx1, s["rest"], s["pa"], s["pb"], s["z"], s["ys"],
                                                    w["w_glu"], w["w_branch_a"], w["w_branch_b"], w["w_out"])
        big["w_out"][l] = matmul_tn(s["mixed"], dx1, "grad_w_out", shard_axis=0, out_dtype=BF16)
        big["w_branch_a"][l] = matmul_tn(s["ya"], dpa, "grad_w_branch_a", shard_axis=1, out_dtype=BF16)
        big["w_branch_b"][l] = matmul_tn(s["yb2"], dpb, "grad_w_branch_b", shard_axis=1, out_dtype=BF16)
        big["w_glu"][l] = matmul_tn(s["yb"], dz, "grad_w_glu", shard_axis=0, out_dtype=BF16)
        small["b_glu"][l] = jnp.sum(dbg, axis=0)
        du, ga8, gwb, gwc, gd8 = ssm_bwd(dys, s["rest"], s["hs"], s["wb4"], s["wc4"], s["tabs"], s["dskip"],
                                         bsz, seq, u_blk)
        gar, gai, gbbr, gbbi, gcr, gci = _ssm_param_grads(ga8, gwb, gwc)
        glr, gli, gdt, gbr, gbi = disc[l][1]((gar, gai, gbbr, gbbi))
        small["ssm_lambda_re"][l], small["ssm_lambda_im"][l], small["ssm_log_dt"][l] = glr, gli, gdt
        small["ssm_b_re"][l], small["ssm_b_im"][l] = gbr, gbi
        small["ssm_c_re"][l], small["ssm_c_im"][l] = gcr, gci
        small["ssm_d"][l] = jnp.sum(gd8, axis=0)
        dq, dk, dv, df, dbf, lands = fox_bwd(s["qkv"], dya, s["ya"], s["lser"], s["cumcol"], s["rest"],
                                             bf_pad[l:l + 1], bsz, seq, f_blk, [("reduce", [big[n][l] for n in early])])
        for n, p in zip(early, lands):
            parts[n][l] = p
        small["b_forget"][l] = jnp.sum(dbf, axis=0)[:ATTN_HEADS]
        dproj = jnp.concatenate([dq, dk, dv, dgab, du, df], axis=1)
        g1 = norm_mix[l].reshape(1, d)
        dwp = matmul_tn(s["x0"], dproj, "grad_w_in", a_kind="norm", gain=g1)
        big["w_in"][l] = jnp.stack(jnp.split(jnp.concatenate(
            [dwp[:, :o_f], dwp[:, o_f + 2 * d + sw:o_f + 2 * d + sw + ATTN_HEADS],
             dwp[:, o_f + 2 * d:o_f + 2 * d + sw], dwp[:, o_f:o_f + 2 * d]], axis=1), 4, axis=1)).astype(BF16)
        dx, dg1, lands = proj_bwd(dproj, w["w_pad"], s["x0"], dx1, g1,
                                  [("reduce", [big["w_in"][0]])] if l == 0 else [])
        if l == 0:
            parts["w_in"][0] = lands[0]
        small["norm_mix"][l] = jnp.sum(dg1, axis=0)
    grad_x = dx.reshape(bsz, seq, d)

    small_g = {n: jnp.stack(small[n]) if n != "norm_final" else jnp.sum(dgf_rows, axis=0) for n in SMALL}
    groups = {BF16: SMALL_WIDE, F32: [n for n in SMALL if n not in SMALL_WIDE]}
    packed = {dt: pack_flat([small_g[n] for n in names]).astype(dt) for dt, names in groups.items()}
    small_parts = exchange([("all", [packed[dt] for dt in groups])], "exchange_last")

    out_g, out_d, out_m, out_v = {}, {}, {}, {}
    for n in SHARDED:
        bufs = None
        for l in range(nl):
            bufs = adamw_layer(l, args[n], args["m_" + n], args["v_" + n], parts[n][l], big[n][l], device_id, chip_id,
                               bufs, "adamw_" + n)
        out_g[n], out_d[n], out_m[n], out_v[n] = bufs

    for (dt, names), landed in zip(groups.items(), small_parts):
        results = sum_and_adamw(landed, packed[dt], device_id, pack_flat([args[n] for n in names]),
                                pack_flat([args["m_" + n] for n in names]), pack_flat([args["v_" + n] for n in names]))
        for res, flat in zip((out_g, out_d, out_m, out_v), results):
            res.update(zip(names, unpack_flat(flat, [args[n].shape for n in names])))

    order = ("norm_mix", "w_in", "b_forget", "ssm_lambda_re", "ssm_lambda_im", "ssm_log_dt", "ssm_b_re",
             "ssm_b_im", "ssm_c_re", "ssm_c_im", "ssm_d", "w_glu", "b_glu", "w_branch_a", "w_branch_b", "w_out",
             "norm_mlp", "w_mlp_up", "w_mlp_down", "norm_final")
    return (loss, grad_x, *[out_g[n] for n in order], *[out_d[n] for n in order],
            *[out_m[n] for n in order], *[out_v[n] for n in order])
```

```python
import math

import jax
import jax.numpy as jnp
from jax import lax
from jax.experimental import pallas as pl
from jax.experimental.pallas import tpu as pltpu

F32 = jnp.float32
BF16 = jnp.bfloat16
MESH = pl.DeviceIdType.MESH
ANY = pl.BlockSpec(memory_space=pl.ANY)

ATTN_HEADS = 8
HEAD_DIM = 64
ATTN_WIDTH = ATTN_HEADS * HEAD_DIM
HEAD_PAIRS = ATTN_HEADS // 2
SSM_GROUPS = 32
SSM_GROUP_CH = 16
SSM_STATE = 64
SSM_WIDTH = SSM_GROUPS * SSM_GROUP_CH
LANES = 128
SUBLANES = 8
SSM_CHUNKS = SSM_WIDTH // LANES
CHUNK_GROUPS = SSM_GROUPS // SSM_CHUNKS
CHUNK_STATES = CHUNK_GROUPS * SSM_STATE
CHUNK_LANES = 2 * CHUNK_STATES
STATE_LANES = SSM_CHUNKS * CHUNK_LANES
F_PAD = LANES
RMS_EPS = 1e-6
ADAM_LR = 0.001
ADAM_B1 = 0.9
ADAM_B2 = 0.999
ADAM_EPS = 1e-08
ADAM_WD = 0.01
ADAM_STEP = 10
PACK_COLS = 1024
PACK_ROW_TILE = 256
PACK_ROW_ALIGN = 32
VMEM_LIMIT = 52 * 1024 * 1024
NEG_BIG = -1e30
GELU_C = math.sqrt(2.0 / math.pi)
GELU_A = 0.044715

NN = (((1,), (0,)), ((), ()))
NT = (((1,), (1,)), ((), ()))
TN = (((0,), (0,)), ((), ()))


def _pick(n, pref):
    if n <= pref:
        return n
    best = LANES
    for t in range(LANES, pref + 1, LANES):
        if n % t == 0:
            best = t
    assert n % best == 0, (n, pref)
    return best


def _rows(n, pref):
    t = min(n, pref)
    while n % t:
        t //= 2
    assert t % 16 == 0 or t == n, (n, pref)
    return t


def _params(sem):
    return pltpu.CompilerParams(dimension_semantics=sem, vmem_limit_bytes=VMEM_LIMIT)


def _fold8(v):
    r, c = v.shape
    return jnp.sum(v.reshape(r // SUBLANES, SUBLANES, c), axis=0)


def _dot(a, b, dims=None):
    if dims is None:
        return jnp.dot(a, b, preferred_element_type=F32)
    return lax.dot_general(a, b, dims, preferred_element_type=F32)


def _dot_exact(a, b, dims):
    return lax.dot_general(a, b, dims, preferred_element_type=F32, precision=lax.Precision.HIGHEST)


def _sigmoid(v):
    return 1.0 / (1.0 + jnp.exp(-v))


def _rms_scale(x):
    return lax.rsqrt(jnp.mean(x * x, axis=-1, keepdims=True) + RMS_EPS)


def _rms_bwd(x, g, dh):
    r = _rms_scale(x)
    xn = x * r
    dxn = dh * g
    dx = r * (dxn - xn * jnp.mean(dxn * xn, axis=-1, keepdims=True))
    return dx, dh * xn


def norm_matmul(x, g, w, out_dtype, name):
    t, d = x.shape
    m = w.shape[1]
    tm, tn = _rows(t, 1024), _pick(m, 1024)

    def body(x_ref, g_ref, w_ref, o_ref):
        xv = x_ref[...]
        h = (xv * _rms_scale(xv) * g_ref[...]).astype(BF16)
        o_ref[...] = _dot(h, w_ref[...]).astype(o_ref.dtype)

    return pl.pallas_call(
        body, name=name, grid=(t // tm, m // tn),
        in_specs=[pl.BlockSpec((tm, d), lambda i, j: (i, 0)),
                  pl.BlockSpec((1, d), lambda i, j: (0, 0)),
                  pl.BlockSpec((d, tn), lambda i, j: (0, j))],
        out_specs=pl.BlockSpec((tm, tn), lambda i, j: (i, j)),
        out_shape=jax.ShapeDtypeStruct((t, m), out_dtype),
        compiler_params=_params(("parallel", "arbitrary")),
    )(x, g, w)


def matmul_tn(a, b, name, a_kind="plain", gain=None, shard_axis=None, out_dtype=F32, tm_pref=1024, tn_pref=1536,
              tk_pref=1024):
    t, ma = a.shape
    nb = b.shape[1]
    tm = ma if a_kind == "norm" else _pick(ma, tm_pref)
    tn = _pick(nb, tn_pref)
    tk = _rows(t, tk_pref)
    nk = t // tk
    if shard_axis == 0:
        per = tm * 4 // ma
        assert per >= 1 and (ma // 4) * per == tm, (ma, tm)
        out_shape, out_spec = (4, ma // 4, nb), pl.BlockSpec((per, ma // 4, tn), lambda i, j, k: (i, 0, j))
    elif shard_axis == 1:
        per = tn * 4 // nb
        assert per >= 1 and (nb // 4) * per == tn, (nb, tn)
        out_shape, out_spec = (4, ma, nb // 4), pl.BlockSpec((per, tm, nb // 4), lambda i, j, k: (j, i, 0))
    else:
        out_shape, out_spec = (ma, nb), pl.BlockSpec((tm, tn), lambda i, j, k: (i, j))

    def body(*refs):
        if a_kind == "norm":
            a_ref, g_ref, b_ref, o_ref, acc = refs
        else:
            a_ref, b_ref, o_ref, acc = refs
        k = pl.program_id(2)

        @pl.when(k == 0)
        def _():
            acc[...] = jnp.zeros_like(acc)

        av = a_ref[...]
        if a_kind == "norm":
            av = av * _rms_scale(av) * g_ref[...]
        elif a_kind == "relu2":
            av = jnp.square(jnp.maximum(av.astype(F32), 0.0))
        acc[...] += _dot(av.astype(BF16), b_ref[...].astype(BF16), TN)

        @pl.when(k == nk - 1)
        def _():
            if shard_axis == 0:
                o_ref[...] = acc[...].reshape(o_ref.shape).astype(o_ref.dtype)
            elif shard_axis == 1:
                cs = nb // 4
                for n in range(o_ref.shape[0]):
                    o_ref[n] = acc[:, n * cs:(n + 1) * cs].astype(o_ref.dtype)
            else:
                o_ref[...] = acc[...].astype(o_ref.dtype)

    in_specs = [pl.BlockSpec((tk, tm), lambda i, j, k: (k, i))]
    args = [a]
    if a_kind == "norm":
        in_specs.append(pl.BlockSpec((1, ma), lambda i, j, k: (0, 0)))
        args.append(gain)
    in_specs.append(pl.BlockSpec((tk, tn), lambda i, j, k: (k, j)))
    args.append(b)
    return pl.pallas_call(
        body, name=name, grid=(ma // tm, nb // tn, nk),
        in_specs=in_specs,
        out_specs=out_spec,
        out_shape=jax.ShapeDtypeStruct(out_shape, out_dtype),
        scratch_shapes=[pltpu.VMEM((tm, tn), F32)],
        compiler_params=_params(("parallel", "parallel", "arbitrary")),
    )(*args)


def _tri(n, upper):
    r = lax.broadcasted_iota(jnp.int32, (n, n), 0)
    c = lax.broadcasted_iota(jnp.int32, (n, n), 1)
    return jnp.where((c >= r) if upper else (c <= r), 1.0, 0.0).astype(F32)


def _head_rows():
    r = lax.broadcasted_iota(jnp.int32, (SUBLANES, LANES), 0)
    c = lax.broadcasted_iota(jnp.int32, (SUBLANES, LANES), 1)
    return jnp.where(r == c, 1.0, 0.0).astype(F32)


def forget_cumsum(rest, bf, bsz, seq, f_blk):
    tc = _rows(seq, 512)
    nc = seq // tc

    def body(f_ref, b_ref, col_ref, row_ref, carry):
        c = pl.program_id(1)

        @pl.when(c == 0)
        def _():
            carry[...] = jnp.zeros_like(carry)

        z = f_ref[...] + b_ref[...]
        logf = jnp.minimum(z, 0.0) - jnp.log(1.0 + jnp.exp(-jnp.abs(z)))
        cum = _dot_exact(_tri(tc, False), logf, NN) + carry[0:1, :]
        col_ref[0] = cum
        row_ref[0] = _dot_exact(_head_rows(), cum, NT)
        carry[...] = jnp.broadcast_to(cum[tc - 1:tc, :], carry.shape)

    return pl.pallas_call(
        body, name="forget_cumsum", grid=(bsz, nc),
        in_specs=[pl.BlockSpec((tc, F_PAD), lambda b, c: (b * nc + c, f_blk)),
                  pl.BlockSpec((1, F_PAD), lambda b, c: (0, 0))],
        out_specs=[pl.BlockSpec((1, tc, LANES), lambda b, c: (b, c, 0)),
                   pl.BlockSpec((1, SUBLANES, tc), lambda b, c: (b, 0, c))],
        out_shape=[jax.ShapeDtypeStruct((bsz, seq, LANES), F32),
                   jax.ShapeDtypeStruct((bsz, SUBLANES, seq), F32)],
        scratch_shapes=[pltpu.VMEM((SUBLANES, LANES), F32)],
        compiler_params=_params(("parallel", "arbitrary")),
    )(rest, bf)


def forget_bwd(dcq, dcp, rest, bf, bsz, seq, f_blk):
    tc = _attn_tile(seq)
    nc = seq // tc

    def body(dq_ref, dc_ref, f_ref, b_ref, df_ref, db_ref, carry):
        b = pl.program_id(0)
        c = pl.program_id(1)

        @pl.when(c == 0)
        def _():
            carry[...] = jnp.zeros_like(carry)

        @pl.when((b == 0) & (c == 0))
        def _():
            db_ref[...] = jnp.zeros_like(db_ref)

        row = lax.broadcasted_iota(jnp.int32, (SUBLANES, tc), 0)
        heads = jnp.zeros((SUBLANES, tc), F32)
        dc = jnp.zeros((tc, LANES), F32)
        for p in range(HEAD_PAIRS):
            blk = dq_ref[0, p, 0]
            heads = heads + jnp.where(row == 2 * p, blk[0:1], 0.0) + jnp.where(row == 2 * p + 1, blk[1:2], 0.0)
            dc = dc + dc_ref[0, p]
        dc = dc + jnp.concatenate([heads, jnp.zeros((LANES - SUBLANES, tc), F32)], axis=0).T
        dlogf = _dot_exact(_tri(tc, True), dc, NN) + carry[0:1, :]
        carry[...] = jnp.broadcast_to(dlogf[0:1, :], carry.shape)
        z = f_ref[...] + b_ref[...]
        lane = lax.broadcasted_iota(jnp.int32, z.shape, 1)
        df = jnp.where(lane < ATTN_HEADS, dlogf * _sigmoid(-z), 0.0)
        df_ref[...] = df.astype(df_ref.dtype)
        db_ref[...] += _fold8(df)

    return pl.pallas_call(
        body, name="forget_bwd", grid=(bsz, nc),
        in_specs=[pl.BlockSpec((1, HEAD_PAIRS, 1, SUBLANES, tc), lambda b, c: (b, 0, nc - 1 - c, 0, 0)),
                  pl.BlockSpec((1, HEAD_PAIRS, tc, LANES), lambda b, c: (b, 0, nc - 1 - c, 0)),
                  pl.BlockSpec((tc, F_PAD), lambda b, c: (b * nc + nc - 1 - c, f_blk)),
                  pl.BlockSpec((1, F_PAD), lambda b, c: (0, 0))],
        out_specs=[pl.BlockSpec((tc, F_PAD), lambda b, c: (b * nc + nc - 1 - c, 0)),
                   pl.BlockSpec((SUBLANES, F_PAD), lambda b, c: (0, 0))],
        out_shape=[jax.ShapeDtypeStruct((bsz * seq, F_PAD), BF16),
                   jax.ShapeDtypeStruct((SUBLANES, F_PAD), F32)],
        scratch_shapes=[pltpu.VMEM((SUBLANES, LANES), F32)],
        compiler_params=_params(("arbitrary", "arbitrary")),
    )(dcq, dcp, rest, bf)


def _attn_tile(seq):
    return 512 if seq >= 2048 else 128


def _lane_head(shape, par):
    lane = lax.broadcasted_iota(jnp.int32, shape, len(shape) - 1)
    return (lane >= HEAD_DIM) if par else (lane < HEAD_DIM)


def _pick_lane(block, idx):
    lane = lax.broadcasted_iota(jnp.int32, block.shape, 1)
    return jnp.sum(jnp.where(lane == idx, block, 0.0), axis=1, keepdims=True)


def _pair_rows(lo_lane, hi_lane):
    r = lax.broadcasted_iota(jnp.int32, (SUBLANES, LANES), 0)
    c = lax.broadcasted_iota(jnp.int32, (SUBLANES, LANES), 1)
    if lo_lane is None:
        sel = ((r == 0) & (c < HEAD_DIM)) | ((r == 1) & (c >= HEAD_DIM))
    else:
        sel = ((r == 0) & (c == lo_lane)) | ((r == 1) & (c == hi_lane))
    return jnp.where(sel, 1.0, 0.0).astype(F32)


def _causal(s, transposed):
    r = lax.broadcasted_iota(jnp.int32, s.shape, 0)
    c = lax.broadcasted_iota(jnp.int32, s.shape, 1)
    return jnp.where((c >= r) if transposed else (r >= c), s, NEG_BIG)


def _causal_pairs(n, key_major):
    if key_major:
        pairs = [(i, j) for j in range(n) for i in range(j, n)]
    else:
        pairs = [(i, j) for i in range(n) for j in range(i + 1)]
    return (jnp.asarray([p[0] for p in pairs], jnp.int32), jnp.asarray([p[1] for p in pairs], jnp.int32))


def fox_fwd(qkv, cumrow, bsz, seq, carries=()):
    tq = _attn_tile(seq)
    nq = seq // tq
    scale = HEAD_DIM ** -0.5
    kb, vb = ATTN_WIDTH // LANES, 2 * ATTN_WIDTH // LANES
    qi_tab, kj_tab = _causal_pairs(nq, False)
    npairs = int(qi_tab.shape[0])
    sends, landings, n_sems, build = _carry_plan(carries)
    nc = len(sends)

    def body(qi_ref, kj_ref, q_ref, k_ref, v_ref, cr_ref, *rest):
        ins, (o_ref, lr_ref), lands = rest[:nc], rest[nc:nc + 2], rest[nc + 2:2 * nc + 2]
        (m_s, acc_s), sems = rest[2 * nc + 2:2 * nc + 4], rest[2 * nc + 4:]
        hp = pl.program_id(1)
        qi = qi_ref[pl.program_id(2)]
        kj = kj_ref[pl.program_id(2)]
        if nc:
            @pl.when((pl.program_id(0) == 0) & (hp == 0) & (pl.program_id(2) == 0))
            def _():
                for cp in build(ins, lands, sems):
                    cp.start()

        @pl.when(kj == 0)
        def _():
            m_s[...] = jnp.full_like(m_s, NEG_BIG)
            acc_s[...] = jnp.zeros_like(acc_s)

        def step(diag):
            q = q_ref[...]
            k = k_ref[...]
            v = v_ref[...]
            for par in range(2):
                sel = _lane_head(q.shape, par)
                qh = jnp.where(sel, q, 0.0) * scale
                s = _dot(qh.astype(BF16), k, NT) - cr_ref[0, pl.ds(2 * hp + par, 1), :]
                if diag:
                    s = _causal(s, False)
                m_prev = m_s[par]
                m_new = jnp.maximum(m_prev, jnp.max(s, axis=1, keepdims=True))
                p = jnp.exp(s - m_new).astype(BF16)
                acc_s[par] = jnp.exp(m_prev - m_new) * acc_s[par] + _dot(p, jnp.where(sel, v, 1.0).astype(BF16))
                m_s[par] = m_new

        @pl.when(kj < qi)
        def _():
            step(False)

        @pl.when(kj == qi)
        def _():
            step(True)
            lo = _lane_head((tq, LANES), 0)
            sums = [pltpu.roll(acc_s[par], HEAD_DIM, 1) for par in range(2)]
            out = jnp.where(lo, acc_s[0] / sums[0], acc_s[1] / sums[1])
            o_ref[...] = out.astype(o_ref.dtype)
            lse = jnp.where(lo, m_s[0] + jnp.log(sums[0]), m_s[1] + jnp.log(sums[1]))
            lr_ref[0, 0] = _dot_exact(_pair_rows(0, HEAD_DIM), lse, NT)

        if nc:
            @pl.when((pl.program_id(0) == bsz - 1) & (hp == HEAD_PAIRS - 1) & (pl.program_id(2) == npairs - 1))
            def _():
                for cp in build(ins, lands, sems):
                    cp.wait()

    sem_shapes = [pltpu.SemaphoreType.DMA((n_sems,)), pltpu.SemaphoreType.DMA((n_sems,))] if nc else []
    outs = pl.pallas_call(
        body, name="fox_fwd",
        grid_spec=pltpu.PrefetchScalarGridSpec(
            num_scalar_prefetch=2, grid=(bsz, HEAD_PAIRS, npairs),
            in_specs=[pl.BlockSpec((tq, LANES), lambda b, h, t, qi, kj: (b * nq + qi[t], h)),
                      pl.BlockSpec((tq, LANES), lambda b, h, t, qi, kj: (b * nq + kj[t], kb + h)),
                      pl.BlockSpec((tq, LANES), lambda b, h, t, qi, kj: (b * nq + kj[t], vb + h)),
                      pl.BlockSpec((1, SUBLANES, tq), lambda b, h, t, qi, kj: (b, 0, kj[t]))] + [ANY] * nc,
            out_specs=[pl.BlockSpec((tq, LANES), lambda b, h, t, qi, kj: (b * nq + qi[t], h)),
                       pl.BlockSpec((1, 1, SUBLANES, tq), lambda b, h, t, qi, kj: (b, h, 0, qi[t]))] + [ANY] * nc,
            scratch_shapes=[pltpu.VMEM((2, tq, 1), F32), pltpu.VMEM((2, tq, LANES), F32)] + sem_shapes),
        out_shape=[jax.ShapeDtypeStruct((bsz * seq, ATTN_WIDTH), BF16),
                   jax.ShapeDtypeStruct((bsz, HEAD_PAIRS, SUBLANES, seq), F32)] + landings,
        compiler_params=_params(("arbitrary", "arbitrary", "arbitrary")),
    )(qi_tab, kj_tab, qkv, qkv, qkv, cumrow, *sends)
    return outs[0], outs[1], list(outs[2:])


def fox_bwd_kernel(qkv, do, o, lser, cumcol, bsz, seq, carries=()):
    tk = _attn_tile(seq)
    nk = seq // tk
    scale = HEAD_DIM ** -0.5
    kb, vb = ATTN_WIDTH // LANES, 2 * ATTN_WIDTH // LANES
    qi_tab, kj_tab = _causal_pairs(nk, True)
    npairs = int(qi_tab.shape[0])
    sends, landings, n_sems, build = _carry_plan(carries)
    nc = len(sends)

    def body(qi_ref, kj_ref, q_ref, k_ref, v_ref, do_ref, o_ref, lr_ref, cc_ref, *rest):
        ins, lands = rest[:nc], rest[nc + 5:2 * nc + 5]
        dq_ref, dk_ref, dv_ref, dc_ref, dr_ref = rest[nc:nc + 5]
        (dq_s, dk_s, dv_s, dc_s, dr_s), sems = rest[2 * nc + 5:2 * nc + 10], rest[2 * nc + 10:]
        hp = pl.program_id(1)
        t = pl.program_id(2)
        qi = qi_ref[t]
        kj = kj_ref[t]
        if nc:
            @pl.when((pl.program_id(0) == 0) & (hp == 0) & (t == 0))
            def _():
                for cp in build(ins, lands, sems):
                    cp.start()

        @pl.when(t == 0)
        def _():
            dq_s[...] = jnp.zeros_like(dq_s)
            dr_s[...] = jnp.zeros_like(dr_s)

        @pl.when(qi == kj)
        def _():
            dk_s[...] = jnp.zeros_like(dk_s)
            dv_s[...] = jnp.zeros_like(dv_s)
            dc_s[...] = jnp.zeros_like(dc_s)

        def step(diag):
            q = q_ref[...]
            k = k_ref[...]
            v = v_ref[...]
            dov = do_ref[...]
            prod = dov.astype(F32) * o_ref[...].astype(F32)
            drow = _dot_exact(_pair_rows(None, None), prod, NT)
            lrow = lr_ref[0, 0]
            lane = lax.broadcasted_iota(jnp.int32, (tk, LANES), 1)
            for par in range(2):
                head = 2 * hp + par
                sel = _lane_head(k.shape, par)
                kh = (jnp.where(sel, k, 0.0) * scale).astype(BF16)
                st = _dot(kh, q, NT) - _pick_lane(cc_ref[0], head)
                if diag:
                    st = _causal(st, True)
                pt = jnp.exp(st - lrow[par:par + 1, :])
                vh = jnp.where(sel, v, 0.0)
                dpt = _dot(vh.astype(BF16), dov, NT)
                dst = pt * (dpt - drow[par:par + 1, :])
                dsb = dst.astype(BF16)
                dv_s[...] += jnp.where(sel, _dot(pt.astype(BF16), dov), 0.0)
                dk_s[...] += jnp.where(sel, _dot(dsb, q), 0.0)
                dq_s[qi] += _dot(dsb, kh, TN)
                dc_s[...] += jnp.where(lane == head, -jnp.sum(dst, axis=1, keepdims=True), 0.0)
                dr_s[qi, par:par + 1, :] += jnp.sum(dst, axis=0, keepdims=True)

        @pl.when(qi > kj)
        def _():
            step(False)

        @pl.when(qi == kj)
        def _():
            step(True)

        @pl.when(qi == nk - 1)
        def _():
            dk_ref[...] = (dk_s[...] * scale).astype(dk_ref.dtype)
            dv_ref[...] = dv_s[...].astype(dv_ref.dtype)
            dc_ref[0, 0] = dc_s[...]

        @pl.when(t == npairs - 1)
        def _():
            for i in range(nk):
                dq_ref[i * tk:(i + 1) * tk, :] = dq_s[i].astype(dq_ref.dtype)
            dr_ref[0, 0] = dr_s[...]

        if nc:
            @pl.when((pl.program_id(0) == bsz - 1) & (hp == HEAD_PAIRS - 1) & (t == npairs - 1))
            def _():
                for cp in build(ins, lands, sems):
                    cp.wait()

    sem_shapes = [pltpu.SemaphoreType.DMA((n_sems,)), pltpu.SemaphoreType.DMA((n_sems,))] if nc else []
    outs = pl.pallas_call(
        body, name="fox_bwd",
        grid_spec=pltpu.PrefetchScalarGridSpec(
            num_scalar_prefetch=2, grid=(bsz, HEAD_PAIRS, npairs),
            in_specs=[pl.BlockSpec((tk, LANES), lambda b, h, t, qi, kj: (b * nk + qi[t], h)),
                      pl.BlockSpec((tk, LANES), lambda b, h, t, qi, kj: (b * nk + kj[t], kb + h)),
                      pl.BlockSpec((tk, LANES), lambda b, h, t, qi, kj: (b * nk + kj[t], vb + h)),
                      pl.BlockSpec((tk, LANES), lambda b, h, t, qi, kj: (b * nk + qi[t], h)),
                      pl.BlockSpec((tk, LANES), lambda b, h, t, qi, kj: (b * nk + qi[t], h)),
                      pl.BlockSpec((1, 1, SUBLANES, tk), lambda b, h, t, qi, kj: (b, h, 0, qi[t])),
                      pl.BlockSpec((1, tk, LANES), lambda b, h, t, qi, kj: (b, kj[t], 0))] + [ANY] * nc,
            out_specs=[pl.BlockSpec((seq, LANES), lambda b, h, t, qi, kj: (b, h)),
                       pl.BlockSpec((tk, LANES), lambda b, h, t, qi, kj: (b * nk + kj[t], h)),
                       pl.BlockSpec((tk, LANES), lambda b, h, t, qi, kj: (b * nk + kj[t], h)),
                       pl.BlockSpec((1, 1, tk, LANES), lambda b, h, t, qi, kj: (b, h, kj[t], 0)),
                       pl.BlockSpec((1, 1, nk, SUBLANES, tk), lambda b, h, t, qi, kj: (b, h, 0, 0, 0))] + [ANY] * nc,
            scratch_shapes=[pltpu.VMEM((nk, tk, LANES), F32), pltpu.VMEM((tk, LANES), F32),
                            pltpu.VMEM((tk, LANES), F32), pltpu.VMEM((tk, LANES), F32),
                            pltpu.VMEM((nk, SUBLANES, tk), F32)] + sem_shapes),
        out_shape=[jax.ShapeDtypeStruct((bsz * seq, ATTN_WIDTH), BF16),
                   jax.ShapeDtypeStruct((bsz * seq, ATTN_WIDTH), BF16),
                   jax.ShapeDtypeStruct((bsz * seq, ATTN_WIDTH), BF16),
                   jax.ShapeDtypeStruct((bsz, HEAD_PAIRS, seq, LANES), F32),
                   jax.ShapeDtypeStruct((bsz, HEAD_PAIRS, nk, SUBLANES, tk), F32)] + landings,
        compiler_params=_params(("arbitrary", "arbitrary", "arbitrary")),
    )(qi_tab, kj_tab, qkv, qkv, qkv, do, o, lser, cumcol, *sends)
    return outs[:5], list(outs[5:])


def fox_bwd(qkv, do, o, lser, cumcol, rest, bf, bsz, seq, f_blk, carries=()):
    (dq, dk, dv, dcp, dcq), lands = fox_bwd_kernel(qkv, do, o, lser, cumcol, bsz, seq, carries)
    df, dbf = forget_bwd(dcq, dcp, rest, bf, bsz, seq, f_blk)
    return dq, dk, dv, df, dbf, lands


SCAN_STEPS = (1, 2, 4)
TAB_FWD = 0
TAB_BWD = 32
TAB_CARRY = 24
TAB_ROWS = 64


def _ssm_tile(seq):
    return 256 if seq >= 1024 else 64


def _scan_block(xr, xi, tab_ref, re, im, cr, ci, reverse):
    base = TAB_BWD if reverse else TAB_FWD
    for n, s in enumerate(SCAN_STEPS):
        ar = tab_ref[base + n * SUBLANES:base + (n + 1) * SUBLANES, re]
        ai = tab_ref[base + n * SUBLANES:base + (n + 1) * SUBLANES, im]
        shift = SUBLANES - s if reverse else s
        sr = pltpu.roll(xr, shift, 0)
        si = pltpu.roll(xi, shift, 0)
        xr, xi = xr + ar * sr - ai * si, xi + ar * si + ai * sr
    pr = tab_ref[base + TAB_CARRY:base + TAB_CARRY + SUBLANES, re]
    pi = tab_ref[base + TAB_CARRY:base + TAB_CARRY + SUBLANES, im]
    xr, xi = xr + pr * cr - pi * ci, xi + pr * ci + pi * cr
    return xr, xi


def ssm_fwd(rest, wb4, wc4, tabs, dskip, bsz, seq, u_blk):
    tt = _ssm_tile(seq)
    nt = seq // tt

    def body(u_ref, wb_ref, wc_ref, tab_ref, d_ref, y_ref, h_ref, carry):
        c = pl.program_id(1)

        @pl.when(c == 0)
        def _():
            carry[...] = jnp.zeros_like(carry)

        u = u_ref[...]
        ub = u.astype(BF16)
        for j in range(SSM_CHUNKS):
            h_ref[:, j * CHUNK_LANES:(j + 1) * CHUNK_LANES] = _dot(ub[:, j * LANES:(j + 1) * LANES], wb_ref[j])
        for j in range(SSM_CHUNKS):
            re = slice(j * CHUNK_LANES, j * CHUNK_LANES + CHUNK_STATES)
            im = slice(j * CHUNK_LANES + CHUNK_STATES, (j + 1) * CHUNK_LANES)

            def blk(bi, car):
                r0 = pl.multiple_of(bi * SUBLANES, SUBLANES)
                xr, xi = _scan_block(h_ref[pl.ds(r0, SUBLANES), re], h_ref[pl.ds(r0, SUBLANES), im],
                                     tab_ref, re, im, car[0], car[1], False)
                h_ref[pl.ds(r0, SUBLANES), re] = xr
                h_ref[pl.ds(r0, SUBLANES), im] = xi
                return xr[SUBLANES - 1:SUBLANES], xi[SUBLANES - 1:SUBLANES]

            cr, ci = lax.fori_loop(0, tt // SUBLANES, blk, (carry[0:1, re], carry[0:1, im]), unroll=2)
            carry[0:1, re] = cr
            carry[0:1, im] = ci
        for j in range(SSM_CHUNKS):
            hj = h_ref[:, j * CHUNK_LANES:(j + 1) * CHUNK_LANES].astype(BF16)
            cols = slice(j * LANES, (j + 1) * LANES)
            y_ref[:, cols] = _dot(hj, wc_ref[j]) + d_ref[:, cols] * u[:, cols]

    return pl.pallas_call(
        body, name="ssm_fwd", grid=(bsz, nt),
        in_specs=[pl.BlockSpec((tt, SSM_WIDTH), lambda b, c: (b * nt + c, u_blk)),
                  pl.BlockSpec((SSM_CHUNKS, LANES, CHUNK_LANES), lambda b, c: (0, 0, 0)),
                  pl.BlockSpec((SSM_CHUNKS, CHUNK_LANES, LANES), lambda b, c: (0, 0, 0)),
                  pl.BlockSpec((TAB_ROWS, STATE_LANES), lambda b, c: (0, 0)),
                  pl.BlockSpec((1, SSM_WIDTH), lambda b, c: (0, 0))],
        out_specs=[pl.BlockSpec((tt, SSM_WIDTH), lambda b, c: (b * nt + c, 0)),
                   pl.BlockSpec((tt, STATE_LANES), lambda b, c: (b * nt + c, 0))],
        out_shape=[jax.ShapeDtypeStruct((bsz * seq, SSM_WIDTH), F32),
                   jax.ShapeDtypeStruct((bsz * seq, STATE_LANES), F32)],
        scratch_shapes=[pltpu.VMEM((SUBLANES, STATE_LANES), F32)],
        compiler_params=_params(("parallel", "arbitrary")),
    )(rest, wb4, wc4, tabs, dskip)


def ssm_bwd(dys, rest, hs, wb4, wc4, tabs, dskip, bsz, seq, u_blk):
    tt = _ssm_tile(seq)
    nt = seq // tt
    nb = tt // SUBLANES

    def body(dy_ref, u_ref, h_ref, hp_ref, wb_ref, wc_ref, tab_ref, d_ref,
             du_ref, ga_ref, gwb_ref, gwc_ref, gd_ref, g_s, carry):
        b = pl.program_id(0)
        c = pl.program_id(1)

        @pl.when(c == 0)
        def _():
            carry[...] = jnp.zeros_like(carry)

        @pl.when((b == 0) & (c == 0))
        def _():
            ga_ref[...] = jnp.zeros_like(ga_ref)
            gwb_ref[...] = jnp.zeros_like(gwb_ref)
            gwc_ref[...] = jnp.zeros_like(gwc_ref)
            gd_ref[...] = jnp.zeros_like(gd_ref)

        dy = dy_ref[...].astype(F32)
        dyb = dy.astype(BF16)
        u = u_ref[...]
        ub = u.astype(BF16)
        first_chunk = c == nt - 1
        for j in range(SSM_CHUNKS):
            g_s[:, j * CHUNK_LANES:(j + 1) * CHUNK_LANES] = _dot(dyb[:, j * LANES:(j + 1) * LANES], wc_ref[j], NT)
        for j in range(SSM_CHUNKS):
            re = slice(j * CHUNK_LANES, j * CHUNK_LANES + CHUNK_STATES)
            im = slice(j * CHUNK_LANES + CHUNK_STATES, (j + 1) * CHUNK_LANES)
            row = lax.broadcasted_iota(jnp.int32, (SUBLANES, CHUNK_STATES), 0)

            def blk(n, car):
                bi = nb - 1 - n
                r0 = pl.multiple_of(bi * SUBLANES, SUBLANES)
                gr, gi = _scan_block(g_s[pl.ds(r0, SUBLANES), re], g_s[pl.ds(r0, SUBLANES), im],
                                     tab_ref, re, im, car[0], car[1], True)
                g_s[pl.ds(r0, SUBLANES), re] = gr
                g_s[pl.ds(r0, SUBLANES), im] = gi
                rp = pl.multiple_of(jnp.maximum(bi - 1, 0) * SUBLANES, SUBLANES)
                inside = bi > 0
                live = jnp.where(jnp.logical_or(inside, jnp.logical_not(first_chunk)), 1.0, 0.0)
                pr = jnp.where(inside, h_ref[pl.ds(rp, SUBLANES), re], hp_ref[:, re])[SUBLANES - 1:SUBLANES] * live
                pi = jnp.where(inside, h_ref[pl.ds(rp, SUBLANES), im], hp_ref[:, im])[SUBLANES - 1:SUBLANES] * live
                hr = jnp.where(row >= 1, pltpu.roll(h_ref[pl.ds(r0, SUBLANES), re], 1, 0), pr)
                hi = jnp.where(row >= 1, pltpu.roll(h_ref[pl.ds(r0, SUBLANES), im], 1, 0), pi)
                return (gr[0:1], gi[0:1], car[2] + gr * hr + gi * hi, car[3] + gi * hr - gr * hi)

            zero = jnp.zeros((SUBLANES, CHUNK_STATES), F32)
            cr, ci, sr, si = lax.fori_loop(0, nb, blk, (carry[0:1, re], carry[0:1, im], zero, zero), unroll=2)
            carry[0:1, re] = cr
            carry[0:1, im] = ci
            ga_ref[:, re] += sr
            ga_ref[:, im] += si
        for j in range(SSM_CHUNKS):
            cols = slice(j * LANES, (j + 1) * LANES)
            lanes = slice(j * CHUNK_LANES, (j + 1) * CHUNK_LANES)
            gj = g_s[:, lanes].astype(BF16)
            du_ref[:, cols] = (_dot(gj, wb_ref[j], NT) + d_ref[:, cols] * dy[:, cols]).astype(du_ref.dtype)
            gwb_ref[j] += _dot(ub[:, cols], gj, TN)
            gwc_ref[j] += _dot(h_ref[:, lanes].astype(BF16), dyb[:, cols], TN)
        gd_ref[...] += _fold8(dy * u)

    def prev_rows(b, c):
        chunk = nt - 1 - c
        return (jnp.maximum((b * nt + chunk) * nb - 1, 0), 0)

    return pl.pallas_call(
        body, name="ssm_bwd", grid=(bsz, nt),
        in_specs=[pl.BlockSpec((tt, SSM_WIDTH), lambda b, c: (b * nt + nt - 1 - c, 0)),
                  pl.BlockSpec((tt, SSM_WIDTH), lambda b, c: (b * nt + nt - 1 - c, u_blk)),
                  pl.BlockSpec((tt, STATE_LANES), lambda b, c: (b * nt + nt - 1 - c, 0)),
                  pl.BlockSpec((SUBLANES, STATE_LANES), prev_rows),
                  pl.BlockSpec((SSM_CHUNKS, LANES, CHUNK_LANES), lambda b, c: (0, 0, 0)),
                  pl.BlockSpec((SSM_CHUNKS, CHUNK_LANES, LANES), lambda b, c: (0, 0, 0)),
                  pl.BlockSpec((TAB_ROWS, STATE_LANES), lambda b, c: (0, 0)),
                  pl.BlockSpec((1, SSM_WIDTH), lambda b, c: (0, 0))],
        out_specs=[pl.BlockSpec((tt, SSM_WIDTH), lambda b, c: (b * nt + nt - 1 - c, 0)),
                   pl.BlockSpec((SUBLANES, STATE_LANES), lambda b, c: (0, 0)),
                   pl.BlockSpec((SSM_CHUNKS, LANES, CHUNK_LANES), lambda b, c: (0, 0, 0)),
                   pl.BlockSpec((SSM_CHUNKS, CHUNK_LANES, LANES), lambda b, c: (0, 0, 0)),
                   pl.BlockSpec((SUBLANES, SSM_WIDTH), lambda b, c: (0, 0))],
        out_shape=[jax.ShapeDtypeStruct((bsz * seq, SSM_WIDTH), BF16),
                   jax.ShapeDtypeStruct((SUBLANES, STATE_LANES), F32),
                   jax.ShapeDtypeStruct((SSM_CHUNKS, LANES, CHUNK_LANES), F32),
                   jax.ShapeDtypeStruct((SSM_CHUNKS, CHUNK_LANES, LANES), F32),
                   jax.ShapeDtypeStruct((SUBLANES, SSM_WIDTH), F32)],
        scratch_shapes=[pltpu.VMEM((tt, STATE_LANES), F32), pltpu.VMEM((SUBLANES, STATE_LANES), F32)],
        compiler_params=_params(("arbitrary", "arbitrary")),
    )(dys, rest, hs, hs, wb4, wc4, tabs, dskip)


def _gelu(v):
    t = jnp.tanh(GELU_C * (v + GELU_A * v * v * v))
    return 0.5 * v * (1.0 + t), t


def mix_fwd(ya, ys, rest, x0, wglu, bglu, wba, wbb, wout):
    t, d = x0.shape
    tm = _rows(t, 512)

    def body(ya_ref, ys_ref, ga_ref, gb_ref, x_ref, wg_ref, bg_ref, wa_ref, wb_ref, wo_ref,
             x1_ref, z_ref, pa_ref, pb_ref, yb_ref, yb2_ref, mx_ref):
        yb, _ = _gelu(ys_ref[...])
        ybb = yb.astype(BF16)
        z = _dot(ybb, wg_ref[...]) + bg_ref[...]
        yb2 = (yb * _sigmoid(z)).astype(BF16)
        pa = _dot(ya_ref[...], wa_ref[...])
        pb = _dot(yb2, wb_ref[...])
        mixed = (_sigmoid(ga_ref[...]) * pa + _sigmoid(gb_ref[...]) * pb).astype(BF16)
        x1_ref[...] = x_ref[...] + _dot(mixed, wo_ref[...])
        z_ref[...] = z.astype(z_ref.dtype)
        pa_ref[...] = pa.astype(pa_ref.dtype)
        pb_ref[...] = pb.astype(pb_ref.dtype)
        yb_ref[...] = ybb
        yb2_ref[...] = yb2
        mx_ref[...] = mixed

    row = lambda w: pl.BlockSpec((tm, w), lambda i: (i, 0))
    full = lambda a: pl.BlockSpec(a.shape, lambda i: (0,) * a.ndim)
    return pl.pallas_call(
        body, name="mix_fwd", grid=(t // tm,),
        in_specs=[row(ATTN_WIDTH), row(SSM_WIDTH),
                  pl.BlockSpec((tm, d), lambda i: (i, 0)), pl.BlockSpec((tm, d), lambda i: (i, 1)),
                  row(d), full(wglu), full(bglu), full(wba), full(wbb), full(wout)],
        out_specs=[row(d), row(SSM_WIDTH), row(d), row(d), row(SSM_WIDTH), row(SSM_WIDTH), row(d)],
        out_shape=[jax.ShapeDtypeStruct((t, d), F32), jax.ShapeDtypeStruct((t, SSM_WIDTH), BF16),
                   jax.ShapeDtypeStruct((t, d), BF16), jax.ShapeDtypeStruct((t, d), BF16),
                   jax.ShapeDtypeStruct((t, SSM_WIDTH), BF16), jax.ShapeDtypeStruct((t, SSM_WIDTH), BF16),
                   jax.ShapeDtypeStruct((t, d), BF16)],
        compiler_params=_params(("parallel",)),
    )(ya, ys, rest, rest, x0, wglu, bglu, wba, wbb, wout)


def mix_bwd(dx1, rest, pa, pb, z, ys, wglu, wba, wbb, wout):
    t, d = dx1.shape
    tm = _rows(t, 512)

    def body(dx_ref, ga_ref, gb_ref, pa_ref, pb_ref, z_ref, ys_ref, wg_ref, wa_ref, wb_ref, wo_ref,
             dya_ref, dys_ref, dg_ref, dpa_ref, dpb_ref, dz_ref, dbg_ref):
        @pl.when(pl.program_id(0) == 0)
        def _():
            dbg_ref[...] = jnp.zeros_like(dbg_ref)

        dmix = _dot(dx_ref[...].astype(BF16), wo_ref[...], NT)
        sa = _sigmoid(ga_ref[...])
        sb = _sigmoid(gb_ref[...])
        dpa = (dmix * sa).astype(BF16)
        dpb = (dmix * sb).astype(BF16)
        dg_ref[:, 0:d] = (dmix * pa_ref[...].astype(F32) * sa * (1.0 - sa)).astype(dg_ref.dtype)
        dg_ref[:, d:2 * d] = (dmix * pb_ref[...].astype(F32) * sb * (1.0 - sb)).astype(dg_ref.dtype)
        dpa_ref[...] = dpa
        dpb_ref[...] = dpb
        dya_ref[...] = _dot(dpa, wa_ref[...], NT).astype(dya_ref.dtype)
        dyb2 = _dot(dpb, wb_ref[...], NT)
        ys = ys_ref[...]
        yb, th = _gelu(ys)
        sg = _sigmoid(z_ref[...].astype(F32))
        dz = dyb2 * yb * sg * (1.0 - sg)
        dzb = dz.astype(BF16)
        dz_ref[...] = dzb
        dbg_ref[...] += _fold8(dz)
        dyb = dyb2 * sg + _dot(dzb, wg_ref[...], NT)
        dgelu = 0.5 * (1.0 + th) + 0.5 * ys * (1.0 - th * th) * GELU_C * (1.0 + 3.0 * GELU_A * ys * ys)
        dys_ref[...] = (dyb * dgelu).astype(dys_ref.dtype)

    row = lambda w: pl.BlockSpec((tm, w), lambda i: (i, 0))
    full = lambda a: pl.BlockSpec(a.shape, lambda i: (0,) * a.ndim)
    return pl.pallas_call(
        body, name="mix_bwd", grid=(t // tm,),
        in_specs=[row(d), pl.BlockSpec((tm, d), lambda i: (i, 0)), pl.BlockSpec((tm, d), lambda i: (i, 1)),
                  row(d), row(d), row(SSM_WIDTH), row(SSM_WIDTH), full(wglu), full(wba), full(wbb), full(wout)],
        out_specs=[row(ATTN_WIDTH), row(SSM_WIDTH), row(2 * d), row(d), row(d), row(SSM_WIDTH),
                   pl.BlockSpec((SUBLANES, SSM_WIDTH), lambda i: (0, 0))],
        out_shape=[jax.ShapeDtypeStruct((t, ATTN_WIDTH), BF16), jax.ShapeDtypeStruct((t, SSM_WIDTH), BF16),
                   jax.ShapeDtypeStruct((t, 2 * d), BF16), jax.ShapeDtypeStruct((t, d), BF16),
                   jax.ShapeDtypeStruct((t, d), BF16), jax.ShapeDtypeStruct((t, SSM_WIDTH), BF16),
                   jax.ShapeDtypeStruct((SUBLANES, SSM_WIDTH), F32)],
        compiler_params=_params(("arbitrary",)),
    )(dx1, rest, rest, pa, pb, z, ys, wglu, wba, wbb, wout)


def mlp_fwd(x1, g, wup, wdown):
    t, d = x1.shape
    ff = wup.shape[1]
    tm, tf = _rows(t, 1024), _pick(ff, 1024)
    nf = ff // tf

    def body(x_ref, g_ref, wu_ref, wd_ref, x2_ref, up_ref, h_s, acc_s):
        f = pl.program_id(1)

        @pl.when(f == 0)
        def _():
            xv = x_ref[...]
            h_s[...] = (xv * _rms_scale(xv) * g_ref[...]).astype(BF16)
            acc_s[...] = jnp.zeros_like(acc_s)

        up = _dot(h_s[...], wu_ref[...])
        up_ref[...] = up.astype(up_ref.dtype)
        act = jnp.square(jnp.maximum(up, 0.0)).astype(BF16)
        acc_s[...] += _dot(act, wd_ref[...])

        @pl.when(f == nf - 1)
        def _():
            x2_ref[...] = x_ref[...] + acc_s[...]

    return pl.pallas_call(
        body, name="mlp_fwd", grid=(t // tm, nf),
        in_specs=[pl.BlockSpec((tm, d), lambda i, f: (i, 0)), pl.BlockSpec((1, d), lambda i, f: (0, 0)),
                  pl.BlockSpec((d, tf), lambda i, f: (0, f)), pl.BlockSpec((tf, d), lambda i, f: (f, 0))],
        out_specs=[pl.BlockSpec((tm, d), lambda i, f: (i, 0)), pl.BlockSpec((tm, tf), lambda i, f: (i, f))],
        out_shape=[jax.ShapeDtypeStruct((t, d), F32), jax.ShapeDtypeStruct((t, ff), BF16)],
        scratch_shapes=[pltpu.VMEM((tm, d), BF16), pltpu.VMEM((tm, d), F32)],
        compiler_params=_params(("parallel", "arbitrary")),
    )(x1, g, wup, wdown)


def mlp_bwd(dx2, up, x1, g, wup, wdown, carries=()):
    t, d = x1.shape
    ff = wup.shape[1]
    tm, tf = _rows(t, 512), _pick(ff, 1024)
    nf = ff // tf
    ni = t // tm
    sends, landings, n_sems, build = _carry_plan(carries)
    nc = len(sends)

    def body(dx_ref, up_ref, x_ref, g_ref, wu_ref, wd_ref, *rest):
        ins, (dup_ref, dx1_ref, dg_ref), lands = rest[:nc], rest[nc:nc + 3], rest[nc + 3:2 * nc + 3]
        (dxb_s, acc_s), sems = rest[2 * nc + 3:2 * nc + 5], rest[2 * nc + 5:]
        i = pl.program_id(0)
        f = pl.program_id(1)
        if nc:
            @pl.when((i == 0) & (f == 0))
            def _():
                for cp in build(ins, lands, sems):
                    cp.start()

        @pl.when((i == 0) & (f == 0))
        def _():
            dg_ref[...] = jnp.zeros_like(dg_ref)

        @pl.when(f == 0)
        def _():
            dxb_s[...] = dx_ref[...].astype(BF16)
            acc_s[...] = jnp.zeros_like(acc_s)

        dact = _dot(dxb_s[...], wd_ref[...], NT)
        dup = (dact * 2.0 * jnp.maximum(up_ref[...].astype(F32), 0.0)).astype(BF16)
        dup_ref[...] = dup
        acc_s[...] += _dot(dup, wu_ref[...], NT)

        @pl.when(f == nf - 1)
        def _():
            dxn, dgain = _rms_bwd(x_ref[...], g_ref[...], acc_s[...])
            dx1_ref[...] = dx_ref[...] + dxn
            dg_ref[...] += _fold8(dgain)

        if nc:
            @pl.when((i == ni - 1) & (f == nf - 1))
            def _():
                for cp in build(ins, lands, sems):
                    cp.wait()

    sem_shapes = [pltpu.SemaphoreType.DMA((n_sems,)), pltpu.SemaphoreType.DMA((n_sems,))] if nc else []
    outs = pl.pallas_call(
        body, name="mlp_bwd", grid=(ni, nf),
        in_specs=[pl.BlockSpec((tm, d), lambda i, f: (i, 0)), pl.BlockSpec((tm, tf), lambda i, f: (i, f)),
                  pl.BlockSpec((tm, d), lambda i, f: (i, 0)), pl.BlockSpec((1, d), lambda i, f: (0, 0)),
                  pl.BlockSpec((d, tf), lambda i, f: (0, f)), pl.BlockSpec((tf, d), lambda i, f: (f, 0))] + [ANY] * nc,
        out_specs=[pl.BlockSpec((tm, tf), lambda i, f: (i, f)), pl.BlockSpec((tm, d), lambda i, f: (i, 0)),
                   pl.BlockSpec((SUBLANES, d), lambda i, f: (0, 0))] + [ANY] * nc,
        out_shape=[jax.ShapeDtypeStruct((t, ff), BF16), jax.ShapeDtypeStruct((t, d), F32),
                   jax.ShapeDtypeStruct((SUBLANES, d), F32)] + landings,
        scratch_shapes=[pltpu.VMEM((tm, d), BF16), pltpu.VMEM((tm, d), F32)] + sem_shapes,
        compiler_params=_params(("arbitrary", "arbitrary")),
    )(dx2, up, x1, g, wup, wdown, *sends)
    return outs[0], outs[1], outs[2], list(outs[3:])


def proj_bwd(dproj, wpad, x0, dx1, g, carries=()):
    t, d = x0.shape
    m = wpad.shape[1]
    tm = _rows(t, 512)
    ni = t // tm
    sends, landings, n_sems, build = _carry_plan(carries)
    nc = len(sends)

    def body(dp_ref, w_ref, x_ref, dx1_ref, g_ref, *rest):
        ins, (dx0_ref, dg_ref), lands, sems = rest[:nc], rest[nc:nc + 2], rest[nc + 2:2 * nc + 2], rest[2 * nc + 2:]

        @pl.when(pl.program_id(0) == 0)
        def _():
            dg_ref[...] = jnp.zeros_like(dg_ref)
            for cp in build(ins, lands, sems):
                cp.start()

        dh = _dot(dp_ref[...], w_ref[...], NT)
        dxn, dgain = _rms_bwd(x_ref[...], g_ref[...], dh)
        dx0_ref[...] = dx1_ref[...] + dxn
        dg_ref[...] += _fold8(dgain)

        if nc:
            @pl.when(pl.program_id(0) == ni - 1)
            def _():
                for cp in build(ins, lands, sems):
                    cp.wait()

    sem_shapes = [pltpu.SemaphoreType.DMA((n_sems,)), pltpu.SemaphoreType.DMA((n_sems,))] if nc else []
    outs = pl.pallas_call(
        body, name="proj_bwd", grid=(ni,),
        in_specs=[pl.BlockSpec((tm, m), lambda i: (i, 0)), pl.BlockSpec((d, m), lambda i: (0, 0)),
                  pl.BlockSpec((tm, d), lambda i: (i, 0)), pl.BlockSpec((tm, d), lambda i: (i, 0)),
                  pl.BlockSpec((1, d), lambda i: (0, 0))] + [ANY] * nc,
        out_specs=[pl.BlockSpec((tm, d), lambda i: (i, 0)), pl.BlockSpec((SUBLANES, d), lambda i: (0, 0))] + [ANY] * nc,
        out_shape=[jax.ShapeDtypeStruct((t, d), F32), jax.ShapeDtypeStruct((SUBLANES, d), F32)] + landings,
        scratch_shapes=sem_shapes,
        compiler_params=_params(("arbitrary",)),
    )(dproj, wpad, x0, dx1, g, *sends)
    return outs[0], outs[1], list(outs[2:])


def final_loss(x, g, target):
    t, d = x.shape
    tm = _rows(t, 512)

    def body(x_ref, g_ref, t_ref, dx_ref, ls_ref, dg_ref):
        @pl.when(pl.program_id(0) == 0)
        def _():
            ls_ref[...] = jnp.zeros_like(ls_ref)
            dg_ref[...] = jnp.zeros_like(dg_ref)

        xv = x_ref[...]
        gv = g_ref[...]
        err = xv * _rms_scale(xv) * gv - t_ref[...]
        ls_ref[...] += _fold8(err * err) * (0.5 / d)
        dxn, dgain = _rms_bwd(xv, gv, err * (1.0 / d))
        dx_ref[...] = dxn
        dg_ref[...] += _fold8(dgain)

    return pl.pallas_call(
        body, name="final_loss", grid=(t // tm,),
        in_specs=[pl.BlockSpec((tm, d), lambda i: (i, 0)), pl.BlockSpec((1, d), lambda i: (0, 0)),
                  pl.BlockSpec((tm, d), lambda i: (i, 0))],
        out_specs=[pl.BlockSpec((tm, d), lambda i: (i, 0)), pl.BlockSpec((SUBLANES, d), lambda i: (0, 0)),
                   pl.BlockSpec((SUBLANES, d), lambda i: (0, 0))],
        out_shape=[jax.ShapeDtypeStruct((t, d), F32), jax.ShapeDtypeStruct((SUBLANES, d), F32),
                   jax.ShapeDtypeStruct((SUBLANES, d), F32)],
        compiler_params=_params(("arbitrary",)),
    )(x, g, target)


def _peers():
    x, y, c = lax.axis_index("x"), lax.axis_index("y"), lax.axis_index("c")
    peers = []
    for m in range(1, 8):
        fx, fy, fc = (m >> 2) & 1, (m >> 1) & 1, m & 1
        peers.append((m, ((1 - x) if fx else x, (1 - y) if fy else y, (1 - c) if fc else c)))
    return x, y, c, peers


def _remote(src, dst, sems, k, peer):
    return pltpu.make_async_remote_copy(src_ref=src, dst_ref=dst, send_sem=sems[0].at[k], recv_sem=sems[1].at[k],
                                        device_id=peer, device_id_type=MESH)


def _gather_copies(w_refs, out_refs, sems, first):
    x, y, c, peers = _peers()
    chip = 2 * x + y
    cps = []
    for t, (w, o) in enumerate(zip(w_refs, out_refs)):
        half = w.shape[0] // 2
        rows = pl.ds(c * half, half)
        for m, peer in peers:
            if m >> 1:
                cps.append(_remote(w.at[rows], o.at[chip, rows], sems, first + 7 * t + m - 1, peer))
    return cps


def _reduce_copies(g_refs, out_refs, sems, first):
    x, y, c, peers = _peers()
    me = 4 * x + 2 * y + c
    cps = []
    for t, (g, o) in enumerate(zip(g_refs, out_refs)):
        for m, (px, py, pc) in peers:
            cps.append(_remote(g.at[2 * px + py], o.at[me], sems, first + 7 * t + m - 1, (px, py, pc)))
    return cps


def _all_copies(s_refs, out_refs, sems, first):
    x, y, c, peers = _peers()
    me = 4 * x + 2 * y + c
    return [_remote(s, o.at[me], sems, first + 7 * t + m - 1, peer)
            for t, (s, o) in enumerate(zip(s_refs, out_refs)) for m, peer in peers]


EXCHANGES = {"gather": (_gather_copies, lambda a: (4,) + a.shape),
             "reduce": (_reduce_copies, lambda a: (8,) + a.shape[1:]),
             "all": (_all_copies, lambda a: (8,) + a.shape)}


def _carry_plan(carries):
    inputs, shapes, spans = [], [], []
    for kind, arrs in carries:
        for a in arrs:
            inputs.append(a)
            shapes.append(jax.ShapeDtypeStruct(EXCHANGES[kind][1](a), a.dtype))
        spans.append((kind, len(arrs)))

    def build(in_refs, out_refs, sems):
        cps, pos = [], 0
        for kind, cnt in spans:
            cps += EXCHANGES[kind][0](in_refs[pos:pos + cnt], out_refs[pos:pos + cnt], sems, 7 * pos)
            pos += cnt
        return cps

    return inputs, shapes, 7 * len(inputs), build


def exchange(carries, name):
    inputs, shapes, n_sems, build = _carry_plan(carries)
    n = len(inputs)

    def body(*refs):
        cps = build(refs[:n], refs[n:2 * n], refs[2 * n:])
        for cp in cps:
            cp.start()
        for cp in cps:
            cp.wait()

    return pl.pallas_call(
        body, name=name, in_specs=[ANY] * n, out_specs=[ANY] * n, out_shape=shapes,
        scratch_shapes=[pltpu.SemaphoreType.DMA((n_sems,)), pltpu.SemaphoreType.DMA((n_sems,))],
    )(*inputs)


def _adamw_math(w, g, m, v):
    m = ADAM_B1 * m + (1.0 - ADAM_B1) * g
    v = ADAM_B2 * v + (1.0 - ADAM_B2) * (g * g)
    m_hat = m / (1.0 - ADAM_B1 ** ADAM_STEP)
    v_hat = v / (1.0 - ADAM_B2 ** ADAM_STEP)
    delta = -ADAM_LR * (m_hat / (jnp.sqrt(v_hat) + ADAM_EPS) + ADAM_WD * w)
    return delta, m, v


def adamw_layer(l, w, m, v, parts, own, device, chip, bufs, name):
    _, r, cdim = w.shape
    tr = _rows(r, PACK_ROW_TILE)

    def body(dev_ref, chip_ref, w_ref, m_ref, v_ref, p_ref, o_ref, *rest):
        g_ref, d_ref, nm_ref, nv_ref = rest[-4:]
        g = None
        for dev in range(8):
            part = jnp.where(dev_ref[0] == dev, o_ref[0], p_ref[dev]).astype(F32)
            g = part if g is None else g + part
        d, nm, nv = _adamw_math(w_ref[0], g, m_ref[0], v_ref[0])
        g_ref[0] = g
        d_ref[0] = d
        nm_ref[0] = nm
        nv_ref[0] = nv

    lay = pl.BlockSpec((1, tr, cdim), lambda i, dev_ref, chip_ref: (l, i, 0))
    in_specs = [lay, lay, lay, pl.BlockSpec((8, tr, cdim), lambda i, dev_ref, chip_ref: (0, i, 0)),
                pl.BlockSpec((1, tr, cdim), lambda i, dev_ref, chip_ref: (chip_ref[0], i, 0))]
    args = [device, chip, w, m, v, parts, own]
    aliases = {}
    if bufs is not None:
        in_specs += [ANY] * 4
        aliases = {len(args) + k: k for k in range(4)}
        args += list(bufs)
    return pl.pallas_call(
        body, name=name,
        grid_spec=pltpu.PrefetchScalarGridSpec(num_scalar_prefetch=2, grid=(r // tr,), in_specs=in_specs,
                                               out_specs=[lay] * 4),
        out_shape=[jax.ShapeDtypeStruct(w.shape, F32)] * 4,
        input_output_aliases=aliases,
        compiler_params=_params(("parallel",)),
    )(*args)


def sum_and_adamw(parts, own, me, w, m, v):
    _, r, cdim = parts.shape
    tr = _rows(r, PACK_ROW_TILE)

    def body(me_ref, p_ref, o_ref, w_ref, m_ref, v_ref, g_ref, d_ref, nm_ref, nv_ref):
        part = lambda k: jnp.where(me_ref[0] == k, o_ref[...], p_ref[k]).astype(F32)
        g = part(0)
        for k in range(1, 8):
            g = g + part(k)
        d, nm, nv = _adamw_math(w_ref[...], g, m_ref[...], v_ref[...])
        g_ref[...] = g
        d_ref[...] = d
        nm_ref[...] = nm
        nv_ref[...] = nv

    spec = pl.BlockSpec((tr, cdim), lambda i, me_ref: (i, 0))
    return pl.pallas_call(
        body, name="sum_and_adamw",
        grid_spec=pltpu.PrefetchScalarGridSpec(
            num_scalar_prefetch=1, grid=(r // tr,),
            in_specs=[pl.BlockSpec((8, tr, cdim), lambda i, me_ref: (0, i, 0)), spec, spec, spec, spec],
            out_specs=[spec] * 4),
        out_shape=[jax.ShapeDtypeStruct((r, cdim), F32)] * 4,
        compiler_params=_params(("parallel",)),
    )(me, parts, own, w, m, v)


SHARDED = ("w_in", "w_glu", "w_branch_a", "w_branch_b", "w_out", "w_mlp_up", "w_mlp_down")
SHARD_AXIS = {"w_in": 2, "w_glu": 1, "w_branch_a": 2, "w_branch_b": 2, "w_out": 1, "w_mlp_up": 2, "w_mlp_down": 1}
SMALL = ("norm_mix", "b_forget", "ssm_lambda_re", "ssm_lambda_im", "ssm_log_dt", "ssm_b_re", "ssm_b_im",
         "ssm_c_re", "ssm_c_im", "ssm_d", "b_glu", "norm_mlp", "norm_final")
SMALL_WIDE = ("ssm_b_re", "ssm_b_im", "ssm_c_re", "ssm_c_im")


def pack_flat(arrs):
    flat = jnp.concatenate([a.reshape(-1).astype(F32) for a in arrs])
    unit = PACK_COLS * PACK_ROW_ALIGN
    rows = (flat.shape[0] + unit - 1) // unit * PACK_ROW_ALIGN
    return jnp.pad(flat, (0, rows * PACK_COLS - flat.shape[0])).reshape(rows, PACK_COLS)


def unpack_flat(packed, shapes):
    flat = packed.reshape(-1)
    out, off = [], 0
    for s in shapes:
        n = math.prod(s)
        out.append(flat[off:off + n].reshape(tuple(s)))
        off += n
    return out


def _discretise(lam_re, lam_im, log_dt, b_re, b_im):
    dt = jnp.exp(log_dt)[:, None]
    mag = jnp.exp(lam_re * dt)
    ar = mag * jnp.cos(lam_im * dt)
    ai = mag * jnp.sin(lam_im * dt)
    den = lam_re * lam_re + lam_im * lam_im
    cr = ((ar - 1.0) * lam_re + ai * lam_im) / den
    ci = (ai * lam_re - (ar - 1.0) * lam_im) / den
    bbr = cr[:, :, None] * b_re - ci[:, :, None] * b_im
    bbi = cr[:, :, None] * b_im + ci[:, :, None] * b_re
    return ar, ai, bbr, bbi


def _ssm_inputs(ar, ai, bbr, bbi, c_re, c_im):
    pr, pi = [ar], [ai]
    for _ in range(SUBLANES - 1):
        pr, pi = pr + [pr[-1] * ar - pi[-1] * ai], pi + [pr[-1] * ai + pi[-1] * ar]
    powers = jnp.stack([jnp.stack(pr), jnp.stack(pi)], axis=1)
    powers = powers.reshape(SUBLANES, 2, SSM_CHUNKS, CHUNK_STATES).transpose(0, 2, 1, 3).reshape(SUBLANES, STATE_LANES)
    conj = powers * jnp.tile(jnp.repeat(jnp.asarray([1.0, -1.0], F32), CHUNK_STATES), SSM_CHUNKS)
    idx = jnp.arange(SUBLANES)[:, None]
    tabs = jnp.concatenate(
        [jnp.where(idx >= s, powers[s - 1][None, :], 0.0) for s in SCAN_STEPS] + [powers]
        + [jnp.where(idx < SUBLANES - s, conj[s - 1][None, :], 0.0) for s in SCAN_STEPS] + [conj[::-1]], axis=0)
    eye = jnp.eye(CHUNK_GROUPS, dtype=F32)
    wb = jnp.stack([bbr, bbi]).reshape(2, SSM_CHUNKS, CHUNK_GROUPS, SSM_STATE, SSM_GROUP_CH).transpose(1, 2, 4, 0, 3)
    wb4 = (wb[:, :, :, :, None, :] * eye[None, :, None, None, :, None]).reshape(SSM_CHUNKS, LANES, CHUNK_LANES)
    wc = jnp.stack([c_re, -c_im]).reshape(2, SSM_CHUNKS, CHUNK_GROUPS, SSM_GROUP_CH, SSM_STATE).transpose(1, 0, 4, 2, 3)
    wc4 = (wc[:, :, None, :, :, :] * eye[None, None, :, None, :, None]).reshape(SSM_CHUNKS, CHUNK_LANES, LANES)
    return tabs, wb4.astype(BF16), wc4.astype(BF16)


def _ssm_param_grads(ga8, gwb, gwc):
    eye = jnp.eye(CHUNK_GROUPS, dtype=F32)
    ga = jnp.sum(ga8, axis=0).reshape(SSM_CHUNKS, 2, CHUNK_STATES)
    gar = ga[:, 0].reshape(SSM_GROUPS, SSM_STATE)
    gai = ga[:, 1].reshape(SSM_GROUPS, SSM_STATE)

    def from_wb(g):
        t = g.reshape(SSM_CHUNKS, CHUNK_GROUPS, SSM_GROUP_CH, CHUNK_GROUPS, SSM_STATE)
        return jnp.einsum("jgchp,gh->jgpc", t, eye).reshape(SSM_GROUPS, SSM_STATE, SSM_GROUP_CH)

    def from_wc(g):
        t = g.reshape(SSM_CHUNKS, CHUNK_GROUPS, SSM_STATE, CHUNK_GROUPS, SSM_GROUP_CH)
        return jnp.einsum("jhpgc,gh->jgcp", t, eye).reshape(SSM_GROUPS, SSM_GROUP_CH, SSM_STATE)

    return (gar, gai, from_wb(gwb[:, :, :CHUNK_STATES]), from_wb(gwb[:, :, CHUNK_STATES:]),
            from_wc(gwc[:, :CHUNK_STATES]), -from_wc(gwc[:, CHUNK_STATES:]))


def kernel(x, norm_mix, w_in, b_forget, ssm_lambda_re, ssm_lambda_im, ssm_log_dt, ssm_b_re, ssm_b_im, ssm_c_re, ssm_c_im, ssm_d, w_glu, b_glu, w_branch_a, w_branch_b, w_out, norm_mlp, w_mlp_up, w_mlp_down, norm_final, loss_target, m_norm_mix, m_w_in, m_b_forget, m_ssm_lambda_re, m_ssm_lambda_im, m_ssm_log_dt, m_ssm_b_re, m_ssm_b_im, m_ssm_c_re, m_ssm_c_im, m_ssm_d, m_w_glu, m_b_glu, m_w_branch_a, m_w_branch_b, m_w_out, m_norm_mlp, m_w_mlp_up, m_w_mlp_down, m_norm_final, v_norm_mix, v_w_in, v_b_forget, v_ssm_lambda_re, v_ssm_lambda_im, v_ssm_log_dt, v_ssm_b_re, v_ssm_b_im, v_ssm_c_re, v_ssm_c_im, v_ssm_d, v_w_glu, v_b_glu, v_w_branch_a, v_w_branch_b, v_w_out, v_norm_mlp, v_w_mlp_up, v_w_mlp_down, v_norm_final):
    args = dict(locals())
    bsz, seq, d = x.shape
    nl = norm_mix.shape[0]
    tokens = bsz * seq
    aw, sw = ATTN_WIDTH, SSM_WIDTH
    chip = (2 * lax.axis_index("x") + lax.axis_index("y")).astype(jnp.int32)
    chip_id = chip.reshape(1)
    device_id = (2 * chip + lax.axis_index("c").astype(jnp.int32)).reshape(1)

    own = {n: args[n].astype(BF16) for n in SHARDED}
    late = [n for n in SHARDED if n != "w_in"]
    o_f, o_u, o_ga, o_gb = 3 * aw, 3 * aw + ATTN_HEADS, 3 * aw + ATTN_HEADS + sw, 3 * aw + ATTN_HEADS + sw + d
    u_blk = 2 * d // sw
    f_blk = (2 * d + sw) // F_PAD
    bf_pad = jnp.pad(b_forget, ((0, 0), (0, F_PAD - ATTN_HEADS)))

    def assemble(l, names, gathered):
        return {n: jnp.concatenate([jnp.where(chip == k, own[n][l], g[k]) for k in range(4)],
                                   axis=SHARD_AXIS[n] - 1) for n, g in zip(names, gathered)}

    def split_w_in(win):
        w = {"w_qkv": win[:, :o_f],
             "w_rest": jnp.concatenate([win[:, o_ga:o_gb], win[:, o_gb:], win[:, o_u:o_ga],
                                        jnp.pad(win[:, o_f:o_u], ((0, 0), (0, F_PAD - ATTN_HEADS)))], axis=1)}
        w["w_pad"] = jnp.concatenate([w["w_qkv"], w["w_rest"]], axis=1)
        return w

    disc = [jax.vjp(_discretise, ssm_lambda_re[l], ssm_lambda_im[l], ssm_log_dt[l], ssm_b_re[l], ssm_b_im[l])
            for l in range(nl)]

    g_in = exchange([("gather", [own["w_in"][0]])], "gather_first")
    xs = x.reshape(tokens, d)
    saved, weights = [], []
    for l in range(nl):
        w = split_w_in(assemble(l, ["w_in"], g_in)["w_in"])
        g1 = norm_mix[l].reshape(1, d)
        qkv = norm_matmul(xs, g1, w["w_qkv"], BF16, "proj_qkv")
        rest = norm_matmul(xs, g1, w["w_rest"], F32, "proj_rest")
        cumcol, cumrow = forget_cumsum(rest, bf_pad[l:l + 1], bsz, seq, f_blk)
        carries = [("gather", [own[n][l] for n in late] + ([own["w_in"][l + 1]] if l + 1 < nl else []))]
        ya, lser, lands = fox_fwd(qkv, cumrow, bsz, seq, carries)
        w.update(assemble(l, late, lands))
        g_in = lands[len(late):]
        tabs, wb4, wc4 = _ssm_inputs(*disc[l][0], ssm_c_re[l], ssm_c_im[l])
        dskip = ssm_d[l].reshape(1, sw)
        ys, hs = ssm_fwd(rest, wb4, wc4, tabs, dskip, bsz, seq, u_blk)
        x1, z, pa, pb, yb, yb2, mixed = mix_fwd(ya, ys, rest, xs, w["w_glu"], b_glu[l].reshape(1, sw),
                                                 w["w_branch_a"], w["w_branch_b"], w["w_out"])
        x2, up = mlp_fwd(x1, norm_mlp[l].reshape(1, d), w["w_mlp_up"], w["w_mlp_down"])
        saved.append(dict(x0=xs, qkv=qkv, rest=rest, cumcol=cumcol, ya=ya, lser=lser,
                          tabs=tabs, wb4=wb4, wc4=wc4, dskip=dskip, ys=ys, hs=hs, x1=x1, z=z, pa=pa, pb=pb,
                          yb=yb, yb2=yb2, mixed=mixed, up=up))
        weights.append(w)
        xs = x2
    dx, loss_rows, dgf_rows = final_loss(xs, norm_final.reshape(1, d), loss_target.reshape(tokens, d))
    loss = lax.psum(jnp.sum(loss_rows), ("x", "y", "c"))

    early = [n for n in SHARDED if n != "w_in"]
    big = {n: [None] * nl for n in SHARDED}
    parts = {n: [None] * nl for n in SHARDED}
    small = {n: [None] * nl for n in SMALL if n != "norm_final"}
    for l in reversed(range(nl)):
        s, w = saved[l], weights[l]
        g2 = norm_mlp[l].reshape(1, d)
        dup, dx1, dg2, lands = mlp_bwd(dx, s["up"], s["x1"], g2, w["w_mlp_up"], w["w_mlp_down"],
                                       [("reduce", [big["w_in"][l + 1]])] if l + 1 < nl else [])
        if l + 1 < nl:
            parts["w_in"][l + 1] = lands[0]
        big["w_mlp_down"][l] = matmul_tn(s["up"], dx, "grad_w_mlp_down", a_kind="relu2", shard_axis=0, out_dtype=BF16)
        big["w_mlp_up"][l] = matmul_tn(s["x1"], dup, "grad_w_mlp_up", a_kind="norm", gain=g2, shard_axis=1,
                                       out_dtype=BF16)
        small["norm_mlp"][l] = jnp.sum(dg2, axis=0)
        dya, dys, dgab, dpa, dpb, dz, dbg = mix_bwd(dx1, s["rest"], s["pa"], s["pb"], s["z"], s["ys"],
                                                    w["w_glu"], w["w_branch_a"], w["w_branch_b"], w["w_out"])
        big["w_out"][l] = matmul_tn(s["mixed"], dx1, "grad_w_out", shard_axis=0, out_dtype=BF16)
        big["w_branch_a"][l] = matmul_tn(s["ya"], dpa, "grad_w_branch_a", shard_axis=1, out_dtype=BF16)
        big["w_branch_b"][l] = matmul_tn(s["yb2"], dpb, "grad_w_branch_b", shard_axis=1, out_dtype=BF16)
        big["w_glu"][l] = matmul_tn(s["yb"], dz, "grad_w_glu", shard_axis=0, out_dtype=BF16)
        small["b_glu"][l] = jnp.sum(dbg, axis=0)
        du, ga8, gwb, gwc, gd8 = ssm_bwd(dys, s["rest"], s["hs"], s["wb4"], s["wc4"], s["tabs"], s["dskip"],
                                         bsz, seq, u_blk)
        gar, gai, gbbr, gbbi, gcr, gci = _ssm_param_grads(ga8, gwb, gwc)
        glr, gli, gdt, gbr, gbi = disc[l][1]((gar, gai, gbbr, gbbi))
        small["ssm_lambda_re"][l], small["ssm_lambda_im"][l], small["ssm_log_dt"][l] = glr, gli, gdt
        small["ssm_b_re"][l], small["ssm_b_im"][l] = gbr, gbi
        small["ssm_c_re"][l], small["ssm_c_im"][l] = gcr, gci
        small["ssm_d"][l] = jnp.sum(gd8, axis=0)
        dq, dk, dv, df, dbf, lands = fox_bwd(s["qkv"], dya, s["ya"], s["lser"], s["cumcol"], s["rest"],
                                             bf_pad[l:l + 1], bsz, seq, f_blk, [("reduce", [big[n][l] for n in early])])
        for n, p in zip(early, lands):
            parts[n][l] = p
        small["b_forget"][l] = jnp.sum(dbf, axis=0)[:ATTN_HEADS]
        dproj = jnp.concatenate([dq, dk, dv, dgab, du, df], axis=1)
        g1 = norm_mix[l].reshape(1, d)
        dwp = matmul_tn(s["x0"], dproj, "grad_w_in", a_kind="norm", gain=g1)
        big["w_in"][l] = jnp.stack(jnp.split(jnp.concatenate(
            [dwp[:, :o_f], dwp[:, o_f + 2 * d + sw:o_f + 2 * d + sw + ATTN_HEADS],
             dwp[:, o_f + 2 * d:o_f + 2 * d + sw], dwp[:, o_f:o_f + 2 * d]], axis=1), 4, axis=1)).astype(BF16)
        dx, dg1, lands = proj_bwd(dproj, w["w_pad"], s["x0"], dx1, g1,
                                  [("reduce", [big["w_in"][0]])] if l == 0 else [])
        if l == 0:
            parts["w_in"][0] = lands[0]
        small["norm_mix"][l] = jnp.sum(dg1, axis=0)
    grad_x = dx.reshape(bsz, seq, d)

    small_g = {n: jnp.stack(small[n]) if n != "norm_final" else jnp.sum(dgf_rows, axis=0) for n in SMALL}
    groups = {BF16: SMALL_WIDE, F32: [n for n in SMALL if n not in SMALL_WIDE]}
    packed = {dt: pack_flat([small_g[n] for n in names]).astype(dt) for dt, names in groups.items()}
    small_parts = exchange([("all", [packed[dt] for dt in groups])], "exchange_last")

    out_g, out_d, out_m, out_v = {}, {}, {}, {}
    for n in SHARDED:
        bufs = None
        for l in range(nl):
            bufs = adamw_layer(l, args[n], args["m_" + n], args["v_" + n], parts[n][l], big[n][l], device_id, chip_id,
                               bufs, "adamw_" + n)
        out_g[n], out_d[n], out_m[n], out_v[n] = bufs

    for (dt, names), landed in zip(groups.items(), small_parts):
        results = sum_and_adamw(landed, packed[dt], device_id, pack_flat([args[n] for n in names]),
                                pack_flat([args["m_" + n] for n in names]), pack_flat([args["v_" + n] for n in names]))
        for res, flat in zip((out_g, out_d, out_m, out_v), results):
            res.update(zip(names, unpack_flat(flat, [args[n].shape for n in names])))

    order = ("norm_mix", "w_in", "b_forget", "ssm_lambda_re", "ssm_lambda_im", "ssm_log_dt", "ssm_b_re",
             "ssm_b_im", "ssm_c_re", "ssm_c_im", "ssm_d", "w_glu", "b_glu", "w_branch_a", "w_branch_b", "w_out",
             "norm_mlp", "w_mlp_up", "w_mlp_down", "norm_final")
    return (loss, grad_x, *[out_g[n] for n in order], *[out_d[n] for n in order],
            *[out_m[n] for n in order], *[out_v[n] for n in order])
```

```python
import math

import jax
import jax.numpy as jnp
from jax import lax
from jax.experimental import pallas as pl
from jax.experimental.pallas import tpu as pltpu

F32 = jnp.float32
BF16 = jnp.bfloat16
MESH = pl.DeviceIdType.MESH
ANY = pl.BlockSpec(memory_space=pl.ANY)

ATTN_HEADS = 8
HEAD_DIM = 64
ATTN_WIDTH = ATTN_HEADS * HEAD_DIM
HEAD_PAIRS = ATTN_HEADS // 2
SSM_GROUPS = 32
SSM_GROUP_CH = 16
SSM_STATE = 64
SSM_WIDTH = SSM_GROUPS * SSM_GROUP_CH
LANES = 128
SUBLANES = 8
SSM_CHUNKS = SSM_WIDTH // LANES
CHUNK_GROUPS = SSM_GROUPS // SSM_CHUNKS
CHUNK_STATES = CHUNK_GROUPS * SSM_STATE
CHUNK_LANES = 2 * CHUNK_STATES
STATE_LANES = SSM_CHUNKS * CHUNK_LANES
F_PAD = LANES
RMS_EPS = 1e-6
ADAM_LR = 0.001
ADAM_B1 = 0.9
ADAM_B2 = 0.999
ADAM_EPS = 1e-08
ADAM_WD = 0.01
ADAM_STEP = 10
PACK_COLS = 1024
PACK_ROW_TILE = 256
PACK_ROW_ALIGN = 32
VMEM_LIMIT = 52 * 1024 * 1024
NEG_BIG = -1e30
GELU_C = math.sqrt(2.0 / math.pi)
GELU_A = 0.044715

NN = (((1,), (0,)), ((), ()))
NT = (((1,), (1,)), ((), ()))
TN = (((0,), (0,)), ((), ()))


def _pick(n, pref):
    if n <= pref:
        return n
    best = LANES
    for t in range(LANES, pref + 1, LANES):
        if n % t == 0:
            best = t
    assert n % best == 0, (n, pref)
    return best


def _rows(n, pref):
    t = min(n, pref)
    while n % t:
        t //= 2
    assert t % 16 == 0 or t == n, (n, pref)
    return t


def _params(sem):
    return pltpu.CompilerParams(dimension_semantics=sem, vmem_limit_bytes=VMEM_LIMIT)


def _fold8(v):
    r, c = v.shape
    return jnp.sum(v.reshape(r // SUBLANES, SUBLANES, c), axis=0)


def _dot(a, b, dims=None):
    if dims is None:
        return jnp.dot(a, b, preferred_element_type=F32)
    return lax.dot_general(a, b, dims, preferred_element_type=F32)


def _dot_exact(a, b, dims):
    return lax.dot_general(a, b, dims, preferred_element_type=F32, precision=lax.Precision.HIGHEST)


def _sigmoid(v):
    return 1.0 / (1.0 + jnp.exp(-v))


def _rms_scale(x):
    return lax.rsqrt(jnp.mean(x * x, axis=-1, keepdims=True) + RMS_EPS)


def _rms_bwd(x, g, dh):
    r = _rms_scale(x)
    xn = x * r
    dxn = dh * g
    dx = r * (dxn - xn * jnp.mean(dxn * xn, axis=-1, keepdims=True))
    return dx, dh * xn


def norm_matmul(x, g, w, out_dtype, name):
    t, d = x.shape
    m = w.shape[1]
    tm, tn = _rows(t, 1024), _pick(m, 1024)

    def body(x_ref, g_ref, w_ref, o_ref):
        xv = x_ref[...]
        h = (xv * _rms_scale(xv) * g_ref[...]).astype(BF16)
        o_ref[...] = _dot(h, w_ref[...]).astype(o_ref.dtype)

    return pl.pallas_call(
        body, name=name, grid=(t // tm, m // tn),
        in_specs=[pl.BlockSpec((tm, d), lambda i, j: (i, 0)),
                  pl.BlockSpec((1, d), lambda i, j: (0, 0)),
                  pl.BlockSpec((d, tn), lambda i, j: (0, j))],
        out_specs=pl.BlockSpec((tm, tn), lambda i, j: (i, j)),
        out_shape=jax.ShapeDtypeStruct((t, m), out_dtype),
        compiler_params=_params(("parallel", "arbitrary")),
    )(x, g, w)


def matmul_tn(a, b, name, a_kind="plain", gain=None, shard_axis=None, out_dtype=F32, tm_pref=1024, tn_pref=1536,
              tk_pref=1024):
    t, ma = a.shape
    nb = b.shape[1]
    tm = ma if a_kind == "norm" else _pick(ma, tm_pref)
    tn = _pick(nb, tn_pref)
    tk = _rows(t, tk_pref)
    nk = t // tk
    if shard_axis == 0:
        per = tm * 4 // ma
        assert per >= 1 and (ma // 4) * per == tm, (ma, tm)
        out_shape, out_spec = (4, ma // 4, nb), pl.BlockSpec((per, ma // 4, tn), lambda i, j, k: (i, 0, j))
    elif shard_axis == 1:
        per = tn * 4 // nb
        assert per >= 1 and (nb // 4) * per == tn, (nb, tn)
        out_shape, out_spec = (4, ma, nb // 4), pl.BlockSpec((per, tm, nb // 4), lambda i, j, k: (j, i, 0))
    else:
        out_shape, out_spec = (ma, nb), pl.BlockSpec((tm, tn), lambda i, j, k: (i, j))

    def body(*refs):
        if a_kind == "norm":
            a_ref, g_ref, b_ref, o_ref, acc = refs
        else:
            a_ref, b_ref, o_ref, acc = refs
        k = pl.program_id(2)

        @pl.when(k == 0)
        def _():
            acc[...] = jnp.zeros_like(acc)

        av = a_ref[...]
        if a_kind == "norm":
            av = av * _rms_scale(av) * g_ref[...]
        elif a_kind == "relu2":
            av = jnp.square(jnp.maximum(av.astype(F32), 0.0))
        acc[...] += _dot(av.astype(BF16), b_ref[...].astype(BF16), TN)

        @pl.when(k == nk - 1)
        def _():
            if shard_axis == 0:
                o_ref[...] = acc[...].reshape(o_ref.shape).astype(o_ref.dtype)
            elif shard_axis == 1:
                cs = nb // 4
                for n in range(o_ref.shape[0]):
                    o_ref[n] = acc[:, n * cs:(n + 1) * cs].astype(o_ref.dtype)
            else:
                o_ref[...] = acc[...].astype(o_ref.dtype)

    in_specs = [pl.BlockSpec((tk, tm), lambda i, j, k: (k, i))]
    args = [a]
    if a_kind == "norm":
        in_specs.append(pl.BlockSpec((1, ma), lambda i, j, k: (0, 0)))
        args.append(gain)
    in_specs.append(pl.BlockSpec((tk, tn), lambda i, j, k: (k, j)))
    args.append(b)
    return pl.pallas_call(
        body, name=name, grid=(ma // tm, nb // tn, nk),
        in_specs=in_specs,
        out_specs=out_spec,
        out_shape=jax.ShapeDtypeStruct(out_shape, out_dtype),
        scratch_shapes=[pltpu.VMEM((tm, tn), F32)],
        compiler_params=_params(("parallel", "parallel", "arbitrary")),
    )(*args)


def _tri(n, upper):
    r = lax.broadcasted_iota(jnp.int32, (n, n), 0)
    c = lax.broadcasted_iota(jnp.int32, (n, n), 1)
    return jnp.where((c >= r) if upper else (c <= r), 1.0, 0.0).astype(F32)


def _head_rows():
    r = lax.broadcasted_iota(jnp.int32, (SUBLANES, LANES), 0)
    c = lax.broadcasted_iota(jnp.int32, (SUBLANES, LANES), 1)
    return jnp.where(r == c, 1.0, 0.0).astype(F32)


def forget_cumsum(rest, bf, bsz, seq, f_blk):
    tc = _rows(seq, 512)
    nc = seq // tc

    def body(f_ref, b_ref, col_ref, row_ref, carry):
        c = pl.program_id(1)

        @pl.when(c == 0)
        def _():
            carry[...] = jnp.zeros_like(carry)

        z = f_ref[...] + b_ref[...]
        logf = jnp.minimum(z, 0.0) - jnp.log(1.0 + jnp.exp(-jnp.abs(z)))
        cum = _dot_exact(_tri(tc, False), logf, NN) + carry[0:1, :]
        col_ref[0] = cum
        row_ref[0] = _dot_exact(_head_rows(), cum, NT)
        carry[...] = jnp.broadcast_to(cum[tc - 1:tc, :], carry.shape)

    return pl.pallas_call(
        body, name="forget_cumsum", grid=(bsz, nc),
        in_specs=[pl.BlockSpec((tc, F_PAD), lambda b, c: (b * nc + c, f_blk)),
                  pl.BlockSpec((1, F_PAD), lambda b, c: (0, 0))],
        out_specs=[pl.BlockSpec((1, tc, LANES), lambda b, c: (b, c, 0)),
                   pl.BlockSpec((1, SUBLANES, tc), lambda b, c: (b, 0, c))],
        out_shape=[jax.ShapeDtypeStruct((bsz, seq, LANES), F32),
                   jax.ShapeDtypeStruct((bsz, SUBLANES, seq), F32)],
        scratch_shapes=[pltpu.VMEM((SUBLANES, LANES), F32)],
        compiler_params=_params(("parallel", "arbitrary")),
    )(rest, bf)


def forget_bwd(dcq, dcp, rest, bf, bsz, seq, f_blk):
    tc = _attn_tile(seq)
    nc = seq // tc

    def body(dq_ref, dc_ref, f_ref, b_ref, df_ref, db_ref, carry):
        b = pl.program_id(0)
        c = pl.program_id(1)

        @pl.when(c == 0)
        def _():
            carry[...] = jnp.zeros_like(carry)

        @pl.when((b == 0) & (c == 0))
        def _():
            db_ref[...] = jnp.zeros_like(db_ref)

        row = lax.broadcasted_iota(jnp.int32, (SUBLANES, tc), 0)
        heads = jnp.zeros((SUBLANES, tc), F32)
        dc = jnp.zeros((tc, LANES), F32)
        for p in range(HEAD_PAIRS):
            blk = dq_ref[0, p, 0]
            heads = heads + jnp.where(row == 2 * p, blk[0:1], 0.0) + jnp.where(row == 2 * p + 1, blk[1:2], 0.0)
            dc = dc + dc_ref[0, p]
        dc = dc + jnp.concatenate([heads, jnp.zeros((LANES - SUBLANES, tc), F32)], axis=0).T
        dlogf = _dot_exact(_tri(tc, True), dc, NN) + carry[0:1, :]
        carry[...] = jnp.broadcast_to(dlogf[0:1, :], carry.shape)
        z = f_ref[...] + b_ref[...]
        lane = lax.broadcasted_iota(jnp.int32, z.shape, 1)
        df = jnp.where(lane < ATTN_HEADS, dlogf * _sigmoid(-z), 0.0)
        df_ref[...] = df.astype(df_ref.dtype)
        db_ref[...] += _fold8(df)

    return pl.pallas_call(
        body, name="forget_bwd", grid=(bsz, nc),
        in_specs=[pl.BlockSpec((1, HEAD_PAIRS, 1, SUBLANES, tc), lambda b, c: (b, 0, nc - 1 - c, 0, 0)),
                  pl.BlockSpec((1, HEAD_PAIRS, tc, LANES), lambda b, c: (b, 0, nc - 1 - c, 0)),
                  pl.BlockSpec((tc, F_PAD), lambda b, c: (b * nc + nc - 1 - c, f_blk)),
                  pl.BlockSpec((1, F_PAD), lambda b, c: (0, 0))],
        out_specs=[pl.BlockSpec((tc, F_PAD), lambda b, c: (b * nc + nc - 1 - c, 0)),
                   pl.BlockSpec((SUBLANES, F_PAD), lambda b, c: (0, 0))],
        out_shape=[jax.ShapeDtypeStruct((bsz * seq, F_PAD), BF16),
                   jax.ShapeDtypeStruct((SUBLANES, F_PAD), F32)],
        scratch_shapes=[pltpu.VMEM((SUBLANES, LANES), F32)],
        compiler_params=_params(("arbitrary", "arbitrary")),
    )(dcq, dcp, rest, bf)


def _attn_tile(seq):
    return 512 if seq >= 2048 else 128


def _lane_head(shape, par):
    lane = lax.broadcasted_iota(jnp.int32, shape, len(shape) - 1)
    return (lane >= HEAD_DIM) if par else (lane < HEAD_DIM)


def _pick_lane(block, idx):
    lane = lax.broadcasted_iota(jnp.int32, block.shape, 1)
    return jnp.sum(jnp.where(lane == idx, block, 0.0), axis=1, keepdims=True)


def _pair_rows(lo_lane, hi_lane):
    r = lax.broadcasted_iota(jnp.int32, (SUBLANES, LANES), 0)
    c = lax.broadcasted_iota(jnp.int32, (SUBLANES, LANES), 1)
    if lo_lane is None:
        sel = ((r == 0) & (c < HEAD_DIM)) | ((r == 1) & (c >= HEAD_DIM))
    else:
        sel = ((r == 0) & (c == lo_lane)) | ((r == 1) & (c == hi_lane))
    return jnp.where(sel, 1.0, 0.0).astype(F32)


def _causal(s, transposed, first_row=0):
    r = lax.broadcasted_iota(jnp.int32, s.shape, 0) + first_row
    c = lax.broadcasted_iota(jnp.int32, s.shape, 1)
    return jnp.where((c >= r) if transposed else (r >= c), s, NEG_BIG)


def _causal_pairs(n, key_major):
    if key_major:
        pairs = [(i, j) for j in range(n) for i in range(j, n)]
    else:
        pairs = [(i, j) for i in range(n) for j in range(i + 1)]
    return (jnp.asarray([p[0] for p in pairs], jnp.int32), jnp.asarray([p[1] for p in pairs], jnp.int32))


def fox_fwd(qkv, cumrow, bsz, seq, carries=()):
    tq = _attn_tile(seq)
    nq = seq // tq
    scale = HEAD_DIM ** -0.5
    kb, vb = ATTN_WIDTH // LANES, 2 * ATTN_WIDTH // LANES
    qi_tab, kj_tab = _causal_pairs(nq, False)
    npairs = int(qi_tab.shape[0])
    sends, landings, n_sems, build = _carry_plan(carries)
    nc = len(sends)

    def body(qi_ref, kj_ref, q_ref, k_ref, v_ref, cr_ref, *rest):
        ins, (o_ref, lr_ref), lands = rest[:nc], rest[nc:nc + 2], rest[nc + 2:2 * nc + 2]
        (m_s, acc_s), sems = rest[2 * nc + 2:2 * nc + 4], rest[2 * nc + 4:]
        hp = pl.program_id(1)
        qi = qi_ref[pl.program_id(2)]
        kj = kj_ref[pl.program_id(2)]
        if nc:
            @pl.when((pl.program_id(0) == 0) & (hp == 0) & (pl.program_id(2) == 0))
            def _():
                for cp in build(ins, lands, sems):
                    cp.start()

        @pl.when(kj == 0)
        def _():
            m_s[...] = jnp.full_like(m_s, NEG_BIG)
            acc_s[...] = jnp.zeros_like(acc_s)

        def step(diag):
            q = q_ref[...]
            k = k_ref[...]
            v = v_ref[...]
            half = tq // 2
            for par in range(2):
                sel = _lane_head(q.shape, par)
                qh = (jnp.where(sel, q, 0.0) * scale).astype(BF16)
                vh = jnp.where(sel, v, 1.0).astype(BF16)
                cj = cr_ref[0, pl.ds(2 * hp + par, 1), :]
                for r0 in (0, half):
                    rows = slice(r0, r0 + half)
                    s = _dot(qh[rows], k, NT) - cj
                    if diag:
                        s = _causal(s, False, r0)
                    m_prev = m_s[par, rows]
                    m_new = jnp.maximum(m_prev, jnp.max(s, axis=1, keepdims=True))
                    p = jnp.exp(s - m_new).astype(BF16)
                    acc_s[par, rows] = jnp.exp(m_prev - m_new) * acc_s[par, rows] + _dot(p, vh)
                    m_s[par, rows] = m_new

        @pl.when(kj < qi)
        def _():
            step(False)

        @pl.when(kj == qi)
        def _():
            step(True)
            lo = _lane_head((tq, LANES), 0)
            sums = [pltpu.roll(acc_s[par], HEAD_DIM, 1) for par in range(2)]
            out = jnp.where(lo, acc_s[0] / sums[0], acc_s[1] / sums[1])
            o_ref[...] = out.astype(o_ref.dtype)
            lse = jnp.where(lo, m_s[0] + jnp.log(sums[0]), m_s[1] + jnp.log(sums[1]))
            lr_ref[0, 0] = _dot_exact(_pair_rows(0, HEAD_DIM), lse, NT)

        if nc:
            @pl.when((pl.program_id(0) == bsz - 1) & (hp == HEAD_PAIRS - 1) & (pl.program_id(2) == npairs - 1))
            def _():
                for cp in build(ins, lands, sems):
                    cp.wait()

    sem_shapes = [pltpu.SemaphoreType.DMA((n_sems,)), pltpu.SemaphoreType.DMA((n_sems,))] if nc else []
    outs = pl.pallas_call(
        body, name="fox_fwd",
        grid_spec=pltpu.PrefetchScalarGridSpec(
            num_scalar_prefetch=2, grid=(bsz, HEAD_PAIRS, npairs),
            in_specs=[pl.BlockSpec((tq, LANES), lambda b, h, t, qi, kj: (b * nq + qi[t], h)),
                      pl.BlockSpec((tq, LANES), lambda b, h, t, qi, kj: (b * nq + kj[t], kb + h)),
                      pl.BlockSpec((tq, LANES), lambda b, h, t, qi, kj: (b * nq + kj[t], vb + h)),
                      pl.BlockSpec((1, SUBLANES, tq), lambda b, h, t, qi, kj: (b, 0, kj[t]))] + [ANY] * nc,
            out_specs=[pl.BlockSpec((tq, LANES), lambda b, h, t, qi, kj: (b * nq + qi[t], h)),
                       pl.BlockSpec((1, 1, SUBLANES, tq), lambda b, h, t, qi, kj: (b, h, 0, qi[t]))] + [ANY] * nc,
            scratch_shapes=[pltpu.VMEM((2, tq, 1), F32), pltpu.VMEM((2, tq, LANES), F32)] + sem_shapes),
        out_shape=[jax.ShapeDtypeStruct((bsz * seq, ATTN_WIDTH), BF16),
                   jax.ShapeDtypeStruct((bsz, HEAD_PAIRS, SUBLANES, seq), F32)] + landings,
        compiler_params=_params(("arbitrary", "arbitrary", "arbitrary")),
    )(qi_tab, kj_tab, qkv, qkv, qkv, cumrow, *sends)
    return outs[0], outs[1], list(outs[2:])


def fox_bwd_kernel(qkv, do, o, lser, cumcol, bsz, seq, carries=()):
    tk = _attn_tile(seq)
    nk = seq // tk
    scale = HEAD_DIM ** -0.5
    kb, vb = ATTN_WIDTH // LANES, 2 * ATTN_WIDTH // LANES
    qi_tab, kj_tab = _causal_pairs(nk, True)
    npairs = int(qi_tab.shape[0])
    sends, landings, n_sems, build = _carry_plan(carries)
    nc = len(sends)

    def body(qi_ref, kj_ref, q_ref, k_ref, v_ref, do_ref, o_ref, lr_ref, cc_ref, *rest):
        ins, lands = rest[:nc], rest[nc + 5:2 * nc + 5]
        dq_ref, dk_ref, dv_ref, dc_ref, dr_ref = rest[nc:nc + 5]
        (dq_s, dk_s, dv_s, dc_s, dr_s), sems = rest[2 * nc + 5:2 * nc + 10], rest[2 * nc + 10:]
        hp = pl.program_id(1)
        t = pl.program_id(2)
        qi = qi_ref[t]
        kj = kj_ref[t]
        if nc:
            @pl.when((pl.program_id(0) == 0) & (hp == 0) & (t == 0))
            def _():
                for cp in build(ins, lands, sems):
                    cp.start()

        @pl.when(t == 0)
        def _():
            dq_s[...] = jnp.zeros_like(dq_s)
            dr_s[...] = jnp.zeros_like(dr_s)

        @pl.when(qi == kj)
        def _():
            dk_s[...] = jnp.zeros_like(dk_s)
            dv_s[...] = jnp.zeros_like(dv_s)
            dc_s[...] = jnp.zeros_like(dc_s)

        def step(diag):
            q = q_ref[...]
            k = k_ref[...]
            v = v_ref[...]
            dov = do_ref[...]
            prod = dov.astype(F32) * o_ref[...].astype(F32)
            drow = _dot_exact(_pair_rows(None, None), prod, NT)
            lrow = lr_ref[0, 0]
            lane = lax.broadcasted_iota(jnp.int32, (tk, LANES), 1)
            for par in range(2):
                head = 2 * hp + par
                sel = _lane_head(k.shape, par)
                kh = (jnp.where(sel, k, 0.0) * scale).astype(BF16)
                st = _dot(kh, q, NT) - _pick_lane(cc_ref[0], head)
                if diag:
                    st = _causal(st, True)
                pt = jnp.exp(st - lrow[par:par + 1, :])
                vh = jnp.where(sel, v, 0.0)
                dpt = _dot(vh.astype(BF16), dov, NT)
                dst = pt * (dpt - drow[par:par + 1, :])
                dsb = dst.astype(BF16)
                dv_s[...] += jnp.where(sel, _dot(pt.astype(BF16), dov), 0.0)
                dk_s[...] += jnp.where(sel, _dot(dsb, q), 0.0)
                dq_s[qi] += _dot(dsb, kh, TN)
                dc_s[...] += jnp.where(lane == head, -jnp.sum(dst, axis=1, keepdims=True), 0.0)
                dr_s[qi, par:par + 1, :] += jnp.sum(dst, axis=0, keepdims=True)

        @pl.when(qi > kj)
        def _():
            step(False)

        @pl.when(qi == kj)
        def _():
            step(True)

        @pl.when(qi == nk - 1)
        def _():
            dk_ref[...] = (dk_s[...] * scale).astype(dk_ref.dtype)
            dv_ref[...] = dv_s[...].astype(dv_ref.dtype)
            dc_ref[0, 0] = dc_s[...]

        @pl.when(t == npairs - 1)
        def _():
            for i in range(nk):
                dq_ref[i * tk:(i + 1) * tk, :] = dq_s[i].astype(dq_ref.dtype)
            dr_ref[0, 0] = dr_s[...]

        if nc:
            @pl.when((pl.program_id(0) == bsz - 1) & (hp == HEAD_PAIRS - 1) & (t == npairs - 1))
            def _():
                for cp in build(ins, lands, sems):
                    cp.wait()

    sem_shapes = [pltpu.SemaphoreType.DMA((n_sems,)), pltpu.SemaphoreType.DMA((n_sems,))] if nc else []
    outs = pl.pallas_call(
        body, name="fox_bwd",
        grid_spec=pltpu.PrefetchScalarGridSpec(
            num_scalar_prefetch=2, grid=(bsz, HEAD_PAIRS, npairs),
            in_specs=[pl.BlockSpec((tk, LANES), lambda b, h, t, qi, kj: (b * nk + qi[t], h)),
                      pl.BlockSpec((tk, LANES), lambda b, h, t, qi, kj: (b * nk + kj[t], kb + h)),
                      pl.BlockSpec((tk, LANES), lambda b, h, t, qi, kj: (b * nk + kj[t], vb + h)),
                      pl.BlockSpec((tk, LANES), lambda b, h, t, qi, kj: (b * nk + qi[t], h)),
                      pl.BlockSpec((tk, LANES), lambda b, h, t, qi, kj: (b * nk + qi[t], h)),
                      pl.BlockSpec((1, 1, SUBLANES, tk), lambda b, h, t, qi, kj: (b, h, 0, qi[t])),
                      pl.BlockSpec((1, tk, LANES), lambda b, h, t, qi, kj: (b, kj[t], 0))] + [ANY] * nc,
            out_specs=[pl.BlockSpec((seq, LANES), lambda b, h, t, qi, kj: (b, h)),
                       pl.BlockSpec((tk, LANES), lambda b, h, t, qi, kj: (b * nk + kj[t], h)),
                       pl.BlockSpec((tk, LANES), lambda b, h, t, qi, kj: (b * nk + kj[t], h)),
                       pl.BlockSpec((1, 1, tk, LANES), lambda b, h, t, qi, kj: (b, h, kj[t], 0)),
                       pl.BlockSpec((1, 1, nk, SUBLANES, tk), lambda b, h, t, qi, kj: (b, h, 0, 0, 0))] + [ANY] * nc,
            scratch_shapes=[pltpu.VMEM((nk, tk, LANES), F32), pltpu.VMEM((tk, LANES), F32),
                            pltpu.VMEM((tk, LANES), F32), pltpu.VMEM((tk, LANES), F32),
                            pltpu.VMEM((nk, SUBLANES, tk), F32)] + sem_shapes),
        out_shape=[jax.ShapeDtypeStruct((bsz * seq, ATTN_WIDTH), BF16),
                   jax.ShapeDtypeStruct((bsz * seq, ATTN_WIDTH), BF16),
                   jax.ShapeDtypeStruct((bsz * seq, ATTN_WIDTH), BF16),
                   jax.ShapeDtypeStruct((bsz, HEAD_PAIRS, seq, LANES), F32),
                   jax.ShapeDtypeStruct((bsz, HEAD_PAIRS, nk, SUBLANES, tk), F32)] + landings,
        compiler_params=_params(("arbitrary", "arbitrary", "arbitrary")),
    )(qi_tab, kj_tab, qkv, qkv, qkv, do, o, lser, cumcol, *sends)
    return outs[:5], list(outs[5:])


def fox_bwd(qkv, do, o, lser, cumcol, rest, bf, bsz, seq, f_blk, carries=()):
    (dq, dk, dv, dcp, dcq), lands = fox_bwd_kernel(qkv, do, o, lser, cumcol, bsz, seq, carries)
    df, dbf = forget_bwd(dcq, dcp, rest, bf, bsz, seq, f_blk)
    return dq, dk, dv, df, dbf, lands


SCAN_STEPS = (1, 2, 4)
TAB_FWD = 0
TAB_BWD = 32
TAB_CARRY = 24
TAB_ROWS = 64


def _ssm_tile(seq):
    return 256 if seq >= 1024 else 64


def _scan_block(xr, xi, tab_ref, re, im, cr, ci, reverse):
    base = TAB_BWD if reverse else TAB_FWD
    for n, s in enumerate(SCAN_STEPS):
        ar = tab_ref[base + n * SUBLANES:base + (n + 1) * SUBLANES, re]
        ai = tab_ref[base + n * SUBLANES:base + (n + 1) * SUBLANES, im]
        shift = SUBLANES - s if reverse else s
        sr = pltpu.roll(xr, shift, 0)
        si = pltpu.roll(xi, shift, 0)
        xr, xi = xr + ar * sr - ai * si, xi + ar * si + ai * sr
    pr = tab_ref[base + TAB_CARRY:base + TAB_CARRY + SUBLANES, re]
    pi = tab_ref[base + TAB_CARRY:base + TAB_CARRY + SUBLANES, im]
    xr, xi = xr + pr * cr - pi * ci, xi + pr * ci + pi * cr
    return xr, xi


def ssm_fwd(rest, wb4, wc4, tabs, dskip, bsz, seq, u_blk):
    tt = _ssm_tile(seq)
    nt = seq // tt

    def body(u_ref, wb_ref, wc_ref, tab_ref, d_ref, y_ref, h_ref, carry):
        c = pl.program_id(1)

        @pl.when(c == 0)
        def _():
            carry[...] = jnp.zeros_like(carry)

        u = u_ref[...]
        ub = u.astype(BF16)
        for j in range(SSM_CHUNKS):
            h_ref[:, j * CHUNK_LANES:(j + 1) * CHUNK_LANES] = _dot(ub[:, j * LANES:(j + 1) * LANES], wb_ref[j])
        for j in range(SSM_CHUNKS):
            re = slice(j * CHUNK_LANES, j * CHUNK_LANES + CHUNK_STATES)
            im = slice(j * CHUNK_LANES + CHUNK_STATES, (j + 1) * CHUNK_LANES)

            def blk(bi, car):
                r0 = pl.multiple_of(bi * SUBLANES, SUBLANES)
                xr, xi = _scan_block(h_ref[pl.ds(r0, SUBLANES), re], h_ref[pl.ds(r0, SUBLANES), im],
                                     tab_ref, re, im, car[0], car[1], False)
                h_ref[pl.ds(r0, SUBLANES), re] = xr
                h_ref[pl.ds(r0, SUBLANES), im] = xi
                return xr[SUBLANES - 1:SUBLANES], xi[SUBLANES - 1:SUBLANES]

            cr, ci = lax.fori_loop(0, tt // SUBLANES, blk, (carry[0:1, re], carry[0:1, im]), unroll=2)
            carry[0:1, re] = cr
            carry[0:1, im] = ci
        for j in range(SSM_CHUNKS):
            hj = h_ref[:, j * CHUNK_LANES:(j + 1) * CHUNK_LANES].astype(BF16)
            cols = slice(j * LANES, (j + 1) * LANES)
            y_ref[:, cols] = _dot(hj, wc_ref[j]) + d_ref[:, cols] * u[:, cols]

    return pl.pallas_call(
        body, name="ssm_fwd", grid=(bsz, nt),
        in_specs=[pl.BlockSpec((tt, SSM_WIDTH), lambda b, c: (b * nt + c, u_blk)),
                  pl.BlockSpec((SSM_CHUNKS, LANES, CHUNK_LANES), lambda b, c: (0, 0, 0)),
                  pl.BlockSpec((SSM_CHUNKS, CHUNK_LANES, LANES), lambda b, c: (0, 0, 0)),
                  pl.BlockSpec((TAB_ROWS, STATE_LANES), lambda b, c: (0, 0)),
                  pl.BlockSpec((1, SSM_WIDTH), lambda b, c: (0, 0))],
        out_specs=[pl.BlockSpec((tt, SSM_WIDTH), lambda b, c: (b * nt + c, 0)),
                   pl.BlockSpec((tt, STATE_LANES), lambda b, c: (b * nt + c, 0))],
        out_shape=[jax.ShapeDtypeStruct((bsz * seq, SSM_WIDTH), F32),
                   jax.ShapeDtypeStruct((bsz * seq, STATE_LANES), F32)],
        scratch_shapes=[pltpu.VMEM((SUBLANES, STATE_LANES), F32)],
        compiler_params=_params(("parallel", "arbitrary")),
    )(rest, wb4, wc4, tabs, dskip)


def ssm_bwd(dys, rest, hs, wb4, wc4, tabs, dskip, bsz, seq, u_blk):
    tt = _ssm_tile(seq)
    nt = seq // tt
    nb = tt // SUBLANES

    def body(dy_ref, u_ref, h_ref, hp_ref, wb_ref, wc_ref, tab_ref, d_ref,
             du_ref, ga_ref, gwb_ref, gwc_ref, gd_ref, g_s, carry):
        b = pl.program_id(0)
        c = pl.program_id(1)

        @pl.when(c == 0)
        def _():
            carry[...] = jnp.zeros_like(carry)

        @pl.when((b == 0) & (c == 0))
        def _():
            ga_ref[...] = jnp.zeros_like(ga_ref)
            gwb_ref[...] = jnp.zeros_like(gwb_ref)
            gwc_ref[...] = jnp.zeros_like(gwc_ref)
            gd_ref[...] = jnp.zeros_like(gd_ref)

        dy = dy_ref[...].astype(F32)
        dyb = dy.astype(BF16)
        u = u_ref[...]
        ub = u.astype(BF16)
        first_chunk = c == nt - 1
        for j in range(SSM_CHUNKS):
            g_s[:, j * CHUNK_LANES:(j + 1) * CHUNK_LANES] = _dot(dyb[:, j * LANES:(j + 1) * LANES], wc_ref[j], NT)
        for j in range(SSM_CHUNKS):
            re = slice(j * CHUNK_LANES, j * CHUNK_LANES + CHUNK_STATES)
            im = slice(j * CHUNK_LANES + CHUNK_STATES, (j + 1) * CHUNK_LANES)
            row = lax.broadcasted_iota(jnp.int32, (SUBLANES, CHUNK_STATES), 0)

            def blk(n, car):
                bi = nb - 1 - n
                r0 = pl.multiple_of(bi * SUBLANES, SUBLANES)
                gr, gi = _scan_block(g_s[pl.ds(r0, SUBLANES), re], g_s[pl.ds(r0, SUBLANES), im],
                                     tab_ref, re, im, car[0], car[1], True)
                g_s[pl.ds(r0, SUBLANES), re] = gr
                g_s[pl.ds(r0, SUBLANES), im] = gi
                rp = pl.multiple_of(jnp.maximum(bi - 1, 0) * SUBLANES, SUBLANES)
                inside = bi > 0
                live = jnp.where(jnp.logical_or(inside, jnp.logical_not(first_chunk)), 1.0, 0.0)
                pr = jnp.where(inside, h_ref[pl.ds(rp, SUBLANES), re], hp_ref[:, re])[SUBLANES - 1:SUBLANES] * live
                pi = jnp.where(inside, h_ref[pl.ds(rp, SUBLANES), im], hp_ref[:, im])[SUBLANES - 1:SUBLANES] * live
                hr = jnp.where(row >= 1, pltpu.roll(h_ref[pl.ds(r0, SUBLANES), re], 1, 0), pr)
                hi = jnp.where(row >= 1, pltpu.roll(h_ref[pl.ds(r0, SUBLANES), im], 1, 0), pi)
                return (gr[0:1], gi[0:1], car[2] + gr * hr + gi * hi, car[3] + gi * hr - gr * hi)

            zero = jnp.zeros((SUBLANES, CHUNK_STATES), F32)
            cr, ci, sr, si = lax.fori_loop(0, nb, blk, (carry[0:1, re], carry[0:1, im], zero, zero), unroll=2)
            carry[0:1, re] = cr
            carry[0:1, im] = ci
            ga_ref[:, re] += sr
            ga_ref[:, im] += si
        for j in range(SSM_CHUNKS):
            cols = slice(j * LANES, (j + 1) * LANES)
            lanes = slice(j * CHUNK_LANES, (j + 1) * CHUNK_LANES)
            gj = g_s[:, lanes].astype(BF16)
            du_ref[:, cols] = (_dot(gj, wb_ref[j], NT) + d_ref[:, cols] * dy[:, cols]).astype(du_ref.dtype)
            gwb_ref[j] += _dot(ub[:, cols], gj, TN)
            gwc_ref[j] += _dot(h_ref[:, lanes].astype(BF16), dyb[:, cols], TN)
        gd_ref[...] += _fold8(dy * u)

    def prev_rows(b, c):
        chunk = nt - 1 - c
        return (jnp.maximum((b * nt + chunk) * nb - 1, 0), 0)

    return pl.pallas_call(
        body, name="ssm_bwd", grid=(bsz, nt),
        in_specs=[pl.BlockSpec((tt, SSM_WIDTH), lambda b, c: (b * nt + nt - 1 - c, 0)),
                  pl.BlockSpec((tt, SSM_WIDTH), lambda b, c: (b * nt + nt - 1 - c, u_blk)),
                  pl.BlockSpec((tt, STATE_LANES), lambda b, c: (b * nt + nt - 1 - c, 0)),
                  pl.BlockSpec((SUBLANES, STATE_LANES), prev_rows),
                  pl.BlockSpec((SSM_CHUNKS, LANES, CHUNK_LANES), lambda b, c: (0, 0, 0)),
                  pl.BlockSpec((SSM_CHUNKS, CHUNK_LANES, LANES), lambda b, c: (0, 0, 0)),
                  pl.BlockSpec((TAB_ROWS, STATE_LANES), lambda b, c: (0, 0)),
                  pl.BlockSpec((1, SSM_WIDTH), lambda b, c: (0, 0))],
        out_specs=[pl.BlockSpec((tt, SSM_WIDTH), lambda b, c: (b * nt + nt - 1 - c, 0)),
                   pl.BlockSpec((SUBLANES, STATE_LANES), lambda b, c: (0, 0)),
                   pl.BlockSpec((SSM_CHUNKS, LANES, CHUNK_LANES), lambda b, c: (0, 0, 0)),
                   pl.BlockSpec((SSM_CHUNKS, CHUNK_LANES, LANES), lambda b, c: (0, 0, 0)),
                   pl.BlockSpec((SUBLANES, SSM_WIDTH), lambda b, c: (0, 0))],
        out_shape=[jax.ShapeDtypeStruct((bsz * seq, SSM_WIDTH), BF16),
                   jax.ShapeDtypeStruct((SUBLANES, STATE_LANES), F32),
                   jax.ShapeDtypeStruct((SSM_CHUNKS, LANES, CHUNK_LANES), F32),
                   jax.ShapeDtypeStruct((SSM_CHUNKS, CHUNK_LANES, LANES), F32),
                   jax.ShapeDtypeStruct((SUBLANES, SSM_WIDTH), F32)],
        scratch_shapes=[pltpu.VMEM((tt, STATE_LANES), F32), pltpu.VMEM((SUBLANES, STATE_LANES), F32)],
        compiler_params=_params(("arbitrary", "arbitrary")),
    )(dys, rest, hs, hs, wb4, wc4, tabs, dskip)


def _gelu(v):
    t = jnp.tanh(GELU_C * (v + GELU_A * v * v * v))
    return 0.5 * v * (1.0 + t), t


def mix_fwd(ya, ys, rest, x0, wglu, bglu, wba, wbb, wout):
    t, d = x0.shape
    tm = _rows(t, 512)

    def body(ya_ref, ys_ref, ga_ref, gb_ref, x_ref, wg_ref, bg_ref, wa_ref, wb_ref, wo_ref,
             x1_ref, z_ref, pa_ref, pb_ref, yb_ref, yb2_ref, mx_ref):
        yb, _ = _gelu(ys_ref[...])
        ybb = yb.astype(BF16)
        z = _dot(ybb, wg_ref[...]) + bg_ref[...]
        yb2 = (yb * _sigmoid(z)).astype(BF16)
        pa = _dot(ya_ref[...], wa_ref[...])
        pb = _dot(yb2, wb_ref[...])
        mixed = (_sigmoid(ga_ref[...]) * pa + _sigmoid(gb_ref[...]) * pb).astype(BF16)
        x1_ref[...] = x_ref[...] + _dot(mixed, wo_ref[...])
        z_ref[...] = z.astype(z_ref.dtype)
        pa_ref[...] = pa.astype(pa_ref.dtype)
        pb_ref[...] = pb.astype(pb_ref.dtype)
        yb_ref[...] = ybb
        yb2_ref[...] = yb2
        mx_ref[...] = mixed

    row = lambda w: pl.BlockSpec((tm, w), lambda i: (i, 0))
    full = lambda a: pl.BlockSpec(a.shape, lambda i: (0,) * a.ndim)
    return pl.pallas_call(
        body, name="mix_fwd", grid=(t // tm,),
        in_specs=[row(ATTN_WIDTH), row(SSM_WIDTH),
                  pl.BlockSpec((tm, d), lambda i: (i, 0)), pl.BlockSpec((tm, d), lambda i: (i, 1)),
                  row(d), full(wglu), full(bglu), full(wba), full(wbb), full(wout)],
        out_specs=[row(d), row(SSM_WIDTH), row(d), row(d), row(SSM_WIDTH), row(SSM_WIDTH), row(d)],
        out_shape=[jax.ShapeDtypeStruct((t, d), F32), jax.ShapeDtypeStruct((t, SSM_WIDTH), BF16),
                   jax.ShapeDtypeStruct((t, d), BF16), jax.ShapeDtypeStruct((t, d), BF16),
                   jax.ShapeDtypeStruct((t, SSM_WIDTH), BF16), jax.ShapeDtypeStruct((t, SSM_WIDTH), BF16),
                   jax.ShapeDtypeStruct((t, d), BF16)],
        compiler_params=_params(("parallel",)),
    )(ya, ys, rest, rest, x0, wglu, bglu, wba, wbb, wout)


def mix_bwd(dx1, rest, pa, pb, z, ys, wglu, wba, wbb, wout):
    t, d = dx1.shape
    tm = _rows(t, 512)

    def body(dx_ref, ga_ref, gb_ref, pa_ref, pb_ref, z_ref, ys_ref, wg_ref, wa_ref, wb_ref, wo_ref,
             dya_ref, dys_ref, dg_ref, dpa_ref, dpb_ref, dz_ref, dbg_ref):
        @pl.when(pl.program_id(0) == 0)
        def _():
            dbg_ref[...] = jnp.zeros_like(dbg_ref)

        dmix = _dot(dx_ref[...].astype(BF16), wo_ref[...], NT)
        sa = _sigmoid(ga_ref[...])
        sb = _sigmoid(gb_ref[...])
        dpa = (dmix * sa).astype(BF16)
        dpb = (dmix * sb).astype(BF16)
        dg_ref[:, 0:d] = (dmix * pa_ref[...].astype(F32) * sa * (1.0 - sa)).astype(dg_ref.dtype)
        dg_ref[:, d:2 * d] = (dmix * pb_ref[...].astype(F32) * sb * (1.0 - sb)).astype(dg_ref.dtype)
        dpa_ref[...] = dpa
        dpb_ref[...] = dpb
        dya_ref[...] = _dot(dpa, wa_ref[...], NT).astype(dya_ref.dtype)
        dyb2 = _dot(dpb, wb_ref[...], NT)
        ys = ys_ref[...]
        yb, th = _gelu(ys)
        sg = _sigmoid(z_ref[...].astype(F32))
        dz = dyb2 * yb * sg * (1.0 - sg)
        dzb = dz.astype(BF16)
        dz_ref[...] = dzb
        dbg_ref[...] += _fold8(dz)
        dyb = dyb2 * sg + _dot(dzb, wg_ref[...], NT)
        dgelu = 0.5 * (1.0 + th) + 0.5 * ys * (1.0 - th * th) * GELU_C * (1.0 + 3.0 * GELU_A * ys * ys)
        dys_ref[...] = (dyb * dgelu).astype(dys_ref.dtype)

    row = lambda w: pl.BlockSpec((tm, w), lambda i: (i, 0))
    full = lambda a: pl.BlockSpec(a.shape, lambda i: (0,) * a.ndim)
    return pl.pallas_call(
        body, name="mix_bwd", grid=(t // tm,),
        in_specs=[row(d), pl.BlockSpec((tm, d), lambda i: (i, 0)), pl.BlockSpec((tm, d), lambda i: (i, 1)),
                  row(d), row(d), row(SSM_WIDTH), row(SSM_WIDTH), full(wglu), full(wba), full(wbb), full(wout)],
        out_specs=[row(ATTN_WIDTH), row(SSM_WIDTH), row(2 * d), row(d), row(d), row(SSM_WIDTH),
                   pl.BlockSpec((SUBLANES, SSM_WIDTH), lambda i: (0, 0))],
        out_shape=[jax.ShapeDtypeStruct((t, ATTN_WIDTH), BF16), jax.ShapeDtypeStruct((t, SSM_WIDTH), BF16),
                   jax.ShapeDtypeStruct((t, 2 * d), BF16), jax.ShapeDtypeStruct((t, d), BF16),
                   jax.ShapeDtypeStruct((t, d), BF16), jax.ShapeDtypeStruct((t, SSM_WIDTH), BF16),
                   jax.ShapeDtypeStruct((SUBLANES, SSM_WIDTH), F32)],
        compiler_params=_params(("arbitrary",)),
    )(dx1, rest, rest, pa, pb, z, ys, wglu, wba, wbb, wout)


def mlp_fwd(x1, g, wup, wdown):
    t, d = x1.shape
    ff = wup.shape[1]
    tm, tf = _rows(t, 1024), _pick(ff, 1024)
    nf = ff // tf

    def body(x_ref, g_ref, wu_ref, wd_ref, x2_ref, up_ref, h_s, acc_s):
        f = pl.program_id(1)

        @pl.when(f == 0)
        def _():
            xv = x_ref[...]
            h_s[...] = (xv * _rms_scale(xv) * g_ref[...]).astype(BF16)
            acc_s[...] = jnp.zeros_like(acc_s)

        up = _dot(h_s[...], wu_ref[...])
        up_ref[...] = up.astype(up_ref.dtype)
        act = jnp.square(jnp.maximum(up, 0.0)).astype(BF16)
        acc_s[...] += _dot(act, wd_ref[...])

        @pl.when(f == nf - 1)
        def _():
            x2_ref[...] = x_ref[...] + acc_s[...]

    return pl.pallas_call(
        body, name="mlp_fwd", grid=(t // tm, nf),
        in_specs=[pl.BlockSpec((tm, d), lambda i, f: (i, 0)), pl.BlockSpec((1, d), lambda i, f: (0, 0)),
                  pl.BlockSpec((d, tf), lambda i, f: (0, f)), pl.BlockSpec((tf, d), lambda i, f: (f, 0))],
        out_specs=[pl.BlockSpec((tm, d), lambda i, f: (i, 0)), pl.BlockSpec((tm, tf), lambda i, f: (i, f))],
        out_shape=[jax.ShapeDtypeStruct((t, d), F32), jax.ShapeDtypeStruct((t, ff), BF16)],
        scratch_shapes=[pltpu.VMEM((tm, d), BF16), pltpu.VMEM((tm, d), F32)],
        compiler_params=_params(("parallel", "arbitrary")),
    )(x1, g, wup, wdown)


def mlp_bwd(dx2, up, x1, g, wup, wdown, carries=()):
    t, d = x1.shape
    ff = wup.shape[1]
    tm, tf = _rows(t, 512), _pick(ff, 1024)
    nf = ff // tf
    ni = t // tm
    sends, landings, n_sems, build = _carry_plan(carries)
    nc = len(sends)

    def body(dx_ref, up_ref, x_ref, g_ref, wu_ref, wd_ref, *rest):
        ins, (dup_ref, dx1_ref, dg_ref), lands = rest[:nc], rest[nc:nc + 3], rest[nc + 3:2 * nc + 3]
        (dxb_s, acc_s), sems = rest[2 * nc + 3:2 * nc + 5], rest[2 * nc + 5:]
        i = pl.program_id(0)
        f = pl.program_id(1)
        if nc:
            @pl.when((i == 0) & (f == 0))
            def _():
                for cp in build(ins, lands, sems):
                    cp.start()

        @pl.when((i == 0) & (f == 0))
        def _():
            dg_ref[...] = jnp.zeros_like(dg_ref)

        @pl.when(f == 0)
        def _():
            dxb_s[...] = dx_ref[...].astype(BF16)
            acc_s[...] = jnp.zeros_like(acc_s)

        dact = _dot(dxb_s[...], wd_ref[...], NT)
        dup = (dact * 2.0 * jnp.maximum(up_ref[...].astype(F32), 0.0)).astype(BF16)
        dup_ref[...] = dup
        acc_s[...] += _dot(dup, wu_ref[...], NT)

        @pl.when(f == nf - 1)
        def _():
            dxn, dgain = _rms_bwd(x_ref[...], g_ref[...], acc_s[...])
            dx1_ref[...] = dx_ref[...] + dxn
            dg_ref[...] += _fold8(dgain)

        if nc:
            @pl.when((i == ni - 1) & (f == nf - 1))
            def _():
                for cp in build(ins, lands, sems):
                    cp.wait()

    sem_shapes = [pltpu.SemaphoreType.DMA((n_sems,)), pltpu.SemaphoreType.DMA((n_sems,))] if nc else []
    outs = pl.pallas_call(
        body, name="mlp_bwd", grid=(ni, nf),
        in_specs=[pl.BlockSpec((tm, d), lambda i, f: (i, 0)), pl.BlockSpec((tm, tf), lambda i, f: (i, f)),
                  pl.BlockSpec((tm, d), lambda i, f: (i, 0)), pl.BlockSpec((1, d), lambda i, f: (0, 0)),
                  pl.BlockSpec((d, tf), lambda i, f: (0, f)), pl.BlockSpec((tf, d), lambda i, f: (f, 0))] + [ANY] * nc,
        out_specs=[pl.BlockSpec((tm, tf), lambda i, f: (i, f)), pl.BlockSpec((tm, d), lambda i, f: (i, 0)),
                   pl.BlockSpec((SUBLANES, d), lambda i, f: (0, 0))] + [ANY] * nc,
        out_shape=[jax.ShapeDtypeStruct((t, ff), BF16), jax.ShapeDtypeStruct((t, d), F32),
                   jax.ShapeDtypeStruct((SUBLANES, d), F32)] + landings,
        scratch_shapes=[pltpu.VMEM((tm, d), BF16), pltpu.VMEM((tm, d), F32)] + sem_shapes,
        compiler_params=_params(("arbitrary", "arbitrary")),
    )(dx2, up, x1, g, wup, wdown, *sends)
    return outs[0], outs[1], outs[2], list(outs[3:])


def proj_bwd(dproj, wpad, x0, dx1, g, carries=()):
    t, d = x0.shape
    m = wpad.shape[1]
    tm = _rows(t, 512)
    ni = t // tm
    sends, landings, n_sems, build = _carry_plan(carries)
    nc = len(sends)

    def body(dp_ref, w_ref, x_ref, dx1_ref, g_ref, *rest):
        ins, (dx0_ref, dg_ref), lands, sems = rest[:nc], rest[nc:nc + 2], rest[nc + 2:2 * nc + 2], rest[2 * nc + 2:]

        @pl.when(pl.program_id(0) == 0)
        def _():
            dg_ref[...] = jnp.zeros_like(dg_ref)
            for cp in build(ins, lands, sems):
                cp.start()

        dh = _dot(dp_ref[...], w_ref[...], NT)
        dxn, dgain = _rms_bwd(x_ref[...], g_ref[...], dh)
        dx0_ref[...] = dx1_ref[...] + dxn
        dg_ref[...] += _fold8(dgain)

        if nc:
            @pl.when(pl.program_id(0) == ni - 1)
            def _():
                for cp in build(ins, lands, sems):
                    cp.wait()

    sem_shapes = [pltpu.SemaphoreType.DMA((n_sems,)), pltpu.SemaphoreType.DMA((n_sems,))] if nc else []
    outs = pl.pallas_call(
        body, name="proj_bwd", grid=(ni,),
        in_specs=[pl.BlockSpec((tm, m), lambda i: (i, 0)), pl.BlockSpec((d, m), lambda i: (0, 0)),
                  pl.BlockSpec((tm, d), lambda i: (i, 0)), pl.BlockSpec((tm, d), lambda i: (i, 0)),
                  pl.BlockSpec((1, d), lambda i: (0, 0))] + [ANY] * nc,
        out_specs=[pl.BlockSpec((tm, d), lambda i: (i, 0)), pl.BlockSpec((SUBLANES, d), lambda i: (0, 0))] + [ANY] * nc,
        out_shape=[jax.ShapeDtypeStruct((t, d), F32), jax.ShapeDtypeStruct((SUBLANES, d), F32)] + landings,
        scratch_shapes=sem_shapes,
        compiler_params=_params(("arbitrary",)),
    )(dproj, wpad, x0, dx1, g, *sends)
    return outs[0], outs[1], list(outs[2:])


def final_loss(x, g, target):
    t, d = x.shape
    tm = _rows(t, 512)

    def body(x_ref, g_ref, t_ref, dx_ref, ls_ref, dg_ref):
        @pl.when(pl.program_id(0) == 0)
        def _():
            ls_ref[...] = jnp.zeros_like(ls_ref)
            dg_ref[...] = jnp.zeros_like(dg_ref)

        xv = x_ref[...]
        gv = g_ref[...]
        err = xv * _rms_scale(xv) * gv - t_ref[...]
        ls_ref[...] += _fold8(err * err) * (0.5 / d)
        dxn, dgain = _rms_bwd(xv, gv, err * (1.0 / d))
        dx_ref[...] = dxn
        dg_ref[...] += _fold8(dgain)

    return pl.pallas_call(
        body, name="final_loss", grid=(t // tm,),
        in_specs=[pl.BlockSpec((tm, d), lambda i: (i, 0)), pl.BlockSpec((1, d), lambda i: (0, 0)),
                  pl.BlockSpec((tm, d), lambda i: (i, 0))],
        out_specs=[pl.BlockSpec((tm, d), lambda i: (i, 0)), pl.BlockSpec((SUBLANES, d), lambda i: (0, 0)),
                   pl.BlockSpec((SUBLANES, d), lambda i: (0, 0))],
        out_shape=[jax.ShapeDtypeStruct((t, d), F32), jax.ShapeDtypeStruct((SUBLANES, d), F32),
                   jax.ShapeDtypeStruct((SUBLANES, d), F32)],
        compiler_params=_params(("arbitrary",)),
    )(x, g, target)


def _peers():
    x, y, c = lax.axis_index("x"), lax.axis_index("y"), lax.axis_index("c")
    peers = []
    for m in range(1, 8):
        fx, fy, fc = (m >> 2) & 1, (m >> 1) & 1, m & 1
        peers.append((m, ((1 - x) if fx else x, (1 - y) if fy else y, (1 - c) if fc else c)))
    return x, y, c, peers


def _remote(src, dst, sems, k, peer):
    return pltpu.make_async_remote_copy(src_ref=src, dst_ref=dst, send_sem=sems[0].at[k], recv_sem=sems[1].at[k],
                                        device_id=peer, device_id_type=MESH)


def _gather_copies(w_refs, out_refs, sems, first):
    x, y, c, peers = _peers()
    chip = 2 * x + y
    cps = []
    for t, (w, o) in enumerate(zip(w_refs, out_refs)):
        half = w.shape[0] // 2
        rows = pl.ds(c * half, half)
        for m, peer in peers:
            if m >> 1:
                cps.append(_remote(w.at[rows], o.at[chip, rows], sems, first + 7 * t + m - 1, peer))
    return cps


def _reduce_copies(g_refs, out_refs, sems, first):
    x, y, c, peers = _peers()
    me = 4 * x + 2 * y + c
    cps = []
    for t, (g, o) in enumerate(zip(g_refs, out_refs)):
        for m, (px, py, pc) in peers:
            cps.append(_remote(g.at[2 * px + py], o.at[me], sems, first + 7 * t + m - 1, (px, py, pc)))
    return cps


def _all_copies(s_refs, out_refs, sems, first):
    x, y, c, peers = _peers()
    me = 4 * x + 2 * y + c
    return [_remote(s, o.at[me], sems, first + 7 * t + m - 1, peer)
            for t, (s, o) in enumerate(zip(s_refs, out_refs)) for m, peer in peers]


EXCHANGES = {"gather": (_gather_copies, lambda a: (4,) + a.shape),
             "reduce": (_reduce_copies, lambda a: (8,) + a.shape[1:]),
             "all": (_all_copies, lambda a: (8,) + a.shape)}


def _carry_plan(carries):
    inputs, shapes, spans = [], [], []
    for kind, arrs in carries:
        for a in arrs:
            inputs.append(a)
            shapes.append(jax.ShapeDtypeStruct(EXCHANGES[kind][1](a), a.dtype))
        spans.append((kind, len(arrs)))

    def build(in_refs, out_refs, sems):
        cps, pos = [], 0
        for kind, cnt in spans:
            cps += EXCHANGES[kind][0](in_refs[pos:pos + cnt], out_refs[pos:pos + cnt], sems, 7 * pos)
            pos += cnt
        return cps

    return inputs, shapes, 7 * len(inputs), build


def exchange(carries, name):
    inputs, shapes, n_sems, build = _carry_plan(carries)
    n = len(inputs)

    def body(*refs):
        cps = build(refs[:n], refs[n:2 * n], refs[2 * n:])
        for cp in cps:
            cp.start()
        for cp in cps:
            cp.wait()

    return pl.pallas_call(
        body, name=name, in_specs=[ANY] * n, out_specs=[ANY] * n, out_shape=shapes,
        scratch_shapes=[pltpu.SemaphoreType.DMA((n_sems,)), pltpu.SemaphoreType.DMA((n_sems,))],
    )(*inputs)


def _adamw_math(w, g, m, v):
    m = ADAM_B1 * m + (1.0 - ADAM_B1) * g
    v = ADAM_B2 * v + (1.0 - ADAM_B2) * (g * g)
    m_hat = m / (1.0 - ADAM_B1 ** ADAM_STEP)
    v_hat = v / (1.0 - ADAM_B2 ** ADAM_STEP)
    delta = -ADAM_LR * (m_hat / (jnp.sqrt(v_hat) + ADAM_EPS) + ADAM_WD * w)
    return delta, m, v


def adamw_layer(l, w, m, v, parts, own, device, chip, bufs, name):
    _, r, cdim = w.shape
    tr = _rows(r, PACK_ROW_TILE)

    def body(dev_ref, chip_ref, w_ref, m_ref, v_ref, p_ref, o_ref, *rest):
        g_ref, d_ref, nm_ref, nv_ref = rest[-4:]
        g = None
        for dev in range(8):
            part = jnp.where(dev_ref[0] == dev, o_ref[0], p_ref[dev]).astype(F32)
            g = part if g is None else g + part
        d, nm, nv = _adamw_math(w_ref[0], g, m_ref[0], v_ref[0])
        g_ref[0] = g
        d_ref[0] = d
        nm_ref[0] = nm
        nv_ref[0] = nv

    lay = pl.BlockSpec((1, tr, cdim), lambda i, dev_ref, chip_ref: (l, i, 0))
    in_specs = [lay, lay, lay, pl.BlockSpec((8, tr, cdim), lambda i, dev_ref, chip_ref: (0, i, 0)),
                pl.BlockSpec((1, tr, cdim), lambda i, dev_ref, chip_ref: (chip_ref[0], i, 0))]
    args = [device, chip, w, m, v, parts, own]
    aliases = {}
    if bufs is not None:
        in_specs += [ANY] * 4
        aliases = {len(args) + k: k for k in range(4)}
        args += list(bufs)
    return pl.pallas_call(
        body, name=name,
        grid_spec=pltpu.PrefetchScalarGridSpec(num_scalar_prefetch=2, grid=(r // tr,), in_specs=in_specs,
                                               out_specs=[lay] * 4),
        out_shape=[jax.ShapeDtypeStruct(w.shape, F32)] * 4,
        input_output_aliases=aliases,
        compiler_params=_params(("parallel",)),
    )(*args)


def sum_and_adamw(parts, own, me, w, m, v):
    _, r, cdim = parts.shape
    tr = _rows(r, PACK_ROW_TILE)

    def body(me_ref, p_ref, o_ref, w_ref, m_ref, v_ref, g_ref, d_ref, nm_ref, nv_ref):
        part = lambda k: jnp.where(me_ref[0] == k, o_ref[...], p_ref[k]).astype(F32)
        g = part(0)
        for k in range(1, 8):
            g = g + part(k)
        d, nm, nv = _adamw_math(w_ref[...], g, m_ref[...], v_ref[...])
        g_ref[...] = g
        d_ref[...] = d
        nm_ref[...] = nm
        nv_ref[...] = nv

    spec = pl.BlockSpec((tr, cdim), lambda i, me_ref: (i, 0))
    return pl.pallas_call(
        body, name="sum_and_adamw",
        grid_spec=pltpu.PrefetchScalarGridSpec(
            num_scalar_prefetch=1, grid=(r // tr,),
            in_specs=[pl.BlockSpec((8, tr, cdim), lambda i, me_ref: (0, i, 0)), spec, spec, spec, spec],
            out_specs=[spec] * 4),
        out_shape=[jax.ShapeDtypeStruct((r, cdim), F32)] * 4,
        compiler_params=_params(("parallel",)),
    )(me, parts, own, w, m, v)


SHARDED = ("w_in", "w_glu", "w_branch_a", "w_branch_b", "w_out", "w_mlp_up", "w_mlp_down")
SHARD_AXIS = {"w_in": 2, "w_glu": 1, "w_branch_a": 2, "w_branch_b": 2, "w_out": 1, "w_mlp_up": 2, "w_mlp_down": 1}
SMALL = ("norm_mix", "b_forget", "ssm_lambda_re", "ssm_lambda_im", "ssm_log_dt", "ssm_b_re", "ssm_b_im",
         "ssm_c_re", "ssm_c_im", "ssm_d", "b_glu", "norm_mlp", "norm_final")
SMALL_WIDE = ("ssm_b_re", "ssm_b_im", "ssm_c_re", "ssm_c_im")


def pack_flat(arrs):
    flat = jnp.concatenate([a.reshape(-1).astype(F32) for a in arrs])
    unit = PACK_COLS * PACK_ROW_ALIGN
    rows = (flat.shape[0] + unit - 1) // unit * PACK_ROW_ALIGN
    return jnp.pad(flat, (0, rows * PACK_COLS - flat.shape[0])).reshape(rows, PACK_COLS)


def unpack_flat(packed, shapes):
    flat = packed.reshape(-1)
    out, off = [], 0
    for s in shapes:
        n = math.prod(s)
        out.append(flat[off:off + n].reshape(tuple(s)))
        off += n
    return out


def _discretise(lam_re, lam_im, log_dt, b_re, b_im):
    dt = jnp.exp(log_dt)[:, None]
    mag = jnp.exp(lam_re * dt)
    ar = mag * jnp.cos(lam_im * dt)
    ai = mag * jnp.sin(lam_im * dt)
    den = lam_re * lam_re + lam_im * lam_im
    cr = ((ar - 1.0) * lam_re + ai * lam_im) / den
    ci = (ai * lam_re - (ar - 1.0) * lam_im) / den
    bbr = cr[:, :, None] * b_re - ci[:, :, None] * b_im
    bbi = cr[:, :, None] * b_im + ci[:, :, None] * b_re
    return ar, ai, bbr, bbi


def _ssm_inputs(ar, ai, bbr, bbi, c_re, c_im):
    pr, pi = [ar], [ai]
    for _ in range(SUBLANES - 1):
        pr, pi = pr + [pr[-1] * ar - pi[-1] * ai], pi + [pr[-1] * ai + pi[-1] * ar]
    powers = jnp.stack([jnp.stack(pr), jnp.stack(pi)], axis=1)
    powers = powers.reshape(SUBLANES, 2, SSM_CHUNKS, CHUNK_STATES).transpose(0, 2, 1, 3).reshape(SUBLANES, STATE_LANES)
    conj = powers * jnp.tile(jnp.repeat(jnp.asarray([1.0, -1.0], F32), CHUNK_STATES), SSM_CHUNKS)
    idx = jnp.arange(SUBLANES)[:, None]
    tabs = jnp.concatenate(
        [jnp.where(idx >= s, powers[s - 1][None, :], 0.0) for s in SCAN_STEPS] + [powers]
        + [jnp.where(idx < SUBLANES - s, conj[s - 1][None, :], 0.0) for s in SCAN_STEPS] + [conj[::-1]], axis=0)
    eye = jnp.eye(CHUNK_GROUPS, dtype=F32)
    wb = jnp.stack([bbr, bbi]).reshape(2, SSM_CHUNKS, CHUNK_GROUPS, SSM_STATE, SSM_GROUP_CH).transpose(1, 2, 4, 0, 3)
    wb4 = (wb[:, :, :, :, None, :] * eye[None, :, None, None, :, None]).reshape(SSM_CHUNKS, LANES, CHUNK_LANES)
    wc = jnp.stack([c_re, -c_im]).reshape(2, SSM_CHUNKS, CHUNK_GROUPS, SSM_GROUP_CH, SSM_STATE).transpose(1, 0, 4, 2, 3)
    wc4 = (wc[:, :, None, :, :, :] * eye[None, None, :, None, :, None]).reshape(SSM_CHUNKS, CHUNK_LANES, LANES)
    return tabs, wb4.astype(BF16), wc4.astype(BF16)


def _ssm_param_grads(ga8, gwb, gwc):
    eye = jnp.eye(CHUNK_GROUPS, dtype=F32)
    ga = jnp.sum(ga8, axis=0).reshape(SSM_CHUNKS, 2, CHUNK_STATES)
    gar = ga[:, 0].reshape(SSM_GROUPS, SSM_STATE)
    gai = ga[:, 1].reshape(SSM_GROUPS, SSM_STATE)

    def from_wb(g):
        t = g.reshape(SSM_CHUNKS, CHUNK_GROUPS, SSM_GROUP_CH, CHUNK_GROUPS, SSM_STATE)
        return jnp.einsum("jgchp,gh->jgpc", t, eye).reshape(SSM_GROUPS, SSM_STATE, SSM_GROUP_CH)

    def from_wc(g):
        t = g.reshape(SSM_CHUNKS, CHUNK_GROUPS, SSM_STATE, CHUNK_GROUPS, SSM_GROUP_CH)
        return jnp.einsum("jhpgc,gh->jgcp", t, eye).reshape(SSM_GROUPS, SSM_GROUP_CH, SSM_STATE)

    return (gar, gai, from_wb(gwb[:, :, :CHUNK_STATES]), from_wb(gwb[:, :, CHUNK_STATES:]),
            from_wc(gwc[:, :CHUNK_STATES]), -from_wc(gwc[:, CHUNK_STATES:]))


def kernel(x, norm_mix, w_in, b_forget, ssm_lambda_re, ssm_lambda_im, ssm_log_dt, ssm_b_re, ssm_b_im, ssm_c_re, ssm_c_im, ssm_d, w_glu, b_glu, w_branch_a, w_branch_b, w_out, norm_mlp, w_mlp_up, w_mlp_down, norm_final, loss_target, m_norm_mix, m_w_in, m_b_forget, m_ssm_lambda_re, m_ssm_lambda_im, m_ssm_log_dt, m_ssm_b_re, m_ssm_b_im, m_ssm_c_re, m_ssm_c_im, m_ssm_d, m_w_glu, m_b_glu, m_w_branch_a, m_w_branch_b, m_w_out, m_norm_mlp, m_w_mlp_up, m_w_mlp_down, m_norm_final, v_norm_mix, v_w_in, v_b_forget, v_ssm_lambda_re, v_ssm_lambda_im, v_ssm_log_dt, v_ssm_b_re, v_ssm_b_im, v_ssm_c_re, v_ssm_c_im, v_ssm_d, v_w_glu, v_b_glu, v_w_branch_a, v_w_branch_b, v_w_out, v_norm_mlp, v_w_mlp_up, v_w_mlp_down, v_norm_final):
    args = dict(locals())
    bsz, seq, d = x.shape
    nl = norm_mix.shape[0]
    tokens = bsz * seq
    aw, sw = ATTN_WIDTH, SSM_WIDTH
    chip = (2 * lax.axis_index("x") + lax.axis_index("y")).astype(jnp.int32)
    chip_id = chip.reshape(1)
    device_id = (2 * chip + lax.axis_index("c").astype(jnp.int32)).reshape(1)

    own = {n: args[n].astype(BF16) for n in SHARDED}
    late = [n for n in SHARDED if n != "w_in"]
    o_f, o_u, o_ga, o_gb = 3 * aw, 3 * aw + ATTN_HEADS, 3 * aw + ATTN_HEADS + sw, 3 * aw + ATTN_HEADS + sw + d
    u_blk = 2 * d // sw
    f_blk = (2 * d + sw) // F_PAD
    bf_pad = jnp.pad(b_forget, ((0, 0), (0, F_PAD - ATTN_HEADS)))

    def assemble(l, names, gathered):
        return {n: jnp.concatenate([jnp.where(chip == k, own[n][l], g[k]) for k in range(4)],
                                   axis=SHARD_AXIS[n] - 1) for n, g in zip(names, gathered)}

    def split_w_in(win):
        w = {"w_qkv": win[:, :o_f],
             "w_rest": jnp.concatenate([win[:, o_ga:o_gb], win[:, o_gb:], win[:, o_u:o_ga],
                                        jnp.pad(win[:, o_f:o_u], ((0, 0), (0, F_PAD - ATTN_HEADS)))], axis=1)}
        w["w_pad"] = jnp.concatenate([w["w_qkv"], w["w_rest"]], axis=1)
        return w

    disc = [jax.vjp(_discretise, ssm_lambda_re[l], ssm_lambda_im[l], ssm_log_dt[l], ssm_b_re[l], ssm_b_im[l])
            for l in range(nl)]

    g_in = exchange([("gather", [own["w_in"][0]])], "gather_first")
    xs = x.reshape(tokens, d)
    saved, weights = [], []
    for l in range(nl):
        w = split_w_in(assemble(l, ["w_in"], g_in)["w_in"])
        g1 = norm_mix[l].reshape(1, d)
        qkv = norm_matmul(xs, g1, w["w_qkv"], BF16, "proj_qkv")
        rest = norm_matmul(xs, g1, w["w_rest"], F32, "proj_rest")
        cumcol, cumrow = forget_cumsum(rest, bf_pad[l:l + 1], bsz, seq, f_blk)
        carries = [("gather", [own[n][l] for n in late] + ([own["w_in"][l + 1]] if l + 1 < nl else []))]
        ya, lser, lands = fox_fwd(qkv, cumrow, bsz, seq, carries)
        w.update(assemble(l, late, lands))
        g_in = lands[len(late):]
        tabs, wb4, wc4 = _ssm_inputs(*disc[l][0], ssm_c_re[l], ssm_c_im[l])
        dskip = ssm_d[l].reshape(1, sw)
        ys, hs = ssm_fwd(rest, wb4, wc4, tabs, dskip, bsz, seq, u_blk)
        x1, z, pa, pb, yb, yb2, mixed = mix_fwd(ya, ys, rest, xs, w["w_glu"], b_glu[l].reshape(1, sw),
                                                 w["w_branch_a"], w["w_branch_b"], w["w_out"])
        x2, up = mlp_fwd(x1, norm_mlp[l].reshape(1, d), w["w_mlp_up"], w["w_mlp_down"])
        saved.append(dict(x0=xs, qkv=qkv, rest=rest, cumcol=cumcol, ya=ya, lser=lser,
                          tabs=tabs, wb4=wb4, wc4=wc4, dskip=dskip, ys=ys, hs=hs, x1=x1, z=z, pa=pa, pb=pb,
                          yb=yb, yb2=yb2, mixed=mixed, up=up))
        weights.append(w)
        xs = x2
    dx, loss_rows, dgf_rows = final_loss(xs, norm_final.reshape(1, d), loss_target.reshape(tokens, d))
    loss = lax.psum(jnp.sum(loss_rows), ("x", "y", "c"))

    early = [n for n in SHARDED if n != "w_in"]
    big = {n: [None] * nl for n in SHARDED}
    parts = {n: [None] * nl for n in SHARDED}
    small = {n: [None] * nl for n in SMALL if n != "norm_final"}
    for l in reversed(range(nl)):
        s, w = saved[l], weights[l]
        g2 = norm_mlp[l].reshape(1, d)
        dup, dx1, dg2, lands = mlp_bwd(dx, s["up"], s["x1"], g2, w["w_mlp_up"], w["w_mlp_down"],
                                       [("reduce", [big["w_in"][l + 1]])] if l + 1 < nl else [])
        if l + 1 < nl:
            parts["w_in"][l + 1] = lands[0]
        big["w_mlp_down"][l] = matmul_tn(s["up"], dx, "grad_w_mlp_down", a_kind="relu2", shard_axis=0, out_dtype=BF16)
        big["w_mlp_up"][l] = matmul_tn(s["x1"], dup, "grad_w_mlp_up", a_kind="norm", gain=g2, shard_axis=1,
                                       out_dtype=BF16)
        small["norm_mlp"][l] = jnp.sum(dg2, axis=0)
        dya, dys, dgab, dpa, dpb, dz, dbg = mix_bwd(dx1, s["rest"], s["pa"], s["pb"], s["z"], s["ys"],
                                                    w["w_glu"], w["w_branch_a"], w["w_branch_b"], w["w_out"])
        big["w_out"][l] = matmul_tn(s["mixed"], dx1, "grad_w_out", shard_axis=0, out_dtype=BF16)
        big["w_branch_a"][l] = matmul_tn(s["ya"], dpa, "grad_w_branch_a", shard_axis=1, out_dtype=BF16)
        big["w_branch_b"][l] = matmul_tn(s["yb2"], dpb, "grad_w_branch_b", shard_axis=1, out_dtype=BF16)
        big["w_glu"][l] = matmul_tn(s["yb"], dz, "grad_w_glu", shard_axis=0, out_dtype=BF16)
        small["b_glu"][l] = jnp.sum(dbg, axis=0)
        du, ga8, gwb, gwc, gd8 = ssm_bwd(dys, s["rest"], s["hs"], s["wb4"], s["wc4"], s["tabs"], s["dskip"],
                                         bsz, seq, u_blk)
        gar, gai, gbbr, gbbi, gcr, gci = _ssm_param_grads(ga8, gwb, gwc)
        glr, gli, gdt, gbr, gbi = disc[l][1]((gar, gai, gbbr, gbbi))
        small["ssm_lambda_re"][l], small["ssm_lambda_im"][l], small["ssm_log_dt"][l] = glr, gli, gdt
        small["ssm_b_re"][l], small["ssm_b_im"][l] = gbr, gbi
        small["ssm_c_re"][l], small["ssm_c_im"][l] = gcr, gci
        small["ssm_d"][l] = jnp.sum(gd8, axis=0)
        dq, dk, dv, df, dbf, lands = fox_bwd(s["qkv"], dya, s["ya"], s["lser"], s["cumcol"], s["rest"],
                                             bf_pad[l:l + 1], bsz, seq, f_blk, [("reduce", [big[n][l] for n in early])])
        for n, p in zip(early, lands):
            parts[n][l] = p
        small["b_forget"][l] = jnp.sum(dbf, axis=0)[:ATTN_HEADS]
        dproj = jnp.concatenate([dq, dk, dv, dgab, du, df], axis=1)
        g1 = norm_mix[l].reshape(1, d)
        dwp = matmul_tn(s["x0"], dproj, "grad_w_in", a_kind="norm", gain=g1)
        big["w_in"][l] = jnp.stack(jnp.split(jnp.concatenate(
            [dwp[:, :o_f], dwp[:, o_f + 2 * d + sw:o_f + 2 * d + sw + ATTN_HEADS],
             dwp[:, o_f + 2 * d:o_f + 2 * d + sw], dwp[:, o_f:o_f + 2 * d]], axis=1), 4, axis=1)).astype(BF16)
        dx, dg1, lands = proj_bwd(dproj, w["w_pad"], s["x0"], dx1, g1,
                                  [("reduce", [big["w_in"][0]])] if l == 0 else [])
        if l == 0:
            parts["w_in"][0] = lands[0]
        small["norm_mix"][l] = jnp.sum(dg1, axis=0)
    grad_x = dx.reshape(bsz, seq, d)

    small_g = {n: jnp.stack(small[n]) if n != "norm_final" else jnp.sum(dgf_rows, axis=0) for n in SMALL}
    groups = {BF16: SMALL_WIDE, F32: [n for n in SMALL if n not in SMALL_WIDE]}
    packed = {dt: pack_flat([small_g[n] for n in names]).astype(dt) for dt, names in groups.items()}
    small_parts = exchange([("all", [packed[dt] for dt in groups])], "exchange_last")

    out_g, out_d, out_m, out_v = {}, {}, {}, {}
    for n in SHARDED:
        bufs = None
        for l in range(nl):
            bufs = adamw_layer(l, args[n], args["m_" + n], args["v_" + n], parts[n][l], big[n][l], device_id, chip_id,
                               bufs, "adamw_" + n)
        out_g[n], out_d[n], out_m[n], out_v[n] = bufs

    for (dt, names), landed in zip(groups.items(), small_parts):
        results = sum_and_adamw(landed, packed[dt], device_id, pack_flat([args[n] for n in names]),
                                pack_flat([args["m_" + n] for n in names]), pack_flat([args["v_" + n] for n in names]))
        for res, flat in zip((out_g, out_d, out_m, out_v), results):
            res.update(zip(names, unpack_flat(flat, [args[n].shape for n in names])))

    order = ("norm_mix", "w_in", "b_forget", "ssm_lambda_re", "ssm_lambda_im", "ssm_log_dt", "ssm_b_re",
             "ssm_b_im", "ssm_c_re", "ssm_c_im", "ssm_d", "w_glu", "b_glu", "w_branch_a", "w_branch_b", "w_out",
             "norm_mlp", "w_mlp_up", "w_mlp_down", "norm_final")
    return (loss, grad_x, *[out_g[n] for n in order], *[out_d[n] for n in order],
            *[out_m[n] for n in order], *[out_v[n] for n in order])
```
